```python
import math
import jax, jax.numpy as jnp
from jax import lax
import numpy as np

D_MODEL = 1024
BATCH = 8
SEQ = 8192
DEPTH = 4

CHUNK = 64
N_META = 16
N_A = DEPTH // 2
N_B = DEPTH - N_A
D_FF = 2816
POOL_WINDOWS = (2, 4, 8, 16)
N_POOL_GROUPS = len(POOL_WINDOWS)
POOL_GROUP = D_MODEL // N_POOL_GROUPS
N_HEADS = 8
QK_NOPE = 64
QK_ROPE = 32
V_HEAD = 64
KV_RANK = 256
Q_RANK = 384
ROPE_THETA = 10000.0
Q_BLOCK = 128
EPS = 1e-6

kernel_name = "yoco_pool_mla_macaron_trunk"


def rmsnorm(x, g):
    xf = x.astype(jnp.float32)
    xf = xf * lax.rsqrt(jnp.mean(xf * xf, axis=-1, keepdims=True) + EPS)
    return xf.astype(x.dtype) * g


def swiglu(h, w_gate, w_up, w_down):
    return (jax.nn.silu(h @ w_gate) * (h @ w_up)) @ w_down


def chunk_ids(n):
    pos = jnp.arange(n)
    return jnp.where(pos < N_META, 0, (pos - N_META) // CHUNK + 1)


def rope_tables(n):
    inv = 1.0 / (ROPE_THETA ** (jnp.arange(0, QK_ROPE, 2, dtype=jnp.float32) / QK_ROPE))
    ang = jnp.arange(n, dtype=jnp.float32)[:, None] * inv[None, :]
    return jnp.cos(ang), jnp.sin(ang)


def apply_rope(x, cos, sin):
    xf = x.astype(jnp.float32)
    x1, x2 = xf[..., : QK_ROPE // 2], xf[..., QK_ROPE // 2:]
    out = jnp.concatenate([x1 * cos - x2 * sin, x2 * cos + x1 * sin], axis=-1)
    return out.astype(x.dtype)


def pool_mixer(h, w_group, scale):
    L = h.shape[1]
    hf = h.astype(jnp.float32)
    cs = jnp.concatenate([jnp.zeros_like(hf[:, :1]), jnp.cumsum(hf, axis=1)], axis=1)
    hi = jnp.arange(1, L + 1)
    outs = []
    for g, w in enumerate(POOL_WINDOWS):
        sl = slice(g * POOL_GROUP, (g + 1) * POOL_GROUP)
        lo = jnp.maximum(hi - w, 0)
        c = cs[..., sl]
        count = (hi - lo).astype(jnp.float32)[None, :, None]
        mean = (jnp.take(c, hi, axis=1) - jnp.take(c, lo, axis=1)) / count
        outs.append(mean - hf[..., sl])
    pooled = jnp.stack(outs, axis=2).astype(h.dtype)
    y = jnp.einsum('blgc,gcd->blgd', pooled, w_group)
    return y.reshape(h.shape) * scale


def mla_shared_kv(h, w_dkv, kv_latent_norm, w_uk, w_uv, cos, sin):
    B, L, _ = h.shape
    ckr = h @ w_dkv
    c_kv = rmsnorm(ckr[..., :KV_RANK], kv_latent_norm)
    k_rope = apply_rope(ckr[..., KV_RANK:], cos, sin)
    k_nope = (c_kv @ w_uk).reshape(B, L, N_HEADS, QK_NOPE)
    v = (c_kv @ w_uv).reshape(B, L, N_HEADS, V_HEAD)
    return k_nope, k_rope, v


def mla_attention(h, w_dq, q_latent_norm, w_uq, w_o, k_nope, k_rope, v, cos, sin):
    B, L, _ = h.shape
    cq = rmsnorm(h @ w_dq, q_latent_norm)
    q = (cq @ w_uq).reshape(B, L, N_HEADS, QK_NOPE + QK_ROPE)
    q_nope = q[..., :QK_NOPE]
    q_rope = apply_rope(q[..., QK_NOPE:], cos[:, None, :], sin[:, None, :])
    n_blk = -(-L // Q_BLOCK)
    Lp = n_blk * Q_BLOCK
    pad = ((0, 0), (0, Lp - L), (0, 0), (0, 0))
    qn_b = jnp.pad(q_nope, pad).reshape(B, n_blk, Q_BLOCK, N_HEADS, QK_NOPE).transpose(1, 0, 2, 3, 4)
    qr_b = jnp.pad(q_rope, pad).reshape(B, n_blk, Q_BLOCK, N_HEADS, QK_ROPE).transpose(1, 0, 2, 3, 4)
    qid_b = chunk_ids(Lp).reshape(n_blk, Q_BLOCK)
    kid = chunk_ids(L)
    sm_scale = 1.0 / math.sqrt(QK_NOPE + QK_ROPE)

    def block(args):
        qn, qr, qid = args
        s = (jnp.einsum('bqhd,bkhd->bhqk', qn, k_nope)
             + jnp.einsum('bqhr,bkr->bhqk', qr, k_rope)).astype(jnp.float32) * sm_scale
        mask = kid[None, :] <= qid[:, None]
        s = jnp.where(mask[None, None], s, jnp.finfo(jnp.float32).min)
        p = jax.nn.softmax(s, axis=-1).astype(v.dtype)
        return jnp.einsum('bhqk,bkhd->bqhd', p, v)

    o = lax.map(block, (qn_b, qr_b, qid_b))
    o = o.transpose(1, 0, 2, 3, 4).reshape(B, Lp, N_HEADS * V_HEAD)[:, :L]
    return o @ w_o


def _fwd_setup_inputs(seed: int = 0) -> dict:
    key = jax.random.key(seed)
    ks = iter(jax.random.split(key, 40))
    f32 = jnp.float32

    def nrm(shape, fan_in):
        return jax.random.normal(next(ks), shape, f32) * (fan_in ** -0.5)

    def gain(shape):
        return 1.0 + 0.1 * jax.random.normal(next(ks), shape, f32)

    return {
        "x": jax.random.normal(next(ks), (BATCH, SEQ, D_MODEL), f32),
        "meta_tokens": jax.random.normal(next(ks), (N_META, D_MODEL), f32),
        "ffn1_norm": gain((DEPTH, D_MODEL)),
        "ffn1_w_gate": nrm((DEPTH, D_MODEL, D_FF), D_MODEL),
        "ffn1_w_up": nrm((DEPTH, D_MODEL, D_FF), D_MODEL),
        "ffn1_w_down": nrm((DEPTH, D_FF, D_MODEL), D_FF),
        "mix_norm": gain((DEPTH, D_MODEL)),
        "ffn2_norm": gain((DEPTH, D_MODEL)),
        "ffn2_w_gate": nrm((DEPTH, D_MODEL, D_FF), D_MODEL),
        "ffn2_w_up": nrm((DEPTH, D_MODEL, D_FF), D_MODEL),
        "ffn2_w_down": nrm((DEPTH, D_FF, D_MODEL), D_FF),
        "pool_w": nrm((N_A, N_POOL_GROUPS, POOL_GROUP, POOL_GROUP), POOL_GROUP),
        "pool_scale": gain((N_A, D_MODEL)),
        "kv_in_norm": gain((D_MODEL,)),
        "w_dkv": nrm((D_MODEL, KV_RANK + QK_ROPE), D_MODEL),
        "kv_latent_norm": gain((KV_RANK,)),
        "w_uk": nrm((KV_RANK, N_HEADS * QK_NOPE), KV_RANK),
        "w_uv": nrm((KV_RANK, N_HEADS * V_HEAD), KV_RANK),
        "w_dq": nrm((N_B, D_MODEL, Q_RANK), D_MODEL),
        "q_latent_norm": gain((N_B, Q_RANK)),
        "w_uq": nrm((N_B, Q_RANK, N_HEADS * (QK_NOPE + QK_ROPE)), Q_RANK),
        "w_o": nrm((N_B, N_HEADS * V_HEAD, D_MODEL), N_HEADS * V_HEAD),
        "final_norm": gain((D_MODEL,)),
    }


def _fwd_reference(x, meta_tokens, ffn1_norm, ffn1_w_gate, ffn1_w_up, ffn1_w_down, mix_norm,
              ffn2_norm, ffn2_w_gate, ffn2_w_up, ffn2_w_down, pool_w, pool_scale,
              kv_in_norm, w_dkv, kv_latent_norm, w_uk, w_uv, w_dq, q_latent_norm, w_uq, w_o,
              final_norm):
    B = x.shape[0]
    meta = jnp.broadcast_to(meta_tokens[None].astype(x.dtype), (B, N_META, D_MODEL))
    h = jnp.concatenate([meta, x], axis=1)
    L = h.shape[1]
    cos, sin = rope_tables(L)
    shared = None
    for l in range(DEPTH):
        h = h + 0.5 * swiglu(rmsnorm(h, ffn1_norm[l]), ffn1_w_gate[l], ffn1_w_up[l], ffn1_w_down[l])
        u = rmsnorm(h, mix_norm[l])
        if l < N_A:
            h = h + pool_mixer(u, pool_w[l], pool_scale[l])
        else:
            j = l - N_A
            k_nope, k_rope, v = shared
            h = h + mla_attention(u, w_dq[j], q_latent_norm[j], w_uq[j], w_o[j],
                                  k_nope, k_rope, v, cos, sin)
        h = h + 0.5 * swiglu(rmsnorm(h, ffn2_norm[l]), ffn2_w_gate[l], ffn2_w_up[l], ffn2_w_down[l])
        if l == N_A - 1:
            shared = mla_shared_kv(rmsnorm(h, kv_in_norm), w_dkv, kv_latent_norm, w_uk, w_uv, cos, sin)
    out = rmsnorm(h, final_norm)
    return out[:, N_META:]


import jax as _jax
import jax.numpy as _jnp

TWIN_FORMAT = 'train_step'
FWD_PARAMS = ['x', 'meta_tokens', 'ffn1_norm', 'ffn1_w_gate', 'ffn1_w_up', 'ffn1_w_down', 'mix_norm', 'ffn2_norm', 'ffn2_w_gate', 'ffn2_w_up', 'ffn2_w_down', 'pool_w', 'pool_scale', 'kv_in_norm', 'w_dkv', 'kv_latent_norm', 'w_uk', 'w_uv', 'w_dq', 'q_latent_norm', 'w_uq', 'w_o', 'final_norm']
TWIN_WEIGHTS = ['meta_tokens', 'ffn1_norm', 'ffn1_w_gate', 'ffn1_w_up', 'ffn1_w_down', 'mix_norm', 'ffn2_norm', 'ffn2_w_gate', 'ffn2_w_up', 'ffn2_w_down', 'pool_w', 'pool_scale', 'kv_in_norm', 'w_dkv', 'kv_latent_norm', 'w_uk', 'w_uv', 'w_dq', 'q_latent_norm', 'w_uq', 'w_o', 'final_norm']
TWIN_DIFF_INPUT = 'x'
TWIN_INPUTS = ['x', 'meta_tokens', 'ffn1_norm', 'ffn1_w_gate', 'ffn1_w_up', 'ffn1_w_down', 'mix_norm', 'ffn2_norm', 'ffn2_w_gate', 'ffn2_w_up', 'ffn2_w_down', 'pool_w', 'pool_scale', 'kv_in_norm', 'w_dkv', 'kv_latent_norm', 'w_uk', 'w_uv', 'w_dq', 'q_latent_norm', 'w_uq', 'w_o', 'final_norm', 'loss_target', 'm_meta_tokens', 'm_ffn1_norm', 'm_ffn1_w_gate', 'm_ffn1_w_up', 'm_ffn1_w_down', 'm_mix_norm', 'm_ffn2_norm', 'm_ffn2_w_gate', 'm_ffn2_w_up', 'm_ffn2_w_down', 'm_pool_w', 'm_pool_scale', 'm_kv_in_norm', 'm_w_dkv', 'm_kv_latent_norm', 'm_w_uk', 'm_w_uv', 'm_w_dq', 'm_q_latent_norm', 'm_w_uq', 'm_w_o', 'm_final_norm', 'v_meta_tokens', 'v_ffn1_norm', 'v_ffn1_w_gate', 'v_ffn1_w_up', 'v_ffn1_w_down', 'v_mix_norm', 'v_ffn2_norm', 'v_ffn2_w_gate', 'v_ffn2_w_up', 'v_ffn2_w_down', 'v_pool_w', 'v_pool_scale', 'v_kv_in_norm', 'v_w_dkv', 'v_kv_latent_norm', 'v_w_uk', 'v_w_uv', 'v_w_dq', 'v_q_latent_norm', 'v_w_uq', 'v_w_o', 'v_final_norm']
TWIN_OUTPUTS = ['loss', 'grad_x', 'grad_meta_tokens', 'grad_ffn1_norm', 'grad_ffn1_w_gate', 'grad_ffn1_w_up', 'grad_ffn1_w_down', 'grad_mix_norm', 'grad_ffn2_norm', 'grad_ffn2_w_gate', 'grad_ffn2_w_up', 'grad_ffn2_w_down', 'grad_pool_w', 'grad_pool_scale', 'grad_kv_in_norm', 'grad_w_dkv', 'grad_kv_latent_norm', 'grad_w_uk', 'grad_w_uv', 'grad_w_dq', 'grad_q_latent_norm', 'grad_w_uq', 'grad_w_o', 'grad_final_norm', 'delta_meta_tokens', 'delta_ffn1_norm', 'delta_ffn1_w_gate', 'delta_ffn1_w_up', 'delta_ffn1_w_down', 'delta_mix_norm', 'delta_ffn2_norm', 'delta_ffn2_w_gate', 'delta_ffn2_w_up', 'delta_ffn2_w_down', 'delta_pool_w', 'delta_pool_scale', 'delta_kv_in_norm', 'delta_w_dkv', 'delta_kv_latent_norm', 'delta_w_uk', 'delta_w_uv', 'delta_w_dq', 'delta_q_latent_norm', 'delta_w_uq', 'delta_w_o', 'delta_final_norm', 'new_m_meta_tokens', 'new_m_ffn1_norm', 'new_m_ffn1_w_gate', 'new_m_ffn1_w_up', 'new_m_ffn1_w_down', 'new_m_mix_norm', 'new_m_ffn2_norm', 'new_m_ffn2_w_gate', 'new_m_ffn2_w_up', 'new_m_ffn2_w_down', 'new_m_pool_w', 'new_m_pool_scale', 'new_m_kv_in_norm', 'new_m_w_dkv', 'new_m_kv_latent_norm', 'new_m_w_uk', 'new_m_w_uv', 'new_m_w_dq', 'new_m_q_latent_norm', 'new_m_w_uq', 'new_m_w_o', 'new_m_final_norm', 'new_v_meta_tokens', 'new_v_ffn1_norm', 'new_v_ffn1_w_gate', 'new_v_ffn1_w_up', 'new_v_ffn1_w_down', 'new_v_mix_norm', 'new_v_ffn2_norm', 'new_v_ffn2_w_gate', 'new_v_ffn2_w_up', 'new_v_ffn2_w_down', 'new_v_pool_w', 'new_v_pool_scale', 'new_v_kv_in_norm', 'new_v_w_dkv', 'new_v_kv_latent_norm', 'new_v_w_uk', 'new_v_w_uv', 'new_v_w_dq', 'new_v_q_latent_norm', 'new_v_w_uq', 'new_v_w_o', 'new_v_final_norm']
TWIN_LEAF_KINDS = {'loss': 'loss', 'grad_x': 'grad_x', 'grad_meta_tokens': 'grad_w', 'grad_ffn1_norm': 'grad_w', 'grad_ffn1_w_gate': 'grad_w', 'grad_ffn1_w_up': 'grad_w', 'grad_ffn1_w_down': 'grad_w', 'grad_mix_norm': 'grad_w', 'grad_ffn2_norm': 'grad_w', 'grad_ffn2_w_gate': 'grad_w', 'grad_ffn2_w_up': 'grad_w', 'grad_ffn2_w_down': 'grad_w', 'grad_pool_w': 'grad_w', 'grad_pool_scale': 'grad_w', 'grad_kv_in_norm': 'grad_w', 'grad_w_dkv': 'grad_w', 'grad_kv_latent_norm': 'grad_w', 'grad_w_uk': 'grad_w', 'grad_w_uv': 'grad_w', 'grad_w_dq': 'grad_w', 'grad_q_latent_norm': 'grad_w', 'grad_w_uq': 'grad_w', 'grad_w_o': 'grad_w', 'grad_final_norm': 'grad_w', 'delta_meta_tokens': 'delta_w', 'delta_ffn1_norm': 'delta_w', 'delta_ffn1_w_gate': 'delta_w', 'delta_ffn1_w_up': 'delta_w', 'delta_ffn1_w_down': 'delta_w', 'delta_mix_norm': 'delta_w', 'delta_ffn2_norm': 'delta_w', 'delta_ffn2_w_gate': 'delta_w', 'delta_ffn2_w_up': 'delta_w', 'delta_ffn2_w_down': 'delta_w', 'delta_pool_w': 'delta_w', 'delta_pool_scale': 'delta_w', 'delta_kv_in_norm': 'delta_w', 'delta_w_dkv': 'delta_w', 'delta_kv_latent_norm': 'delta_w', 'delta_w_uk': 'delta_w', 'delta_w_uv': 'delta_w', 'delta_w_dq': 'delta_w', 'delta_q_latent_norm': 'delta_w', 'delta_w_uq': 'delta_w', 'delta_w_o': 'delta_w', 'delta_final_norm': 'delta_w', 'new_m_meta_tokens': 'new_m', 'new_m_ffn1_norm': 'new_m', 'new_m_ffn1_w_gate': 'new_m', 'new_m_ffn1_w_up': 'new_m', 'new_m_ffn1_w_down': 'new_m', 'new_m_mix_norm': 'new_m', 'new_m_ffn2_norm': 'new_m', 'new_m_ffn2_w_gate': 'new_m', 'new_m_ffn2_w_up': 'new_m', 'new_m_ffn2_w_down': 'new_m', 'new_m_pool_w': 'new_m', 'new_m_pool_scale': 'new_m', 'new_m_kv_in_norm': 'new_m', 'new_m_w_dkv': 'new_m', 'new_m_kv_latent_norm': 'new_m', 'new_m_w_uk': 'new_m', 'new_m_w_uv': 'new_m', 'new_m_w_dq': 'new_m', 'new_m_q_latent_norm': 'new_m', 'new_m_w_uq': 'new_m', 'new_m_w_o': 'new_m', 'new_m_final_norm': 'new_m', 'new_v_meta_tokens': 'new_v', 'new_v_ffn1_norm': 'new_v', 'new_v_ffn1_w_gate': 'new_v', 'new_v_ffn1_w_up': 'new_v', 'new_v_ffn1_w_down': 'new_v', 'new_v_mix_norm': 'new_v', 'new_v_ffn2_norm': 'new_v', 'new_v_ffn2_w_gate': 'new_v', 'new_v_ffn2_w_up': 'new_v', 'new_v_ffn2_w_down': 'new_v', 'new_v_pool_w': 'new_v', 'new_v_pool_scale': 'new_v', 'new_v_kv_in_norm': 'new_v', 'new_v_w_dkv': 'new_v', 'new_v_kv_latent_norm': 'new_v', 'new_v_w_uk': 'new_v', 'new_v_w_uv': 'new_v', 'new_v_w_dq': 'new_v', 'new_v_q_latent_norm': 'new_v', 'new_v_w_uq': 'new_v', 'new_v_w_o': 'new_v', 'new_v_final_norm': 'new_v'}


def _forward(args):
    return _fwd_reference(*[args[k] for k in FWD_PARAMS])


def _output_shape():
    def fwd():
        inp = _fwd_setup_inputs(0)
        return _fwd_reference(*[inp[k] for k in FWD_PARAMS])
    out = _jax.eval_shape(fwd)
    return out.shape, out.dtype

N_MICROBATCH = 1
ADAM_LR = 0.001
ADAM_B1 = 0.9
ADAM_B2 = 0.999
ADAM_EPS = 1e-08
ADAM_WD = 0.01
ADAM_STEP = 10
PER_EXAMPLE_BATCH_AXIS = {'x': 0, 'loss_target': 0}
SHARED_INPUTS = []
_WEIGHT_DTYPES = {'meta_tokens': _jnp.float32, 'ffn1_norm': _jnp.float32, 'ffn1_w_gate': _jnp.float32, 'ffn1_w_up': _jnp.float32, 'ffn1_w_down': _jnp.float32, 'mix_norm': _jnp.float32, 'ffn2_norm': _jnp.float32, 'ffn2_w_gate': _jnp.float32, 'ffn2_w_up': _jnp.float32, 'ffn2_w_down': _jnp.float32, 'pool_w': _jnp.float32, 'pool_scale': _jnp.float32, 'kv_in_norm': _jnp.float32, 'w_dkv': _jnp.float32, 'kv_latent_norm': _jnp.float32, 'w_uk': _jnp.float32, 'w_uv': _jnp.float32, 'w_dq': _jnp.float32, 'q_latent_norm': _jnp.float32, 'w_uq': _jnp.float32, 'w_o': _jnp.float32, 'final_norm': _jnp.float32}
MOMENT_SCALE = {'meta_tokens': 5.188472e-03, 'ffn1_norm': 8.977266e-02, 'ffn1_w_gate': 3.772945e-02, 'ffn1_w_up': 3.744343e-02, 'ffn1_w_down': 6.229284e-02, 'mix_norm': 2.829554e-01, 'ffn2_norm': 7.313034e-02, 'ffn2_w_gate': 3.089345e-02, 'ffn2_w_up': 3.117236e-02, 'ffn2_w_down': 5.187242e-02, 'pool_w': 3.143404e-01, 'pool_scale': 3.166964e+00, 'kv_in_norm': 4.021225e-02, 'w_dkv': 7.572070e-02, 'kv_latent_norm': 8.483259e-02, 'w_uk': 3.749887e-02, 'w_uv': 4.203383e-02, 'w_dq': 3.631008e-02, 'q_latent_norm': 3.472213e-02, 'w_uq': 2.556519e-02, 'w_o': 2.095645e-02, 'final_norm': 6.513117e+01}


def _to_microbatches(a, axis):
    t = _jnp.moveaxis(a, axis, 0)
    t = t.reshape((N_MICROBATCH, t.shape[0] // N_MICROBATCH) + t.shape[1:])
    return _jnp.moveaxis(t, 1, axis + 1)


def setup_inputs(seed: int = 0) -> dict:
    inp = _fwd_setup_inputs(seed)
    key = _jax.random.fold_in(_jax.random.key(seed), 7919)
    shape, _ = _output_shape()
    out = dict(inp)
    out["loss_target"] = _jax.random.normal(_jax.random.fold_in(key, 0), shape, _jnp.float32)
    for i, name in enumerate(TWIN_WEIGHTS):
        w = inp[name].astype(_jnp.float32)
        if MOMENT_SCALE is None:
            s = _jnp.sqrt(_jnp.mean(_jnp.square(w)) + 1e-30)
        else:
            s = MOMENT_SCALE[name]
        km, kv = _jax.random.split(_jax.random.fold_in(key, i + 1))
        out[name] = w
        out["m_" + name] = s * _jax.random.normal(km, w.shape, _jnp.float32)
        out["v_" + name] = (s * s) * _jax.random.uniform(kv, w.shape, _jnp.float32, 0.5, 1.5)
    if N_MICROBATCH > 1:
        for name, axis in PER_EXAMPLE_BATCH_AXIS.items():
            out[name] = _to_microbatches(out[name], axis)
    return {'x': out['x'], 'meta_tokens': out['meta_tokens'], 'ffn1_norm': out['ffn1_norm'], 'ffn1_w_gate': out['ffn1_w_gate'], 'ffn1_w_up': out['ffn1_w_up'], 'ffn1_w_down': out['ffn1_w_down'], 'mix_norm': out['mix_norm'], 'ffn2_norm': out['ffn2_norm'], 'ffn2_w_gate': out['ffn2_w_gate'], 'ffn2_w_up': out['ffn2_w_up'], 'ffn2_w_down': out['ffn2_w_down'], 'pool_w': out['pool_w'], 'pool_scale': out['pool_scale'], 'kv_in_norm': out['kv_in_norm'], 'w_dkv': out['w_dkv'], 'kv_latent_norm': out['kv_latent_norm'], 'w_uk': out['w_uk'], 'w_uv': out['w_uv'], 'w_dq': out['w_dq'], 'q_latent_norm': out['q_latent_norm'], 'w_uq': out['w_uq'], 'w_o': out['w_o'], 'final_norm': out['final_norm'], 'loss_target': out['loss_target'], 'm_meta_tokens': out['m_meta_tokens'], 'm_ffn1_norm': out['m_ffn1_norm'], 'm_ffn1_w_gate': out['m_ffn1_w_gate'], 'm_ffn1_w_up': out['m_ffn1_w_up'], 'm_ffn1_w_down': out['m_ffn1_w_down'], 'm_mix_norm': out['m_mix_norm'], 'm_ffn2_norm': out['m_ffn2_norm'], 'm_ffn2_w_gate': out['m_ffn2_w_gate'], 'm_ffn2_w_up': out['m_ffn2_w_up'], 'm_ffn2_w_down': out['m_ffn2_w_down'], 'm_pool_w': out['m_pool_w'], 'm_pool_scale': out['m_pool_scale'], 'm_kv_in_norm': out['m_kv_in_norm'], 'm_w_dkv': out['m_w_dkv'], 'm_kv_latent_norm': out['m_kv_latent_norm'], 'm_w_uk': out['m_w_uk'], 'm_w_uv': out['m_w_uv'], 'm_w_dq': out['m_w_dq'], 'm_q_latent_norm': out['m_q_latent_norm'], 'm_w_uq': out['m_w_uq'], 'm_w_o': out['m_w_o'], 'm_final_norm': out['m_final_norm'], 'v_meta_tokens': out['v_meta_tokens'], 'v_ffn1_norm': out['v_ffn1_norm'], 'v_ffn1_w_gate': out['v_ffn1_w_gate'], 'v_ffn1_w_up': out['v_ffn1_w_up'], 'v_ffn1_w_down': out['v_ffn1_w_down'], 'v_mix_norm': out['v_mix_norm'], 'v_ffn2_norm': out['v_ffn2_norm'], 'v_ffn2_w_gate': out['v_ffn2_w_gate'], 'v_ffn2_w_up': out['v_ffn2_w_up'], 'v_ffn2_w_down': out['v_ffn2_w_down'], 'v_pool_w': out['v_pool_w'], 'v_pool_scale': out['v_pool_scale'], 'v_kv_in_norm': out['v_kv_in_norm'], 'v_w_dkv': out['v_w_dkv'], 'v_kv_latent_norm': out['v_kv_latent_norm'], 'v_w_uk': out['v_w_uk'], 'v_w_uv': out['v_w_uv'], 'v_w_dq': out['v_w_dq'], 'v_q_latent_norm': out['v_q_latent_norm'], 'v_w_uq': out['v_w_uq'], 'v_w_o': out['v_w_o'], 'v_final_norm': out['v_final_norm']}


def _loss(weights, diff, rest, loss_target):
    with _jax.named_scope("forward"):
        args = {**rest, TWIN_DIFF_INPUT: diff, **{k: w.astype(_WEIGHT_DTYPES[k]) for k, w in weights.items()}}
        y = _forward(args)
    with _jax.named_scope("loss_head"):
        err = _jnp.square(y.astype(_jnp.float32) - loss_target)
        return 0.5 * _jnp.sum(_jnp.mean(err, axis=-1)) if err.ndim else 0.5 * err


def _adamw(w, g, m, v):
    m = ADAM_B1 * m + (1.0 - ADAM_B1) * g
    v = ADAM_B2 * v + (1.0 - ADAM_B2) * _jnp.square(g)
    m_hat = m / (1.0 - ADAM_B1 ** ADAM_STEP)
    v_hat = v / (1.0 - ADAM_B2 ** ADAM_STEP)
    delta = -ADAM_LR * (m_hat / (_jnp.sqrt(v_hat) + ADAM_EPS) + ADAM_WD * w)
    return delta, m, v


def reference(x, meta_tokens, ffn1_norm, ffn1_w_gate, ffn1_w_up, ffn1_w_down, mix_norm, ffn2_norm, ffn2_w_gate, ffn2_w_up, ffn2_w_down, pool_w, pool_scale, kv_in_norm, w_dkv, kv_latent_norm, w_uk, w_uv, w_dq, q_latent_norm, w_uq, w_o, final_norm, loss_target, m_meta_tokens, m_ffn1_norm, m_ffn1_w_gate, m_ffn1_w_up, m_ffn1_w_down, m_mix_norm, m_ffn2_norm, m_ffn2_w_gate, m_ffn2_w_up, m_ffn2_w_down, m_pool_w, m_pool_scale, m_kv_in_norm, m_w_dkv, m_kv_latent_norm, m_w_uk, m_w_uv, m_w_dq, m_q_latent_norm, m_w_uq, m_w_o, m_final_norm, v_meta_tokens, v_ffn1_norm, v_ffn1_w_gate, v_ffn1_w_up, v_ffn1_w_down, v_mix_norm, v_ffn2_norm, v_ffn2_w_gate, v_ffn2_w_up, v_ffn2_w_down, v_pool_w, v_pool_scale, v_kv_in_norm, v_w_dkv, v_kv_latent_norm, v_w_uk, v_w_uv, v_w_dq, v_q_latent_norm, v_w_uq, v_w_o, v_final_norm):
    given = dict(x=x, meta_tokens=meta_tokens, ffn1_norm=ffn1_norm, ffn1_w_gate=ffn1_w_gate, ffn1_w_up=ffn1_w_up, ffn1_w_down=ffn1_w_down, mix_norm=mix_norm, ffn2_norm=ffn2_norm, ffn2_w_gate=ffn2_w_gate, ffn2_w_up=ffn2_w_up, ffn2_w_down=ffn2_w_down, pool_w=pool_w, pool_scale=pool_scale, kv_in_norm=kv_in_norm, w_dkv=w_dkv, kv_latent_norm=kv_latent_norm, w_uk=w_uk, w_uv=w_uv, w_dq=w_dq, q_latent_norm=q_latent_norm, w_uq=w_uq, w_o=w_o, final_norm=final_norm, loss_target=loss_target, m_meta_tokens=m_meta_tokens, m_ffn1_norm=m_ffn1_norm, m_ffn1_w_gate=m_ffn1_w_gate, m_ffn1_w_up=m_ffn1_w_up, m_ffn1_w_down=m_ffn1_w_down, m_mix_norm=m_mix_norm, m_ffn2_norm=m_ffn2_norm, m_ffn2_w_gate=m_ffn2_w_gate, m_ffn2_w_up=m_ffn2_w_up, m_ffn2_w_down=m_ffn2_w_down, m_pool_w=m_pool_w, m_pool_scale=m_pool_scale, m_kv_in_norm=m_kv_in_norm, m_w_dkv=m_w_dkv, m_kv_latent_norm=m_kv_latent_norm, m_w_uk=m_w_uk, m_w_uv=m_w_uv, m_w_dq=m_w_dq, m_q_latent_norm=m_q_latent_norm, m_w_uq=m_w_uq, m_w_o=m_w_o, m_final_norm=m_final_norm, v_meta_tokens=v_meta_tokens, v_ffn1_norm=v_ffn1_norm, v_ffn1_w_gate=v_ffn1_w_gate, v_ffn1_w_up=v_ffn1_w_up, v_ffn1_w_down=v_ffn1_w_down, v_mix_norm=v_mix_norm, v_ffn2_norm=v_ffn2_norm, v_ffn2_w_gate=v_ffn2_w_gate, v_ffn2_w_up=v_ffn2_w_up, v_ffn2_w_down=v_ffn2_w_down, v_pool_w=v_pool_w, v_pool_scale=v_pool_scale, v_kv_in_norm=v_kv_in_norm, v_w_dkv=v_w_dkv, v_kv_latent_norm=v_kv_latent_norm, v_w_uk=v_w_uk, v_w_uv=v_w_uv, v_w_dq=v_w_dq, v_q_latent_norm=v_q_latent_norm, v_w_uq=v_w_uq, v_w_o=v_w_o, v_final_norm=v_final_norm)
    weights = {n: given[n] for n in TWIN_WEIGHTS}
    shared = {n: given[n] for n in SHARED_INPUTS}
    per_example = {n: given[n] for n in ['x']}
    grad_fn = _jax.value_and_grad(_loss, argnums=(0, 1))

    def one_microbatch(ex, loss_target):
        ex = dict(ex)
        diff = ex.pop(TWIN_DIFF_INPUT)
        return grad_fn(weights, diff, {**shared, **ex}, loss_target)

    if N_MICROBATCH == 1:
        loss, (grad_w, grad_x) = one_microbatch(per_example, given["loss_target"])
    else:
        def body(carry, xs):
            loss_sum, grad_sum = carry
            l_k, (gw_k, gx_k) = one_microbatch(xs[0], xs[1])
            with _jax.named_scope("update"):
                return (loss_sum + l_k, _jax.tree.map(_jnp.add, grad_sum, gw_k)), gx_k

        init = (_jnp.zeros((), _jnp.float32), _jax.tree.map(_jnp.zeros_like, weights))
        (loss, grad_w), grad_x = _jax.lax.scan(body, init, (per_example, given["loss_target"]))
    with _jax.named_scope("update"):
        delta_w, new_m, new_v = {}, {}, {}
        for n in TWIN_WEIGHTS:
            delta_w[n], new_m[n], new_v[n] = _adamw(weights[n], grad_w[n], given["m_" + n], given["v_" + n])
    return (loss, grad_x, *[grad_w[n] for n in TWIN_WEIGHTS], *[delta_w[n] for n in TWIN_WEIGHTS],
            *[new_m[n] for n in TWIN_WEIGHTS], *[new_v[n] for n in TWIN_WEIGHTS])
```

```python
import functools
import math

import jax
import jax.numpy as jnp
from jax import lax
from jax.experimental import pallas as pl
from jax.experimental.pallas import tpu as pltpu

F32 = jnp.float32
BF16 = jnp.bfloat16
MESH = pl.DeviceIdType.MESH
ANY = pl.BlockSpec(memory_space=pl.ANY)

EPS = 1e-6
CHUNK = 64
CHUNK_SHIFT = 6
N_META = 16
FRONT = 64
META_ROW0 = FRONT - N_META
POOL_WINDOWS = (2, 4, 8, 16)
HALO = 16
N_HEADS = 8
QK_NOPE = 64
QK_ROPE = 32
V_HEAD = 64
HEAD_W = 128
KV_RANK = 256
Q_RANK = 384
ROPE_THETA = 10000.0
NEG = -1e30
N_SHARD = 4
LANES = 128
SUBLANES = 8
PACK_W = 512
VMEM_BIG = 52 * 1024 * 1024

ADAM_LR = 0.001
ADAM_B1 = 0.9
ADAM_B2 = 0.999
ADAM_EPS = 1e-08
ADAM_WD = 0.01
ADAM_STEP = 10

NT = (((1,), (1,)), ((), ()))
TN = (((0,), (0,)), ((), ()))


def _params(sem=None, vmem=None):
    return pltpu.CompilerParams(dimension_semantics=sem, vmem_limit_bytes=vmem)


def _tile(n, pref, mult=SUBLANES):
    best = None
    for t in range(mult, min(n, pref) + 1, mult):
        if n % t == 0:
            best = t
    return best if best is not None else n


def _row_tile(r):
    return 640 if r % 640 == 0 else 128


def _rms(x):
    rstd = lax.rsqrt(jnp.mean(x * x, axis=-1, keepdims=True) + EPS)
    return x * rstd, rstd


def _rms_bwd(xh, rstd, dxh):
    return rstd * (dxh - xh * jnp.mean(dxh * xh, axis=-1, keepdims=True))


def _sigmoid(x):
    return 1.0 / (1.0 + jnp.exp(-x))


def _place():
    x, y, c = lax.axis_index("x"), lax.axis_index("y"), lax.axis_index("c")
    chips = [(1 - x, y), (x, 1 - y), (1 - x, 1 - y)]
    return x, y, c, chips


def all_gather_shards(shards, name):
    n = len(shards)

    def body(*refs):
        ins, outs = refs[:n], refs[n:2 * n]
        send1, recv1, send2, recv2, lsem = refs[2 * n:]
        x, y, c, chips = _place()
        s = 2 * x + y
        sib = (x, y, 1 - c)

        def rcopy(k, j, src, dst, to, first):
            return pltpu.make_async_remote_copy(
                src_ref=src, dst_ref=dst,
                send_sem=(send1 if first else send2).at[k, j],
                recv_sem=(recv1 if first else recv2).at[k, j],
                device_id=to, device_id_type=MESH)

        started, local = [], []
        for k in range(n):
            hf = ins[k].shape[0] // 2
            cp = pltpu.make_async_copy(ins[k], outs[k].at[s], lsem.at[k])
            cp.start()
            local.append(cp)
            for j, (cx, cy) in enumerate(chips):
                r = rcopy(k, j, ins[k].at[pl.ds(c * hf, hf)], outs[k].at[s, pl.ds(c * hf, hf)],
                          (cx, cy, c), True)
                r.start()
                started.append(r)
        for k in range(n):
            hf = ins[k].shape[0] // 2
            for j, (cx, cy) in enumerate(chips):
                blk = outs[k].at[2 * cx + cy, pl.ds(c * hf, hf)]
                rcopy(k, j, blk, blk, (cx, cy, c), True).wait_recv()
                f = rcopy(k, j, blk, blk, sib, False)
                f.start()
                started.append(f)
        for k in range(n):
            hf = ins[k].shape[0] // 2
            for j, (cx, cy) in enumerate(chips):
                blk = outs[k].at[2 * cx + cy, pl.ds((1 - c) * hf, hf)]
                rcopy(k, j, blk, blk, sib, False).wait_recv()
        for r in started:
            r.wait_send()
        for cp in local:
            cp.wait()

    return pl.pallas_call(
        body, name=name,
        out_shape=[jax.ShapeDtypeStruct((N_SHARD,) + a.shape, a.dtype) for a in shards],
        in_specs=[ANY] * n, out_specs=[ANY] * n,
        scratch_shapes=[pltpu.SemaphoreType.DMA((n, 3))] * 4 + [pltpu.SemaphoreType.DMA((n,))],
    )(*shards)


def sibling_swap_halves(gs, name):
    n = len(gs)

    def body(*refs):
        ins, outs = refs[:n], refs[n:2 * n]
        send, recv = refs[2 * n:]
        x, y, c, _ = _place()
        sib = (x, y, 1 - c)
        cps = []
        for k in range(n):
            hf = ins[k].shape[1] // 2
            r = pltpu.make_async_remote_copy(
                src_ref=ins[k].at[:, pl.ds((1 - c) * hf, hf)], dst_ref=outs[k],
                send_sem=send.at[k], recv_sem=recv.at[k], device_id=sib, device_id_type=MESH)
            r.start()
            cps.append(r)
        for r in cps:
            r.wait_recv()
        for r in cps:
            r.wait_send()

    return pl.pallas_call(
        body, name=name,
        out_shape=[jax.ShapeDtypeStruct((a.shape[0], a.shape[1] // 2, a.shape[2]), a.dtype) for a in gs],
        in_specs=[ANY] * n, out_specs=[ANY] * n,
        scratch_shapes=[pltpu.SemaphoreType.DMA((n,))] * 2,
    )(*gs)


def chip_exchange(ps, name):
    n = len(ps)

    def body(*refs):
        ins, outs = refs[:n], refs[n:2 * n]
        send, recv = refs[2 * n:]
        x, y, c, chips = _place()
        cps = []
        for k in range(n):
            for j, (cx, cy) in enumerate(chips):
                r = pltpu.make_async_remote_copy(
                    src_ref=ins[k].at[2 * cx + cy], dst_ref=outs[k].at[j],
                    send_sem=send.at[k, j], recv_sem=recv.at[k, j],
                    device_id=(cx, cy, c), device_id_type=MESH)
                r.start()
                cps.append(r)
        for r in cps:
            r.wait_recv()
        for r in cps:
            r.wait_send()

    return pl.pallas_call(
        body, name=name,
        out_shape=[jax.ShapeDtypeStruct((3,) + a.shape[1:], a.dtype) for a in ps],
        in_specs=[ANY] * n, out_specs=[ANY] * n,
        scratch_shapes=[pltpu.SemaphoreType.DMA((n, 3))] * 2,
    )(*ps)


def sibling_join_halves(ts, name):
    n = len(ts)

    def body(*refs):
        ins, outs = refs[:n], refs[n:2 * n]
        send, recv, lsem = refs[2 * n:]
        x, y, c, _ = _place()
        sib = (x, y, 1 - c)
        cps, local = [], []
        for k in range(n):
            hf = ins[k].shape[0]
            cp = pltpu.make_async_copy(ins[k], outs[k].at[pl.ds(c * hf, hf)], lsem.at[k])
            cp.start()
            local.append(cp)
            r = pltpu.make_async_remote_copy(
                src_ref=ins[k], dst_ref=outs[k].at[pl.ds(c * hf, hf)],
                send_sem=send.at[k], recv_sem=recv.at[k], device_id=sib, device_id_type=MESH)
            r.start()
            cps.append(r)
        for k in range(n):
            hf = ins[k].shape[0]
            pltpu.make_async_remote_copy(
                src_ref=ins[k], dst_ref=outs[k].at[pl.ds((1 - c) * hf, hf)],
                send_sem=send.at[k], recv_sem=recv.at[k], device_id=sib, device_id_type=MESH).wait_recv()
        for r in cps:
            r.wait_send()
        for cp in local:
            cp.wait()

    return pl.pallas_call(
        body, name=name,
        out_shape=[jax.ShapeDtypeStruct((2 * a.shape[0], a.shape[1]), a.dtype) for a in ts],
        in_specs=[ANY] * n, out_specs=[ANY] * n,
        scratch_shapes=[pltpu.SemaphoreType.DMA((n,))] * 3,
    )(*ts)


def all_reduce_small(part, name):
    m, w = part.shape

    def body(x_ref, tot_ref, gat_ref, send_sems, recv_sems):
        x, y, c, chips = _place()
        me, sib = (x, y, c), (x, y, 1 - c)

        def slot(px, py, pc):
            return gat_ref.at[4 * px + 2 * py + pc]

        def copy(k, block, to, src=None):
            return pltpu.make_async_remote_copy(
                src_ref=slot(*block) if src is None else src, dst_ref=slot(*block),
                send_sem=send_sems.at[k], recv_sem=recv_sems.at[k], device_id=to, device_id_type=MESH)

        gat_ref[4 * x + 2 * y + c] = x_ref[...]
        first = [copy(0, me, sib, src=x_ref)]
        first += [copy(1 + j, me, (*chip, c), src=x_ref) for j, chip in enumerate(chips)]
        for cp in first:
            cp.start()
        passed = [copy(4 + j, (*chip, c), sib) for j, chip in enumerate(chips)]
        for j, chip in enumerate(chips):
            copy(1 + j, (*chip, c), me).wait_recv()
            passed[j].start()
        copy(0, sib, me).wait_recv()
        for j, chip in enumerate(chips):
            copy(4 + j, (*chip, 1 - c), me).wait_recv()
        for cp in first + passed:
            cp.wait_send()
        tot = gat_ref[0]
        for d in range(1, 8):
            tot = tot + gat_ref[d]
        tot_ref[...] = tot

    return pl.pallas_call(
        body, name=name,
        out_shape=jax.ShapeDtypeStruct((m, w), F32),
        in_specs=[pl.BlockSpec(memory_space=pltpu.VMEM)],
        out_specs=pl.BlockSpec(memory_space=pltpu.VMEM),
        scratch_shapes=[pltpu.VMEM((8, m, w), F32), pltpu.SemaphoreType.DMA((7,)), pltpu.SemaphoreType.DMA((7,))],
    )(part)


def add_own_half(g, rec, c_arr, name):
    _, a, b = g.shape
    hf = a // 2
    tb = _tile(hf, 256)
    nb = hf // tb

    def body(c_ref, g_ref, r_ref, o_ref):
        o_ref[...] = g_ref[...] + r_ref[...]

    return pl.pallas_call(
        body, name=name,
        out_shape=jax.ShapeDtypeStruct((N_SHARD, hf, b), F32),
        grid_spec=pltpu.PrefetchScalarGridSpec(
            num_scalar_prefetch=1, grid=(N_SHARD, nb),
            in_specs=[pl.BlockSpec((1, tb, b), lambda s, i, c: (s, c[0] * nb + i, 0)),
                      pl.BlockSpec((1, tb, b), lambda s, i, c: (s, i, 0))],
            out_specs=pl.BlockSpec((1, tb, b), lambda s, i, c: (s, i, 0))),
        compiler_params=_params(("arbitrary", "arbitrary")),
    )(c_arr, g, rec)


def sum_four(p, rec, s_arr, name):
    _, ah, b = p.shape
    tb = _tile(ah, 256)

    def body(s_ref, p_ref, r0, r1, r2, o_ref):
        o_ref[...] = ((p_ref[0] + r0[0]) + r1[0]) + r2[0]

    def rspec(j):
        return pl.BlockSpec((1, tb, b), lambda i, s: (j, i, 0))

    return pl.pallas_call(
        body, name=name,
        out_shape=jax.ShapeDtypeStruct((ah, b), F32),
        grid_spec=pltpu.PrefetchScalarGridSpec(
            num_scalar_prefetch=1, grid=(ah // tb,),
            in_specs=[pl.BlockSpec((1, tb, b), lambda i, s: (s[0], i, 0)), rspec(0), rspec(1), rspec(2)],
            out_specs=pl.BlockSpec((tb, b), lambda i, s: (i, 0))),
        compiler_params=_params(("arbitrary",)),
    )(s_arr, p, rec, rec, rec)


def adamw(w, g, m, v, name):
    a, b = w.shape
    tb = _tile(a, 256)
    c1 = 1.0 - ADAM_B1 ** ADAM_STEP
    c2 = 1.0 - ADAM_B2 ** ADAM_STEP

    def body(w_ref, g_ref, m_ref, v_ref, d_ref, mo_ref, vo_ref):
        g_ = g_ref[...]
        m_ = ADAM_B1 * m_ref[...] + (1.0 - ADAM_B1) * g_
        v_ = ADAM_B2 * v_ref[...] + (1.0 - ADAM_B2) * (g_ * g_)
        m_hat = m_ / c1
        v_hat = v_ / c2
        d_ref[...] = -ADAM_LR * (m_hat / (jnp.sqrt(v_hat) + ADAM_EPS) + ADAM_WD * w_ref[...])
        mo_ref[...] = m_
        vo_ref[...] = v_

    spec = pl.BlockSpec((tb, b), lambda i: (i, 0))
    return pl.pallas_call(
        body, name=name,
        out_shape=[jax.ShapeDtypeStruct((a, b), F32)] * 3,
        grid=(a // tb,), in_specs=[spec] * 4, out_specs=[spec] * 3,
        compiler_params=_params(("arbitrary",)),
    )(w, g, m, v)


def reduce_scatter(gs, c_arr, s_arr, tag):
    rec_a = sibling_swap_halves(gs, f"rs_swap_{tag}")
    ps = [add_own_half(g, r, c_arr, f"rs_add_{tag}_{k}") for k, (g, r) in enumerate(zip(gs, rec_a))]
    rec_b = chip_exchange(ps, f"rs_ici_{tag}")
    ts = [sum_four(p, r, s_arr, f"rs_sum_{tag}_{k}") for k, (p, r) in enumerate(zip(ps, rec_b))]
    return sibling_join_halves(ts, f"rs_join_{tag}")


def ffn_fwd(h, gamma, wg, wu, wd, name):
    r, d = h.shape
    ns, _, fs = wg.shape
    tm = _row_tile(r)

    def body(h_ref, g_ref, wg_ref, wu_ref, wd_ref, ho_ref, gg_ref, uu_ref, hn_sc, acc_sc):
        s = pl.program_id(1)

        @pl.when(s == 0)
        def _():
            xh, _ = _rms(h_ref[...])
            hn_sc[...] = (xh * g_ref[...]).astype(BF16)
            acc_sc[...] = jnp.zeros_like(acc_sc)

        hn = hn_sc[...]
        g = jnp.dot(hn, wg_ref[0], preferred_element_type=F32)
        u = jnp.dot(hn, wu_ref[0], preferred_element_type=F32)
        gg_ref[0] = g.astype(BF16)
        uu_ref[0] = u.astype(BF16)
        a = (g * _sigmoid(g) * u).astype(BF16)
        acc_sc[...] += jnp.dot(a, wd_ref[0], preferred_element_type=F32)

        @pl.when(s == ns - 1)
        def _():
            ho_ref[...] = h_ref[...] + 0.5 * acc_sc[...]

    return pl.pallas_call(
        body, name=name,
        out_shape=[jax.ShapeDtypeStruct((r, d), F32), jax.ShapeDtypeStruct((ns, r, fs), BF16),
                   jax.ShapeDtypeStruct((ns, r, fs), BF16)],
        grid=(r // tm, ns),
        in_specs=[pl.BlockSpec((tm, d), lambda i, s: (i, 0)), pl.BlockSpec((1, d), lambda i, s: (0, 0)),
                  pl.BlockSpec((1, d, fs), lambda i, s: (s, 0, 0)), pl.BlockSpec((1, d, fs), lambda i, s: (s, 0, 0)),
                  pl.BlockSpec((1, fs, d), lambda i, s: (s, 0, 0))],
        out_specs=[pl.BlockSpec((tm, d), lambda i, s: (i, 0)), pl.BlockSpec((1, tm, fs), lambda i, s: (s, i, 0)),
                   pl.BlockSpec((1, tm, fs), lambda i, s: (s, i, 0))],
        scratch_shapes=[pltpu.VMEM((tm, d), BF16), pltpu.VMEM((tm, d), F32)],
        compiler_params=_params(("arbitrary", "arbitrary"), VMEM_BIG),
    )(h, gamma, wg, wu, wd)


def ffn_bwd_act(h, gamma, dh, gg, uu, wg, wu, wd, name):
    r, d = h.shape
    ns, _, fs = wg.shape
    tm = _row_tile(r)

    def body(h_ref, g_ref, dh_ref, gg_ref, uu_ref, wg_ref, wu_ref, wd_ref,
             dho_ref, dgam_ref, hn_ref, dy_ref, dg_ref, du_ref, a_ref, acc_sc):
        i, s = pl.program_id(0), pl.program_id(1)

        @pl.when(s == 0)
        def _():
            xh, _ = _rms(h_ref[...])
            hn_ref[...] = (xh * g_ref[...]).astype(BF16)
            dy_ref[...] = (0.5 * dh_ref[...]).astype(BF16)
            acc_sc[...] = jnp.zeros_like(acc_sc)

        @pl.when((i == 0) & (s == 0))
        def _():
            dgam_ref[...] = jnp.zeros_like(dgam_ref)

        g = gg_ref[0].astype(F32)
        u = uu_ref[0].astype(F32)
        da = lax.dot_general(dy_ref[...], wd_ref[0], NT, preferred_element_type=F32)
        sig = _sigmoid(g)
        sl = g * sig
        a_ref[0] = (sl * u).astype(BF16)
        du = (da * sl).astype(BF16)
        dg = (da * u * (sig * (1.0 + g * (1.0 - sig)))).astype(BF16)
        dg_ref[0] = dg
        du_ref[0] = du
        acc_sc[...] += (lax.dot_general(dg, wg_ref[0], NT, preferred_element_type=F32)
                        + lax.dot_general(du, wu_ref[0], NT, preferred_element_type=F32))

        @pl.when(s == ns - 1)
        def _():
            xh, rstd = _rms(h_ref[...])
            dhn = acc_sc[...]
            dgam_ref[...] += jnp.sum(dhn * xh, axis=0, keepdims=True)
            dho_ref[...] = dh_ref[...] + _rms_bwd(xh, rstd, dhn * g_ref[...])

    row = pl.BlockSpec((tm, d), lambda i, s: (i, 0))
    act = pl.BlockSpec((1, tm, fs), lambda i, s: (s, i, 0))
    return pl.pallas_call(
        body, name=name,
        out_shape=[jax.ShapeDtypeStruct((r, d), F32), jax.ShapeDtypeStruct((1, d), F32),
                   jax.ShapeDtypeStruct((r, d), BF16), jax.ShapeDtypeStruct((r, d), BF16),
                   jax.ShapeDtypeStruct((ns, r, fs), BF16), jax.ShapeDtypeStruct((ns, r, fs), BF16),
                   jax.ShapeDtypeStruct((ns, r, fs), BF16)],
        grid=(r // tm, ns),
        in_specs=[row, pl.BlockSpec((1, d), lambda i, s: (0, 0)), row, act, act,
                  pl.BlockSpec((1, d, fs), lambda i, s: (s, 0, 0)), pl.BlockSpec((1, d, fs), lambda i, s: (s, 0, 0)),
                  pl.BlockSpec((1, fs, d), lambda i, s: (s, 0, 0))],
        out_specs=[row, pl.BlockSpec((1, d), lambda i, s: (0, 0)), row, row, act, act, act],
        scratch_shapes=[pltpu.VMEM((tm, d), F32)],
        compiler_params=_params(("arbitrary", "arbitrary"), VMEM_BIG),
    )(h, gamma, dh, gg, uu, wg, wu, wd)


def ffn_bwd_weights(hn, dy, a, dg, du, name):
    r, d = hn.shape
    ns, _, fs = a.shape
    tm = _row_tile(r)

    def body(hn_ref, dy_ref, a_ref, dg_ref, du_ref, wg_ref, wu_ref, wd_ref):
        @pl.when(pl.program_id(1) == 0)
        def _():
            wg_ref[...] = jnp.zeros_like(wg_ref)
            wu_ref[...] = jnp.zeros_like(wu_ref)
            wd_ref[...] = jnp.zeros_like(wd_ref)

        hn_ = hn_ref[...]
        wg_ref[0] += lax.dot_general(hn_, dg_ref[0], TN, preferred_element_type=F32)
        wu_ref[0] += lax.dot_general(hn_, du_ref[0], TN, preferred_element_type=F32)
        wd_ref[0] += lax.dot_general(a_ref[0], dy_ref[...], TN, preferred_element_type=F32)

    row = pl.BlockSpec((tm, d), lambda s, i: (i, 0))
    act = pl.BlockSpec((1, tm, fs), lambda s, i: (s, i, 0))
    return pl.pallas_call(
        body, name=name,
        out_shape=[jax.ShapeDtypeStruct((ns, d, fs), F32), jax.ShapeDtypeStruct((ns, d, fs), F32),
                   jax.ShapeDtypeStruct((ns, fs, d), F32)],
        grid=(ns, r // tm),
        in_specs=[row, row, act, act, act],
        out_specs=[pl.BlockSpec((1, d, fs), lambda s, i: (s, 0, 0)), pl.BlockSpec((1, d, fs), lambda s, i: (s, 0, 0)),
                   pl.BlockSpec((1, fs, d), lambda s, i: (s, 0, 0))],
        compiler_params=_params(("arbitrary", "arbitrary"), VMEM_BIG),
    )(hn, dy, a, dg, du)


def norm_fwd(x, gamma, name):
    r = x.shape[0]
    w = gamma.shape[1]
    tm = _row_tile(r)

    def body(x_ref, g_ref, o_ref):
        xh, _ = _rms(x_ref[...])
        o_ref[...] = (xh * g_ref[...]).astype(BF16)

    return pl.pallas_call(
        body, name=name, out_shape=jax.ShapeDtypeStruct((r, w), BF16), grid=(r // tm,),
        in_specs=[pl.BlockSpec((tm, w), lambda i: (i, 0)), pl.BlockSpec((1, w), lambda i: (0, 0))],
        out_specs=pl.BlockSpec((tm, w), lambda i: (i, 0)),
        compiler_params=_params(("arbitrary",)),
    )(x, gamma)


def norm_bwd(x, gamma, dy, dres, name):
    r = x.shape[0]
    w = gamma.shape[1]
    tm = _row_tile(r)
    has_res = dres is not None

    def body(*refs):
        if has_res:
            x_ref, g_ref, dy_ref, dr_ref, dx_ref, dgam_ref = refs
        else:
            x_ref, g_ref, dy_ref, dx_ref, dgam_ref = refs

        @pl.when(pl.program_id(0) == 0)
        def _():
            dgam_ref[...] = jnp.zeros_like(dgam_ref)

        xh, rstd = _rms(x_ref[...])
        dy_ = dy_ref[...].astype(F32)
        dgam_ref[...] += jnp.sum(dy_ * xh, axis=0, keepdims=True)
        dx = _rms_bwd(xh, rstd, dy_ * g_ref[...])
        if has_res:
            dx = dx + dr_ref[...]
        dx_ref[...] = dx

    row = pl.BlockSpec((tm, w), lambda i: (i, 0))
    vec = pl.BlockSpec((1, w), lambda i: (0, 0))
    ins = [x, gamma, dy] + ([dres] if has_res else [])
    return pl.pallas_call(
        body, name=name,
        out_shape=[jax.ShapeDtypeStruct((r, w), F32), jax.ShapeDtypeStruct((1, w), F32)],
        grid=(r // tm,), in_specs=[row, vec, row] + ([row] if has_res else []), out_specs=[row, vec],
        compiler_params=_params(("arbitrary",)),
    )(*ins)


def rowmm(a, w, name, *, nt=False, res=None, out_dtype=F32, heads_out=False):
    ha, r, ka = a.shape
    hw = w.shape[0]
    nh = max(ha, hw)
    n = w.shape[1] if nt else w.shape[2]
    tm = _row_tile(r)
    dims = NT if nt else (((1,), (0,)), ((), ()))
    has_res = res is not None

    def prod(a_ref, w_ref):
        return lax.dot_general(a_ref[0].astype(BF16), w_ref[0], dims, preferred_element_type=F32)

    if heads_out:
        def body(a_ref, w_ref, o_ref):
            o_ref[0] = prod(a_ref, w_ref).astype(out_dtype)

        return pl.pallas_call(
            body, name=name, out_shape=jax.ShapeDtypeStruct((nh, r, n), out_dtype), grid=(nh, r // tm),
            in_specs=[pl.BlockSpec((1, tm, ka), lambda h, i: (h if ha > 1 else 0, i, 0)),
                      pl.BlockSpec((1,) + w.shape[1:], lambda h, i: (h if hw > 1 else 0, 0, 0))],
            out_specs=pl.BlockSpec((1, tm, n), lambda h, i: (h, i, 0)),
            compiler_params=_params(("arbitrary", "arbitrary")),
        )(a, w)

    def body(*refs):
        if has_res:
            a_ref, w_ref, r_ref, o_ref, acc_sc = refs
        else:
            a_ref, w_ref, o_ref, acc_sc = refs
        h = pl.program_id(1)

        @pl.when(h == 0)
        def _():
            acc_sc[...] = jnp.zeros_like(acc_sc)

        acc_sc[...] += prod(a_ref, w_ref)

        @pl.when(h == nh - 1)
        def _():
            out = acc_sc[...]
            if has_res:
                out = out + r_ref[...]
            o_ref[...] = out.astype(out_dtype)

    in_specs = [pl.BlockSpec((1, tm, ka), lambda i, h: (h if ha > 1 else 0, i, 0)),
                pl.BlockSpec((1,) + w.shape[1:], lambda i, h: (h if hw > 1 else 0, 0, 0))]
    ins = [a, w]
    if has_res:
        in_specs.append(pl.BlockSpec((tm, n), lambda i, h: (i, 0)))
        ins.append(res)
    return pl.pallas_call(
        body, name=name, out_shape=jax.ShapeDtypeStruct((r, n), out_dtype), grid=(r // tm, nh),
        in_specs=in_specs, out_specs=pl.BlockSpec((tm, n), lambda i, h: (i, 0)),
        scratch_shapes=[pltpu.VMEM((tm, n), F32)],
        compiler_params=_params(("arbitrary", "arbitrary")),
    )(*ins)


def tnmm(a, b, name):
    ha, r, ka = a.shape
    hb, _, nb = b.shape
    nh = max(ha, hb)
    tm = _row_tile(r)

    def body(a_ref, b_ref, o_ref):
        @pl.when(pl.program_id(1) == 0)
        def _():
            o_ref[...] = jnp.zeros_like(o_ref)

        o_ref[0] += lax.dot_general(a_ref[0].astype(BF16), b_ref[0].astype(BF16), TN, preferred_element_type=F32)

    return pl.pallas_call(
        body, name=name, out_shape=jax.ShapeDtypeStruct((nh, ka, nb), F32), grid=(nh, r // tm),
        in_specs=[pl.BlockSpec((1, tm, ka), lambda h, i: (h if ha > 1 else 0, i, 0)),
                  pl.BlockSpec((1, tm, nb), lambda h, i: (h if hb > 1 else 0, i, 0))],
        out_specs=pl.BlockSpec((1, ka, nb), lambda h, i: (h, 0, 0)),
        compiler_params=_params(("arbitrary", "arbitrary")),
    )(a, b)


def rope_tables(r):
    inv = 1.0 / (ROPE_THETA ** (jnp.arange(0, QK_ROPE, 2, dtype=F32) / QK_ROPE))
    pos = (jnp.arange(r, dtype=F32) - META_ROW0)[:, None]
    ang = pos * inv[None, :]
    cos, sin = jnp.cos(ang), jnp.sin(ang)
    ones = jnp.ones((r, HEAD_W - QK_ROPE), F32)
    ctab = jnp.concatenate([cos, cos, ones], axis=1)
    stab = jnp.concatenate([-sin, sin, jnp.zeros_like(ones)], axis=1)
    return ctab, stab


def _swap_halves(z):
    lane = lax.broadcasted_iota(jnp.int32, z.shape, 1)
    up = pltpu.roll(z, HEAD_W - QK_ROPE // 2, 1)
    down = pltpu.roll(z, QK_ROPE // 2, 1)
    return jnp.where(lane < QK_ROPE // 2, up, jnp.where(lane < QK_ROPE, down, 0.0))


def proj_rope(a, w, ctab, stab, extra, name):
    r, ka = a.shape
    nh = w.shape[0]
    tm = _row_tile(r)
    has_extra = extra is not None

    def body(*refs):
        if has_extra:
            a_ref, w_ref, c_ref, s_ref, e_ref, o_ref = refs
        else:
            a_ref, w_ref, c_ref, s_ref, o_ref = refs
        x = jnp.dot(a_ref[...], w_ref[0], preferred_element_type=F32)
        if has_extra:
            x = x + e_ref[...]
        o_ref[0] = (x * c_ref[...] + _swap_halves(x) * s_ref[...]).astype(BF16)

    tab = pl.BlockSpec((tm, HEAD_W), lambda h, i: (i, 0))
    in_specs = [pl.BlockSpec((tm, ka), lambda h, i: (i, 0)), pl.BlockSpec((1, ka, HEAD_W), lambda h, i: (h, 0, 0)),
                tab, tab]
    ins = [a, w, ctab, stab]
    if has_extra:
        in_specs.append(pl.BlockSpec((tm, HEAD_W), lambda h, i: (i, 2)))
        ins.append(extra)
    return pl.pallas_call(
        body, name=name, out_shape=jax.ShapeDtypeStruct((nh, r, HEAD_W), BF16), grid=(nh, r // tm),
        in_specs=in_specs, out_specs=pl.BlockSpec((1, tm, HEAD_W), lambda h, i: (h, i, 0)),
        compiler_params=_params(("arbitrary", "arbitrary")),
    )(*ins)


def rope_bwd_heads(d, ctab, stab, name):
    nh, r, _ = d.shape
    tm = _row_tile(r)

    def body(d_ref, c_ref, s_ref, o_ref):
        d_ = d_ref[0]
        o_ref[0] = (d_ * c_ref[...] + _swap_halves(d_ * s_ref[...])).astype(BF16)

    tab = pl.BlockSpec((tm, HEAD_W), lambda h, i: (i, 0))
    blk = pl.BlockSpec((1, tm, HEAD_W), lambda h, i: (h, i, 0))
    return pl.pallas_call(
        body, name=name, out_shape=jax.ShapeDtypeStruct((nh, r, HEAD_W), BF16), grid=(nh, r // tm),
        in_specs=[blk, tab, tab], out_specs=blk,
        compiler_params=_params(("arbitrary", "arbitrary")),
    )(d, ctab, stab)


def rope_bwd_sum(d, ctab, stab, name):
    nh, r, _ = d.shape
    tm = _row_tile(r)

    def body(d_ref, c_ref, s_ref, o_ref):
        @pl.when(pl.program_id(1) == 0)
        def _():
            o_ref[...] = jnp.zeros_like(o_ref)

        d_ = d_ref[0]
        lane = lax.broadcasted_iota(jnp.int32, d_.shape, 1)
        g = d_ * c_ref[...] + _swap_halves(d_ * s_ref[...])
        o_ref[...] += jnp.where(lane < QK_ROPE, g, 0.0)

    tab = pl.BlockSpec((tm, HEAD_W), lambda i, h: (i, 0))
    return pl.pallas_call(
        body, name=name, out_shape=jax.ShapeDtypeStruct((r, HEAD_W), F32), grid=(r // tm, nh),
        in_specs=[pl.BlockSpec((1, tm, HEAD_W), lambda i, h: (h, i, 0)), tab, tab], out_specs=tab,
        compiler_params=_params(("arbitrary", "arbitrary")),
    )(d, ctab, stab)


def _attn_tiles(r):
    t = _row_tile(r)
    return t, t


def _mask(i, j, tq, tk):
    rq = i * tq + lax.broadcasted_iota(jnp.int32, (tq, tk), 0)
    rk = j * tk + lax.broadcasted_iota(jnp.int32, (tq, tk), 1)
    return ((rk >> CHUNK_SHIFT) <= (rq >> CHUNK_SHIFT)) & (rk >= META_ROW0)


SM_SCALE = 1.0 / math.sqrt(QK_NOPE + QK_ROPE)


def attn_fwd(q, k, v, name):
    nh, r, dk = q.shape
    dv = v.shape[-1]
    tq, tk = _attn_tiles(r)
    nq, nk = r // tq, r // tk

    def last_k(i):
        return ((i + 1) * tq - 1) // tk

    def body(q_ref, k_ref, v_ref, o_ref, lse_ref, m_sc, l_sc, acc_sc):
        i, j = pl.program_id(1), pl.program_id(2)

        @pl.when(j == 0)
        def _():
            m_sc[...] = jnp.full_like(m_sc, NEG)
            l_sc[...] = jnp.zeros_like(l_sc)
            acc_sc[...] = jnp.zeros_like(acc_sc)

        @pl.when(j <= last_k(i))
        def _():
            s = lax.dot_general(q_ref[0], k_ref[0], NT, preferred_element_type=F32) * SM_SCALE
            s = jnp.where(_mask(i, j, tq, tk), s, NEG)
            m_old = m_sc[...]
            m_new = jnp.maximum(m_old, jnp.max(s, axis=-1, keepdims=True))
            alpha = jnp.exp(m_old - m_new)
            p = jnp.exp(s - m_new)
            l_sc[...] = alpha * l_sc[...] + jnp.sum(p, axis=-1, keepdims=True)
            acc_sc[...] = alpha * acc_sc[...] + jnp.dot(p.astype(BF16), v_ref[0], preferred_element_type=F32)
            m_sc[...] = m_new

        @pl.when(j == last_k(i))
        def _():
            o_ref[0] = (acc_sc[...] / l_sc[...]).astype(BF16)
            lse_ref[0] = m_sc[...] + jnp.log(l_sc[...])

    def kv_map(h, i, j):
        return (h, jnp.minimum(j, last_k(i)), 0)

    return pl.pallas_call(
        body, name=name,
        out_shape=[jax.ShapeDtypeStruct((nh, r, dv), BF16), jax.ShapeDtypeStruct((nh, r, 1), F32)],
        grid=(nh, nq, nk),
        in_specs=[pl.BlockSpec((1, tq, dk), lambda h, i, j: (h, i, 0)),
                  pl.BlockSpec((1, tk, dk), kv_map), pl.BlockSpec((1, tk, dv), kv_map)],
        out_specs=[pl.BlockSpec((1, tq, dv), lambda h, i, j: (h, i, 0)),
                   pl.BlockSpec((1, tq, 1), lambda h, i, j: (h, i, 0))],
        scratch_shapes=[pltpu.VMEM((tq, 1), F32), pltpu.VMEM((tq, 1), F32), pltpu.VMEM((tq, dv), F32)],
        compiler_params=_params(("arbitrary", "arbitrary", "arbitrary")),
    )(q, k, v)


def attn_bwd_dq(q, k, v, do, lse, delta, name):
    nh, r, dk = q.shape
    dv = v.shape[-1]
    tq, tk = _attn_tiles(r)
    nq, nk = r // tq, r // tk

    def last_k(i):
        return ((i + 1) * tq - 1) // tk

    def body(q_ref, k_ref, v_ref, do_ref, lse_ref, dl_ref, dq_ref, acc_sc):
        i, j = pl.program_id(1), pl.program_id(2)

        @pl.when(j == 0)
        def _():
            acc_sc[...] = jnp.zeros_like(acc_sc)

        @pl.when(j <= last_k(i))
        def _():
            s = lax.dot_general(q_ref[0], k_ref[0], NT, preferred_element_type=F32) * SM_SCALE
            s = jnp.where(_mask(i, j, tq, tk), s, NEG)
            p = jnp.exp(s - lse_ref[0])
            dp = lax.dot_general(do_ref[0], v_ref[0], NT, preferred_element_type=F32)
            ds = (p * (dp - dl_ref[0]) * SM_SCALE).astype(BF16)
            acc_sc[...] += jnp.dot(ds, k_ref[0], preferred_element_type=F32)

        @pl.when(j == last_k(i))
        def _():
            dq_ref[0] = acc_sc[...]

    def kv_map(h, i, j):
        return (h, jnp.minimum(j, last_k(i)), 0)

    qrow = lambda w: pl.BlockSpec((1, tq, w), lambda h, i, j: (h, i, 0))
    return pl.pallas_call(
        body, name=name, out_shape=jax.ShapeDtypeStruct((nh, r, dk), F32), grid=(nh, nq, nk),
        in_specs=[qrow(dk), pl.BlockSpec((1, tk, dk), kv_map), pl.BlockSpec((1, tk, dv), kv_map),
                  qrow(dv), qrow(1), qrow(1)],
        out_specs=qrow(dk),
        scratch_shapes=[pltpu.VMEM((tq, dk), F32)],
        compiler_params=_params(("arbitrary", "arbitrary", "arbitrary")),
    )(q, k, v, do, lse, delta)


def attn_bwd_dkv(q, k, v, do, lse, delta, dk_prev, dv_prev, name):
    nh, r, dk = q.shape
    dv = v.shape[-1]
    tq, tk = _attn_tiles(r)
    nq, nk = r // tq, r // tk
    has_prev = dk_prev is not None

    def first_q(j):
        return (j * tk) // tq

    def body(*refs):
        if has_prev:
            q_ref, k_ref, v_ref, do_ref, lse_ref, dl_ref, pk_ref, pv_ref, dk_ref, dv_ref, dk_sc, dv_sc = refs
        else:
            q_ref, k_ref, v_ref, do_ref, lse_ref, dl_ref, dk_ref, dv_ref, dk_sc, dv_sc = refs
        j, i = pl.program_id(1), pl.program_id(2)

        @pl.when(i == 0)
        def _():
            dk_sc[...] = jnp.zeros_like(dk_sc)
            dv_sc[...] = jnp.zeros_like(dv_sc)

        @pl.when(i >= first_q(j))
        def _():
            s = lax.dot_general(q_ref[0], k_ref[0], NT, preferred_element_type=F32) * SM_SCALE
            s = jnp.where(_mask(i, j, tq, tk), s, NEG)
            p = jnp.exp(s - lse_ref[0])
            dv_sc[...] += lax.dot_general(p.astype(BF16), do_ref[0], TN, preferred_element_type=F32)
            dp = lax.dot_general(do_ref[0], v_ref[0], NT, preferred_element_type=F32)
            ds = (p * (dp - dl_ref[0]) * SM_SCALE).astype(BF16)
            dk_sc[...] += lax.dot_general(ds, q_ref[0], TN, preferred_element_type=F32)

        @pl.when(i == nq - 1)
        def _():
            if has_prev:
                dk_ref[0] = dk_sc[...] + pk_ref[0]
                dv_ref[0] = dv_sc[...] + pv_ref[0]
            else:
                dk_ref[0] = dk_sc[...]
                dv_ref[0] = dv_sc[...]

    def q_map(h, j, i):
        return (h, jnp.maximum(i, first_q(j)), 0)

    krow = lambda w: pl.BlockSpec((1, tk, w), lambda h, j, i: (h, j, 0))
    qrow = lambda w: pl.BlockSpec((1, tq, w), q_map)
    in_specs = [qrow(dk), krow(dk), krow(dv), qrow(dv), qrow(1), qrow(1)]
    ins = [q, k, v, do, lse, delta]
    if has_prev:
        in_specs += [krow(dk), krow(dv)]
        ins += [dk_prev, dv_prev]
    return pl.pallas_call(
        body, name=name,
        out_shape=[jax.ShapeDtypeStruct((nh, r, dk), F32), jax.ShapeDtypeStruct((nh, r, dv), F32)],
        grid=(nh, nk, nq), in_specs=in_specs, out_specs=[krow(dk), krow(dv)],
        scratch_shapes=[pltpu.VMEM((tk, dk), F32), pltpu.VMEM((tk, dv), F32)],
        compiler_params=_params(("arbitrary", "arbitrary", "arbitrary")),
    )(*ins)


def attn_out_bwd(dattn, wo, o, name):
    r, d = dattn.shape
    nh, dv, _ = wo.shape
    tm = _row_tile(r)

    def body(da_ref, w_ref, o_ref, do_ref, dl_ref):
        do_ = lax.dot_general(da_ref[...].astype(BF16), w_ref[0], NT, preferred_element_type=F32).astype(BF16)
        do_ref[0] = do_
        dl_ref[0] = jnp.sum(do_.astype(F32) * o_ref[0].astype(F32), axis=-1, keepdims=True)

    return pl.pallas_call(
        body, name=name,
        out_shape=[jax.ShapeDtypeStruct((nh, r, dv), BF16), jax.ShapeDtypeStruct((nh, r, 1), F32)],
        grid=(nh, r // tm),
        in_specs=[pl.BlockSpec((tm, d), lambda h, i: (i, 0)), pl.BlockSpec((1, dv, d), lambda h, i: (h, 0, 0)),
                  pl.BlockSpec((1, tm, dv), lambda h, i: (h, i, 0))],
        out_specs=[pl.BlockSpec((1, tm, dv), lambda h, i: (h, i, 0)), pl.BlockSpec((1, tm, 1), lambda h, i: (h, i, 0))],
        compiler_params=_params(("arbitrary", "arbitrary")),
    )(dattn, wo, o)


def _pool_counts(row0, n, window):
    rows = row0 + lax.broadcasted_iota(jnp.int32, (n, 1), 0)
    cnt = jnp.clip(rows - META_ROW0 + 1, 1, window)
    return 1.0 / cnt.astype(F32)


def pool_fwd(h, gamma, wp, scale, name):
    r, d = h.shape
    ng, cg, _ = wp.shape
    tm = _row_tile(r)
    hb = tm // HALO

    def body(h_ref, hp_ref, g_ref, w_ref, sc_ref, o_ref):
        i = pl.program_id(0)
        xm = h_ref[...]
        xp = hp_ref[...] * jnp.where(i > 0, 1.0, 0.0)
        xx = jnp.concatenate([xp, xm], axis=0)
        xh, _ = _rms(xx)
        u = xh * g_ref[...]
        for g, win in enumerate(POOL_WINDOWS):
            sl = slice(g * cg, (g + 1) * cg)
            ug = u[:, sl]
            acc, k = ug, 1
            while k < win:
                acc = acc + pltpu.roll(acc, k, 0)
                k *= 2
            pooled = acc[HALO:] * _pool_counts(i * tm, tm, win) - ug[HALO:]
            y = jnp.dot(pooled.astype(BF16), w_ref[g], preferred_element_type=F32)
            o_ref[:, sl] = xm[:, sl] + y * sc_ref[:, sl]

    return pl.pallas_call(
        body, name=name, out_shape=jax.ShapeDtypeStruct((r, d), F32), grid=(r // tm,),
        in_specs=[pl.BlockSpec((tm, d), lambda i: (i, 0)),
                  pl.BlockSpec((HALO, d), lambda i: (jnp.maximum(i * hb - 1, 0), 0)),
                  pl.BlockSpec((1, d), lambda i: (0, 0)), pl.BlockSpec((ng, cg, cg), lambda i: (0, 0, 0)),
                  pl.BlockSpec((1, d), lambda i: (0, 0))],
        out_specs=pl.BlockSpec((tm, d), lambda i: (i, 0)),
        compiler_params=_params(("arbitrary",), VMEM_BIG),
    )(h, h, gamma, wp, scale)


def pool_bwd(h, gamma, wp, scale, dh, name):
    r, d = h.shape
    ng, cg, _ = wp.shape
    tm = _row_tile(r)
    hb = tm // HALO
    nt = r // tm

    def body(h_ref, hp_ref, dh_ref, dn_ref, g_ref, w_ref, sc_ref, dx_ref, dgam_ref, dw_ref, dsc_ref, du_sc):
        i = pl.program_id(0)

        @pl.when(i == 0)
        def _():
            dgam_ref[...] = jnp.zeros_like(dgam_ref)
            dw_ref[...] = jnp.zeros_like(dw_ref)
            dsc_ref[...] = jnp.zeros_like(dsc_ref)

        xm = h_ref[...]
        xp = hp_ref[...] * jnp.where(i > 0, 1.0, 0.0)
        xh_all, rstd_all = _rms(jnp.concatenate([xp, xm], axis=0))
        u = xh_all * g_ref[...]
        dm = dh_ref[...]
        dn = dn_ref[...] * jnp.where(i < nt - 1, 1.0, 0.0)
        dd = jnp.concatenate([dm, dn], axis=0)
        for g, win in enumerate(POOL_WINDOWS):
            sl = slice(g * cg, (g + 1) * cg)
            ug = u[:, sl]
            acc, k = ug, 1
            while k < win:
                acc = acc + pltpu.roll(acc, k, 0)
                k *= 2
            pooled = (acc[HALO:] * _pool_counts(i * tm, tm, win) - ug[HALO:]).astype(BF16)
            y = jnp.dot(pooled, w_ref[g], preferred_element_type=F32)
            dsc_ref[:, sl] += jnp.sum(dm[:, sl] * y, axis=0, keepdims=True)
            dyp = (dd[:, sl] * sc_ref[:, sl]).astype(BF16)
            dw_ref[g] += lax.dot_general(pooled, dyp[:tm], TN, preferred_element_type=F32)
            dpo = lax.dot_general(dyp, w_ref[g], NT, preferred_element_type=F32)
            z = dpo * _pool_counts(i * tm, tm + HALO, win)
            fwd, k = z, 1
            while k < win:
                fwd = fwd + pltpu.roll(fwd, tm + HALO - k, 0)
                k *= 2
            du_sc[:, sl] = fwd[:tm] - dpo[:tm]
        du = du_sc[...]
        xh, rstd = xh_all[HALO:], rstd_all[HALO:]
        dgam_ref[...] += jnp.sum(du * xh, axis=0, keepdims=True)
        dx_ref[...] = dm + _rms_bwd(xh, rstd, du * g_ref[...])

    row = pl.BlockSpec((tm, d), lambda i: (i, 0))
    vec = pl.BlockSpec((1, d), lambda i: (0, 0))
    prev = pl.BlockSpec((HALO, d), lambda i: (jnp.maximum(i * hb - 1, 0), 0))
    nxt = pl.BlockSpec((HALO, d), lambda i: (jnp.minimum((i + 1) * hb, r // HALO - 1), 0))
    wsp = pl.BlockSpec((ng, cg, cg), lambda i: (0, 0, 0))
    return pl.pallas_call(
        body, name=name,
        out_shape=[jax.ShapeDtypeStruct((r, d), F32), jax.ShapeDtypeStruct((1, d), F32),
                   jax.ShapeDtypeStruct((ng, cg, cg), F32), jax.ShapeDtypeStruct((1, d), F32)],
        grid=(nt,), in_specs=[row, prev, row, nxt, vec, wsp, vec], out_specs=[row, vec, wsp, vec],
        scratch_shapes=[pltpu.VMEM((tm, d), F32)],
        compiler_params=_params(("arbitrary",), VMEM_BIG),
    )(h, h, dh, dh, gamma, wp, scale)


def loss_head(h, gamma, target, seq, name):
    r, d = h.shape
    tm = _row_tile(r)

    def body(h_ref, g_ref, t_ref, sse_ref, dh_ref, dgam_ref):
        i = pl.program_id(0)

        @pl.when(i == 0)
        def _():
            sse_ref[...] = jnp.zeros_like(sse_ref)
            dgam_ref[...] = jnp.zeros_like(dgam_ref)

        xh, rstd = _rms(h_ref[...])
        rows = i * tm + lax.broadcasted_iota(jnp.int32, (tm, 1), 0)
        valid = ((rows >= FRONT) & (rows < FRONT + seq)).astype(F32)
        e = (xh * g_ref[...] - t_ref[...]) * valid
        sse_ref[...] += jnp.sum(jnp.sum(e * e, axis=1, keepdims=True), axis=0, keepdims=True)
        dy = e * (1.0 / d)
        dgam_ref[...] += jnp.sum(dy * xh, axis=0, keepdims=True)
        dh_ref[...] = _rms_bwd(xh, rstd, dy * g_ref[...])

    row = pl.BlockSpec((tm, d), lambda i: (i, 0))
    vec = pl.BlockSpec((1, d), lambda i: (0, 0))
    return pl.pallas_call(
        body, name=name,
        out_shape=[jax.ShapeDtypeStruct((1, 1), F32), jax.ShapeDtypeStruct((r, d), F32),
                   jax.ShapeDtypeStruct((1, d), F32)],
        grid=(r // tm,), in_specs=[row, vec, row],
        out_specs=[pl.BlockSpec((1, 1), lambda i: (0, 0)), row, vec],
        compiler_params=_params(("arbitrary",)),
    )(h, gamma, target)


SMALL = ("meta_tokens", "pool_w", "pool_scale", "w_dkv", "w_uk", "w_uv", "w_dq", "w_uq", "w_o")


def _pack(parts):
    flat = jnp.concatenate([p.reshape(-1) for p in parts])
    n = flat.shape[0]
    unit = PACK_W * 2 * SUBLANES
    n_pad = -(-n // unit) * unit
    return jnp.pad(flat, (0, n_pad - n)).reshape(n_pad // PACK_W, PACK_W)


def _unpack(buf, shapes, lead=()):
    flat = buf.reshape(lead + (-1,))
    out, off = [], 0
    for shp in shapes:
        n = math.prod(shp)
        out.append(flat[..., off:off + n].reshape(lead + tuple(shp)))
        off += n
    return out


def _cols_from_shards(a, axis):
    a = jnp.moveaxis(a, 0, axis)
    shp = a.shape
    return a.reshape(shp[:axis] + (shp[axis] * shp[axis + 1],) + shp[axis + 2:])


def _cols_to_shards(a, axis):
    shp = a.shape
    a = a.reshape(shp[:axis] + (N_SHARD, shp[axis] // N_SHARD) + shp[axis + 1:])
    return jnp.moveaxis(a, axis, 0)


SMALL_AXIS = {"meta_tokens": 1, "pool_w": 2, "pool_scale": 1, "w_dkv": 0, "w_uk": 1, "w_uv": 1,
              "w_dq": 1, "w_uq": 2, "w_o": 2}


def kernel(x, meta_tokens, ffn1_norm, ffn1_w_gate, ffn1_w_up, ffn1_w_down, mix_norm, ffn2_norm, ffn2_w_gate, ffn2_w_up, ffn2_w_down, pool_w, pool_scale, kv_in_norm, w_dkv, kv_latent_norm, w_uk, w_uv, w_dq, q_latent_norm, w_uq, w_o, final_norm, loss_target, m_meta_tokens, m_ffn1_norm, m_ffn1_w_gate, m_ffn1_w_up, m_ffn1_w_down, m_mix_norm, m_ffn2_norm, m_ffn2_w_gate, m_ffn2_w_up, m_ffn2_w_down, m_pool_w, m_pool_scale, m_kv_in_norm, m_w_dkv, m_kv_latent_norm, m_w_uk, m_w_uv, m_w_dq, m_q_latent_norm, m_w_uq, m_w_o, m_final_norm, v_meta_tokens, v_ffn1_norm, v_ffn1_w_gate, v_ffn1_w_up, v_ffn1_w_down, v_mix_norm, v_ffn2_norm, v_ffn2_w_gate, v_ffn2_w_up, v_ffn2_w_down, v_pool_w, v_pool_scale, v_kv_in_norm, v_w_dkv, v_kv_latent_norm, v_w_uk, v_w_uv, v_w_dq, v_q_latent_norm, v_w_uq, v_w_o, v_final_norm):
    args = dict(locals())
    W = {n: args[n] for n in NAMES}
    M = {n: args["m_" + n] for n in NAMES}
    V = {n: args["v_" + n] for n in NAMES}

    depth = ffn1_norm.shape[0]
    n_a = pool_w.shape[0]
    seq, d = x.shape[1], x.shape[2]
    nh = N_HEADS
    r = -(-(FRONT + seq) // LANES) * LANES

    cx, cy, cc = lax.axis_index("x"), lax.axis_index("y"), lax.axis_index("c")
    c_arr = jnp.reshape(cc, (1,)).astype(jnp.int32)
    s_arr = jnp.reshape(2 * cx + cy, (1,)).astype(jnp.int32)

    small_shapes = [W[n].shape for n in SMALL]
    gathered = all_gather_shards([_pack([W[n] for n in SMALL])], "ag_small")[0]
    small_full = {}
    for n, part in zip(SMALL, _unpack(gathered, small_shapes, (N_SHARD,))):
        small_full[n] = _cols_from_shards(part, SMALL_AXIS[n])

    ffn_w = {}
    for l in range(depth):
        for f, (g_, u_, d_) in (("ffn1", (ffn1_w_gate, ffn1_w_up, ffn1_w_down)),
                                ("ffn2", (ffn2_w_gate, ffn2_w_up, ffn2_w_down))):
            ffn_w[f, l] = all_gather_shards(
                [g_[l].astype(BF16), u_[l].astype(BF16), d_[l].astype(BF16)], f"ag_{f}_{l}")

    meta_full = small_full["meta_tokens"]
    wp = small_full["pool_w"].astype(BF16)
    pscale = small_full["pool_scale"]
    wdkv = jnp.pad(small_full["w_dkv"], ((0, 0), (0, HEAD_W - QK_ROPE))).astype(BF16)[None]
    wuk = small_full["w_uk"].reshape(KV_RANK, nh, QK_NOPE).transpose(1, 0, 2)
    wk_h = jnp.concatenate([jnp.zeros((nh, KV_RANK, HEAD_W - QK_NOPE), F32), wuk], axis=-1).astype(BF16)
    wv_h = small_full["w_uv"].reshape(KV_RANK, nh, V_HEAD).transpose(1, 0, 2).astype(BF16)
    wdq = small_full["w_dq"].astype(BF16)
    wuq = small_full["w_uq"].reshape(-1, Q_RANK, nh, QK_NOPE + QK_ROPE).transpose(0, 2, 1, 3)
    wq_h = jnp.concatenate([wuq[..., QK_NOPE:], jnp.zeros(wuq.shape[:-1] + (HEAD_W - QK_NOPE - QK_ROPE,), F32),
                            wuq[..., :QK_NOPE]], axis=-1).astype(BF16)
    wo_h = small_full["w_o"].reshape(-1, nh, V_HEAD, d).astype(BF16)

    ctab, stab = rope_tables(r)

    def vec(a):
        return a.reshape(1, -1)

    h = jnp.concatenate([jnp.zeros((META_ROW0, d), F32), meta_full, x[0],
                         jnp.zeros((r - FRONT - seq, d), F32)], axis=0)
    target = jnp.concatenate([jnp.zeros((FRONT, d), F32), loss_target[0],
                              jnp.zeros((r - FRONT - seq, d), F32)], axis=0)
    saved = []
    kv = None
    for l in range(depth):
        sv = {"h0": h}
        h, sv["g1"], sv["u1"] = ffn_fwd(h, vec(ffn1_norm[l]), *ffn_w["ffn1", l], f"ffn1_fwd_{l}")
        sv["h1"] = h
        if l < n_a:
            h = pool_fwd(h, vec(mix_norm[l]), wp[l], vec(pscale[l]), f"pool_fwd_{l}")
        else:
            j = l - n_a
            u = norm_fwd(h, vec(mix_norm[l]), f"mixnorm_{l}")
            cq0 = rowmm(u[None], wdq[j][None], f"dq_{l}")
            cq = norm_fwd(cq0, vec(q_latent_norm[j]), f"qnorm_{l}")
            q = proj_rope(cq, wq_h[j], ctab, stab, None, f"qproj_{l}")
            o, lse = attn_fwd(q, kv["k"], kv["v"], f"attn_fwd_{l}")
            h = rowmm(o, wo_h[j], f"oproj_{l}", res=h)
            sv.update(u=u, cq0=cq0, cq=cq, q=q, o=o, lse=lse)
        sv["h2"] = h
        h, sv["g2"], sv["u2"] = ffn_fwd(h, vec(ffn2_norm[l]), *ffn_w["ffn2", l], f"ffn2_fwd_{l}")
        saved.append(sv)
        if l == n_a - 1:
            hkv = norm_fwd(h, vec(kv_in_norm), "kvin_norm")
            ckr = rowmm(hkv[None], wdkv, "dkv")
            ckv = norm_fwd(ckr, vec(kv_latent_norm), "kvlat_norm")
            kv = {"h": h, "hkv": hkv, "ckr": ckr, "ckv": ckv,
                  "k": proj_rope(ckv, wk_h, ctab, stab, ckr, "kproj"),
                  "v": rowmm(ckv[None], wv_h, "vproj", out_dtype=BF16, heads_out=True)}

    sse, dh, dfinal = loss_head(h, vec(final_norm), target, seq, "loss_head")
    loss = lax.psum(0.5 / d * sse[0, 0], ("x", "y", "c"))

    G = {}
    GF = {}
    dnorm = {n: [None] * depth for n in ("ffn1_norm", "mix_norm", "ffn2_norm")}
    dqnorm = [None] * (depth - n_a)
    dpool_w, dpool_scale = [None] * n_a, [None] * n_a
    dwdq, dwq_h, dwo_h = [None] * (depth - n_a), [None] * (depth - n_a), [None] * (depth - n_a)

    def ffn_backward(f, l, h_in, dh_, gg, uu):
        gam = ffn1_norm[l] if f == "ffn1" else ffn2_norm[l]
        wg_, wu_, wd_ = ffn_w[f, l]
        dh_in, dgam, hn, dy, dg, du, a = ffn_bwd_act(h_in, vec(gam), dh_, gg, uu, wg_, wu_, wd_, f"{f}_bwd_act_{l}")
        dwg, dwu, dwd = ffn_bwd_weights(hn, dy, a, dg, du, f"{f}_bwd_w_{l}")
        GF[f + "_w_gate", l], GF[f + "_w_up", l], GF[f + "_w_down", l] = dwg, dwu, dwd
        dnorm[f + "_norm"][l] = dgam[0]
        return dh_in

    dk_tot = dv_tot = None
    for l in reversed(range(depth)):
        sv = saved[l]
        if l == n_a - 1:
            dkr = rope_bwd_sum(dk_tot, ctab, stab, "kproj_bwd_rope")
            dckv = rowmm(dk_tot, wk_h, "kproj_bwd", nt=True)
            dckv = rowmm(dv_tot, wv_h, "vproj_bwd", nt=True, res=dckv)
            dwk_h = tnmm(kv["ckv"][None], dk_tot, "kproj_bwd_w")
            dwv_h = tnmm(kv["ckv"][None], dv_tot, "vproj_bwd_w")
            dlat, dkvlat = norm_bwd(kv["ckr"], vec(kv_latent_norm), dckv, None, "kvlat_norm_bwd")
            dckr = jnp.concatenate([dlat, dkr], axis=1).astype(BF16)
            dhkv = rowmm(dckr[None], wdkv, "dkv_bwd", nt=True)
            dwdkv = tnmm(kv["hkv"][None], dckr[None], "dkv_bwd_w")[0]
            dh, dkvin = norm_bwd(kv["h"], vec(kv_in_norm), dhkv, dh, "kvin_norm_bwd")
            G["w_dkv"] = dwdkv[:, :KV_RANK + QK_ROPE]
            G["w_uk"] = dwk_h[..., HEAD_W - QK_NOPE:].transpose(1, 0, 2).reshape(KV_RANK, nh * QK_NOPE)
            G["w_uv"] = dwv_h.transpose(1, 0, 2).reshape(KV_RANK, nh * V_HEAD)
        dh = ffn_backward("ffn2", l, sv["h2"], dh, sv["g2"], sv["u2"])
        if l < n_a:
            dh, dmix, dpool_w[l], dps = pool_bwd(sv["h1"], vec(mix_norm[l]), wp[l], vec(pscale[l]), dh, f"pool_bwd_{l}")
            dnorm["mix_norm"][l] = dmix[0]
            dpool_scale[l] = dps[0]
        else:
            j = l - n_a
            do, delta = attn_out_bwd(dh, wo_h[j], sv["o"], f"oproj_bwd_{l}")
            dwo_h[j] = tnmm(sv["o"], dh[None], f"oproj_bwd_w_{l}")
            dq = attn_bwd_dq(sv["q"], kv["k"], kv["v"], do, sv["lse"], delta, f"attn_bwd_dq_{l}")
            dk_tot, dv_tot = attn_bwd_dkv(sv["q"], kv["k"], kv["v"], do, sv["lse"], delta, dk_tot, dv_tot,
                                          f"attn_bwd_dkv_{l}")
            dxq = rope_bwd_heads(dq, ctab, stab, f"qproj_bwd_rope_{l}")
            dcq = rowmm(dxq, wq_h[j], f"qproj_bwd_{l}", nt=True)
            dwq_h[j] = tnmm(sv["cq"][None], dxq, f"qproj_bwd_w_{l}")
            dcq0, dqn = norm_bwd(sv["cq0"], vec(q_latent_norm[j]), dcq, None, f"qnorm_bwd_{l}")
            dqnorm[j] = dqn[0]
            dcq0b = dcq0.astype(BF16)
            du = rowmm(dcq0b[None], wdq[j][None], f"dq_bwd_{l}", nt=True)
            dwdq[j] = tnmm(sv["u"][None], dcq0b[None], f"dq_bwd_w_{l}")[0]
            dh, dmix = norm_bwd(sv["h1"], vec(mix_norm[l]), du, dh, f"mixnorm_bwd_{l}")
            dnorm["mix_norm"][l] = dmix[0]
        dh = ffn_backward("ffn1", l, sv["h0"], dh, sv["g1"], sv["u1"])

    grad_x = dh[FRONT:FRONT + seq][None]
    G["meta_tokens"] = dh[META_ROW0:FRONT]
    G["pool_w"] = jnp.stack(dpool_w)
    G["pool_scale"] = jnp.stack(dpool_scale)
    G["w_dq"] = jnp.stack(dwdq)
    dwq = jnp.stack(dwq_h)
    dwq = jnp.concatenate([dwq[..., HEAD_W - QK_NOPE:], dwq[..., :QK_ROPE]], axis=-1)
    G["w_uq"] = dwq.transpose(0, 2, 1, 3).reshape(-1, Q_RANK, nh * (QK_NOPE + QK_ROPE))
    G["w_o"] = jnp.stack(dwo_h).reshape(-1, nh * V_HEAD, d)

    REPL = ("ffn1_norm", "mix_norm", "ffn2_norm", "kv_in_norm", "kv_latent_norm", "q_latent_norm", "final_norm")
    grep = {"ffn1_norm": jnp.stack(dnorm["ffn1_norm"]), "mix_norm": jnp.stack(dnorm["mix_norm"]),
            "ffn2_norm": jnp.stack(dnorm["ffn2_norm"]), "kv_in_norm": dkvin[0], "kv_latent_norm": dkvlat[0],
            "q_latent_norm": jnp.stack(dqnorm), "final_norm": dfinal[0]}

    def pack128(parts):
        flat = jnp.concatenate([p.reshape(-1) for p in parts])
        n = flat.shape[0]
        n_pad = -(-n // (LANES * SUBLANES)) * (LANES * SUBLANES)
        return jnp.pad(flat, (0, n_pad - n)).reshape(-1, LANES)

    rep_shapes = [W[n].shape for n in REPL]
    g_rep = all_reduce_small(pack128([grep[n] for n in REPL]), "ar_repl")
    d_rep, m_rep, v_rep = adamw(pack128([W[n] for n in REPL]), g_rep, pack128([M[n] for n in REPL]),
                                pack128([V[n] for n in REPL]), "adamw_repl")
    out_g, out_d, out_m, out_v = {}, {}, {}, {}
    for dst, buf in ((out_g, g_rep), (out_d, d_rep), (out_m, m_rep), (out_v, v_rep)):
        for n, a in zip(REPL, _unpack(buf, rep_shapes)):
            dst[n] = a

    g_small = jnp.stack([_pack([_cols_to_shards(G[n], SMALL_AXIS[n])[s] for n in SMALL]) for s in range(N_SHARD)])
    g_small = reduce_scatter([g_small], c_arr, s_arr, "small")[0]
    d_s, m_s, v_s = adamw(_pack([W[n] for n in SMALL]), g_small, _pack([M[n] for n in SMALL]),
                          _pack([V[n] for n in SMALL]), "adamw_small")
    for dst, buf in ((out_g, g_small), (out_d, d_s), (out_m, m_s), (out_v, v_s)):
        for n, a in zip(SMALL, _unpack(buf, small_shapes)):
            dst[n] = a

    FFN = ("ffn1_w_gate", "ffn1_w_up", "ffn1_w_down", "ffn2_w_gate", "ffn2_w_up", "ffn2_w_down")
    per = {n: [] for n in FFN}
    for l in range(depth):
        for f in ("ffn1", "ffn2"):
            names = [f + "_w_gate", f + "_w_up", f + "_w_down"]
            gs = reduce_scatter([GF[n, l] for n in names], c_arr, s_arr, f"{f}_{l}")
            for n, g_ in zip(names, gs):
                per[n].append((g_,) + tuple(adamw(W[n][l], g_, M[n][l], V[n][l], f"adamw_{n}_{l}")))
    for n in FFN:
        out_g[n], out_d[n], out_m[n], out_v[n] = (jnp.stack([t[k] for t in per[n]]) for k in range(4))

    return (loss, grad_x, *[out_g[n] for n in NAMES], *[out_d[n] for n in NAMES],
            *[out_m[n] for n in NAMES], *[out_v[n] for n in NAMES])


NAMES = ("meta_tokens", "ffn1_norm", "ffn1_w_gate", "ffn1_w_up", "ffn1_w_down", "mix_norm", "ffn2_norm",
         "ffn2_w_gate", "ffn2_w_up", "ffn2_w_down", "pool_w", "pool_scale", "kv_in_norm", "w_dkv",
         "kv_latent_norm", "w_uk", "w_uv", "w_dq", "q_latent_norm", "w_uq", "w_o", "final_norm")
```

```python
import functools
import math

import jax
import jax.numpy as jnp
from jax import lax
from jax.experimental import pallas as pl
from jax.experimental.pallas import tpu as pltpu

F32 = jnp.float32
BF16 = jnp.bfloat16
MESH = pl.DeviceIdType.MESH
ANY = pl.BlockSpec(memory_space=pl.ANY)

EPS = 1e-6
CHUNK = 64
CHUNK_SHIFT = 6
N_META = 16
FRONT = 64
META_ROW0 = FRONT - N_META
POOL_WINDOWS = (2, 4, 8, 16)
HALO = 16
N_HEADS = 8
QK_NOPE = 64
QK_ROPE = 32
V_HEAD = 64
HEAD_W = 128
KV_RANK = 256
Q_RANK = 384
ROPE_THETA = 10000.0
NEG = -1e30
N_SHARD = 4
LANES = 128
SUBLANES = 8
PACK_W = 512
VMEM_BIG = 52 * 1024 * 1024

ADAM_LR = 0.001
ADAM_B1 = 0.9
ADAM_B2 = 0.999
ADAM_EPS = 1e-08
ADAM_WD = 0.01
ADAM_STEP = 10

NT = (((1,), (1,)), ((), ()))
TN = (((0,), (0,)), ((), ()))


def _params(sem=None, vmem=None):
    return pltpu.CompilerParams(dimension_semantics=sem, vmem_limit_bytes=vmem)


def _tile(n, pref, mult=SUBLANES):
    best = None
    for t in range(mult, min(n, pref) + 1, mult):
        if n % t == 0:
            best = t
    return best if best is not None else n


def _row_tile(r):
    return 640 if r % 640 == 0 else 128


def _rms(x):
    rstd = lax.rsqrt(jnp.mean(x * x, axis=-1, keepdims=True) + EPS)
    return x * rstd, rstd


def _rms_bwd(xh, rstd, dxh):
    return rstd * (dxh - xh * jnp.mean(dxh * xh, axis=-1, keepdims=True))


def _sigmoid(x):
    return 1.0 / (1.0 + jnp.exp(-x))


def _place():
    x, y, c = lax.axis_index("x"), lax.axis_index("y"), lax.axis_index("c")
    chips = [(1 - x, y), (x, 1 - y), (1 - x, 1 - y)]
    return x, y, c, chips


def all_gather_shards(shards, name):
    n = len(shards)

    def body(*refs):
        ins, outs = refs[:n], refs[n:2 * n]
        send1, recv1, send2, recv2, lsem = refs[2 * n:]
        x, y, c, chips = _place()
        s = 2 * x + y
        sib = (x, y, 1 - c)

        def rcopy(k, j, src, dst, to, first):
            return pltpu.make_async_remote_copy(
                src_ref=src, dst_ref=dst,
                send_sem=(send1 if first else send2).at[k, j],
                recv_sem=(recv1 if first else recv2).at[k, j],
                device_id=to, device_id_type=MESH)

        started, local = [], []
        for k in range(n):
            hf = ins[k].shape[0] // 2
            cp = pltpu.make_async_copy(ins[k], outs[k].at[s], lsem.at[k])
            cp.start()
            local.append(cp)
            for j, (cx, cy) in enumerate(chips):
                r = rcopy(k, j, ins[k].at[pl.ds(c * hf, hf)], outs[k].at[s, pl.ds(c * hf, hf)],
                          (cx, cy, c), True)
                r.start()
                started.append(r)
        for k in range(n):
            hf = ins[k].shape[0] // 2
            for j, (cx, cy) in enumerate(chips):
                blk = outs[k].at[2 * cx + cy, pl.ds(c * hf, hf)]
                rcopy(k, j, blk, blk, (cx, cy, c), True).wait_recv()
                f = rcopy(k, j, blk, blk, sib, False)
                f.start()
                started.append(f)
        for k in range(n):
            hf = ins[k].shape[0] // 2
            for j, (cx, cy) in enumerate(chips):
                blk = outs[k].at[2 * cx + cy, pl.ds((1 - c) * hf, hf)]
                rcopy(k, j, blk, blk, sib, False).wait_recv()
        for r in started:
            r.wait_send()
        for cp in local:
            cp.wait()

    return pl.pallas_call(
        body, name=name,
        out_shape=[jax.ShapeDtypeStruct((N_SHARD,) + a.shape, a.dtype) for a in shards],
        in_specs=[ANY] * n, out_specs=[ANY] * n,
        scratch_shapes=[pltpu.SemaphoreType.DMA((n, 3))] * 4 + [pltpu.SemaphoreType.DMA((n,))],
    )(*shards)


def sibling_swap_halves(gs, name):
    n = len(gs)

    def body(*refs):
        ins, outs = refs[:n], refs[n:2 * n]
        send, recv = refs[2 * n:]
        x, y, c, _ = _place()
        sib = (x, y, 1 - c)
        cps = []
        for k in range(n):
            hf = ins[k].shape[1] // 2
            r = pltpu.make_async_remote_copy(
                src_ref=ins[k].at[:, pl.ds((1 - c) * hf, hf)], dst_ref=outs[k],
                send_sem=send.at[k], recv_sem=recv.at[k], device_id=sib, device_id_type=MESH)
            r.start()
            cps.append(r)
        for r in cps:
            r.wait_recv()
        for r in cps:
            r.wait_send()

    return pl.pallas_call(
        body, name=name,
        out_shape=[jax.ShapeDtypeStruct((a.shape[0], a.shape[1] // 2, a.shape[2]), a.dtype) for a in gs],
        in_specs=[ANY] * n, out_specs=[ANY] * n,
        scratch_shapes=[pltpu.SemaphoreType.DMA((n,))] * 2,
    )(*gs)


def chip_exchange(ps, name):
    n = len(ps)

    def body(*refs):
        ins, outs = refs[:n], refs[n:2 * n]
        send, recv = refs[2 * n:]
        x, y, c, chips = _place()
        cps = []
        for k in range(n):
            for j, (cx, cy) in enumerate(chips):
                r = pltpu.make_async_remote_copy(
                    src_ref=ins[k].at[2 * cx + cy], dst_ref=outs[k].at[j],
                    send_sem=send.at[k, j], recv_sem=recv.at[k, j],
                    device_id=(cx, cy, c), device_id_type=MESH)
                r.start()
                cps.append(r)
        for r in cps:
            r.wait_recv()
        for r in cps:
            r.wait_send()

    return pl.pallas_call(
        body, name=name,
        out_shape=[jax.ShapeDtypeStruct((3,) + a.shape[1:], a.dtype) for a in ps],
        in_specs=[ANY] * n, out_specs=[ANY] * n,
        scratch_shapes=[pltpu.SemaphoreType.DMA((n, 3))] * 2,
    )(*ps)


def sibling_join_halves(ts, name):
    n = len(ts)

    def body(*refs):
        ins, outs = refs[:n], refs[n:2 * n]
        send, recv, lsem = refs[2 * n:]
        x, y, c, _ = _place()
        sib = (x, y, 1 - c)
        cps, local = [], []
        for k in range(n):
            hf = ins[k].shape[0]
            cp = pltpu.make_async_copy(ins[k], outs[k].at[pl.ds(c * hf, hf)], lsem.at[k])
            cp.start()
            local.append(cp)
            r = pltpu.make_async_remote_copy(
                src_ref=ins[k], dst_ref=outs[k].at[pl.ds(c * hf, hf)],
                send_sem=send.at[k], recv_sem=recv.at[k], device_id=sib, device_id_type=MESH)
            r.start()
            cps.append(r)
        for k in range(n):
            hf = ins[k].shape[0]
            pltpu.make_async_remote_copy(
                src_ref=ins[k], dst_ref=outs[k].at[pl.ds((1 - c) * hf, hf)],
                send_sem=send.at[k], recv_sem=recv.at[k], device_id=sib, device_id_type=MESH).wait_recv()
        for r in cps:
            r.wait_send()
        for cp in local:
            cp.wait()

    return pl.pallas_call(
        body, name=name,
        out_shape=[jax.ShapeDtypeStruct((2 * a.shape[0], a.shape[1]), a.dtype) for a in ts],
        in_specs=[ANY] * n, out_specs=[ANY] * n,
        scratch_shapes=[pltpu.SemaphoreType.DMA((n,))] * 3,
    )(*ts)


def all_reduce_small(part, name):
    m, w = part.shape

    def body(x_ref, tot_ref, gat_ref, send_sems, recv_sems):
        x, y, c, chips = _place()
        me, sib = (x, y, c), (x, y, 1 - c)

        def slot(px, py, pc):
            return gat_ref.at[4 * px + 2 * py + pc]

        def copy(k, block, to, src=None):
            return pltpu.make_async_remote_copy(
                src_ref=slot(*block) if src is None else src, dst_ref=slot(*block),
                send_sem=send_sems.at[k], recv_sem=recv_sems.at[k], device_id=to, device_id_type=MESH)

        gat_ref[4 * x + 2 * y + c] = x_ref[...]
        first = [copy(0, me, sib, src=x_ref)]
        first += [copy(1 + j, me, (*chip, c), src=x_ref) for j, chip in enumerate(chips)]
        for cp in first:
            cp.start()
        passed = [copy(4 + j, (*chip, c), sib) for j, chip in enumerate(chips)]
        for j, chip in enumerate(chips):
            copy(1 + j, (*chip, c), me).wait_recv()
            passed[j].start()
        copy(0, sib, me).wait_recv()
        for j, chip in enumerate(chips):
            copy(4 + j, (*chip, 1 - c), me).wait_recv()
        for cp in first + passed:
            cp.wait_send()
        tot = gat_ref[0]
        for d in range(1, 8):
            tot = tot + gat_ref[d]
        tot_ref[...] = tot

    return pl.pallas_call(
        body, name=name,
        out_shape=jax.ShapeDtypeStruct((m, w), F32),
        in_specs=[pl.BlockSpec(memory_space=pltpu.VMEM)],
        out_specs=pl.BlockSpec(memory_space=pltpu.VMEM),
        scratch_shapes=[pltpu.VMEM((8, m, w), F32), pltpu.SemaphoreType.DMA((7,)), pltpu.SemaphoreType.DMA((7,))],
    )(part)


def add_own_half(g, rec, c_arr, name):
    _, a, b = g.shape
    hf = a // 2
    tb = _tile(hf, 256)
    nb = hf // tb

    def body(c_ref, g_ref, r_ref, o_ref):
        o_ref[...] = g_ref[...] + r_ref[...]

    return pl.pallas_call(
        body, name=name,
        out_shape=jax.ShapeDtypeStruct((N_SHARD, hf, b), F32),
        grid_spec=pltpu.PrefetchScalarGridSpec(
            num_scalar_prefetch=1, grid=(N_SHARD, nb),
            in_specs=[pl.BlockSpec((1, tb, b), lambda s, i, c: (s, c[0] * nb + i, 0)),
                      pl.BlockSpec((1, tb, b), lambda s, i, c: (s, i, 0))],
            out_specs=pl.BlockSpec((1, tb, b), lambda s, i, c: (s, i, 0))),
        compiler_params=_params(("arbitrary", "arbitrary")),
    )(c_arr, g, rec)


def sum_four(p, rec, s_arr, name):
    _, ah, b = p.shape
    tb = _tile(ah, 256)

    def body(s_ref, p_ref, r0, r1, r2, o_ref):
        o_ref[...] = ((p_ref[0] + r0[0]) + r1[0]) + r2[0]

    def rspec(j):
        return pl.BlockSpec((1, tb, b), lambda i, s: (j, i, 0))

    return pl.pallas_call(
        body, name=name,
        out_shape=jax.ShapeDtypeStruct((ah, b), F32),
        grid_spec=pltpu.PrefetchScalarGridSpec(
            num_scalar_prefetch=1, grid=(ah // tb,),
            in_specs=[pl.BlockSpec((1, tb, b), lambda i, s: (s[0], i, 0)), rspec(0), rspec(1), rspec(2)],
            out_specs=pl.BlockSpec((tb, b), lambda i, s: (i, 0))),
        compiler_params=_params(("arbitrary",)),
    )(s_arr, p, rec, rec, rec)


def adamw(w, g, m, v, name):
    a, b = w.shape
    tb = _tile(a, 256)
    c1 = 1.0 - ADAM_B1 ** ADAM_STEP
    c2 = 1.0 - ADAM_B2 ** ADAM_STEP

    def body(w_ref, g_ref, m_ref, v_ref, d_ref, mo_ref, vo_ref):
        g_ = g_ref[...]
        m_ = ADAM_B1 * m_ref[...] + (1.0 - ADAM_B1) * g_
        v_ = ADAM_B2 * v_ref[...] + (1.0 - ADAM_B2) * (g_ * g_)
        m_hat = m_ / c1
        v_hat = v_ / c2
        d_ref[...] = -ADAM_LR * (m_hat / (jnp.sqrt(v_hat) + ADAM_EPS) + ADAM_WD * w_ref[...])
        mo_ref[...] = m_
        vo_ref[...] = v_

    spec = pl.BlockSpec((tb, b), lambda i: (i, 0))
    return pl.pallas_call(
        body, name=name,
        out_shape=[jax.ShapeDtypeStruct((a, b), F32)] * 3,
        grid=(a // tb,), in_specs=[spec] * 4, out_specs=[spec] * 3,
        compiler_params=_params(("arbitrary",)),
    )(w, g, m, v)


def reduce_scatter(gs, c_arr, s_arr, tag):
    rec_a = sibling_swap_halves(gs, f"rs_swap_{tag}")
    ps = [add_own_half(g, r, c_arr, f"rs_add_{tag}_{k}") for k, (g, r) in enumerate(zip(gs, rec_a))]
    rec_b = chip_exchange(ps, f"rs_ici_{tag}")
    ts = [sum_four(p, r, s_arr, f"rs_sum_{tag}_{k}") for k, (p, r) in enumerate(zip(ps, rec_b))]
    return sibling_join_halves(ts, f"rs_join_{tag}")


def ffn_fwd(h, gamma, wg, wu, wd, name):
    r, d = h.shape
    ns, _, fs = wg.shape
    tm = _row_tile(r)

    def body(h_ref, g_ref, wg_ref, wu_ref, wd_ref, ho_ref, gg_ref, uu_ref, hn_sc, acc_sc):
        s = pl.program_id(1)

        @pl.when(s == 0)
        def _():
            xh, _ = _rms(h_ref[...])
            hn_sc[...] = (xh * g_ref[...]).astype(BF16)
            acc_sc[...] = jnp.zeros_like(acc_sc)

        hn = hn_sc[...]
        g = jnp.dot(hn, wg_ref[0], preferred_element_type=F32)
        u = jnp.dot(hn, wu_ref[0], preferred_element_type=F32)
        gg_ref[0] = g.astype(BF16)
        uu_ref[0] = u.astype(BF16)
        a = (g * _sigmoid(g) * u).astype(BF16)
        acc_sc[...] += jnp.dot(a, wd_ref[0], preferred_element_type=F32)

        @pl.when(s == ns - 1)
        def _():
            ho_ref[...] = h_ref[...] + 0.5 * acc_sc[...]

    return pl.pallas_call(
        body, name=name,
        out_shape=[jax.ShapeDtypeStruct((r, d), F32), jax.ShapeDtypeStruct((ns, r, fs), BF16),
                   jax.ShapeDtypeStruct((ns, r, fs), BF16)],
        grid=(r // tm, ns),
        in_specs=[pl.BlockSpec((tm, d), lambda i, s: (i, 0)), pl.BlockSpec((1, d), lambda i, s: (0, 0)),
                  pl.BlockSpec((1, d, fs), lambda i, s: (s, 0, 0)), pl.BlockSpec((1, d, fs), lambda i, s: (s, 0, 0)),
                  pl.BlockSpec((1, fs, d), lambda i, s: (s, 0, 0))],
        out_specs=[pl.BlockSpec((tm, d), lambda i, s: (i, 0)), pl.BlockSpec((1, tm, fs), lambda i, s: (s, i, 0)),
                   pl.BlockSpec((1, tm, fs), lambda i, s: (s, i, 0))],
        scratch_shapes=[pltpu.VMEM((tm, d), BF16), pltpu.VMEM((tm, d), F32)],
        compiler_params=_params(("arbitrary", "arbitrary"), VMEM_BIG),
    )(h, gamma, wg, wu, wd)


def ffn_bwd_act(h, gamma, dh, gg, uu, wg, wu, wd, name):
    r, d = h.shape
    ns, _, fs = wg.shape
    tm = _row_tile(r)

    def body(h_ref, g_ref, dh_ref, gg_ref, uu_ref, wg_ref, wu_ref, wd_ref,
             dho_ref, dgam_ref, hn_ref, dy_ref, dg_ref, du_ref, a_ref, acc_sc):
        i, s = pl.program_id(0), pl.program_id(1)

        @pl.when(s == 0)
        def _():
            xh, _ = _rms(h_ref[...])
            hn_ref[...] = (xh * g_ref[...]).astype(BF16)
            dy_ref[...] = (0.5 * dh_ref[...]).astype(BF16)
            acc_sc[...] = jnp.zeros_like(acc_sc)

        @pl.when((i == 0) & (s == 0))
        def _():
            dgam_ref[...] = jnp.zeros_like(dgam_ref)

        g = gg_ref[0].astype(F32)
        u = uu_ref[0].astype(F32)
        da = lax.dot_general(dy_ref[...], wd_ref[0], NT, preferred_element_type=F32)
        sig = _sigmoid(g)
        sl = g * sig
        a_ref[0] = (sl * u).astype(BF16)
        du = (da * sl).astype(BF16)
        dg = (da * u * (sig * (1.0 + g * (1.0 - sig)))).astype(BF16)
        dg_ref[0] = dg
        du_ref[0] = du
        acc_sc[...] += (lax.dot_general(dg, wg_ref[0], NT, preferred_element_type=F32)
                        + lax.dot_general(du, wu_ref[0], NT, preferred_element_type=F32))

        @pl.when(s == ns - 1)
        def _():
            xh, rstd = _rms(h_ref[...])
            dhn = acc_sc[...]
            dgam_ref[...] += jnp.sum(dhn * xh, axis=0, keepdims=True)
            dho_ref[...] = dh_ref[...] + _rms_bwd(xh, rstd, dhn * g_ref[...])

    row = pl.BlockSpec((tm, d), lambda i, s: (i, 0))
    act = pl.BlockSpec((1, tm, fs), lambda i, s: (s, i, 0))
    return pl.pallas_call(
        body, name=name,
        out_shape=[jax.ShapeDtypeStruct((r, d), F32), jax.ShapeDtypeStruct((1, d), F32),
                   jax.ShapeDtypeStruct((r, d), BF16), jax.ShapeDtypeStruct((r, d), BF16),
                   jax.ShapeDtypeStruct((ns, r, fs), BF16), jax.ShapeDtypeStruct((ns, r, fs), BF16),
                   jax.ShapeDtypeStruct((ns, r, fs), BF16)],
        grid=(r // tm, ns),
        in_specs=[row, pl.BlockSpec((1, d), lambda i, s: (0, 0)), row, act, act,
                  pl.BlockSpec((1, d, fs), lambda i, s: (s, 0, 0)), pl.BlockSpec((1, d, fs), lambda i, s: (s, 0, 0)),
                  pl.BlockSpec((1, fs, d), lambda i, s: (s, 0, 0))],
        out_specs=[row, pl.BlockSpec((1, d), lambda i, s: (0, 0)), row, row, act, act, act],
        scratch_shapes=[pltpu.VMEM((tm, d), F32)],
        compiler_params=_params(("arbitrary", "arbitrary"), VMEM_BIG),
    )(h, gamma, dh, gg, uu, wg, wu, wd)


def ffn_bwd_weights(hn, dy, a, dg, du, name):
    r, d = hn.shape
    ns, _, fs = a.shape
    tm = _row_tile(r)

    def body(hn_ref, dy_ref, a_ref, dg_ref, du_ref, wg_ref, wu_ref, wd_ref):
        @pl.when(pl.program_id(1) == 0)
        def _():
            wg_ref[...] = jnp.zeros_like(wg_ref)
            wu_ref[...] = jnp.zeros_like(wu_ref)
            wd_ref[...] = jnp.zeros_like(wd_ref)

        hn_ = hn_ref[...]
        wg_ref[0] += lax.dot_general(hn_, dg_ref[0], TN, preferred_element_type=F32)
        wu_ref[0] += lax.dot_general(hn_, du_ref[0], TN, preferred_element_type=F32)
        wd_ref[0] += lax.dot_general(a_ref[0], dy_ref[...], TN, preferred_element_type=F32)

    row = pl.BlockSpec((tm, d), lambda s, i: (i, 0))
    act = pl.BlockSpec((1, tm, fs), lambda s, i: (s, i, 0))
    return pl.pallas_call(
        body, name=name,
        out_shape=[jax.ShapeDtypeStruct((ns, d, fs), F32), jax.ShapeDtypeStruct((ns, d, fs), F32),
                   jax.ShapeDtypeStruct((ns, fs, d), F32)],
        grid=(ns, r // tm),
        in_specs=[row, row, act, act, act],
        out_specs=[pl.BlockSpec((1, d, fs), lambda s, i: (s, 0, 0)), pl.BlockSpec((1, d, fs), lambda s, i: (s, 0, 0)),
                   pl.BlockSpec((1, fs, d), lambda s, i: (s, 0, 0))],
        compiler_params=_params(("arbitrary", "arbitrary"), VMEM_BIG),
    )(hn, dy, a, dg, du)


def norm_fwd(x, gamma, name):
    r = x.shape[0]
    w = gamma.shape[1]
    tm = _row_tile(r)

    def body(x_ref, g_ref, o_ref):
        xh, _ = _rms(x_ref[...])
        o_ref[...] = (xh * g_ref[...]).astype(BF16)

    return pl.pallas_call(
        body, name=name, out_shape=jax.ShapeDtypeStruct((r, w), BF16), grid=(r // tm,),
        in_specs=[pl.BlockSpec((tm, w), lambda i: (i, 0)), pl.BlockSpec((1, w), lambda i: (0, 0))],
        out_specs=pl.BlockSpec((tm, w), lambda i: (i, 0)),
        compiler_params=_params(("arbitrary",)),
    )(x, gamma)


def norm_bwd(x, gamma, dy, dres, name):
    r = x.shape[0]
    w = gamma.shape[1]
    tm = _row_tile(r)
    has_res = dres is not None

    def body(*refs):
        if has_res:
            x_ref, g_ref, dy_ref, dr_ref, dx_ref, dgam_ref = refs
        else:
            x_ref, g_ref, dy_ref, dx_ref, dgam_ref = refs

        @pl.when(pl.program_id(0) == 0)
        def _():
            dgam_ref[...] = jnp.zeros_like(dgam_ref)

        xh, rstd = _rms(x_ref[...])
        dy_ = dy_ref[...].astype(F32)
        dgam_ref[...] += jnp.sum(dy_ * xh, axis=0, keepdims=True)
        dx = _rms_bwd(xh, rstd, dy_ * g_ref[...])
        if has_res:
            dx = dx + dr_ref[...]
        dx_ref[...] = dx

    row = pl.BlockSpec((tm, w), lambda i: (i, 0))
    vec = pl.BlockSpec((1, w), lambda i: (0, 0))
    ins = [x, gamma, dy] + ([dres] if has_res else [])
    return pl.pallas_call(
        body, name=name,
        out_shape=[jax.ShapeDtypeStruct((r, w), F32), jax.ShapeDtypeStruct((1, w), F32)],
        grid=(r // tm,), in_specs=[row, vec, row] + ([row] if has_res else []), out_specs=[row, vec],
        compiler_params=_params(("arbitrary",)),
    )(*ins)


def rowmm(a, w, name, *, nt=False, res=None, out_dtype=F32, heads_out=False):
    ha, r, ka = a.shape
    hw = w.shape[0]
    nh = max(ha, hw)
    n = w.shape[1] if nt else w.shape[2]
    tm = _row_tile(r)
    dims = NT if nt else (((1,), (0,)), ((), ()))
    has_res = res is not None

    def body(*refs):
        if has_res:
            a_ref, w_ref, r_ref, o_ref = refs
        else:
            a_ref, w_ref, o_ref = refs
        shared = a_ref[0].astype(BF16) if ha == 1 else None
        acc = None
        for h in range(nh):
            lhs = shared if ha == 1 else a_ref[h].astype(BF16)
            p = lax.dot_general(lhs, w_ref[h if hw > 1 else 0], dims, preferred_element_type=F32)
            if heads_out:
                o_ref[h] = p.astype(out_dtype)
            else:
                acc = p if acc is None else acc + p
        if not heads_out:
            if has_res:
                acc = acc + r_ref[...]
            o_ref[...] = acc.astype(out_dtype)

    in_specs = [pl.BlockSpec((ha, tm, ka), lambda i: (0, i, 0)), pl.BlockSpec(w.shape, lambda i: (0, 0, 0))]
    ins = [a, w]
    if has_res:
        in_specs.append(pl.BlockSpec((tm, n), lambda i: (i, 0)))
        ins.append(res)
    if heads_out:
        out_shape = jax.ShapeDtypeStruct((nh, r, n), out_dtype)
        out_spec = pl.BlockSpec((nh, tm, n), lambda i: (0, i, 0))
    else:
        out_shape = jax.ShapeDtypeStruct((r, n), out_dtype)
        out_spec = pl.BlockSpec((tm, n), lambda i: (i, 0))
    return pl.pallas_call(
        body, name=name, out_shape=out_shape, grid=(r // tm,), in_specs=in_specs, out_specs=out_spec,
        compiler_params=_params(("arbitrary",), VMEM_BIG),
    )(*ins)


def tnmm(a, b, name):
    ha, r, ka = a.shape
    hb, _, nb = b.shape
    nh = max(ha, hb)
    tm = _row_tile(r)

    def body(a_ref, b_ref, o_ref):
        @pl.when(pl.program_id(0) == 0)
        def _():
            o_ref[...] = jnp.zeros_like(o_ref)

        a_shared = a_ref[0].astype(BF16) if ha == 1 else None
        b_shared = b_ref[0].astype(BF16) if hb == 1 else None
        for h in range(nh):
            lhs = a_shared if ha == 1 else a_ref[h].astype(BF16)
            rhs = b_shared if hb == 1 else b_ref[h].astype(BF16)
            o_ref[h] += lax.dot_general(lhs, rhs, TN, preferred_element_type=F32)

    return pl.pallas_call(
        body, name=name, out_shape=jax.ShapeDtypeStruct((nh, ka, nb), F32), grid=(r // tm,),
        in_specs=[pl.BlockSpec((ha, tm, ka), lambda i: (0, i, 0)), pl.BlockSpec((hb, tm, nb), lambda i: (0, i, 0))],
        out_specs=pl.BlockSpec((nh, ka, nb), lambda i: (0, 0, 0)),
        compiler_params=_params(("arbitrary",), VMEM_BIG),
    )(a, b)


def rope_tables(r):
    inv = 1.0 / (ROPE_THETA ** (jnp.arange(0, QK_ROPE, 2, dtype=F32) / QK_ROPE))
    pos = (jnp.arange(r, dtype=F32) - META_ROW0)[:, None]
    ang = pos * inv[None, :]
    cos, sin = jnp.cos(ang), jnp.sin(ang)
    ones = jnp.ones((r, HEAD_W - QK_ROPE), F32)
    ctab = jnp.concatenate([cos, cos, ones], axis=1)
    stab = jnp.concatenate([-sin, sin, jnp.zeros_like(ones)], axis=1)
    return ctab, stab


def _swap_halves(z):
    lane = lax.broadcasted_iota(jnp.int32, z.shape, 1)
    up = pltpu.roll(z, HEAD_W - QK_ROPE // 2, 1)
    down = pltpu.roll(z, QK_ROPE // 2, 1)
    return jnp.where(lane < QK_ROPE // 2, up, jnp.where(lane < QK_ROPE, down, 0.0))


def proj_rope(a, w, ctab, stab, extra, name, scale=1.0):
    r, ka = a.shape
    nh = w.shape[0]
    tm = _row_tile(r)
    has_extra = extra is not None

    def body(*refs):
        if has_extra:
            a_ref, w_ref, c_ref, s_ref, e_ref, o_ref = refs
        else:
            a_ref, w_ref, c_ref, s_ref, o_ref = refs
        a_ = a_ref[...]
        ctab_, stab_ = c_ref[...], s_ref[...]
        if scale != 1.0:
            ctab_, stab_ = ctab_ * scale, stab_ * scale
        for h in range(nh):
            x = jnp.dot(a_, w_ref[h], preferred_element_type=F32)
            if has_extra:
                x = x + e_ref[...]
            o_ref[h] = (x * ctab_ + _swap_halves(x) * stab_).astype(BF16)

    tab = pl.BlockSpec((tm, HEAD_W), lambda i: (i, 0))
    in_specs = [pl.BlockSpec((tm, ka), lambda i: (i, 0)), pl.BlockSpec((nh, ka, HEAD_W), lambda i: (0, 0, 0)),
                tab, tab]
    ins = [a, w, ctab, stab]
    if has_extra:
        in_specs.append(pl.BlockSpec((tm, HEAD_W), lambda i: (i, 2)))
        ins.append(extra)
    return pl.pallas_call(
        body, name=name, out_shape=jax.ShapeDtypeStruct((nh, r, HEAD_W), BF16), grid=(r // tm,),
        in_specs=in_specs, out_specs=pl.BlockSpec((nh, tm, HEAD_W), lambda i: (0, i, 0)),
        compiler_params=_params(("arbitrary",)),
    )(*ins)


def rope_bwd_heads(d, ctab, stab, name, scale=1.0):
    nh, r, _ = d.shape
    tm = _row_tile(r)

    def body(d_ref, c_ref, s_ref, o_ref):
        ctab_, stab_ = c_ref[...], s_ref[...]
        if scale != 1.0:
            ctab_, stab_ = ctab_ * scale, stab_ * scale
        for h in range(nh):
            d_ = d_ref[h]
            o_ref[h] = (d_ * ctab_ + _swap_halves(d_ * stab_)).astype(BF16)

    tab = pl.BlockSpec((tm, HEAD_W), lambda i: (i, 0))
    blk = pl.BlockSpec((nh, tm, HEAD_W), lambda i: (0, i, 0))
    return pl.pallas_call(
        body, name=name, out_shape=jax.ShapeDtypeStruct((nh, r, HEAD_W), BF16), grid=(r // tm,),
        in_specs=[blk, tab, tab], out_specs=blk,
        compiler_params=_params(("arbitrary",)),
    )(d, ctab, stab)


def rope_bwd_sum(d, ctab, stab, name):
    nh, r, _ = d.shape
    tm = _row_tile(r)

    def body(d_ref, c_ref, s_ref, o_ref):
        d_ = d_ref[0]
        for h in range(1, nh):
            d_ = d_ + d_ref[h]
        lane = lax.broadcasted_iota(jnp.int32, d_.shape, 1)
        g = d_ * c_ref[...] + _swap_halves(d_ * s_ref[...])
        o_ref[...] = jnp.where(lane < QK_ROPE, g, 0.0)

    tab = pl.BlockSpec((tm, HEAD_W), lambda i: (i, 0))
    return pl.pallas_call(
        body, name=name, out_shape=jax.ShapeDtypeStruct((r, HEAD_W), F32), grid=(r // tm,),
        in_specs=[pl.BlockSpec((nh, tm, HEAD_W), lambda i: (0, i, 0)), tab, tab], out_specs=tab,
        compiler_params=_params(("arbitrary",)),
    )(d, ctab, stab)


def _attn_tiles(r):
    t = _row_tile(r)
    return t, t


def _mask(i, j, tq, tk, keys_on_rows=False):
    shape = (tk, tq) if keys_on_rows else (tq, tk)
    rq = i * tq + lax.broadcasted_iota(jnp.int32, shape, 1 if keys_on_rows else 0)
    rk = j * tk + lax.broadcasted_iota(jnp.int32, shape, 0 if keys_on_rows else 1)
    return ((rk >> CHUNK_SHIFT) <= (rq >> CHUNK_SHIFT)) & (rk >= META_ROW0)


SM_SCALE = 1.0 / math.sqrt(QK_NOPE + QK_ROPE)
LOG2E = math.log2(math.e)
Q_SCALE = SM_SCALE * LOG2E


def attn_fwd(q, k, v, name):
    nh, r, dk = q.shape
    dv = v.shape[-1]
    tq, tk = _attn_tiles(r)
    nq, nk = r // tq, r // tk

    def last_k(i):
        return ((i + 1) * tq - 1) // tk

    def body(q_ref, k_ref, v_ref, o_ref, lse_ref, m_sc, l_sc, acc_sc):
        i, j = pl.program_id(1), pl.program_id(2)

        @pl.when(j == 0)
        def _():
            m_sc[...] = jnp.full_like(m_sc, NEG)
            l_sc[...] = jnp.zeros_like(l_sc)
            acc_sc[...] = jnp.zeros_like(acc_sc)

        def step(masked):
            s = lax.dot_general(q_ref[0], k_ref[0], NT, preferred_element_type=F32)
            if masked:
                s = jnp.where(_mask(i, j, tq, tk), s, NEG)
            m_old = m_sc[...]
            m_new = jnp.maximum(m_old, jnp.max(s, axis=-1, keepdims=True))
            alpha = jnp.exp2(m_old - m_new)
            p = jnp.exp2(s - jnp.tile(m_new, (1, tk // LANES)))
            l_sc[...] = alpha * l_sc[...] + jnp.sum(p, axis=-1, keepdims=True)
            acc_sc[...] = alpha[:, :dv] * acc_sc[...] + jnp.dot(p.astype(BF16), v_ref[0], preferred_element_type=F32)
            m_sc[...] = m_new

        needs_mask = (j == last_k(i)) | (j == 0)
        pl.when(needs_mask)(functools.partial(step, True))
        pl.when((j < last_k(i)) & (j > 0))(functools.partial(step, False))

        @pl.when(j == last_k(i))
        def _():
            l = l_sc[...]
            o_ref[0] = (acc_sc[...] / l[:, :dv]).astype(BF16)
            lse_ref[0] = (m_sc[...] + jnp.log2(l))[:, :1]

    def kv_map(h, i, j):
        return (h, jnp.minimum(j, last_k(i)), 0)

    return pl.pallas_call(
        body, name=name,
        out_shape=[jax.ShapeDtypeStruct((nh, r, dv), BF16), jax.ShapeDtypeStruct((nh, r, 1), F32)],
        grid=(nh, nq, nk),
        in_specs=[pl.BlockSpec((1, tq, dk), lambda h, i, j: (h, i, 0)),
                  pl.BlockSpec((1, tk, dk), kv_map), pl.BlockSpec((1, tk, dv), kv_map)],
        out_specs=[pl.BlockSpec((1, tq, dv), lambda h, i, j: (h, i, 0)),
                   pl.BlockSpec((1, tq, 1), lambda h, i, j: (h, i, 0))],
        scratch_shapes=[pltpu.VMEM((tq, LANES), F32), pltpu.VMEM((tq, LANES), F32), pltpu.VMEM((tq, dv), F32)],
        compiler_params=_params(("arbitrary", "arbitrary", "arbitrary")),
    )(q, k, v)


def attn_bwd(q, k, v, do, lse, delta, dk_prev, dv_prev, name):
    nh, r, dk = q.shape
    dv = v.shape[-1]
    tq, tk = _attn_tiles(r)
    nq, nk = r // tq, r // tk
    has_prev = dk_prev is not None

    def first_q(j):
        return (j * tk) // tq

    def body(*refs):
        if has_prev:
            (q_ref, k_ref, v_ref, do_ref, lse_ref, dl_ref, pk_ref, pv_ref,
             dq_ref, dk_ref, dv_ref, dk_sc, dv_sc) = refs
        else:
            q_ref, k_ref, v_ref, do_ref, lse_ref, dl_ref, dq_ref, dk_ref, dv_ref, dk_sc, dv_sc = refs
        j, i = pl.program_id(1), pl.program_id(2)

        @pl.when((j == 0) & (i == 0))
        def _():
            dq_ref[...] = jnp.zeros_like(dq_ref)

        @pl.when(i == 0)
        def _():
            dk_sc[...] = jnp.zeros_like(dk_sc)
            dv_sc[...] = jnp.zeros_like(dv_sc)

        def step(masked):
            q_, k_, do_ = q_ref[0], k_ref[0], do_ref[0]
            st = lax.dot_general(k_, q_, NT, preferred_element_type=F32)
            if masked:
                st = jnp.where(_mask(i, j, tq, tk, keys_on_rows=True), st, NEG)
            pt = jnp.exp2(st - lse_ref[0])
            dv_sc[...] += jnp.dot(pt.astype(BF16), do_, preferred_element_type=F32)
            dpt = lax.dot_general(v_ref[0], do_, NT, preferred_element_type=F32)
            dst = (pt * (dpt - dl_ref[0])).astype(BF16)
            dk_sc[...] += jnp.dot(dst, q_, preferred_element_type=F32)
            rows = pl.ds(pl.multiple_of(i * tq, tq), tq)
            dq_ref[0, rows, :] += lax.dot_general(dst, k_, TN, preferred_element_type=F32)

        live = i >= first_q(j)
        needs_mask = (i == first_q(j)) | (j == 0)
        pl.when(live & needs_mask)(functools.partial(step, True))
        pl.when(live & jnp.logical_not(needs_mask))(functools.partial(step, False))

        @pl.when(i == nq - 1)
        def _():
            dk_ = dk_sc[...] * (1.0 / LOG2E)
            dv_ = dv_sc[...]
            if has_prev:
                dk_ = dk_ + pk_ref[0]
                dv_ = dv_ + pv_ref[0]
            dk_ref[0] = dk_
            dv_ref[0] = dv_

    def q_map(h, j, i):
        return (h, jnp.maximum(i, first_q(j)), 0)

    def q_vec_map(h, j, i):
        return (h, 0, jnp.maximum(i, first_q(j)))

    krow = lambda w: pl.BlockSpec((1, tk, w), lambda h, j, i: (h, j, 0))
    qrow = lambda w: pl.BlockSpec((1, tq, w), q_map)
    qvec = pl.BlockSpec((1, 1, tq), q_vec_map)
    in_specs = [qrow(dk), krow(dk), krow(dv), qrow(dv), qvec, qvec]
    ins = [q, k, v, do, lse, delta]
    if has_prev:
        in_specs += [krow(dk), krow(dv)]
        ins += [dk_prev, dv_prev]
    return pl.pallas_call(
        body, name=name,
        out_shape=[jax.ShapeDtypeStruct((nh, r, dk), F32), jax.ShapeDtypeStruct((nh, r, dk), F32),
                   jax.ShapeDtypeStruct((nh, r, dv), F32)],
        grid=(nh, nk, nq), in_specs=in_specs,
        out_specs=[pl.BlockSpec((1, r, dk), lambda h, j, i: (h, 0, 0)), krow(dk), krow(dv)],
        scratch_shapes=[pltpu.VMEM((tk, dk), F32), pltpu.VMEM((tk, dv), F32)],
        compiler_params=_params(("arbitrary", "arbitrary", "arbitrary"), VMEM_BIG),
    )(*ins)


def attn_out_bwd(dattn, wo, o, name):
    r, d = dattn.shape
    nh, dv, _ = wo.shape
    tm = _row_tile(r)

    def body(da_ref, w_ref, o_ref, do_ref, dl_ref):
        da = da_ref[...].astype(BF16)
        for h in range(nh):
            do_ = lax.dot_general(da, w_ref[h], NT, preferred_element_type=F32).astype(BF16)
            do_ref[h] = do_
            dl_ref[h] = jnp.sum(do_.astype(F32) * o_ref[h].astype(F32), axis=-1, keepdims=True)

    return pl.pallas_call(
        body, name=name,
        out_shape=[jax.ShapeDtypeStruct((nh, r, dv), BF16), jax.ShapeDtypeStruct((nh, r, 1), F32)],
        grid=(r // tm,),
        in_specs=[pl.BlockSpec((tm, d), lambda i: (i, 0)), pl.BlockSpec((nh, dv, d), lambda i: (0, 0, 0)),
                  pl.BlockSpec((nh, tm, dv), lambda i: (0, i, 0))],
        out_specs=[pl.BlockSpec((nh, tm, dv), lambda i: (0, i, 0)), pl.BlockSpec((nh, tm, 1), lambda i: (0, i, 0))],
        compiler_params=_params(("arbitrary",)),
    )(dattn, wo, o)


def _pool_counts(row0, n, window):
    rows = row0 + lax.broadcasted_iota(jnp.int32, (n, 1), 0)
    cnt = jnp.clip(rows - META_ROW0 + 1, 1, window)
    return 1.0 / cnt.astype(F32)


def pool_fwd(h, gamma, wp, scale, name):
    r, d = h.shape
    ng, cg, _ = wp.shape
    tm = _row_tile(r)
    hb = tm // HALO

    def body(h_ref, hp_ref, g_ref, w_ref, sc_ref, o_ref):
        i = pl.program_id(0)
        xm = h_ref[...]
        xp = hp_ref[...] * jnp.where(i > 0, 1.0, 0.0)
        xx = jnp.concatenate([xp, xm], axis=0)
        xh, _ = _rms(xx)
        u = xh * g_ref[...]
        for g, win in enumerate(POOL_WINDOWS):
            sl = slice(g * cg, (g + 1) * cg)
            ug = u[:, sl]
            acc, k = ug, 1
            while k < win:
                acc = acc + pltpu.roll(acc, k, 0)
                k *= 2
            pooled = acc[HALO:] * _pool_counts(i * tm, tm, win) - ug[HALO:]
            y = jnp.dot(pooled.astype(BF16), w_ref[g], preferred_element_type=F32)
            o_ref[:, sl] = xm[:, sl] + y * sc_ref[:, sl]

    return pl.pallas_call(
        body, name=name, out_shape=jax.ShapeDtypeStruct((r, d), F32), grid=(r // tm,),
        in_specs=[pl.BlockSpec((tm, d), lambda i: (i, 0)),
                  pl.BlockSpec((HALO, d), lambda i: (jnp.maximum(i * hb - 1, 0), 0)),
                  pl.BlockSpec((1, d), lambda i: (0, 0)), pl.BlockSpec((ng, cg, cg), lambda i: (0, 0, 0)),
                  pl.BlockSpec((1, d), lambda i: (0, 0))],
        out_specs=pl.BlockSpec((tm, d), lambda i: (i, 0)),
        compiler_params=_params(("arbitrary",), VMEM_BIG),
    )(h, h, gamma, wp, scale)


def pool_bwd(h, gamma, wp, scale, dh, name):
    r, d = h.shape
    ng, cg, _ = wp.shape
    tm = _row_tile(r)
    hb = tm // HALO
    nt = r // tm

    def body(h_ref, hp_ref, dh_ref, dn_ref, g_ref, w_ref, sc_ref, dx_ref, dgam_ref, dw_ref, dsc_ref, du_sc):
        i = pl.program_id(0)

        @pl.when(i == 0)
        def _():
            dgam_ref[...] = jnp.zeros_like(dgam_ref)
            dw_ref[...] = jnp.zeros_like(dw_ref)
            dsc_ref[...] = jnp.zeros_like(dsc_ref)

        xm = h_ref[...]
        xp = hp_ref[...] * jnp.where(i > 0, 1.0, 0.0)
        xh_all, rstd_all = _rms(jnp.concatenate([xp, xm], axis=0))
        u = xh_all * g_ref[...]
        dm = dh_ref[...]
        dn = dn_ref[...] * jnp.where(i < nt - 1, 1.0, 0.0)
        dd = jnp.concatenate([dm, dn], axis=0)
        for g, win in enumerate(POOL_WINDOWS):
            sl = slice(g * cg, (g + 1) * cg)
            ug = u[:, sl]
            acc, k = ug, 1
            while k < win:
                acc = acc + pltpu.roll(acc, k, 0)
                k *= 2
            pooled = (acc[HALO:] * _pool_counts(i * tm, tm, win) - ug[HALO:]).astype(BF16)
            y = jnp.dot(pooled, w_ref[g], preferred_element_type=F32)
            dsc_ref[:, sl] += jnp.sum(dm[:, sl] * y, axis=0, keepdims=True)
            dyp = (dd[:, sl] * sc_ref[:, sl]).astype(BF16)
            dw_ref[g] += lax.dot_general(pooled, dyp[:tm], TN, preferred_element_type=F32)
            dpo = lax.dot_general(dyp, w_ref[g], NT, preferred_element_type=F32)
            z = dpo * _pool_counts(i * tm, tm + HALO, win)
            fwd, k = z, 1
            while k < win:
                fwd = fwd + pltpu.roll(fwd, tm + HALO - k, 0)
                k *= 2
            du_sc[:, sl] = fwd[:tm] - dpo[:tm]
        du = du_sc[...]
        xh, rstd = xh_all[HALO:], rstd_all[HALO:]
        dgam_ref[...] += jnp.sum(du * xh, axis=0, keepdims=True)
        dx_ref[...] = dm + _rms_bwd(xh, rstd, du * g_ref[...])

    row = pl.BlockSpec((tm, d), lambda i: (i, 0))
    vec = pl.BlockSpec((1, d), lambda i: (0, 0))
    prev = pl.BlockSpec((HALO, d), lambda i: (jnp.maximum(i * hb - 1, 0), 0))
    nxt = pl.BlockSpec((HALO, d), lambda i: (jnp.minimum((i + 1) * hb, r // HALO - 1), 0))
    wsp = pl.BlockSpec((ng, cg, cg), lambda i: (0, 0, 0))
    return pl.pallas_call(
        body, name=name,
        out_shape=[jax.ShapeDtypeStruct((r, d), F32), jax.ShapeDtypeStruct((1, d), F32),
                   jax.ShapeDtypeStruct((ng, cg, cg), F32), jax.ShapeDtypeStruct((1, d), F32)],
        grid=(nt,), in_specs=[row, prev, row, nxt, vec, wsp, vec], out_specs=[row, vec, wsp, vec],
        scratch_shapes=[pltpu.VMEM((tm, d), F32)],
        compiler_params=_params(("arbitrary",), VMEM_BIG),
    )(h, h, dh, dh, gamma, wp, scale)


def loss_head(h, gamma, target, seq, name):
    r, d = h.shape
    tm = _row_tile(r)

    def body(h_ref, g_ref, t_ref, sse_ref, dh_ref, dgam_ref):
        i = pl.program_id(0)

        @pl.when(i == 0)
        def _():
            sse_ref[...] = jnp.zeros_like(sse_ref)
            dgam_ref[...] = jnp.zeros_like(dgam_ref)

        xh, rstd = _rms(h_ref[...])
        rows = i * tm + lax.broadcasted_iota(jnp.int32, (tm, 1), 0)
        valid = ((rows >= FRONT) & (rows < FRONT + seq)).astype(F32)
        e = (xh * g_ref[...] - t_ref[...]) * valid
        sse_ref[...] += jnp.sum(jnp.sum(e * e, axis=1, keepdims=True), axis=0, keepdims=True)
        dy = e * (1.0 / d)
        dgam_ref[...] += jnp.sum(dy * xh, axis=0, keepdims=True)
        dh_ref[...] = _rms_bwd(xh, rstd, dy * g_ref[...])

    row = pl.BlockSpec((tm, d), lambda i: (i, 0))
    vec = pl.BlockSpec((1, d), lambda i: (0, 0))
    return pl.pallas_call(
        body, name=name,
        out_shape=[jax.ShapeDtypeStruct((1, 1), F32), jax.ShapeDtypeStruct((r, d), F32),
                   jax.ShapeDtypeStruct((1, d), F32)],
        grid=(r // tm,), in_specs=[row, vec, row],
        out_specs=[pl.BlockSpec((1, 1), lambda i: (0, 0)), row, vec],
        compiler_params=_params(("arbitrary",)),
    )(h, gamma, target)


SMALL = ("meta_tokens", "pool_w", "pool_scale", "w_dkv", "w_uk", "w_uv", "w_dq", "w_uq", "w_o")


def _pack(parts):
    flat = jnp.concatenate([p.reshape(-1) for p in parts])
    n = flat.shape[0]
    unit = PACK_W * 2 * SUBLANES
    n_pad = -(-n // unit) * unit
    return jnp.pad(flat, (0, n_pad - n)).reshape(n_pad // PACK_W, PACK_W)


def _unpack(buf, shapes, lead=()):
    flat = buf.reshape(lead + (-1,))
    out, off = [], 0
    for shp in shapes:
        n = math.prod(shp)
        out.append(flat[..., off:off + n].reshape(lead + tuple(shp)))
        off += n
    return out


def _cols_from_shards(a, axis):
    a = jnp.moveaxis(a, 0, axis)
    shp = a.shape
    return a.reshape(shp[:axis] + (shp[axis] * shp[axis + 1],) + shp[axis + 2:])


def _cols_to_shards(a, axis):
    shp = a.shape
    a = a.reshape(shp[:axis] + (N_SHARD, shp[axis] // N_SHARD) + shp[axis + 1:])
    return jnp.moveaxis(a, axis, 0)


SMALL_AXIS = {"meta_tokens": 1, "pool_w": 2, "pool_scale": 1, "w_dkv": 0, "w_uk": 1, "w_uv": 1,
              "w_dq": 1, "w_uq": 2, "w_o": 2}


def kernel(x, meta_tokens, ffn1_norm, ffn1_w_gate, ffn1_w_up, ffn1_w_down, mix_norm, ffn2_norm, ffn2_w_gate, ffn2_w_up, ffn2_w_down, pool_w, pool_scale, kv_in_norm, w_dkv, kv_latent_norm, w_uk, w_uv, w_dq, q_latent_norm, w_uq, w_o, final_norm, loss_target, m_meta_tokens, m_ffn1_norm, m_ffn1_w_gate, m_ffn1_w_up, m_ffn1_w_down, m_mix_norm, m_ffn2_norm, m_ffn2_w_gate, m_ffn2_w_up, m_ffn2_w_down, m_pool_w, m_pool_scale, m_kv_in_norm, m_w_dkv, m_kv_latent_norm, m_w_uk, m_w_uv, m_w_dq, m_q_latent_norm, m_w_uq, m_w_o, m_final_norm, v_meta_tokens, v_ffn1_norm, v_ffn1_w_gate, v_ffn1_w_up, v_ffn1_w_down, v_mix_norm, v_ffn2_norm, v_ffn2_w_gate, v_ffn2_w_up, v_ffn2_w_down, v_pool_w, v_pool_scale, v_kv_in_norm, v_w_dkv, v_kv_latent_norm, v_w_uk, v_w_uv, v_w_dq, v_q_latent_norm, v_w_uq, v_w_o, v_final_norm):
    args = dict(locals())
    W = {n: args[n] for n in NAMES}
    M = {n: args["m_" + n] for n in NAMES}
    V = {n: args["v_" + n] for n in NAMES}

    depth = ffn1_norm.shape[0]
    n_a = pool_w.shape[0]
    seq, d = x.shape[1], x.shape[2]
    nh = N_HEADS
    r = -(-(FRONT + seq) // LANES) * LANES

    cx, cy, cc = lax.axis_index("x"), lax.axis_index("y"), lax.axis_index("c")
    c_arr = jnp.reshape(cc, (1,)).astype(jnp.int32)
    s_arr = jnp.reshape(2 * cx + cy, (1,)).astype(jnp.int32)

    small_shapes = [W[n].shape for n in SMALL]
    gathered = all_gather_shards([_pack([W[n] for n in SMALL])], "ag_small")[0]
    small_full = {}
    for n, part in zip(SMALL, _unpack(gathered, small_shapes, (N_SHARD,))):
        small_full[n] = _cols_from_shards(part, SMALL_AXIS[n])

    ffn_w = {}
    for l in range(depth):
        for f, (g_, u_, d_) in (("ffn1", (ffn1_w_gate, ffn1_w_up, ffn1_w_down)),
                                ("ffn2", (ffn2_w_gate, ffn2_w_up, ffn2_w_down))):
            ffn_w[f, l] = all_gather_shards(
                [g_[l].astype(BF16), u_[l].astype(BF16), d_[l].astype(BF16)], f"ag_{f}_{l}")

    meta_full = small_full["meta_tokens"]
    wp = small_full["pool_w"].astype(BF16)
    pscale = small_full["pool_scale"]
    wdkv = jnp.pad(small_full["w_dkv"], ((0, 0), (0, HEAD_W - QK_ROPE))).astype(BF16)[None]
    wuk = small_full["w_uk"].reshape(KV_RANK, nh, QK_NOPE).transpose(1, 0, 2)
    wk_h = jnp.concatenate([jnp.zeros((nh, KV_RANK, HEAD_W - QK_NOPE), F32), wuk], axis=-1).astype(BF16)
    wv_h = small_full["w_uv"].reshape(KV_RANK, nh, V_HEAD).transpose(1, 0, 2).astype(BF16)
    wdq = small_full["w_dq"].astype(BF16)
    wuq = small_full["w_uq"].reshape(-1, Q_RANK, nh, QK_NOPE + QK_ROPE).transpose(0, 2, 1, 3)
    wq_h = jnp.concatenate([wuq[..., QK_NOPE:], jnp.zeros(wuq.shape[:-1] + (HEAD_W - QK_NOPE - QK_ROPE,), F32),
                            wuq[..., :QK_NOPE]], axis=-1).astype(BF16)
    wo_h = small_full["w_o"].reshape(-1, nh, V_HEAD, d).astype(BF16)

    ctab, stab = rope_tables(r)

    def vec(a):
        return a.reshape(1, -1)

    h = jnp.concatenate([jnp.zeros((META_ROW0, d), F32), meta_full, x[0],
                         jnp.zeros((r - FRONT - seq, d), F32)], axis=0)
    target = jnp.concatenate([jnp.zeros((FRONT, d), F32), loss_target[0],
                              jnp.zeros((r - FRONT - seq, d), F32)], axis=0)
    saved = []
    kv = None
    for l in range(depth):
        sv = {"h0": h}
        h, sv["g1"], sv["u1"] = ffn_fwd(h, vec(ffn1_norm[l]), *ffn_w["ffn1", l], f"ffn1_fwd_{l}")
        sv["h1"] = h
        if l < n_a:
            h = pool_fwd(h, vec(mix_norm[l]), wp[l], vec(pscale[l]), f"pool_fwd_{l}")
        else:
            j = l - n_a
            u = norm_fwd(h, vec(mix_norm[l]), f"mixnorm_{l}")
            cq0 = rowmm(u[None], wdq[j][None], f"dq_{l}")
            cq = norm_fwd(cq0, vec(q_latent_norm[j]), f"qnorm_{l}")
            q = proj_rope(cq, wq_h[j], ctab, stab, None, f"qproj_{l}", scale=Q_SCALE)
            o, lse = attn_fwd(q, kv["k"], kv["v"], f"attn_fwd_{l}")
            h = rowmm(o, wo_h[j], f"oproj_{l}", res=h)
            sv.update(u=u, cq0=cq0, cq=cq, q=q, o=o, lse=lse)
        sv["h2"] = h
        h, sv["g2"], sv["u2"] = ffn_fwd(h, vec(ffn2_norm[l]), *ffn_w["ffn2", l], f"ffn2_fwd_{l}")
        saved.append(sv)
        if l == n_a - 1:
            hkv = norm_fwd(h, vec(kv_in_norm), "kvin_norm")
            ckr = rowmm(hkv[None], wdkv, "dkv")
            ckv = norm_fwd(ckr, vec(kv_latent_norm), "kvlat_norm")
            kv = {"h": h, "hkv": hkv, "ckr": ckr, "ckv": ckv,
                  "k": proj_rope(ckv, wk_h, ctab, stab, ckr, "kproj"),
                  "v": rowmm(ckv[None], wv_h, "vproj", out_dtype=BF16, heads_out=True)}

    sse, dh, dfinal = loss_head(h, vec(final_norm), target, seq, "loss_head")
    loss = lax.psum(0.5 / d * sse[0, 0], ("x", "y", "c"))

    G = {}
    GF = {}
    dnorm = {n: [None] * depth for n in ("ffn1_norm", "mix_norm", "ffn2_norm")}
    dqnorm = [None] * (depth - n_a)
    dpool_w, dpool_scale = [None] * n_a, [None] * n_a
    dwdq, dwq_h, dwo_h = [None] * (depth - n_a), [None] * (depth - n_a), [None] * (depth - n_a)

    def ffn_backward(f, l, h_in, dh_, gg, uu):
        gam = ffn1_norm[l] if f == "ffn1" else ffn2_norm[l]
        wg_, wu_, wd_ = ffn_w[f, l]
        dh_in, dgam, hn, dy, dg, du, a = ffn_bwd_act(h_in, vec(gam), dh_, gg, uu, wg_, wu_, wd_, f"{f}_bwd_act_{l}")
        dwg, dwu, dwd = ffn_bwd_weights(hn, dy, a, dg, du, f"{f}_bwd_w_{l}")
        GF[f + "_w_gate", l], GF[f + "_w_up", l], GF[f + "_w_down", l] = dwg, dwu, dwd
        dnorm[f + "_norm"][l] = dgam[0]
        return dh_in

    dk_tot = dv_tot = None
    for l in reversed(range(depth)):
        sv = saved[l]
        if l == n_a - 1:
            dkr = rope_bwd_sum(dk_tot, ctab, stab, "kproj_bwd_rope")
            dckv = rowmm(dk_tot, wk_h, "kproj_bwd", nt=True)
            dckv = rowmm(dv_tot, wv_h, "vproj_bwd", nt=True, res=dckv)
            dwk_h = tnmm(kv["ckv"][None], dk_tot, "kproj_bwd_w")
            dwv_h = tnmm(kv["ckv"][None], dv_tot, "vproj_bwd_w")
            dlat, dkvlat = norm_bwd(kv["ckr"], vec(kv_latent_norm), dckv, None, "kvlat_norm_bwd")
            dckr = jnp.concatenate([dlat, dkr], axis=1).astype(BF16)
            dhkv = rowmm(dckr[None], wdkv, "dkv_bwd", nt=True)
            dwdkv = tnmm(kv["hkv"][None], dckr[None], "dkv_bwd_w")[0]
            dh, dkvin = norm_bwd(kv["h"], vec(kv_in_norm), dhkv, dh, "kvin_norm_bwd")
            G["w_dkv"] = dwdkv[:, :KV_RANK + QK_ROPE]
            G["w_uk"] = dwk_h[..., HEAD_W - QK_NOPE:].transpose(1, 0, 2).reshape(KV_RANK, nh * QK_NOPE)
            G["w_uv"] = dwv_h.transpose(1, 0, 2).reshape(KV_RANK, nh * V_HEAD)
        dh = ffn_backward("ffn2", l, sv["h2"], dh, sv["g2"], sv["u2"])
        if l < n_a:
            dh, dmix, dpool_w[l], dps = pool_bwd(sv["h1"], vec(mix_norm[l]), wp[l], vec(pscale[l]), dh, f"pool_bwd_{l}")
            dnorm["mix_norm"][l] = dmix[0]
            dpool_scale[l] = dps[0]
        else:
            j = l - n_a
            do, delta = attn_out_bwd(dh, wo_h[j], sv["o"], f"oproj_bwd_{l}")
            dwo_h[j] = tnmm(sv["o"], dh[None], f"oproj_bwd_w_{l}")
            dq, dk_tot, dv_tot = attn_bwd(sv["q"], kv["k"], kv["v"], do, sv["lse"].reshape(nh, 1, r),
                                          delta.reshape(nh, 1, r), dk_tot, dv_tot, f"attn_bwd_{l}")
            dxq = rope_bwd_heads(dq, ctab, stab, f"qproj_bwd_rope_{l}", scale=SM_SCALE)
            dcq = rowmm(dxq, wq_h[j], f"qproj_bwd_{l}", nt=True)
            dwq_h[j] = tnmm(sv["cq"][None], dxq, f"qproj_bwd_w_{l}")
            dcq0, dqn = norm_bwd(sv["cq0"], vec(q_latent_norm[j]), dcq, None, f"qnorm_bwd_{l}")
            dqnorm[j] = dqn[0]
            dcq0b = dcq0.astype(BF16)
            du = rowmm(dcq0b[None], wdq[j][None], f"dq_bwd_{l}", nt=True)
            dwdq[j] = tnmm(sv["u"][None], dcq0b[None], f"dq_bwd_w_{l}")[0]
            dh, dmix = norm_bwd(sv["h1"], vec(mix_norm[l]), du, dh, f"mixnorm_bwd_{l}")
            dnorm["mix_norm"][l] = dmix[0]
        dh = ffn_backward("ffn1", l, sv["h0"], dh, sv["g1"], sv["u1"])

    grad_x = dh[FRONT:FRONT + seq][None]
    G["meta_tokens"] = dh[META_ROW0:FRONT]
    G["pool_w"] = jnp.stack(dpool_w)
    G["pool_scale"] = jnp.stack(dpool_scale)
    G["w_dq"] = jnp.stack(dwdq)
    dwq = jnp.stack(dwq_h)
    dwq = jnp.concatenate([dwq[..., HEAD_W - QK_NOPE:], dwq[..., :QK_ROPE]], axis=-1)
    G["w_uq"] = dwq.transpose(0, 2, 1, 3).reshape(-1, Q_RANK, nh * (QK_NOPE + QK_ROPE))
    G["w_o"] = jnp.stack(dwo_h).reshape(-1, nh * V_HEAD, d)

    REPL = ("ffn1_norm", "mix_norm", "ffn2_norm", "kv_in_norm", "kv_latent_norm", "q_latent_norm", "final_norm")
    grep = {"ffn1_norm": jnp.stack(dnorm["ffn1_norm"]), "mix_norm": jnp.stack(dnorm["mix_norm"]),
            "ffn2_norm": jnp.stack(dnorm["ffn2_norm"]), "kv_in_norm": dkvin[0], "kv_latent_norm": dkvlat[0],
            "q_latent_norm": jnp.stack(dqnorm), "final_norm": dfinal[0]}

    def pack128(parts):
        flat = jnp.concatenate([p.reshape(-1) for p in parts])
        n = flat.shape[0]
        n_pad = -(-n // (LANES * SUBLANES)) * (LANES * SUBLANES)
        return jnp.pad(flat, (0, n_pad - n)).reshape(-1, LANES)

    rep_shapes = [W[n].shape for n in REPL]
    g_rep = all_reduce_small(pack128([grep[n] for n in REPL]), "ar_repl")
    d_rep, m_rep, v_rep = adamw(pack128([W[n] for n in REPL]), g_rep, pack128([M[n] for n in REPL]),
                                pack128([V[n] for n in REPL]), "adamw_repl")
    out_g, out_d, out_m, out_v = {}, {}, {}, {}
    for dst, buf in ((out_g, g_rep), (out_d, d_rep), (out_m, m_rep), (out_v, v_rep)):
        for n, a in zip(REPL, _unpack(buf, rep_shapes)):
            dst[n] = a

    g_small = jnp.stack([_pack([_cols_to_shards(G[n], SMALL_AXIS[n])[s] for n in SMALL]) for s in range(N_SHARD)])
    g_small = reduce_scatter([g_small], c_arr, s_arr, "small")[0]
    d_s, m_s, v_s = adamw(_pack([W[n] for n in SMALL]), g_small, _pack([M[n] for n in SMALL]),
                          _pack([V[n] for n in SMALL]), "adamw_small")
    for dst, buf in ((out_g, g_small), (out_d, d_s), (out_m, m_s), (out_v, v_s)):
        for n, a in zip(SMALL, _unpack(buf, small_shapes)):
            dst[n] = a

    FFN = ("ffn1_w_gate", "ffn1_w_up", "ffn1_w_down", "ffn2_w_gate", "ffn2_w_up", "ffn2_w_down")
    per = {n: [] for n in FFN}
    for l in range(depth):
        for f in ("ffn1", "ffn2"):
            names = [f + "_w_gate", f + "_w_up", f + "_w_down"]
            gs = reduce_scatter([GF[n, l] for n in names], c_arr, s_arr, f"{f}_{l}")
            for n, g_ in zip(names, gs):
                per[n].append((g_,) + tuple(adamw(W[n][l], g_, M[n][l], V[n][l], f"adamw_{n}_{l}")))
    for n in FFN:
        out_g[n], out_d[n], out_m[n], out_v[n] = (jnp.stack([t[k] for t in per[n]]) for k in range(4))

    return (loss, grad_x, *[out_g[n] for n in NAMES], *[out_d[n] for n in NAMES],
            *[out_m[n] for n in NAMES], *[out_v[n] for n in NAMES])


NAMES = ("meta_tokens", "ffn1_norm", "ffn1_w_gate", "ffn1_w_up", "ffn1_w_down", "mix_norm", "ffn2_norm",
         "ffn2_w_gate", "ffn2_w_up", "ffn2_w_down", "pool_w", "pool_scale", "kv_in_norm", "w_dkv",
         "kv_latent_norm", "w_uk", "w_uv", "w_dq", "q_latent_norm", "w_uq", "w_o", "final_norm")
```

```python
import functools
import math

import jax
import jax.numpy as jnp
from jax import lax
from jax.experimental import pallas as pl
from jax.experimental.pallas import tpu as pltpu

F32 = jnp.float32
BF16 = jnp.bfloat16
MESH = pl.DeviceIdType.MESH
ANY = pl.BlockSpec(memory_space=pl.ANY)

EPS = 1e-6
CHUNK = 64
CHUNK_SHIFT = 6
N_META = 16
FRONT = 64
META_ROW0 = FRONT - N_META
POOL_WINDOWS = (2, 4, 8, 16)
HALO = 16
N_HEADS = 8
QK_NOPE = 64
QK_ROPE = 32
V_HEAD = 64
HEAD_W = 128
KV_RANK = 256
Q_RANK = 384
ROPE_THETA = 10000.0
NEG = -1e30
N_SHARD = 4
LANES = 128
SUBLANES = 8
PACK_W = 512
VMEM_BIG = 52 * 1024 * 1024

ADAM_LR = 0.001
ADAM_B1 = 0.9
ADAM_B2 = 0.999
ADAM_EPS = 1e-08
ADAM_WD = 0.01
ADAM_STEP = 10

NT = (((1,), (1,)), ((), ()))
TN = (((0,), (0,)), ((), ()))


def _params(sem=None, vmem=None):
    return pltpu.CompilerParams(dimension_semantics=sem, vmem_limit_bytes=vmem)


def _tile(n, pref, mult=SUBLANES):
    best = None
    for t in range(mult, min(n, pref) + 1, mult):
        if n % t == 0:
            best = t
    return best if best is not None else n


def _row_tile(r):
    return 640 if r % 640 == 0 else 128


def _rms(x):
    rstd = lax.rsqrt(jnp.mean(x * x, axis=-1, keepdims=True) + EPS)
    return x * rstd, rstd


def _rms_bwd(xh, rstd, dxh):
    return rstd * (dxh - xh * jnp.mean(dxh * xh, axis=-1, keepdims=True))


def _sigmoid(x):
    return 1.0 / (1.0 + jnp.exp(-x))


def _place():
    x, y, c = lax.axis_index("x"), lax.axis_index("y"), lax.axis_index("c")
    chips = [(1 - x, y), (x, 1 - y), (1 - x, 1 - y)]
    return x, y, c, chips


def all_gather_shards(shards, name):
    n = len(shards)

    def body(*refs):
        ins, outs = refs[:n], refs[n:2 * n]
        send1, recv1, send2, recv2, lsem = refs[2 * n:]
        x, y, c, chips = _place()
        s = 2 * x + y
        sib = (x, y, 1 - c)

        def rcopy(k, j, src, dst, to, first):
            return pltpu.make_async_remote_copy(
                src_ref=src, dst_ref=dst,
                send_sem=(send1 if first else send2).at[k, j],
                recv_sem=(recv1 if first else recv2).at[k, j],
                device_id=to, device_id_type=MESH)

        started, local = [], []
        for k in range(n):
            hf = ins[k].shape[0] // 2
            cp = pltpu.make_async_copy(ins[k], outs[k].at[s], lsem.at[k])
            cp.start()
            local.append(cp)
            for j, (cx, cy) in enumerate(chips):
                r = rcopy(k, j, ins[k].at[pl.ds(c * hf, hf)], outs[k].at[s, pl.ds(c * hf, hf)],
                          (cx, cy, c), True)
                r.start()
                started.append(r)
        for k in range(n):
            hf = ins[k].shape[0] // 2
            for j, (cx, cy) in enumerate(chips):
                blk = outs[k].at[2 * cx + cy, pl.ds(c * hf, hf)]
                rcopy(k, j, blk, blk, (cx, cy, c), True).wait_recv()
                f = rcopy(k, j, blk, blk, sib, False)
                f.start()
                started.append(f)
        for k in range(n):
            hf = ins[k].shape[0] // 2
            for j, (cx, cy) in enumerate(chips):
                blk = outs[k].at[2 * cx + cy, pl.ds((1 - c) * hf, hf)]
                rcopy(k, j, blk, blk, sib, False).wait_recv()
        for r in started:
            r.wait_send()
        for cp in local:
            cp.wait()

    return pl.pallas_call(
        body, name=name,
        out_shape=[jax.ShapeDtypeStruct((N_SHARD,) + a.shape, a.dtype) for a in shards],
        in_specs=[ANY] * n, out_specs=[ANY] * n,
        scratch_shapes=[pltpu.SemaphoreType.DMA((n, 3))] * 4 + [pltpu.SemaphoreType.DMA((n,))],
    )(*shards)


def sibling_swap_halves(gs, name):
    n = len(gs)

    def body(*refs):
        ins, outs = refs[:n], refs[n:2 * n]
        send, recv = refs[2 * n:]
        x, y, c, _ = _place()
        sib = (x, y, 1 - c)
        cps = []
        for k in range(n):
            hf = ins[k].shape[1] // 2
            r = pltpu.make_async_remote_copy(
                src_ref=ins[k].at[:, pl.ds((1 - c) * hf, hf)], dst_ref=outs[k],
                send_sem=send.at[k], recv_sem=recv.at[k], device_id=sib, device_id_type=MESH)
            r.start()
            cps.append(r)
        for r in cps:
            r.wait_recv()
        for r in cps:
            r.wait_send()

    return pl.pallas_call(
        body, name=name,
        out_shape=[jax.ShapeDtypeStruct((a.shape[0], a.shape[1] // 2, a.shape[2]), a.dtype) for a in gs],
        in_specs=[ANY] * n, out_specs=[ANY] * n,
        scratch_shapes=[pltpu.SemaphoreType.DMA((n,))] * 2,
    )(*gs)


def chip_exchange(ps, name):
    n = len(ps)

    def body(*refs):
        ins, outs = refs[:n], refs[n:2 * n]
        send, recv = refs[2 * n:]
        x, y, c, chips = _place()
        cps = []
        for k in range(n):
            for j, (cx, cy) in enumerate(chips):
                r = pltpu.make_async_remote_copy(
                    src_ref=ins[k].at[2 * cx + cy], dst_ref=outs[k].at[j],
                    send_sem=send.at[k, j], recv_sem=recv.at[k, j],
                    device_id=(cx, cy, c), device_id_type=MESH)
                r.start()
                cps.append(r)
        for r in cps:
            r.wait_recv()
        for r in cps:
            r.wait_send()

    return pl.pallas_call(
        body, name=name,
        out_shape=[jax.ShapeDtypeStruct((3,) + a.shape[1:], a.dtype) for a in ps],
        in_specs=[ANY] * n, out_specs=[ANY] * n,
        scratch_shapes=[pltpu.SemaphoreType.DMA((n, 3))] * 2,
    )(*ps)


def sibling_join_halves(ts, name):
    n = len(ts)

    def body(*refs):
        ins, outs = refs[:n], refs[n:2 * n]
        send, recv, lsem = refs[2 * n:]
        x, y, c, _ = _place()
        sib = (x, y, 1 - c)
        cps, local = [], []
        for k in range(n):
            hf = ins[k].shape[0]
            cp = pltpu.make_async_copy(ins[k], outs[k].at[pl.ds(c * hf, hf)], lsem.at[k])
            cp.start()
            local.append(cp)
            r = pltpu.make_async_remote_copy(
                src_ref=ins[k], dst_ref=outs[k].at[pl.ds(c * hf, hf)],
                send_sem=send.at[k], recv_sem=recv.at[k], device_id=sib, device_id_type=MESH)
            r.start()
            cps.append(r)
        for k in range(n):
            hf = ins[k].shape[0]
            pltpu.make_async_remote_copy(
                src_ref=ins[k], dst_ref=outs[k].at[pl.ds((1 - c) * hf, hf)],
                send_sem=send.at[k], recv_sem=recv.at[k], device_id=sib, device_id_type=MESH).wait_recv()
        for r in cps:
            r.wait_send()
        for cp in local:
            cp.wait()

    return pl.pallas_call(
        body, name=name,
        out_shape=[jax.ShapeDtypeStruct((2 * a.shape[0], a.shape[1]), a.dtype) for a in ts],
        in_specs=[ANY] * n, out_specs=[ANY] * n,
        scratch_shapes=[pltpu.SemaphoreType.DMA((n,))] * 3,
    )(*ts)


def all_reduce_small(part, name):
    m, w = part.shape

    def body(x_ref, tot_ref, gat_ref, send_sems, recv_sems):
        x, y, c, chips = _place()
        me, sib = (x, y, c), (x, y, 1 - c)

        def slot(px, py, pc):
            return gat_ref.at[4 * px + 2 * py + pc]

        def copy(k, block, to, src=None):
            return pltpu.make_async_remote_copy(
                src_ref=slot(*block) if src is None else src, dst_ref=slot(*block),
                send_sem=send_sems.at[k], recv_sem=recv_sems.at[k], device_id=to, device_id_type=MESH)

        gat_ref[4 * x + 2 * y + c] = x_ref[...]
        first = [copy(0, me, sib, src=x_ref)]
        first += [copy(1 + j, me, (*chip, c), src=x_ref) for j, chip in enumerate(chips)]
        for cp in first:
            cp.start()
        passed = [copy(4 + j, (*chip, c), sib) for j, chip in enumerate(chips)]
        for j, chip in enumerate(chips):
            copy(1 + j, (*chip, c), me).wait_recv()
            passed[j].start()
        copy(0, sib, me).wait_recv()
        for j, chip in enumerate(chips):
            copy(4 + j, (*chip, 1 - c), me).wait_recv()
        for cp in first + passed:
            cp.wait_send()
        tot = gat_ref[0]
        for d in range(1, 8):
            tot = tot + gat_ref[d]
        tot_ref[...] = tot

    return pl.pallas_call(
        body, name=name,
        out_shape=jax.ShapeDtypeStruct((m, w), F32),
        in_specs=[pl.BlockSpec(memory_space=pltpu.VMEM)],
        out_specs=pl.BlockSpec(memory_space=pltpu.VMEM),
        scratch_shapes=[pltpu.VMEM((8, m, w), F32), pltpu.SemaphoreType.DMA((7,)), pltpu.SemaphoreType.DMA((7,))],
    )(part)


def add_own_half(g, rec, c_arr, name):
    _, a, b = g.shape
    hf = a // 2
    tb = _tile(hf, 256)
    nb = hf // tb

    def body(c_ref, g_ref, r_ref, o_ref):
        o_ref[...] = g_ref[...] + r_ref[...]

    return pl.pallas_call(
        body, name=name,
        out_shape=jax.ShapeDtypeStruct((N_SHARD, hf, b), F32),
        grid_spec=pltpu.PrefetchScalarGridSpec(
            num_scalar_prefetch=1, grid=(N_SHARD, nb),
            in_specs=[pl.BlockSpec((1, tb, b), lambda s, i, c: (s, c[0] * nb + i, 0)),
                      pl.BlockSpec((1, tb, b), lambda s, i, c: (s, i, 0))],
            out_specs=pl.BlockSpec((1, tb, b), lambda s, i, c: (s, i, 0))),
        compiler_params=_params(("arbitrary", "arbitrary")),
    )(c_arr, g, rec)


def sum_four(p, rec, s_arr, name):
    _, ah, b = p.shape
    tb = _tile(ah, 256)

    def body(s_ref, p_ref, r0, r1, r2, o_ref):
        o_ref[...] = ((p_ref[0] + r0[0]) + r1[0]) + r2[0]

    def rspec(j):
        return pl.BlockSpec((1, tb, b), lambda i, s: (j, i, 0))

    return pl.pallas_call(
        body, name=name,
        out_shape=jax.ShapeDtypeStruct((ah, b), F32),
        grid_spec=pltpu.PrefetchScalarGridSpec(
            num_scalar_prefetch=1, grid=(ah // tb,),
            in_specs=[pl.BlockSpec((1, tb, b), lambda i, s: (s[0], i, 0)), rspec(0), rspec(1), rspec(2)],
            out_specs=pl.BlockSpec((tb, b), lambda i, s: (i, 0))),
        compiler_params=_params(("arbitrary",)),
    )(s_arr, p, rec, rec, rec)


def adamw(w, g, m, v, name):
    a, b = w.shape
    tb = _tile(a, 256)
    c1 = 1.0 - ADAM_B1 ** ADAM_STEP
    c2 = 1.0 - ADAM_B2 ** ADAM_STEP

    def body(w_ref, g_ref, m_ref, v_ref, d_ref, mo_ref, vo_ref):
        g_ = g_ref[...]
        m_ = ADAM_B1 * m_ref[...] + (1.0 - ADAM_B1) * g_
        v_ = ADAM_B2 * v_ref[...] + (1.0 - ADAM_B2) * (g_ * g_)
        m_hat = m_ / c1
        v_hat = v_ / c2
        d_ref[...] = -ADAM_LR * (m_hat / (jnp.sqrt(v_hat) + ADAM_EPS) + ADAM_WD * w_ref[...])
        mo_ref[...] = m_
        vo_ref[...] = v_

    spec = pl.BlockSpec((tb, b), lambda i: (i, 0))
    return pl.pallas_call(
        body, name=name,
        out_shape=[jax.ShapeDtypeStruct((a, b), F32)] * 3,
        grid=(a // tb,), in_specs=[spec] * 4, out_specs=[spec] * 3,
        compiler_params=_params(("arbitrary",)),
    )(w, g, m, v)


def reduce_scatter(gs, c_arr, s_arr, tag):
    rec_a = sibling_swap_halves(gs, f"rs_swap_{tag}")
    ps = [add_own_half(g, r, c_arr, f"rs_add_{tag}_{k}") for k, (g, r) in enumerate(zip(gs, rec_a))]
    rec_b = chip_exchange(ps, f"rs_ici_{tag}")
    ts = [sum_four(p, r, s_arr, f"rs_sum_{tag}_{k}") for k, (p, r) in enumerate(zip(ps, rec_b))]
    return sibling_join_halves(ts, f"rs_join_{tag}")


HBM_SPEC = pl.BlockSpec(memory_space=pltpu.HBM)
SEM_SPEC = pl.BlockSpec(memory_space=pltpu.SEMAPHORE)
EFFECT = pltpu.SideEffectType.DATAFLOW_SIDE_EFFECTING


def _in_hbm(a):
    return pltpu.with_memory_space_constraint(a, pltpu.HBM)


def _exchange_copy(k, j, chip, c, s, srcs, lands, send, recv, gather, receiving):
    cx, cy = chip
    src = srcs[k] if gather else srcs[k].at[2 * cx + cy]
    if gather:
        dst = lands[k].at[2 * cx + cy] if receiving else lands[k].at[s]
    else:
        dst = lands[k].at[j]
    return pltpu.make_async_remote_copy(src_ref=src, dst_ref=dst, send_sem=send.at[3 * k + j], recv_sem=recv.at[3 * k + j],
                                        device_id=(cx, cy, c), device_id_type=MESH)


def exchange_start(srcs, lands, name, gather):
    n = len(srcs)

    def body(*refs):
        srcs_in, lands_in = refs[:n], refs[n:2 * n]
        send, recv = refs[2 * n], refs[2 * n + 1]
        token = refs[-1]
        x, y, c, chips = _place()
        for k in range(n):
            for j, chip in enumerate(chips):
                _exchange_copy(k, j, chip, c, 2 * x + y, srcs_in, lands_in, send, recv, gather, False).start()
        token[...] = jnp.zeros_like(token)

    outs = pl.pallas_call(
        body, name=name,
        out_shape=(pltpu.SemaphoreType.DMA((3 * n,)), pltpu.SemaphoreType.DMA((3 * n,)),
                   *[pltpu.HBM(a.shape, a.dtype) for a in srcs], *[pltpu.HBM(a.shape, a.dtype) for a in lands],
                   jax.ShapeDtypeStruct((SUBLANES, LANES), F32)),
        in_specs=[HBM_SPEC] * (2 * n),
        out_specs=(SEM_SPEC, SEM_SPEC, *[HBM_SPEC] * (2 * n), pl.BlockSpec(memory_space=pltpu.VMEM)),
        input_output_aliases={k: 2 + k for k in range(2 * n)},
        compiler_params=pltpu.CompilerParams(has_side_effects=EFFECT),
    )(*[_in_hbm(a) for a in srcs], *[_in_hbm(a) for a in lands])
    return outs[0], outs[1], list(outs[2:2 + n]), list(outs[2 + n:2 + 2 * n]), outs[-1]


def exchange_wait(state, after, name, gather):
    send, recv, srcs, lands, _ = state
    n = len(srcs)

    def body(*refs):
        srcs_in, lands_in = refs[:n], refs[n:2 * n]
        send_, recv_ = refs[2 * n], refs[2 * n + 1]
        x, y, c, chips = _place()
        for k in range(n):
            for j, chip in enumerate(chips):
                cp = _exchange_copy(k, j, chip, c, 2 * x + y, srcs_in, lands_in, send_, recv_, gather, True)
                cp.wait_send()
                cp.wait_recv()

    outs = pl.pallas_call(
        body, name=name,
        out_shape=tuple(pltpu.HBM(a.shape, a.dtype) for a in srcs + lands),
        in_specs=[HBM_SPEC] * (2 * n) + [SEM_SPEC, SEM_SPEC, ANY],
        out_specs=tuple([HBM_SPEC] * (2 * n)),
        input_output_aliases={k: k for k in range(2 * n)},
        compiler_params=pltpu.CompilerParams(has_side_effects=EFFECT),
    )(*srcs, *lands, send, recv, after)
    return list(outs[:n]), list(outs[n:])


def rs_begin(gs, c_arr, tag):
    rec_a = sibling_swap_halves(gs, f"rs_swap_{tag}")
    ps = [add_own_half(g, r, c_arr, f"rs_add_{tag}_{k}") for k, (g, r) in enumerate(zip(gs, rec_a))]
    lands = [lax.empty((3,) + p.shape[1:], p.dtype) for p in ps]
    return exchange_start(ps, lands, f"rs_ici_start_{tag}", False)


def rs_finish(state, after, s_arr, tag):
    ps, rec_b = exchange_wait(state, after, f"rs_ici_wait_{tag}", False)
    ts = [sum_four(p, r, s_arr, f"rs_sum_{tag}_{k}") for k, (p, r) in enumerate(zip(ps, rec_b))]
    return sibling_join_halves(ts, f"rs_join_{tag}")


def ffn_fwd(h, gamma, wg, wu, wd, name):
    r, d = h.shape
    ns, _, fs = wg.shape
    tm = _row_tile(r)

    def body(h_ref, g_ref, wg_ref, wu_ref, wd_ref, ho_ref, gg_ref, uu_ref, hn_sc, acc_sc):
        s = pl.program_id(1)

        @pl.when(s == 0)
        def _():
            xh, _ = _rms(h_ref[...])
            hn_sc[...] = (xh * g_ref[...]).astype(BF16)
            acc_sc[...] = jnp.zeros_like(acc_sc)

        hn = hn_sc[...]
        g = jnp.dot(hn, wg_ref[0], preferred_element_type=F32)
        u = jnp.dot(hn, wu_ref[0], preferred_element_type=F32)
        gg_ref[0] = g.astype(BF16)
        uu_ref[0] = u.astype(BF16)
        a = (g * _sigmoid(g) * u).astype(BF16)
        acc_sc[...] += jnp.dot(a, wd_ref[0], preferred_element_type=F32)

        @pl.when(s == ns - 1)
        def _():
            ho_ref[...] = h_ref[...] + 0.5 * acc_sc[...]

    return pl.pallas_call(
        body, name=name,
        out_shape=[jax.ShapeDtypeStruct((r, d), F32), jax.ShapeDtypeStruct((ns, r, fs), BF16),
                   jax.ShapeDtypeStruct((ns, r, fs), BF16)],
        grid=(r // tm, ns),
        in_specs=[pl.BlockSpec((tm, d), lambda i, s: (i, 0)), pl.BlockSpec((1, d), lambda i, s: (0, 0)),
                  pl.BlockSpec((1, d, fs), lambda i, s: (s, 0, 0)), pl.BlockSpec((1, d, fs), lambda i, s: (s, 0, 0)),
                  pl.BlockSpec((1, fs, d), lambda i, s: (s, 0, 0))],
        out_specs=[pl.BlockSpec((tm, d), lambda i, s: (i, 0)), pl.BlockSpec((1, tm, fs), lambda i, s: (s, i, 0)),
                   pl.BlockSpec((1, tm, fs), lambda i, s: (s, i, 0))],
        scratch_shapes=[pltpu.VMEM((tm, d), BF16), pltpu.VMEM((tm, d), F32)],
        compiler_params=_params(("arbitrary", "arbitrary"), VMEM_BIG),
    )(h, gamma, wg, wu, wd)


def ffn_bwd_act(h, gamma, dh, gg, uu, wg, wu, wd, name):
    r, d = h.shape
    ns, _, fs = wg.shape
    tm = _row_tile(r)

    def body(h_ref, g_ref, dh_ref, gg_ref, uu_ref, wg_ref, wu_ref, wd_ref,
             dho_ref, dgam_ref, hn_ref, dy_ref, dg_ref, du_ref, a_ref, acc_sc):
        i, s = pl.program_id(0), pl.program_id(1)

        @pl.when(s == 0)
        def _():
            xh, _ = _rms(h_ref[...])
            hn_ref[...] = (xh * g_ref[...]).astype(BF16)
            dy_ref[...] = (0.5 * dh_ref[...]).astype(BF16)
            acc_sc[...] = jnp.zeros_like(acc_sc)

        @pl.when((i == 0) & (s == 0))
        def _():
            dgam_ref[...] = jnp.zeros_like(dgam_ref)

        g = gg_ref[0].astype(F32)
        u = uu_ref[0].astype(F32)
        da = lax.dot_general(dy_ref[...], wd_ref[0], NT, preferred_element_type=F32)
        sig = _sigmoid(g)
        sl = g * sig
        a_ref[0] = (sl * u).astype(BF16)
        du = (da * sl).astype(BF16)
        dg = (da * u * (sig * (1.0 + g * (1.0 - sig)))).astype(BF16)
        dg_ref[0] = dg
        du_ref[0] = du
        acc_sc[...] += (lax.dot_general(dg, wg_ref[0], NT, preferred_element_type=F32)
                        + lax.dot_general(du, wu_ref[0], NT, preferred_element_type=F32))

        @pl.when(s == ns - 1)
        def _():
            xh, rstd = _rms(h_ref[...])
            dhn = acc_sc[...]
            dgam_ref[...] += jnp.sum(dhn * xh, axis=0, keepdims=True)
            dho_ref[...] = dh_ref[...] + _rms_bwd(xh, rstd, dhn * g_ref[...])

    row = pl.BlockSpec((tm, d), lambda i, s: (i, 0))
    act = pl.BlockSpec((1, tm, fs), lambda i, s: (s, i, 0))
    return pl.pallas_call(
        body, name=name,
        out_shape=[jax.ShapeDtypeStruct((r, d), F32), jax.ShapeDtypeStruct((1, d), F32),
                   jax.ShapeDtypeStruct((r, d), BF16), jax.ShapeDtypeStruct((r, d), BF16),
                   jax.ShapeDtypeStruct((ns, r, fs), BF16), jax.ShapeDtypeStruct((ns, r, fs), BF16),
                   jax.ShapeDtypeStruct((ns, r, fs), BF16)],
        grid=(r // tm, ns),
        in_specs=[row, pl.BlockSpec((1, d), lambda i, s: (0, 0)), row, act, act,
                  pl.BlockSpec((1, d, fs), lambda i, s: (s, 0, 0)), pl.BlockSpec((1, d, fs), lambda i, s: (s, 0, 0)),
                  pl.BlockSpec((1, fs, d), lambda i, s: (s, 0, 0))],
        out_specs=[row, pl.BlockSpec((1, d), lambda i, s: (0, 0)), row, row, act, act, act],
        scratch_shapes=[pltpu.VMEM((tm, d), F32)],
        compiler_params=_params(("arbitrary", "arbitrary"), VMEM_BIG),
    )(h, gamma, dh, gg, uu, wg, wu, wd)


def ffn_bwd_weights(hn, dy, a, dg, du, name):
    r, d = hn.shape
    ns, _, fs = a.shape
    tm = _row_tile(r)

    def body(hn_ref, dy_ref, a_ref, dg_ref, du_ref, wg_ref, wu_ref, wd_ref):
        @pl.when(pl.program_id(1) == 0)
        def _():
            wg_ref[...] = jnp.zeros_like(wg_ref)
            wu_ref[...] = jnp.zeros_like(wu_ref)
            wd_ref[...] = jnp.zeros_like(wd_ref)

        hn_ = hn_ref[...]
        wg_ref[0] += lax.dot_general(hn_, dg_ref[0], TN, preferred_element_type=F32)
        wu_ref[0] += lax.dot_general(hn_, du_ref[0], TN, preferred_element_type=F32)
        wd_ref[0] += lax.dot_general(a_ref[0], dy_ref[...], TN, preferred_element_type=F32)

    row = pl.BlockSpec((tm, d), lambda s, i: (i, 0))
    act = pl.BlockSpec((1, tm, fs), lambda s, i: (s, i, 0))
    return pl.pallas_call(
        body, name=name,
        out_shape=[jax.ShapeDtypeStruct((ns, d, fs), F32), jax.ShapeDtypeStruct((ns, d, fs), F32),
                   jax.ShapeDtypeStruct((ns, fs, d), F32)],
        grid=(ns, r // tm),
        in_specs=[row, row, act, act, act],
        out_specs=[pl.BlockSpec((1, d, fs), lambda s, i: (s, 0, 0)), pl.BlockSpec((1, d, fs), lambda s, i: (s, 0, 0)),
                   pl.BlockSpec((1, fs, d), lambda s, i: (s, 0, 0))],
        compiler_params=_params(("arbitrary", "arbitrary"), VMEM_BIG),
    )(hn, dy, a, dg, du)


def norm_fwd(x, gamma, name):
    r = x.shape[0]
    w = gamma.shape[1]
    tm = _row_tile(r)

    def body(x_ref, g_ref, o_ref):
        xh, _ = _rms(x_ref[...])
        o_ref[...] = (xh * g_ref[...]).astype(BF16)

    return pl.pallas_call(
        body, name=name, out_shape=jax.ShapeDtypeStruct((r, w), BF16), grid=(r // tm,),
        in_specs=[pl.BlockSpec((tm, w), lambda i: (i, 0)), pl.BlockSpec((1, w), lambda i: (0, 0))],
        out_specs=pl.BlockSpec((tm, w), lambda i: (i, 0)),
        compiler_params=_params(("arbitrary",)),
    )(x, gamma)


def norm_bwd(x, gamma, dy, dres, name):
    r = x.shape[0]
    w = gamma.shape[1]
    tm = _row_tile(r)
    has_res = dres is not None

    def body(*refs):
        if has_res:
            x_ref, g_ref, dy_ref, dr_ref, dx_ref, dgam_ref = refs
        else:
            x_ref, g_ref, dy_ref, dx_ref, dgam_ref = refs

        @pl.when(pl.program_id(0) == 0)
        def _():
            dgam_ref[...] = jnp.zeros_like(dgam_ref)

        xh, rstd = _rms(x_ref[...])
        dy_ = dy_ref[...].astype(F32)
        dgam_ref[...] += jnp.sum(dy_ * xh, axis=0, keepdims=True)
        dx = _rms_bwd(xh, rstd, dy_ * g_ref[...])
        if has_res:
            dx = dx + dr_ref[...]
        dx_ref[...] = dx

    row = pl.BlockSpec((tm, w), lambda i: (i, 0))
    vec = pl.BlockSpec((1, w), lambda i: (0, 0))
    ins = [x, gamma, dy] + ([dres] if has_res else [])
    return pl.pallas_call(
        body, name=name,
        out_shape=[jax.ShapeDtypeStruct((r, w), F32), jax.ShapeDtypeStruct((1, w), F32)],
        grid=(r // tm,), in_specs=[row, vec, row] + ([row] if has_res else []), out_specs=[row, vec],
        compiler_params=_params(("arbitrary",)),
    )(*ins)


def rowmm(a, w, name, *, nt=False, res=None, out_dtype=F32, heads_out=False):
    ha, r, ka = a.shape
    hw = w.shape[0]
    nh = max(ha, hw)
    n = w.shape[1] if nt else w.shape[2]
    tm = _row_tile(r)
    dims = NT if nt else (((1,), (0,)), ((), ()))
    has_res = res is not None

    def body(*refs):
        if has_res:
            a_ref, w_ref, r_ref, o_ref = refs
        else:
            a_ref, w_ref, o_ref = refs
        shared = a_ref[0].astype(BF16) if ha == 1 else None
        acc = None
        for h in range(nh):
            lhs = shared if ha == 1 else a_ref[h].astype(BF16)
            p = lax.dot_general(lhs, w_ref[h if hw > 1 else 0], dims, preferred_element_type=F32)
            if heads_out:
                o_ref[h] = p.astype(out_dtype)
            else:
                acc = p if acc is None else acc + p
        if not heads_out:
            if has_res:
                acc = acc + r_ref[...]
            o_ref[...] = acc.astype(out_dtype)

    in_specs = [pl.BlockSpec((ha, tm, ka), lambda i: (0, i, 0)), pl.BlockSpec(w.shape, lambda i: (0, 0, 0))]
    ins = [a, w]
    if has_res:
        in_specs.append(pl.BlockSpec((tm, n), lambda i: (i, 0)))
        ins.append(res)
    if heads_out:
        out_shape = jax.ShapeDtypeStruct((nh, r, n), out_dtype)
        out_spec = pl.BlockSpec((nh, tm, n), lambda i: (0, i, 0))
    else:
        out_shape = jax.ShapeDtypeStruct((r, n), out_dtype)
        out_spec = pl.BlockSpec((tm, n), lambda i: (i, 0))
    return pl.pallas_call(
        body, name=name, out_shape=out_shape, grid=(r // tm,), in_specs=in_specs, out_specs=out_spec,
        compiler_params=_params(("arbitrary",), VMEM_BIG),
    )(*ins)


def tnmm(a, b, name):
    ha, r, ka = a.shape
    hb, _, nb = b.shape
    nh = max(ha, hb)
    tm = _row_tile(r)

    def body(a_ref, b_ref, o_ref):
        @pl.when(pl.program_id(0) == 0)
        def _():
            o_ref[...] = jnp.zeros_like(o_ref)

        a_shared = a_ref[0].astype(BF16) if ha == 1 else None
        b_shared = b_ref[0].astype(BF16) if hb == 1 else None
        for h in range(nh):
            lhs = a_shared if ha == 1 else a_ref[h].astype(BF16)
            rhs = b_shared if hb == 1 else b_ref[h].astype(BF16)
            o_ref[h] += lax.dot_general(lhs, rhs, TN, preferred_element_type=F32)

    return pl.pallas_call(
        body, name=name, out_shape=jax.ShapeDtypeStruct((nh, ka, nb), F32), grid=(r // tm,),
        in_specs=[pl.BlockSpec((ha, tm, ka), lambda i: (0, i, 0)), pl.BlockSpec((hb, tm, nb), lambda i: (0, i, 0))],
        out_specs=pl.BlockSpec((nh, ka, nb), lambda i: (0, 0, 0)),
        compiler_params=_params(("arbitrary",), VMEM_BIG),
    )(a, b)


def rope_tables(r):
    inv = 1.0 / (ROPE_THETA ** (jnp.arange(0, QK_ROPE, 2, dtype=F32) / QK_ROPE))
    pos = (jnp.arange(r, dtype=F32) - META_ROW0)[:, None]
    ang = pos * inv[None, :]
    cos, sin = jnp.cos(ang), jnp.sin(ang)
    ones = jnp.ones((r, HEAD_W - QK_ROPE), F32)
    ctab = jnp.concatenate([cos, cos, ones], axis=1)
    stab = jnp.concatenate([-sin, sin, jnp.zeros_like(ones)], axis=1)
    return ctab, stab


def _swap_halves(z):
    lane = lax.broadcasted_iota(jnp.int32, z.shape, 1)
    up = pltpu.roll(z, HEAD_W - QK_ROPE // 2, 1)
    down = pltpu.roll(z, QK_ROPE // 2, 1)
    return jnp.where(lane < QK_ROPE // 2, up, jnp.where(lane < QK_ROPE, down, 0.0))


def proj_rope(a, w, ctab, stab, extra, name, scale=1.0):
    r, ka = a.shape
    nh = w.shape[0]
    tm = _row_tile(r)
    has_extra = extra is not None

    def body(*refs):
        if has_extra:
            a_ref, w_ref, c_ref, s_ref, e_ref, o_ref = refs
        else:
            a_ref, w_ref, c_ref, s_ref, o_ref = refs
        a_ = a_ref[...]
        ctab_, stab_ = c_ref[...], s_ref[...]
        if scale != 1.0:
            ctab_, stab_ = ctab_ * scale, stab_ * scale
        for h in range(nh):
            x = jnp.dot(a_, w_ref[h], preferred_element_type=F32)
            if has_extra:
                x = x + e_ref[...]
            o_ref[h] = (x * ctab_ + _swap_halves(x) * stab_).astype(BF16)

    tab = pl.BlockSpec((tm, HEAD_W), lambda i: (i, 0))
    in_specs = [pl.BlockSpec((tm, ka), lambda i: (i, 0)), pl.BlockSpec((nh, ka, HEAD_W), lambda i: (0, 0, 0)),
                tab, tab]
    ins = [a, w, ctab, stab]
    if has_extra:
        in_specs.append(pl.BlockSpec((tm, HEAD_W), lambda i: (i, 2)))
        ins.append(extra)
    return pl.pallas_call(
        body, name=name, out_shape=jax.ShapeDtypeStruct((nh, r, HEAD_W), BF16), grid=(r // tm,),
        in_specs=in_specs, out_specs=pl.BlockSpec((nh, tm, HEAD_W), lambda i: (0, i, 0)),
        compiler_params=_params(("arbitrary",)),
    )(*ins)


def rope_bwd_heads(d, ctab, stab, name, scale=1.0):
    nh, r, _ = d.shape
    tm = _row_tile(r)

    def body(d_ref, c_ref, s_ref, o_ref):
        ctab_, stab_ = c_ref[...], s_ref[...]
        if scale != 1.0:
            ctab_, stab_ = ctab_ * scale, stab_ * scale
        for h in range(nh):
            d_ = d_ref[h]
            o_ref[h] = (d_ * ctab_ + _swap_halves(d_ * stab_)).astype(BF16)

    tab = pl.BlockSpec((tm, HEAD_W), lambda i: (i, 0))
    blk = pl.BlockSpec((nh, tm, HEAD_W), lambda i: (0, i, 0))
    return pl.pallas_call(
        body, name=name, out_shape=jax.ShapeDtypeStruct((nh, r, HEAD_W), BF16), grid=(r // tm,),
        in_specs=[blk, tab, tab], out_specs=blk,
        compiler_params=_params(("arbitrary",)),
    )(d, ctab, stab)


def rope_bwd_sum(d, ctab, stab, name):
    nh, r, _ = d.shape
    tm = _row_tile(r)

    def body(d_ref, c_ref, s_ref, o_ref):
        d_ = d_ref[0]
        for h in range(1, nh):
            d_ = d_ + d_ref[h]
        lane = lax.broadcasted_iota(jnp.int32, d_.shape, 1)
        g = d_ * c_ref[...] + _swap_halves(d_ * s_ref[...])
        o_ref[...] = jnp.where(lane < QK_ROPE, g, 0.0)

    tab = pl.BlockSpec((tm, HEAD_W), lambda i: (i, 0))
    return pl.pallas_call(
        body, name=name, out_shape=jax.ShapeDtypeStruct((r, HEAD_W), F32), grid=(r // tm,),
        in_specs=[pl.BlockSpec((nh, tm, HEAD_W), lambda i: (0, i, 0)), tab, tab], out_specs=tab,
        compiler_params=_params(("arbitrary",)),
    )(d, ctab, stab)


def _attn_tiles(r):
    t = _row_tile(r)
    return t, t


def _mask(q0, k0, nq_, nk_, keys_on_rows=False):
    shape = (nk_, nq_) if keys_on_rows else (nq_, nk_)
    rq = q0 + lax.broadcasted_iota(jnp.int32, shape, 1 if keys_on_rows else 0)
    rk = k0 + lax.broadcasted_iota(jnp.int32, shape, 0 if keys_on_rows else 1)
    return ((rk >> CHUNK_SHIFT) <= (rq >> CHUNK_SHIFT)) & (rk >= META_ROW0)


ATTN_SUB = 640


SM_SCALE = 1.0 / math.sqrt(QK_NOPE + QK_ROPE)
LOG2E = math.log2(math.e)
Q_SCALE = SM_SCALE * LOG2E


def attn_fwd(q, k, v, name):
    nh, r, dk = q.shape
    dv = v.shape[-1]
    tq, tk = _attn_tiles(r)
    nq, nk = r // tq, r // tk

    def last_k(i):
        return ((i + 1) * tq - 1) // tk

    pairs = [(i, j) for i in range(nq) for j in range(last_k(i) + 1)]
    qi_tab = jnp.asarray([p[0] for p in pairs], jnp.int32)
    kj_tab = jnp.asarray([p[1] for p in pairs], jnp.int32)
    sub = _tile(tq, ATTN_SUB)

    def body(qi_ref, kj_ref, q_ref, k_ref, v_ref, o_ref, lse_ref, m_sc, l_sc, acc_sc):
        t = pl.program_id(1)
        i, j = qi_ref[t], kj_ref[t]

        @pl.when(j == 0)
        def _():
            m_sc[...] = jnp.full_like(m_sc, NEG)
            l_sc[...] = jnp.zeros_like(l_sc)
            acc_sc[...] = jnp.zeros_like(acc_sc)

        def step(masked):
            k_, v_ = k_ref[0], v_ref[0]
            for r0 in range(0, tq, sub):
                rows = slice(r0, r0 + sub)
                s = lax.dot_general(q_ref[0, rows, :], k_, NT, preferred_element_type=F32)
                if masked:
                    s = jnp.where(_mask(i * tq + r0, j * tk, sub, tk), s, NEG)
                m_old = m_sc[rows, :]
                m_new = jnp.maximum(m_old, jnp.max(s, axis=-1, keepdims=True))
                alpha = jnp.exp2(m_old - m_new)
                p = jnp.exp2(s - jnp.tile(m_new, (1, tk // LANES)))
                l_sc[rows, :] = alpha * l_sc[rows, :] + jnp.sum(p, axis=-1, keepdims=True)
                acc_sc[rows, :] = (alpha[:, :dv] * acc_sc[rows, :]
                                   + jnp.dot(p.astype(BF16), v_, preferred_element_type=F32))
                m_sc[rows, :] = m_new

        needs_mask = (j == last_k(i)) | (j == 0)
        pl.when(needs_mask)(functools.partial(step, True))
        pl.when(jnp.logical_not(needs_mask))(functools.partial(step, False))

        @pl.when(j == last_k(i))
        def _():
            l = l_sc[...]
            o_ref[0] = (acc_sc[...] / l[:, :dv]).astype(BF16)
            lse_ref[0] = (m_sc[...] + jnp.log2(l))[:, :1]

    qspec = lambda w: pl.BlockSpec((1, tq, w), lambda h, t, qi, kj: (h, qi[t], 0))
    kspec = lambda w: pl.BlockSpec((1, tk, w), lambda h, t, qi, kj: (h, kj[t], 0))
    return pl.pallas_call(
        body, name=name,
        out_shape=[jax.ShapeDtypeStruct((nh, r, dv), BF16), jax.ShapeDtypeStruct((nh, r, 1), F32)],
        grid_spec=pltpu.PrefetchScalarGridSpec(
            num_scalar_prefetch=2, grid=(nh, len(pairs)),
            in_specs=[qspec(dk), kspec(dk), kspec(dv)], out_specs=[qspec(dv), qspec(1)],
            scratch_shapes=[pltpu.VMEM((tq, LANES), F32), pltpu.VMEM((tq, LANES), F32), pltpu.VMEM((tq, dv), F32)]),
        compiler_params=_params(("arbitrary", "arbitrary")),
    )(qi_tab, kj_tab, q, k, v)


def attn_bwd(q, k, v, do, lse, delta, dk_prev, dv_prev, name):
    nh, r, dk = q.shape
    dv = v.shape[-1]
    tq, tk = _attn_tiles(r)
    nq, nk = r // tq, r // tk
    has_prev = dk_prev is not None

    def first_q(j):
        return (j * tk) // tq

    pairs = [(j, i) for j in range(nk) for i in range(first_q(j), nq)]
    kj_tab = jnp.asarray([p[0] for p in pairs], jnp.int32)
    qi_tab = jnp.asarray([p[1] for p in pairs], jnp.int32)
    sub = _tile(tk, ATTN_SUB)

    def body(*refs):
        if has_prev:
            (kj_ref, qi_ref, q_ref, k_ref, v_ref, do_ref, lse_ref, dl_ref, pk_ref, pv_ref,
             dq_ref, dk_ref, dv_ref, dk_sc, dv_sc) = refs
        else:
            (kj_ref, qi_ref, q_ref, k_ref, v_ref, do_ref, lse_ref, dl_ref,
             dq_ref, dk_ref, dv_ref, dk_sc, dv_sc) = refs
        t = pl.program_id(1)
        j, i = kj_ref[t], qi_ref[t]

        @pl.when(t == 0)
        def _():
            dq_ref[...] = jnp.zeros_like(dq_ref)

        @pl.when(i == first_q(j))
        def _():
            dk_sc[...] = jnp.zeros_like(dk_sc)
            dv_sc[...] = jnp.zeros_like(dv_sc)

        def step(masked):
            q_, do_ = q_ref[0], do_ref[0]
            lse_, dl_ = lse_ref[0], dl_ref[0]
            dq_acc = None
            for r0 in range(0, tk, sub):
                rows = slice(r0, r0 + sub)
                k_ = k_ref[0, rows, :]
                st = lax.dot_general(k_, q_, NT, preferred_element_type=F32)
                if masked:
                    st = jnp.where(_mask(i * tq, j * tk + r0, tq, sub, keys_on_rows=True), st, NEG)
                pt = jnp.exp2(st - lse_)
                dv_sc[rows, :] += jnp.dot(pt.astype(BF16), do_, preferred_element_type=F32)
                dpt = lax.dot_general(v_ref[0, rows, :], do_, NT, preferred_element_type=F32)
                dst = (pt * (dpt - dl_)).astype(BF16)
                dk_sc[rows, :] += jnp.dot(dst, q_, preferred_element_type=F32)
                part = lax.dot_general(dst, k_, TN, preferred_element_type=F32)
                dq_acc = part if dq_acc is None else dq_acc + part
            qrows = pl.ds(pl.multiple_of(i * tq, tq), tq)
            dq_ref[0, qrows, :] += dq_acc

        needs_mask = (i == first_q(j)) | (j == 0)
        pl.when(needs_mask)(functools.partial(step, True))
        pl.when(jnp.logical_not(needs_mask))(functools.partial(step, False))

        @pl.when(i == nq - 1)
        def _():
            dk_ = dk_sc[...] * (1.0 / LOG2E)
            dv_ = dv_sc[...]
            if has_prev:
                dk_ = dk_ + pk_ref[0]
                dv_ = dv_ + pv_ref[0]
            dk_ref[0] = dk_
            dv_ref[0] = dv_

    krow = lambda w: pl.BlockSpec((1, tk, w), lambda h, t, kj, qi: (h, kj[t], 0))
    qrow = lambda w: pl.BlockSpec((1, tq, w), lambda h, t, kj, qi: (h, qi[t], 0))
    qvec = pl.BlockSpec((1, 1, tq), lambda h, t, kj, qi: (h, 0, qi[t]))
    in_specs = [qrow(dk), krow(dk), krow(dv), qrow(dv), qvec, qvec]
    ins = [q, k, v, do, lse, delta]
    if has_prev:
        in_specs += [krow(dk), krow(dv)]
        ins += [dk_prev, dv_prev]
    return pl.pallas_call(
        body, name=name,
        out_shape=[jax.ShapeDtypeStruct((nh, r, dk), F32), jax.ShapeDtypeStruct((nh, r, dk), F32),
                   jax.ShapeDtypeStruct((nh, r, dv), F32)],
        grid_spec=pltpu.PrefetchScalarGridSpec(
            num_scalar_prefetch=2, grid=(nh, len(pairs)), in_specs=in_specs,
            out_specs=[pl.BlockSpec((1, r, dk), lambda h, t, kj, qi: (h, 0, 0)), krow(dk), krow(dv)],
            scratch_shapes=[pltpu.VMEM((tk, dk), F32), pltpu.VMEM((tk, dv), F32)]),
        compiler_params=_params(("arbitrary", "arbitrary"), VMEM_BIG),
    )(kj_tab, qi_tab, *ins)


def attn_out_bwd(dattn, wo, o, name):
    r, d = dattn.shape
    nh, dv, _ = wo.shape
    tm = _row_tile(r)

    def body(da_ref, w_ref, o_ref, do_ref, dl_ref):
        da = da_ref[...].astype(BF16)
        for h in range(nh):
            do_ = lax.dot_general(da, w_ref[h], NT, preferred_element_type=F32).astype(BF16)
            do_ref[h] = do_
            dl_ref[h] = jnp.sum(do_.astype(F32) * o_ref[h].astype(F32), axis=-1, keepdims=True)

    return pl.pallas_call(
        body, name=name,
        out_shape=[jax.ShapeDtypeStruct((nh, r, dv), BF16), jax.ShapeDtypeStruct((nh, r, 1), F32)],
        grid=(r // tm,),
        in_specs=[pl.BlockSpec((tm, d), lambda i: (i, 0)), pl.BlockSpec((nh, dv, d), lambda i: (0, 0, 0)),
                  pl.BlockSpec((nh, tm, dv), lambda i: (0, i, 0))],
        out_specs=[pl.BlockSpec((nh, tm, dv), lambda i: (0, i, 0)), pl.BlockSpec((nh, tm, 1), lambda i: (0, i, 0))],
        compiler_params=_params(("arbitrary",)),
    )(dattn, wo, o)


def _pool_counts(row0, n, window):
    rows = row0 + lax.broadcasted_iota(jnp.int32, (n, 1), 0)
    cnt = jnp.clip(rows - META_ROW0 + 1, 1, window)
    return 1.0 / cnt.astype(F32)


def pool_fwd(h, gamma, wp, scale, name):
    r, d = h.shape
    ng, cg, _ = wp.shape
    tm = _row_tile(r)
    hb = tm // HALO

    def body(h_ref, hp_ref, g_ref, w_ref, sc_ref, o_ref):
        i = pl.program_id(0)
        xm = h_ref[...]
        xp = hp_ref[...] * jnp.where(i > 0, 1.0, 0.0)
        xx = jnp.concatenate([xp, xm], axis=0)
        xh, _ = _rms(xx)
        u = xh * g_ref[...]
        for g, win in enumerate(POOL_WINDOWS):
            sl = slice(g * cg, (g + 1) * cg)
            ug = u[:, sl]
            acc, k = ug, 1
            while k < win:
                acc = acc + pltpu.roll(acc, k, 0)
                k *= 2
            pooled = acc[HALO:] * _pool_counts(i * tm, tm, win) - ug[HALO:]
            y = jnp.dot(pooled.astype(BF16), w_ref[g], preferred_element_type=F32)
            o_ref[:, sl] = xm[:, sl] + y * sc_ref[:, sl]

    return pl.pallas_call(
        body, name=name, out_shape=jax.ShapeDtypeStruct((r, d), F32), grid=(r // tm,),
        in_specs=[pl.BlockSpec((tm, d), lambda i: (i, 0)),
                  pl.BlockSpec((HALO, d), lambda i: (jnp.maximum(i * hb - 1, 0), 0)),
                  pl.BlockSpec((1, d), lambda i: (0, 0)), pl.BlockSpec((ng, cg, cg), lambda i: (0, 0, 0)),
                  pl.BlockSpec((1, d), lambda i: (0, 0))],
        out_specs=pl.BlockSpec((tm, d), lambda i: (i, 0)),
        compiler_params=_params(("arbitrary",), VMEM_BIG),
    )(h, h, gamma, wp, scale)


def pool_bwd(h, gamma, wp, scale, dh, name):
    r, d = h.shape
    ng, cg, _ = wp.shape
    tm = _row_tile(r)
    hb = tm // HALO
    nt = r // tm

    def body(h_ref, hp_ref, dh_ref, dn_ref, g_ref, w_ref, sc_ref, dx_ref, dgam_ref, dw_ref, dsc_ref, du_sc):
        i = pl.program_id(0)

        @pl.when(i == 0)
        def _():
            dgam_ref[...] = jnp.zeros_like(dgam_ref)
            dw_ref[...] = jnp.zeros_like(dw_ref)
            dsc_ref[...] = jnp.zeros_like(dsc_ref)

        xm = h_ref[...]
        xp = hp_ref[...] * jnp.where(i > 0, 1.0, 0.0)
        xh_all, rstd_all = _rms(jnp.concatenate([xp, xm], axis=0))
        u = xh_all * g_ref[...]
        dm = dh_ref[...]
        dn = dn_ref[...] * jnp.where(i < nt - 1, 1.0, 0.0)
        dd = jnp.concatenate([dm, dn], axis=0)
        for g, win in enumerate(POOL_WINDOWS):
            sl = slice(g * cg, (g + 1) * cg)
            ug = u[:, sl]
            acc, k = ug, 1
            while k < win:
                acc = acc + pltpu.roll(acc, k, 0)
                k *= 2
            pooled = (acc[HALO:] * _pool_counts(i * tm, tm, win) - ug[HALO:]).astype(BF16)
            y = jnp.dot(pooled, w_ref[g], preferred_element_type=F32)
            dsc_ref[:, sl] += jnp.sum(dm[:, sl] * y, axis=0, keepdims=True)
            dyp = (dd[:, sl] * sc_ref[:, sl]).astype(BF16)
            dw_ref[g] += lax.dot_general(pooled, dyp[:tm], TN, preferred_element_type=F32)
            dpo = lax.dot_general(dyp, w_ref[g], NT, preferred_element_type=F32)
            z = dpo * _pool_counts(i * tm, tm + HALO, win)
            fwd, k = z, 1
            while k < win:
                fwd = fwd + pltpu.roll(fwd, tm + HALO - k, 0)
                k *= 2
            du_sc[:, sl] = fwd[:tm] - dpo[:tm]
        du = du_sc[...]
        xh, rstd = xh_all[HALO:], rstd_all[HALO:]
        dgam_ref[...] += jnp.sum(du * xh, axis=0, keepdims=True)
        dx_ref[...] = dm + _rms_bwd(xh, rstd, du * g_ref[...])

    row = pl.BlockSpec((tm, d), lambda i: (i, 0))
    vec = pl.BlockSpec((1, d), lambda i: (0, 0))
    prev = pl.BlockSpec((HALO, d), lambda i: (jnp.maximum(i * hb - 1, 0), 0))
    nxt = pl.BlockSpec((HALO, d), lambda i: (jnp.minimum((i + 1) * hb, r // HALO - 1), 0))
    wsp = pl.BlockSpec((ng, cg, cg), lambda i: (0, 0, 0))
    return pl.pallas_call(
        body, name=name,
        out_shape=[jax.ShapeDtypeStruct((r, d), F32), jax.ShapeDtypeStruct((1, d), F32),
                   jax.ShapeDtypeStruct((ng, cg, cg), F32), jax.ShapeDtypeStruct((1, d), F32)],
        grid=(nt,), in_specs=[row, prev, row, nxt, vec, wsp, vec], out_specs=[row, vec, wsp, vec],
        scratch_shapes=[pltpu.VMEM((tm, d), F32)],
        compiler_params=_params(("arbitrary",), VMEM_BIG),
    )(h, h, dh, dh, gamma, wp, scale)


def loss_head(h, gamma, target, seq, name):
    r, d = h.shape
    tm = _row_tile(r)

    def body(h_ref, g_ref, t_ref, sse_ref, dh_ref, dgam_ref):
        i = pl.program_id(0)

        @pl.when(i == 0)
        def _():
            sse_ref[...] = jnp.zeros_like(sse_ref)
            dgam_ref[...] = jnp.zeros_like(dgam_ref)

        xh, rstd = _rms(h_ref[...])
        rows = i * tm + lax.broadcasted_iota(jnp.int32, (tm, 1), 0)
        valid = ((rows >= FRONT) & (rows < FRONT + seq)).astype(F32)
        e = (xh * g_ref[...] - t_ref[...]) * valid
        sse_ref[...] += jnp.sum(jnp.sum(e * e, axis=1, keepdims=True), axis=0, keepdims=True)
        dy = e * (1.0 / d)
        dgam_ref[...] += jnp.sum(dy * xh, axis=0, keepdims=True)
        dh_ref[...] = _rms_bwd(xh, rstd, dy * g_ref[...])

    row = pl.BlockSpec((tm, d), lambda i: (i, 0))
    vec = pl.BlockSpec((1, d), lambda i: (0, 0))
    return pl.pallas_call(
        body, name=name,
        out_shape=[jax.ShapeDtypeStruct((1, 1), F32), jax.ShapeDtypeStruct((r, d), F32),
                   jax.ShapeDtypeStruct((1, d), F32)],
        grid=(r // tm,), in_specs=[row, vec, row],
        out_specs=[pl.BlockSpec((1, 1), lambda i: (0, 0)), row, vec],
        compiler_params=_params(("arbitrary",)),
    )(h, gamma, target)


SMALL = ("meta_tokens", "pool_w", "pool_scale", "w_dkv", "w_uk", "w_uv", "w_dq", "w_uq", "w_o")


def _pack(parts):
    flat = jnp.concatenate([p.reshape(-1) for p in parts])
    n = flat.shape[0]
    unit = PACK_W * 2 * SUBLANES
    n_pad = -(-n // unit) * unit
    return jnp.pad(flat, (0, n_pad - n)).reshape(n_pad // PACK_W, PACK_W)


def _unpack(buf, shapes, lead=()):
    flat = buf.reshape(lead + (-1,))
    out, off = [], 0
    for shp in shapes:
        n = math.prod(shp)
        out.append(flat[..., off:off + n].reshape(lead + tuple(shp)))
        off += n
    return out


def _cols_from_shards(a, axis):
    a = jnp.moveaxis(a, 0, axis)
    shp = a.shape
    return a.reshape(shp[:axis] + (shp[axis] * shp[axis + 1],) + shp[axis + 2:])


def _cols_to_shards(a, axis):
    shp = a.shape
    a = a.reshape(shp[:axis] + (N_SHARD, shp[axis] // N_SHARD) + shp[axis + 1:])
    return jnp.moveaxis(a, axis, 0)


SMALL_AXIS = {"meta_tokens": 1, "pool_w": 2, "pool_scale": 1, "w_dkv": 0, "w_uk": 1, "w_uv": 1,
              "w_dq": 1, "w_uq": 2, "w_o": 2}


def kernel(x, meta_tokens, ffn1_norm, ffn1_w_gate, ffn1_w_up, ffn1_w_down, mix_norm, ffn2_norm, ffn2_w_gate, ffn2_w_up, ffn2_w_down, pool_w, pool_scale, kv_in_norm, w_dkv, kv_latent_norm, w_uk, w_uv, w_dq, q_latent_norm, w_uq, w_o, final_norm, loss_target, m_meta_tokens, m_ffn1_norm, m_ffn1_w_gate, m_ffn1_w_up, m_ffn1_w_down, m_mix_norm, m_ffn2_norm, m_ffn2_w_gate, m_ffn2_w_up, m_ffn2_w_down, m_pool_w, m_pool_scale, m_kv_in_norm, m_w_dkv, m_kv_latent_norm, m_w_uk, m_w_uv, m_w_dq, m_q_latent_norm, m_w_uq, m_w_o, m_final_norm, v_meta_tokens, v_ffn1_norm, v_ffn1_w_gate, v_ffn1_w_up, v_ffn1_w_down, v_mix_norm, v_ffn2_norm, v_ffn2_w_gate, v_ffn2_w_up, v_ffn2_w_down, v_pool_w, v_pool_scale, v_kv_in_norm, v_w_dkv, v_kv_latent_norm, v_w_uk, v_w_uv, v_w_dq, v_q_latent_norm, v_w_uq, v_w_o, v_final_norm):
    args = dict(locals())
    W = {n: args[n] for n in NAMES}
    M = {n: args["m_" + n] for n in NAMES}
    V = {n: args["v_" + n] for n in NAMES}

    depth = ffn1_norm.shape[0]
    n_a = pool_w.shape[0]
    seq, d = x.shape[1], x.shape[2]
    nh = N_HEADS
    r = -(-(FRONT + seq) // LANES) * LANES

    cx, cy, cc = lax.axis_index("x"), lax.axis_index("y"), lax.axis_index("c")
    c_arr = jnp.reshape(cc, (1,)).astype(jnp.int32)
    s_arr = jnp.reshape(2 * cx + cy, (1,)).astype(jnp.int32)

    small_shapes = [W[n].shape for n in SMALL]
    gathered = all_gather_shards([_pack([W[n] for n in SMALL])], "ag_small")[0]
    small_full = {}
    for n, part in zip(SMALL, _unpack(gathered, small_shapes, (N_SHARD,))):
        small_full[n] = _cols_from_shards(part, SMALL_AXIS[n])

    ffn_src = {"ffn1": (ffn1_w_gate, ffn1_w_up, ffn1_w_down), "ffn2": (ffn2_w_gate, ffn2_w_up, ffn2_w_down)}
    ffn_order = [(f, l) for l in range(depth) for f in ("ffn1", "ffn2")]
    ffn_w = {}
    gate = [None]
    ag_state = [None]

    def gated(a):
        if gate[0] is not None:
            a = a + gate[0][0, 0].astype(a.dtype)
            gate[0] = None
        return a

    def vec(a):
        return gated(a.reshape(1, -1))

    def ag_start(idx, dep):
        f, l = ffn_order[idx]
        shards = [(w_[l] + dep).astype(BF16) for w_ in ffn_src[f]]
        lands = [lax.dynamic_update_slice(lax.empty((N_SHARD,) + a.shape, BF16), a[None], (2 * cx + cy, 0, 0))
                 for a in shards]
        ag_state[0] = exchange_start(shards, lands, f"ag_start_{f}_{l}", True)
        gate[0] = ag_state[0][4]

    def ag_wait(idx, after):
        f, l = ffn_order[idx]
        _, lands = exchange_wait(ag_state[0], after, f"ag_wait_{f}_{l}", True)
        ffn_w[f, l] = lands
        if idx + 1 < len(ffn_order):
            ag_start(idx + 1, lands[0][0, 0, 0].astype(F32) * 0.0)

    ag_start(0, jnp.zeros((), F32))
    ag_wait(0, gathered)

    meta_full = small_full["meta_tokens"]
    wp = small_full["pool_w"].astype(BF16)
    pscale = small_full["pool_scale"]
    wdkv = jnp.pad(small_full["w_dkv"], ((0, 0), (0, HEAD_W - QK_ROPE))).astype(BF16)[None]
    wuk = small_full["w_uk"].reshape(KV_RANK, nh, QK_NOPE).transpose(1, 0, 2)
    wk_h = jnp.concatenate([jnp.zeros((nh, KV_RANK, HEAD_W - QK_NOPE), F32), wuk], axis=-1).astype(BF16)
    wv_h = small_full["w_uv"].reshape(KV_RANK, nh, V_HEAD).transpose(1, 0, 2).astype(BF16)
    wdq = small_full["w_dq"].astype(BF16)
    wuq = small_full["w_uq"].reshape(-1, Q_RANK, nh, QK_NOPE + QK_ROPE).transpose(0, 2, 1, 3)
    wq_h = jnp.concatenate([wuq[..., QK_NOPE:], jnp.zeros(wuq.shape[:-1] + (HEAD_W - QK_NOPE - QK_ROPE,), F32),
                            wuq[..., :QK_NOPE]], axis=-1).astype(BF16)
    wo_h = small_full["w_o"].reshape(-1, nh, V_HEAD, d).astype(BF16)

    ctab, stab = rope_tables(r)

    h = jnp.concatenate([jnp.zeros((META_ROW0, d), F32), meta_full, x[0],
                         jnp.zeros((r - FRONT - seq, d), F32)], axis=0)
    target = jnp.concatenate([jnp.zeros((FRONT, d), F32), loss_target[0],
                              jnp.zeros((r - FRONT - seq, d), F32)], axis=0)
    saved = []
    kv = None
    for l in range(depth):
        sv = {"h0": h}
        if l > 0:
            ag_wait(2 * l, h)
        h, sv["g1"], sv["u1"] = ffn_fwd(h, vec(ffn1_norm[l]), *ffn_w["ffn1", l], f"ffn1_fwd_{l}")
        sv["h1"] = h
        if l < n_a:
            h = pool_fwd(h, vec(mix_norm[l]), wp[l], vec(pscale[l]), f"pool_fwd_{l}")
        else:
            j = l - n_a
            u = norm_fwd(h, vec(mix_norm[l]), f"mixnorm_{l}")
            cq0 = rowmm(u[None], wdq[j][None], f"dq_{l}")
            cq = norm_fwd(cq0, vec(q_latent_norm[j]), f"qnorm_{l}")
            q = proj_rope(cq, wq_h[j], ctab, stab, None, f"qproj_{l}", scale=Q_SCALE)
            o, lse = attn_fwd(q, kv["k"], kv["v"], f"attn_fwd_{l}")
            h = rowmm(o, wo_h[j], f"oproj_{l}", res=h)
            sv.update(u=u, cq0=cq0, cq=cq, q=q, o=o, lse=lse)
        sv["h2"] = h
        ag_wait(2 * l + 1, h)
        h, sv["g2"], sv["u2"] = ffn_fwd(h, vec(ffn2_norm[l]), *ffn_w["ffn2", l], f"ffn2_fwd_{l}")
        saved.append(sv)
        if l == n_a - 1:
            hkv = norm_fwd(h, vec(kv_in_norm), "kvin_norm")
            ckr = rowmm(hkv[None], wdkv, "dkv")
            ckv = norm_fwd(ckr, vec(kv_latent_norm), "kvlat_norm")
            kv = {"h": h, "hkv": hkv, "ckr": ckr, "ckv": ckv,
                  "k": proj_rope(ckv, wk_h, ctab, stab, ckr, "kproj"),
                  "v": rowmm(ckv[None], wv_h, "vproj", out_dtype=BF16, heads_out=True)}

    sse, dh, dfinal = loss_head(h, vec(final_norm), target, seq, "loss_head")
    loss = lax.psum(0.5 / d * sse[0, 0], ("x", "y", "c"))

    G = {}
    FFN = ("ffn1_w_gate", "ffn1_w_up", "ffn1_w_down", "ffn2_w_gate", "ffn2_w_up", "ffn2_w_down")
    per = {n: [None] * depth for n in FFN}
    pending = []

    def rs_complete(after):
        f, l, state = pending.pop()
        names = [f + "_w_gate", f + "_w_up", f + "_w_down"]
        for n, g_ in zip(names, rs_finish(state, after, s_arr, f"{f}_{l}")):
            per[n][l] = (g_,) + tuple(adamw(W[n][l], g_, M[n][l], V[n][l], f"adamw_{n}_{l}"))

    dnorm = {n: [None] * depth for n in ("ffn1_norm", "mix_norm", "ffn2_norm")}
    dqnorm = [None] * (depth - n_a)
    dpool_w, dpool_scale = [None] * n_a, [None] * n_a
    dwdq, dwq_h, dwo_h = [None] * (depth - n_a), [None] * (depth - n_a), [None] * (depth - n_a)

    def ffn_backward(f, l, h_in, dh_, gg, uu):
        gam = ffn1_norm[l] if f == "ffn1" else ffn2_norm[l]
        wg_, wu_, wd_ = ffn_w[f, l]
        dh_in, dgam, hn, dy, dg, du, a = ffn_bwd_act(h_in, vec(gam), dh_, gg, uu, wg_, wu_, wd_, f"{f}_bwd_act_{l}")
        dwg, dwu, dwd = ffn_bwd_weights(hn, dy, a, dg, du, f"{f}_bwd_w_{l}")
        state = rs_begin([dwg, dwu, dwd], c_arr, f"{f}_{l}")
        gate[0] = state[4]
        if pending:
            rs_complete(state[4])
        pending.append((f, l, state))
        dnorm[f + "_norm"][l] = dgam[0]
        return dh_in

    dk_tot = dv_tot = None
    for l in reversed(range(depth)):
        sv = saved[l]
        if l == n_a - 1:
            dckv = rowmm(dk_tot, gated(wk_h), "kproj_bwd", nt=True)
            dkr = rope_bwd_sum(dk_tot, ctab, stab, "kproj_bwd_rope")
            dckv = rowmm(dv_tot, wv_h, "vproj_bwd", nt=True, res=dckv)
            dwk_h = tnmm(kv["ckv"][None], dk_tot, "kproj_bwd_w")
            dwv_h = tnmm(kv["ckv"][None], dv_tot, "vproj_bwd_w")
            dlat, dkvlat = norm_bwd(kv["ckr"], vec(kv_latent_norm), dckv, None, "kvlat_norm_bwd")
            dckr = jnp.concatenate([dlat, dkr], axis=1).astype(BF16)
            dhkv = rowmm(dckr[None], wdkv, "dkv_bwd", nt=True)
            dwdkv = tnmm(kv["hkv"][None], dckr[None], "dkv_bwd_w")[0]
            dh, dkvin = norm_bwd(kv["h"], vec(kv_in_norm), dhkv, dh, "kvin_norm_bwd")
            G["w_dkv"] = dwdkv[:, :KV_RANK + QK_ROPE]
            G["w_uk"] = dwk_h[..., HEAD_W - QK_NOPE:].transpose(1, 0, 2).reshape(KV_RANK, nh * QK_NOPE)
            G["w_uv"] = dwv_h.transpose(1, 0, 2).reshape(KV_RANK, nh * V_HEAD)
        dh = ffn_backward("ffn2", l, sv["h2"], dh, sv["g2"], sv["u2"])
        if l < n_a:
            dh, dmix, dpool_w[l], dps = pool_bwd(sv["h1"], vec(mix_norm[l]), wp[l], vec(pscale[l]), dh, f"pool_bwd_{l}")
            dnorm["mix_norm"][l] = dmix[0]
            dpool_scale[l] = dps[0]
        else:
            j = l - n_a
            do, delta = attn_out_bwd(dh, gated(wo_h[j]), sv["o"], f"oproj_bwd_{l}")
            dwo_h[j] = tnmm(sv["o"], dh[None], f"oproj_bwd_w_{l}")
            dq, dk_tot, dv_tot = attn_bwd(sv["q"], kv["k"], kv["v"], do, sv["lse"].reshape(nh, 1, r),
                                          delta.reshape(nh, 1, r), dk_tot, dv_tot, f"attn_bwd_{l}")
            dxq = rope_bwd_heads(dq, ctab, stab, f"qproj_bwd_rope_{l}", scale=SM_SCALE)
            dcq = rowmm(dxq, wq_h[j], f"qproj_bwd_{l}", nt=True)
            dwq_h[j] = tnmm(sv["cq"][None], dxq, f"qproj_bwd_w_{l}")
            dcq0, dqn = norm_bwd(sv["cq0"], vec(q_latent_norm[j]), dcq, None, f"qnorm_bwd_{l}")
            dqnorm[j] = dqn[0]
            dcq0b = dcq0.astype(BF16)
            du = rowmm(dcq0b[None], wdq[j][None], f"dq_bwd_{l}", nt=True)
            dwdq[j] = tnmm(sv["u"][None], dcq0b[None], f"dq_bwd_w_{l}")[0]
            dh, dmix = norm_bwd(sv["h1"], vec(mix_norm[l]), du, dh, f"mixnorm_bwd_{l}")
            dnorm["mix_norm"][l] = dmix[0]
        dh = ffn_backward("ffn1", l, sv["h0"], dh, sv["g1"], sv["u1"])

    grad_x = dh[FRONT:FRONT + seq][None]
    G["meta_tokens"] = dh[META_ROW0:FRONT]
    G["pool_w"] = jnp.stack(dpool_w)
    G["pool_scale"] = jnp.stack(dpool_scale)
    G["w_dq"] = jnp.stack(dwdq)
    dwq = jnp.stack(dwq_h)
    dwq = jnp.concatenate([dwq[..., HEAD_W - QK_NOPE:], dwq[..., :QK_ROPE]], axis=-1)
    G["w_uq"] = dwq.transpose(0, 2, 1, 3).reshape(-1, Q_RANK, nh * (QK_NOPE + QK_ROPE))
    G["w_o"] = jnp.stack(dwo_h).reshape(-1, nh * V_HEAD, d)

    REPL = ("ffn1_norm", "mix_norm", "ffn2_norm", "kv_in_norm", "kv_latent_norm", "q_latent_norm", "final_norm")
    grep = {"ffn1_norm": jnp.stack(dnorm["ffn1_norm"]), "mix_norm": jnp.stack(dnorm["mix_norm"]),
            "ffn2_norm": jnp.stack(dnorm["ffn2_norm"]), "kv_in_norm": dkvin[0], "kv_latent_norm": dkvlat[0],
            "q_latent_norm": jnp.stack(dqnorm), "final_norm": dfinal[0]}

    def pack128(parts):
        flat = jnp.concatenate([p.reshape(-1) for p in parts])
        n = flat.shape[0]
        n_pad = -(-n // (LANES * SUBLANES)) * (LANES * SUBLANES)
        return jnp.pad(flat, (0, n_pad - n)).reshape(-1, LANES)

    rep_shapes = [W[n].shape for n in REPL]
    g_rep = all_reduce_small(pack128([grep[n] for n in REPL]), "ar_repl")
    d_rep, m_rep, v_rep = adamw(pack128([W[n] for n in REPL]), g_rep, pack128([M[n] for n in REPL]),
                                pack128([V[n] for n in REPL]), "adamw_repl")
    out_g, out_d, out_m, out_v = {}, {}, {}, {}
    for dst, buf in ((out_g, g_rep), (out_d, d_rep), (out_m, m_rep), (out_v, v_rep)):
        for n, a in zip(REPL, _unpack(buf, rep_shapes)):
            dst[n] = a

    g_small = jnp.stack([_pack([_cols_to_shards(G[n], SMALL_AXIS[n])[s] for n in SMALL]) for s in range(N_SHARD)])
    g_small = reduce_scatter([g_small], c_arr, s_arr, "small")[0]
    d_s, m_s, v_s = adamw(_pack([W[n] for n in SMALL]), g_small, _pack([M[n] for n in SMALL]),
                          _pack([V[n] for n in SMALL]), "adamw_small")
    for dst, buf in ((out_g, g_small), (out_d, d_s), (out_m, m_s), (out_v, v_s)):
        for n, a in zip(SMALL, _unpack(buf, small_shapes)):
            dst[n] = a

    rs_complete(g_small)
    for n in FFN:
        out_g[n], out_d[n], out_m[n], out_v[n] = (jnp.stack([t[k] for t in per[n]]) for k in range(4))

    return (loss, grad_x, *[out_g[n] for n in NAMES], *[out_d[n] for n in NAMES],
            *[out_m[n] for n in NAMES], *[out_v[n] for n in NAMES])


NAMES = ("meta_tokens", "ffn1_norm", "ffn1_w_gate", "ffn1_w_up", "ffn1_w_down", "mix_norm", "ffn2_norm",
         "ffn2_w_gate", "ffn2_w_up", "ffn2_w_down", "pool_w", "pool_scale", "kv_in_norm", "w_dkv",
         "kv_latent_norm", "w_uk", "w_uv", "w_dq", "q_latent_norm", "w_uq", "w_o", "final_norm")
```

```python
import functools
import math

import jax
import jax.numpy as jnp
from jax import lax
from jax.experimental import pallas as pl
from jax.experimental.pallas import tpu as pltpu

F32 = jnp.float32
BF16 = jnp.bfloat16
MESH = pl.DeviceIdType.MESH
ANY = pl.BlockSpec(memory_space=pl.ANY)

EPS = 1e-6
CHUNK = 64
CHUNK_SHIFT = 6
N_META = 16
FRONT = 64
META_ROW0 = FRONT - N_META
POOL_WINDOWS = (2, 4, 8, 16)
HALO = 16
N_HEADS = 8
QK_NOPE = 64
QK_ROPE = 32
V_HEAD = 64
HEAD_W = 128
KV_RANK = 256
Q_RANK = 384
ROPE_THETA = 10000.0
NEG = -1e30
N_SHARD = 4
LANES = 128
SUBLANES = 8
PACK_W = 512
VMEM_BIG = 52 * 1024 * 1024
VMEM_MAX = 60 * 1024 * 1024
WGRAD_ROWS = 1664

ADAM_LR = 0.001
ADAM_B1 = 0.9
ADAM_B2 = 0.999
ADAM_EPS = 1e-08
ADAM_WD = 0.01
ADAM_STEP = 10

NT = (((1,), (1,)), ((), ()))
TN = (((0,), (0,)), ((), ()))


def _params(sem=None, vmem=None):
    return pltpu.CompilerParams(dimension_semantics=sem, vmem_limit_bytes=vmem)


def _tile(n, pref, mult=SUBLANES):
    best = None
    for t in range(mult, min(n, pref) + 1, mult):
        if n % t == 0:
            best = t
    return best if best is not None else n


def _row_tile(r):
    return 640 if r % 640 == 0 else 128


def _rms(x):
    rstd = lax.rsqrt(jnp.mean(x * x, axis=-1, keepdims=True) + EPS)
    return x * rstd, rstd


def _rms_bwd(xh, rstd, dxh):
    return rstd * (dxh - xh * jnp.mean(dxh * xh, axis=-1, keepdims=True))


def _sigmoid(x):
    return 1.0 / (1.0 + jnp.exp(-x))


def _place():
    x, y, c = lax.axis_index("x"), lax.axis_index("y"), lax.axis_index("c")
    chips = [(1 - x, y), (x, 1 - y), (1 - x, 1 - y)]
    return x, y, c, chips


def all_gather_shards(shards, name):
    n = len(shards)

    def body(*refs):
        ins, outs = refs[:n], refs[n:2 * n]
        send1, recv1, send2, recv2, lsem = refs[2 * n:]
        x, y, c, chips = _place()
        s = 2 * x + y
        sib = (x, y, 1 - c)

        def rcopy(k, j, src, dst, to, first):
            return pltpu.make_async_remote_copy(
                src_ref=src, dst_ref=dst,
                send_sem=(send1 if first else send2).at[k, j],
                recv_sem=(recv1 if first else recv2).at[k, j],
                device_id=to, device_id_type=MESH)

        started, local = [], []
        for k in range(n):
            hf = ins[k].shape[0] // 2
            cp = pltpu.make_async_copy(ins[k], outs[k].at[s], lsem.at[k])
            cp.start()
            local.append(cp)
            for j, (cx, cy) in enumerate(chips):
                r = rcopy(k, j, ins[k].at[pl.ds(c * hf, hf)], outs[k].at[s, pl.ds(c * hf, hf)],
                          (cx, cy, c), True)
                r.start()
                started.append(r)
        for k in range(n):
            hf = ins[k].shape[0] // 2
            for j, (cx, cy) in enumerate(chips):
                blk = outs[k].at[2 * cx + cy, pl.ds(c * hf, hf)]
                rcopy(k, j, blk, blk, (cx, cy, c), True).wait_recv()
                f = rcopy(k, j, blk, blk, sib, False)
                f.start()
                started.append(f)
        for k in range(n):
            hf = ins[k].shape[0] // 2
            for j, (cx, cy) in enumerate(chips):
                blk = outs[k].at[2 * cx + cy, pl.ds((1 - c) * hf, hf)]
                rcopy(k, j, blk, blk, sib, False).wait_recv()
        for r in started:
            r.wait_send()
        for cp in local:
            cp.wait()

    return pl.pallas_call(
        body, name=name,
        out_shape=[jax.ShapeDtypeStruct((N_SHARD,) + a.shape, a.dtype) for a in shards],
        in_specs=[ANY] * n, out_specs=[ANY] * n,
        scratch_shapes=[pltpu.SemaphoreType.DMA((n, 3))] * 4 + [pltpu.SemaphoreType.DMA((n,))],
    )(*shards)


def sibling_swap_halves(gs, name):
    n = len(gs)

    def body(*refs):
        ins, outs = refs[:n], refs[n:2 * n]
        send, recv = refs[2 * n:]
        x, y, c, _ = _place()
        sib = (x, y, 1 - c)
        cps = []
        for k in range(n):
            hf = ins[k].shape[1] // 2
            r = pltpu.make_async_remote_copy(
                src_ref=ins[k].at[:, pl.ds((1 - c) * hf, hf)], dst_ref=outs[k],
                send_sem=send.at[k], recv_sem=recv.at[k], device_id=sib, device_id_type=MESH)
            r.start()
            cps.append(r)
        for r in cps:
            r.wait_recv()
        for r in cps:
            r.wait_send()

    return pl.pallas_call(
        body, name=name,
        out_shape=[jax.ShapeDtypeStruct((a.shape[0], a.shape[1] // 2, a.shape[2]), a.dtype) for a in gs],
        in_specs=[ANY] * n, out_specs=[ANY] * n,
        scratch_shapes=[pltpu.SemaphoreType.DMA((n,))] * 2,
    )(*gs)


def chip_exchange(ps, name):
    n = len(ps)

    def body(*refs):
        ins, outs = refs[:n], refs[n:2 * n]
        send, recv = refs[2 * n:]
        x, y, c, chips = _place()
        cps = []
        for k in range(n):
            for j, (cx, cy) in enumerate(chips):
                r = pltpu.make_async_remote_copy(
                    src_ref=ins[k].at[2 * cx + cy], dst_ref=outs[k].at[j],
                    send_sem=send.at[k, j], recv_sem=recv.at[k, j],
                    device_id=(cx, cy, c), device_id_type=MESH)
                r.start()
                cps.append(r)
        for r in cps:
            r.wait_recv()
        for r in cps:
            r.wait_send()

    return pl.pallas_call(
        body, name=name,
        out_shape=[jax.ShapeDtypeStruct((3,) + a.shape[1:], a.dtype) for a in ps],
        in_specs=[ANY] * n, out_specs=[ANY] * n,
        scratch_shapes=[pltpu.SemaphoreType.DMA((n, 3))] * 2,
    )(*ps)


def sibling_join_halves(ts, name):
    n = len(ts)

    def body(*refs):
        ins, outs = refs[:n], refs[n:2 * n]
        send, recv, lsem = refs[2 * n:]
        x, y, c, _ = _place()
        sib = (x, y, 1 - c)
        cps, local = [], []
        for k in range(n):
            hf = ins[k].shape[0]
            cp = pltpu.make_async_copy(ins[k], outs[k].at[pl.ds(c * hf, hf)], lsem.at[k])
            cp.start()
            local.append(cp)
            r = pltpu.make_async_remote_copy(
                src_ref=ins[k], dst_ref=outs[k].at[pl.ds(c * hf, hf)],
                send_sem=send.at[k], recv_sem=recv.at[k], device_id=sib, device_id_type=MESH)
            r.start()
            cps.append(r)
        for k in range(n):
            hf = ins[k].shape[0]
            pltpu.make_async_remote_copy(
                src_ref=ins[k], dst_ref=outs[k].at[pl.ds((1 - c) * hf, hf)],
                send_sem=send.at[k], recv_sem=recv.at[k], device_id=sib, device_id_type=MESH).wait_recv()
        for r in cps:
            r.wait_send()
        for cp in local:
            cp.wait()

    return pl.pallas_call(
        body, name=name,
        out_shape=[jax.ShapeDtypeStruct((2 * a.shape[0], a.shape[1]), a.dtype) for a in ts],
        in_specs=[ANY] * n, out_specs=[ANY] * n,
        scratch_shapes=[pltpu.SemaphoreType.DMA((n,))] * 3,
    )(*ts)


def all_reduce_small(part, name):
    m, w = part.shape

    def body(x_ref, tot_ref, gat_ref, send_sems, recv_sems):
        x, y, c, chips = _place()
        me, sib = (x, y, c), (x, y, 1 - c)

        def slot(px, py, pc):
            return gat_ref.at[4 * px + 2 * py + pc]

        def copy(k, block, to, src=None):
            return pltpu.make_async_remote_copy(
                src_ref=slot(*block) if src is None else src, dst_ref=slot(*block),
                send_sem=send_sems.at[k], recv_sem=recv_sems.at[k], device_id=to, device_id_type=MESH)

        gat_ref[4 * x + 2 * y + c] = x_ref[...]
        first = [copy(0, me, sib, src=x_ref)]
        first += [copy(1 + j, me, (*chip, c), src=x_ref) for j, chip in enumerate(chips)]
        for cp in first:
            cp.start()
        passed = [copy(4 + j, (*chip, c), sib) for j, chip in enumerate(chips)]
        for j, chip in enumerate(chips):
            copy(1 + j, (*chip, c), me).wait_recv()
            passed[j].start()
        copy(0, sib, me).wait_recv()
        for j, chip in enumerate(chips):
            copy(4 + j, (*chip, 1 - c), me).wait_recv()
        for cp in first + passed:
            cp.wait_send()
        tot = gat_ref[0]
        for d in range(1, 8):
            tot = tot + gat_ref[d]
        tot_ref[...] = tot

    return pl.pallas_call(
        body, name=name,
        out_shape=jax.ShapeDtypeStruct((m, w), F32),
        in_specs=[pl.BlockSpec(memory_space=pltpu.VMEM)],
        out_specs=pl.BlockSpec(memory_space=pltpu.VMEM),
        scratch_shapes=[pltpu.VMEM((8, m, w), F32), pltpu.SemaphoreType.DMA((7,)), pltpu.SemaphoreType.DMA((7,))],
    )(part)


def add_own_half(g, rec, c_arr, name):
    _, a, b = g.shape
    hf = a // 2
    tb = _tile(hf, 256)
    nb = hf // tb

    def body(c_ref, g_ref, r_ref, o_ref):
        o_ref[...] = g_ref[...] + r_ref[...]

    return pl.pallas_call(
        body, name=name,
        out_shape=jax.ShapeDtypeStruct((N_SHARD, hf, b), F32),
        grid_spec=pltpu.PrefetchScalarGridSpec(
            num_scalar_prefetch=1, grid=(N_SHARD, nb),
            in_specs=[pl.BlockSpec((1, tb, b), lambda s, i, c: (s, c[0] * nb + i, 0)),
                      pl.BlockSpec((1, tb, b), lambda s, i, c: (s, i, 0))],
            out_specs=pl.BlockSpec((1, tb, b), lambda s, i, c: (s, i, 0))),
        compiler_params=_params(("arbitrary", "arbitrary")),
    )(c_arr, g, rec)


def sum_four(p, rec, s_arr, name):
    _, ah, b = p.shape
    tb = _tile(ah, 256)

    def body(s_ref, p_ref, r0, r1, r2, o_ref):
        o_ref[...] = ((p_ref[0] + r0[0]) + r1[0]) + r2[0]

    def rspec(j):
        return pl.BlockSpec((1, tb, b), lambda i, s: (j, i, 0))

    return pl.pallas_call(
        body, name=name,
        out_shape=jax.ShapeDtypeStruct((ah, b), F32),
        grid_spec=pltpu.PrefetchScalarGridSpec(
            num_scalar_prefetch=1, grid=(ah // tb,),
            in_specs=[pl.BlockSpec((1, tb, b), lambda i, s: (s[0], i, 0)), rspec(0), rspec(1), rspec(2)],
            out_specs=pl.BlockSpec((tb, b), lambda i, s: (i, 0))),
        compiler_params=_params(("arbitrary",)),
    )(s_arr, p, rec, rec, rec)


def adamw(w, g, m, v, name):
    a, b = w.shape
    tb = _tile(a, 256)
    c1 = 1.0 - ADAM_B1 ** ADAM_STEP
    c2 = 1.0 - ADAM_B2 ** ADAM_STEP

    def body(w_ref, g_ref, m_ref, v_ref, d_ref, mo_ref, vo_ref):
        g_ = g_ref[...]
        m_ = ADAM_B1 * m_ref[...] + (1.0 - ADAM_B1) * g_
        v_ = ADAM_B2 * v_ref[...] + (1.0 - ADAM_B2) * (g_ * g_)
        m_hat = m_ / c1
        v_hat = v_ / c2
        d_ref[...] = -ADAM_LR * (m_hat / (jnp.sqrt(v_hat) + ADAM_EPS) + ADAM_WD * w_ref[...])
        mo_ref[...] = m_
        vo_ref[...] = v_

    spec = pl.BlockSpec((tb, b), lambda i: (i, 0))
    return pl.pallas_call(
        body, name=name,
        out_shape=[jax.ShapeDtypeStruct((a, b), F32)] * 3,
        grid=(a // tb,), in_specs=[spec] * 4, out_specs=[spec] * 3,
        compiler_params=_params(("arbitrary",)),
    )(w, g, m, v)


def adamw_into(w_all, g, m_all, v_all, prev, l, name):
    nl, a, b = w_all.shape
    tb = _tile(a, 256)
    c1 = 1.0 - ADAM_B1 ** ADAM_STEP
    c2 = 1.0 - ADAM_B2 ** ADAM_STEP

    def body(w_ref, g_ref, m_ref, v_ref, p0, p1, p2, p3, go_ref, d_ref, mo_ref, vo_ref):
        g_ = g_ref[...]
        m_ = ADAM_B1 * m_ref[0] + (1.0 - ADAM_B1) * g_
        v_ = ADAM_B2 * v_ref[0] + (1.0 - ADAM_B2) * (g_ * g_)
        m_hat = m_ / c1
        v_hat = v_ / c2
        go_ref[0] = g_
        d_ref[0] = -ADAM_LR * (m_hat / (jnp.sqrt(v_hat) + ADAM_EPS) + ADAM_WD * w_ref[0])
        mo_ref[0] = m_
        vo_ref[0] = v_

    lay = pl.BlockSpec((1, tb, b), lambda i: (l, i, 0))
    return pl.pallas_call(
        body, name=name,
        out_shape=[jax.ShapeDtypeStruct((nl, a, b), F32)] * 4,
        grid=(a // tb,), in_specs=[lay, pl.BlockSpec((tb, b), lambda i: (i, 0)), lay, lay] + [ANY] * 4,
        out_specs=[lay] * 4, input_output_aliases={4: 0, 5: 1, 6: 2, 7: 3},
        compiler_params=_params(("arbitrary",)),
    )(w_all, g, m_all, v_all, *prev)


def reduce_scatter(gs, c_arr, s_arr, tag):
    rec_a = sibling_swap_halves(gs, f"rs_swap_{tag}")
    ps = [add_own_half(g, r, c_arr, f"rs_add_{tag}_{k}") for k, (g, r) in enumerate(zip(gs, rec_a))]
    rec_b = chip_exchange(ps, f"rs_ici_{tag}")
    ts = [sum_four(p, r, s_arr, f"rs_sum_{tag}_{k}") for k, (p, r) in enumerate(zip(ps, rec_b))]
    return sibling_join_halves(ts, f"rs_join_{tag}")


HBM_SPEC = pl.BlockSpec(memory_space=pltpu.HBM)
SEM_SPEC = pl.BlockSpec(memory_space=pltpu.SEMAPHORE)
EFFECT = pltpu.SideEffectType.DATAFLOW_SIDE_EFFECTING


def _in_hbm(a):
    return pltpu.with_memory_space_constraint(a, pltpu.HBM)


def _exchange_copy(k, j, chip, c, s, srcs, lands, send, recv, gather, receiving):
    cx, cy = chip
    src = srcs[k] if gather else srcs[k].at[2 * cx + cy]
    if gather:
        dst = lands[k].at[2 * cx + cy] if receiving else lands[k].at[s]
    else:
        dst = lands[k].at[j]
    return pltpu.make_async_remote_copy(src_ref=src, dst_ref=dst, send_sem=send.at[3 * k + j], recv_sem=recv.at[3 * k + j],
                                        device_id=(cx, cy, c), device_id_type=MESH)


def exchange_start(srcs, lands, name, gather):
    n = len(srcs)

    def body(*refs):
        srcs_in, lands_in = refs[:n], refs[n:2 * n]
        send, recv = refs[2 * n], refs[2 * n + 1]
        token = refs[-1]
        x, y, c, chips = _place()
        for k in range(n):
            for j, chip in enumerate(chips):
                _exchange_copy(k, j, chip, c, 2 * x + y, srcs_in, lands_in, send, recv, gather, False).start()
        token[...] = jnp.zeros_like(token)

    outs = pl.pallas_call(
        body, name=name,
        out_shape=(pltpu.SemaphoreType.DMA((3 * n,)), pltpu.SemaphoreType.DMA((3 * n,)),
                   *[pltpu.HBM(a.shape, a.dtype) for a in srcs], *[pltpu.HBM(a.shape, a.dtype) for a in lands],
                   jax.ShapeDtypeStruct((SUBLANES, LANES), F32)),
        in_specs=[HBM_SPEC] * (2 * n),
        out_specs=(SEM_SPEC, SEM_SPEC, *[HBM_SPEC] * (2 * n), pl.BlockSpec(memory_space=pltpu.VMEM)),
        input_output_aliases={k: 2 + k for k in range(2 * n)},
        compiler_params=pltpu.CompilerParams(has_side_effects=EFFECT),
    )(*[_in_hbm(a) for a in srcs], *[_in_hbm(a) for a in lands])
    return outs[0], outs[1], list(outs[2:2 + n]), list(outs[2 + n:2 + 2 * n]), outs[-1]


def exchange_wait(state, after, name, gather):
    send, recv, srcs, lands, _ = state
    n = len(srcs)

    def body(*refs):
        srcs_in, lands_in = refs[:n], refs[n:2 * n]
        send_, recv_ = refs[2 * n], refs[2 * n + 1]
        x, y, c, chips = _place()
        for k in range(n):
            for j, chip in enumerate(chips):
                cp = _exchange_copy(k, j, chip, c, 2 * x + y, srcs_in, lands_in, send_, recv_, gather, True)
                cp.wait_send()
                cp.wait_recv()

    outs = pl.pallas_call(
        body, name=name,
        out_shape=tuple(pltpu.HBM(a.shape, a.dtype) for a in srcs + lands),
        in_specs=[HBM_SPEC] * (2 * n) + [SEM_SPEC, SEM_SPEC, ANY],
        out_specs=tuple([HBM_SPEC] * (2 * n)),
        input_output_aliases={k: k for k in range(2 * n)},
        compiler_params=pltpu.CompilerParams(has_side_effects=EFFECT),
    )(*srcs, *lands, send, recv, after)
    return list(outs[:n]), list(outs[n:])


def rs_begin(gs, c_arr, tag):
    rec_a = sibling_swap_halves(gs, f"rs_swap_{tag}")
    ps = [add_own_half(g, r, c_arr, f"rs_add_{tag}_{k}") for k, (g, r) in enumerate(zip(gs, rec_a))]
    lands = [lax.empty((3,) + p.shape[1:], p.dtype) for p in ps]
    return exchange_start(ps, lands, f"rs_ici_start_{tag}", False)


def rs_finish(state, after, s_arr, tag):
    ps, rec_b = exchange_wait(state, after, f"rs_ici_wait_{tag}", False)
    ts = [sum_four(p, r, s_arr, f"rs_sum_{tag}_{k}") for k, (p, r) in enumerate(zip(ps, rec_b))]
    return sibling_join_halves(ts, f"rs_join_{tag}")


def ffn_fwd(h, gamma, wg, wu, wd, name):
    r, d = h.shape
    ns, _, fs = wg.shape
    tm = _row_tile(r)

    def body(h_ref, g_ref, wg_ref, wu_ref, wd_ref, ho_ref, gg_ref, uu_ref, hn_sc, acc_sc):
        s = pl.program_id(1)

        @pl.when(s == 0)
        def _():
            xh, _ = _rms(h_ref[...])
            hn_sc[...] = (xh * g_ref[...]).astype(BF16)
            acc_sc[...] = jnp.zeros_like(acc_sc)

        hn = hn_sc[...]
        g = jnp.dot(hn, wg_ref[0], preferred_element_type=F32)
        u = jnp.dot(hn, wu_ref[0], preferred_element_type=F32)
        gg_ref[0] = g.astype(BF16)
        uu_ref[0] = u.astype(BF16)
        a = (g * _sigmoid(g) * u).astype(BF16)
        acc_sc[...] += jnp.dot(a, wd_ref[0], preferred_element_type=F32)

        @pl.when(s == ns - 1)
        def _():
            ho_ref[...] = h_ref[...] + 0.5 * acc_sc[...]

    return pl.pallas_call(
        body, name=name,
        out_shape=[jax.ShapeDtypeStruct((r, d), F32), jax.ShapeDtypeStruct((ns, r, fs), BF16),
                   jax.ShapeDtypeStruct((ns, r, fs), BF16)],
        grid=(r // tm, ns),
        in_specs=[pl.BlockSpec((tm, d), lambda i, s: (i, 0)), pl.BlockSpec((1, d), lambda i, s: (0, 0)),
                  pl.BlockSpec((1, d, fs), lambda i, s: (s, 0, 0)), pl.BlockSpec((1, d, fs), lambda i, s: (s, 0, 0)),
                  pl.BlockSpec((1, fs, d), lambda i, s: (s, 0, 0))],
        out_specs=[pl.BlockSpec((tm, d), lambda i, s: (i, 0)), pl.BlockSpec((1, tm, fs), lambda i, s: (s, i, 0)),
                   pl.BlockSpec((1, tm, fs), lambda i, s: (s, i, 0))],
        scratch_shapes=[pltpu.VMEM((tm, d), BF16), pltpu.VMEM((tm, d), F32)],
        compiler_params=_params(("arbitrary", "arbitrary"), VMEM_BIG),
    )(h, gamma, wg, wu, wd)


def ffn_bwd_act(h, gamma, dh, gg, uu, wg, wu, wd, name):
    r, d = h.shape
    ns, _, fs = wg.shape
    tm = _row_tile(r)

    def body(h_ref, g_ref, dh_ref, gg_ref, uu_ref, wg_ref, wu_ref, wd_ref,
             dho_ref, dgam_ref, hn_ref, dy_ref, dg_ref, du_ref, a_ref, acc_sc):
        i, s = pl.program_id(0), pl.program_id(1)

        @pl.when(s == 0)
        def _():
            xh, _ = _rms(h_ref[...])
            hn_ref[...] = (xh * g_ref[...]).astype(BF16)
            dy_ref[...] = (0.5 * dh_ref[...]).astype(BF16)
            acc_sc[...] = jnp.zeros_like(acc_sc)

        @pl.when((i == 0) & (s == 0))
        def _():
            dgam_ref[...] = jnp.zeros_like(dgam_ref)

        g = gg_ref[0].astype(F32)
        u = uu_ref[0].astype(F32)
        da = lax.dot_general(dy_ref[...], wd_ref[0], NT, preferred_element_type=F32)
        sig = _sigmoid(g)
        sl = g * sig
        a_ref[0] = (sl * u).astype(BF16)
        du = (da * sl).astype(BF16)
        dg = (da * u * (sig * (1.0 + g * (1.0 - sig)))).astype(BF16)
        dg_ref[0] = dg
        du_ref[0] = du
        acc_sc[...] += (lax.dot_general(dg, wg_ref[0], NT, preferred_element_type=F32)
                        + lax.dot_general(du, wu_ref[0], NT, preferred_element_type=F32))

        @pl.when(s == ns - 1)
        def _():
            xh, rstd = _rms(h_ref[...])
            dhn = acc_sc[...]
            dgam_ref[...] += jnp.sum(dhn * xh, axis=0, keepdims=True)
            dho_ref[...] = dh_ref[...] + _rms_bwd(xh, rstd, dhn * g_ref[...])

    row = pl.BlockSpec((tm, d), lambda i, s: (i, 0))
    act = pl.BlockSpec((1, tm, fs), lambda i, s: (s, i, 0))
    return pl.pallas_call(
        body, name=name,
        out_shape=[jax.ShapeDtypeStruct((r, d), F32), jax.ShapeDtypeStruct((1, d), F32),
                   jax.ShapeDtypeStruct((r, d), BF16), jax.ShapeDtypeStruct((r, d), BF16),
                   jax.ShapeDtypeStruct((ns, r, fs), BF16), jax.ShapeDtypeStruct((ns, r, fs), BF16),
                   jax.ShapeDtypeStruct((ns, r, fs), BF16)],
        grid=(r // tm, ns),
        in_specs=[row, pl.BlockSpec((1, d), lambda i, s: (0, 0)), row, act, act,
                  pl.BlockSpec((1, d, fs), lambda i, s: (s, 0, 0)), pl.BlockSpec((1, d, fs), lambda i, s: (s, 0, 0)),
                  pl.BlockSpec((1, fs, d), lambda i, s: (s, 0, 0))],
        out_specs=[row, pl.BlockSpec((1, d), lambda i, s: (0, 0)), row, row, act, act, act],
        scratch_shapes=[pltpu.VMEM((tm, d), F32)],
        compiler_params=_params(("arbitrary", "arbitrary"), VMEM_BIG),
    )(h, gamma, dh, gg, uu, wg, wu, wd)


def ffn_bwd_weights(hn, dy, a, dg, du, name):
    r, d = hn.shape
    ns, _, fs = a.shape
    tm = WGRAD_ROWS if r % WGRAD_ROWS == 0 else _row_tile(r)

    def body(hn_ref, dy_ref, a_ref, dg_ref, du_ref, wg_ref, wu_ref, wd_ref):
        @pl.when(pl.program_id(1) == 0)
        def _():
            wg_ref[...] = jnp.zeros_like(wg_ref)
            wu_ref[...] = jnp.zeros_like(wu_ref)
            wd_ref[...] = jnp.zeros_like(wd_ref)

        hn_ = hn_ref[...]
        wg_ref[0] += lax.dot_general(hn_, dg_ref[0], TN, preferred_element_type=F32)
        wu_ref[0] += lax.dot_general(hn_, du_ref[0], TN, preferred_element_type=F32)
        wd_ref[0] += lax.dot_general(a_ref[0], dy_ref[...], TN, preferred_element_type=F32)

    row = pl.BlockSpec((tm, d), lambda s, i: (i, 0))
    act = pl.BlockSpec((1, tm, fs), lambda s, i: (s, i, 0))
    return pl.pallas_call(
        body, name=name,
        out_shape=[jax.ShapeDtypeStruct((ns, d, fs), F32), jax.ShapeDtypeStruct((ns, d, fs), F32),
                   jax.ShapeDtypeStruct((ns, fs, d), F32)],
        grid=(ns, r // tm),
        in_specs=[row, row, act, act, act],
        out_specs=[pl.BlockSpec((1, d, fs), lambda s, i: (s, 0, 0)), pl.BlockSpec((1, d, fs), lambda s, i: (s, 0, 0)),
                   pl.BlockSpec((1, fs, d), lambda s, i: (s, 0, 0))],
        compiler_params=_params(("arbitrary", "arbitrary"), VMEM_MAX),
    )(hn, dy, a, dg, du)


def norm_fwd(x, gamma, name):
    r = x.shape[0]
    w = gamma.shape[1]
    tm = _row_tile(r)

    def body(x_ref, g_ref, o_ref):
        xh, _ = _rms(x_ref[...])
        o_ref[...] = (xh * g_ref[...]).astype(BF16)

    return pl.pallas_call(
        body, name=name, out_shape=jax.ShapeDtypeStruct((r, w), BF16), grid=(r // tm,),
        in_specs=[pl.BlockSpec((tm, w), lambda i: (i, 0)), pl.BlockSpec((1, w), lambda i: (0, 0))],
        out_specs=pl.BlockSpec((tm, w), lambda i: (i, 0)),
        compiler_params=_params(("arbitrary",)),
    )(x, gamma)


def norm_bwd(x, gamma, dy, dres, name):
    r = x.shape[0]
    w = gamma.shape[1]
    tm = _row_tile(r)
    has_res = dres is not None

    def body(*refs):
        if has_res:
            x_ref, g_ref, dy_ref, dr_ref, dx_ref, dgam_ref = refs
        else:
            x_ref, g_ref, dy_ref, dx_ref, dgam_ref = refs

        @pl.when(pl.program_id(0) == 0)
        def _():
            dgam_ref[...] = jnp.zeros_like(dgam_ref)

        xh, rstd = _rms(x_ref[...])
        dy_ = dy_ref[...].astype(F32)
        dgam_ref[...] += jnp.sum(dy_ * xh, axis=0, keepdims=True)
        dx = _rms_bwd(xh, rstd, dy_ * g_ref[...])
        if has_res:
            dx = dx + dr_ref[...]
        dx_ref[...] = dx

    row = pl.BlockSpec((tm, w), lambda i: (i, 0))
    vec = pl.BlockSpec((1, w), lambda i: (0, 0))
    ins = [x, gamma, dy] + ([dres] if has_res else [])
    return pl.pallas_call(
        body, name=name,
        out_shape=[jax.ShapeDtypeStruct((r, w), F32), jax.ShapeDtypeStruct((1, w), F32)],
        grid=(r // tm,), in_specs=[row, vec, row] + ([row] if has_res else []), out_specs=[row, vec],
        compiler_params=_params(("arbitrary",)),
    )(*ins)


def rowmm(a, w, name, *, nt=False, res=None, out_dtype=F32, heads_out=False):
    ha, r, ka = a.shape
    hw = w.shape[0]
    nh = max(ha, hw)
    n = w.shape[1] if nt else w.shape[2]
    tm = _row_tile(r)
    dims = NT if nt else (((1,), (0,)), ((), ()))
    has_res = res is not None

    def body(*refs):
        if has_res:
            a_ref, w_ref, r_ref, o_ref = refs
        else:
            a_ref, w_ref, o_ref = refs
        shared = a_ref[0].astype(BF16) if ha == 1 else None
        acc = None
        for h in range(nh):
            lhs = shared if ha == 1 else a_ref[h].astype(BF16)
            p = lax.dot_general(lhs, w_ref[h if hw > 1 else 0], dims, preferred_element_type=F32)
            if heads_out:
                o_ref[h] = p.astype(out_dtype)
            else:
                acc = p if acc is None else acc + p
        if not heads_out:
            if has_res:
                acc = acc + r_ref[...]
            o_ref[...] = acc.astype(out_dtype)

    in_specs = [pl.BlockSpec((ha, tm, ka), lambda i: (0, i, 0)), pl.BlockSpec(w.shape, lambda i: (0, 0, 0))]
    ins = [a, w]
    if has_res:
        in_specs.append(pl.BlockSpec((tm, n), lambda i: (i, 0)))
        ins.append(res)
    if heads_out:
        out_shape = jax.ShapeDtypeStruct((nh, r, n), out_dtype)
        out_spec = pl.BlockSpec((nh, tm, n), lambda i: (0, i, 0))
    else:
        out_shape = jax.ShapeDtypeStruct((r, n), out_dtype)
        out_spec = pl.BlockSpec((tm, n), lambda i: (i, 0))
    return pl.pallas_call(
        body, name=name, out_shape=out_shape, grid=(r // tm,), in_specs=in_specs, out_specs=out_spec,
        compiler_params=_params(("arbitrary",), VMEM_BIG),
    )(*ins)


def tnmm(a, b, name):
    ha, r, ka = a.shape
    hb, _, nb = b.shape
    nh = max(ha, hb)
    tm = _row_tile(r)

    def body(a_ref, b_ref, o_ref):
        @pl.when(pl.program_id(0) == 0)
        def _():
            o_ref[...] = jnp.zeros_like(o_ref)

        a_shared = a_ref[0].astype(BF16) if ha == 1 else None
        b_shared = b_ref[0].astype(BF16) if hb == 1 else None
        for h in range(nh):
            lhs = a_shared if ha == 1 else a_ref[h].astype(BF16)
            rhs = b_shared if hb == 1 else b_ref[h].astype(BF16)
            o_ref[h] += lax.dot_general(lhs, rhs, TN, preferred_element_type=F32)

    return pl.pallas_call(
        body, name=name, out_shape=jax.ShapeDtypeStruct((nh, ka, nb), F32), grid=(r // tm,),
        in_specs=[pl.BlockSpec((ha, tm, ka), lambda i: (0, i, 0)), pl.BlockSpec((hb, tm, nb), lambda i: (0, i, 0))],
        out_specs=pl.BlockSpec((nh, ka, nb), lambda i: (0, 0, 0)),
        compiler_params=_params(("arbitrary",), VMEM_BIG),
    )(a, b)


def rope_tables(r):
    inv = 1.0 / (ROPE_THETA ** (jnp.arange(0, QK_ROPE, 2, dtype=F32) / QK_ROPE))
    pos = (jnp.arange(r, dtype=F32) - META_ROW0)[:, None]
    ang = pos * inv[None, :]
    cos, sin = jnp.cos(ang), jnp.sin(ang)
    ones = jnp.ones((r, HEAD_W - QK_ROPE), F32)
    ctab = jnp.concatenate([cos, cos, ones], axis=1)
    stab = jnp.concatenate([-sin, sin, jnp.zeros_like(ones)], axis=1)
    return ctab, stab


def _swap_halves(z):
    lane = lax.broadcasted_iota(jnp.int32, z.shape, 1)
    up = pltpu.roll(z, HEAD_W - QK_ROPE // 2, 1)
    down = pltpu.roll(z, QK_ROPE // 2, 1)
    return jnp.where(lane < QK_ROPE // 2, up, jnp.where(lane < QK_ROPE, down, 0.0))


def proj_rope(a, w, ctab, stab, extra, name, scale=1.0):
    r, ka = a.shape
    nh = w.shape[0]
    tm = _row_tile(r)
    has_extra = extra is not None

    def body(*refs):
        if has_extra:
            a_ref, w_ref, c_ref, s_ref, e_ref, o_ref = refs
        else:
            a_ref, w_ref, c_ref, s_ref, o_ref = refs
        a_ = a_ref[...]
        ctab_, stab_ = c_ref[...], s_ref[...]
        if scale != 1.0:
            ctab_, stab_ = ctab_ * scale, stab_ * scale
        for h in range(nh):
            x = jnp.dot(a_, w_ref[h], preferred_element_type=F32)
            if has_extra:
                x = x + e_ref[...]
            o_ref[h] = (x * ctab_ + _swap_halves(x) * stab_).astype(BF16)

    tab = pl.BlockSpec((tm, HEAD_W), lambda i: (i, 0))
    in_specs = [pl.BlockSpec((tm, ka), lambda i: (i, 0)), pl.BlockSpec((nh, ka, HEAD_W), lambda i: (0, 0, 0)),
                tab, tab]
    ins = [a, w, ctab, stab]
    if has_extra:
        in_specs.append(pl.BlockSpec((tm, HEAD_W), lambda i: (i, 2)))
        ins.append(extra)
    return pl.pallas_call(
        body, name=name, out_shape=jax.ShapeDtypeStruct((nh, r, HEAD_W), BF16), grid=(r // tm,),
        in_specs=in_specs, out_specs=pl.BlockSpec((nh, tm, HEAD_W), lambda i: (0, i, 0)),
        compiler_params=_params(("arbitrary",)),
    )(*ins)


def rope_bwd_heads(d, ctab, stab, name, scale=1.0):
    nh, r, _ = d.shape
    tm = _row_tile(r)

    def body(d_ref, c_ref, s_ref, o_ref):
        ctab_, stab_ = c_ref[...], s_ref[...]
        if scale != 1.0:
            ctab_, stab_ = ctab_ * scale, stab_ * scale
        for h in range(nh):
            d_ = d_ref[h]
            o_ref[h] = (d_ * ctab_ + _swap_halves(d_ * stab_)).astype(BF16)

    tab = pl.BlockSpec((tm, HEAD_W), lambda i: (i, 0))
    blk = pl.BlockSpec((nh, tm, HEAD_W), lambda i: (0, i, 0))
    return pl.pallas_call(
        body, name=name, out_shape=jax.ShapeDtypeStruct((nh, r, HEAD_W), BF16), grid=(r // tm,),
        in_specs=[blk, tab, tab], out_specs=blk,
        compiler_params=_params(("arbitrary",)),
    )(d, ctab, stab)


def rope_bwd_sum(d, ctab, stab, name):
    nh, r, _ = d.shape
    tm = _row_tile(r)

    def body(d_ref, c_ref, s_ref, o_ref):
        d_ = d_ref[0]
        for h in range(1, nh):
            d_ = d_ + d_ref[h]
        lane = lax.broadcasted_iota(jnp.int32, d_.shape, 1)
        g = d_ * c_ref[...] + _swap_halves(d_ * s_ref[...])
        o_ref[...] = jnp.where(lane < QK_ROPE, g, 0.0)

    tab = pl.BlockSpec((tm, HEAD_W), lambda i: (i, 0))
    return pl.pallas_call(
        body, name=name, out_shape=jax.ShapeDtypeStruct((r, HEAD_W), F32), grid=(r // tm,),
        in_specs=[pl.BlockSpec((nh, tm, HEAD_W), lambda i: (0, i, 0)), tab, tab], out_specs=tab,
        compiler_params=_params(("arbitrary",)),
    )(d, ctab, stab)


def _attn_tiles(r):
    t = _row_tile(r)
    return t, t


def _mask(q0, k0, nq_, nk_, keys_on_rows=False):
    shape = (nk_, nq_) if keys_on_rows else (nq_, nk_)
    rq = q0 + lax.broadcasted_iota(jnp.int32, shape, 1 if keys_on_rows else 0)
    rk = k0 + lax.broadcasted_iota(jnp.int32, shape, 0 if keys_on_rows else 1)
    return ((rk >> CHUNK_SHIFT) <= (rq >> CHUNK_SHIFT)) & (rk >= META_ROW0)


ATTN_SUB = 640


SM_SCALE = 1.0 / math.sqrt(QK_NOPE + QK_ROPE)
LOG2E = math.log2(math.e)
Q_SCALE = SM_SCALE * LOG2E


def attn_fwd(q, k, v, name):
    nh, r, dk = q.shape
    dv = v.shape[-1]
    tq, tk = _attn_tiles(r)
    nq, nk = r // tq, r // tk

    def last_k(i):
        return ((i + 1) * tq - 1) // tk

    pairs = [(i, j) for i in range(nq) for j in range(last_k(i) + 1)]
    qi_tab = jnp.asarray([p[0] for p in pairs], jnp.int32)
    kj_tab = jnp.asarray([p[1] for p in pairs], jnp.int32)
    sub = _tile(tq, ATTN_SUB)

    def body(qi_ref, kj_ref, q_ref, k_ref, v_ref, o_ref, lse_ref, m_sc, l_sc, acc_sc):
        t = pl.program_id(1)
        i, j = qi_ref[t], kj_ref[t]

        @pl.when(j == 0)
        def _():
            m_sc[...] = jnp.full_like(m_sc, NEG)
            l_sc[...] = jnp.zeros_like(l_sc)
            acc_sc[...] = jnp.zeros_like(acc_sc)

        def step(masked):
            k_, v_ = k_ref[0], v_ref[0]
            for r0 in range(0, tq, sub):
                rows = slice(r0, r0 + sub)
                s = lax.dot_general(q_ref[0, rows, :], k_, NT, preferred_element_type=F32)
                if masked:
                    s = jnp.where(_mask(i * tq + r0, j * tk, sub, tk), s, NEG)
                m_old = m_sc[rows, :]
                m_new = jnp.maximum(m_old, jnp.max(s, axis=-1, keepdims=True))
                alpha = jnp.exp2(m_old - m_new)
                p = jnp.exp2(s - jnp.tile(m_new, (1, tk // LANES)))
                l_sc[rows, :] = alpha * l_sc[rows, :] + jnp.sum(p, axis=-1, keepdims=True)
                acc_sc[rows, :] = (alpha[:, :dv] * acc_sc[rows, :]
                                   + jnp.dot(p.astype(BF16), v_, preferred_element_type=F32))
                m_sc[rows, :] = m_new

        needs_mask = (j == last_k(i)) | (j == 0)
        pl.when(needs_mask)(functools.partial(step, True))
        pl.when(jnp.logical_not(needs_mask))(functools.partial(step, False))

        @pl.when(j == last_k(i))
        def _():
            l = l_sc[...]
            o_ref[0] = (acc_sc[...] / l[:, :dv]).astype(BF16)
            lse_ref[0] = (m_sc[...] + jnp.log2(l))[:, :1]

    qspec = lambda w: pl.BlockSpec((1, tq, w), lambda h, t, qi, kj: (h, qi[t], 0))
    kspec = lambda w: pl.BlockSpec((1, tk, w), lambda h, t, qi, kj: (h, kj[t], 0))
    return pl.pallas_call(
        body, name=name,
        out_shape=[jax.ShapeDtypeStruct((nh, r, dv), BF16), jax.ShapeDtypeStruct((nh, r, 1), F32)],
        grid_spec=pltpu.PrefetchScalarGridSpec(
            num_scalar_prefetch=2, grid=(nh, len(pairs)),
            in_specs=[qspec(dk), kspec(dk), kspec(dv)], out_specs=[qspec(dv), qspec(1)],
            scratch_shapes=[pltpu.VMEM((tq, LANES), F32), pltpu.VMEM((tq, LANES), F32), pltpu.VMEM((tq, dv), F32)]),
        compiler_params=_params(("arbitrary", "arbitrary")),
    )(qi_tab, kj_tab, q, k, v)


def attn_bwd(q, k, v, do, lse, delta, dk_prev, dv_prev, name):
    nh, r, dk = q.shape
    dv = v.shape[-1]
    tq, tk = _attn_tiles(r)
    nq, nk = r // tq, r // tk
    has_prev = dk_prev is not None

    def first_q(j):
        return (j * tk) // tq

    pairs = [(j, i) for j in range(nk) for i in range(first_q(j), nq)]
    kj_tab = jnp.asarray([p[0] for p in pairs], jnp.int32)
    qi_tab = jnp.asarray([p[1] for p in pairs], jnp.int32)
    sub = _tile(tk, ATTN_SUB)

    def body(*refs):
        if has_prev:
            (kj_ref, qi_ref, q_ref, k_ref, v_ref, do_ref, lse_ref, dl_ref, pk_ref, pv_ref,
             dq_ref, dk_ref, dv_ref, dk_sc, dv_sc) = refs
        else:
            (kj_ref, qi_ref, q_ref, k_ref, v_ref, do_ref, lse_ref, dl_ref,
             dq_ref, dk_ref, dv_ref, dk_sc, dv_sc) = refs
        t = pl.program_id(1)
        j, i = kj_ref[t], qi_ref[t]

        @pl.when(t == 0)
        def _():
            dq_ref[...] = jnp.zeros_like(dq_ref)

        @pl.when(i == first_q(j))
        def _():
            dk_sc[...] = jnp.zeros_like(dk_sc)
            dv_sc[...] = jnp.zeros_like(dv_sc)

        def step(masked):
            q_, do_ = q_ref[0], do_ref[0]
            lse_, dl_ = lse_ref[0], dl_ref[0]
            dq_acc = None
            for r0 in range(0, tk, sub):
                rows = slice(r0, r0 + sub)
                k_ = k_ref[0, rows, :]
                st = lax.dot_general(k_, q_, NT, preferred_element_type=F32)
                if masked:
                    st = jnp.where(_mask(i * tq, j * tk + r0, tq, sub, keys_on_rows=True), st, NEG)
                pt = jnp.exp2(st - lse_)
                dv_sc[rows, :] += jnp.dot(pt.astype(BF16), do_, preferred_element_type=F32)
                dpt = lax.dot_general(v_ref[0, rows, :], do_, NT, preferred_element_type=F32)
                dst = (pt * (dpt - dl_)).astype(BF16)
                dk_sc[rows, :] += jnp.dot(dst, q_, preferred_element_type=F32)
                part = lax.dot_general(dst, k_, TN, preferred_element_type=F32)
                dq_acc = part if dq_acc is None else dq_acc + part
            qrows = pl.ds(pl.multiple_of(i * tq, tq), tq)
            dq_ref[0, qrows, :] += dq_acc

        needs_mask = (i == first_q(j)) | (j == 0)
        pl.when(needs_mask)(functools.partial(step, True))
        pl.when(jnp.logical_not(needs_mask))(functools.partial(step, False))

        @pl.when(i == nq - 1)
        def _():
            dk_ = dk_sc[...] * (1.0 / LOG2E)
            dv_ = dv_sc[...]
            if has_prev:
                dk_ = dk_ + pk_ref[0]
                dv_ = dv_ + pv_ref[0]
            dk_ref[0] = dk_
            dv_ref[0] = dv_

    krow = lambda w: pl.BlockSpec((1, tk, w), lambda h, t, kj, qi: (h, kj[t], 0))
    qrow = lambda w: pl.BlockSpec((1, tq, w), lambda h, t, kj, qi: (h, qi[t], 0))
    qvec = pl.BlockSpec((1, 1, tq), lambda h, t, kj, qi: (h, 0, qi[t]))
    in_specs = [qrow(dk), krow(dk), krow(dv), qrow(dv), qvec, qvec]
    ins = [q, k, v, do, lse, delta]
    if has_prev:
        in_specs += [krow(dk), krow(dv)]
        ins += [dk_prev, dv_prev]
    return pl.pallas_call(
        body, name=name,
        out_shape=[jax.ShapeDtypeStruct((nh, r, dk), F32), jax.ShapeDtypeStruct((nh, r, dk), F32),
                   jax.ShapeDtypeStruct((nh, r, dv), F32)],
        grid_spec=pltpu.PrefetchScalarGridSpec(
            num_scalar_prefetch=2, grid=(nh, len(pairs)), in_specs=in_specs,
            out_specs=[pl.BlockSpec((1, r, dk), lambda h, t, kj, qi: (h, 0, 0)), krow(dk), krow(dv)],
            scratch_shapes=[pltpu.VMEM((tk, dk), F32), pltpu.VMEM((tk, dv), F32)]),
        compiler_params=_params(("arbitrary", "arbitrary"), VMEM_BIG),
    )(kj_tab, qi_tab, *ins)


def attn_out_bwd(dattn, wo, o, name):
    r, d = dattn.shape
    nh, dv, _ = wo.shape
    tm = _row_tile(r)

    def body(da_ref, w_ref, o_ref, do_ref, dl_ref):
        da = da_ref[...].astype(BF16)
        for h in range(nh):
            do_ = lax.dot_general(da, w_ref[h], NT, preferred_element_type=F32).astype(BF16)
            do_ref[h] = do_
            dl_ref[h] = jnp.sum(do_.astype(F32) * o_ref[h].astype(F32), axis=-1, keepdims=True)

    return pl.pallas_call(
        body, name=name,
        out_shape=[jax.ShapeDtypeStruct((nh, r, dv), BF16), jax.ShapeDtypeStruct((nh, r, 1), F32)],
        grid=(r // tm,),
        in_specs=[pl.BlockSpec((tm, d), lambda i: (i, 0)), pl.BlockSpec((nh, dv, d), lambda i: (0, 0, 0)),
                  pl.BlockSpec((nh, tm, dv), lambda i: (0, i, 0))],
        out_specs=[pl.BlockSpec((nh, tm, dv), lambda i: (0, i, 0)), pl.BlockSpec((nh, tm, 1), lambda i: (0, i, 0))],
        compiler_params=_params(("arbitrary",)),
    )(dattn, wo, o)


def _pool_counts(row0, n, window):
    rows = row0 + lax.broadcasted_iota(jnp.int32, (n, 1), 0)
    cnt = jnp.clip(rows - META_ROW0 + 1, 1, window)
    return 1.0 / cnt.astype(F32)


def pool_fwd(h, gamma, wp, scale, name):
    r, d = h.shape
    ng, cg, _ = wp.shape
    tm = _row_tile(r)
    hb = tm // HALO

    def body(h_ref, hp_ref, g_ref, w_ref, sc_ref, o_ref):
        i = pl.program_id(0)
        xm = h_ref[...]
        xp = hp_ref[...] * jnp.where(i > 0, 1.0, 0.0)
        xx = jnp.concatenate([xp, xm], axis=0)
        xh, _ = _rms(xx)
        u = xh * g_ref[...]
        for g, win in enumerate(POOL_WINDOWS):
            sl = slice(g * cg, (g + 1) * cg)
            ug = u[:, sl]
            acc, k = ug, 1
            while k < win:
                acc = acc + pltpu.roll(acc, k, 0)
                k *= 2
            pooled = acc[HALO:] * _pool_counts(i * tm, tm, win) - ug[HALO:]
            y = jnp.dot(pooled.astype(BF16), w_ref[g], preferred_element_type=F32)
            o_ref[:, sl] = xm[:, sl] + y * sc_ref[:, sl]

    return pl.pallas_call(
        body, name=name, out_shape=jax.ShapeDtypeStruct((r, d), F32), grid=(r // tm,),
        in_specs=[pl.BlockSpec((tm, d), lambda i: (i, 0)),
                  pl.BlockSpec((HALO, d), lambda i: (jnp.maximum(i * hb - 1, 0), 0)),
                  pl.BlockSpec((1, d), lambda i: (0, 0)), pl.BlockSpec((ng, cg, cg), lambda i: (0, 0, 0)),
                  pl.BlockSpec((1, d), lambda i: (0, 0))],
        out_specs=pl.BlockSpec((tm, d), lambda i: (i, 0)),
        compiler_params=_params(("arbitrary",), VMEM_BIG),
    )(h, h, gamma, wp, scale)


def pool_bwd(h, gamma, wp, scale, dh, name):
    r, d = h.shape
    ng, cg, _ = wp.shape
    tm = _row_tile(r)
    hb = tm // HALO
    nt = r // tm

    def body(h_ref, hp_ref, dh_ref, dn_ref, g_ref, w_ref, sc_ref, dx_ref, dgam_ref, dw_ref, dsc_ref, du_sc):
        i = pl.program_id(0)

        @pl.when(i == 0)
        def _():
            dgam_ref[...] = jnp.zeros_like(dgam_ref)
            dw_ref[...] = jnp.zeros_like(dw_ref)
            dsc_ref[...] = jnp.zeros_like(dsc_ref)

        xm = h_ref[...]
        xp = hp_ref[...] * jnp.where(i > 0, 1.0, 0.0)
        xh_all, rstd_all = _rms(jnp.concatenate([xp, xm], axis=0))
        u = xh_all * g_ref[...]
        dm = dh_ref[...]
        dn = dn_ref[...] * jnp.where(i < nt - 1, 1.0, 0.0)
        dd = jnp.concatenate([dm, dn], axis=0)
        for g, win in enumerate(POOL_WINDOWS):
            sl = slice(g * cg, (g + 1) * cg)
            ug = u[:, sl]
            acc, k = ug, 1
            while k < win:
                acc = acc + pltpu.roll(acc, k, 0)
                k *= 2
            pooled = (acc[HALO:] * _pool_counts(i * tm, tm, win) - ug[HALO:]).astype(BF16)
            y = jnp.dot(pooled, w_ref[g], preferred_element_type=F32)
            dsc_ref[:, sl] += jnp.sum(dm[:, sl] * y, axis=0, keepdims=True)
            dyp = (dd[:, sl] * sc_ref[:, sl]).astype(BF16)
            dw_ref[g] += lax.dot_general(pooled, dyp[:tm], TN, preferred_element_type=F32)
            dpo = lax.dot_general(dyp, w_ref[g], NT, preferred_element_type=F32)
            z = dpo * _pool_counts(i * tm, tm + HALO, win)
            fwd, k = z, 1
            while k < win:
                fwd = fwd + pltpu.roll(fwd, tm + HALO - k, 0)
                k *= 2
            du_sc[:, sl] = fwd[:tm] - dpo[:tm]
        du = du_sc[...]
        xh, rstd = xh_all[HALO:], rstd_all[HALO:]
        dgam_ref[...] += jnp.sum(du * xh, axis=0, keepdims=True)
        dx_ref[...] = dm + _rms_bwd(xh, rstd, du * g_ref[...])

    row = pl.BlockSpec((tm, d), lambda i: (i, 0))
    vec = pl.BlockSpec((1, d), lambda i: (0, 0))
    prev = pl.BlockSpec((HALO, d), lambda i: (jnp.maximum(i * hb - 1, 0), 0))
    nxt = pl.BlockSpec((HALO, d), lambda i: (jnp.minimum((i + 1) * hb, r // HALO - 1), 0))
    wsp = pl.BlockSpec((ng, cg, cg), lambda i: (0, 0, 0))
    return pl.pallas_call(
        body, name=name,
        out_shape=[jax.ShapeDtypeStruct((r, d), F32), jax.ShapeDtypeStruct((1, d), F32),
                   jax.ShapeDtypeStruct((ng, cg, cg), F32), jax.ShapeDtypeStruct((1, d), F32)],
        grid=(nt,), in_specs=[row, prev, row, nxt, vec, wsp, vec], out_specs=[row, vec, wsp, vec],
        scratch_shapes=[pltpu.VMEM((tm, d), F32)],
        compiler_params=_params(("arbitrary",), VMEM_BIG),
    )(h, h, dh, dh, gamma, wp, scale)


def loss_head(h, gamma, target, seq, name):
    r, d = h.shape
    tm = _row_tile(r)

    def body(h_ref, g_ref, t_ref, sse_ref, dh_ref, dgam_ref):
        i = pl.program_id(0)

        @pl.when(i == 0)
        def _():
            sse_ref[...] = jnp.zeros_like(sse_ref)
            dgam_ref[...] = jnp.zeros_like(dgam_ref)

        xh, rstd = _rms(h_ref[...])
        rows = i * tm + lax.broadcasted_iota(jnp.int32, (tm, 1), 0)
        valid = ((rows >= FRONT) & (rows < FRONT + seq)).astype(F32)
        e = (xh * g_ref[...] - t_ref[...]) * valid
        sse_ref[...] += jnp.sum(jnp.sum(e * e, axis=1, keepdims=True), axis=0, keepdims=True)
        dy = e * (1.0 / d)
        dgam_ref[...] += jnp.sum(dy * xh, axis=0, keepdims=True)
        dh_ref[...] = _rms_bwd(xh, rstd, dy * g_ref[...])

    row = pl.BlockSpec((tm, d), lambda i: (i, 0))
    vec = pl.BlockSpec((1, d), lambda i: (0, 0))
    return pl.pallas_call(
        body, name=name,
        out_shape=[jax.ShapeDtypeStruct((1, 1), F32), jax.ShapeDtypeStruct((r, d), F32),
                   jax.ShapeDtypeStruct((1, d), F32)],
        grid=(r // tm,), in_specs=[row, vec, row],
        out_specs=[pl.BlockSpec((1, 1), lambda i: (0, 0)), row, vec],
        compiler_params=_params(("arbitrary",)),
    )(h, gamma, target)


SMALL = ("meta_tokens", "pool_w", "pool_scale", "w_dkv", "w_uk", "w_uv", "w_dq", "w_uq", "w_o")


def _pack(parts):
    flat = jnp.concatenate([p.reshape(-1) for p in parts])
    n = flat.shape[0]
    unit = PACK_W * 2 * SUBLANES
    n_pad = -(-n // unit) * unit
    return jnp.pad(flat, (0, n_pad - n)).reshape(n_pad // PACK_W, PACK_W)


def _unpack(buf, shapes, lead=()):
    flat = buf.reshape(lead + (-1,))
    out, off = [], 0
    for shp in shapes:
        n = math.prod(shp)
        out.append(flat[..., off:off + n].reshape(lead + tuple(shp)))
        off += n
    return out


def _cols_from_shards(a, axis):
    a = jnp.moveaxis(a, 0, axis)
    shp = a.shape
    return a.reshape(shp[:axis] + (shp[axis] * shp[axis + 1],) + shp[axis + 2:])


def _cols_to_shards(a, axis):
    shp = a.shape
    a = a.reshape(shp[:axis] + (N_SHARD, shp[axis] // N_SHARD) + shp[axis + 1:])
    return jnp.moveaxis(a, axis, 0)


SMALL_AXIS = {"meta_tokens": 1, "pool_w": 2, "pool_scale": 1, "w_dkv": 0, "w_uk": 1, "w_uv": 1,
              "w_dq": 1, "w_uq": 2, "w_o": 2}


def kernel(x, meta_tokens, ffn1_norm, ffn1_w_gate, ffn1_w_up, ffn1_w_down, mix_norm, ffn2_norm, ffn2_w_gate, ffn2_w_up, ffn2_w_down, pool_w, pool_scale, kv_in_norm, w_dkv, kv_latent_norm, w_uk, w_uv, w_dq, q_latent_norm, w_uq, w_o, final_norm, loss_target, m_meta_tokens, m_ffn1_norm, m_ffn1_w_gate, m_ffn1_w_up, m_ffn1_w_down, m_mix_norm, m_ffn2_norm, m_ffn2_w_gate, m_ffn2_w_up, m_ffn2_w_down, m_pool_w, m_pool_scale, m_kv_in_norm, m_w_dkv, m_kv_latent_norm, m_w_uk, m_w_uv, m_w_dq, m_q_latent_norm, m_w_uq, m_w_o, m_final_norm, v_meta_tokens, v_ffn1_norm, v_ffn1_w_gate, v_ffn1_w_up, v_ffn1_w_down, v_mix_norm, v_ffn2_norm, v_ffn2_w_gate, v_ffn2_w_up, v_ffn2_w_down, v_pool_w, v_pool_scale, v_kv_in_norm, v_w_dkv, v_kv_latent_norm, v_w_uk, v_w_uv, v_w_dq, v_q_latent_norm, v_w_uq, v_w_o, v_final_norm):
    args = dict(locals())
    W = {n: args[n] for n in NAMES}
    M = {n: args["m_" + n] for n in NAMES}
    V = {n: args["v_" + n] for n in NAMES}

    depth = ffn1_norm.shape[0]
    n_a = pool_w.shape[0]
    seq, d = x.shape[1], x.shape[2]
    nh = N_HEADS
    r = -(-(FRONT + seq) // LANES) * LANES

    cx, cy, cc = lax.axis_index("x"), lax.axis_index("y"), lax.axis_index("c")
    c_arr = jnp.reshape(cc, (1,)).astype(jnp.int32)
    s_arr = jnp.reshape(2 * cx + cy, (1,)).astype(jnp.int32)

    small_shapes = [W[n].shape for n in SMALL]
    gathered = all_gather_shards([_pack([W[n] for n in SMALL])], "ag_small")[0]
    small_full = {}
    for n, part in zip(SMALL, _unpack(gathered, small_shapes, (N_SHARD,))):
        small_full[n] = _cols_from_shards(part, SMALL_AXIS[n])

    ffn_src = {"ffn1": tuple(a.astype(BF16) for a in (ffn1_w_gate, ffn1_w_up, ffn1_w_down)),
               "ffn2": tuple(a.astype(BF16) for a in (ffn2_w_gate, ffn2_w_up, ffn2_w_down))}
    ffn_order = [(f, l) for l in range(depth) for f in ("ffn1", "ffn2")]
    ffn_w = {}
    gate = [None]
    ag_state = [None]

    def gated(a):
        if gate[0] is not None:
            a = a + gate[0][0, 0].astype(a.dtype)
            gate[0] = None
        return a

    def vec(a):
        return gated(a.reshape(1, -1))

    def ag_start(idx, dep):
        f, l = ffn_order[idx]
        shards = [w_[l] + dep for w_ in ffn_src[f]]
        lands = [lax.dynamic_update_slice(lax.empty((N_SHARD,) + a.shape, BF16), a[None], (2 * cx + cy, 0, 0))
                 for a in shards]
        ag_state[0] = exchange_start(shards, lands, f"ag_start_{f}_{l}", True)
        gate[0] = ag_state[0][4]

    def ag_wait(idx, after):
        f, l = ffn_order[idx]
        _, lands = exchange_wait(ag_state[0], after, f"ag_wait_{f}_{l}", True)
        ffn_w[f, l] = lands
        if idx + 1 < len(ffn_order):
            ag_start(idx + 1, lands[0][0, 0, 0] * jnp.zeros((), BF16))

    ag_start(0, jnp.zeros((), BF16))
    ag_wait(0, gathered)

    meta_full = small_full["meta_tokens"]
    wp = small_full["pool_w"].astype(BF16)
    pscale = small_full["pool_scale"]
    wdkv = jnp.pad(small_full["w_dkv"], ((0, 0), (0, HEAD_W - QK_ROPE))).astype(BF16)[None]
    wuk = small_full["w_uk"].reshape(KV_RANK, nh, QK_NOPE).transpose(1, 0, 2)
    wk_h = jnp.concatenate([jnp.zeros((nh, KV_RANK, HEAD_W - QK_NOPE), F32), wuk], axis=-1).astype(BF16)
    wv_h = small_full["w_uv"].reshape(KV_RANK, nh, V_HEAD).transpose(1, 0, 2).astype(BF16)
    wdq = small_full["w_dq"].astype(BF16)
    wuq = small_full["w_uq"].reshape(-1, Q_RANK, nh, QK_NOPE + QK_ROPE).transpose(0, 2, 1, 3)
    wq_h = jnp.concatenate([wuq[..., QK_NOPE:], jnp.zeros(wuq.shape[:-1] + (HEAD_W - QK_NOPE - QK_ROPE,), F32),
                            wuq[..., :QK_NOPE]], axis=-1).astype(BF16)
    wo_h = small_full["w_o"].reshape(-1, nh, V_HEAD, d).astype(BF16)

    ctab, stab = rope_tables(r)

    h = jnp.concatenate([jnp.zeros((META_ROW0, d), F32), meta_full, x[0],
                         jnp.zeros((r - FRONT - seq, d), F32)], axis=0)
    target = jnp.concatenate([jnp.zeros((FRONT, d), F32), loss_target[0],
                              jnp.zeros((r - FRONT - seq, d), F32)], axis=0)
    saved = []
    kv = None
    for l in range(depth):
        sv = {"h0": h}
        if l > 0:
            ag_wait(2 * l, h)
        h, sv["g1"], sv["u1"] = ffn_fwd(h, vec(ffn1_norm[l]), *ffn_w["ffn1", l], f"ffn1_fwd_{l}")
        sv["h1"] = h
        if l < n_a:
            h = pool_fwd(h, vec(mix_norm[l]), wp[l], vec(pscale[l]), f"pool_fwd_{l}")
        else:
            j = l - n_a
            u = norm_fwd(h, vec(mix_norm[l]), f"mixnorm_{l}")
            cq0 = rowmm(u[None], wdq[j][None], f"dq_{l}")
            cq = norm_fwd(cq0, vec(q_latent_norm[j]), f"qnorm_{l}")
            q = proj_rope(cq, wq_h[j], ctab, stab, None, f"qproj_{l}", scale=Q_SCALE)
            o, lse = attn_fwd(q, kv["k"], kv["v"], f"attn_fwd_{l}")
            h = rowmm(o, wo_h[j], f"oproj_{l}", res=h)
            sv.update(u=u, cq0=cq0, cq=cq, q=q, o=o, lse=lse)
        sv["h2"] = h
        ag_wait(2 * l + 1, h)
        h, sv["g2"], sv["u2"] = ffn_fwd(h, vec(ffn2_norm[l]), *ffn_w["ffn2", l], f"ffn2_fwd_{l}")
        saved.append(sv)
        if l == n_a - 1:
            hkv = norm_fwd(h, vec(kv_in_norm), "kvin_norm")
            ckr = rowmm(hkv[None], wdkv, "dkv")
            ckv = norm_fwd(ckr, vec(kv_latent_norm), "kvlat_norm")
            kv = {"h": h, "hkv": hkv, "ckr": ckr, "ckv": ckv,
                  "k": proj_rope(ckv, wk_h, ctab, stab, ckr, "kproj"),
                  "v": rowmm(ckv[None], wv_h, "vproj", out_dtype=BF16, heads_out=True)}

    sse, dh, dfinal = loss_head(h, vec(final_norm), target, seq, "loss_head")
    loss = lax.psum(0.5 / d * sse[0, 0], ("x", "y", "c"))

    G = {}
    FFN = ("ffn1_w_gate", "ffn1_w_up", "ffn1_w_down", "ffn2_w_gate", "ffn2_w_up", "ffn2_w_down")
    per = {n: [lax.empty(W[n].shape, F32) for _ in range(4)] for n in FFN}
    pending = []

    def rs_complete(after):
        f, l, state = pending.pop()
        names = [f + "_w_gate", f + "_w_up", f + "_w_down"]
        for n, g_ in zip(names, rs_finish(state, after, s_arr, f"{f}_{l}")):
            per[n] = adamw_into(W[n], g_, M[n], V[n], per[n], l, f"adamw_{n}_{l}")

    dnorm = {n: [None] * depth for n in ("ffn1_norm", "mix_norm", "ffn2_norm")}
    dqnorm = [None] * (depth - n_a)
    dpool_w, dpool_scale = [None] * n_a, [None] * n_a
    dwdq, dwq_h, dwo_h = [None] * (depth - n_a), [None] * (depth - n_a), [None] * (depth - n_a)

    def ffn_backward(f, l, h_in, dh_, gg, uu):
        gam = ffn1_norm[l] if f == "ffn1" else ffn2_norm[l]
        wg_, wu_, wd_ = ffn_w[f, l]
        dh_in, dgam, hn, dy, dg, du, a = ffn_bwd_act(h_in, vec(gam), dh_, gg, uu, wg_, wu_, wd_, f"{f}_bwd_act_{l}")
        dwg, dwu, dwd = ffn_bwd_weights(hn, dy, a, dg, du, f"{f}_bwd_w_{l}")
        if pending:
            rs_complete(dwg)
        state = rs_begin([dwg, dwu, dwd], c_arr, f"{f}_{l}")
        gate[0] = state[4]
        pending.append((f, l, state))
        dnorm[f + "_norm"][l] = dgam[0]
        return dh_in

    dk_tot = dv_tot = None
    for l in reversed(range(depth)):
        sv = saved[l]
        if l == n_a - 1:
            dckv = rowmm(dk_tot, gated(wk_h), "kproj_bwd", nt=True)
            dkr = rope_bwd_sum(dk_tot, ctab, stab, "kproj_bwd_rope")
            dckv = rowmm(dv_tot, wv_h, "vproj_bwd", nt=True, res=dckv)
            dwk_h = tnmm(kv["ckv"][None], dk_tot, "kproj_bwd_w")
            dwv_h = tnmm(kv["ckv"][None], dv_tot, "vproj_bwd_w")
            dlat, dkvlat = norm_bwd(kv["ckr"], vec(kv_latent_norm), dckv, None, "kvlat_norm_bwd")
            dckr = jnp.concatenate([dlat, dkr], axis=1).astype(BF16)
            dhkv = rowmm(dckr[None], wdkv, "dkv_bwd", nt=True)
            dwdkv = tnmm(kv["hkv"][None], dckr[None], "dkv_bwd_w")[0]
            dh, dkvin = norm_bwd(kv["h"], vec(kv_in_norm), dhkv, dh, "kvin_norm_bwd")
            G["w_dkv"] = dwdkv[:, :KV_RANK + QK_ROPE]
            G["w_uk"] = dwk_h[..., HEAD_W - QK_NOPE:].transpose(1, 0, 2).reshape(KV_RANK, nh * QK_NOPE)
            G["w_uv"] = dwv_h.transpose(1, 0, 2).reshape(KV_RANK, nh * V_HEAD)
        dh = ffn_backward("ffn2", l, sv["h2"], dh, sv["g2"], sv["u2"])
        if l < n_a:
            dh, dmix, dpool_w[l], dps = pool_bwd(sv["h1"], vec(mix_norm[l]), wp[l], vec(pscale[l]), dh, f"pool_bwd_{l}")
            dnorm["mix_norm"][l] = dmix[0]
            dpool_scale[l] = dps[0]
        else:
            j = l - n_a
            do, delta = attn_out_bwd(dh, gated(wo_h[j]), sv["o"], f"oproj_bwd_{l}")
            dwo_h[j] = tnmm(sv["o"], dh[None], f"oproj_bwd_w_{l}")
            dq, dk_tot, dv_tot = attn_bwd(sv["q"], kv["k"], kv["v"], do, sv["lse"].reshape(nh, 1, r),
                                          delta.reshape(nh, 1, r), dk_tot, dv_tot, f"attn_bwd_{l}")
            dxq = rope_bwd_heads(dq, ctab, stab, f"qproj_bwd_rope_{l}", scale=SM_SCALE)
            dcq = rowmm(dxq, wq_h[j], f"qproj_bwd_{l}", nt=True)
            dwq_h[j] = tnmm(sv["cq"][None], dxq, f"qproj_bwd_w_{l}")
            dcq0, dqn = norm_bwd(sv["cq0"], vec(q_latent_norm[j]), dcq, None, f"qnorm_bwd_{l}")
            dqnorm[j] = dqn[0]
            dcq0b = dcq0.astype(BF16)
            du = rowmm(dcq0b[None], wdq[j][None], f"dq_bwd_{l}", nt=True)
            dwdq[j] = tnmm(sv["u"][None], dcq0b[None], f"dq_bwd_w_{l}")[0]
            dh, dmix = norm_bwd(sv["h1"], vec(mix_norm[l]), du, dh, f"mixnorm_bwd_{l}")
            dnorm["mix_norm"][l] = dmix[0]
        dh = ffn_backward("ffn1", l, sv["h0"], dh, sv["g1"], sv["u1"])

    grad_x = dh[FRONT:FRONT + seq][None]
    G["meta_tokens"] = dh[META_ROW0:FRONT]
    G["pool_w"] = jnp.stack(dpool_w)
    G["pool_scale"] = jnp.stack(dpool_scale)
    G["w_dq"] = jnp.stack(dwdq)
    dwq = jnp.stack(dwq_h)
    dwq = jnp.concatenate([dwq[..., HEAD_W - QK_NOPE:], dwq[..., :QK_ROPE]], axis=-1)
    G["w_uq"] = dwq.transpose(0, 2, 1, 3).reshape(-1, Q_RANK, nh * (QK_NOPE + QK_ROPE))
    G["w_o"] = jnp.stack(dwo_h).reshape(-1, nh * V_HEAD, d)

    REPL = ("ffn1_norm", "mix_norm", "ffn2_norm", "kv_in_norm", "kv_latent_norm", "q_latent_norm", "final_norm")
    grep = {"ffn1_norm": jnp.stack(dnorm["ffn1_norm"]), "mix_norm": jnp.stack(dnorm["mix_norm"]),
            "ffn2_norm": jnp.stack(dnorm["ffn2_norm"]), "kv_in_norm": dkvin[0], "kv_latent_norm": dkvlat[0],
            "q_latent_norm": jnp.stack(dqnorm), "final_norm": dfinal[0]}

    def pack128(parts):
        flat = jnp.concatenate([p.reshape(-1) for p in parts])
        n = flat.shape[0]
        n_pad = -(-n // (LANES * SUBLANES)) * (LANES * SUBLANES)
        return jnp.pad(flat, (0, n_pad - n)).reshape(-1, LANES)

    rep_shapes = [W[n].shape for n in REPL]
    g_rep = all_reduce_small(pack128([grep[n] for n in REPL]), "ar_repl")
    d_rep, m_rep, v_rep = adamw(pack128([W[n] for n in REPL]), g_rep, pack128([M[n] for n in REPL]),
                                pack128([V[n] for n in REPL]), "adamw_repl")
    out_g, out_d, out_m, out_v = {}, {}, {}, {}
    for dst, buf in ((out_g, g_rep), (out_d, d_rep), (out_m, m_rep), (out_v, v_rep)):
        for n, a in zip(REPL, _unpack(buf, rep_shapes)):
            dst[n] = a

    g_small = jnp.stack([_pack([_cols_to_shards(G[n], SMALL_AXIS[n])[s] for n in SMALL]) for s in range(N_SHARD)])
    g_small = reduce_scatter([g_small], c_arr, s_arr, "small")[0]
    d_s, m_s, v_s = adamw(_pack([W[n] for n in SMALL]), g_small, _pack([M[n] for n in SMALL]),
                          _pack([V[n] for n in SMALL]), "adamw_small")
    for dst, buf in ((out_g, g_small), (out_d, d_s), (out_m, m_s), (out_v, v_s)):
        for n, a in zip(SMALL, _unpack(buf, small_shapes)):
            dst[n] = a

    rs_complete(g_small)
    for n in FFN:
        out_g[n], out_d[n], out_m[n], out_v[n] = per[n]

    return (loss, grad_x, *[out_g[n] for n in NAMES], *[out_d[n] for n in NAMES],
            *[out_m[n] for n in NAMES], *[out_v[n] for n in NAMES])


NAMES = ("meta_tokens", "ffn1_norm", "ffn1_w_gate", "ffn1_w_up", "ffn1_w_down", "mix_norm", "ffn2_norm",
         "ffn2_w_gate", "ffn2_w_up", "ffn2_w_down", "pool_w", "pool_scale", "kv_in_norm", "w_dkv",
         "kv_latent_norm", "w_uk", "w_uv", "w_dq", "q_latent_norm", "w_uq", "w_o", "final_norm")
```

```python
import functools
import math

import jax
import jax.numpy as jnp
from jax import lax
from jax.experimental import pallas as pl
from jax.experimental.pallas import tpu as pltpu

F32 = jnp.float32
BF16 = jnp.bfloat16
MESH = pl.DeviceIdType.MESH
ANY = pl.BlockSpec(memory_space=pl.ANY)

EPS = 1e-6
CHUNK = 64
CHUNK_SHIFT = 6
N_META = 16
FRONT = 64
META_ROW0 = FRONT - N_META
POOL_WINDOWS = (2, 4, 8, 16)
HALO = 16
N_HEADS = 8
QK_NOPE = 64
QK_ROPE = 32
V_HEAD = 64
HEAD_W = 128
KV_RANK = 256
Q_RANK = 384
ROPE_THETA = 10000.0
NEG = -1e30
N_SHARD = 4
LANES = 128
SUBLANES = 8
PACK_W = 512
VMEM_BIG = 52 * 1024 * 1024
VMEM_MAX = 60 * 1024 * 1024
WGRAD_ROWS = 1664

ADAM_LR = 0.001
ADAM_B1 = 0.9
ADAM_B2 = 0.999
ADAM_EPS = 1e-08
ADAM_WD = 0.01
ADAM_STEP = 10

NT = (((1,), (1,)), ((), ()))
TN = (((0,), (0,)), ((), ()))


def _params(sem=None, vmem=None):
    return pltpu.CompilerParams(dimension_semantics=sem, vmem_limit_bytes=vmem)


def _tile(n, pref, mult=SUBLANES):
    best = None
    for t in range(mult, min(n, pref) + 1, mult):
        if n % t == 0:
            best = t
    return best if best is not None else n


def _row_tile(r):
    return 640 if r % 640 == 0 else 128


def _rms(x):
    rstd = lax.rsqrt(jnp.mean(x * x, axis=-1, keepdims=True) + EPS)
    return x * rstd, rstd


def _rms_bwd(xh, rstd, dxh):
    return rstd * (dxh - xh * jnp.mean(dxh * xh, axis=-1, keepdims=True))


def _sigmoid(x):
    return 1.0 / (1.0 + jnp.exp(-x))


def _place():
    x, y, c = lax.axis_index("x"), lax.axis_index("y"), lax.axis_index("c")
    chips = [(1 - x, y), (x, 1 - y), (1 - x, 1 - y)]
    return x, y, c, chips


def all_gather_shards(shards, name):
    n = len(shards)
    slot = 2 * lax.axis_index("x") + lax.axis_index("y")
    lands = [lax.dynamic_update_slice(lax.empty((N_SHARD,) + a.shape, a.dtype), a[None], (slot, 0, 0)) for a in shards]

    def body(*refs):
        ins, outs = refs[:n], refs[2 * n:3 * n]
        send1, recv1, send2, recv2 = refs[3 * n:]
        x, y, c, chips = _place()
        s = 2 * x + y
        sib = (x, y, 1 - c)

        def rcopy(k, j, src, dst, to, first):
            return pltpu.make_async_remote_copy(
                src_ref=src, dst_ref=dst,
                send_sem=(send1 if first else send2).at[k, j],
                recv_sem=(recv1 if first else recv2).at[k, j],
                device_id=to, device_id_type=MESH)

        started = []
        for k in range(n):
            hf = ins[k].shape[0] // 2
            for j, (cx, cy) in enumerate(chips):
                r = rcopy(k, j, ins[k].at[pl.ds(c * hf, hf)], outs[k].at[s, pl.ds(c * hf, hf)],
                          (cx, cy, c), True)
                r.start()
                started.append(r)
        for k in range(n):
            hf = ins[k].shape[0] // 2
            for j, (cx, cy) in enumerate(chips):
                blk = outs[k].at[2 * cx + cy, pl.ds(c * hf, hf)]
                rcopy(k, j, blk, blk, (cx, cy, c), True).wait_recv()
                f = rcopy(k, j, blk, blk, sib, False)
                f.start()
                started.append(f)
        for k in range(n):
            hf = ins[k].shape[0] // 2
            for j, (cx, cy) in enumerate(chips):
                blk = outs[k].at[2 * cx + cy, pl.ds((1 - c) * hf, hf)]
                rcopy(k, j, blk, blk, sib, False).wait_recv()
        for r in started:
            r.wait_send()

    return pl.pallas_call(
        body, name=name,
        out_shape=[jax.ShapeDtypeStruct((N_SHARD,) + a.shape, a.dtype) for a in shards],
        in_specs=[ANY] * (2 * n), out_specs=[ANY] * n,
        input_output_aliases={n + k: k for k in range(n)},
        scratch_shapes=[pltpu.SemaphoreType.DMA((n, 3))] * 4,
    )(*shards, *lands)


def sibling_swap_halves(gs, name):
    n = len(gs)

    def body(*refs):
        ins, outs = refs[:n], refs[n:2 * n]
        send, recv = refs[2 * n:]
        x, y, c, _ = _place()
        sib = (x, y, 1 - c)
        cps = []
        for k in range(n):
            hf = ins[k].shape[1] // 2
            r = pltpu.make_async_remote_copy(
                src_ref=ins[k].at[:, pl.ds((1 - c) * hf, hf)], dst_ref=outs[k],
                send_sem=send.at[k], recv_sem=recv.at[k], device_id=sib, device_id_type=MESH)
            r.start()
            cps.append(r)
        for r in cps:
            r.wait_recv()
        for r in cps:
            r.wait_send()

    return pl.pallas_call(
        body, name=name,
        out_shape=[jax.ShapeDtypeStruct((a.shape[0], a.shape[1] // 2, a.shape[2]), a.dtype) for a in gs],
        in_specs=[ANY] * n, out_specs=[ANY] * n,
        scratch_shapes=[pltpu.SemaphoreType.DMA((n,))] * 2,
    )(*gs)


def chip_exchange(ps, name):
    n = len(ps)

    def body(*refs):
        ins, outs = refs[:n], refs[n:2 * n]
        send, recv = refs[2 * n:]
        x, y, c, chips = _place()
        cps = []
        for k in range(n):
            for j, (cx, cy) in enumerate(chips):
                r = pltpu.make_async_remote_copy(
                    src_ref=ins[k].at[2 * cx + cy], dst_ref=outs[k].at[j],
                    send_sem=send.at[k, j], recv_sem=recv.at[k, j],
                    device_id=(cx, cy, c), device_id_type=MESH)
                r.start()
                cps.append(r)
        for r in cps:
            r.wait_recv()
        for r in cps:
            r.wait_send()

    return pl.pallas_call(
        body, name=name,
        out_shape=[jax.ShapeDtypeStruct((3,) + a.shape[1:], a.dtype) for a in ps],
        in_specs=[ANY] * n, out_specs=[ANY] * n,
        scratch_shapes=[pltpu.SemaphoreType.DMA((n, 3))] * 2,
    )(*ps)


def sibling_join_halves(ts, name):
    n = len(ts)

    def body(*refs):
        ins, outs = refs[:n], refs[n:2 * n]
        send, recv = refs[2 * n:]
        x, y, c, _ = _place()
        sib = (x, y, 1 - c)

        def copy(k, half):
            hf = ins[k].shape[0] // 2
            rows = pl.ds(half * hf, hf)
            return pltpu.make_async_remote_copy(
                src_ref=ins[k].at[rows], dst_ref=outs[k].at[rows],
                send_sem=send.at[k], recv_sem=recv.at[k], device_id=sib, device_id_type=MESH)

        cps = [copy(k, c) for k in range(n)]
        for r in cps:
            r.start()
        for k in range(n):
            copy(k, 1 - c).wait_recv()
        for r in cps:
            r.wait_send()

    return pl.pallas_call(
        body, name=name,
        out_shape=[jax.ShapeDtypeStruct(a.shape, a.dtype) for a in ts],
        in_specs=[ANY] * n, out_specs=[ANY] * n, input_output_aliases={k: k for k in range(n)},
        scratch_shapes=[pltpu.SemaphoreType.DMA((n,))] * 2,
    )(*ts)


def all_reduce_small(part, name):
    m, w = part.shape

    def body(x_ref, tot_ref, gat_ref, send_sems, recv_sems):
        x, y, c, chips = _place()
        me, sib = (x, y, c), (x, y, 1 - c)

        def slot(px, py, pc):
            return gat_ref.at[4 * px + 2 * py + pc]

        def copy(k, block, to, src=None):
            return pltpu.make_async_remote_copy(
                src_ref=slot(*block) if src is None else src, dst_ref=slot(*block),
                send_sem=send_sems.at[k], recv_sem=recv_sems.at[k], device_id=to, device_id_type=MESH)

        gat_ref[4 * x + 2 * y + c] = x_ref[...]
        first = [copy(0, me, sib, src=x_ref)]
        first += [copy(1 + j, me, (*chip, c), src=x_ref) for j, chip in enumerate(chips)]
        for cp in first:
            cp.start()
        passed = [copy(4 + j, (*chip, c), sib) for j, chip in enumerate(chips)]
        for j, chip in enumerate(chips):
            copy(1 + j, (*chip, c), me).wait_recv()
            passed[j].start()
        copy(0, sib, me).wait_recv()
        for j, chip in enumerate(chips):
            copy(4 + j, (*chip, 1 - c), me).wait_recv()
        for cp in first + passed:
            cp.wait_send()
        tot = gat_ref[0]
        for d in range(1, 8):
            tot = tot + gat_ref[d]
        tot_ref[...] = tot

    return pl.pallas_call(
        body, name=name,
        out_shape=jax.ShapeDtypeStruct((m, w), F32),
        in_specs=[pl.BlockSpec(memory_space=pltpu.VMEM)],
        out_specs=pl.BlockSpec(memory_space=pltpu.VMEM),
        scratch_shapes=[pltpu.VMEM((8, m, w), F32), pltpu.SemaphoreType.DMA((7,)), pltpu.SemaphoreType.DMA((7,))],
    )(part)


def add_own_half(g, rec, c_arr, name):
    _, a, b = g.shape
    hf = a // 2
    tb = _tile(hf, 256)
    nb = hf // tb

    def body(c_ref, g_ref, r_ref, o_ref):
        o_ref[...] = g_ref[...] + r_ref[...]

    return pl.pallas_call(
        body, name=name,
        out_shape=jax.ShapeDtypeStruct((N_SHARD, hf, b), F32),
        grid_spec=pltpu.PrefetchScalarGridSpec(
            num_scalar_prefetch=1, grid=(N_SHARD, nb),
            in_specs=[pl.BlockSpec((1, tb, b), lambda s, i, c: (s, c[0] * nb + i, 0)),
                      pl.BlockSpec((1, tb, b), lambda s, i, c: (s, i, 0))],
            out_specs=pl.BlockSpec((1, tb, b), lambda s, i, c: (s, i, 0))),
        compiler_params=_params(("arbitrary", "arbitrary")),
    )(c_arr, g, rec)


def sum_four(p, rec, s_arr, c_arr, name):
    _, ah, b = p.shape
    tb = _tile(ah, 256)
    nb = ah // tb

    def body(s_ref, c_ref, p_ref, r0, r1, r2, o_ref):
        o_ref[...] = ((p_ref[0] + r0[0]) + r1[0]) + r2[0]

    def rspec(j):
        return pl.BlockSpec((1, tb, b), lambda i, s, c: (j, i, 0))

    return pl.pallas_call(
        body, name=name,
        out_shape=jax.ShapeDtypeStruct((2 * ah, b), F32),
        grid_spec=pltpu.PrefetchScalarGridSpec(
            num_scalar_prefetch=2, grid=(nb,),
            in_specs=[pl.BlockSpec((1, tb, b), lambda i, s, c: (s[0], i, 0)), rspec(0), rspec(1), rspec(2)],
            out_specs=pl.BlockSpec((tb, b), lambda i, s, c: (c[0] * nb + i, 0))),
        compiler_params=_params(("arbitrary",)),
    )(s_arr, c_arr, p, rec, rec, rec)


def adamw(w, g, m, v, name):
    a, b = w.shape
    tb = _tile(a, 256)
    c1 = 1.0 - ADAM_B1 ** ADAM_STEP
    c2 = 1.0 - ADAM_B2 ** ADAM_STEP

    def body(w_ref, g_ref, m_ref, v_ref, d_ref, mo_ref, vo_ref):
        g_ = g_ref[...]
        m_ = ADAM_B1 * m_ref[...] + (1.0 - ADAM_B1) * g_
        v_ = ADAM_B2 * v_ref[...] + (1.0 - ADAM_B2) * (g_ * g_)
        m_hat = m_ / c1
        v_hat = v_ / c2
        d_ref[...] = -ADAM_LR * (m_hat / (jnp.sqrt(v_hat) + ADAM_EPS) + ADAM_WD * w_ref[...])
        mo_ref[...] = m_
        vo_ref[...] = v_

    spec = pl.BlockSpec((tb, b), lambda i: (i, 0))
    return pl.pallas_call(
        body, name=name,
        out_shape=[jax.ShapeDtypeStruct((a, b), F32)] * 3,
        grid=(a // tb,), in_specs=[spec] * 4, out_specs=[spec] * 3,
        compiler_params=_params(("arbitrary",)),
    )(w, g, m, v)


def adamw_into(w_all, g, m_all, v_all, prev, l, name):
    nl, a, b = w_all.shape
    tb = _tile(a, 256)
    c1 = 1.0 - ADAM_B1 ** ADAM_STEP
    c2 = 1.0 - ADAM_B2 ** ADAM_STEP

    def body(w_ref, g_ref, m_ref, v_ref, p0, p1, p2, p3, go_ref, d_ref, mo_ref, vo_ref):
        g_ = g_ref[...]
        m_ = ADAM_B1 * m_ref[0] + (1.0 - ADAM_B1) * g_
        v_ = ADAM_B2 * v_ref[0] + (1.0 - ADAM_B2) * (g_ * g_)
        m_hat = m_ / c1
        v_hat = v_ / c2
        go_ref[0] = g_
        d_ref[0] = -ADAM_LR * (m_hat / (jnp.sqrt(v_hat) + ADAM_EPS) + ADAM_WD * w_ref[0])
        mo_ref[0] = m_
        vo_ref[0] = v_

    lay = pl.BlockSpec((1, tb, b), lambda i: (l, i, 0))
    return pl.pallas_call(
        body, name=name,
        out_shape=[jax.ShapeDtypeStruct((nl, a, b), F32)] * 4,
        grid=(a // tb,), in_specs=[lay, pl.BlockSpec((tb, b), lambda i: (i, 0)), lay, lay] + [ANY] * 4,
        out_specs=[lay] * 4, input_output_aliases={4: 0, 5: 1, 6: 2, 7: 3},
        compiler_params=_params(("arbitrary",)),
    )(w_all, g, m_all, v_all, *prev)


def reduce_scatter(gs, c_arr, s_arr, tag):
    rec_a = sibling_swap_halves(gs, f"rs_swap_{tag}")
    ps = [add_own_half(g, r, c_arr, f"rs_add_{tag}_{k}") for k, (g, r) in enumerate(zip(gs, rec_a))]
    rec_b = chip_exchange(ps, f"rs_ici_{tag}")
    ts = [sum_four(p, r, s_arr, c_arr, f"rs_sum_{tag}_{k}") for k, (p, r) in enumerate(zip(ps, rec_b))]
    return sibling_join_halves(ts, f"rs_join_{tag}")


HBM_SPEC = pl.BlockSpec(memory_space=pltpu.HBM)
SEM_SPEC = pl.BlockSpec(memory_space=pltpu.SEMAPHORE)
EFFECT = pltpu.SideEffectType.DATAFLOW_SIDE_EFFECTING


def _in_hbm(a):
    return pltpu.with_memory_space_constraint(a, pltpu.HBM)


def _exchange_copy(k, j, chip, c, s, srcs, lands, send, recv, gather, receiving):
    cx, cy = chip
    src = srcs[k] if gather else srcs[k].at[2 * cx + cy]
    if gather:
        dst = lands[k].at[2 * cx + cy] if receiving else lands[k].at[s]
    else:
        dst = lands[k].at[j]
    return pltpu.make_async_remote_copy(src_ref=src, dst_ref=dst, send_sem=send.at[3 * k + j], recv_sem=recv.at[3 * k + j],
                                        device_id=(cx, cy, c), device_id_type=MESH)


def exchange_start(srcs, lands, name, gather):
    n = len(srcs)

    def body(*refs):
        srcs_in, lands_in = refs[:n], refs[n:2 * n]
        send, recv = refs[2 * n], refs[2 * n + 1]
        token = refs[-1]
        x, y, c, chips = _place()
        for k in range(n):
            for j, chip in enumerate(chips):
                _exchange_copy(k, j, chip, c, 2 * x + y, srcs_in, lands_in, send, recv, gather, False).start()
        token[...] = jnp.zeros_like(token)

    outs = pl.pallas_call(
        body, name=name,
        out_shape=(pltpu.SemaphoreType.DMA((3 * n,)), pltpu.SemaphoreType.DMA((3 * n,)),
                   *[pltpu.HBM(a.shape, a.dtype) for a in srcs], *[pltpu.HBM(a.shape, a.dtype) for a in lands],
                   jax.ShapeDtypeStruct((SUBLANES, LANES), F32)),
        in_specs=[HBM_SPEC] * (2 * n),
        out_specs=(SEM_SPEC, SEM_SPEC, *[HBM_SPEC] * (2 * n), pl.BlockSpec(memory_space=pltpu.VMEM)),
        input_output_aliases={k: 2 + k for k in range(2 * n)},
        compiler_params=pltpu.CompilerParams(has_side_effects=EFFECT),
    )(*[_in_hbm(a) for a in srcs], *[_in_hbm(a) for a in lands])
    return outs[0], outs[1], list(outs[2:2 + n]), list(outs[2 + n:2 + 2 * n]), outs[-1]


def exchange_wait(state, after, name, gather):
    send, recv, srcs, lands, _ = state
    n = len(srcs)

    def body(*refs):
        srcs_in, lands_in = refs[:n], refs[n:2 * n]
        send_, recv_ = refs[2 * n], refs[2 * n + 1]
        x, y, c, chips = _place()
        for k in range(n):
            for j, chip in enumerate(chips):
                cp = _exchange_copy(k, j, chip, c, 2 * x + y, srcs_in, lands_in, send_, recv_, gather, True)
                cp.wait_send()
                cp.wait_recv()

    outs = pl.pallas_call(
        body, name=name,
        out_shape=tuple(pltpu.HBM(a.shape, a.dtype) for a in srcs + lands),
        in_specs=[HBM_SPEC] * (2 * n) + [SEM_SPEC, SEM_SPEC, ANY],
        out_specs=tuple([HBM_SPEC] * (2 * n)),
        input_output_aliases={k: k for k in range(2 * n)},
        compiler_params=pltpu.CompilerParams(has_side_effects=EFFECT),
    )(*srcs, *lands, send, recv, after)
    return list(outs[:n]), list(outs[n:])


def rs_begin(gs, c_arr, tag):
    rec_a = sibling_swap_halves(gs, f"rs_swap_{tag}")
    ps = [add_own_half(g, r, c_arr, f"rs_add_{tag}_{k}") for k, (g, r) in enumerate(zip(gs, rec_a))]
    lands = [lax.empty((3,) + p.shape[1:], p.dtype) for p in ps]
    return exchange_start(ps, lands, f"rs_ici_start_{tag}", False)


def rs_finish(state, after, s_arr, c_arr, tag):
    ps, rec_b = exchange_wait(state, after, f"rs_ici_wait_{tag}", False)
    ts = [sum_four(p, r, s_arr, c_arr, f"rs_sum_{tag}_{k}") for k, (p, r) in enumerate(zip(ps, rec_b))]
    return sibling_join_halves(ts, f"rs_join_{tag}")


def ffn_fwd(h, gamma, wg, wu, wd, name):
    r, d = h.shape
    ns, fs, _ = wg.shape
    tm = _row_tile(r)

    def body(h_ref, g_ref, wg_ref, wu_ref, wd_ref, ho_ref, gg_ref, uu_ref, hn_sc, acc_sc):
        s = pl.program_id(1)

        @pl.when(s == 0)
        def _():
            xh, _ = _rms(h_ref[...])
            hn_sc[...] = (xh * g_ref[...]).astype(BF16)
            acc_sc[...] = jnp.zeros_like(acc_sc)

        hn = hn_sc[...]
        g = lax.dot_general(hn, wg_ref[0], NT, preferred_element_type=F32)
        u = lax.dot_general(hn, wu_ref[0], NT, preferred_element_type=F32)
        gg_ref[0] = g.astype(BF16)
        uu_ref[0] = u.astype(BF16)
        a = (g * _sigmoid(g) * u).astype(BF16)
        acc_sc[...] += jnp.dot(a, wd_ref[0], preferred_element_type=F32)

        @pl.when(s == ns - 1)
        def _():
            ho_ref[...] = h_ref[...] + 0.5 * acc_sc[...]

    return pl.pallas_call(
        body, name=name,
        out_shape=[jax.ShapeDtypeStruct((r, d), F32), jax.ShapeDtypeStruct((ns, r, fs), BF16),
                   jax.ShapeDtypeStruct((ns, r, fs), BF16)],
        grid=(r // tm, ns),
        in_specs=[pl.BlockSpec((tm, d), lambda i, s: (i, 0)), pl.BlockSpec((1, d), lambda i, s: (0, 0)),
                  pl.BlockSpec((1, fs, d), lambda i, s: (s, 0, 0)), pl.BlockSpec((1, fs, d), lambda i, s: (s, 0, 0)),
                  pl.BlockSpec((1, fs, d), lambda i, s: (s, 0, 0))],
        out_specs=[pl.BlockSpec((tm, d), lambda i, s: (i, 0)), pl.BlockSpec((1, tm, fs), lambda i, s: (s, i, 0)),
                   pl.BlockSpec((1, tm, fs), lambda i, s: (s, i, 0))],
        scratch_shapes=[pltpu.VMEM((tm, d), BF16), pltpu.VMEM((tm, d), F32)],
        compiler_params=_params(("arbitrary", "arbitrary"), VMEM_BIG),
    )(h, gamma, wg, wu, wd)


def ffn_bwd_act(h, gamma, dh, gg, uu, wg, wu, wd, name):
    r, d = h.shape
    ns, fs, _ = wg.shape
    tm = _row_tile(r)

    def body(h_ref, g_ref, dh_ref, gg_ref, uu_ref, wg_ref, wu_ref, wd_ref,
             dho_ref, dgam_ref, hn_ref, dy_ref, dg_ref, du_ref, a_ref, acc_sc):
        i, s = pl.program_id(0), pl.program_id(1)

        @pl.when(s == 0)
        def _():
            xh, _ = _rms(h_ref[...])
            hn_ref[...] = (xh * g_ref[...]).astype(BF16)
            dy_ref[...] = (0.5 * dh_ref[...]).astype(BF16)
            acc_sc[...] = jnp.zeros_like(acc_sc)

        @pl.when((i == 0) & (s == 0))
        def _():
            dgam_ref[...] = jnp.zeros_like(dgam_ref)

        g = gg_ref[0].astype(F32)
        u = uu_ref[0].astype(F32)
        da = lax.dot_general(dy_ref[...], wd_ref[0], NT, preferred_element_type=F32)
        sig = _sigmoid(g)
        sl = g * sig
        a_ref[0] = (sl * u).astype(BF16)
        du = (da * sl).astype(BF16)
        dg = (da * u * (sig * (1.0 + g * (1.0 - sig)))).astype(BF16)
        dg_ref[0] = dg
        du_ref[0] = du
        acc_sc[...] += (jnp.dot(dg, wg_ref[0], preferred_element_type=F32)
                        + jnp.dot(du, wu_ref[0], preferred_element_type=F32))

        @pl.when(s == ns - 1)
        def _():
            xh, rstd = _rms(h_ref[...])
            dhn = acc_sc[...]
            dgam_ref[...] += jnp.sum(dhn * xh, axis=0, keepdims=True)
            dho_ref[...] = dh_ref[...] + _rms_bwd(xh, rstd, dhn * g_ref[...])

    row = pl.BlockSpec((tm, d), lambda i, s: (i, 0))
    act = pl.BlockSpec((1, tm, fs), lambda i, s: (s, i, 0))
    return pl.pallas_call(
        body, name=name,
        out_shape=[jax.ShapeDtypeStruct((r, d), F32), jax.ShapeDtypeStruct((1, d), F32),
                   jax.ShapeDtypeStruct((r, d), BF16), jax.ShapeDtypeStruct((r, d), BF16),
                   jax.ShapeDtypeStruct((ns, r, fs), BF16), jax.ShapeDtypeStruct((ns, r, fs), BF16),
                   jax.ShapeDtypeStruct((ns, r, fs), BF16)],
        grid=(r // tm, ns),
        in_specs=[row, pl.BlockSpec((1, d), lambda i, s: (0, 0)), row, act, act,
                  pl.BlockSpec((1, fs, d), lambda i, s: (s, 0, 0)), pl.BlockSpec((1, fs, d), lambda i, s: (s, 0, 0)),
                  pl.BlockSpec((1, fs, d), lambda i, s: (s, 0, 0))],
        out_specs=[row, pl.BlockSpec((1, d), lambda i, s: (0, 0)), row, row, act, act, act],
        scratch_shapes=[pltpu.VMEM((tm, d), F32)],
        compiler_params=_params(("arbitrary", "arbitrary"), VMEM_BIG),
    )(h, gamma, dh, gg, uu, wg, wu, wd)


def ffn_bwd_weights(hn, dy, a, dg, du, name):
    r, d = hn.shape
    ns, _, fs = a.shape
    tm = WGRAD_ROWS if r % WGRAD_ROWS == 0 else _row_tile(r)

    def body(hn_ref, dy_ref, a_ref, dg_ref, du_ref, wg_ref, wu_ref, wd_ref):
        @pl.when(pl.program_id(1) == 0)
        def _():
            wg_ref[...] = jnp.zeros_like(wg_ref)
            wu_ref[...] = jnp.zeros_like(wu_ref)
            wd_ref[...] = jnp.zeros_like(wd_ref)

        hn_ = hn_ref[...]
        wg_ref[0] += lax.dot_general(dg_ref[0], hn_, TN, preferred_element_type=F32)
        wu_ref[0] += lax.dot_general(du_ref[0], hn_, TN, preferred_element_type=F32)
        wd_ref[0] += lax.dot_general(a_ref[0], dy_ref[...], TN, preferred_element_type=F32)

    row = pl.BlockSpec((tm, d), lambda s, i: (i, 0))
    act = pl.BlockSpec((1, tm, fs), lambda s, i: (s, i, 0))
    wsp = pl.BlockSpec((1, fs, d), lambda s, i: (s, 0, 0))
    return pl.pallas_call(
        body, name=name,
        out_shape=[jax.ShapeDtypeStruct((ns, fs, d), F32)] * 3,
        grid=(ns, r // tm),
        in_specs=[row, row, act, act, act],
        out_specs=[wsp, wsp, wsp],
        compiler_params=_params(("arbitrary", "arbitrary"), VMEM_MAX),
    )(hn, dy, a, dg, du)


def norm_fwd(x, gamma, name):
    r = x.shape[0]
    w = gamma.shape[1]
    tm = _row_tile(r)

    def body(x_ref, g_ref, o_ref):
        xh, _ = _rms(x_ref[...])
        o_ref[...] = (xh * g_ref[...]).astype(BF16)

    return pl.pallas_call(
        body, name=name, out_shape=jax.ShapeDtypeStruct((r, w), BF16), grid=(r // tm,),
        in_specs=[pl.BlockSpec((tm, w), lambda i: (i, 0)), pl.BlockSpec((1, w), lambda i: (0, 0))],
        out_specs=pl.BlockSpec((tm, w), lambda i: (i, 0)),
        compiler_params=_params(("arbitrary",)),
    )(x, gamma)


def norm_bwd(x, gamma, dy, dres, name):
    r = x.shape[0]
    w = gamma.shape[1]
    tm = _row_tile(r)
    has_res = dres is not None

    def body(*refs):
        if has_res:
            x_ref, g_ref, dy_ref, dr_ref, dx_ref, dgam_ref = refs
        else:
            x_ref, g_ref, dy_ref, dx_ref, dgam_ref = refs

        @pl.when(pl.program_id(0) == 0)
        def _():
            dgam_ref[...] = jnp.zeros_like(dgam_ref)

        xh, rstd = _rms(x_ref[...])
        dy_ = dy_ref[...].astype(F32)
        dgam_ref[...] += jnp.sum(dy_ * xh, axis=0, keepdims=True)
        dx = _rms_bwd(xh, rstd, dy_ * g_ref[...])
        if has_res:
            dx = dx + dr_ref[...]
        dx_ref[...] = dx

    row = pl.BlockSpec((tm, w), lambda i: (i, 0))
    vec = pl.BlockSpec((1, w), lambda i: (0, 0))
    ins = [x, gamma, dy] + ([dres] if has_res else [])
    return pl.pallas_call(
        body, name=name,
        out_shape=[jax.ShapeDtypeStruct((r, w), F32), jax.ShapeDtypeStruct((1, w), F32)],
        grid=(r // tm,), in_specs=[row, vec, row] + ([row] if has_res else []), out_specs=[row, vec],
        compiler_params=_params(("arbitrary",)),
    )(*ins)


def rowmm(a, w, name, *, nt=False, res=None, out_dtype=F32, heads_out=False):
    ha, r, ka = a.shape
    hw = w.shape[0]
    nh = max(ha, hw)
    n = w.shape[1] if nt else w.shape[2]
    tm = _row_tile(r)
    dims = NT if nt else (((1,), (0,)), ((), ()))
    has_res = res is not None

    def body(*refs):
        if has_res:
            a_ref, w_ref, r_ref, o_ref = refs
        else:
            a_ref, w_ref, o_ref = refs
        shared = a_ref[0].astype(BF16) if ha == 1 else None
        acc = None
        for h in range(nh):
            lhs = shared if ha == 1 else a_ref[h].astype(BF16)
            p = lax.dot_general(lhs, w_ref[h if hw > 1 else 0], dims, preferred_element_type=F32)
            if heads_out:
                o_ref[h] = p.astype(out_dtype)
            else:
                acc = p if acc is None else acc + p
        if not heads_out:
            if has_res:
                acc = acc + r_ref[...]
            o_ref[...] = acc.astype(out_dtype)

    in_specs = [pl.BlockSpec((ha, tm, ka), lambda i: (0, i, 0)), pl.BlockSpec(w.shape, lambda i: (0, 0, 0))]
    ins = [a, w]
    if has_res:
        in_specs.append(pl.BlockSpec((tm, n), lambda i: (i, 0)))
        ins.append(res)
    if heads_out:
        out_shape = jax.ShapeDtypeStruct((nh, r, n), out_dtype)
        out_spec = pl.BlockSpec((nh, tm, n), lambda i: (0, i, 0))
    else:
        out_shape = jax.ShapeDtypeStruct((r, n), out_dtype)
        out_spec = pl.BlockSpec((tm, n), lambda i: (i, 0))
    return pl.pallas_call(
        body, name=name, out_shape=out_shape, grid=(r // tm,), in_specs=in_specs, out_specs=out_spec,
        compiler_params=_params(("arbitrary",), VMEM_BIG),
    )(*ins)


def tnmm(a, b, name):
    ha, r, ka = a.shape
    hb, _, nb = b.shape
    nh = max(ha, hb)
    tm = _row_tile(r)

    def body(a_ref, b_ref, o_ref):
        @pl.when(pl.program_id(0) == 0)
        def _():
            o_ref[...] = jnp.zeros_like(o_ref)

        a_shared = a_ref[0].astype(BF16) if ha == 1 else None
        b_shared = b_ref[0].astype(BF16) if hb == 1 else None
        for h in range(nh):
            lhs = a_shared if ha == 1 else a_ref[h].astype(BF16)
            rhs = b_shared if hb == 1 else b_ref[h].astype(BF16)
            o_ref[h] += lax.dot_general(lhs, rhs, TN, preferred_element_type=F32)

    return pl.pallas_call(
        body, name=name, out_shape=jax.ShapeDtypeStruct((nh, ka, nb), F32), grid=(r // tm,),
        in_specs=[pl.BlockSpec((ha, tm, ka), lambda i: (0, i, 0)), pl.BlockSpec((hb, tm, nb), lambda i: (0, i, 0))],
        out_specs=pl.BlockSpec((nh, ka, nb), lambda i: (0, 0, 0)),
        compiler_params=_params(("arbitrary",), VMEM_BIG),
    )(a, b)


def rope_tables(r):
    inv = 1.0 / (ROPE_THETA ** (jnp.arange(0, QK_ROPE, 2, dtype=F32) / QK_ROPE))
    pos = (jnp.arange(r, dtype=F32) - META_ROW0)[:, None]
    ang = pos * inv[None, :]
    cos, sin = jnp.cos(ang), jnp.sin(ang)
    ones = jnp.ones((r, HEAD_W - QK_ROPE), F32)
    ctab = jnp.concatenate([cos, cos, ones], axis=1)
    stab = jnp.concatenate([-sin, sin, jnp.zeros_like(ones)], axis=1)
    return ctab, stab


def _swap_halves(z):
    lane = lax.broadcasted_iota(jnp.int32, z.shape, 1)
    up = pltpu.roll(z, HEAD_W - QK_ROPE // 2, 1)
    down = pltpu.roll(z, QK_ROPE // 2, 1)
    return jnp.where(lane < QK_ROPE // 2, up, jnp.where(lane < QK_ROPE, down, 0.0))


def proj_rope(a, w, ctab, stab, extra, name, scale=1.0):
    r, ka = a.shape
    nh = w.shape[0]
    tm = _row_tile(r)
    has_extra = extra is not None

    def body(*refs):
        if has_extra:
            a_ref, w_ref, c_ref, s_ref, e_ref, o_ref = refs
        else:
            a_ref, w_ref, c_ref, s_ref, o_ref = refs
        a_ = a_ref[...]
        ctab_, stab_ = c_ref[...], s_ref[...]
        if scale != 1.0:
            ctab_, stab_ = ctab_ * scale, stab_ * scale
        for h in range(nh):
            x = jnp.dot(a_, w_ref[h], preferred_element_type=F32)
            if has_extra:
                x = x + e_ref[...]
            o_ref[h] = (x * ctab_ + _swap_halves(x) * stab_).astype(BF16)

    tab = pl.BlockSpec((tm, HEAD_W), lambda i: (i, 0))
    in_specs = [pl.BlockSpec((tm, ka), lambda i: (i, 0)), pl.BlockSpec((nh, ka, HEAD_W), lambda i: (0, 0, 0)),
                tab, tab]
    ins = [a, w, ctab, stab]
    if has_extra:
        in_specs.append(pl.BlockSpec((tm, HEAD_W), lambda i: (i, 2)))
        ins.append(extra)
    return pl.pallas_call(
        body, name=name, out_shape=jax.ShapeDtypeStruct((nh, r, HEAD_W), BF16), grid=(r // tm,),
        in_specs=in_specs, out_specs=pl.BlockSpec((nh, tm, HEAD_W), lambda i: (0, i, 0)),
        compiler_params=_params(("arbitrary",)),
    )(*ins)


def rope_bwd_heads(d, ctab, stab, name, scale=1.0):
    nh, r, _ = d.shape
    tm = _row_tile(r)

    def body(d_ref, c_ref, s_ref, o_ref):
        ctab_, stab_ = c_ref[...], s_ref[...]
        if scale != 1.0:
            ctab_, stab_ = ctab_ * scale, stab_ * scale
        for h in range(nh):
            d_ = d_ref[h]
            o_ref[h] = (d_ * ctab_ + _swap_halves(d_ * stab_)).astype(BF16)

    tab = pl.BlockSpec((tm, HEAD_W), lambda i: (i, 0))
    blk = pl.BlockSpec((nh, tm, HEAD_W), lambda i: (0, i, 0))
    return pl.pallas_call(
        body, name=name, out_shape=jax.ShapeDtypeStruct((nh, r, HEAD_W), BF16), grid=(r // tm,),
        in_specs=[blk, tab, tab], out_specs=blk,
        compiler_params=_params(("arbitrary",)),
    )(d, ctab, stab)


def rope_bwd_sum(d, ctab, stab, name):
    nh, r, _ = d.shape
    tm = _row_tile(r)

    def body(d_ref, c_ref, s_ref, o_ref):
        d_ = d_ref[0]
        for h in range(1, nh):
            d_ = d_ + d_ref[h]
        lane = lax.broadcasted_iota(jnp.int32, d_.shape, 1)
        g = d_ * c_ref[...] + _swap_halves(d_ * s_ref[...])
        o_ref[...] = jnp.where(lane < QK_ROPE, g, 0.0)

    tab = pl.BlockSpec((tm, HEAD_W), lambda i: (i, 0))
    return pl.pallas_call(
        body, name=name, out_shape=jax.ShapeDtypeStruct((r, HEAD_W), F32), grid=(r // tm,),
        in_specs=[pl.BlockSpec((nh, tm, HEAD_W), lambda i: (0, i, 0)), tab, tab], out_specs=tab,
        compiler_params=_params(("arbitrary",)),
    )(d, ctab, stab)


def _attn_tiles(r):
    t = _row_tile(r)
    return t, t


def _mask(q0, k0, nq_, nk_, keys_on_rows=False):
    shape = (nk_, nq_) if keys_on_rows else (nq_, nk_)
    rq = q0 + lax.broadcasted_iota(jnp.int32, shape, 1 if keys_on_rows else 0)
    rk = k0 + lax.broadcasted_iota(jnp.int32, shape, 0 if keys_on_rows else 1)
    return ((rk >> CHUNK_SHIFT) <= (rq >> CHUNK_SHIFT)) & (rk >= META_ROW0)


ATTN_SUB = 640


SM_SCALE = 1.0 / math.sqrt(QK_NOPE + QK_ROPE)
LOG2E = math.log2(math.e)
Q_SCALE = SM_SCALE * LOG2E


def attn_fwd(q, k, v, name):
    nh, r, dk = q.shape
    dv = v.shape[-1]
    tq, tk = _attn_tiles(r)
    nq, nk = r // tq, r // tk

    def last_k(i):
        return ((i + 1) * tq - 1) // tk

    pairs = [(i, j) for i in range(nq) for j in range(last_k(i) + 1)]
    qi_tab = jnp.asarray([p[0] for p in pairs], jnp.int32)
    kj_tab = jnp.asarray([p[1] for p in pairs], jnp.int32)
    sub = _tile(tq, ATTN_SUB)

    def body(qi_ref, kj_ref, q_ref, k_ref, v_ref, o_ref, lse_ref, m_sc, l_sc, acc_sc):
        t = pl.program_id(1)
        i, j = qi_ref[t], kj_ref[t]

        @pl.when(j == 0)
        def _():
            m_sc[...] = jnp.full_like(m_sc, NEG)
            l_sc[...] = jnp.zeros_like(l_sc)
            acc_sc[...] = jnp.zeros_like(acc_sc)

        def step(masked):
            k_, v_ = k_ref[0], v_ref[0]
            for r0 in range(0, tq, sub):
                rows = slice(r0, r0 + sub)
                s = lax.dot_general(q_ref[0, rows, :], k_, NT, preferred_element_type=F32)
                if masked:
                    s = jnp.where(_mask(i * tq + r0, j * tk, sub, tk), s, NEG)
                m_old = m_sc[rows, :]
                m_new = jnp.maximum(m_old, jnp.max(s, axis=-1, keepdims=True))
                alpha = jnp.exp2(m_old - m_new)
                p = jnp.exp2(s - jnp.tile(m_new, (1, tk // LANES)))
                l_sc[rows, :] = alpha * l_sc[rows, :] + jnp.sum(p, axis=-1, keepdims=True)
                acc_sc[rows, :] = (alpha[:, :dv] * acc_sc[rows, :]
                                   + jnp.dot(p.astype(BF16), v_, preferred_element_type=F32))
                m_sc[rows, :] = m_new

        needs_mask = (j == last_k(i)) | (j == 0)
        pl.when(needs_mask)(functools.partial(step, True))
        pl.when(jnp.logical_not(needs_mask))(functools.partial(step, False))

        @pl.when(j == last_k(i))
        def _():
            l = l_sc[...]
            o_ref[0] = (acc_sc[...] / l[:, :dv]).astype(BF16)
            lse_ref[0] = (m_sc[...] + jnp.log2(l))[:, :1]

    qspec = lambda w: pl.BlockSpec((1, tq, w), lambda h, t, qi, kj: (h, qi[t], 0))
    kspec = lambda w: pl.BlockSpec((1, tk, w), lambda h, t, qi, kj: (h, kj[t], 0))
    return pl.pallas_call(
        body, name=name,
        out_shape=[jax.ShapeDtypeStruct((nh, r, dv), BF16), jax.ShapeDtypeStruct((nh, r, 1), F32)],
        grid_spec=pltpu.PrefetchScalarGridSpec(
            num_scalar_prefetch=2, grid=(nh, len(pairs)),
            in_specs=[qspec(dk), kspec(dk), kspec(dv)], out_specs=[qspec(dv), qspec(1)],
            scratch_shapes=[pltpu.VMEM((tq, LANES), F32), pltpu.VMEM((tq, LANES), F32), pltpu.VMEM((tq, dv), F32)]),
        compiler_params=_params(("arbitrary", "arbitrary")),
    )(qi_tab, kj_tab, q, k, v)


def attn_bwd(q, k, v, do, lse, delta, dk_prev, dv_prev, name):
    nh, r, dk = q.shape
    dv = v.shape[-1]
    tq, tk = _attn_tiles(r)
    nq, nk = r // tq, r // tk
    has_prev = dk_prev is not None

    def first_q(j):
        return (j * tk) // tq

    pairs = [(j, i) for j in range(nk) for i in range(first_q(j), nq)]
    kj_tab = jnp.asarray([p[0] for p in pairs], jnp.int32)
    qi_tab = jnp.asarray([p[1] for p in pairs], jnp.int32)
    sub = _tile(tk, ATTN_SUB)

    def body(*refs):
        if has_prev:
            (kj_ref, qi_ref, q_ref, k_ref, v_ref, do_ref, lse_ref, dl_ref, pk_ref, pv_ref,
             dq_ref, dk_ref, dv_ref, dk_sc, dv_sc) = refs
        else:
            (kj_ref, qi_ref, q_ref, k_ref, v_ref, do_ref, lse_ref, dl_ref,
             dq_ref, dk_ref, dv_ref, dk_sc, dv_sc) = refs
        t = pl.program_id(1)
        j, i = kj_ref[t], qi_ref[t]

        @pl.when(t == 0)
        def _():
            dq_ref[...] = jnp.zeros_like(dq_ref)

        @pl.when(i == first_q(j))
        def _():
            dk_sc[...] = jnp.zeros_like(dk_sc)
            dv_sc[...] = jnp.zeros_like(dv_sc)

        def step(masked):
            q_, do_ = q_ref[0], do_ref[0]
            lse_, dl_ = lse_ref[0], dl_ref[0]
            dq_acc = None
            for r0 in range(0, tk, sub):
                rows = slice(r0, r0 + sub)
                k_ = k_ref[0, rows, :]
                st = lax.dot_general(k_, q_, NT, preferred_element_type=F32)
                if masked:
                    st = jnp.where(_mask(i * tq, j * tk + r0, tq, sub, keys_on_rows=True), st, NEG)
                pt = jnp.exp2(st - lse_)
                dv_sc[rows, :] += jnp.dot(pt.astype(BF16), do_, preferred_element_type=F32)
                dpt = lax.dot_general(v_ref[0, rows, :], do_, NT, preferred_element_type=F32)
                dst = (pt * (dpt - dl_)).astype(BF16)
                dk_sc[rows, :] += jnp.dot(dst, q_, preferred_element_type=F32)
                part = lax.dot_general(dst, k_, TN, preferred_element_type=F32)
                dq_acc = part if dq_acc is None else dq_acc + part
            qrows = pl.ds(pl.multiple_of(i * tq, tq), tq)
            dq_ref[0, qrows, :] += dq_acc

        needs_mask = (i == first_q(j)) | (j == 0)
        pl.when(needs_mask)(functools.partial(step, True))
        pl.when(jnp.logical_not(needs_mask))(functools.partial(step, False))

        @pl.when(i == nq - 1)
        def _():
            dk_ = dk_sc[...] * (1.0 / LOG2E)
            dv_ = dv_sc[...]
            if has_prev:
                dk_ = dk_ + pk_ref[0]
                dv_ = dv_ + pv_ref[0]
            dk_ref[0] = dk_
            dv_ref[0] = dv_

    krow = lambda w: pl.BlockSpec((1, tk, w), lambda h, t, kj, qi: (h, kj[t], 0))
    qrow = lambda w: pl.BlockSpec((1, tq, w), lambda h, t, kj, qi: (h, qi[t], 0))
    qvec = pl.BlockSpec((1, 1, tq), lambda h, t, kj, qi: (h, 0, qi[t]))
    in_specs = [qrow(dk), krow(dk), krow(dv), qrow(dv), qvec, qvec]
    ins = [q, k, v, do, lse, delta]
    if has_prev:
        in_specs += [krow(dk), krow(dv)]
        ins += [dk_prev, dv_prev]
    return pl.pallas_call(
        body, name=name,
        out_shape=[jax.ShapeDtypeStruct((nh, r, dk), F32), jax.ShapeDtypeStruct((nh, r, dk), F32),
                   jax.ShapeDtypeStruct((nh, r, dv), F32)],
        grid_spec=pltpu.PrefetchScalarGridSpec(
            num_scalar_prefetch=2, grid=(nh, len(pairs)), in_specs=in_specs,
            out_specs=[pl.BlockSpec((1, r, dk), lambda h, t, kj, qi: (h, 0, 0)), krow(dk), krow(dv)],
            scratch_shapes=[pltpu.VMEM((tk, dk), F32), pltpu.VMEM((tk, dv), F32)]),
        compiler_params=_params(("arbitrary", "arbitrary"), VMEM_BIG),
    )(kj_tab, qi_tab, *ins)


def attn_out_bwd(dattn, wo, o, name):
    r, d = dattn.shape
    nh, dv, _ = wo.shape
    tm = _row_tile(r)

    def body(da_ref, w_ref, o_ref, do_ref, dl_ref):
        da = da_ref[...].astype(BF16)
        for h in range(nh):
            do_ = lax.dot_general(da, w_ref[h], NT, preferred_element_type=F32).astype(BF16)
            do_ref[h] = do_
            dl_ref[h] = jnp.sum(do_.astype(F32) * o_ref[h].astype(F32), axis=-1, keepdims=True)

    return pl.pallas_call(
        body, name=name,
        out_shape=[jax.ShapeDtypeStruct((nh, r, dv), BF16), jax.ShapeDtypeStruct((nh, r, 1), F32)],
        grid=(r // tm,),
        in_specs=[pl.BlockSpec((tm, d), lambda i: (i, 0)), pl.BlockSpec((nh, dv, d), lambda i: (0, 0, 0)),
                  pl.BlockSpec((nh, tm, dv), lambda i: (0, i, 0))],
        out_specs=[pl.BlockSpec((nh, tm, dv), lambda i: (0, i, 0)), pl.BlockSpec((nh, tm, 1), lambda i: (0, i, 0))],
        compiler_params=_params(("arbitrary",)),
    )(dattn, wo, o)


def _pool_counts(row0, n, window):
    rows = row0 + lax.broadcasted_iota(jnp.int32, (n, 1), 0)
    cnt = jnp.clip(rows - META_ROW0 + 1, 1, window)
    return 1.0 / cnt.astype(F32)


def pool_fwd(h, gamma, wp, scale, name):
    r, d = h.shape
    ng, cg, _ = wp.shape
    tm = _row_tile(r)
    hb = tm // HALO

    def body(h_ref, hp_ref, g_ref, w_ref, sc_ref, o_ref):
        i = pl.program_id(0)
        xm = h_ref[...]
        xp = hp_ref[...] * jnp.where(i > 0, 1.0, 0.0)
        xx = jnp.concatenate([xp, xm], axis=0)
        xh, _ = _rms(xx)
        u = xh * g_ref[...]
        for g, win in enumerate(POOL_WINDOWS):
            sl = slice(g * cg, (g + 1) * cg)
            ug = u[:, sl]
            acc, k = ug, 1
            while k < win:
                acc = acc + pltpu.roll(acc, k, 0)
                k *= 2
            pooled = acc[HALO:] * _pool_counts(i * tm, tm, win) - ug[HALO:]
            y = jnp.dot(pooled.astype(BF16), w_ref[g], preferred_element_type=F32)
            o_ref[:, sl] = xm[:, sl] + y * sc_ref[:, sl]

    return pl.pallas_call(
        body, name=name, out_shape=jax.ShapeDtypeStruct((r, d), F32), grid=(r // tm,),
        in_specs=[pl.BlockSpec((tm, d), lambda i: (i, 0)),
                  pl.BlockSpec((HALO, d), lambda i: (jnp.maximum(i * hb - 1, 0), 0)),
                  pl.BlockSpec((1, d), lambda i: (0, 0)), pl.BlockSpec((ng, cg, cg), lambda i: (0, 0, 0)),
                  pl.BlockSpec((1, d), lambda i: (0, 0))],
        out_specs=pl.BlockSpec((tm, d), lambda i: (i, 0)),
        compiler_params=_params(("arbitrary",), VMEM_BIG),
    )(h, h, gamma, wp, scale)


def pool_bwd(h, gamma, wp, scale, dh, name):
    r, d = h.shape
    ng, cg, _ = wp.shape
    tm = _row_tile(r)
    hb = tm // HALO
    nt = r // tm

    def body(h_ref, hp_ref, dh_ref, dn_ref, g_ref, w_ref, sc_ref, dx_ref, dgam_ref, dw_ref, dsc_ref, du_sc):
        i = pl.program_id(0)

        @pl.when(i == 0)
        def _():
            dgam_ref[...] = jnp.zeros_like(dgam_ref)
            dw_ref[...] = jnp.zeros_like(dw_ref)
            dsc_ref[...] = jnp.zeros_like(dsc_ref)

        xm = h_ref[...]
        xp = hp_ref[...] * jnp.where(i > 0, 1.0, 0.0)
        xh_all, rstd_all = _rms(jnp.concatenate([xp, xm], axis=0))
        u = xh_all * g_ref[...]
        dm = dh_ref[...]
        dn = dn_ref[...] * jnp.where(i < nt - 1, 1.0, 0.0)
        dd = jnp.concatenate([dm, dn], axis=0)
        for g, win in enumerate(POOL_WINDOWS):
            sl = slice(g * cg, (g + 1) * cg)
            ug = u[:, sl]
            acc, k = ug, 1
            while k < win:
                acc = acc + pltpu.roll(acc, k, 0)
                k *= 2
            pooled = (acc[HALO:] * _pool_counts(i * tm, tm, win) - ug[HALO:]).astype(BF16)
            y = jnp.dot(pooled, w_ref[g], preferred_element_type=F32)
            dsc_ref[:, sl] += jnp.sum(dm[:, sl] * y, axis=0, keepdims=True)
            dyp = (dd[:, sl] * sc_ref[:, sl]).astype(BF16)
            dw_ref[g] += lax.dot_general(pooled, dyp[:tm], TN, preferred_element_type=F32)
            dpo = lax.dot_general(dyp, w_ref[g], NT, preferred_element_type=F32)
            z = dpo * _pool_counts(i * tm, tm + HALO, win)
            fwd, k = z, 1
            while k < win:
                fwd = fwd + pltpu.roll(fwd, tm + HALO - k, 0)
                k *= 2
            du_sc[:, sl] = fwd[:tm] - dpo[:tm]
        du = du_sc[...]
        xh, rstd = xh_all[HALO:], rstd_all[HALO:]
        dgam_ref[...] += jnp.sum(du * xh, axis=0, keepdims=True)
        dx_ref[...] = dm + _rms_bwd(xh, rstd, du * g_ref[...])

    row = pl.BlockSpec((tm, d), lambda i: (i, 0))
    vec = pl.BlockSpec((1, d), lambda i: (0, 0))
    prev = pl.BlockSpec((HALO, d), lambda i: (jnp.maximum(i * hb - 1, 0), 0))
    nxt = pl.BlockSpec((HALO, d), lambda i: (jnp.minimum((i + 1) * hb, r // HALO - 1), 0))
    wsp = pl.BlockSpec((ng, cg, cg), lambda i: (0, 0, 0))
    return pl.pallas_call(
        body, name=name,
        out_shape=[jax.ShapeDtypeStruct((r, d), F32), jax.ShapeDtypeStruct((1, d), F32),
                   jax.ShapeDtypeStruct((ng, cg, cg), F32), jax.ShapeDtypeStruct((1, d), F32)],
        grid=(nt,), in_specs=[row, prev, row, nxt, vec, wsp, vec], out_specs=[row, vec, wsp, vec],
        scratch_shapes=[pltpu.VMEM((tm, d), F32)],
        compiler_params=_params(("arbitrary",), VMEM_BIG),
    )(h, h, dh, dh, gamma, wp, scale)


def loss_head(h, gamma, target, seq, name):
    r, d = h.shape
    tm = _row_tile(r)

    def body(h_ref, g_ref, t_ref, sse_ref, dh_ref, dgam_ref):
        i = pl.program_id(0)

        @pl.when(i == 0)
        def _():
            sse_ref[...] = jnp.zeros_like(sse_ref)
            dgam_ref[...] = jnp.zeros_like(dgam_ref)

        xh, rstd = _rms(h_ref[...])
        rows = i * tm + lax.broadcasted_iota(jnp.int32, (tm, 1), 0)
        valid = ((rows >= FRONT) & (rows < FRONT + seq)).astype(F32)
        e = (xh * g_ref[...] - t_ref[...]) * valid
        sse_ref[...] += jnp.sum(jnp.sum(e * e, axis=1, keepdims=True), axis=0, keepdims=True)
        dy = e * (1.0 / d)
        dgam_ref[...] += jnp.sum(dy * xh, axis=0, keepdims=True)
        dh_ref[...] = _rms_bwd(xh, rstd, dy * g_ref[...])

    row = pl.BlockSpec((tm, d), lambda i: (i, 0))
    vec = pl.BlockSpec((1, d), lambda i: (0, 0))
    return pl.pallas_call(
        body, name=name,
        out_shape=[jax.ShapeDtypeStruct((1, 1), F32), jax.ShapeDtypeStruct((r, d), F32),
                   jax.ShapeDtypeStruct((1, d), F32)],
        grid=(r // tm,), in_specs=[row, vec, row],
        out_specs=[pl.BlockSpec((1, 1), lambda i: (0, 0)), row, vec],
        compiler_params=_params(("arbitrary",)),
    )(h, gamma, target)


SMALL = ("meta_tokens", "pool_w", "pool_scale", "w_dkv", "w_uk", "w_uv", "w_dq", "w_uq", "w_o")


def _pack(parts):
    flat = jnp.concatenate([p.reshape(-1) for p in parts])
    n = flat.shape[0]
    unit = PACK_W * 2 * SUBLANES
    n_pad = -(-n // unit) * unit
    return jnp.pad(flat, (0, n_pad - n)).reshape(n_pad // PACK_W, PACK_W)


def _unpack(buf, shapes, lead=()):
    flat = buf.reshape(lead + (-1,))
    out, off = [], 0
    for shp in shapes:
        n = math.prod(shp)
        out.append(flat[..., off:off + n].reshape(lead + tuple(shp)))
        off += n
    return out


def _cols_from_shards(a, axis):
    a = jnp.moveaxis(a, 0, axis)
    shp = a.shape
    return a.reshape(shp[:axis] + (shp[axis] * shp[axis + 1],) + shp[axis + 2:])


def _cols_to_shards(a, axis):
    shp = a.shape
    a = a.reshape(shp[:axis] + (N_SHARD, shp[axis] // N_SHARD) + shp[axis + 1:])
    return jnp.moveaxis(a, axis, 0)


SMALL_AXIS = {"meta_tokens": 1, "pool_w": 2, "pool_scale": 1, "w_dkv": 0, "w_uk": 1, "w_uv": 1,
              "w_dq": 1, "w_uq": 2, "w_o": 2}


def kernel(x, meta_tokens, ffn1_norm, ffn1_w_gate, ffn1_w_up, ffn1_w_down, mix_norm, ffn2_norm, ffn2_w_gate, ffn2_w_up, ffn2_w_down, pool_w, pool_scale, kv_in_norm, w_dkv, kv_latent_norm, w_uk, w_uv, w_dq, q_latent_norm, w_uq, w_o, final_norm, loss_target, m_meta_tokens, m_ffn1_norm, m_ffn1_w_gate, m_ffn1_w_up, m_ffn1_w_down, m_mix_norm, m_ffn2_norm, m_ffn2_w_gate, m_ffn2_w_up, m_ffn2_w_down, m_pool_w, m_pool_scale, m_kv_in_norm, m_w_dkv, m_kv_latent_norm, m_w_uk, m_w_uv, m_w_dq, m_q_latent_norm, m_w_uq, m_w_o, m_final_norm, v_meta_tokens, v_ffn1_norm, v_ffn1_w_gate, v_ffn1_w_up, v_ffn1_w_down, v_mix_norm, v_ffn2_norm, v_ffn2_w_gate, v_ffn2_w_up, v_ffn2_w_down, v_pool_w, v_pool_scale, v_kv_in_norm, v_w_dkv, v_kv_latent_norm, v_w_uk, v_w_uv, v_w_dq, v_q_latent_norm, v_w_uq, v_w_o, v_final_norm):
    args = dict(locals())
    W = {n: args[n] for n in NAMES}
    M = {n: args["m_" + n] for n in NAMES}
    V = {n: args["v_" + n] for n in NAMES}
    TRANSPOSED = ("ffn1_w_gate", "ffn1_w_up", "ffn2_w_gate", "ffn2_w_up")
    for n in TRANSPOSED:
        W[n], M[n], V[n] = (jnp.swapaxes(a, 1, 2) for a in (W[n], M[n], V[n]))

    depth = ffn1_norm.shape[0]
    n_a = pool_w.shape[0]
    seq, d = x.shape[1], x.shape[2]
    nh = N_HEADS
    r = -(-(FRONT + seq) // LANES) * LANES

    cx, cy, cc = lax.axis_index("x"), lax.axis_index("y"), lax.axis_index("c")
    c_arr = jnp.reshape(cc, (1,)).astype(jnp.int32)
    s_arr = jnp.reshape(2 * cx + cy, (1,)).astype(jnp.int32)

    small_shapes = [W[n].shape for n in SMALL]
    gathered = all_gather_shards([_pack([W[n] for n in SMALL])], "ag_small")[0]
    small_full = {}
    for n, part in zip(SMALL, _unpack(gathered, small_shapes, (N_SHARD,))):
        small_full[n] = _cols_from_shards(part, SMALL_AXIS[n])

    ffn_src = {f: tuple(W[f + t].astype(BF16) for t in ("_w_gate", "_w_up", "_w_down")) for f in ("ffn1", "ffn2")}
    ffn_order = [(f, l) for l in range(depth) for f in ("ffn1", "ffn2")]
    ffn_w = {}
    gate = [None]
    ag_state = [None]

    def gated(a):
        if gate[0] is not None:
            a = a + gate[0][0, 0].astype(a.dtype)
            gate[0] = None
        return a

    def vec(a):
        return gated(a.reshape(1, -1))

    def ag_start(idx, dep):
        f, l = ffn_order[idx]
        shards = [w_[l] + dep for w_ in ffn_src[f]]
        lands = [lax.dynamic_update_slice(lax.empty((N_SHARD,) + a.shape, BF16), a[None], (2 * cx + cy, 0, 0))
                 for a in shards]
        ag_state[0] = exchange_start(shards, lands, f"ag_start_{f}_{l}", True)
        gate[0] = ag_state[0][4]

    def ag_wait(idx, after):
        f, l = ffn_order[idx]
        _, lands = exchange_wait(ag_state[0], after, f"ag_wait_{f}_{l}", True)
        ffn_w[f, l] = lands
        if idx + 1 < len(ffn_order):
            ag_start(idx + 1, lands[0][0, 0, 0] * jnp.zeros((), BF16))

    ag_start(0, jnp.zeros((), BF16))
    ag_wait(0, gathered)

    meta_full = small_full["meta_tokens"]
    wp = small_full["pool_w"].astype(BF16)
    pscale = small_full["pool_scale"]
    wdkv = jnp.pad(small_full["w_dkv"], ((0, 0), (0, HEAD_W - QK_ROPE))).astype(BF16)[None]
    wuk = small_full["w_uk"].reshape(KV_RANK, nh, QK_NOPE).transpose(1, 0, 2)
    wk_h = jnp.concatenate([jnp.zeros((nh, KV_RANK, HEAD_W - QK_NOPE), F32), wuk], axis=-1).astype(BF16)
    wv_h = small_full["w_uv"].reshape(KV_RANK, nh, V_HEAD).transpose(1, 0, 2).astype(BF16)
    wdq = small_full["w_dq"].astype(BF16)
    wuq = small_full["w_uq"].reshape(-1, Q_RANK, nh, QK_NOPE + QK_ROPE).transpose(0, 2, 1, 3)
    wq_h = jnp.concatenate([wuq[..., QK_NOPE:], jnp.zeros(wuq.shape[:-1] + (HEAD_W - QK_NOPE - QK_ROPE,), F32),
                            wuq[..., :QK_NOPE]], axis=-1).astype(BF16)
    wo_h = small_full["w_o"].reshape(-1, nh, V_HEAD, d).astype(BF16)

    ctab, stab = rope_tables(r)

    h = jnp.concatenate([jnp.zeros((META_ROW0, d), F32), meta_full, x[0],
                         jnp.zeros((r - FRONT - seq, d), F32)], axis=0)
    target = jnp.concatenate([jnp.zeros((FRONT, d), F32), loss_target[0],
                              jnp.zeros((r - FRONT - seq, d), F32)], axis=0)
    saved = []
    kv = None
    for l in range(depth):
        sv = {"h0": h}
        if l > 0:
            ag_wait(2 * l, h)
        h, sv["g1"], sv["u1"] = ffn_fwd(h, vec(ffn1_norm[l]), *ffn_w["ffn1", l], f"ffn1_fwd_{l}")
        sv["h1"] = h
        if l < n_a:
            h = pool_fwd(h, vec(mix_norm[l]), wp[l], vec(pscale[l]), f"pool_fwd_{l}")
        else:
            j = l - n_a
            u = norm_fwd(h, vec(mix_norm[l]), f"mixnorm_{l}")
            cq0 = rowmm(u[None], wdq[j][None], f"dq_{l}")
            cq = norm_fwd(cq0, vec(q_latent_norm[j]), f"qnorm_{l}")
            q = proj_rope(cq, wq_h[j], ctab, stab, None, f"qproj_{l}", scale=Q_SCALE)
            o, lse = attn_fwd(q, kv["k"], kv["v"], f"attn_fwd_{l}")
            h = rowmm(o, wo_h[j], f"oproj_{l}", res=h)
            sv.update(u=u, cq0=cq0, cq=cq, q=q, o=o, lse=lse)
        sv["h2"] = h
        ag_wait(2 * l + 1, h)
        h, sv["g2"], sv["u2"] = ffn_fwd(h, vec(ffn2_norm[l]), *ffn_w["ffn2", l], f"ffn2_fwd_{l}")
        saved.append(sv)
        if l == n_a - 1:
            hkv = norm_fwd(h, vec(kv_in_norm), "kvin_norm")
            ckr = rowmm(hkv[None], wdkv, "dkv")
            ckv = norm_fwd(ckr, vec(kv_latent_norm), "kvlat_norm")
            kv = {"h": h, "hkv": hkv, "ckr": ckr, "ckv": ckv,
                  "k": proj_rope(ckv, wk_h, ctab, stab, ckr, "kproj"),
                  "v": rowmm(ckv[None], wv_h, "vproj", out_dtype=BF16, heads_out=True)}

    sse, dh, dfinal = loss_head(h, vec(final_norm), target, seq, "loss_head")
    loss = lax.psum(0.5 / d * sse[0, 0], ("x", "y", "c"))

    G = {}
    FFN = ("ffn1_w_gate", "ffn1_w_up", "ffn1_w_down", "ffn2_w_gate", "ffn2_w_up", "ffn2_w_down")
    per = {n: [lax.empty(W[n].shape, F32) for _ in range(4)] for n in FFN}
    pending = []

    def rs_complete(after):
        f, l, state = pending.pop()
        names = [f + "_w_gate", f + "_w_up", f + "_w_down"]
        for n, g_ in zip(names, rs_finish(state, after, s_arr, c_arr, f"{f}_{l}")):
            per[n] = adamw_into(W[n], g_, M[n], V[n], per[n], l, f"adamw_{n}_{l}")

    dnorm = {n: [None] * depth for n in ("ffn1_norm", "mix_norm", "ffn2_norm")}
    dqnorm = [None] * (depth - n_a)
    dpool_w, dpool_scale = [None] * n_a, [None] * n_a
    dwdq, dwq_h, dwo_h = [None] * (depth - n_a), [None] * (depth - n_a), [None] * (depth - n_a)

    def ffn_backward(f, l, h_in, dh_, gg, uu):
        gam = ffn1_norm[l] if f == "ffn1" else ffn2_norm[l]
        wg_, wu_, wd_ = ffn_w[f, l]
        dh_in, dgam, hn, dy, dg, du, a = ffn_bwd_act(h_in, vec(gam), dh_, gg, uu, wg_, wu_, wd_, f"{f}_bwd_act_{l}")
        dwg, dwu, dwd = ffn_bwd_weights(hn, dy, a, dg, du, f"{f}_bwd_w_{l}")
        if pending:
            rs_complete(dwg)
        state = rs_begin([dwg, dwu, dwd], c_arr, f"{f}_{l}")
        gate[0] = state[4]
        pending.append((f, l, state))
        dnorm[f + "_norm"][l] = dgam[0]
        return dh_in

    dk_tot = dv_tot = None
    for l in reversed(range(depth)):
        sv = saved[l]
        if l == n_a - 1:
            dckv = rowmm(dk_tot, gated(wk_h), "kproj_bwd", nt=True)
            dkr = rope_bwd_sum(dk_tot, ctab, stab, "kproj_bwd_rope")
            dckv = rowmm(dv_tot, wv_h, "vproj_bwd", nt=True, res=dckv)
            dwk_h = tnmm(kv["ckv"][None], dk_tot, "kproj_bwd_w")
            dwv_h = tnmm(kv["ckv"][None], dv_tot, "vproj_bwd_w")
            dlat, dkvlat = norm_bwd(kv["ckr"], vec(kv_latent_norm), dckv, None, "kvlat_norm_bwd")
            dckr = jnp.concatenate([dlat, dkr], axis=1).astype(BF16)
            dhkv = rowmm(dckr[None], wdkv, "dkv_bwd", nt=True)
            dwdkv = tnmm(kv["hkv"][None], dckr[None], "dkv_bwd_w")[0]
            dh, dkvin = norm_bwd(kv["h"], vec(kv_in_norm), dhkv, dh, "kvin_norm_bwd")
            G["w_dkv"] = dwdkv[:, :KV_RANK + QK_ROPE]
            G["w_uk"] = dwk_h[..., HEAD_W - QK_NOPE:].transpose(1, 0, 2).reshape(KV_RANK, nh * QK_NOPE)
            G["w_uv"] = dwv_h.transpose(1, 0, 2).reshape(KV_RANK, nh * V_HEAD)
        dh = ffn_backward("ffn2", l, sv["h2"], dh, sv["g2"], sv["u2"])
        if l < n_a:
            dh, dmix, dpool_w[l], dps = pool_bwd(sv["h1"], vec(mix_norm[l]), wp[l], vec(pscale[l]), dh, f"pool_bwd_{l}")
            dnorm["mix_norm"][l] = dmix[0]
            dpool_scale[l] = dps[0]
        else:
            j = l - n_a
            do, delta = attn_out_bwd(dh, gated(wo_h[j]), sv["o"], f"oproj_bwd_{l}")
            dwo_h[j] = tnmm(sv["o"], dh[None], f"oproj_bwd_w_{l}")
            dq, dk_tot, dv_tot = attn_bwd(sv["q"], kv["k"], kv["v"], do, sv["lse"].reshape(nh, 1, r),
                                          delta.reshape(nh, 1, r), dk_tot, dv_tot, f"attn_bwd_{l}")
            dxq = rope_bwd_heads(dq, ctab, stab, f"qproj_bwd_rope_{l}", scale=SM_SCALE)
            dcq = rowmm(dxq, wq_h[j], f"qproj_bwd_{l}", nt=True)
            dwq_h[j] = tnmm(sv["cq"][None], dxq, f"qproj_bwd_w_{l}")
            dcq0, dqn = norm_bwd(sv["cq0"], vec(q_latent_norm[j]), dcq, None, f"qnorm_bwd_{l}")
            dqnorm[j] = dqn[0]
            dcq0b = dcq0.astype(BF16)
            du = rowmm(dcq0b[None], wdq[j][None], f"dq_bwd_{l}", nt=True)
            dwdq[j] = tnmm(sv["u"][None], dcq0b[None], f"dq_bwd_w_{l}")[0]
            dh, dmix = norm_bwd(sv["h1"], vec(mix_norm[l]), du, dh, f"mixnorm_bwd_{l}")
            dnorm["mix_norm"][l] = dmix[0]
        dh = ffn_backward("ffn1", l, sv["h0"], dh, sv["g1"], sv["u1"])

    grad_x = dh[FRONT:FRONT + seq][None]
    G["meta_tokens"] = dh[META_ROW0:FRONT]
    G["pool_w"] = jnp.stack(dpool_w)
    G["pool_scale"] = jnp.stack(dpool_scale)
    G["w_dq"] = jnp.stack(dwdq)
    dwq = jnp.stack(dwq_h)
    dwq = jnp.concatenate([dwq[..., HEAD_W - QK_NOPE:], dwq[..., :QK_ROPE]], axis=-1)
    G["w_uq"] = dwq.transpose(0, 2, 1, 3).reshape(-1, Q_RANK, nh * (QK_NOPE + QK_ROPE))
    G["w_o"] = jnp.stack(dwo_h).reshape(-1, nh * V_HEAD, d)

    REPL = ("ffn1_norm", "mix_norm", "ffn2_norm", "kv_in_norm", "kv_latent_norm", "q_latent_norm", "final_norm")
    grep = {"ffn1_norm": jnp.stack(dnorm["ffn1_norm"]), "mix_norm": jnp.stack(dnorm["mix_norm"]),
            "ffn2_norm": jnp.stack(dnorm["ffn2_norm"]), "kv_in_norm": dkvin[0], "kv_latent_norm": dkvlat[0],
            "q_latent_norm": jnp.stack(dqnorm), "final_norm": dfinal[0]}

    def pack128(parts):
        flat = jnp.concatenate([p.reshape(-1) for p in parts])
        n = flat.shape[0]
        n_pad = -(-n // (LANES * SUBLANES)) * (LANES * SUBLANES)
        return jnp.pad(flat, (0, n_pad - n)).reshape(-1, LANES)

    rep_shapes = [W[n].shape for n in REPL]
    g_rep = all_reduce_small(pack128([grep[n] for n in REPL]), "ar_repl")
    d_rep, m_rep, v_rep = adamw(pack128([W[n] for n in REPL]), g_rep, pack128([M[n] for n in REPL]),
                                pack128([V[n] for n in REPL]), "adamw_repl")
    out_g, out_d, out_m, out_v = {}, {}, {}, {}
    for dst, buf in ((out_g, g_rep), (out_d, d_rep), (out_m, m_rep), (out_v, v_rep)):
        for n, a in zip(REPL, _unpack(buf, rep_shapes)):
            dst[n] = a

    g_small = jnp.stack([_pack([_cols_to_shards(G[n], SMALL_AXIS[n])[s] for n in SMALL]) for s in range(N_SHARD)])
    g_small = reduce_scatter([g_small], c_arr, s_arr, "small")[0]
    d_s, m_s, v_s = adamw(_pack([W[n] for n in SMALL]), g_small, _pack([M[n] for n in SMALL]),
                          _pack([V[n] for n in SMALL]), "adamw_small")
    for dst, buf in ((out_g, g_small), (out_d, d_s), (out_m, m_s), (out_v, v_s)):
        for n, a in zip(SMALL, _unpack(buf, small_shapes)):
            dst[n] = a

    rs_complete(per["ffn2_w_down"][1])
    for n in FFN:
        out_g[n], out_d[n], out_m[n], out_v[n] = (
            jnp.swapaxes(a, 1, 2) if n in TRANSPOSED else a for a in per[n])

    return (loss, grad_x, *[out_g[n] for n in NAMES], *[out_d[n] for n in NAMES],
            *[out_m[n] for n in NAMES], *[out_v[n] for n in NAMES])


NAMES = ("meta_tokens", "ffn1_norm", "ffn1_w_gate", "ffn1_w_up", "ffn1_w_down", "mix_norm", "ffn2_norm",
         "ffn2_w_gate", "ffn2_w_up", "ffn2_w_down", "pool_w", "pool_scale", "kv_in_norm", "w_dkv",
         "kv_latent_norm", "w_uk", "w_uv", "w_dq", "q_latent_norm", "w_uq", "w_o", "final_norm")
```

```python
import functools
import math

import jax
import jax.numpy as jnp
from jax import lax
from jax.experimental import pallas as pl
from jax.experimental.pallas import tpu as pltpu

F32 = jnp.float32
BF16 = jnp.bfloat16
MESH = pl.DeviceIdType.MESH
ANY = pl.BlockSpec(memory_space=pl.ANY)

EPS = 1e-6
CHUNK = 64
CHUNK_SHIFT = 6
N_META = 16
FRONT = 64
META_ROW0 = FRONT - N_META
POOL_WINDOWS = (2, 4, 8, 16)
HALO = 16
N_HEADS = 8
QK_NOPE = 64
QK_ROPE = 32
V_HEAD = 64
HEAD_W = 128
KV_RANK = 256
Q_RANK = 384
ROPE_THETA = 10000.0
NEG = -1e30
N_SHARD = 4
LANES = 128
SUBLANES = 8
PACK_W = 512
VMEM_BIG = 52 * 1024 * 1024
VMEM_MAX = 60 * 1024 * 1024
WGRAD_ROWS = 1664

ADAM_LR = 0.001
ADAM_B1 = 0.9
ADAM_B2 = 0.999
ADAM_EPS = 1e-08
ADAM_WD = 0.01
ADAM_STEP = 10

NT = (((1,), (1,)), ((), ()))
TN = (((0,), (0,)), ((), ()))


def _params(sem=None, vmem=None):
    return pltpu.CompilerParams(dimension_semantics=sem, vmem_limit_bytes=vmem)


def _tile(n, pref, mult=SUBLANES):
    best = None
    for t in range(mult, min(n, pref) + 1, mult):
        if n % t == 0:
            best = t
    return best if best is not None else n


def _row_tile(r):
    return 640 if r % 640 == 0 else 128


def _rms(x):
    rstd = lax.rsqrt(jnp.mean(x * x, axis=-1, keepdims=True) + EPS)
    return x * rstd, rstd


def _rms_bwd(xh, rstd, dxh):
    return rstd * (dxh - xh * jnp.mean(dxh * xh, axis=-1, keepdims=True))


def _sigmoid(x):
    return 1.0 / (1.0 + jnp.exp(-x))


def _place():
    x, y, c = lax.axis_index("x"), lax.axis_index("y"), lax.axis_index("c")
    chips = [(1 - x, y), (x, 1 - y), (1 - x, 1 - y)]
    return x, y, c, chips


def all_gather_shards(shards, name):
    n = len(shards)
    slot = 2 * lax.axis_index("x") + lax.axis_index("y")
    lands = [lax.dynamic_update_slice(lax.empty((N_SHARD,) + a.shape, a.dtype), a[None], (slot, 0, 0)) for a in shards]

    def body(*refs):
        ins, outs = refs[:n], refs[2 * n:3 * n]
        send1, recv1, send2, recv2 = refs[3 * n:]
        x, y, c, chips = _place()
        s = 2 * x + y
        sib = (x, y, 1 - c)

        def rcopy(k, j, src, dst, to, first):
            return pltpu.make_async_remote_copy(
                src_ref=src, dst_ref=dst,
                send_sem=(send1 if first else send2).at[k, j],
                recv_sem=(recv1 if first else recv2).at[k, j],
                device_id=to, device_id_type=MESH)

        started = []
        for k in range(n):
            hf = ins[k].shape[0] // 2
            for j, (cx, cy) in enumerate(chips):
                r = rcopy(k, j, ins[k].at[pl.ds(c * hf, hf)], outs[k].at[s, pl.ds(c * hf, hf)],
                          (cx, cy, c), True)
                r.start()
                started.append(r)
        for k in range(n):
            hf = ins[k].shape[0] // 2
            for j, (cx, cy) in enumerate(chips):
                blk = outs[k].at[2 * cx + cy, pl.ds(c * hf, hf)]
                rcopy(k, j, blk, blk, (cx, cy, c), True).wait_recv()
                f = rcopy(k, j, blk, blk, sib, False)
                f.start()
                started.append(f)
        for k in range(n):
            hf = ins[k].shape[0] // 2
            for j, (cx, cy) in enumerate(chips):
                blk = outs[k].at[2 * cx + cy, pl.ds((1 - c) * hf, hf)]
                rcopy(k, j, blk, blk, sib, False).wait_recv()
        for r in started:
            r.wait_send()

    return pl.pallas_call(
        body, name=name,
        out_shape=[jax.ShapeDtypeStruct((N_SHARD,) + a.shape, a.dtype) for a in shards],
        in_specs=[ANY] * (2 * n), out_specs=[ANY] * n,
        input_output_aliases={n + k: k for k in range(n)},
        scratch_shapes=[pltpu.SemaphoreType.DMA((n, 3))] * 4,
    )(*shards, *lands)


def sibling_swap_halves(gs, name):
    n = len(gs)

    def body(*refs):
        ins, outs = refs[:n], refs[n:2 * n]
        send, recv = refs[2 * n:]
        x, y, c, _ = _place()
        sib = (x, y, 1 - c)
        cps = []
        for k in range(n):
            hf = ins[k].shape[1] // 2
            r = pltpu.make_async_remote_copy(
                src_ref=ins[k].at[:, pl.ds((1 - c) * hf, hf)], dst_ref=outs[k],
                send_sem=send.at[k], recv_sem=recv.at[k], device_id=sib, device_id_type=MESH)
            r.start()
            cps.append(r)
        for r in cps:
            r.wait_recv()
        for r in cps:
            r.wait_send()

    return pl.pallas_call(
        body, name=name,
        out_shape=[jax.ShapeDtypeStruct((a.shape[0], a.shape[1] // 2, a.shape[2]), a.dtype) for a in gs],
        in_specs=[ANY] * n, out_specs=[ANY] * n,
        scratch_shapes=[pltpu.SemaphoreType.DMA((n,))] * 2,
    )(*gs)


def chip_exchange(ps, name):
    n = len(ps)

    def body(*refs):
        ins, outs = refs[:n], refs[n:2 * n]
        send, recv = refs[2 * n:]
        x, y, c, chips = _place()
        cps = []
        for k in range(n):
            for j, (cx, cy) in enumerate(chips):
                r = pltpu.make_async_remote_copy(
                    src_ref=ins[k].at[2 * cx + cy], dst_ref=outs[k].at[j],
                    send_sem=send.at[k, j], recv_sem=recv.at[k, j],
                    device_id=(cx, cy, c), device_id_type=MESH)
                r.start()
                cps.append(r)
        for r in cps:
            r.wait_recv()
        for r in cps:
            r.wait_send()

    return pl.pallas_call(
        body, name=name,
        out_shape=[jax.ShapeDtypeStruct((3,) + a.shape[1:], a.dtype) for a in ps],
        in_specs=[ANY] * n, out_specs=[ANY] * n,
        scratch_shapes=[pltpu.SemaphoreType.DMA((n, 3))] * 2,
    )(*ps)


def sibling_join_halves(ts, name):
    n = len(ts)

    def body(*refs):
        ins, outs = refs[:n], refs[n:2 * n]
        send, recv = refs[2 * n:]
        x, y, c, _ = _place()
        sib = (x, y, 1 - c)

        def copy(k, half):
            hf = ins[k].shape[0] // 2
            rows = pl.ds(half * hf, hf)
            return pltpu.make_async_remote_copy(
                src_ref=ins[k].at[rows], dst_ref=outs[k].at[rows],
                send_sem=send.at[k], recv_sem=recv.at[k], device_id=sib, device_id_type=MESH)

        cps = [copy(k, c) for k in range(n)]
        for r in cps:
            r.start()
        for k in range(n):
            copy(k, 1 - c).wait_recv()
        for r in cps:
            r.wait_send()

    return pl.pallas_call(
        body, name=name,
        out_shape=[jax.ShapeDtypeStruct(a.shape, a.dtype) for a in ts],
        in_specs=[ANY] * n, out_specs=[ANY] * n, input_output_aliases={k: k for k in range(n)},
        scratch_shapes=[pltpu.SemaphoreType.DMA((n,))] * 2,
    )(*ts)


def all_reduce_small(part, name):
    m, w = part.shape

    def body(x_ref, tot_ref, gat_ref, send_sems, recv_sems):
        x, y, c, chips = _place()
        me, sib = (x, y, c), (x, y, 1 - c)

        def slot(px, py, pc):
            return gat_ref.at[4 * px + 2 * py + pc]

        def copy(k, block, to, src=None):
            return pltpu.make_async_remote_copy(
                src_ref=slot(*block) if src is None else src, dst_ref=slot(*block),
                send_sem=send_sems.at[k], recv_sem=recv_sems.at[k], device_id=to, device_id_type=MESH)

        gat_ref[4 * x + 2 * y + c] = x_ref[...]
        first = [copy(0, me, sib, src=x_ref)]
        first += [copy(1 + j, me, (*chip, c), src=x_ref) for j, chip in enumerate(chips)]
        for cp in first:
            cp.start()
        passed = [copy(4 + j, (*chip, c), sib) for j, chip in enumerate(chips)]
        for j, chip in enumerate(chips):
            copy(1 + j, (*chip, c), me).wait_recv()
            passed[j].start()
        copy(0, sib, me).wait_recv()
        for j, chip in enumerate(chips):
            copy(4 + j, (*chip, 1 - c), me).wait_recv()
        for cp in first + passed:
            cp.wait_send()
        tot = gat_ref[0]
        for d in range(1, 8):
            tot = tot + gat_ref[d]
        tot_ref[...] = tot

    return pl.pallas_call(
        body, name=name,
        out_shape=jax.ShapeDtypeStruct((m, w), F32),
        in_specs=[pl.BlockSpec(memory_space=pltpu.VMEM)],
        out_specs=pl.BlockSpec(memory_space=pltpu.VMEM),
        scratch_shapes=[pltpu.VMEM((8, m, w), F32), pltpu.SemaphoreType.DMA((7,)), pltpu.SemaphoreType.DMA((7,))],
    )(part)


def add_own_half(g, rec, c_arr, name):
    _, a, b = g.shape
    hf = a // 2
    tb = _tile(hf, 256)
    nb = hf // tb

    def body(c_ref, g_ref, r_ref, o_ref):
        o_ref[...] = g_ref[...] + r_ref[...]

    return pl.pallas_call(
        body, name=name,
        out_shape=jax.ShapeDtypeStruct((N_SHARD, hf, b), F32),
        grid_spec=pltpu.PrefetchScalarGridSpec(
            num_scalar_prefetch=1, grid=(N_SHARD, nb),
            in_specs=[pl.BlockSpec((1, tb, b), lambda s, i, c: (s, c[0] * nb + i, 0)),
                      pl.BlockSpec((1, tb, b), lambda s, i, c: (s, i, 0))],
            out_specs=pl.BlockSpec((1, tb, b), lambda s, i, c: (s, i, 0))),
        compiler_params=_params(("arbitrary", "arbitrary")),
    )(c_arr, g, rec)


def sum_four(p, rec, s_arr, c_arr, name):
    _, ah, b = p.shape
    tb = _tile(ah, 256)
    nb = ah // tb

    def body(s_ref, c_ref, p_ref, r0, r1, r2, o_ref):
        o_ref[...] = ((p_ref[0] + r0[0]) + r1[0]) + r2[0]

    def rspec(j):
        return pl.BlockSpec((1, tb, b), lambda i, s, c: (j, i, 0))

    return pl.pallas_call(
        body, name=name,
        out_shape=jax.ShapeDtypeStruct((2 * ah, b), F32),
        grid_spec=pltpu.PrefetchScalarGridSpec(
            num_scalar_prefetch=2, grid=(nb,),
            in_specs=[pl.BlockSpec((1, tb, b), lambda i, s, c: (s[0], i, 0)), rspec(0), rspec(1), rspec(2)],
            out_specs=pl.BlockSpec((tb, b), lambda i, s, c: (c[0] * nb + i, 0))),
        compiler_params=_params(("arbitrary",)),
    )(s_arr, c_arr, p, rec, rec, rec)


def adamw(w, g, m, v, name):
    a, b = w.shape
    tb = _tile(a, 256)
    c1 = 1.0 - ADAM_B1 ** ADAM_STEP
    c2 = 1.0 - ADAM_B2 ** ADAM_STEP

    def body(w_ref, g_ref, m_ref, v_ref, d_ref, mo_ref, vo_ref):
        g_ = g_ref[...]
        m_ = ADAM_B1 * m_ref[...] + (1.0 - ADAM_B1) * g_
        v_ = ADAM_B2 * v_ref[...] + (1.0 - ADAM_B2) * (g_ * g_)
        m_hat = m_ / c1
        v_hat = v_ / c2
        d_ref[...] = -ADAM_LR * (m_hat / (jnp.sqrt(v_hat) + ADAM_EPS) + ADAM_WD * w_ref[...])
        mo_ref[...] = m_
        vo_ref[...] = v_

    spec = pl.BlockSpec((tb, b), lambda i: (i, 0))
    return pl.pallas_call(
        body, name=name,
        out_shape=[jax.ShapeDtypeStruct((a, b), F32)] * 3,
        grid=(a // tb,), in_specs=[spec] * 4, out_specs=[spec] * 3,
        compiler_params=_params(("arbitrary",)),
    )(w, g, m, v)


def adamw_into(w_all, g, m_all, v_all, prev, l, name):
    nl, a, b = w_all.shape
    tb = _tile(a, 256)
    c1 = 1.0 - ADAM_B1 ** ADAM_STEP
    c2 = 1.0 - ADAM_B2 ** ADAM_STEP

    def body(w_ref, g_ref, m_ref, v_ref, p0, p1, p2, p3, go_ref, d_ref, mo_ref, vo_ref):
        g_ = g_ref[...]
        m_ = ADAM_B1 * m_ref[0] + (1.0 - ADAM_B1) * g_
        v_ = ADAM_B2 * v_ref[0] + (1.0 - ADAM_B2) * (g_ * g_)
        m_hat = m_ / c1
        v_hat = v_ / c2
        go_ref[0] = g_
        d_ref[0] = -ADAM_LR * (m_hat / (jnp.sqrt(v_hat) + ADAM_EPS) + ADAM_WD * w_ref[0])
        mo_ref[0] = m_
        vo_ref[0] = v_

    lay = pl.BlockSpec((1, tb, b), lambda i: (l, i, 0))
    return pl.pallas_call(
        body, name=name,
        out_shape=[jax.ShapeDtypeStruct((nl, a, b), F32)] * 4,
        grid=(a // tb,), in_specs=[lay, pl.BlockSpec((tb, b), lambda i: (i, 0)), lay, lay] + [ANY] * 4,
        out_specs=[lay] * 4, input_output_aliases={4: 0, 5: 1, 6: 2, 7: 3},
        compiler_params=_params(("arbitrary",)),
    )(w_all, g, m_all, v_all, *prev)


def reduce_scatter(gs, c_arr, s_arr, tag):
    rec_a = sibling_swap_halves(gs, f"rs_swap_{tag}")
    ps = [add_own_half(g, r, c_arr, f"rs_add_{tag}_{k}") for k, (g, r) in enumerate(zip(gs, rec_a))]
    rec_b = chip_exchange(ps, f"rs_ici_{tag}")
    ts = [sum_four(p, r, s_arr, c_arr, f"rs_sum_{tag}_{k}") for k, (p, r) in enumerate(zip(ps, rec_b))]
    return sibling_join_halves(ts, f"rs_join_{tag}")


HBM_SPEC = pl.BlockSpec(memory_space=pltpu.HBM)
SEM_SPEC = pl.BlockSpec(memory_space=pltpu.SEMAPHORE)
EFFECT = pltpu.SideEffectType.DATAFLOW_SIDE_EFFECTING


def _in_hbm(a):
    return pltpu.with_memory_space_constraint(a, pltpu.HBM)


def _exchange_copy(k, j, chip, c, s, srcs, lands, send, recv, gather, receiving):
    cx, cy = chip
    src = srcs[k] if gather else srcs[k].at[2 * cx + cy]
    if gather:
        dst = lands[k].at[2 * cx + cy] if receiving else lands[k].at[s]
    else:
        dst = lands[k].at[j]
    return pltpu.make_async_remote_copy(src_ref=src, dst_ref=dst, send_sem=send.at[3 * k + j], recv_sem=recv.at[3 * k + j],
                                        device_id=(cx, cy, c), device_id_type=MESH)


def exchange_start(srcs, lands, name, gather):
    n = len(srcs)

    def body(*refs):
        srcs_in, lands_in = refs[:n], refs[n:2 * n]
        send, recv = refs[2 * n], refs[2 * n + 1]
        token = refs[-1]
        x, y, c, chips = _place()
        for k in range(n):
            for j, chip in enumerate(chips):
                _exchange_copy(k, j, chip, c, 2 * x + y, srcs_in, lands_in, send, recv, gather, False).start()
        token[...] = jnp.zeros_like(token)

    outs = pl.pallas_call(
        body, name=name,
        out_shape=(pltpu.SemaphoreType.DMA((3 * n,)), pltpu.SemaphoreType.DMA((3 * n,)),
                   *[pltpu.HBM(a.shape, a.dtype) for a in srcs], *[pltpu.HBM(a.shape, a.dtype) for a in lands],
                   jax.ShapeDtypeStruct((SUBLANES, LANES), F32)),
        in_specs=[HBM_SPEC] * (2 * n),
        out_specs=(SEM_SPEC, SEM_SPEC, *[HBM_SPEC] * (2 * n), pl.BlockSpec(memory_space=pltpu.VMEM)),
        input_output_aliases={k: 2 + k for k in range(2 * n)},
        compiler_params=pltpu.CompilerParams(has_side_effects=EFFECT),
    )(*[_in_hbm(a) for a in srcs], *[_in_hbm(a) for a in lands])
    return outs[0], outs[1], list(outs[2:2 + n]), list(outs[2 + n:2 + 2 * n]), outs[-1]


def exchange_wait(state, after, name, gather):
    send, recv, srcs, lands, _ = state
    n = len(srcs)

    def body(*refs):
        srcs_in, lands_in = refs[:n], refs[n:2 * n]
        send_, recv_ = refs[2 * n], refs[2 * n + 1]
        x, y, c, chips = _place()
        for k in range(n):
            for j, chip in enumerate(chips):
                cp = _exchange_copy(k, j, chip, c, 2 * x + y, srcs_in, lands_in, send_, recv_, gather, True)
                cp.wait_send()
                cp.wait_recv()

    outs = pl.pallas_call(
        body, name=name,
        out_shape=tuple(pltpu.HBM(a.shape, a.dtype) for a in srcs + lands),
        in_specs=[HBM_SPEC] * (2 * n) + [SEM_SPEC, SEM_SPEC, ANY],
        out_specs=tuple([HBM_SPEC] * (2 * n)),
        input_output_aliases={k: k for k in range(2 * n)},
        compiler_params=pltpu.CompilerParams(has_side_effects=EFFECT),
    )(*srcs, *lands, send, recv, after)
    return list(outs[:n]), list(outs[n:])


def rs_begin(gs, c_arr, tag):
    rec_a = sibling_swap_halves(gs, f"rs_swap_{tag}")
    ps = [add_own_half(g, r, c_arr, f"rs_add_{tag}_{k}") for k, (g, r) in enumerate(zip(gs, rec_a))]
    lands = [lax.empty((3,) + p.shape[1:], p.dtype) for p in ps]
    return exchange_start(ps, lands, f"rs_ici_start_{tag}", False)


def rs_finish(state, after, s_arr, c_arr, tag):
    ps, rec_b = exchange_wait(state, after, f"rs_ici_wait_{tag}", False)
    ts = [sum_four(p, r, s_arr, c_arr, f"rs_sum_{tag}_{k}") for k, (p, r) in enumerate(zip(ps, rec_b))]
    return sibling_join_halves(ts, f"rs_join_{tag}")


def ffn_fwd(h, gamma, wg, wu, wd, name):
    r, d = h.shape
    ns, fs, _ = wg.shape
    tm = _row_tile(r)

    def body(h_ref, g_ref, wg_ref, wu_ref, wd_ref, ho_ref, gg_ref, uu_ref, hn_sc, acc_sc):
        s = pl.program_id(1)

        @pl.when(s == 0)
        def _():
            xh, _ = _rms(h_ref[...])
            hn_sc[...] = (xh * g_ref[...]).astype(BF16)
            acc_sc[...] = jnp.zeros_like(acc_sc)

        hn = hn_sc[...]
        g = lax.dot_general(hn, wg_ref[0], NT, preferred_element_type=F32)
        u = lax.dot_general(hn, wu_ref[0], NT, preferred_element_type=F32)
        gg_ref[0] = g.astype(BF16)
        uu_ref[0] = u.astype(BF16)
        a = (g * _sigmoid(g) * u).astype(BF16)
        acc_sc[...] += jnp.dot(a, wd_ref[0], preferred_element_type=F32)

        @pl.when(s == ns - 1)
        def _():
            ho_ref[...] = h_ref[...] + 0.5 * acc_sc[...]

    return pl.pallas_call(
        body, name=name,
        out_shape=[jax.ShapeDtypeStruct((r, d), F32), jax.ShapeDtypeStruct((ns, r, fs), BF16),
                   jax.ShapeDtypeStruct((ns, r, fs), BF16)],
        grid=(r // tm, ns),
        in_specs=[pl.BlockSpec((tm, d), lambda i, s: (i, 0)), pl.BlockSpec((1, d), lambda i, s: (0, 0)),
                  pl.BlockSpec((1, fs, d), lambda i, s: (s, 0, 0)), pl.BlockSpec((1, fs, d), lambda i, s: (s, 0, 0)),
                  pl.BlockSpec((1, fs, d), lambda i, s: (s, 0, 0))],
        out_specs=[pl.BlockSpec((tm, d), lambda i, s: (i, 0)), pl.BlockSpec((1, tm, fs), lambda i, s: (s, i, 0)),
                   pl.BlockSpec((1, tm, fs), lambda i, s: (s, i, 0))],
        scratch_shapes=[pltpu.VMEM((tm, d), BF16), pltpu.VMEM((tm, d), F32)],
        compiler_params=_params(("arbitrary", "arbitrary"), VMEM_BIG),
    )(h, gamma, wg, wu, wd)


def ffn_bwd_act(h, gamma, dh, gg, uu, wg, wu, wd, name):
    r, d = h.shape
    ns, fs, _ = wg.shape
    tm = _row_tile(r)

    def body(h_ref, g_ref, dh_ref, gg_ref, uu_ref, wg_ref, wu_ref, wd_ref,
             dho_ref, dgam_ref, hn_ref, dy_ref, dg_ref, du_ref, a_ref, acc_sc):
        i, s = pl.program_id(0), pl.program_id(1)

        @pl.when(s == 0)
        def _():
            xh, _ = _rms(h_ref[...])
            hn_ref[...] = (xh * g_ref[...]).astype(BF16)
            dy_ref[...] = (0.5 * dh_ref[...]).astype(BF16)
            acc_sc[...] = jnp.zeros_like(acc_sc)

        @pl.when((i == 0) & (s == 0))
        def _():
            dgam_ref[...] = jnp.zeros_like(dgam_ref)

        g = gg_ref[0].astype(F32)
        u = uu_ref[0].astype(F32)
        da = lax.dot_general(dy_ref[...], wd_ref[0], NT, preferred_element_type=F32)
        sig = _sigmoid(g)
        sl = g * sig
        a_ref[0] = (sl * u).astype(BF16)
        du = (da * sl).astype(BF16)
        dg = (da * u * (sig * (1.0 + g * (1.0 - sig)))).astype(BF16)
        dg_ref[0] = dg
        du_ref[0] = du
        acc_sc[...] += (jnp.dot(dg, wg_ref[0], preferred_element_type=F32)
                        + jnp.dot(du, wu_ref[0], preferred_element_type=F32))

        @pl.when(s == ns - 1)
        def _():
            xh, rstd = _rms(h_ref[...])
            dhn = acc_sc[...]
            dgam_ref[...] += jnp.sum(dhn * xh, axis=0, keepdims=True)
            dho_ref[...] = dh_ref[...] + _rms_bwd(xh, rstd, dhn * g_ref[...])

    row = pl.BlockSpec((tm, d), lambda i, s: (i, 0))
    act = pl.BlockSpec((1, tm, fs), lambda i, s: (s, i, 0))
    return pl.pallas_call(
        body, name=name,
        out_shape=[jax.ShapeDtypeStruct((r, d), F32), jax.ShapeDtypeStruct((1, d), F32),
                   jax.ShapeDtypeStruct((r, d), BF16), jax.ShapeDtypeStruct((r, d), BF16),
                   jax.ShapeDtypeStruct((ns, r, fs), BF16), jax.ShapeDtypeStruct((ns, r, fs), BF16),
                   jax.ShapeDtypeStruct((ns, r, fs), BF16)],
        grid=(r // tm, ns),
        in_specs=[row, pl.BlockSpec((1, d), lambda i, s: (0, 0)), row, act, act,
                  pl.BlockSpec((1, fs, d), lambda i, s: (s, 0, 0)), pl.BlockSpec((1, fs, d), lambda i, s: (s, 0, 0)),
                  pl.BlockSpec((1, fs, d), lambda i, s: (s, 0, 0))],
        out_specs=[row, pl.BlockSpec((1, d), lambda i, s: (0, 0)), row, row, act, act, act],
        scratch_shapes=[pltpu.VMEM((tm, d), F32)],
        compiler_params=_params(("arbitrary", "arbitrary"), VMEM_BIG),
    )(h, gamma, dh, gg, uu, wg, wu, wd)


def ffn_bwd_weights(hn, dy, a, dg, du, name):
    r, d = hn.shape
    ns, _, fs = a.shape
    tm = WGRAD_ROWS if r % WGRAD_ROWS == 0 else _row_tile(r)

    def body(hn_ref, dy_ref, a_ref, dg_ref, du_ref, wg_ref, wu_ref, wd_ref):
        @pl.when(pl.program_id(1) == 0)
        def _():
            wg_ref[...] = jnp.zeros_like(wg_ref)
            wu_ref[...] = jnp.zeros_like(wu_ref)
            wd_ref[...] = jnp.zeros_like(wd_ref)

        hn_ = hn_ref[...]
        wg_ref[0] += lax.dot_general(dg_ref[0], hn_, TN, preferred_element_type=F32)
        wu_ref[0] += lax.dot_general(du_ref[0], hn_, TN, preferred_element_type=F32)
        wd_ref[0] += lax.dot_general(a_ref[0], dy_ref[...], TN, preferred_element_type=F32)

    row = pl.BlockSpec((tm, d), lambda s, i: (i, 0))
    act = pl.BlockSpec((1, tm, fs), lambda s, i: (s, i, 0))
    wsp = pl.BlockSpec((1, fs, d), lambda s, i: (s, 0, 0))
    return pl.pallas_call(
        body, name=name,
        out_shape=[jax.ShapeDtypeStruct((ns, fs, d), F32)] * 3,
        grid=(ns, r // tm),
        in_specs=[row, row, act, act, act],
        out_specs=[wsp, wsp, wsp],
        compiler_params=_params(("arbitrary", "arbitrary"), VMEM_MAX),
    )(hn, dy, a, dg, du)


def norm_fwd(x, gamma, name):
    r = x.shape[0]
    w = gamma.shape[1]
    tm = _row_tile(r)

    def body(x_ref, g_ref, o_ref):
        xh, _ = _rms(x_ref[...])
        o_ref[...] = (xh * g_ref[...]).astype(BF16)

    return pl.pallas_call(
        body, name=name, out_shape=jax.ShapeDtypeStruct((r, w), BF16), grid=(r // tm,),
        in_specs=[pl.BlockSpec((tm, w), lambda i: (i, 0)), pl.BlockSpec((1, w), lambda i: (0, 0))],
        out_specs=pl.BlockSpec((tm, w), lambda i: (i, 0)),
        compiler_params=_params(("arbitrary",)),
    )(x, gamma)


def norm_bwd(x, gamma, dy, dres, name):
    r = x.shape[0]
    w = gamma.shape[1]
    tm = _row_tile(r)
    has_res = dres is not None

    def body(*refs):
        if has_res:
            x_ref, g_ref, dy_ref, dr_ref, dx_ref, dgam_ref = refs
        else:
            x_ref, g_ref, dy_ref, dx_ref, dgam_ref = refs

        @pl.when(pl.program_id(0) == 0)
        def _():
            dgam_ref[...] = jnp.zeros_like(dgam_ref)

        xh, rstd = _rms(x_ref[...])
        dy_ = dy_ref[...].astype(F32)
        dgam_ref[...] += jnp.sum(dy_ * xh, axis=0, keepdims=True)
        dx = _rms_bwd(xh, rstd, dy_ * g_ref[...])
        if has_res:
            dx = dx + dr_ref[...]
        dx_ref[...] = dx

    row = pl.BlockSpec((tm, w), lambda i: (i, 0))
    vec = pl.BlockSpec((1, w), lambda i: (0, 0))
    ins = [x, gamma, dy] + ([dres] if has_res else [])
    return pl.pallas_call(
        body, name=name,
        out_shape=[jax.ShapeDtypeStruct((r, w), F32), jax.ShapeDtypeStruct((1, w), F32)],
        grid=(r // tm,), in_specs=[row, vec, row] + ([row] if has_res else []), out_specs=[row, vec],
        compiler_params=_params(("arbitrary",)),
    )(*ins)


def rowmm(a, w, name, *, nt=False, res=None, out_dtype=F32, heads_out=False):
    ha, r, ka = a.shape
    hw = w.shape[0]
    nh = max(ha, hw)
    n = w.shape[1] if nt else w.shape[2]
    tm = _row_tile(r)
    dims = NT if nt else (((1,), (0,)), ((), ()))
    has_res = res is not None

    def body(*refs):
        if has_res:
            a_ref, w_ref, r_ref, o_ref = refs
        else:
            a_ref, w_ref, o_ref = refs
        shared = a_ref[0].astype(BF16) if ha == 1 else None
        acc = None
        for h in range(nh):
            lhs = shared if ha == 1 else a_ref[h].astype(BF16)
            p = lax.dot_general(lhs, w_ref[h if hw > 1 else 0], dims, preferred_element_type=F32)
            if heads_out:
                o_ref[h] = p.astype(out_dtype)
            else:
                acc = p if acc is None else acc + p
        if not heads_out:
            if has_res:
                acc = acc + r_ref[...]
            o_ref[...] = acc.astype(out_dtype)

    in_specs = [pl.BlockSpec((ha, tm, ka), lambda i: (0, i, 0)), pl.BlockSpec(w.shape, lambda i: (0, 0, 0))]
    ins = [a, w]
    if has_res:
        in_specs.append(pl.BlockSpec((tm, n), lambda i: (i, 0)))
        ins.append(res)
    if heads_out:
        out_shape = jax.ShapeDtypeStruct((nh, r, n), out_dtype)
        out_spec = pl.BlockSpec((nh, tm, n), lambda i: (0, i, 0))
    else:
        out_shape = jax.ShapeDtypeStruct((r, n), out_dtype)
        out_spec = pl.BlockSpec((tm, n), lambda i: (i, 0))
    return pl.pallas_call(
        body, name=name, out_shape=out_shape, grid=(r // tm,), in_specs=in_specs, out_specs=out_spec,
        compiler_params=_params(("arbitrary",), VMEM_BIG),
    )(*ins)


def tnmm(a, b, name):
    ha, r, ka = a.shape
    hb, _, nb = b.shape
    nh = max(ha, hb)
    tm = _row_tile(r)

    def body(a_ref, b_ref, o_ref):
        @pl.when(pl.program_id(0) == 0)
        def _():
            o_ref[...] = jnp.zeros_like(o_ref)

        a_shared = a_ref[0].astype(BF16) if ha == 1 else None
        b_shared = b_ref[0].astype(BF16) if hb == 1 else None
        for h in range(nh):
            lhs = a_shared if ha == 1 else a_ref[h].astype(BF16)
            rhs = b_shared if hb == 1 else b_ref[h].astype(BF16)
            o_ref[h] += lax.dot_general(lhs, rhs, TN, preferred_element_type=F32)

    return pl.pallas_call(
        body, name=name, out_shape=jax.ShapeDtypeStruct((nh, ka, nb), F32), grid=(r // tm,),
        in_specs=[pl.BlockSpec((ha, tm, ka), lambda i: (0, i, 0)), pl.BlockSpec((hb, tm, nb), lambda i: (0, i, 0))],
        out_specs=pl.BlockSpec((nh, ka, nb), lambda i: (0, 0, 0)),
        compiler_params=_params(("arbitrary",), VMEM_BIG),
    )(a, b)


def rope_tables(r):
    inv = 1.0 / (ROPE_THETA ** (jnp.arange(0, QK_ROPE, 2, dtype=F32) / QK_ROPE))
    pos = (jnp.arange(r, dtype=F32) - META_ROW0)[:, None]
    ang = pos * inv[None, :]
    cos, sin = jnp.cos(ang), jnp.sin(ang)
    ones = jnp.ones((r, HEAD_W - QK_ROPE), F32)
    ctab = jnp.concatenate([cos, cos, ones], axis=1)
    stab = jnp.concatenate([-sin, sin, jnp.zeros_like(ones)], axis=1)
    return ctab, stab


def _swap_halves(z):
    lane = lax.broadcasted_iota(jnp.int32, z.shape, 1)
    up = pltpu.roll(z, HEAD_W - QK_ROPE // 2, 1)
    down = pltpu.roll(z, QK_ROPE // 2, 1)
    return jnp.where(lane < QK_ROPE // 2, up, jnp.where(lane < QK_ROPE, down, 0.0))


def proj_rope(a, w, ctab, stab, extra, name, scale=1.0):
    r, ka = a.shape
    nh = w.shape[0]
    tm = _row_tile(r)
    has_extra = extra is not None

    def body(*refs):
        if has_extra:
            a_ref, w_ref, c_ref, s_ref, e_ref, o_ref = refs
        else:
            a_ref, w_ref, c_ref, s_ref, o_ref = refs
        a_ = a_ref[...]
        ctab_, stab_ = c_ref[...], s_ref[...]
        if scale != 1.0:
            ctab_, stab_ = ctab_ * scale, stab_ * scale
        for h in range(nh):
            x = jnp.dot(a_, w_ref[h], preferred_element_type=F32)
            if has_extra:
                x = x + e_ref[...]
            o_ref[h] = (x * ctab_ + _swap_halves(x) * stab_).astype(BF16)

    tab = pl.BlockSpec((tm, HEAD_W), lambda i: (i, 0))
    in_specs = [pl.BlockSpec((tm, ka), lambda i: (i, 0)), pl.BlockSpec((nh, ka, HEAD_W), lambda i: (0, 0, 0)),
                tab, tab]
    ins = [a, w, ctab, stab]
    if has_extra:
        in_specs.append(pl.BlockSpec((tm, HEAD_W), lambda i: (i, 2)))
        ins.append(extra)
    return pl.pallas_call(
        body, name=name, out_shape=jax.ShapeDtypeStruct((nh, r, HEAD_W), BF16), grid=(r // tm,),
        in_specs=in_specs, out_specs=pl.BlockSpec((nh, tm, HEAD_W), lambda i: (0, i, 0)),
        compiler_params=_params(("arbitrary",)),
    )(*ins)


def rope_bwd_heads(d, ctab, stab, name, scale=1.0):
    nh, r, _ = d.shape
    tm = _row_tile(r)

    def body(d_ref, c_ref, s_ref, o_ref):
        ctab_, stab_ = c_ref[...], s_ref[...]
        if scale != 1.0:
            ctab_, stab_ = ctab_ * scale, stab_ * scale
        for h in range(nh):
            d_ = d_ref[h]
            o_ref[h] = (d_ * ctab_ + _swap_halves(d_ * stab_)).astype(BF16)

    tab = pl.BlockSpec((tm, HEAD_W), lambda i: (i, 0))
    blk = pl.BlockSpec((nh, tm, HEAD_W), lambda i: (0, i, 0))
    return pl.pallas_call(
        body, name=name, out_shape=jax.ShapeDtypeStruct((nh, r, HEAD_W), BF16), grid=(r // tm,),
        in_specs=[blk, tab, tab], out_specs=blk,
        compiler_params=_params(("arbitrary",)),
    )(d, ctab, stab)


def rope_bwd_sum(d, ctab, stab, name):
    nh, r, _ = d.shape
    tm = _row_tile(r)

    def body(d_ref, c_ref, s_ref, o_ref):
        d_ = d_ref[0]
        for h in range(1, nh):
            d_ = d_ + d_ref[h]
        lane = lax.broadcasted_iota(jnp.int32, d_.shape, 1)
        g = d_ * c_ref[...] + _swap_halves(d_ * s_ref[...])
        o_ref[...] = jnp.where(lane < QK_ROPE, g, 0.0)

    tab = pl.BlockSpec((tm, HEAD_W), lambda i: (i, 0))
    return pl.pallas_call(
        body, name=name, out_shape=jax.ShapeDtypeStruct((r, HEAD_W), F32), grid=(r // tm,),
        in_specs=[pl.BlockSpec((nh, tm, HEAD_W), lambda i: (0, i, 0)), tab, tab], out_specs=tab,
        compiler_params=_params(("arbitrary",)),
    )(d, ctab, stab)


def _attn_tiles(r):
    t = _row_tile(r)
    return t, t


def _mask(q0, k0, nq_, nk_, keys_on_rows=False):
    shape = (nk_, nq_) if keys_on_rows else (nq_, nk_)
    rq = q0 + lax.broadcasted_iota(jnp.int32, shape, 1 if keys_on_rows else 0)
    rk = k0 + lax.broadcasted_iota(jnp.int32, shape, 0 if keys_on_rows else 1)
    return ((rk >> CHUNK_SHIFT) <= (rq >> CHUNK_SHIFT)) & (rk >= META_ROW0)


ATTN_FWD_HEADS = 8
ATTN_BWD_HEADS = 2


SM_SCALE = 1.0 / math.sqrt(QK_NOPE + QK_ROPE)
LOG2E = math.log2(math.e)
Q_SCALE = SM_SCALE * LOG2E


def attn_fwd(q, k, v, name):
    nh, r, dk = q.shape
    dv = v.shape[-1]
    tq, tk = _attn_tiles(r)
    nq, nk = r // tq, r // tk

    def last_k(i):
        return ((i + 1) * tq - 1) // tk

    pairs = [(i, j) for i in range(nq) for j in range(last_k(i) + 1)]
    qi_tab = jnp.asarray([p[0] for p in pairs], jnp.int32)
    kj_tab = jnp.asarray([p[1] for p in pairs], jnp.int32)
    hb = _tile(nh, ATTN_FWD_HEADS, 1)

    def body(qi_ref, kj_ref, q_ref, k_ref, v_ref, o_ref, lse_ref, m_sc, l_sc, acc_sc):
        t = pl.program_id(1)
        i, j = qi_ref[t], kj_ref[t]

        @pl.when(j == 0)
        def _():
            m_sc[...] = jnp.full_like(m_sc, NEG)
            l_sc[...] = jnp.zeros_like(l_sc)
            acc_sc[...] = jnp.zeros_like(acc_sc)

        def step(masked):
            def one_head(hh, carry):
                s = lax.dot_general(q_ref[hh], k_ref[hh], NT, preferred_element_type=F32)
                if masked:
                    s = jnp.where(_mask(i * tq, j * tk, tq, tk), s, NEG)
                m_old = m_sc[hh]
                m_new = jnp.maximum(m_old, jnp.max(s, axis=-1, keepdims=True))
                alpha = jnp.exp2(m_old - m_new)
                p = jnp.exp2(s - jnp.tile(m_new, (1, tk // LANES)))
                l_sc[hh] = alpha * l_sc[hh] + jnp.sum(p, axis=-1, keepdims=True)
                acc_sc[hh] = (alpha[:, :dv] * acc_sc[hh]
                              + jnp.dot(p.astype(BF16), v_ref[hh], preferred_element_type=F32))
                m_sc[hh] = m_new
                return carry

            lax.fori_loop(0, hb, one_head, 0)

        needs_mask = (j == last_k(i)) | (j == 0)
        pl.when(needs_mask)(functools.partial(step, True))
        pl.when(jnp.logical_not(needs_mask))(functools.partial(step, False))

        @pl.when(j == last_k(i))
        def _():
            def one_head(hh, carry):
                l = l_sc[hh]
                o_ref[hh] = (acc_sc[hh] / l[:, :dv]).astype(BF16)
                lse_ref[hh] = (m_sc[hh] + jnp.log2(l))[:, :1]
                return carry

            lax.fori_loop(0, hb, one_head, 0)

    qspec = lambda w: pl.BlockSpec((hb, tq, w), lambda h, t, qi, kj: (h, qi[t], 0))
    kspec = lambda w: pl.BlockSpec((hb, tk, w), lambda h, t, qi, kj: (h, kj[t], 0))
    return pl.pallas_call(
        body, name=name,
        out_shape=[jax.ShapeDtypeStruct((nh, r, dv), BF16), jax.ShapeDtypeStruct((nh, r, 1), F32)],
        grid_spec=pltpu.PrefetchScalarGridSpec(
            num_scalar_prefetch=2, grid=(nh // hb, len(pairs)),
            in_specs=[qspec(dk), kspec(dk), kspec(dv)], out_specs=[qspec(dv), qspec(1)],
            scratch_shapes=[pltpu.VMEM((hb, tq, LANES), F32), pltpu.VMEM((hb, tq, LANES), F32),
                            pltpu.VMEM((hb, tq, dv), F32)]),
        compiler_params=_params(("arbitrary", "arbitrary"), VMEM_BIG),
    )(qi_tab, kj_tab, q, k, v)


def attn_bwd(q, k, v, do, lse, delta, dk_prev, dv_prev, name):
    nh, r, dk = q.shape
    dv = v.shape[-1]
    tq, tk = _attn_tiles(r)
    nq, nk = r // tq, r // tk
    has_prev = dk_prev is not None

    def first_q(j):
        return (j * tk) // tq

    pairs = [(j, i) for j in range(nk) for i in range(first_q(j), nq)]
    kj_tab = jnp.asarray([p[0] for p in pairs], jnp.int32)
    qi_tab = jnp.asarray([p[1] for p in pairs], jnp.int32)
    hb = _tile(nh, ATTN_BWD_HEADS, 1)

    def body(*refs):
        if has_prev:
            (kj_ref, qi_ref, q_ref, k_ref, v_ref, do_ref, lse_ref, dl_ref, pk_ref, pv_ref,
             dq_ref, dk_ref, dv_ref, dk_sc, dv_sc) = refs
        else:
            (kj_ref, qi_ref, q_ref, k_ref, v_ref, do_ref, lse_ref, dl_ref,
             dq_ref, dk_ref, dv_ref, dk_sc, dv_sc) = refs
        t = pl.program_id(1)
        j, i = kj_ref[t], qi_ref[t]

        @pl.when(t == 0)
        def _():
            dq_ref[...] = jnp.zeros_like(dq_ref)

        @pl.when(i == first_q(j))
        def _():
            dk_sc[...] = jnp.zeros_like(dk_sc)
            dv_sc[...] = jnp.zeros_like(dv_sc)

        def step(masked):
            qrows = pl.ds(pl.multiple_of(i * tq, tq), tq)
            for hh in range(hb):
                q_, k_, do_ = q_ref[hh], k_ref[hh], do_ref[hh]
                st = lax.dot_general(k_, q_, NT, preferred_element_type=F32)
                if masked:
                    st = jnp.where(_mask(i * tq, j * tk, tq, tk, keys_on_rows=True), st, NEG)
                pt = jnp.exp2(st - lse_ref[hh])
                dv_sc[hh] += jnp.dot(pt.astype(BF16), do_, preferred_element_type=F32)
                dpt = lax.dot_general(v_ref[hh], do_, NT, preferred_element_type=F32)
                dst = (pt * (dpt - dl_ref[hh])).astype(BF16)
                dk_sc[hh] += jnp.dot(dst, q_, preferred_element_type=F32)
                dq_ref[hh, qrows, :] += lax.dot_general(dst, k_, TN, preferred_element_type=F32)

        needs_mask = (i == first_q(j)) | (j == 0)
        pl.when(needs_mask)(functools.partial(step, True))
        pl.when(jnp.logical_not(needs_mask))(functools.partial(step, False))

        @pl.when(i == nq - 1)
        def _():
            dk_ = dk_sc[...] * (1.0 / LOG2E)
            dv_ = dv_sc[...]
            if has_prev:
                dk_ = dk_ + pk_ref[...]
                dv_ = dv_ + pv_ref[...]
            dk_ref[...] = dk_
            dv_ref[...] = dv_

    krow = lambda w: pl.BlockSpec((hb, tk, w), lambda h, t, kj, qi: (h, kj[t], 0))
    qrow = lambda w: pl.BlockSpec((hb, tq, w), lambda h, t, kj, qi: (h, qi[t], 0))
    qvec = pl.BlockSpec((hb, 1, tq), lambda h, t, kj, qi: (h, 0, qi[t]))
    in_specs = [qrow(dk), krow(dk), krow(dv), qrow(dv), qvec, qvec]
    ins = [q, k, v, do, lse, delta]
    if has_prev:
        in_specs += [krow(dk), krow(dv)]
        ins += [dk_prev, dv_prev]
    return pl.pallas_call(
        body, name=name,
        out_shape=[jax.ShapeDtypeStruct((nh, r, dk), F32), jax.ShapeDtypeStruct((nh, r, dk), F32),
                   jax.ShapeDtypeStruct((nh, r, dv), F32)],
        grid_spec=pltpu.PrefetchScalarGridSpec(
            num_scalar_prefetch=2, grid=(nh // hb, len(pairs)), in_specs=in_specs,
            out_specs=[pl.BlockSpec((hb, r, dk), lambda h, t, kj, qi: (h, 0, 0)), krow(dk), krow(dv)],
            scratch_shapes=[pltpu.VMEM((hb, tk, dk), F32), pltpu.VMEM((hb, tk, dv), F32)]),
        compiler_params=_params(("arbitrary", "arbitrary"), VMEM_BIG),
    )(kj_tab, qi_tab, *ins)


def attn_out_bwd(dattn, wo, o, name):
    r, d = dattn.shape
    nh, dv, _ = wo.shape
    tm = _row_tile(r)

    def body(da_ref, w_ref, o_ref, do_ref, dl_ref):
        da = da_ref[...].astype(BF16)
        for h in range(nh):
            do_ = lax.dot_general(da, w_ref[h], NT, preferred_element_type=F32).astype(BF16)
            do_ref[h] = do_
            dl_ref[h] = jnp.sum(do_.astype(F32) * o_ref[h].astype(F32), axis=-1, keepdims=True)

    return pl.pallas_call(
        body, name=name,
        out_shape=[jax.ShapeDtypeStruct((nh, r, dv), BF16), jax.ShapeDtypeStruct((nh, r, 1), F32)],
        grid=(r // tm,),
        in_specs=[pl.BlockSpec((tm, d), lambda i: (i, 0)), pl.BlockSpec((nh, dv, d), lambda i: (0, 0, 0)),
                  pl.BlockSpec((nh, tm, dv), lambda i: (0, i, 0))],
        out_specs=[pl.BlockSpec((nh, tm, dv), lambda i: (0, i, 0)), pl.BlockSpec((nh, tm, 1), lambda i: (0, i, 0))],
        compiler_params=_params(("arbitrary",)),
    )(dattn, wo, o)


def _pool_counts(row0, n, window):
    rows = row0 + lax.broadcasted_iota(jnp.int32, (n, 1), 0)
    cnt = jnp.clip(rows - META_ROW0 + 1, 1, window)
    return 1.0 / cnt.astype(F32)


def pool_fwd(h, gamma, wp, scale, name):
    r, d = h.shape
    ng, cg, _ = wp.shape
    tm = _row_tile(r)
    hb = tm // HALO

    def body(h_ref, hp_ref, g_ref, w_ref, sc_ref, o_ref):
        i = pl.program_id(0)
        xm = h_ref[...]
        xp = hp_ref[...] * jnp.where(i > 0, 1.0, 0.0)
        xx = jnp.concatenate([xp, xm], axis=0)
        xh, _ = _rms(xx)
        u = xh * g_ref[...]
        for g, win in enumerate(POOL_WINDOWS):
            sl = slice(g * cg, (g + 1) * cg)
            ug = u[:, sl]
            acc, k = ug, 1
            while k < win:
                acc = acc + pltpu.roll(acc, k, 0)
                k *= 2
            pooled = acc[HALO:] * _pool_counts(i * tm, tm, win) - ug[HALO:]
            y = jnp.dot(pooled.astype(BF16), w_ref[g], preferred_element_type=F32)
            o_ref[:, sl] = xm[:, sl] + y * sc_ref[:, sl]

    return pl.pallas_call(
        body, name=name, out_shape=jax.ShapeDtypeStruct((r, d), F32), grid=(r // tm,),
        in_specs=[pl.BlockSpec((tm, d), lambda i: (i, 0)),
                  pl.BlockSpec((HALO, d), lambda i: (jnp.maximum(i * hb - 1, 0), 0)),
                  pl.BlockSpec((1, d), lambda i: (0, 0)), pl.BlockSpec((ng, cg, cg), lambda i: (0, 0, 0)),
                  pl.BlockSpec((1, d), lambda i: (0, 0))],
        out_specs=pl.BlockSpec((tm, d), lambda i: (i, 0)),
        compiler_params=_params(("arbitrary",), VMEM_BIG),
    )(h, h, gamma, wp, scale)


def pool_bwd(h, gamma, wp, scale, dh, name):
    r, d = h.shape
    ng, cg, _ = wp.shape
    tm = _row_tile(r)
    hb = tm // HALO
    nt = r // tm

    def body(h_ref, hp_ref, dh_ref, dn_ref, g_ref, w_ref, sc_ref, dx_ref, dgam_ref, dw_ref, dsc_ref, du_sc):
        i = pl.program_id(0)

        @pl.when(i == 0)
        def _():
            dgam_ref[...] = jnp.zeros_like(dgam_ref)
            dw_ref[...] = jnp.zeros_like(dw_ref)
            dsc_ref[...] = jnp.zeros_like(dsc_ref)

        xm = h_ref[...]
        xp = hp_ref[...] * jnp.where(i > 0, 1.0, 0.0)
        xh_all, rstd_all = _rms(jnp.concatenate([xp, xm], axis=0))
        u = xh_all * g_ref[...]
        dm = dh_ref[...]
        dn = dn_ref[...] * jnp.where(i < nt - 1, 1.0, 0.0)
        dd = jnp.concatenate([dm, dn], axis=0)
        for g, win in enumerate(POOL_WINDOWS):
            sl = slice(g * cg, (g + 1) * cg)
            ug = u[:, sl]
            acc, k = ug, 1
            while k < win:
                acc = acc + pltpu.roll(acc, k, 0)
                k *= 2
            pooled = (acc[HALO:] * _pool_counts(i * tm, tm, win) - ug[HALO:]).astype(BF16)
            y = jnp.dot(pooled, w_ref[g], preferred_element_type=F32)
            dsc_ref[:, sl] += jnp.sum(dm[:, sl] * y, axis=0, keepdims=True)
            dyp = (dd[:, sl] * sc_ref[:, sl]).astype(BF16)
            dw_ref[g] += lax.dot_general(pooled, dyp[:tm], TN, preferred_element_type=F32)
            dpo = lax.dot_general(dyp, w_ref[g], NT, preferred_element_type=F32)
            z = dpo * _pool_counts(i * tm, tm + HALO, win)
            fwd, k = z, 1
            while k < win:
                fwd = fwd + pltpu.roll(fwd, tm + HALO - k, 0)
                k *= 2
            du_sc[:, sl] = fwd[:tm] - dpo[:tm]
        du = du_sc[...]
        xh, rstd = xh_all[HALO:], rstd_all[HALO:]
        dgam_ref[...] += jnp.sum(du * xh, axis=0, keepdims=True)
        dx_ref[...] = dm + _rms_bwd(xh, rstd, du * g_ref[...])

    row = pl.BlockSpec((tm, d), lambda i: (i, 0))
    vec = pl.BlockSpec((1, d), lambda i: (0, 0))
    prev = pl.BlockSpec((HALO, d), lambda i: (jnp.maximum(i * hb - 1, 0), 0))
    nxt = pl.BlockSpec((HALO, d), lambda i: (jnp.minimum((i + 1) * hb, r // HALO - 1), 0))
    wsp = pl.BlockSpec((ng, cg, cg), lambda i: (0, 0, 0))
    return pl.pallas_call(
        body, name=name,
        out_shape=[jax.ShapeDtypeStruct((r, d), F32), jax.ShapeDtypeStruct((1, d), F32),
                   jax.ShapeDtypeStruct((ng, cg, cg), F32), jax.ShapeDtypeStruct((1, d), F32)],
        grid=(nt,), in_specs=[row, prev, row, nxt, vec, wsp, vec], out_specs=[row, vec, wsp, vec],
        scratch_shapes=[pltpu.VMEM((tm, d), F32)],
        compiler_params=_params(("arbitrary",), VMEM_BIG),
    )(h, h, dh, dh, gamma, wp, scale)


def loss_head(h, gamma, target, seq, name):
    r, d = h.shape
    tm = _row_tile(r)

    def body(h_ref, g_ref, t_ref, sse_ref, dh_ref, dgam_ref):
        i = pl.program_id(0)

        @pl.when(i == 0)
        def _():
            sse_ref[...] = jnp.zeros_like(sse_ref)
            dgam_ref[...] = jnp.zeros_like(dgam_ref)

        xh, rstd = _rms(h_ref[...])
        rows = i * tm + lax.broadcasted_iota(jnp.int32, (tm, 1), 0)
        valid = ((rows >= FRONT) & (rows < FRONT + seq)).astype(F32)
        e = (xh * g_ref[...] - t_ref[...]) * valid
        sse_ref[...] += jnp.sum(jnp.sum(e * e, axis=1, keepdims=True), axis=0, keepdims=True)
        dy = e * (1.0 / d)
        dgam_ref[...] += jnp.sum(dy * xh, axis=0, keepdims=True)
        dh_ref[...] = _rms_bwd(xh, rstd, dy * g_ref[...])

    row = pl.BlockSpec((tm, d), lambda i: (i, 0))
    vec = pl.BlockSpec((1, d), lambda i: (0, 0))
    return pl.pallas_call(
        body, name=name,
        out_shape=[jax.ShapeDtypeStruct((1, 1), F32), jax.ShapeDtypeStruct((r, d), F32),
                   jax.ShapeDtypeStruct((1, d), F32)],
        grid=(r // tm,), in_specs=[row, vec, row],
        out_specs=[pl.BlockSpec((1, 1), lambda i: (0, 0)), row, vec],
        compiler_params=_params(("arbitrary",)),
    )(h, gamma, target)


SMALL = ("meta_tokens", "pool_w", "pool_scale", "w_dkv", "w_uk", "w_uv", "w_dq", "w_uq", "w_o")


def _pack(parts):
    flat = jnp.concatenate([p.reshape(-1) for p in parts])
    n = flat.shape[0]
    unit = PACK_W * 2 * SUBLANES
    n_pad = -(-n // unit) * unit
    return jnp.pad(flat, (0, n_pad - n)).reshape(n_pad // PACK_W, PACK_W)


def _unpack(buf, shapes, lead=()):
    flat = buf.reshape(lead + (-1,))
    out, off = [], 0
    for shp in shapes:
        n = math.prod(shp)
        out.append(flat[..., off:off + n].reshape(lead + tuple(shp)))
        off += n
    return out


def _cols_from_shards(a, axis):
    a = jnp.moveaxis(a, 0, axis)
    shp = a.shape
    return a.reshape(shp[:axis] + (shp[axis] * shp[axis + 1],) + shp[axis + 2:])


def _cols_to_shards(a, axis):
    shp = a.shape
    a = a.reshape(shp[:axis] + (N_SHARD, shp[axis] // N_SHARD) + shp[axis + 1:])
    return jnp.moveaxis(a, axis, 0)


SMALL_AXIS = {"meta_tokens": 1, "pool_w": 2, "pool_scale": 1, "w_dkv": 0, "w_uk": 1, "w_uv": 1,
              "w_dq": 1, "w_uq": 2, "w_o": 2}


def kernel(x, meta_tokens, ffn1_norm, ffn1_w_gate, ffn1_w_up, ffn1_w_down, mix_norm, ffn2_norm, ffn2_w_gate, ffn2_w_up, ffn2_w_down, pool_w, pool_scale, kv_in_norm, w_dkv, kv_latent_norm, w_uk, w_uv, w_dq, q_latent_norm, w_uq, w_o, final_norm, loss_target, m_meta_tokens, m_ffn1_norm, m_ffn1_w_gate, m_ffn1_w_up, m_ffn1_w_down, m_mix_norm, m_ffn2_norm, m_ffn2_w_gate, m_ffn2_w_up, m_ffn2_w_down, m_pool_w, m_pool_scale, m_kv_in_norm, m_w_dkv, m_kv_latent_norm, m_w_uk, m_w_uv, m_w_dq, m_q_latent_norm, m_w_uq, m_w_o, m_final_norm, v_meta_tokens, v_ffn1_norm, v_ffn1_w_gate, v_ffn1_w_up, v_ffn1_w_down, v_mix_norm, v_ffn2_norm, v_ffn2_w_gate, v_ffn2_w_up, v_ffn2_w_down, v_pool_w, v_pool_scale, v_kv_in_norm, v_w_dkv, v_kv_latent_norm, v_w_uk, v_w_uv, v_w_dq, v_q_latent_norm, v_w_uq, v_w_o, v_final_norm):
    args = dict(locals())
    W = {n: args[n] for n in NAMES}
    M = {n: args["m_" + n] for n in NAMES}
    V = {n: args["v_" + n] for n in NAMES}
    TRANSPOSED = ("ffn1_w_gate", "ffn1_w_up", "ffn2_w_gate", "ffn2_w_up")
    for n in TRANSPOSED:
        W[n], M[n], V[n] = (jnp.swapaxes(a, 1, 2) for a in (W[n], M[n], V[n]))

    depth = ffn1_norm.shape[0]
    n_a = pool_w.shape[0]
    seq, d = x.shape[1], x.shape[2]
    nh = N_HEADS
    r = -(-(FRONT + seq) // LANES) * LANES

    cx, cy, cc = lax.axis_index("x"), lax.axis_index("y"), lax.axis_index("c")
    c_arr = jnp.reshape(cc, (1,)).astype(jnp.int32)
    s_arr = jnp.reshape(2 * cx + cy, (1,)).astype(jnp.int32)

    small_shapes = [W[n].shape for n in SMALL]

    ffn_src = {f: tuple(W[f + t].astype(BF16) for t in ("_w_gate", "_w_up", "_w_down")) for f in ("ffn1", "ffn2")}
    ffn_order = [(f, l) for l in range(depth) for f in ("ffn1", "ffn2")]
    ffn_w = {}
    gate = []
    ag_state = [None]

    def gated(a):
        if gate:
            a = a + sum(gate[1:], gate[0]).astype(a.dtype)
            gate.clear()
        return a

    def vec(a):
        return gated(a.reshape(1, -1))

    def ag_start(idx, dep):
        f, l = ffn_order[idx]
        shards = [w_[l] + dep for w_ in ffn_src[f]]
        lands = [lax.dynamic_update_slice(lax.empty((N_SHARD,) + a.shape, BF16), a[None], (2 * cx + cy, 0, 0))
                 for a in shards]
        ag_state[0] = exchange_start(shards, lands, f"ag_start_{f}_{l}", True)
        gate.append(ag_state[0][4][0, 0])

    def ag_wait(idx, after):
        f, l = ffn_order[idx]
        _, lands = exchange_wait(ag_state[0], after, f"ag_wait_{f}_{l}", True)
        ffn_w[f, l] = lands
        if idx + 1 < len(ffn_order):
            ag_start(idx + 1, lands[0][0, 0, 0] * jnp.zeros((), BF16))

    ag_start(0, jnp.zeros((), BF16))
    gathered = all_gather_shards([gated(_pack([W[n] for n in SMALL]))], "ag_small")[0]
    small_full = {}
    for n, part in zip(SMALL, _unpack(gathered, small_shapes, (N_SHARD,))):
        small_full[n] = _cols_from_shards(part, SMALL_AXIS[n])

    meta_full = small_full["meta_tokens"]
    wp = small_full["pool_w"].astype(BF16)
    pscale = small_full["pool_scale"]
    wdkv = jnp.pad(small_full["w_dkv"], ((0, 0), (0, HEAD_W - QK_ROPE))).astype(BF16)[None]
    wuk = small_full["w_uk"].reshape(KV_RANK, nh, QK_NOPE).transpose(1, 0, 2)
    wk_h = jnp.concatenate([jnp.zeros((nh, KV_RANK, HEAD_W - QK_NOPE), F32), wuk], axis=-1).astype(BF16)
    wv_h = small_full["w_uv"].reshape(KV_RANK, nh, V_HEAD).transpose(1, 0, 2).astype(BF16)
    wdq = small_full["w_dq"].astype(BF16)
    wuq = small_full["w_uq"].reshape(-1, Q_RANK, nh, QK_NOPE + QK_ROPE).transpose(0, 2, 1, 3)
    wq_h = jnp.concatenate([wuq[..., QK_NOPE:], jnp.zeros(wuq.shape[:-1] + (HEAD_W - QK_NOPE - QK_ROPE,), F32),
                            wuq[..., :QK_NOPE]], axis=-1).astype(BF16)
    wo_h = small_full["w_o"].reshape(-1, nh, V_HEAD, d).astype(BF16)

    ctab, stab = rope_tables(r)

    h = jnp.concatenate([jnp.zeros((META_ROW0, d), F32), meta_full, x[0],
                         jnp.zeros((r - FRONT - seq, d), F32)], axis=0)
    target = jnp.concatenate([jnp.zeros((FRONT, d), F32), loss_target[0],
                              jnp.zeros((r - FRONT - seq, d), F32)], axis=0)
    saved = []
    kv = None
    for l in range(depth):
        sv = {"h0": h}
        ag_wait(2 * l, h)
        h, sv["g1"], sv["u1"] = ffn_fwd(h, vec(ffn1_norm[l]), *ffn_w["ffn1", l], f"ffn1_fwd_{l}")
        sv["h1"] = h
        if l < n_a:
            h = pool_fwd(h, vec(mix_norm[l]), wp[l], vec(pscale[l]), f"pool_fwd_{l}")
        else:
            j = l - n_a
            u = norm_fwd(h, vec(mix_norm[l]), f"mixnorm_{l}")
            cq0 = rowmm(u[None], wdq[j][None], f"dq_{l}")
            cq = norm_fwd(cq0, vec(q_latent_norm[j]), f"qnorm_{l}")
            q = proj_rope(cq, wq_h[j], ctab, stab, None, f"qproj_{l}", scale=Q_SCALE)
            o, lse = attn_fwd(q, kv["k"], kv["v"], f"attn_fwd_{l}")
            h = rowmm(o, wo_h[j], f"oproj_{l}", res=h)
            sv.update(u=u, cq0=cq0, cq=cq, q=q, o=o, lse=lse)
        sv["h2"] = h
        ag_wait(2 * l + 1, h)
        h, sv["g2"], sv["u2"] = ffn_fwd(h, vec(ffn2_norm[l]), *ffn_w["ffn2", l], f"ffn2_fwd_{l}")
        saved.append(sv)
        if l == n_a - 1:
            hkv = norm_fwd(h, vec(kv_in_norm), "kvin_norm")
            ckr = rowmm(hkv[None], wdkv, "dkv")
            ckv = norm_fwd(ckr, vec(kv_latent_norm), "kvlat_norm")
            kv = {"h": h, "hkv": hkv, "ckr": ckr, "ckv": ckv,
                  "k": proj_rope(ckv, wk_h, ctab, stab, ckr, "kproj"),
                  "v": rowmm(ckv[None], wv_h, "vproj", out_dtype=BF16, heads_out=True)}

    sse, dh, dfinal = loss_head(h, vec(final_norm), target, seq, "loss_head")
    loss = lax.psum(0.5 / d * sse[0, 0], ("x", "y", "c"))

    G = {}
    FFN = ("ffn1_w_gate", "ffn1_w_up", "ffn1_w_down", "ffn2_w_gate", "ffn2_w_up", "ffn2_w_down")
    per = {n: [lax.empty(W[n].shape, F32) for _ in range(4)] for n in FFN}
    pending = []

    def rs_complete(after):
        f, l, state = pending.pop()
        names = [f + "_w_gate", f + "_w_up", f + "_w_down"]
        for n, g_ in zip(names, rs_finish(state, after, s_arr, c_arr, f"{f}_{l}")):
            per[n] = adamw_into(W[n], g_, M[n], V[n], per[n], l, f"adamw_{n}_{l}")
        gate.append(per[names[-1]][1][0, 0, 0] * 0.0)

    dnorm = {n: [None] * depth for n in ("ffn1_norm", "mix_norm", "ffn2_norm")}
    dqnorm = [None] * (depth - n_a)
    dpool_w, dpool_scale = [None] * n_a, [None] * n_a
    dwdq, dwq_h, dwo_h = [None] * (depth - n_a), [None] * (depth - n_a), [None] * (depth - n_a)

    def ffn_backward(f, l, h_in, dh_, gg, uu):
        gam = ffn1_norm[l] if f == "ffn1" else ffn2_norm[l]
        wg_, wu_, wd_ = ffn_w[f, l]
        dh_in, dgam, hn, dy, dg, du, a = ffn_bwd_act(h_in, vec(gam), dh_, gg, uu, wg_, wu_, wd_, f"{f}_bwd_act_{l}")
        dwg, dwu, dwd = ffn_bwd_weights(hn, dy, a, dg, du, f"{f}_bwd_w_{l}")
        state = rs_begin([dwg, dwu, dwd], c_arr, f"{f}_{l}")
        gate.append(state[4][0, 0])
        if pending:
            rs_complete(state[4])
        pending.append((f, l, state))
        dnorm[f + "_norm"][l] = dgam[0]
        return dh_in

    dk_tot = dv_tot = None
    for l in reversed(range(depth)):
        sv = saved[l]
        if l == n_a - 1:
            dckv = rowmm(dk_tot, gated(wk_h), "kproj_bwd", nt=True)
            dkr = rope_bwd_sum(dk_tot, ctab, stab, "kproj_bwd_rope")
            dckv = rowmm(dv_tot, wv_h, "vproj_bwd", nt=True, res=dckv)
            dwk_h = tnmm(kv["ckv"][None], dk_tot, "kproj_bwd_w")
            dwv_h = tnmm(kv["ckv"][None], dv_tot, "vproj_bwd_w")
            dlat, dkvlat = norm_bwd(kv["ckr"], vec(kv_latent_norm), dckv, None, "kvlat_norm_bwd")
            dckr = jnp.concatenate([dlat, dkr], axis=1).astype(BF16)
            dhkv = rowmm(dckr[None], wdkv, "dkv_bwd", nt=True)
            dwdkv = tnmm(kv["hkv"][None], dckr[None], "dkv_bwd_w")[0]
            dh, dkvin = norm_bwd(kv["h"], vec(kv_in_norm), dhkv, dh, "kvin_norm_bwd")
            G["w_dkv"] = dwdkv[:, :KV_RANK + QK_ROPE]
            G["w_uk"] = dwk_h[..., HEAD_W - QK_NOPE:].transpose(1, 0, 2).reshape(KV_RANK, nh * QK_NOPE)
            G["w_uv"] = dwv_h.transpose(1, 0, 2).reshape(KV_RANK, nh * V_HEAD)
        dh = ffn_backward("ffn2", l, sv["h2"], dh, sv["g2"], sv["u2"])
        if l < n_a:
            dh, dmix, dpool_w[l], dps = pool_bwd(sv["h1"], vec(mix_norm[l]), wp[l], vec(pscale[l]), dh, f"pool_bwd_{l}")
            dnorm["mix_norm"][l] = dmix[0]
            dpool_scale[l] = dps[0]
        else:
            j = l - n_a
            do, delta = attn_out_bwd(dh, gated(wo_h[j]), sv["o"], f"oproj_bwd_{l}")
            dwo_h[j] = tnmm(sv["o"], dh[None], f"oproj_bwd_w_{l}")
            dq, dk_tot, dv_tot = attn_bwd(sv["q"], kv["k"], kv["v"], do, sv["lse"].reshape(nh, 1, r),
                                          delta.reshape(nh, 1, r), dk_tot, dv_tot, f"attn_bwd_{l}")
            dxq = rope_bwd_heads(dq, ctab, stab, f"qproj_bwd_rope_{l}", scale=SM_SCALE)
            dcq = rowmm(dxq, wq_h[j], f"qproj_bwd_{l}", nt=True)
            dwq_h[j] = tnmm(sv["cq"][None], dxq, f"qproj_bwd_w_{l}")
            dcq0, dqn = norm_bwd(sv["cq0"], vec(q_latent_norm[j]), dcq, None, f"qnorm_bwd_{l}")
            dqnorm[j] = dqn[0]
            dcq0b = dcq0.astype(BF16)
            du = rowmm(dcq0b[None], wdq[j][None], f"dq_bwd_{l}", nt=True)
            dwdq[j] = tnmm(sv["u"][None], dcq0b[None], f"dq_bwd_w_{l}")[0]
            dh, dmix = norm_bwd(sv["h1"], vec(mix_norm[l]), du, dh, f"mixnorm_bwd_{l}")
            dnorm["mix_norm"][l] = dmix[0]
        dh = ffn_backward("ffn1", l, sv["h0"], dh, sv["g1"], sv["u1"])

    grad_x = dh[FRONT:FRONT + seq][None]
    G["meta_tokens"] = dh[META_ROW0:FRONT]
    G["pool_w"] = jnp.stack(dpool_w)
    G["pool_scale"] = jnp.stack(dpool_scale)
    G["w_dq"] = jnp.stack(dwdq)
    dwq = jnp.stack(dwq_h)
    dwq = jnp.concatenate([dwq[..., HEAD_W - QK_NOPE:], dwq[..., :QK_ROPE]], axis=-1)
    G["w_uq"] = dwq.transpose(0, 2, 1, 3).reshape(-1, Q_RANK, nh * (QK_NOPE + QK_ROPE))
    G["w_o"] = jnp.stack(dwo_h).reshape(-1, nh * V_HEAD, d)

    REPL = ("ffn1_norm", "mix_norm", "ffn2_norm", "kv_in_norm", "kv_latent_norm", "q_latent_norm", "final_norm")
    grep = {"ffn1_norm": jnp.stack(dnorm["ffn1_norm"]), "mix_norm": jnp.stack(dnorm["mix_norm"]),
            "ffn2_norm": jnp.stack(dnorm["ffn2_norm"]), "kv_in_norm": dkvin[0], "kv_latent_norm": dkvlat[0],
            "q_latent_norm": jnp.stack(dqnorm), "final_norm": dfinal[0]}

    def pack128(parts):
        flat = jnp.concatenate([p.reshape(-1) for p in parts])
        n = flat.shape[0]
        n_pad = -(-n // (LANES * SUBLANES)) * (LANES * SUBLANES)
        return jnp.pad(flat, (0, n_pad - n)).reshape(-1, LANES)

    rep_shapes = [W[n].shape for n in REPL]
    g_rep = all_reduce_small(pack128([grep[n] for n in REPL]), "ar_repl")
    d_rep, m_rep, v_rep = adamw(pack128([W[n] for n in REPL]), g_rep, pack128([M[n] for n in REPL]),
                                pack128([V[n] for n in REPL]), "adamw_repl")
    out_g, out_d, out_m, out_v = {}, {}, {}, {}
    for dst, buf in ((out_g, g_rep), (out_d, d_rep), (out_m, m_rep), (out_v, v_rep)):
        for n, a in zip(REPL, _unpack(buf, rep_shapes)):
            dst[n] = a

    g_small = jnp.stack([_pack([_cols_to_shards(G[n], SMALL_AXIS[n])[s] for n in SMALL]) for s in range(N_SHARD)])
    g_small = reduce_scatter([gated(g_small)], c_arr, s_arr, "small")[0]
    d_s, m_s, v_s = adamw(_pack([W[n] for n in SMALL]), g_small, _pack([M[n] for n in SMALL]),
                          _pack([V[n] for n in SMALL]), "adamw_small")
    for dst, buf in ((out_g, g_small), (out_d, d_s), (out_m, m_s), (out_v, v_s)):
        for n, a in zip(SMALL, _unpack(buf, small_shapes)):
            dst[n] = a

    rs_complete(d_s)
    for n in FFN:
        out_g[n], out_d[n], out_m[n], out_v[n] = (
            jnp.swapaxes(a, 1, 2) if n in TRANSPOSED else a for a in per[n])

    return (loss, grad_x, *[out_g[n] for n in NAMES], *[out_d[n] for n in NAMES],
            *[out_m[n] for n in NAMES], *[out_v[n] for n in NAMES])


NAMES = ("meta_tokens", "ffn1_norm", "ffn1_w_gate", "ffn1_w_up", "ffn1_w_down", "mix_norm", "ffn2_norm",
         "ffn2_w_gate", "ffn2_w_up", "ffn2_w_down", "pool_w", "pool_scale", "kv_in_norm", "w_dkv",
         "kv_latent_norm", "w_uk", "w_uv", "w_dq", "q_latent_norm", "w_uq", "w_o", "final_norm")
```

```python
import functools
import math

import jax
import jax.numpy as jnp
from jax import lax
from jax.experimental import pallas as pl
from jax.experimental.pallas import tpu as pltpu

F32 = jnp.float32
BF16 = jnp.bfloat16
MESH = pl.DeviceIdType.MESH
ANY = pl.BlockSpec(memory_space=pl.ANY)

EPS = 1e-6
CHUNK = 64
CHUNK_SHIFT = 6
N_META = 16
FRONT = 64
META_ROW0 = FRONT - N_META
POOL_WINDOWS = (2, 4, 8, 16)
HALO = 16
N_HEADS = 8
QK_NOPE = 64
QK_ROPE = 32
V_HEAD = 64
HEAD_W = 128
KV_RANK = 256
Q_RANK = 384
ROPE_THETA = 10000.0
NEG = -1e30
N_SHARD = 4
LANES = 128
SUBLANES = 8
PACK_W = 512
VMEM_BIG = 52 * 1024 * 1024
VMEM_MAX = 60 * 1024 * 1024
WGRAD_ROWS = 1664

ADAM_LR = 0.001
ADAM_B1 = 0.9
ADAM_B2 = 0.999
ADAM_EPS = 1e-08
ADAM_WD = 0.01
ADAM_STEP = 10

NT = (((1,), (1,)), ((), ()))
TN = (((0,), (0,)), ((), ()))


def _params(sem=None, vmem=None):
    return pltpu.CompilerParams(dimension_semantics=sem, vmem_limit_bytes=vmem)


def _tile(n, pref, mult=SUBLANES):
    best = None
    for t in range(mult, min(n, pref) + 1, mult):
        if n % t == 0:
            best = t
    return best if best is not None else n


def _row_tile(r):
    return 640 if r % 640 == 0 else 128


def _rms(x):
    rstd = lax.rsqrt(jnp.mean(x * x, axis=-1, keepdims=True) + EPS)
    return x * rstd, rstd


def _rms_bwd(xh, rstd, dxh):
    return rstd * (dxh - xh * jnp.mean(dxh * xh, axis=-1, keepdims=True))


def _sigmoid(x):
    return 1.0 / (1.0 + jnp.exp(-x))


def _place():
    x, y, c = lax.axis_index("x"), lax.axis_index("y"), lax.axis_index("c")
    chips = [(1 - x, y), (x, 1 - y), (1 - x, 1 - y)]
    return x, y, c, chips


def all_gather_shards(shards, name):
    n = len(shards)
    slot = 2 * lax.axis_index("x") + lax.axis_index("y")
    lands = [lax.dynamic_update_slice(lax.empty((N_SHARD,) + a.shape, a.dtype), a[None], (slot, 0, 0)) for a in shards]

    def body(*refs):
        ins, outs = refs[:n], refs[2 * n:3 * n]
        send1, recv1, send2, recv2 = refs[3 * n:]
        x, y, c, chips = _place()
        s = 2 * x + y
        sib = (x, y, 1 - c)

        def rcopy(k, j, src, dst, to, first):
            return pltpu.make_async_remote_copy(
                src_ref=src, dst_ref=dst,
                send_sem=(send1 if first else send2).at[k, j],
                recv_sem=(recv1 if first else recv2).at[k, j],
                device_id=to, device_id_type=MESH)

        started = []
        for k in range(n):
            hf = ins[k].shape[0] // 2
            for j, (cx, cy) in enumerate(chips):
                r = rcopy(k, j, ins[k].at[pl.ds(c * hf, hf)], outs[k].at[s, pl.ds(c * hf, hf)],
                          (cx, cy, c), True)
                r.start()
                started.append(r)
        for k in range(n):
            hf = ins[k].shape[0] // 2
            for j, (cx, cy) in enumerate(chips):
                blk = outs[k].at[2 * cx + cy, pl.ds(c * hf, hf)]
                rcopy(k, j, blk, blk, (cx, cy, c), True).wait_recv()
                f = rcopy(k, j, blk, blk, sib, False)
                f.start()
                started.append(f)
        for k in range(n):
            hf = ins[k].shape[0] // 2
            for j, (cx, cy) in enumerate(chips):
                blk = outs[k].at[2 * cx + cy, pl.ds((1 - c) * hf, hf)]
                rcopy(k, j, blk, blk, sib, False).wait_recv()
        for r in started:
            r.wait_send()

    return pl.pallas_call(
        body, name=name,
        out_shape=[jax.ShapeDtypeStruct((N_SHARD,) + a.shape, a.dtype) for a in shards],
        in_specs=[ANY] * (2 * n), out_specs=[ANY] * n,
        input_output_aliases={n + k: k for k in range(n)},
        scratch_shapes=[pltpu.SemaphoreType.DMA((n, 3))] * 4,
    )(*shards, *lands)


def sibling_join_halves(ts, name):
    n = len(ts)

    def body(*refs):
        ins, outs = refs[:n], refs[n:2 * n]
        send, recv = refs[2 * n:]
        x, y, c, _ = _place()
        sib = (x, y, 1 - c)

        def copy(k, half):
            hf = ins[k].shape[0] // 2
            rows = pl.ds(half * hf, hf)
            return pltpu.make_async_remote_copy(
                src_ref=ins[k].at[rows], dst_ref=outs[k].at[rows],
                send_sem=send.at[k], recv_sem=recv.at[k], device_id=sib, device_id_type=MESH)

        cps = [copy(k, c) for k in range(n)]
        for r in cps:
            r.start()
        for k in range(n):
            copy(k, 1 - c).wait_recv()
        for r in cps:
            r.wait_send()

    return pl.pallas_call(
        body, name=name,
        out_shape=[jax.ShapeDtypeStruct(a.shape, a.dtype) for a in ts],
        in_specs=[ANY] * n, out_specs=[ANY] * n, input_output_aliases={k: k for k in range(n)},
        scratch_shapes=[pltpu.SemaphoreType.DMA((n,))] * 2,
    )(*ts)


def all_reduce_small(part, name):
    m, w = part.shape

    def body(x_ref, tot_ref, gat_ref, send_sems, recv_sems):
        x, y, c, chips = _place()
        me, sib = (x, y, c), (x, y, 1 - c)

        def slot(px, py, pc):
            return gat_ref.at[4 * px + 2 * py + pc]

        def copy(k, block, to, src=None):
            return pltpu.make_async_remote_copy(
                src_ref=slot(*block) if src is None else src, dst_ref=slot(*block),
                send_sem=send_sems.at[k], recv_sem=recv_sems.at[k], device_id=to, device_id_type=MESH)

        gat_ref[4 * x + 2 * y + c] = x_ref[...]
        first = [copy(0, me, sib, src=x_ref)]
        first += [copy(1 + j, me, (*chip, c), src=x_ref) for j, chip in enumerate(chips)]
        for cp in first:
            cp.start()
        passed = [copy(4 + j, (*chip, c), sib) for j, chip in enumerate(chips)]
        for j, chip in enumerate(chips):
            copy(1 + j, (*chip, c), me).wait_recv()
            passed[j].start()
        copy(0, sib, me).wait_recv()
        for j, chip in enumerate(chips):
            copy(4 + j, (*chip, 1 - c), me).wait_recv()
        for cp in first + passed:
            cp.wait_send()
        tot = gat_ref[0]
        for d in range(1, 8):
            tot = tot + gat_ref[d]
        tot_ref[...] = tot

    return pl.pallas_call(
        body, name=name,
        out_shape=jax.ShapeDtypeStruct((m, w), F32),
        in_specs=[pl.BlockSpec(memory_space=pltpu.VMEM)],
        out_specs=pl.BlockSpec(memory_space=pltpu.VMEM),
        scratch_shapes=[pltpu.VMEM((8, m, w), F32), pltpu.SemaphoreType.DMA((7,)), pltpu.SemaphoreType.DMA((7,))],
    )(part)


def adamw(w, g, m, v, name):
    a, b = w.shape
    tb = _tile(a, 256)
    c1 = 1.0 - ADAM_B1 ** ADAM_STEP
    c2 = 1.0 - ADAM_B2 ** ADAM_STEP

    def body(w_ref, g_ref, m_ref, v_ref, d_ref, mo_ref, vo_ref):
        g_ = g_ref[...]
        m_ = ADAM_B1 * m_ref[...] + (1.0 - ADAM_B1) * g_
        v_ = ADAM_B2 * v_ref[...] + (1.0 - ADAM_B2) * (g_ * g_)
        m_hat = m_ / c1
        v_hat = v_ / c2
        d_ref[...] = -ADAM_LR * (m_hat / (jnp.sqrt(v_hat) + ADAM_EPS) + ADAM_WD * w_ref[...])
        mo_ref[...] = m_
        vo_ref[...] = v_

    spec = pl.BlockSpec((tb, b), lambda i: (i, 0))
    return pl.pallas_call(
        body, name=name,
        out_shape=[jax.ShapeDtypeStruct((a, b), F32)] * 3,
        grid=(a // tb,), in_specs=[spec] * 4, out_specs=[spec] * 3,
        compiler_params=_params(("arbitrary",)),
    )(w, g, m, v)


def adamw_into(w_all, g, m_all, v_all, prev, l, name):
    nl, a, b = w_all.shape
    tb = _tile(a, 256)
    c1 = 1.0 - ADAM_B1 ** ADAM_STEP
    c2 = 1.0 - ADAM_B2 ** ADAM_STEP

    def body(w_ref, g_ref, m_ref, v_ref, p0, p1, p2, p3, go_ref, d_ref, mo_ref, vo_ref):
        g_ = g_ref[...]
        m_ = ADAM_B1 * m_ref[0] + (1.0 - ADAM_B1) * g_
        v_ = ADAM_B2 * v_ref[0] + (1.0 - ADAM_B2) * (g_ * g_)
        m_hat = m_ / c1
        v_hat = v_ / c2
        go_ref[0] = g_
        d_ref[0] = -ADAM_LR * (m_hat / (jnp.sqrt(v_hat) + ADAM_EPS) + ADAM_WD * w_ref[0])
        mo_ref[0] = m_
        vo_ref[0] = v_

    lay = pl.BlockSpec((1, tb, b), lambda i: (l, i, 0))
    return pl.pallas_call(
        body, name=name,
        out_shape=[jax.ShapeDtypeStruct((nl, a, b), F32)] * 4,
        grid=(a // tb,), in_specs=[lay, pl.BlockSpec((tb, b), lambda i: (i, 0)), lay, lay] + [ANY] * 4,
        out_specs=[lay] * 4, input_output_aliases={4: 0, 5: 1, 6: 2, 7: 3},
        compiler_params=_params(("arbitrary",)),
    )(w_all, g, m_all, v_all, *prev)


def reduce_scatter(gs, c_arr, s_arr, tag):
    lands = rs_exchange(gs, f"rs_exchange_{tag}")
    ts = [sum_eight(g, r, s_arr, c_arr, f"rs_sum_{tag}_{k}") for k, (g, r) in enumerate(zip(gs, lands))]
    return sibling_join_halves(ts, f"rs_join_{tag}")


HBM_SPEC = pl.BlockSpec(memory_space=pltpu.HBM)
SEM_SPEC = pl.BlockSpec(memory_space=pltpu.SEMAPHORE)
EFFECT = pltpu.SideEffectType.DATAFLOW_SIDE_EFFECTING


def _in_hbm(a):
    return pltpu.with_memory_space_constraint(a, pltpu.HBM)


def _exchange_copy(k, j, chip, c, s, srcs, lands, send, recv, gather, receiving):
    cx, cy = chip
    src = srcs[k] if gather else srcs[k].at[2 * cx + cy]
    if gather:
        dst = lands[k].at[2 * cx + cy] if receiving else lands[k].at[s]
    else:
        dst = lands[k].at[j]
    return pltpu.make_async_remote_copy(src_ref=src, dst_ref=dst, send_sem=send.at[3 * k + j], recv_sem=recv.at[3 * k + j],
                                        device_id=(cx, cy, c), device_id_type=MESH)


def exchange_start(srcs, lands, name, gather):
    n = len(srcs)

    def body(*refs):
        srcs_in, lands_in = refs[:n], refs[n:2 * n]
        send, recv = refs[2 * n], refs[2 * n + 1]
        token = refs[-1]
        x, y, c, chips = _place()
        for k in range(n):
            for j, chip in enumerate(chips):
                _exchange_copy(k, j, chip, c, 2 * x + y, srcs_in, lands_in, send, recv, gather, False).start()
        token[...] = jnp.zeros_like(token)

    outs = pl.pallas_call(
        body, name=name,
        out_shape=(pltpu.SemaphoreType.DMA((3 * n,)), pltpu.SemaphoreType.DMA((3 * n,)),
                   *[pltpu.HBM(a.shape, a.dtype) for a in srcs], *[pltpu.HBM(a.shape, a.dtype) for a in lands],
                   jax.ShapeDtypeStruct((SUBLANES, LANES), F32)),
        in_specs=[HBM_SPEC] * (2 * n),
        out_specs=(SEM_SPEC, SEM_SPEC, *[HBM_SPEC] * (2 * n), pl.BlockSpec(memory_space=pltpu.VMEM)),
        input_output_aliases={k: 2 + k for k in range(2 * n)},
        compiler_params=pltpu.CompilerParams(has_side_effects=EFFECT),
    )(*[_in_hbm(a) for a in srcs], *[_in_hbm(a) for a in lands])
    return outs[0], outs[1], list(outs[2:2 + n]), list(outs[2 + n:2 + 2 * n]), outs[-1]


def exchange_wait(state, after, name, gather):
    send, recv, srcs, lands, _ = state
    n = len(srcs)

    def body(*refs):
        srcs_in, lands_in = refs[:n], refs[n:2 * n]
        send_, recv_ = refs[2 * n], refs[2 * n + 1]
        x, y, c, chips = _place()
        for k in range(n):
            for j, chip in enumerate(chips):
                cp = _exchange_copy(k, j, chip, c, 2 * x + y, srcs_in, lands_in, send_, recv_, gather, True)
                cp.wait_send()
                cp.wait_recv()

    outs = pl.pallas_call(
        body, name=name,
        out_shape=tuple(pltpu.HBM(a.shape, a.dtype) for a in srcs + lands),
        in_specs=[HBM_SPEC] * (2 * n) + [SEM_SPEC, SEM_SPEC, ANY],
        out_specs=tuple([HBM_SPEC] * (2 * n)),
        input_output_aliases={k: k for k in range(2 * n)},
        compiler_params=pltpu.CompilerParams(has_side_effects=EFFECT),
    )(*srcs, *lands, send, recv, after)
    return list(outs[:n]), list(outs[n:])


RS_SLOTS = 7


def _rs_copies(k, gs, lands, send, recv, x, y, c, chips):
    hf = gs[k].shape[1] // 2
    base = RS_SLOTS * k
    out = []
    for j, (cx, cy) in enumerate(chips):
        for cd in range(2):
            out.append(pltpu.make_async_remote_copy(
                src_ref=gs[k].at[2 * cx + cy, pl.ds(cd * hf, hf)], dst_ref=lands[k].at[2 * j + c],
                send_sem=send.at[base + 2 * j + cd], recv_sem=recv.at[base + 2 * j + c],
                device_id=(cx, cy, cd), device_id_type=MESH))
    out.append(pltpu.make_async_remote_copy(
        src_ref=gs[k].at[2 * x + y, pl.ds((1 - c) * hf, hf)], dst_ref=lands[k].at[RS_SLOTS - 1],
        send_sem=send.at[base + RS_SLOTS - 1], recv_sem=recv.at[base + RS_SLOTS - 1],
        device_id=(x, y, 1 - c), device_id_type=MESH))
    return out


def _rs_wait_all(n, gs, lands, send, recv):
    x, y, c, chips = _place()
    for k in range(n):
        for cp in _rs_copies(k, gs, lands, send, recv, x, y, c, chips):
            cp.wait_send()
        hf = gs[k].shape[1] // 2
        for slot in range(RS_SLOTS):
            pltpu.make_async_remote_copy(
                src_ref=gs[k].at[0, pl.ds(0, hf)], dst_ref=lands[k].at[slot],
                send_sem=send.at[RS_SLOTS * k + slot], recv_sem=recv.at[RS_SLOTS * k + slot],
                device_id=(x, y, 1 - c), device_id_type=MESH).wait_recv()


def _rs_land_shapes(gs):
    return [(RS_SLOTS, g.shape[1] // 2, g.shape[2]) for g in gs]


def rs_exchange(gs, name):
    n = len(gs)

    def body(*refs):
        ins, outs = refs[:n], refs[n:2 * n]
        send, recv = refs[2 * n:]
        x, y, c, chips = _place()
        for k in range(n):
            for cp in _rs_copies(k, ins, outs, send, recv, x, y, c, chips):
                cp.start()
        _rs_wait_all(n, ins, outs, send, recv)

    return pl.pallas_call(
        body, name=name,
        out_shape=[jax.ShapeDtypeStruct(s, F32) for s in _rs_land_shapes(gs)],
        in_specs=[ANY] * n, out_specs=[ANY] * n,
        scratch_shapes=[pltpu.SemaphoreType.DMA((RS_SLOTS * n,))] * 2,
    )(*gs)


def rs_exchange_start(gs, name):
    n = len(gs)
    lands = [lax.empty(s, F32) for s in _rs_land_shapes(gs)]

    def body(*refs):
        ins, lands_in = refs[:n], refs[n:2 * n]
        send, recv = refs[2 * n], refs[2 * n + 1]
        token = refs[-1]
        x, y, c, chips = _place()
        for k in range(n):
            for cp in _rs_copies(k, ins, lands_in, send, recv, x, y, c, chips):
                cp.start()
        token[...] = jnp.zeros_like(token)

    outs = pl.pallas_call(
        body, name=name,
        out_shape=(pltpu.SemaphoreType.DMA((RS_SLOTS * n,)), pltpu.SemaphoreType.DMA((RS_SLOTS * n,)),
                   *[pltpu.HBM(a.shape, a.dtype) for a in gs], *[pltpu.HBM(a.shape, a.dtype) for a in lands],
                   jax.ShapeDtypeStruct((SUBLANES, LANES), F32)),
        in_specs=[HBM_SPEC] * (2 * n),
        out_specs=(SEM_SPEC, SEM_SPEC, *[HBM_SPEC] * (2 * n), pl.BlockSpec(memory_space=pltpu.VMEM)),
        input_output_aliases={k: 2 + k for k in range(2 * n)},
        compiler_params=pltpu.CompilerParams(has_side_effects=EFFECT),
    )(*[_in_hbm(a) for a in gs], *[_in_hbm(a) for a in lands])
    return outs[0], outs[1], list(outs[2:2 + n]), list(outs[2 + n:2 + 2 * n]), outs[-1]


def rs_exchange_wait(state, after, name):
    send, recv, gs, lands, _ = state
    n = len(gs)

    def body(*refs):
        _rs_wait_all(n, refs[:n], refs[n:2 * n], refs[2 * n], refs[2 * n + 1])

    outs = pl.pallas_call(
        body, name=name,
        out_shape=tuple(pltpu.HBM(a.shape, a.dtype) for a in gs + lands),
        in_specs=[HBM_SPEC] * (2 * n) + [SEM_SPEC, SEM_SPEC, ANY],
        out_specs=tuple([HBM_SPEC] * (2 * n)),
        input_output_aliases={k: k for k in range(2 * n)},
        compiler_params=pltpu.CompilerParams(has_side_effects=EFFECT),
    )(*gs, *lands, send, recv, after)
    return list(outs[:n]), list(outs[n:])


def sum_eight(g, land, s_arr, c_arr, name):
    _, a, b = g.shape
    ah = a // 2
    tb = _tile(ah, 256)
    nb = ah // tb

    def body(s_ref, c_ref, g_ref, *rest):
        lands_, o_ref = rest[:RS_SLOTS], rest[RS_SLOTS]
        tot = g_ref[0] + lands_[RS_SLOTS - 1][0]
        for slot in range(RS_SLOTS - 1):
            tot = tot + lands_[slot][0]
        o_ref[...] = tot

    def lspec(slot):
        return pl.BlockSpec((1, tb, b), lambda i, s, c: (slot, i, 0))

    return pl.pallas_call(
        body, name=name,
        out_shape=jax.ShapeDtypeStruct((a, b), F32),
        grid_spec=pltpu.PrefetchScalarGridSpec(
            num_scalar_prefetch=2, grid=(nb,),
            in_specs=[pl.BlockSpec((1, tb, b), lambda i, s, c: (s[0], c[0] * nb + i, 0))]
            + [lspec(slot) for slot in range(RS_SLOTS)],
            out_specs=pl.BlockSpec((tb, b), lambda i, s, c: (c[0] * nb + i, 0))),
        compiler_params=_params(("arbitrary",)),
    )(s_arr, c_arr, g, *([land] * RS_SLOTS))


def rs_begin(gs, tag):
    return rs_exchange_start(gs, f"rs_start_{tag}")


def rs_finish(state, after, s_arr, c_arr, tag):
    gs, lands = rs_exchange_wait(state, after, f"rs_wait_{tag}")
    ts = [sum_eight(g, r, s_arr, c_arr, f"rs_sum_{tag}_{k}") for k, (g, r) in enumerate(zip(gs, lands))]
    return sibling_join_halves(ts, f"rs_join_{tag}")


def ffn_fwd(h, gamma, wg, wu, wd, name):
    r, d = h.shape
    ns, fs, _ = wg.shape
    tm = _row_tile(r)

    def body(h_ref, g_ref, wg_ref, wu_ref, wd_ref, ho_ref, gg_ref, uu_ref, hn_sc, acc_sc):
        s = pl.program_id(1)

        @pl.when(s == 0)
        def _():
            xh, _ = _rms(h_ref[...])
            hn_sc[...] = (xh * g_ref[...]).astype(BF16)
            acc_sc[...] = jnp.zeros_like(acc_sc)

        hn = hn_sc[...]
        g = lax.dot_general(hn, wg_ref[0], NT, preferred_element_type=F32)
        u = lax.dot_general(hn, wu_ref[0], NT, preferred_element_type=F32)
        gg_ref[0] = g.astype(BF16)
        uu_ref[0] = u.astype(BF16)
        a = (g * _sigmoid(g) * u).astype(BF16)
        acc_sc[...] += jnp.dot(a, wd_ref[0], preferred_element_type=F32)

        @pl.when(s == ns - 1)
        def _():
            ho_ref[...] = h_ref[...] + 0.5 * acc_sc[...]

    return pl.pallas_call(
        body, name=name,
        out_shape=[jax.ShapeDtypeStruct((r, d), F32), jax.ShapeDtypeStruct((ns, r, fs), BF16),
                   jax.ShapeDtypeStruct((ns, r, fs), BF16)],
        grid=(r // tm, ns),
        in_specs=[pl.BlockSpec((tm, d), lambda i, s: (i, 0)), pl.BlockSpec((1, d), lambda i, s: (0, 0)),
                  pl.BlockSpec((1, fs, d), lambda i, s: (s, 0, 0)), pl.BlockSpec((1, fs, d), lambda i, s: (s, 0, 0)),
                  pl.BlockSpec((1, fs, d), lambda i, s: (s, 0, 0))],
        out_specs=[pl.BlockSpec((tm, d), lambda i, s: (i, 0)), pl.BlockSpec((1, tm, fs), lambda i, s: (s, i, 0)),
                   pl.BlockSpec((1, tm, fs), lambda i, s: (s, i, 0))],
        scratch_shapes=[pltpu.VMEM((tm, d), BF16), pltpu.VMEM((tm, d), F32)],
        compiler_params=_params(("arbitrary", "arbitrary"), VMEM_BIG),
    )(h, gamma, wg, wu, wd)


def ffn_bwd_act(h, gamma, dh, gg, uu, wg, wu, wd, name):
    r, d = h.shape
    ns, fs, _ = wg.shape
    tm = _row_tile(r)

    def body(h_ref, g_ref, dh_ref, gg_ref, uu_ref, wg_ref, wu_ref, wd_ref,
             dho_ref, dgam_ref, hn_ref, dy_ref, dg_ref, du_ref, a_ref, acc_sc):
        i, s = pl.program_id(0), pl.program_id(1)

        @pl.when(s == 0)
        def _():
            xh, _ = _rms(h_ref[...])
            hn_ref[...] = (xh * g_ref[...]).astype(BF16)
            dy_ref[...] = (0.5 * dh_ref[...]).astype(BF16)
            acc_sc[...] = jnp.zeros_like(acc_sc)

        @pl.when((i == 0) & (s == 0))
        def _():
            dgam_ref[...] = jnp.zeros_like(dgam_ref)

        g = gg_ref[0].astype(F32)
        u = uu_ref[0].astype(F32)
        da = lax.dot_general(dy_ref[...], wd_ref[0], NT, preferred_element_type=F32)
        sig = _sigmoid(g)
        sl = g * sig
        a_ref[0] = (sl * u).astype(BF16)
        du = (da * sl).astype(BF16)
        dg = (da * u * (sig * (1.0 + g * (1.0 - sig)))).astype(BF16)
        dg_ref[0] = dg
        du_ref[0] = du
        acc_sc[...] += (jnp.dot(dg, wg_ref[0], preferred_element_type=F32)
                        + jnp.dot(du, wu_ref[0], preferred_element_type=F32))

        @pl.when(s == ns - 1)
        def _():
            xh, rstd = _rms(h_ref[...])
            dhn = acc_sc[...]
            dgam_ref[...] += jnp.sum(dhn * xh, axis=0, keepdims=True)
            dho_ref[...] = dh_ref[...] + _rms_bwd(xh, rstd, dhn * g_ref[...])

    row = pl.BlockSpec((tm, d), lambda i, s: (i, 0))
    act = pl.BlockSpec((1, tm, fs), lambda i, s: (s, i, 0))
    return pl.pallas_call(
        body, name=name,
        out_shape=[jax.ShapeDtypeStruct((r, d), F32), jax.ShapeDtypeStruct((1, d), F32),
                   jax.ShapeDtypeStruct((r, d), BF16), jax.ShapeDtypeStruct((r, d), BF16),
                   jax.ShapeDtypeStruct((ns, r, fs), BF16), jax.ShapeDtypeStruct((ns, r, fs), BF16),
                   jax.ShapeDtypeStruct((ns, r, fs), BF16)],
        grid=(r // tm, ns),
        in_specs=[row, pl.BlockSpec((1, d), lambda i, s: (0, 0)), row, act, act,
                  pl.BlockSpec((1, fs, d), lambda i, s: (s, 0, 0)), pl.BlockSpec((1, fs, d), lambda i, s: (s, 0, 0)),
                  pl.BlockSpec((1, fs, d), lambda i, s: (s, 0, 0))],
        out_specs=[row, pl.BlockSpec((1, d), lambda i, s: (0, 0)), row, row, act, act, act],
        scratch_shapes=[pltpu.VMEM((tm, d), F32)],
        compiler_params=_params(("arbitrary", "arbitrary"), VMEM_BIG),
    )(h, gamma, dh, gg, uu, wg, wu, wd)


def ffn_bwd_weights(hn, dy, a, dg, du, name):
    r, d = hn.shape
    ns, _, fs = a.shape
    tm = WGRAD_ROWS if r % WGRAD_ROWS == 0 else _row_tile(r)

    def body(hn_ref, dy_ref, a_ref, dg_ref, du_ref, wg_ref, wu_ref, wd_ref):
        @pl.when(pl.program_id(1) == 0)
        def _():
            wg_ref[...] = jnp.zeros_like(wg_ref)
            wu_ref[...] = jnp.zeros_like(wu_ref)
            wd_ref[...] = jnp.zeros_like(wd_ref)

        hn_ = hn_ref[...]
        wg_ref[0] += lax.dot_general(dg_ref[0], hn_, TN, preferred_element_type=F32)
        wu_ref[0] += lax.dot_general(du_ref[0], hn_, TN, preferred_element_type=F32)
        wd_ref[0] += lax.dot_general(a_ref[0], dy_ref[...], TN, preferred_element_type=F32)

    row = pl.BlockSpec((tm, d), lambda s, i: (i, 0))
    act = pl.BlockSpec((1, tm, fs), lambda s, i: (s, i, 0))
    wsp = pl.BlockSpec((1, fs, d), lambda s, i: (s, 0, 0))
    return pl.pallas_call(
        body, name=name,
        out_shape=[jax.ShapeDtypeStruct((ns, fs, d), F32)] * 3,
        grid=(ns, r // tm),
        in_specs=[row, row, act, act, act],
        out_specs=[wsp, wsp, wsp],
        compiler_params=_params(("arbitrary", "arbitrary"), VMEM_MAX),
    )(hn, dy, a, dg, du)


def norm_fwd(x, gamma, name):
    r = x.shape[0]
    w = gamma.shape[1]
    tm = _row_tile(r)

    def body(x_ref, g_ref, o_ref):
        xh, _ = _rms(x_ref[...])
        o_ref[...] = (xh * g_ref[...]).astype(BF16)

    return pl.pallas_call(
        body, name=name, out_shape=jax.ShapeDtypeStruct((r, w), BF16), grid=(r // tm,),
        in_specs=[pl.BlockSpec((tm, w), lambda i: (i, 0)), pl.BlockSpec((1, w), lambda i: (0, 0))],
        out_specs=pl.BlockSpec((tm, w), lambda i: (i, 0)),
        compiler_params=_params(("arbitrary",)),
    )(x, gamma)


def norm_bwd(x, gamma, dy, dres, name):
    r = x.shape[0]
    w = gamma.shape[1]
    tm = _row_tile(r)
    has_res = dres is not None

    def body(*refs):
        if has_res:
            x_ref, g_ref, dy_ref, dr_ref, dx_ref, dgam_ref = refs
        else:
            x_ref, g_ref, dy_ref, dx_ref, dgam_ref = refs

        @pl.when(pl.program_id(0) == 0)
        def _():
            dgam_ref[...] = jnp.zeros_like(dgam_ref)

        xh, rstd = _rms(x_ref[...])
        dy_ = dy_ref[...].astype(F32)
        dgam_ref[...] += jnp.sum(dy_ * xh, axis=0, keepdims=True)
        dx = _rms_bwd(xh, rstd, dy_ * g_ref[...])
        if has_res:
            dx = dx + dr_ref[...]
        dx_ref[...] = dx

    row = pl.BlockSpec((tm, w), lambda i: (i, 0))
    vec = pl.BlockSpec((1, w), lambda i: (0, 0))
    ins = [x, gamma, dy] + ([dres] if has_res else [])
    return pl.pallas_call(
        body, name=name,
        out_shape=[jax.ShapeDtypeStruct((r, w), F32), jax.ShapeDtypeStruct((1, w), F32)],
        grid=(r // tm,), in_specs=[row, vec, row] + ([row] if has_res else []), out_specs=[row, vec],
        compiler_params=_params(("arbitrary",)),
    )(*ins)


def rowmm(a, w, name, *, nt=False, res=None, out_dtype=F32, heads_out=False):
    ha, r, ka = a.shape
    hw = w.shape[0]
    nh = max(ha, hw)
    n = w.shape[1] if nt else w.shape[2]
    tm = _row_tile(r)
    dims = NT if nt else (((1,), (0,)), ((), ()))
    has_res = res is not None

    def body(*refs):
        if has_res:
            a_ref, w_ref, r_ref, o_ref = refs
        else:
            a_ref, w_ref, o_ref = refs
        shared = a_ref[0].astype(BF16) if ha == 1 else None
        acc = None
        for h in range(nh):
            lhs = shared if ha == 1 else a_ref[h].astype(BF16)
            p = lax.dot_general(lhs, w_ref[h if hw > 1 else 0], dims, preferred_element_type=F32)
            if heads_out:
                o_ref[h] = p.astype(out_dtype)
            else:
                acc = p if acc is None else acc + p
        if not heads_out:
            if has_res:
                acc = acc + r_ref[...]
            o_ref[...] = acc.astype(out_dtype)

    in_specs = [pl.BlockSpec((ha, tm, ka), lambda i: (0, i, 0)), pl.BlockSpec(w.shape, lambda i: (0, 0, 0))]
    ins = [a, w]
    if has_res:
        in_specs.append(pl.BlockSpec((tm, n), lambda i: (i, 0)))
        ins.append(res)
    if heads_out:
        out_shape = jax.ShapeDtypeStruct((nh, r, n), out_dtype)
        out_spec = pl.BlockSpec((nh, tm, n), lambda i: (0, i, 0))
    else:
        out_shape = jax.ShapeDtypeStruct((r, n), out_dtype)
        out_spec = pl.BlockSpec((tm, n), lambda i: (i, 0))
    return pl.pallas_call(
        body, name=name, out_shape=out_shape, grid=(r // tm,), in_specs=in_specs, out_specs=out_spec,
        compiler_params=_params(("arbitrary",), VMEM_BIG),
    )(*ins)


def tnmm(a, b, name):
    ha, r, ka = a.shape
    hb, _, nb = b.shape
    nh = max(ha, hb)
    tm = _row_tile(r)

    def body(a_ref, b_ref, o_ref):
        @pl.when(pl.program_id(0) == 0)
        def _():
            o_ref[...] = jnp.zeros_like(o_ref)

        a_shared = a_ref[0].astype(BF16) if ha == 1 else None
        b_shared = b_ref[0].astype(BF16) if hb == 1 else None
        for h in range(nh):
            lhs = a_shared if ha == 1 else a_ref[h].astype(BF16)
            rhs = b_shared if hb == 1 else b_ref[h].astype(BF16)
            o_ref[h] += lax.dot_general(lhs, rhs, TN, preferred_element_type=F32)

    return pl.pallas_call(
        body, name=name, out_shape=jax.ShapeDtypeStruct((nh, ka, nb), F32), grid=(r // tm,),
        in_specs=[pl.BlockSpec((ha, tm, ka), lambda i: (0, i, 0)), pl.BlockSpec((hb, tm, nb), lambda i: (0, i, 0))],
        out_specs=pl.BlockSpec((nh, ka, nb), lambda i: (0, 0, 0)),
        compiler_params=_params(("arbitrary",), VMEM_BIG),
    )(a, b)


def rope_tables(r):
    inv = 1.0 / (ROPE_THETA ** (jnp.arange(0, QK_ROPE, 2, dtype=F32) / QK_ROPE))
    pos = (jnp.arange(r, dtype=F32) - META_ROW0)[:, None]
    ang = pos * inv[None, :]
    cos, sin = jnp.cos(ang), jnp.sin(ang)
    ones = jnp.ones((r, HEAD_W - QK_ROPE), F32)
    ctab = jnp.concatenate([cos, cos, ones], axis=1)
    stab = jnp.concatenate([-sin, sin, jnp.zeros_like(ones)], axis=1)
    return ctab, stab


def _swap_halves(z):
    lane = lax.broadcasted_iota(jnp.int32, z.shape, 1)
    up = pltpu.roll(z, HEAD_W - QK_ROPE // 2, 1)
    down = pltpu.roll(z, QK_ROPE // 2, 1)
    return jnp.where(lane < QK_ROPE // 2, up, jnp.where(lane < QK_ROPE, down, 0.0))


def proj_rope(a, w, ctab, stab, extra, name, scale=1.0):
    r, ka = a.shape
    nh = w.shape[0]
    tm = _row_tile(r)
    has_extra = extra is not None

    def body(*refs):
        if has_extra:
            a_ref, w_ref, c_ref, s_ref, e_ref, o_ref = refs
        else:
            a_ref, w_ref, c_ref, s_ref, o_ref = refs
        a_ = a_ref[...]
        ctab_, stab_ = c_ref[...], s_ref[...]
        if scale != 1.0:
            ctab_, stab_ = ctab_ * scale, stab_ * scale
        for h in range(nh):
            x = jnp.dot(a_, w_ref[h], preferred_element_type=F32)
            if has_extra:
                x = x + e_ref[...]
            o_ref[h] = (x * ctab_ + _swap_halves(x) * stab_).astype(BF16)

    tab = pl.BlockSpec((tm, HEAD_W), lambda i: (i, 0))
    in_specs = [pl.BlockSpec((tm, ka), lambda i: (i, 0)), pl.BlockSpec((nh, ka, HEAD_W), lambda i: (0, 0, 0)),
                tab, tab]
    ins = [a, w, ctab, stab]
    if has_extra:
        in_specs.append(pl.BlockSpec((tm, HEAD_W), lambda i: (i, 2)))
        ins.append(extra)
    return pl.pallas_call(
        body, name=name, out_shape=jax.ShapeDtypeStruct((nh, r, HEAD_W), BF16), grid=(r // tm,),
        in_specs=in_specs, out_specs=pl.BlockSpec((nh, tm, HEAD_W), lambda i: (0, i, 0)),
        compiler_params=_params(("arbitrary",)),
    )(*ins)


def rope_bwd_heads(d, ctab, stab, name, scale=1.0):
    nh, r, _ = d.shape
    tm = _row_tile(r)

    def body(d_ref, c_ref, s_ref, o_ref):
        ctab_, stab_ = c_ref[...], s_ref[...]
        if scale != 1.0:
            ctab_, stab_ = ctab_ * scale, stab_ * scale
        for h in range(nh):
            d_ = d_ref[h]
            o_ref[h] = (d_ * ctab_ + _swap_halves(d_ * stab_)).astype(BF16)

    tab = pl.BlockSpec((tm, HEAD_W), lambda i: (i, 0))
    blk = pl.BlockSpec((nh, tm, HEAD_W), lambda i: (0, i, 0))
    return pl.pallas_call(
        body, name=name, out_shape=jax.ShapeDtypeStruct((nh, r, HEAD_W), BF16), grid=(r // tm,),
        in_specs=[blk, tab, tab], out_specs=blk,
        compiler_params=_params(("arbitrary",)),
    )(d, ctab, stab)


def rope_bwd_sum(d, ctab, stab, name):
    nh, r, _ = d.shape
    tm = _row_tile(r)

    def body(d_ref, c_ref, s_ref, o_ref):
        d_ = d_ref[0]
        for h in range(1, nh):
            d_ = d_ + d_ref[h]
        lane = lax.broadcasted_iota(jnp.int32, d_.shape, 1)
        g = d_ * c_ref[...] + _swap_halves(d_ * s_ref[...])
        o_ref[...] = jnp.where(lane < QK_ROPE, g, 0.0)

    tab = pl.BlockSpec((tm, HEAD_W), lambda i: (i, 0))
    return pl.pallas_call(
        body, name=name, out_shape=jax.ShapeDtypeStruct((r, HEAD_W), F32), grid=(r // tm,),
        in_specs=[pl.BlockSpec((nh, tm, HEAD_W), lambda i: (0, i, 0)), tab, tab], out_specs=tab,
        compiler_params=_params(("arbitrary",)),
    )(d, ctab, stab)


def _attn_tiles(r):
    t = _row_tile(r)
    return t, t


def _mask(q0, k0, nq_, nk_, keys_on_rows=False):
    shape = (nk_, nq_) if keys_on_rows else (nq_, nk_)
    rq = q0 + lax.broadcasted_iota(jnp.int32, shape, 1 if keys_on_rows else 0)
    rk = k0 + lax.broadcasted_iota(jnp.int32, shape, 0 if keys_on_rows else 1)
    return ((rk >> CHUNK_SHIFT) <= (rq >> CHUNK_SHIFT)) & (rk >= META_ROW0)


ATTN_FWD_HEADS = 8
ATTN_BWD_HEADS = 2


SM_SCALE = 1.0 / math.sqrt(QK_NOPE + QK_ROPE)
LOG2E = math.log2(math.e)
Q_SCALE = SM_SCALE * LOG2E


def attn_fwd(q, k, v, name):
    nh, r, dk = q.shape
    dv = v.shape[-1]
    tq, tk = _attn_tiles(r)
    nq, nk = r // tq, r // tk

    def last_k(i):
        return ((i + 1) * tq - 1) // tk

    pairs = [(i, j) for i in range(nq) for j in range(last_k(i) + 1)]
    qi_tab = jnp.asarray([p[0] for p in pairs], jnp.int32)
    kj_tab = jnp.asarray([p[1] for p in pairs], jnp.int32)
    hb = _tile(nh, ATTN_FWD_HEADS, 1)

    def body(qi_ref, kj_ref, q_ref, k_ref, v_ref, o_ref, lse_ref, m_sc, l_sc, acc_sc):
        t = pl.program_id(1)
        i, j = qi_ref[t], kj_ref[t]

        @pl.when(j == 0)
        def _():
            m_sc[...] = jnp.full_like(m_sc, NEG)
            l_sc[...] = jnp.zeros_like(l_sc)
            acc_sc[...] = jnp.zeros_like(acc_sc)

        def step(masked):
            def one_head(hh, carry):
                s = lax.dot_general(q_ref[hh], k_ref[hh], NT, preferred_element_type=F32)
                if masked:
                    s = jnp.where(_mask(i * tq, j * tk, tq, tk), s, NEG)
                m_old = m_sc[hh]
                m_new = jnp.maximum(m_old, jnp.max(s, axis=-1, keepdims=True))
                alpha = jnp.exp2(m_old - m_new)
                p = jnp.exp2(s - jnp.tile(m_new, (1, tk // LANES)))
                l_sc[hh] = alpha * l_sc[hh] + jnp.sum(p, axis=-1, keepdims=True)
                acc_sc[hh] = (alpha[:, :dv] * acc_sc[hh]
                              + jnp.dot(p.astype(BF16), v_ref[hh], preferred_element_type=F32))
                m_sc[hh] = m_new
                return carry

            lax.fori_loop(0, hb, one_head, 0)

        needs_mask = (j == last_k(i)) | (j == 0)
        pl.when(needs_mask)(functools.partial(step, True))
        pl.when(jnp.logical_not(needs_mask))(functools.partial(step, False))

        @pl.when(j == last_k(i))
        def _():
            def one_head(hh, carry):
                l = l_sc[hh]
                o_ref[hh] = (acc_sc[hh] / l[:, :dv]).astype(BF16)
                lse_ref[hh] = (m_sc[hh] + jnp.log2(l))[:, :1]
                return carry

            lax.fori_loop(0, hb, one_head, 0)

    qspec = lambda w: pl.BlockSpec((hb, tq, w), lambda h, t, qi, kj: (h, qi[t], 0))
    kspec = lambda w: pl.BlockSpec((hb, tk, w), lambda h, t, qi, kj: (h, kj[t], 0))
    return pl.pallas_call(
        body, name=name,
        out_shape=[jax.ShapeDtypeStruct((nh, r, dv), BF16), jax.ShapeDtypeStruct((nh, r, 1), F32)],
        grid_spec=pltpu.PrefetchScalarGridSpec(
            num_scalar_prefetch=2, grid=(nh // hb, len(pairs)),
            in_specs=[qspec(dk), kspec(dk), kspec(dv)], out_specs=[qspec(dv), qspec(1)],
            scratch_shapes=[pltpu.VMEM((hb, tq, LANES), F32), pltpu.VMEM((hb, tq, LANES), F32),
                            pltpu.VMEM((hb, tq, dv), F32)]),
        compiler_params=_params(("arbitrary", "arbitrary"), VMEM_BIG),
    )(qi_tab, kj_tab, q, k, v)


def attn_bwd(q, k, v, do, lse, delta, dk_prev, dv_prev, name):
    nh, r, dk = q.shape
    dv = v.shape[-1]
    tq, tk = _attn_tiles(r)
    nq, nk = r // tq, r // tk
    has_prev = dk_prev is not None

    def first_q(j):
        return (j * tk) // tq

    pairs = [(j, i) for j in range(nk) for i in range(first_q(j), nq)]
    kj_tab = jnp.asarray([p[0] for p in pairs], jnp.int32)
    qi_tab = jnp.asarray([p[1] for p in pairs], jnp.int32)
    hb = _tile(nh, ATTN_BWD_HEADS, 1)

    def body(*refs):
        if has_prev:
            (kj_ref, qi_ref, q_ref, k_ref, v_ref, do_ref, lse_ref, dl_ref, pk_ref, pv_ref,
             dq_ref, dk_ref, dv_ref, dk_sc, dv_sc) = refs
        else:
            (kj_ref, qi_ref, q_ref, k_ref, v_ref, do_ref, lse_ref, dl_ref,
             dq_ref, dk_ref, dv_ref, dk_sc, dv_sc) = refs
        t = pl.program_id(1)
        j, i = kj_ref[t], qi_ref[t]

        @pl.when(t == 0)
        def _():
            dq_ref[...] = jnp.zeros_like(dq_ref)

        @pl.when(i == first_q(j))
        def _():
            dk_sc[...] = jnp.zeros_like(dk_sc)
            dv_sc[...] = jnp.zeros_like(dv_sc)

        def step(masked):
            qrows = pl.ds(pl.multiple_of(i * tq, tq), tq)
            for hh in range(hb):
                q_, k_, do_ = q_ref[hh], k_ref[hh], do_ref[hh]
                st = lax.dot_general(k_, q_, NT, preferred_element_type=F32)
                if masked:
                    st = jnp.where(_mask(i * tq, j * tk, tq, tk, keys_on_rows=True), st, NEG)
                pt = jnp.exp2(st - lse_ref[hh])
                dv_sc[hh] += jnp.dot(pt.astype(BF16), do_, preferred_element_type=F32)
                dpt = lax.dot_general(v_ref[hh], do_, NT, preferred_element_type=F32)
                dst = (pt * (dpt - dl_ref[hh])).astype(BF16)
                dk_sc[hh] += jnp.dot(dst, q_, preferred_element_type=F32)
                dq_ref[hh, qrows, :] += lax.dot_general(dst, k_, TN, preferred_element_type=F32)

        needs_mask = (i == first_q(j)) | (j == 0)
        pl.when(needs_mask)(functools.partial(step, True))
        pl.when(jnp.logical_not(needs_mask))(functools.partial(step, False))

        @pl.when(i == nq - 1)
        def _():
            dk_ = dk_sc[...] * (1.0 / LOG2E)
            dv_ = dv_sc[...]
            if has_prev:
                dk_ = dk_ + pk_ref[...]
                dv_ = dv_ + pv_ref[...]
            dk_ref[...] = dk_
            dv_ref[...] = dv_

    krow = lambda w: pl.BlockSpec((hb, tk, w), lambda h, t, kj, qi: (h, kj[t], 0))
    qrow = lambda w: pl.BlockSpec((hb, tq, w), lambda h, t, kj, qi: (h, qi[t], 0))
    qvec = pl.BlockSpec((hb, 1, tq), lambda h, t, kj, qi: (h, 0, qi[t]))
    in_specs = [qrow(dk), krow(dk), krow(dv), qrow(dv), qvec, qvec]
    ins = [q, k, v, do, lse, delta]
    if has_prev:
        in_specs += [krow(dk), krow(dv)]
        ins += [dk_prev, dv_prev]
    return pl.pallas_call(
        body, name=name,
        out_shape=[jax.ShapeDtypeStruct((nh, r, dk), F32), jax.ShapeDtypeStruct((nh, r, dk), F32),
                   jax.ShapeDtypeStruct((nh, r, dv), F32)],
        grid_spec=pltpu.PrefetchScalarGridSpec(
            num_scalar_prefetch=2, grid=(nh // hb, len(pairs)), in_specs=in_specs,
            out_specs=[pl.BlockSpec((hb, r, dk), lambda h, t, kj, qi: (h, 0, 0)), krow(dk), krow(dv)],
            scratch_shapes=[pltpu.VMEM((hb, tk, dk), F32), pltpu.VMEM((hb, tk, dv), F32)]),
        compiler_params=_params(("arbitrary", "arbitrary"), VMEM_BIG),
    )(kj_tab, qi_tab, *ins)


def attn_out_bwd(dattn, wo, o, name):
    r, d = dattn.shape
    nh, dv, _ = wo.shape
    tm = _row_tile(r)

    def body(da_ref, w_ref, o_ref, do_ref, dl_ref):
        da = da_ref[...].astype(BF16)
        for h in range(nh):
            do_ = lax.dot_general(da, w_ref[h], NT, preferred_element_type=F32).astype(BF16)
            do_ref[h] = do_
            dl_ref[h] = jnp.sum(do_.astype(F32) * o_ref[h].astype(F32), axis=-1, keepdims=True)

    return pl.pallas_call(
        body, name=name,
        out_shape=[jax.ShapeDtypeStruct((nh, r, dv), BF16), jax.ShapeDtypeStruct((nh, r, 1), F32)],
        grid=(r // tm,),
        in_specs=[pl.BlockSpec((tm, d), lambda i: (i, 0)), pl.BlockSpec((nh, dv, d), lambda i: (0, 0, 0)),
                  pl.BlockSpec((nh, tm, dv), lambda i: (0, i, 0))],
        out_specs=[pl.BlockSpec((nh, tm, dv), lambda i: (0, i, 0)), pl.BlockSpec((nh, tm, 1), lambda i: (0, i, 0))],
        compiler_params=_params(("arbitrary",)),
    )(dattn, wo, o)


def _pool_counts(row0, n, window):
    rows = row0 + lax.broadcasted_iota(jnp.int32, (n, 1), 0)
    cnt = jnp.clip(rows - META_ROW0 + 1, 1, window)
    return 1.0 / cnt.astype(F32)


def pool_fwd(h, gamma, wp, scale, name):
    r, d = h.shape
    ng, cg, _ = wp.shape
    tm = _row_tile(r)
    hb = tm // HALO

    def body(h_ref, hp_ref, g_ref, w_ref, sc_ref, o_ref):
        i = pl.program_id(0)
        xm = h_ref[...]
        xp = hp_ref[...] * jnp.where(i > 0, 1.0, 0.0)
        xx = jnp.concatenate([xp, xm], axis=0)
        xh, _ = _rms(xx)
        u = xh * g_ref[...]
        for g, win in enumerate(POOL_WINDOWS):
            sl = slice(g * cg, (g + 1) * cg)
            ug = u[:, sl]
            acc, k = ug, 1
            while k < win:
                acc = acc + pltpu.roll(acc, k, 0)
                k *= 2
            pooled = acc[HALO:] * _pool_counts(i * tm, tm, win) - ug[HALO:]
            y = jnp.dot(pooled.astype(BF16), w_ref[g], preferred_element_type=F32)
            o_ref[:, sl] = xm[:, sl] + y * sc_ref[:, sl]

    return pl.pallas_call(
        body, name=name, out_shape=jax.ShapeDtypeStruct((r, d), F32), grid=(r // tm,),
        in_specs=[pl.BlockSpec((tm, d), lambda i: (i, 0)),
                  pl.BlockSpec((HALO, d), lambda i: (jnp.maximum(i * hb - 1, 0), 0)),
                  pl.BlockSpec((1, d), lambda i: (0, 0)), pl.BlockSpec((ng, cg, cg), lambda i: (0, 0, 0)),
                  pl.BlockSpec((1, d), lambda i: (0, 0))],
        out_specs=pl.BlockSpec((tm, d), lambda i: (i, 0)),
        compiler_params=_params(("arbitrary",), VMEM_BIG),
    )(h, h, gamma, wp, scale)


def pool_bwd(h, gamma, wp, scale, dh, name):
    r, d = h.shape
    ng, cg, _ = wp.shape
    tm = _row_tile(r)
    hb = tm // HALO
    nt = r // tm

    def body(h_ref, hp_ref, dh_ref, dn_ref, g_ref, w_ref, sc_ref, dx_ref, dgam_ref, dw_ref, dsc_ref, du_sc):
        i = pl.program_id(0)

        @pl.when(i == 0)
        def _():
            dgam_ref[...] = jnp.zeros_like(dgam_ref)
            dw_ref[...] = jnp.zeros_like(dw_ref)
            dsc_ref[...] = jnp.zeros_like(dsc_ref)

        xm = h_ref[...]
        xp = hp_ref[...] * jnp.where(i > 0, 1.0, 0.0)
        xh_all, rstd_all = _rms(jnp.concatenate([xp, xm], axis=0))
        u = xh_all * g_ref[...]
        dm = dh_ref[...]
        dn = dn_ref[...] * jnp.where(i < nt - 1, 1.0, 0.0)
        dd = jnp.concatenate([dm, dn], axis=0)
        for g, win in enumerate(POOL_WINDOWS):
            sl = slice(g * cg, (g + 1) * cg)
            ug = u[:, sl]
            acc, k = ug, 1
            while k < win:
                acc = acc + pltpu.roll(acc, k, 0)
                k *= 2
            pooled = (acc[HALO:] * _pool_counts(i * tm, tm, win) - ug[HALO:]).astype(BF16)
            y = jnp.dot(pooled, w_ref[g], preferred_element_type=F32)
            dsc_ref[:, sl] += jnp.sum(dm[:, sl] * y, axis=0, keepdims=True)
            dyp = (dd[:, sl] * sc_ref[:, sl]).astype(BF16)
            dw_ref[g] += lax.dot_general(pooled, dyp[:tm], TN, preferred_element_type=F32)
            dpo = lax.dot_general(dyp, w_ref[g], NT, preferred_element_type=F32)
            z = dpo * _pool_counts(i * tm, tm + HALO, win)
            fwd, k = z, 1
            while k < win:
                fwd = fwd + pltpu.roll(fwd, tm + HALO - k, 0)
                k *= 2
            du_sc[:, sl] = fwd[:tm] - dpo[:tm]
        du = du_sc[...]
        xh, rstd = xh_all[HALO:], rstd_all[HALO:]
        dgam_ref[...] += jnp.sum(du * xh, axis=0, keepdims=True)
        dx_ref[...] = dm + _rms_bwd(xh, rstd, du * g_ref[...])

    row = pl.BlockSpec((tm, d), lambda i: (i, 0))
    vec = pl.BlockSpec((1, d), lambda i: (0, 0))
    prev = pl.BlockSpec((HALO, d), lambda i: (jnp.maximum(i * hb - 1, 0), 0))
    nxt = pl.BlockSpec((HALO, d), lambda i: (jnp.minimum((i + 1) * hb, r // HALO - 1), 0))
    wsp = pl.BlockSpec((ng, cg, cg), lambda i: (0, 0, 0))
    return pl.pallas_call(
        body, name=name,
        out_shape=[jax.ShapeDtypeStruct((r, d), F32), jax.ShapeDtypeStruct((1, d), F32),
                   jax.ShapeDtypeStruct((ng, cg, cg), F32), jax.ShapeDtypeStruct((1, d), F32)],
        grid=(nt,), in_specs=[row, prev, row, nxt, vec, wsp, vec], out_specs=[row, vec, wsp, vec],
        scratch_shapes=[pltpu.VMEM((tm, d), F32)],
        compiler_params=_params(("arbitrary",), VMEM_BIG),
    )(h, h, dh, dh, gamma, wp, scale)


def loss_head(h, gamma, target, seq, name):
    r, d = h.shape
    tm = _row_tile(r)

    def body(h_ref, g_ref, t_ref, sse_ref, dh_ref, dgam_ref):
        i = pl.program_id(0)

        @pl.when(i == 0)
        def _():
            sse_ref[...] = jnp.zeros_like(sse_ref)
            dgam_ref[...] = jnp.zeros_like(dgam_ref)

        xh, rstd = _rms(h_ref[...])
        rows = i * tm + lax.broadcasted_iota(jnp.int32, (tm, 1), 0)
        valid = ((rows >= FRONT) & (rows < FRONT + seq)).astype(F32)
        e = (xh * g_ref[...] - t_ref[...]) * valid
        sse_ref[...] += jnp.sum(jnp.sum(e * e, axis=1, keepdims=True), axis=0, keepdims=True)
        dy = e * (1.0 / d)
        dgam_ref[...] += jnp.sum(dy * xh, axis=0, keepdims=True)
        dh_ref[...] = _rms_bwd(xh, rstd, dy * g_ref[...])

    row = pl.BlockSpec((tm, d), lambda i: (i, 0))
    vec = pl.BlockSpec((1, d), lambda i: (0, 0))
    return pl.pallas_call(
        body, name=name,
        out_shape=[jax.ShapeDtypeStruct((1, 1), F32), jax.ShapeDtypeStruct((r, d), F32),
                   jax.ShapeDtypeStruct((1, d), F32)],
        grid=(r // tm,), in_specs=[row, vec, row],
        out_specs=[pl.BlockSpec((1, 1), lambda i: (0, 0)), row, vec],
        compiler_params=_params(("arbitrary",)),
    )(h, gamma, target)


SMALL = ("meta_tokens", "pool_w", "pool_scale", "w_dkv", "w_uk", "w_uv", "w_dq", "w_uq", "w_o")


def _pack(parts):
    flat = jnp.concatenate([p.reshape(-1) for p in parts])
    n = flat.shape[0]
    unit = PACK_W * 2 * SUBLANES
    n_pad = -(-n // unit) * unit
    return jnp.pad(flat, (0, n_pad - n)).reshape(n_pad // PACK_W, PACK_W)


def _unpack(buf, shapes, lead=()):
    flat = buf.reshape(lead + (-1,))
    out, off = [], 0
    for shp in shapes:
        n = math.prod(shp)
        out.append(flat[..., off:off + n].reshape(lead + tuple(shp)))
        off += n
    return out


def _cols_from_shards(a, axis):
    a = jnp.moveaxis(a, 0, axis)
    shp = a.shape
    return a.reshape(shp[:axis] + (shp[axis] * shp[axis + 1],) + shp[axis + 2:])


def _cols_to_shards(a, axis):
    shp = a.shape
    a = a.reshape(shp[:axis] + (N_SHARD, shp[axis] // N_SHARD) + shp[axis + 1:])
    return jnp.moveaxis(a, axis, 0)


SMALL_AXIS = {"meta_tokens": 1, "pool_w": 2, "pool_scale": 1, "w_dkv": 0, "w_uk": 1, "w_uv": 1,
              "w_dq": 1, "w_uq": 2, "w_o": 2}


def kernel(x, meta_tokens, ffn1_norm, ffn1_w_gate, ffn1_w_up, ffn1_w_down, mix_norm, ffn2_norm, ffn2_w_gate, ffn2_w_up, ffn2_w_down, pool_w, pool_scale, kv_in_norm, w_dkv, kv_latent_norm, w_uk, w_uv, w_dq, q_latent_norm, w_uq, w_o, final_norm, loss_target, m_meta_tokens, m_ffn1_norm, m_ffn1_w_gate, m_ffn1_w_up, m_ffn1_w_down, m_mix_norm, m_ffn2_norm, m_ffn2_w_gate, m_ffn2_w_up, m_ffn2_w_down, m_pool_w, m_pool_scale, m_kv_in_norm, m_w_dkv, m_kv_latent_norm, m_w_uk, m_w_uv, m_w_dq, m_q_latent_norm, m_w_uq, m_w_o, m_final_norm, v_meta_tokens, v_ffn1_norm, v_ffn1_w_gate, v_ffn1_w_up, v_ffn1_w_down, v_mix_norm, v_ffn2_norm, v_ffn2_w_gate, v_ffn2_w_up, v_ffn2_w_down, v_pool_w, v_pool_scale, v_kv_in_norm, v_w_dkv, v_kv_latent_norm, v_w_uk, v_w_uv, v_w_dq, v_q_latent_norm, v_w_uq, v_w_o, v_final_norm):
    args = dict(locals())
    W = {n: args[n] for n in NAMES}
    M = {n: args["m_" + n] for n in NAMES}
    V = {n: args["v_" + n] for n in NAMES}
    TRANSPOSED = ("ffn1_w_gate", "ffn1_w_up", "ffn2_w_gate", "ffn2_w_up")
    for n in TRANSPOSED:
        W[n], M[n], V[n] = (jnp.swapaxes(a, 1, 2) for a in (W[n], M[n], V[n]))

    depth = ffn1_norm.shape[0]
    n_a = pool_w.shape[0]
    seq, d = x.shape[1], x.shape[2]
    nh = N_HEADS
    r = -(-(FRONT + seq) // LANES) * LANES

    cx, cy, cc = lax.axis_index("x"), lax.axis_index("y"), lax.axis_index("c")
    c_arr = jnp.reshape(cc, (1,)).astype(jnp.int32)
    s_arr = jnp.reshape(2 * cx + cy, (1,)).astype(jnp.int32)

    small_shapes = [W[n].shape for n in SMALL]

    ffn_src = {f: tuple(W[f + t].astype(BF16) for t in ("_w_gate", "_w_up", "_w_down")) for f in ("ffn1", "ffn2")}
    ffn_order = [(f, l) for l in range(depth) for f in ("ffn1", "ffn2")]
    ffn_w = {}
    gate = []
    ag_state = [None]

    def gated(a):
        if gate:
            a = a + sum(gate[1:], gate[0]).astype(a.dtype)
            gate.clear()
        return a

    def vec(a):
        return gated(a.reshape(1, -1))

    def ag_start(idx, dep):
        f, l = ffn_order[idx]
        shards = [w_[l] + dep for w_ in ffn_src[f]]
        lands = [lax.dynamic_update_slice(lax.empty((N_SHARD,) + a.shape, BF16), a[None], (2 * cx + cy, 0, 0))
                 for a in shards]
        ag_state[0] = exchange_start(shards, lands, f"ag_start_{f}_{l}", True)
        gate.append(ag_state[0][4][0, 0])

    def ag_wait(idx, after):
        f, l = ffn_order[idx]
        _, lands = exchange_wait(ag_state[0], after, f"ag_wait_{f}_{l}", True)
        ffn_w[f, l] = lands
        if idx + 1 < len(ffn_order):
            ag_start(idx + 1, lands[0][0, 0, 0] * jnp.zeros((), BF16))

    gathered = all_gather_shards([_pack([W[n] for n in SMALL])], "ag_small")[0]
    ag_start(0, (gathered[0, 0, 0] * 0.0).astype(BF16))
    small_full = {}
    for n, part in zip(SMALL, _unpack(gathered, small_shapes, (N_SHARD,))):
        small_full[n] = _cols_from_shards(part, SMALL_AXIS[n])

    meta_full = small_full["meta_tokens"]
    wp = small_full["pool_w"].astype(BF16)
    pscale = small_full["pool_scale"]
    wdkv = jnp.pad(small_full["w_dkv"], ((0, 0), (0, HEAD_W - QK_ROPE))).astype(BF16)[None]
    wuk = small_full["w_uk"].reshape(KV_RANK, nh, QK_NOPE).transpose(1, 0, 2)
    wk_h = jnp.concatenate([jnp.zeros((nh, KV_RANK, HEAD_W - QK_NOPE), F32), wuk], axis=-1).astype(BF16)
    wv_h = small_full["w_uv"].reshape(KV_RANK, nh, V_HEAD).transpose(1, 0, 2).astype(BF16)
    wdq = small_full["w_dq"].astype(BF16)
    wuq = small_full["w_uq"].reshape(-1, Q_RANK, nh, QK_NOPE + QK_ROPE).transpose(0, 2, 1, 3)
    wq_h = jnp.concatenate([wuq[..., QK_NOPE:], jnp.zeros(wuq.shape[:-1] + (HEAD_W - QK_NOPE - QK_ROPE,), F32),
                            wuq[..., :QK_NOPE]], axis=-1).astype(BF16)
    wo_h = small_full["w_o"].reshape(-1, nh, V_HEAD, d).astype(BF16)

    ctab, stab = rope_tables(r)

    h = jnp.concatenate([jnp.zeros((META_ROW0, d), F32), meta_full, x[0],
                         jnp.zeros((r - FRONT - seq, d), F32)], axis=0)
    target = jnp.concatenate([jnp.zeros((FRONT, d), F32), loss_target[0],
                              jnp.zeros((r - FRONT - seq, d), F32)], axis=0)
    saved = []
    kv = None
    for l in range(depth):
        sv = {"h0": h}
        ag_wait(2 * l, h)
        h, sv["g1"], sv["u1"] = ffn_fwd(h, vec(ffn1_norm[l]), *ffn_w["ffn1", l], f"ffn1_fwd_{l}")
        sv["h1"] = h
        if l < n_a:
            h = pool_fwd(h, vec(mix_norm[l]), wp[l], vec(pscale[l]), f"pool_fwd_{l}")
        else:
            j = l - n_a
            u = norm_fwd(h, vec(mix_norm[l]), f"mixnorm_{l}")
            cq0 = rowmm(u[None], wdq[j][None], f"dq_{l}")
            cq = norm_fwd(cq0, vec(q_latent_norm[j]), f"qnorm_{l}")
            q = proj_rope(cq, wq_h[j], ctab, stab, None, f"qproj_{l}", scale=Q_SCALE)
            o, lse = attn_fwd(q, kv["k"], kv["v"], f"attn_fwd_{l}")
            h = rowmm(o, wo_h[j], f"oproj_{l}", res=h)
            sv.update(u=u, cq0=cq0, cq=cq, q=q, o=o, lse=lse)
        sv["h2"] = h
        ag_wait(2 * l + 1, h)
        h, sv["g2"], sv["u2"] = ffn_fwd(h, vec(ffn2_norm[l]), *ffn_w["ffn2", l], f"ffn2_fwd_{l}")
        saved.append(sv)
        if l == n_a - 1:
            hkv = norm_fwd(h, vec(kv_in_norm), "kvin_norm")
            ckr = rowmm(hkv[None], wdkv, "dkv")
            ckv = norm_fwd(ckr, vec(kv_latent_norm), "kvlat_norm")
            kv = {"h": h, "hkv": hkv, "ckr": ckr, "ckv": ckv,
                  "k": proj_rope(ckv, wk_h, ctab, stab, ckr, "kproj"),
                  "v": rowmm(ckv[None], wv_h, "vproj", out_dtype=BF16, heads_out=True)}

    sse, dh, dfinal = loss_head(h, vec(final_norm), target, seq, "loss_head")
    loss = lax.psum(0.5 / d * sse[0, 0], ("x", "y", "c"))

    G = {}
    FFN = ("ffn1_w_gate", "ffn1_w_up", "ffn1_w_down", "ffn2_w_gate", "ffn2_w_up", "ffn2_w_down")
    per = {n: [lax.empty(W[n].shape, F32) for _ in range(4)] for n in FFN}
    pending = []

    def rs_complete(after):
        f, l, state = pending.pop()
        names = [f + "_w_gate", f + "_w_up", f + "_w_down"]
        for n, g_ in zip(names, rs_finish(state, after, s_arr, c_arr, f"{f}_{l}")):
            per[n] = adamw_into(W[n], g_, M[n], V[n], per[n], l, f"adamw_{n}_{l}")
        gate.append(per[names[-1]][1][0, 0, 0] * 0.0)

    dnorm = {n: [None] * depth for n in ("ffn1_norm", "mix_norm", "ffn2_norm")}
    dqnorm = [None] * (depth - n_a)
    dpool_w, dpool_scale = [None] * n_a, [None] * n_a
    dwdq, dwq_h, dwo_h = [None] * (depth - n_a), [None] * (depth - n_a), [None] * (depth - n_a)

    def ffn_backward(f, l, h_in, dh_, gg, uu):
        gam = ffn1_norm[l] if f == "ffn1" else ffn2_norm[l]
        wg_, wu_, wd_ = ffn_w[f, l]
        dh_in, dgam, hn, dy, dg, du, a = ffn_bwd_act(h_in, vec(gam), dh_, gg, uu, wg_, wu_, wd_, f"{f}_bwd_act_{l}")
        dwg, dwu, dwd = ffn_bwd_weights(hn, dy, a, dg, du, f"{f}_bwd_w_{l}")
        state = rs_begin([dwg, dwu, dwd], f"{f}_{l}")
        gate.append(state[4][0, 0])
        if pending:
            rs_complete(state[4])
        pending.append((f, l, state))
        dnorm[f + "_norm"][l] = dgam[0]
        return dh_in

    dk_tot = dv_tot = None
    for l in reversed(range(depth)):
        sv = saved[l]
        if l == n_a - 1:
            dckv = rowmm(dk_tot, gated(wk_h), "kproj_bwd", nt=True)
            dkr = rope_bwd_sum(dk_tot, ctab, stab, "kproj_bwd_rope")
            dckv = rowmm(dv_tot, wv_h, "vproj_bwd", nt=True, res=dckv)
            dwk_h = tnmm(kv["ckv"][None], dk_tot, "kproj_bwd_w")
            dwv_h = tnmm(kv["ckv"][None], dv_tot, "vproj_bwd_w")
            dlat, dkvlat = norm_bwd(kv["ckr"], vec(kv_latent_norm), dckv, None, "kvlat_norm_bwd")
            dckr = jnp.concatenate([dlat, dkr], axis=1).astype(BF16)
            dhkv = rowmm(dckr[None], wdkv, "dkv_bwd", nt=True)
            dwdkv = tnmm(kv["hkv"][None], dckr[None], "dkv_bwd_w")[0]
            dh, dkvin = norm_bwd(kv["h"], vec(kv_in_norm), dhkv, dh, "kvin_norm_bwd")
            G["w_dkv"] = dwdkv[:, :KV_RANK + QK_ROPE]
            G["w_uk"] = dwk_h[..., HEAD_W - QK_NOPE:].transpose(1, 0, 2).reshape(KV_RANK, nh * QK_NOPE)
            G["w_uv"] = dwv_h.transpose(1, 0, 2).reshape(KV_RANK, nh * V_HEAD)
        dh = ffn_backward("ffn2", l, sv["h2"], dh, sv["g2"], sv["u2"])
        if l < n_a:
            dh, dmix, dpool_w[l], dps = pool_bwd(sv["h1"], vec(mix_norm[l]), wp[l], vec(pscale[l]), dh, f"pool_bwd_{l}")
            dnorm["mix_norm"][l] = dmix[0]
            dpool_scale[l] = dps[0]
        else:
            j = l - n_a
            do, delta = attn_out_bwd(dh, gated(wo_h[j]), sv["o"], f"oproj_bwd_{l}")
            dwo_h[j] = tnmm(sv["o"], dh[None], f"oproj_bwd_w_{l}")
            dq, dk_tot, dv_tot = attn_bwd(sv["q"], kv["k"], kv["v"], do, sv["lse"].reshape(nh, 1, r),
                                          delta.reshape(nh, 1, r), dk_tot, dv_tot, f"attn_bwd_{l}")
            dxq = rope_bwd_heads(dq, ctab, stab, f"qproj_bwd_rope_{l}", scale=SM_SCALE)
            dcq = rowmm(dxq, wq_h[j], f"qproj_bwd_{l}", nt=True)
            dwq_h[j] = tnmm(sv["cq"][None], dxq, f"qproj_bwd_w_{l}")
            dcq0, dqn = norm_bwd(sv["cq0"], vec(q_latent_norm[j]), dcq, None, f"qnorm_bwd_{l}")
            dqnorm[j] = dqn[0]
            dcq0b = dcq0.astype(BF16)
            du = rowmm(dcq0b[None], wdq[j][None], f"dq_bwd_{l}", nt=True)
            dwdq[j] = tnmm(sv["u"][None], dcq0b[None], f"dq_bwd_w_{l}")[0]
            dh, dmix = norm_bwd(sv["h1"], vec(mix_norm[l]), du, dh, f"mixnorm_bwd_{l}")
            dnorm["mix_norm"][l] = dmix[0]
        dh = ffn_backward("ffn1", l, sv["h0"], dh, sv["g1"], sv["u1"])

    grad_x = dh[FRONT:FRONT + seq][None]
    G["meta_tokens"] = dh[META_ROW0:FRONT]
    G["pool_w"] = jnp.stack(dpool_w)
    G["pool_scale"] = jnp.stack(dpool_scale)
    G["w_dq"] = jnp.stack(dwdq)
    dwq = jnp.stack(dwq_h)
    dwq = jnp.concatenate([dwq[..., HEAD_W - QK_NOPE:], dwq[..., :QK_ROPE]], axis=-1)
    G["w_uq"] = dwq.transpose(0, 2, 1, 3).reshape(-1, Q_RANK, nh * (QK_NOPE + QK_ROPE))
    G["w_o"] = jnp.stack(dwo_h).reshape(-1, nh * V_HEAD, d)

    REPL = ("ffn1_norm", "mix_norm", "ffn2_norm", "kv_in_norm", "kv_latent_norm", "q_latent_norm", "final_norm")
    grep = {"ffn1_norm": jnp.stack(dnorm["ffn1_norm"]), "mix_norm": jnp.stack(dnorm["mix_norm"]),
            "ffn2_norm": jnp.stack(dnorm["ffn2_norm"]), "kv_in_norm": dkvin[0], "kv_latent_norm": dkvlat[0],
            "q_latent_norm": jnp.stack(dqnorm), "final_norm": dfinal[0]}

    def pack128(parts):
        flat = jnp.concatenate([p.reshape(-1) for p in parts])
        n = flat.shape[0]
        n_pad = -(-n // (LANES * SUBLANES)) * (LANES * SUBLANES)
        return jnp.pad(flat, (0, n_pad - n)).reshape(-1, LANES)

    rep_shapes = [W[n].shape for n in REPL]
    g_rep = all_reduce_small(pack128([grep[n] for n in REPL]), "ar_repl")
    d_rep, m_rep, v_rep = adamw(pack128([W[n] for n in REPL]), g_rep, pack128([M[n] for n in REPL]),
                                pack128([V[n] for n in REPL]), "adamw_repl")
    out_g, out_d, out_m, out_v = {}, {}, {}, {}
    for dst, buf in ((out_g, g_rep), (out_d, d_rep), (out_m, m_rep), (out_v, v_rep)):
        for n, a in zip(REPL, _unpack(buf, rep_shapes)):
            dst[n] = a

    g_small = jnp.stack([_pack([_cols_to_shards(G[n], SMALL_AXIS[n])[s] for n in SMALL]) for s in range(N_SHARD)])
    g_small = reduce_scatter([gated(g_small)], c_arr, s_arr, "small")[0]
    d_s, m_s, v_s = adamw(_pack([W[n] for n in SMALL]), g_small, _pack([M[n] for n in SMALL]),
                          _pack([V[n] for n in SMALL]), "adamw_small")
    for dst, buf in ((out_g, g_small), (out_d, d_s), (out_m, m_s), (out_v, v_s)):
        for n, a in zip(SMALL, _unpack(buf, small_shapes)):
            dst[n] = a

    rs_complete(d_s)
    for n in FFN:
        out_g[n], out_d[n], out_m[n], out_v[n] = (
            jnp.swapaxes(a, 1, 2) if n in TRANSPOSED else a for a in per[n])

    return (loss, grad_x, *[out_g[n] for n in NAMES], *[out_d[n] for n in NAMES],
            *[out_m[n] for n in NAMES], *[out_v[n] for n in NAMES])


NAMES = ("meta_tokens", "ffn1_norm", "ffn1_w_gate", "ffn1_w_up", "ffn1_w_down", "mix_norm", "ffn2_norm",
         "ffn2_w_gate", "ffn2_w_up", "ffn2_w_down", "pool_w", "pool_scale", "kv_in_norm", "w_dkv",
         "kv_latent_norm", "w_uk", "w_uv", "w_dq", "q_latent_norm", "w_uq", "w_o", "final_norm")
```

```python
import functools
import math

import jax
import jax.numpy as jnp
from jax import lax
from jax.experimental import pallas as pl
from jax.experimental.pallas import tpu as pltpu

F32 = jnp.float32
BF16 = jnp.bfloat16
MESH = pl.DeviceIdType.MESH
ANY = pl.BlockSpec(memory_space=pl.ANY)

EPS = 1e-6
CHUNK = 64
CHUNK_SHIFT = 6
N_META = 16
FRONT = 64
META_ROW0 = FRONT - N_META
POOL_WINDOWS = (2, 4, 8, 16)
HALO = 16
N_HEADS = 8
QK_NOPE = 64
QK_ROPE = 32
V_HEAD = 64
HEAD_W = 128
KV_RANK = 256
Q_RANK = 384
ROPE_THETA = 10000.0
NEG = -1e30
N_SHARD = 4
LANES = 128
SUBLANES = 8
PACK_W = 512
VMEM_BIG = 52 * 1024 * 1024
VMEM_MAX = 60 * 1024 * 1024
WGRAD_ROWS = 1664

ADAM_LR = 0.001
ADAM_B1 = 0.9
ADAM_B2 = 0.999
ADAM_EPS = 1e-08
ADAM_WD = 0.01
ADAM_STEP = 10

NT = (((1,), (1,)), ((), ()))
TN = (((0,), (0,)), ((), ()))


def _params(sem=None, vmem=None):
    return pltpu.CompilerParams(dimension_semantics=sem, vmem_limit_bytes=vmem)


def _tile(n, pref, mult=SUBLANES):
    best = None
    for t in range(mult, min(n, pref) + 1, mult):
        if n % t == 0:
            best = t
    return best if best is not None else n


def _row_tile(r):
    return 640 if r % 640 == 0 else 128


def _rms(x):
    rstd = lax.rsqrt(jnp.mean(x * x, axis=-1, keepdims=True) + EPS)
    return x * rstd, rstd


def _rms_bwd(xh, rstd, dxh):
    return rstd * (dxh - xh * jnp.mean(dxh * xh, axis=-1, keepdims=True))


def _sigmoid(x):
    return 1.0 / (1.0 + jnp.exp(-x))


def _place():
    x, y, c = lax.axis_index("x"), lax.axis_index("y"), lax.axis_index("c")
    chips = [(1 - x, y), (x, 1 - y), (1 - x, 1 - y)]
    return x, y, c, chips


def all_gather_shards(shards, name):
    n = len(shards)
    slot = 2 * lax.axis_index("x") + lax.axis_index("y")
    lands = [lax.dynamic_update_slice(lax.empty((N_SHARD,) + a.shape, a.dtype), a[None], (slot, 0, 0)) for a in shards]

    def body(*refs):
        ins, outs = refs[:n], refs[2 * n:3 * n]
        send1, recv1, send2, recv2 = refs[3 * n:]
        x, y, c, chips = _place()
        s = 2 * x + y
        sib = (x, y, 1 - c)

        def rcopy(k, j, src, dst, to, first):
            return pltpu.make_async_remote_copy(
                src_ref=src, dst_ref=dst,
                send_sem=(send1 if first else send2).at[k, j],
                recv_sem=(recv1 if first else recv2).at[k, j],
                device_id=to, device_id_type=MESH)

        started = []
        for k in range(n):
            hf = ins[k].shape[0] // 2
            for j, (cx, cy) in enumerate(chips):
                r = rcopy(k, j, ins[k].at[pl.ds(c * hf, hf)], outs[k].at[s, pl.ds(c * hf, hf)],
                          (cx, cy, c), True)
                r.start()
                started.append(r)
        for k in range(n):
            hf = ins[k].shape[0] // 2
            for j, (cx, cy) in enumerate(chips):
                blk = outs[k].at[2 * cx + cy, pl.ds(c * hf, hf)]
                rcopy(k, j, blk, blk, (cx, cy, c), True).wait_recv()
                f = rcopy(k, j, blk, blk, sib, False)
                f.start()
                started.append(f)
        for k in range(n):
            hf = ins[k].shape[0] // 2
            for j, (cx, cy) in enumerate(chips):
                blk = outs[k].at[2 * cx + cy, pl.ds((1 - c) * hf, hf)]
                rcopy(k, j, blk, blk, sib, False).wait_recv()
        for r in started:
            r.wait_send()

    return pl.pallas_call(
        body, name=name,
        out_shape=[jax.ShapeDtypeStruct((N_SHARD,) + a.shape, a.dtype) for a in shards],
        in_specs=[ANY] * (2 * n), out_specs=[ANY] * n,
        input_output_aliases={n + k: k for k in range(n)},
        scratch_shapes=[pltpu.SemaphoreType.DMA((n, 3))] * 4,
    )(*shards, *lands)


def sibling_join_halves(ts, name):
    n = len(ts)

    def body(*refs):
        ins, outs = refs[:n], refs[n:2 * n]
        send, recv = refs[2 * n:]
        x, y, c, _ = _place()
        sib = (x, y, 1 - c)

        def copy(k, half):
            hf = ins[k].shape[0] // 2
            rows = pl.ds(half * hf, hf)
            return pltpu.make_async_remote_copy(
                src_ref=ins[k].at[rows], dst_ref=outs[k].at[rows],
                send_sem=send.at[k], recv_sem=recv.at[k], device_id=sib, device_id_type=MESH)

        cps = [copy(k, c) for k in range(n)]
        for r in cps:
            r.start()
        for k in range(n):
            copy(k, 1 - c).wait_recv()
        for r in cps:
            r.wait_send()

    return pl.pallas_call(
        body, name=name,
        out_shape=[jax.ShapeDtypeStruct(a.shape, a.dtype) for a in ts],
        in_specs=[ANY] * n, out_specs=[ANY] * n, input_output_aliases={k: k for k in range(n)},
        scratch_shapes=[pltpu.SemaphoreType.DMA((n,))] * 2,
    )(*ts)


def all_reduce_small(part, name):
    m, w = part.shape

    def body(x_ref, tot_ref, gat_ref, send_sems, recv_sems):
        x, y, c, chips = _place()
        me, sib = (x, y, c), (x, y, 1 - c)

        def slot(px, py, pc):
            return gat_ref.at[4 * px + 2 * py + pc]

        def copy(k, block, to, src=None):
            return pltpu.make_async_remote_copy(
                src_ref=slot(*block) if src is None else src, dst_ref=slot(*block),
                send_sem=send_sems.at[k], recv_sem=recv_sems.at[k], device_id=to, device_id_type=MESH)

        gat_ref[4 * x + 2 * y + c] = x_ref[...]
        first = [copy(0, me, sib, src=x_ref)]
        first += [copy(1 + j, me, (*chip, c), src=x_ref) for j, chip in enumerate(chips)]
        for cp in first:
            cp.start()
        passed = [copy(4 + j, (*chip, c), sib) for j, chip in enumerate(chips)]
        for j, chip in enumerate(chips):
            copy(1 + j, (*chip, c), me).wait_recv()
            passed[j].start()
        copy(0, sib, me).wait_recv()
        for j, chip in enumerate(chips):
            copy(4 + j, (*chip, 1 - c), me).wait_recv()
        for cp in first + passed:
            cp.wait_send()
        tot = gat_ref[0]
        for d in range(1, 8):
            tot = tot + gat_ref[d]
        tot_ref[...] = tot

    return pl.pallas_call(
        body, name=name,
        out_shape=jax.ShapeDtypeStruct((m, w), F32),
        in_specs=[pl.BlockSpec(memory_space=pltpu.VMEM)],
        out_specs=pl.BlockSpec(memory_space=pltpu.VMEM),
        scratch_shapes=[pltpu.VMEM((8, m, w), F32), pltpu.SemaphoreType.DMA((7,)), pltpu.SemaphoreType.DMA((7,))],
    )(part)


def adamw(w, g, m, v, name):
    a, b = w.shape
    tb = _tile(a, 256)
    c1 = 1.0 - ADAM_B1 ** ADAM_STEP
    c2 = 1.0 - ADAM_B2 ** ADAM_STEP

    def body(w_ref, g_ref, m_ref, v_ref, d_ref, mo_ref, vo_ref):
        g_ = g_ref[...]
        m_ = ADAM_B1 * m_ref[...] + (1.0 - ADAM_B1) * g_
        v_ = ADAM_B2 * v_ref[...] + (1.0 - ADAM_B2) * (g_ * g_)
        m_hat = m_ / c1
        v_hat = v_ / c2
        d_ref[...] = -ADAM_LR * (m_hat / (jnp.sqrt(v_hat) + ADAM_EPS) + ADAM_WD * w_ref[...])
        mo_ref[...] = m_
        vo_ref[...] = v_

    spec = pl.BlockSpec((tb, b), lambda i: (i, 0))
    return pl.pallas_call(
        body, name=name,
        out_shape=[jax.ShapeDtypeStruct((a, b), F32)] * 3,
        grid=(a // tb,), in_specs=[spec] * 4, out_specs=[spec] * 3,
        compiler_params=_params(("arbitrary",)),
    )(w, g, m, v)


def adamw_into(w_all, g, m_all, v_all, prev, l, name):
    nl, a, b = w_all.shape
    tb = _tile(a, 256)
    c1 = 1.0 - ADAM_B1 ** ADAM_STEP
    c2 = 1.0 - ADAM_B2 ** ADAM_STEP

    def body(w_ref, g_ref, m_ref, v_ref, p0, p1, p2, p3, go_ref, d_ref, mo_ref, vo_ref):
        g_ = g_ref[...]
        m_ = ADAM_B1 * m_ref[0] + (1.0 - ADAM_B1) * g_
        v_ = ADAM_B2 * v_ref[0] + (1.0 - ADAM_B2) * (g_ * g_)
        m_hat = m_ / c1
        v_hat = v_ / c2
        go_ref[0] = g_
        d_ref[0] = -ADAM_LR * (m_hat / (jnp.sqrt(v_hat) + ADAM_EPS) + ADAM_WD * w_ref[0])
        mo_ref[0] = m_
        vo_ref[0] = v_

    lay = pl.BlockSpec((1, tb, b), lambda i: (l, i, 0))
    return pl.pallas_call(
        body, name=name,
        out_shape=[jax.ShapeDtypeStruct((nl, a, b), F32)] * 4,
        grid=(a // tb,), in_specs=[lay, pl.BlockSpec((tb, b), lambda i: (i, 0)), lay, lay] + [ANY] * 4,
        out_specs=[lay] * 4, input_output_aliases={4: 0, 5: 1, 6: 2, 7: 3},
        compiler_params=_params(("arbitrary",)),
    )(w_all, g, m_all, v_all, *prev)


def reduce_scatter(gs, c_arr, s_arr, tag):
    lands = rs_exchange(gs, f"rs_exchange_{tag}")
    ts = [sum_eight(g, r, s_arr, c_arr, f"rs_sum_{tag}_{k}") for k, (g, r) in enumerate(zip(gs, lands))]
    return sibling_join_halves(ts, f"rs_join_{tag}")


HBM_SPEC = pl.BlockSpec(memory_space=pltpu.HBM)
SEM_SPEC = pl.BlockSpec(memory_space=pltpu.SEMAPHORE)
EFFECT = pltpu.SideEffectType.DATAFLOW_SIDE_EFFECTING


def _in_hbm(a):
    return pltpu.with_memory_space_constraint(a, pltpu.HBM)


def _exchange_copy(k, j, chip, c, s, srcs, lands, send, recv, gather, receiving):
    cx, cy = chip
    src = srcs[k] if gather else srcs[k].at[2 * cx + cy]
    if gather:
        dst = lands[k].at[2 * cx + cy] if receiving else lands[k].at[s]
    else:
        dst = lands[k].at[j]
    return pltpu.make_async_remote_copy(src_ref=src, dst_ref=dst, send_sem=send.at[3 * k + j], recv_sem=recv.at[3 * k + j],
                                        device_id=(cx, cy, c), device_id_type=MESH)


def exchange_start(srcs, lands, name, gather):
    n = len(srcs)

    def body(*refs):
        srcs_in, lands_in = refs[:n], refs[n:2 * n]
        send, recv = refs[2 * n], refs[2 * n + 1]
        token = refs[-1]
        x, y, c, chips = _place()
        for k in range(n):
            for j, chip in enumerate(chips):
                _exchange_copy(k, j, chip, c, 2 * x + y, srcs_in, lands_in, send, recv, gather, False).start()
        token[...] = jnp.zeros_like(token)

    outs = pl.pallas_call(
        body, name=name,
        out_shape=(pltpu.SemaphoreType.DMA((3 * n,)), pltpu.SemaphoreType.DMA((3 * n,)),
                   *[pltpu.HBM(a.shape, a.dtype) for a in srcs], *[pltpu.HBM(a.shape, a.dtype) for a in lands],
                   jax.ShapeDtypeStruct((SUBLANES, LANES), F32)),
        in_specs=[HBM_SPEC] * (2 * n),
        out_specs=(SEM_SPEC, SEM_SPEC, *[HBM_SPEC] * (2 * n), pl.BlockSpec(memory_space=pltpu.VMEM)),
        input_output_aliases={k: 2 + k for k in range(2 * n)},
        compiler_params=pltpu.CompilerParams(has_side_effects=EFFECT),
    )(*[_in_hbm(a) for a in srcs], *[_in_hbm(a) for a in lands])
    return outs[0], outs[1], list(outs[2:2 + n]), list(outs[2 + n:2 + 2 * n]), outs[-1]


def exchange_wait(state, after, name, gather):
    send, recv, srcs, lands, _ = state
    n = len(srcs)

    def body(*refs):
        srcs_in, lands_in = refs[:n], refs[n:2 * n]
        send_, recv_ = refs[2 * n], refs[2 * n + 1]
        x, y, c, chips = _place()
        for k in range(n):
            for j, chip in enumerate(chips):
                cp = _exchange_copy(k, j, chip, c, 2 * x + y, srcs_in, lands_in, send_, recv_, gather, True)
                cp.wait_send()
                cp.wait_recv()

    outs = pl.pallas_call(
        body, name=name,
        out_shape=tuple(pltpu.HBM(a.shape, a.dtype) for a in srcs + lands),
        in_specs=[HBM_SPEC] * (2 * n) + [SEM_SPEC, SEM_SPEC, ANY],
        out_specs=tuple([HBM_SPEC] * (2 * n)),
        input_output_aliases={k: k for k in range(2 * n)},
        compiler_params=pltpu.CompilerParams(has_side_effects=EFFECT),
    )(*srcs, *lands, send, recv, after)
    return list(outs[:n]), list(outs[n:])


RS_SLOTS = 7


def _rs_copies(k, gs, lands, send, recv, x, y, c, chips):
    hf = gs[k].shape[1] // 2
    base = RS_SLOTS * k
    out = []
    for j, (cx, cy) in enumerate(chips):
        for cd in range(2):
            out.append(pltpu.make_async_remote_copy(
                src_ref=gs[k].at[2 * cx + cy, pl.ds(cd * hf, hf)], dst_ref=lands[k].at[2 * j + c],
                send_sem=send.at[base + 2 * j + cd], recv_sem=recv.at[base + 2 * j + c],
                device_id=(cx, cy, cd), device_id_type=MESH))
    out.append(pltpu.make_async_remote_copy(
        src_ref=gs[k].at[2 * x + y, pl.ds((1 - c) * hf, hf)], dst_ref=lands[k].at[RS_SLOTS - 1],
        send_sem=send.at[base + RS_SLOTS - 1], recv_sem=recv.at[base + RS_SLOTS - 1],
        device_id=(x, y, 1 - c), device_id_type=MESH))
    return out


def _rs_wait_all(n, gs, lands, send, recv):
    x, y, c, chips = _place()
    for k in range(n):
        for cp in _rs_copies(k, gs, lands, send, recv, x, y, c, chips):
            cp.wait_send()
        hf = gs[k].shape[1] // 2
        for slot in range(RS_SLOTS):
            pltpu.make_async_remote_copy(
                src_ref=gs[k].at[0, pl.ds(0, hf)], dst_ref=lands[k].at[slot],
                send_sem=send.at[RS_SLOTS * k + slot], recv_sem=recv.at[RS_SLOTS * k + slot],
                device_id=(x, y, 1 - c), device_id_type=MESH).wait_recv()


def _rs_land_shapes(gs):
    return [(RS_SLOTS, g.shape[1] // 2, g.shape[2]) for g in gs]


def rs_exchange(gs, name):
    n = len(gs)

    def body(*refs):
        ins, outs = refs[:n], refs[n:2 * n]
        send, recv = refs[2 * n:]
        x, y, c, chips = _place()
        for k in range(n):
            for cp in _rs_copies(k, ins, outs, send, recv, x, y, c, chips):
                cp.start()
        _rs_wait_all(n, ins, outs, send, recv)

    return pl.pallas_call(
        body, name=name,
        out_shape=[jax.ShapeDtypeStruct(s, F32) for s in _rs_land_shapes(gs)],
        in_specs=[ANY] * n, out_specs=[ANY] * n,
        scratch_shapes=[pltpu.SemaphoreType.DMA((RS_SLOTS * n,))] * 2,
    )(*gs)


def rs_exchange_start(gs, name):
    n = len(gs)
    lands = [lax.empty(s, F32) for s in _rs_land_shapes(gs)]

    def body(*refs):
        ins, lands_in = refs[:n], refs[n:2 * n]
        send, recv = refs[2 * n], refs[2 * n + 1]
        token = refs[-1]
        x, y, c, chips = _place()
        for k in range(n):
            for cp in _rs_copies(k, ins, lands_in, send, recv, x, y, c, chips):
                cp.start()
        token[...] = jnp.zeros_like(token)

    outs = pl.pallas_call(
        body, name=name,
        out_shape=(pltpu.SemaphoreType.DMA((RS_SLOTS * n,)), pltpu.SemaphoreType.DMA((RS_SLOTS * n,)),
                   *[pltpu.HBM(a.shape, a.dtype) for a in gs], *[pltpu.HBM(a.shape, a.dtype) for a in lands],
                   jax.ShapeDtypeStruct((SUBLANES, LANES), F32)),
        in_specs=[HBM_SPEC] * (2 * n),
        out_specs=(SEM_SPEC, SEM_SPEC, *[HBM_SPEC] * (2 * n), pl.BlockSpec(memory_space=pltpu.VMEM)),
        input_output_aliases={k: 2 + k for k in range(2 * n)},
        compiler_params=pltpu.CompilerParams(has_side_effects=EFFECT),
    )(*[_in_hbm(a) for a in gs], *[_in_hbm(a) for a in lands])
    return outs[0], outs[1], list(outs[2:2 + n]), list(outs[2 + n:2 + 2 * n]), outs[-1]


def rs_exchange_wait(state, after, name):
    send, recv, gs, lands, _ = state
    n = len(gs)

    def body(*refs):
        _rs_wait_all(n, refs[:n], refs[n:2 * n], refs[2 * n], refs[2 * n + 1])

    outs = pl.pallas_call(
        body, name=name,
        out_shape=tuple(pltpu.HBM(a.shape, a.dtype) for a in gs + lands),
        in_specs=[HBM_SPEC] * (2 * n) + [SEM_SPEC, SEM_SPEC, ANY],
        out_specs=tuple([HBM_SPEC] * (2 * n)),
        input_output_aliases={k: k for k in range(2 * n)},
        compiler_params=pltpu.CompilerParams(has_side_effects=EFFECT),
    )(*gs, *lands, send, recv, after)
    return list(outs[:n]), list(outs[n:])


def sum_eight(g, land, s_arr, c_arr, name):
    _, a, b = g.shape
    ah = a // 2
    tb = _tile(ah, 256)
    nb = ah // tb

    def body(s_ref, c_ref, g_ref, *rest):
        lands_, o_ref = rest[:RS_SLOTS], rest[RS_SLOTS]
        tot = g_ref[0] + lands_[RS_SLOTS - 1][0]
        for slot in range(RS_SLOTS - 1):
            tot = tot + lands_[slot][0]
        o_ref[...] = tot

    def lspec(slot):
        return pl.BlockSpec((1, tb, b), lambda i, s, c: (slot, i, 0))

    return pl.pallas_call(
        body, name=name,
        out_shape=jax.ShapeDtypeStruct((a, b), F32),
        grid_spec=pltpu.PrefetchScalarGridSpec(
            num_scalar_prefetch=2, grid=(nb,),
            in_specs=[pl.BlockSpec((1, tb, b), lambda i, s, c: (s[0], c[0] * nb + i, 0))]
            + [lspec(slot) for slot in range(RS_SLOTS)],
            out_specs=pl.BlockSpec((tb, b), lambda i, s, c: (c[0] * nb + i, 0))),
        compiler_params=_params(("arbitrary",)),
    )(s_arr, c_arr, g, *([land] * RS_SLOTS))


def rs_begin(gs, tag):
    return rs_exchange_start(gs, f"rs_start_{tag}")


def rs_finish(state, after, s_arr, c_arr, tag):
    gs, lands = rs_exchange_wait(state, after, f"rs_wait_{tag}")
    ts = [sum_eight(g, r, s_arr, c_arr, f"rs_sum_{tag}_{k}") for k, (g, r) in enumerate(zip(gs, lands))]
    return sibling_join_halves(ts, f"rs_join_{tag}")


def ffn_fwd(h, gamma, wg, wu, wd, name):
    r, d = h.shape
    ns, fs, _ = wg.shape
    tm = _row_tile(r)

    def body(h_ref, g_ref, wg_ref, wu_ref, wd_ref, ho_ref, gg_ref, uu_ref, hn_sc, acc_sc):
        s = pl.program_id(1)

        @pl.when(s == 0)
        def _():
            xh, _ = _rms(h_ref[...])
            hn_sc[...] = (xh * g_ref[...]).astype(BF16)
            acc_sc[...] = jnp.zeros_like(acc_sc)

        hn = hn_sc[...]
        g = lax.dot_general(hn, wg_ref[0], NT, preferred_element_type=F32)
        u = lax.dot_general(hn, wu_ref[0], NT, preferred_element_type=F32)
        gg_ref[0] = g.astype(BF16)
        uu_ref[0] = u.astype(BF16)
        a = (g * _sigmoid(g) * u).astype(BF16)
        acc_sc[...] += jnp.dot(a, wd_ref[0], preferred_element_type=F32)

        @pl.when(s == ns - 1)
        def _():
            ho_ref[...] = h_ref[...] + 0.5 * acc_sc[...]

    return pl.pallas_call(
        body, name=name,
        out_shape=[jax.ShapeDtypeStruct((r, d), F32), jax.ShapeDtypeStruct((ns, r, fs), BF16),
                   jax.ShapeDtypeStruct((ns, r, fs), BF16)],
        grid=(r // tm, ns),
        in_specs=[pl.BlockSpec((tm, d), lambda i, s: (i, 0)), pl.BlockSpec((1, d), lambda i, s: (0, 0)),
                  pl.BlockSpec((1, fs, d), lambda i, s: (s, 0, 0)), pl.BlockSpec((1, fs, d), lambda i, s: (s, 0, 0)),
                  pl.BlockSpec((1, fs, d), lambda i, s: (s, 0, 0))],
        out_specs=[pl.BlockSpec((tm, d), lambda i, s: (i, 0)), pl.BlockSpec((1, tm, fs), lambda i, s: (s, i, 0)),
                   pl.BlockSpec((1, tm, fs), lambda i, s: (s, i, 0))],
        scratch_shapes=[pltpu.VMEM((tm, d), BF16), pltpu.VMEM((tm, d), F32)],
        compiler_params=_params(("arbitrary", "arbitrary"), VMEM_BIG),
    )(h, gamma, wg, wu, wd)


def ffn_bwd_act(h, gamma, dh, gg, uu, wg, wu, wd, name):
    r, d = h.shape
    ns, fs, _ = wg.shape
    tm = _row_tile(r)
    row = pl.BlockSpec((tm, d), lambda i, s: (i, 0))
    act = pl.BlockSpec((1, tm, fs), lambda i, s: (s, i, 0))
    wsp = pl.BlockSpec((1, fs, d), lambda i, s: (s, 0, 0))
    vsp = pl.BlockSpec((1, d), lambda i, s: (0, 0))

    def gate_body(dh_ref, gg_ref, uu_ref, wd_ref, dy_ref, dg_ref, du_ref, a_ref):
        @pl.when(pl.program_id(1) == 0)
        def _():
            dy_ref[...] = (0.5 * dh_ref[...]).astype(BF16)

        g = gg_ref[0].astype(F32)
        u = uu_ref[0].astype(F32)
        da = lax.dot_general(dy_ref[...], wd_ref[0], NT, preferred_element_type=F32)
        sig = _sigmoid(g)
        sl = g * sig
        a_ref[0] = (sl * u).astype(BF16)
        du_ref[0] = (da * sl).astype(BF16)
        dg_ref[0] = (da * u * (sig * (1.0 + g * (1.0 - sig)))).astype(BF16)

    dy, dg, du, a = pl.pallas_call(
        gate_body, name=name + "_gate",
        out_shape=[jax.ShapeDtypeStruct((r, d), BF16)] + [jax.ShapeDtypeStruct((ns, r, fs), BF16)] * 3,
        grid=(r // tm, ns), in_specs=[row, act, act, wsp], out_specs=[row, act, act, act],
        compiler_params=_params(("arbitrary", "arbitrary"), VMEM_BIG),
    )(dh, gg, uu, wd)

    def in_body(h_ref, g_ref, dh_ref, dg_ref, du_ref, wg_ref, wu_ref, dho_ref, dgam_ref, hn_ref, acc_sc):
        i, s = pl.program_id(0), pl.program_id(1)

        @pl.when(s == 0)
        def _():
            xh, _ = _rms(h_ref[...])
            hn_ref[...] = (xh * g_ref[...]).astype(BF16)
            acc_sc[...] = jnp.zeros_like(acc_sc)

        @pl.when((i == 0) & (s == 0))
        def _():
            dgam_ref[...] = jnp.zeros_like(dgam_ref)

        acc_sc[...] += (jnp.dot(dg_ref[0], wg_ref[0], preferred_element_type=F32)
                        + jnp.dot(du_ref[0], wu_ref[0], preferred_element_type=F32))

        @pl.when(s == ns - 1)
        def _():
            xh, rstd = _rms(h_ref[...])
            dhn = acc_sc[...]
            dgam_ref[...] += jnp.sum(dhn * xh, axis=0, keepdims=True)
            dho_ref[...] = dh_ref[...] + _rms_bwd(xh, rstd, dhn * g_ref[...])

    dh_in, dgam, hn = pl.pallas_call(
        in_body, name=name + "_in",
        out_shape=[jax.ShapeDtypeStruct((r, d), F32), jax.ShapeDtypeStruct((1, d), F32),
                   jax.ShapeDtypeStruct((r, d), BF16)],
        grid=(r // tm, ns), in_specs=[row, vsp, row, act, act, wsp, wsp], out_specs=[row, vsp, row],
        scratch_shapes=[pltpu.VMEM((tm, d), F32)],
        compiler_params=_params(("arbitrary", "arbitrary"), VMEM_BIG),
    )(h, gamma, dh, dg, du, wg, wu)
    return dh_in, dgam, hn, dy, dg, du, a


def ffn_bwd_weights(hn, dy, a, dg, du, name):
    r, d = hn.shape
    ns, _, fs = a.shape
    tm = WGRAD_ROWS if r % WGRAD_ROWS == 0 else _row_tile(r)

    def body(hn_ref, dy_ref, a_ref, dg_ref, du_ref, wg_ref, wu_ref, wd_ref):
        @pl.when(pl.program_id(1) == 0)
        def _():
            wg_ref[...] = jnp.zeros_like(wg_ref)
            wu_ref[...] = jnp.zeros_like(wu_ref)
            wd_ref[...] = jnp.zeros_like(wd_ref)

        hn_ = hn_ref[...]
        wg_ref[0] += lax.dot_general(dg_ref[0], hn_, TN, preferred_element_type=F32)
        wu_ref[0] += lax.dot_general(du_ref[0], hn_, TN, preferred_element_type=F32)
        wd_ref[0] += lax.dot_general(a_ref[0], dy_ref[...], TN, preferred_element_type=F32)

    row = pl.BlockSpec((tm, d), lambda s, i: (i, 0))
    act = pl.BlockSpec((1, tm, fs), lambda s, i: (s, i, 0))
    wsp = pl.BlockSpec((1, fs, d), lambda s, i: (s, 0, 0))
    return pl.pallas_call(
        body, name=name,
        out_shape=[jax.ShapeDtypeStruct((ns, fs, d), F32)] * 3,
        grid=(ns, r // tm),
        in_specs=[row, row, act, act, act],
        out_specs=[wsp, wsp, wsp],
        compiler_params=_params(("arbitrary", "arbitrary"), VMEM_MAX),
    )(hn, dy, a, dg, du)


def norm_fwd(x, gamma, name):
    r = x.shape[0]
    w = gamma.shape[1]
    tm = _row_tile(r)

    def body(x_ref, g_ref, o_ref):
        xh, _ = _rms(x_ref[...])
        o_ref[...] = (xh * g_ref[...]).astype(BF16)

    return pl.pallas_call(
        body, name=name, out_shape=jax.ShapeDtypeStruct((r, w), BF16), grid=(r // tm,),
        in_specs=[pl.BlockSpec((tm, w), lambda i: (i, 0)), pl.BlockSpec((1, w), lambda i: (0, 0))],
        out_specs=pl.BlockSpec((tm, w), lambda i: (i, 0)),
        compiler_params=_params(("arbitrary",)),
    )(x, gamma)


def norm_bwd(x, gamma, dy, dres, name):
    r = x.shape[0]
    w = gamma.shape[1]
    tm = _row_tile(r)
    has_res = dres is not None

    def body(*refs):
        if has_res:
            x_ref, g_ref, dy_ref, dr_ref, dx_ref, dgam_ref = refs
        else:
            x_ref, g_ref, dy_ref, dx_ref, dgam_ref = refs

        @pl.when(pl.program_id(0) == 0)
        def _():
            dgam_ref[...] = jnp.zeros_like(dgam_ref)

        xh, rstd = _rms(x_ref[...])
        dy_ = dy_ref[...].astype(F32)
        dgam_ref[...] += jnp.sum(dy_ * xh, axis=0, keepdims=True)
        dx = _rms_bwd(xh, rstd, dy_ * g_ref[...])
        if has_res:
            dx = dx + dr_ref[...]
        dx_ref[...] = dx

    row = pl.BlockSpec((tm, w), lambda i: (i, 0))
    vec = pl.BlockSpec((1, w), lambda i: (0, 0))
    ins = [x, gamma, dy] + ([dres] if has_res else [])
    return pl.pallas_call(
        body, name=name,
        out_shape=[jax.ShapeDtypeStruct((r, w), F32), jax.ShapeDtypeStruct((1, w), F32)],
        grid=(r // tm,), in_specs=[row, vec, row] + ([row] if has_res else []), out_specs=[row, vec],
        compiler_params=_params(("arbitrary",)),
    )(*ins)


def rowmm(a, w, name, *, nt=False, res=None, out_dtype=F32, heads_out=False):
    ha, r, ka = a.shape
    hw = w.shape[0]
    nh = max(ha, hw)
    n = w.shape[1] if nt else w.shape[2]
    tm = _row_tile(r)
    dims = NT if nt else (((1,), (0,)), ((), ()))
    has_res = res is not None

    def body(*refs):
        if has_res:
            a_ref, w_ref, r_ref, o_ref = refs
        else:
            a_ref, w_ref, o_ref = refs
        shared = a_ref[0].astype(BF16) if ha == 1 else None
        acc = None
        for h in range(nh):
            lhs = shared if ha == 1 else a_ref[h].astype(BF16)
            p = lax.dot_general(lhs, w_ref[h if hw > 1 else 0], dims, preferred_element_type=F32)
            if heads_out:
                o_ref[h] = p.astype(out_dtype)
            else:
                acc = p if acc is None else acc + p
        if not heads_out:
            if has_res:
                acc = acc + r_ref[...]
            o_ref[...] = acc.astype(out_dtype)

    in_specs = [pl.BlockSpec((ha, tm, ka), lambda i: (0, i, 0)), pl.BlockSpec(w.shape, lambda i: (0, 0, 0))]
    ins = [a, w]
    if has_res:
        in_specs.append(pl.BlockSpec((tm, n), lambda i: (i, 0)))
        ins.append(res)
    if heads_out:
        out_shape = jax.ShapeDtypeStruct((nh, r, n), out_dtype)
        out_spec = pl.BlockSpec((nh, tm, n), lambda i: (0, i, 0))
    else:
        out_shape = jax.ShapeDtypeStruct((r, n), out_dtype)
        out_spec = pl.BlockSpec((tm, n), lambda i: (i, 0))
    return pl.pallas_call(
        body, name=name, out_shape=out_shape, grid=(r // tm,), in_specs=in_specs, out_specs=out_spec,
        compiler_params=_params(("arbitrary",), VMEM_BIG),
    )(*ins)


def tnmm(a, b, name):
    ha, r, ka = a.shape
    hb, _, nb = b.shape
    nh = max(ha, hb)
    tm = _row_tile(r)

    def body(a_ref, b_ref, o_ref):
        @pl.when(pl.program_id(0) == 0)
        def _():
            o_ref[...] = jnp.zeros_like(o_ref)

        a_shared = a_ref[0].astype(BF16) if ha == 1 else None
        b_shared = b_ref[0].astype(BF16) if hb == 1 else None
        for h in range(nh):
            lhs = a_shared if ha == 1 else a_ref[h].astype(BF16)
            rhs = b_shared if hb == 1 else b_ref[h].astype(BF16)
            o_ref[h] += lax.dot_general(lhs, rhs, TN, preferred_element_type=F32)

    return pl.pallas_call(
        body, name=name, out_shape=jax.ShapeDtypeStruct((nh, ka, nb), F32), grid=(r // tm,),
        in_specs=[pl.BlockSpec((ha, tm, ka), lambda i: (0, i, 0)), pl.BlockSpec((hb, tm, nb), lambda i: (0, i, 0))],
        out_specs=pl.BlockSpec((nh, ka, nb), lambda i: (0, 0, 0)),
        compiler_params=_params(("arbitrary",), VMEM_BIG),
    )(a, b)


def rope_tables(r):
    inv = 1.0 / (ROPE_THETA ** (jnp.arange(0, QK_ROPE, 2, dtype=F32) / QK_ROPE))
    pos = (jnp.arange(r, dtype=F32) - META_ROW0)[:, None]
    ang = pos * inv[None, :]
    cos, sin = jnp.cos(ang), jnp.sin(ang)
    ones = jnp.ones((r, HEAD_W - QK_ROPE), F32)
    ctab = jnp.concatenate([cos, cos, ones], axis=1)
    stab = jnp.concatenate([-sin, sin, jnp.zeros_like(ones)], axis=1)
    return ctab, stab


def _swap_halves(z):
    lane = lax.broadcasted_iota(jnp.int32, z.shape, 1)
    up = pltpu.roll(z, HEAD_W - QK_ROPE // 2, 1)
    down = pltpu.roll(z, QK_ROPE // 2, 1)
    return jnp.where(lane < QK_ROPE // 2, up, jnp.where(lane < QK_ROPE, down, 0.0))


def proj_rope(a, w, ctab, stab, extra, name, scale=1.0):
    r, ka = a.shape
    nh = w.shape[0]
    tm = _row_tile(r)
    has_extra = extra is not None

    def body(*refs):
        if has_extra:
            a_ref, w_ref, c_ref, s_ref, e_ref, o_ref = refs
        else:
            a_ref, w_ref, c_ref, s_ref, o_ref = refs
        a_ = a_ref[...]
        ctab_, stab_ = c_ref[...], s_ref[...]
        if scale != 1.0:
            ctab_, stab_ = ctab_ * scale, stab_ * scale
        for h in range(nh):
            x = jnp.dot(a_, w_ref[h], preferred_element_type=F32)
            if has_extra:
                x = x + e_ref[...]
            o_ref[h] = (x * ctab_ + _swap_halves(x) * stab_).astype(BF16)

    tab = pl.BlockSpec((tm, HEAD_W), lambda i: (i, 0))
    in_specs = [pl.BlockSpec((tm, ka), lambda i: (i, 0)), pl.BlockSpec((nh, ka, HEAD_W), lambda i: (0, 0, 0)),
                tab, tab]
    ins = [a, w, ctab, stab]
    if has_extra:
        in_specs.append(pl.BlockSpec((tm, HEAD_W), lambda i: (i, 2)))
        ins.append(extra)
    return pl.pallas_call(
        body, name=name, out_shape=jax.ShapeDtypeStruct((nh, r, HEAD_W), BF16), grid=(r // tm,),
        in_specs=in_specs, out_specs=pl.BlockSpec((nh, tm, HEAD_W), lambda i: (0, i, 0)),
        compiler_params=_params(("arbitrary",)),
    )(*ins)


def rope_bwd_heads(d, ctab, stab, name, scale=1.0):
    nh, r, _ = d.shape
    tm = _row_tile(r)

    def body(d_ref, c_ref, s_ref, o_ref):
        ctab_, stab_ = c_ref[...], s_ref[...]
        if scale != 1.0:
            ctab_, stab_ = ctab_ * scale, stab_ * scale
        for h in range(nh):
            d_ = d_ref[h]
            o_ref[h] = (d_ * ctab_ + _swap_halves(d_ * stab_)).astype(BF16)

    tab = pl.BlockSpec((tm, HEAD_W), lambda i: (i, 0))
    blk = pl.BlockSpec((nh, tm, HEAD_W), lambda i: (0, i, 0))
    return pl.pallas_call(
        body, name=name, out_shape=jax.ShapeDtypeStruct((nh, r, HEAD_W), BF16), grid=(r // tm,),
        in_specs=[blk, tab, tab], out_specs=blk,
        compiler_params=_params(("arbitrary",)),
    )(d, ctab, stab)


def rope_bwd_sum(d, ctab, stab, name):
    nh, r, _ = d.shape
    tm = _row_tile(r)

    def body(d_ref, c_ref, s_ref, o_ref):
        d_ = d_ref[0]
        for h in range(1, nh):
            d_ = d_ + d_ref[h]
        lane = lax.broadcasted_iota(jnp.int32, d_.shape, 1)
        g = d_ * c_ref[...] + _swap_halves(d_ * s_ref[...])
        o_ref[...] = jnp.where(lane < QK_ROPE, g, 0.0)

    tab = pl.BlockSpec((tm, HEAD_W), lambda i: (i, 0))
    return pl.pallas_call(
        body, name=name, out_shape=jax.ShapeDtypeStruct((r, HEAD_W), F32), grid=(r // tm,),
        in_specs=[pl.BlockSpec((nh, tm, HEAD_W), lambda i: (0, i, 0)), tab, tab], out_specs=tab,
        compiler_params=_params(("arbitrary",)),
    )(d, ctab, stab)


def _attn_tiles(r):
    t = _row_tile(r)
    return t, t


def _mask(q0, k0, nq_, nk_, keys_on_rows=False):
    shape = (nk_, nq_) if keys_on_rows else (nq_, nk_)
    rq = q0 + lax.broadcasted_iota(jnp.int32, shape, 1 if keys_on_rows else 0)
    rk = k0 + lax.broadcasted_iota(jnp.int32, shape, 0 if keys_on_rows else 1)
    return ((rk >> CHUNK_SHIFT) <= (rq >> CHUNK_SHIFT)) & (rk >= META_ROW0)


ATTN_FWD_HEADS = 8
ATTN_BWD_HEADS = 2


SM_SCALE = 1.0 / math.sqrt(QK_NOPE + QK_ROPE)
LOG2E = math.log2(math.e)
Q_SCALE = SM_SCALE * LOG2E


def attn_fwd(q, k, v, name):
    nh, r, dk = q.shape
    dv = v.shape[-1]
    tq, tk = _attn_tiles(r)
    nq, nk = r // tq, r // tk

    def last_k(i):
        return ((i + 1) * tq - 1) // tk

    pairs = [(i, j) for i in range(nq) for j in range(last_k(i) + 1)]
    qi_tab = jnp.asarray([p[0] for p in pairs], jnp.int32)
    kj_tab = jnp.asarray([p[1] for p in pairs], jnp.int32)
    hb = _tile(nh, ATTN_FWD_HEADS, 1)

    def body(qi_ref, kj_ref, q_ref, k_ref, v_ref, o_ref, lse_ref, m_sc, l_sc, acc_sc):
        t = pl.program_id(1)
        i, j = qi_ref[t], kj_ref[t]

        @pl.when(j == 0)
        def _():
            m_sc[...] = jnp.full_like(m_sc, NEG)
            l_sc[...] = jnp.zeros_like(l_sc)
            acc_sc[...] = jnp.zeros_like(acc_sc)

        def step(masked):
            def one_head(hh, carry):
                s = lax.dot_general(q_ref[hh], k_ref[hh], NT, preferred_element_type=F32)
                if masked:
                    s = jnp.where(_mask(i * tq, j * tk, tq, tk), s, NEG)
                m_old = m_sc[hh]
                m_new = jnp.maximum(m_old, jnp.max(s, axis=-1, keepdims=True))
                alpha = jnp.exp2(m_old - m_new)
                p = jnp.exp2(s - jnp.tile(m_new, (1, tk // LANES)))
                l_sc[hh] = alpha * l_sc[hh] + jnp.sum(p, axis=-1, keepdims=True)
                acc_sc[hh] = (alpha[:, :dv] * acc_sc[hh]
                              + jnp.dot(p.astype(BF16), v_ref[hh], preferred_element_type=F32))
                m_sc[hh] = m_new
                return carry

            lax.fori_loop(0, hb, one_head, 0)

        needs_mask = (j == last_k(i)) | (j == 0)
        pl.when(needs_mask)(functools.partial(step, True))
        pl.when(jnp.logical_not(needs_mask))(functools.partial(step, False))

        @pl.when(j == last_k(i))
        def _():
            def one_head(hh, carry):
                l = l_sc[hh]
                o_ref[hh] = (acc_sc[hh] / l[:, :dv]).astype(BF16)
                lse_ref[hh] = (m_sc[hh] + jnp.log2(l))[:, :1]
                return carry

            lax.fori_loop(0, hb, one_head, 0)

    qspec = lambda w: pl.BlockSpec((hb, tq, w), lambda h, t, qi, kj: (h, qi[t], 0))
    kspec = lambda w: pl.BlockSpec((hb, tk, w), lambda h, t, qi, kj: (h, kj[t], 0))
    return pl.pallas_call(
        body, name=name,
        out_shape=[jax.ShapeDtypeStruct((nh, r, dv), BF16), jax.ShapeDtypeStruct((nh, r, 1), F32)],
        grid_spec=pltpu.PrefetchScalarGridSpec(
            num_scalar_prefetch=2, grid=(nh // hb, len(pairs)),
            in_specs=[qspec(dk), kspec(dk), kspec(dv)], out_specs=[qspec(dv), qspec(1)],
            scratch_shapes=[pltpu.VMEM((hb, tq, LANES), F32), pltpu.VMEM((hb, tq, LANES), F32),
                            pltpu.VMEM((hb, tq, dv), F32)]),
        compiler_params=_params(("arbitrary", "arbitrary"), VMEM_BIG),
    )(qi_tab, kj_tab, q, k, v)


def attn_bwd(q, k, v, do, lse, delta, dk_prev, dv_prev, name):
    nh, r, dk = q.shape
    dv = v.shape[-1]
    tq, tk = _attn_tiles(r)
    nq, nk = r // tq, r // tk
    has_prev = dk_prev is not None

    def first_q(j):
        return (j * tk) // tq

    pairs = [(j, i) for j in range(nk) for i in range(first_q(j), nq)]
    kj_tab = jnp.asarray([p[0] for p in pairs], jnp.int32)
    qi_tab = jnp.asarray([p[1] for p in pairs], jnp.int32)
    hb = _tile(nh, ATTN_BWD_HEADS, 1)

    def body(*refs):
        if has_prev:
            (kj_ref, qi_ref, q_ref, k_ref, v_ref, do_ref, lse_ref, dl_ref, pk_ref, pv_ref,
             dq_ref, dk_ref, dv_ref, dk_sc, dv_sc) = refs
        else:
            (kj_ref, qi_ref, q_ref, k_ref, v_ref, do_ref, lse_ref, dl_ref,
             dq_ref, dk_ref, dv_ref, dk_sc, dv_sc) = refs
        t = pl.program_id(1)
        j, i = kj_ref[t], qi_ref[t]

        @pl.when(t == 0)
        def _():
            dq_ref[...] = jnp.zeros_like(dq_ref)

        @pl.when(i == first_q(j))
        def _():
            dk_sc[...] = jnp.zeros_like(dk_sc)
            dv_sc[...] = jnp.zeros_like(dv_sc)

        def step(masked):
            qrows = pl.ds(pl.multiple_of(i * tq, tq), tq)
            for hh in range(hb):
                q_, k_, do_ = q_ref[hh], k_ref[hh], do_ref[hh]
                st = lax.dot_general(k_, q_, NT, preferred_element_type=F32)
                if masked:
                    st = jnp.where(_mask(i * tq, j * tk, tq, tk, keys_on_rows=True), st, NEG)
                pt = jnp.exp2(st - lse_ref[hh])
                dv_sc[hh] += jnp.dot(pt.astype(BF16), do_, preferred_element_type=F32)
                dpt = lax.dot_general(v_ref[hh], do_, NT, preferred_element_type=F32)
                dst = (pt * (dpt - dl_ref[hh])).astype(BF16)
                dk_sc[hh] += jnp.dot(dst, q_, preferred_element_type=F32)
                dq_ref[hh, qrows, :] += lax.dot_general(dst, k_, TN, preferred_element_type=F32)

        needs_mask = (i == first_q(j)) | (j == 0)
        pl.when(needs_mask)(functools.partial(step, True))
        pl.when(jnp.logical_not(needs_mask))(functools.partial(step, False))

        @pl.when(i == nq - 1)
        def _():
            dk_ = dk_sc[...] * (1.0 / LOG2E)
            dv_ = dv_sc[...]
            if has_prev:
                dk_ = dk_ + pk_ref[...]
                dv_ = dv_ + pv_ref[...]
            dk_ref[...] = dk_
            dv_ref[...] = dv_

    krow = lambda w: pl.BlockSpec((hb, tk, w), lambda h, t, kj, qi: (h, kj[t], 0))
    qrow = lambda w: pl.BlockSpec((hb, tq, w), lambda h, t, kj, qi: (h, qi[t], 0))
    qvec = pl.BlockSpec((hb, 1, tq), lambda h, t, kj, qi: (h, 0, qi[t]))
    in_specs = [qrow(dk), krow(dk), krow(dv), qrow(dv), qvec, qvec]
    ins = [q, k, v, do, lse, delta]
    if has_prev:
        in_specs += [krow(dk), krow(dv)]
        ins += [dk_prev, dv_prev]
    return pl.pallas_call(
        body, name=name,
        out_shape=[jax.ShapeDtypeStruct((nh, r, dk), F32), jax.ShapeDtypeStruct((nh, r, dk), F32),
                   jax.ShapeDtypeStruct((nh, r, dv), F32)],
        grid_spec=pltpu.PrefetchScalarGridSpec(
            num_scalar_prefetch=2, grid=(nh // hb, len(pairs)), in_specs=in_specs,
            out_specs=[pl.BlockSpec((hb, r, dk), lambda h, t, kj, qi: (h, 0, 0)), krow(dk), krow(dv)],
            scratch_shapes=[pltpu.VMEM((hb, tk, dk), F32), pltpu.VMEM((hb, tk, dv), F32)]),
        compiler_params=_params(("arbitrary", "arbitrary"), VMEM_BIG),
    )(kj_tab, qi_tab, *ins)


def attn_out_bwd(dattn, wo, o, name):
    r, d = dattn.shape
    nh, dv, _ = wo.shape
    tm = _row_tile(r)

    def body(da_ref, w_ref, o_ref, do_ref, dl_ref):
        da = da_ref[...].astype(BF16)
        for h in range(nh):
            do_ = lax.dot_general(da, w_ref[h], NT, preferred_element_type=F32).astype(BF16)
            do_ref[h] = do_
            dl_ref[h] = jnp.sum(do_.astype(F32) * o_ref[h].astype(F32), axis=-1, keepdims=True)

    return pl.pallas_call(
        body, name=name,
        out_shape=[jax.ShapeDtypeStruct((nh, r, dv), BF16), jax.ShapeDtypeStruct((nh, r, 1), F32)],
        grid=(r // tm,),
        in_specs=[pl.BlockSpec((tm, d), lambda i: (i, 0)), pl.BlockSpec((nh, dv, d), lambda i: (0, 0, 0)),
                  pl.BlockSpec((nh, tm, dv), lambda i: (0, i, 0))],
        out_specs=[pl.BlockSpec((nh, tm, dv), lambda i: (0, i, 0)), pl.BlockSpec((nh, tm, 1), lambda i: (0, i, 0))],
        compiler_params=_params(("arbitrary",)),
    )(dattn, wo, o)


def _pool_counts(row0, n, window):
    rows = row0 + lax.broadcasted_iota(jnp.int32, (n, 1), 0)
    cnt = jnp.clip(rows - META_ROW0 + 1, 1, window)
    return 1.0 / cnt.astype(F32)


def pool_fwd(h, gamma, wp, scale, name):
    r, d = h.shape
    ng, cg, _ = wp.shape
    tm = _row_tile(r)
    hb = tm // HALO

    def body(h_ref, hp_ref, g_ref, w_ref, sc_ref, o_ref):
        i = pl.program_id(0)
        xm = h_ref[...]
        xp = hp_ref[...] * jnp.where(i > 0, 1.0, 0.0)
        xx = jnp.concatenate([xp, xm], axis=0)
        xh, _ = _rms(xx)
        u = xh * g_ref[...]
        for g, win in enumerate(POOL_WINDOWS):
            sl = slice(g * cg, (g + 1) * cg)
            ug = u[:, sl]
            acc, k = ug, 1
            while k < win:
                acc = acc + pltpu.roll(acc, k, 0)
                k *= 2
            pooled = acc[HALO:] * _pool_counts(i * tm, tm, win) - ug[HALO:]
            y = jnp.dot(pooled.astype(BF16), w_ref[g], preferred_element_type=F32)
            o_ref[:, sl] = xm[:, sl] + y * sc_ref[:, sl]

    return pl.pallas_call(
        body, name=name, out_shape=jax.ShapeDtypeStruct((r, d), F32), grid=(r // tm,),
        in_specs=[pl.BlockSpec((tm, d), lambda i: (i, 0)),
                  pl.BlockSpec((HALO, d), lambda i: (jnp.maximum(i * hb - 1, 0), 0)),
                  pl.BlockSpec((1, d), lambda i: (0, 0)), pl.BlockSpec((ng, cg, cg), lambda i: (0, 0, 0)),
                  pl.BlockSpec((1, d), lambda i: (0, 0))],
        out_specs=pl.BlockSpec((tm, d), lambda i: (i, 0)),
        compiler_params=_params(("arbitrary",), VMEM_BIG),
    )(h, h, gamma, wp, scale)


def pool_bwd(h, gamma, wp, scale, dh, name):
    r, d = h.shape
    ng, cg, _ = wp.shape
    tm = _row_tile(r)
    hb = tm // HALO
    nt = r // tm

    def body(h_ref, hp_ref, dh_ref, dn_ref, g_ref, w_ref, sc_ref, dx_ref, dgam_ref, dw_ref, dsc_ref, du_sc):
        i = pl.program_id(0)

        @pl.when(i == 0)
        def _():
            dgam_ref[...] = jnp.zeros_like(dgam_ref)
            dw_ref[...] = jnp.zeros_like(dw_ref)
            dsc_ref[...] = jnp.zeros_like(dsc_ref)

        xm = h_ref[...]
        xp = hp_ref[...] * jnp.where(i > 0, 1.0, 0.0)
        xh_all, rstd_all = _rms(jnp.concatenate([xp, xm], axis=0))
        u = xh_all * g_ref[...]
        dm = dh_ref[...]
        dn = dn_ref[...] * jnp.where(i < nt - 1, 1.0, 0.0)
        dd = jnp.concatenate([dm, dn], axis=0)
        for g, win in enumerate(POOL_WINDOWS):
            sl = slice(g * cg, (g + 1) * cg)
            ug = u[:, sl]
            acc, k = ug, 1
            while k < win:
                acc = acc + pltpu.roll(acc, k, 0)
                k *= 2
            pooled = (acc[HALO:] * _pool_counts(i * tm, tm, win) - ug[HALO:]).astype(BF16)
            y = jnp.dot(pooled, w_ref[g], preferred_element_type=F32)
            dsc_ref[:, sl] += jnp.sum(dm[:, sl] * y, axis=0, keepdims=True)
            dyp = (dd[:, sl] * sc_ref[:, sl]).astype(BF16)
            dw_ref[g] += lax.dot_general(pooled, dyp[:tm], TN, preferred_element_type=F32)
            dpo = lax.dot_general(dyp, w_ref[g], NT, preferred_element_type=F32)
            z = dpo * _pool_counts(i * tm, tm + HALO, win)
            fwd, k = z, 1
            while k < win:
                fwd = fwd + pltpu.roll(fwd, tm + HALO - k, 0)
                k *= 2
            du_sc[:, sl] = fwd[:tm] - dpo[:tm]
        du = du_sc[...]
        xh, rstd = xh_all[HALO:], rstd_all[HALO:]
        dgam_ref[...] += jnp.sum(du * xh, axis=0, keepdims=True)
        dx_ref[...] = dm + _rms_bwd(xh, rstd, du * g_ref[...])

    row = pl.BlockSpec((tm, d), lambda i: (i, 0))
    vec = pl.BlockSpec((1, d), lambda i: (0, 0))
    prev = pl.BlockSpec((HALO, d), lambda i: (jnp.maximum(i * hb - 1, 0), 0))
    nxt = pl.BlockSpec((HALO, d), lambda i: (jnp.minimum((i + 1) * hb, r // HALO - 1), 0))
    wsp = pl.BlockSpec((ng, cg, cg), lambda i: (0, 0, 0))
    return pl.pallas_call(
        body, name=name,
        out_shape=[jax.ShapeDtypeStruct((r, d), F32), jax.ShapeDtypeStruct((1, d), F32),
                   jax.ShapeDtypeStruct((ng, cg, cg), F32), jax.ShapeDtypeStruct((1, d), F32)],
        grid=(nt,), in_specs=[row, prev, row, nxt, vec, wsp, vec], out_specs=[row, vec, wsp, vec],
        scratch_shapes=[pltpu.VMEM((tm, d), F32)],
        compiler_params=_params(("arbitrary",), VMEM_BIG),
    )(h, h, dh, dh, gamma, wp, scale)


def loss_head(h, gamma, target, seq, name):
    r, d = h.shape
    tm = _row_tile(r)

    def body(h_ref, g_ref, t_ref, sse_ref, dh_ref, dgam_ref):
        i = pl.program_id(0)

        @pl.when(i == 0)
        def _():
            sse_ref[...] = jnp.zeros_like(sse_ref)
            dgam_ref[...] = jnp.zeros_like(dgam_ref)

        xh, rstd = _rms(h_ref[...])
        rows = i * tm + lax.broadcasted_iota(jnp.int32, (tm, 1), 0)
        valid = ((rows >= FRONT) & (rows < FRONT + seq)).astype(F32)
        e = (xh * g_ref[...] - t_ref[...]) * valid
        sse_ref[...] += jnp.sum(jnp.sum(e * e, axis=1, keepdims=True), axis=0, keepdims=True)
        dy = e * (1.0 / d)
        dgam_ref[...] += jnp.sum(dy * xh, axis=0, keepdims=True)
        dh_ref[...] = _rms_bwd(xh, rstd, dy * g_ref[...])

    row = pl.BlockSpec((tm, d), lambda i: (i, 0))
    vec = pl.BlockSpec((1, d), lambda i: (0, 0))
    return pl.pallas_call(
        body, name=name,
        out_shape=[jax.ShapeDtypeStruct((1, 1), F32), jax.ShapeDtypeStruct((r, d), F32),
                   jax.ShapeDtypeStruct((1, d), F32)],
        grid=(r // tm,), in_specs=[row, vec, row],
        out_specs=[pl.BlockSpec((1, 1), lambda i: (0, 0)), row, vec],
        compiler_params=_params(("arbitrary",)),
    )(h, gamma, target)


SMALL = ("meta_tokens", "pool_w", "pool_scale", "w_dkv", "w_uk", "w_uv", "w_dq", "w_uq", "w_o")


def _pack(parts):
    flat = jnp.concatenate([p.reshape(-1) for p in parts])
    n = flat.shape[0]
    unit = PACK_W * 2 * SUBLANES
    n_pad = -(-n // unit) * unit
    return jnp.pad(flat, (0, n_pad - n)).reshape(n_pad // PACK_W, PACK_W)


def _unpack(buf, shapes, lead=()):
    flat = buf.reshape(lead + (-1,))
    out, off = [], 0
    for shp in shapes:
        n = math.prod(shp)
        out.append(flat[..., off:off + n].reshape(lead + tuple(shp)))
        off += n
    return out


def _cols_from_shards(a, axis):
    a = jnp.moveaxis(a, 0, axis)
    shp = a.shape
    return a.reshape(shp[:axis] + (shp[axis] * shp[axis + 1],) + shp[axis + 2:])


def _cols_to_shards(a, axis):
    shp = a.shape
    a = a.reshape(shp[:axis] + (N_SHARD, shp[axis] // N_SHARD) + shp[axis + 1:])
    return jnp.moveaxis(a, axis, 0)


SMALL_AXIS = {"meta_tokens": 1, "pool_w": 2, "pool_scale": 1, "w_dkv": 0, "w_uk": 1, "w_uv": 1,
              "w_dq": 1, "w_uq": 2, "w_o": 2}


def kernel(x, meta_tokens, ffn1_norm, ffn1_w_gate, ffn1_w_up, ffn1_w_down, mix_norm, ffn2_norm, ffn2_w_gate, ffn2_w_up, ffn2_w_down, pool_w, pool_scale, kv_in_norm, w_dkv, kv_latent_norm, w_uk, w_uv, w_dq, q_latent_norm, w_uq, w_o, final_norm, loss_target, m_meta_tokens, m_ffn1_norm, m_ffn1_w_gate, m_ffn1_w_up, m_ffn1_w_down, m_mix_norm, m_ffn2_norm, m_ffn2_w_gate, m_ffn2_w_up, m_ffn2_w_down, m_pool_w, m_pool_scale, m_kv_in_norm, m_w_dkv, m_kv_latent_norm, m_w_uk, m_w_uv, m_w_dq, m_q_latent_norm, m_w_uq, m_w_o, m_final_norm, v_meta_tokens, v_ffn1_norm, v_ffn1_w_gate, v_ffn1_w_up, v_ffn1_w_down, v_mix_norm, v_ffn2_norm, v_ffn2_w_gate, v_ffn2_w_up, v_ffn2_w_down, v_pool_w, v_pool_scale, v_kv_in_norm, v_w_dkv, v_kv_latent_norm, v_w_uk, v_w_uv, v_w_dq, v_q_latent_norm, v_w_uq, v_w_o, v_final_norm):
    args = dict(locals())
    W = {n: args[n] for n in NAMES}
    M = {n: args["m_" + n] for n in NAMES}
    V = {n: args["v_" + n] for n in NAMES}
    TRANSPOSED = ("ffn1_w_gate", "ffn1_w_up", "ffn2_w_gate", "ffn2_w_up")
    for n in TRANSPOSED:
        W[n], M[n], V[n] = (jnp.swapaxes(a, 1, 2) for a in (W[n], M[n], V[n]))

    depth = ffn1_norm.shape[0]
    n_a = pool_w.shape[0]
    seq, d = x.shape[1], x.shape[2]
    nh = N_HEADS
    r = -(-(FRONT + seq) // LANES) * LANES

    cx, cy, cc = lax.axis_index("x"), lax.axis_index("y"), lax.axis_index("c")
    c_arr = jnp.reshape(cc, (1,)).astype(jnp.int32)
    s_arr = jnp.reshape(2 * cx + cy, (1,)).astype(jnp.int32)

    small_shapes = [W[n].shape for n in SMALL]

    ffn_src = {f: tuple(W[f + t].astype(BF16) for t in ("_w_gate", "_w_up", "_w_down")) for f in ("ffn1", "ffn2")}
    ffn_order = [(f, l) for l in range(depth) for f in ("ffn1", "ffn2")]
    ffn_w = {}
    gate = []
    ag_state = [None]

    def gated(a):
        if gate:
            a = a + sum(gate[1:], gate[0]).astype(a.dtype)
            gate.clear()
        return a

    def vec(a):
        return gated(a.reshape(1, -1))

    def ag_start(idx, dep):
        f, l = ffn_order[idx]
        shards = [w_[l] + dep for w_ in ffn_src[f]]
        lands = [lax.dynamic_update_slice(lax.empty((N_SHARD,) + a.shape, BF16), a[None], (2 * cx + cy, 0, 0))
                 for a in shards]
        ag_state[0] = exchange_start(shards, lands, f"ag_start_{f}_{l}", True)
        gate.append(ag_state[0][4][0, 0])

    def ag_wait(idx, after):
        f, l = ffn_order[idx]
        _, lands = exchange_wait(ag_state[0], after, f"ag_wait_{f}_{l}", True)
        ffn_w[f, l] = lands
        if idx + 1 < len(ffn_order):
            ag_start(idx + 1, lands[0][0, 0, 0] * jnp.zeros((), BF16))

    gathered = all_gather_shards([_pack([W[n] for n in SMALL])], "ag_small")[0]
    ag_start(0, (gathered[0, 0, 0] * 0.0).astype(BF16))
    small_full = {}
    for n, part in zip(SMALL, _unpack(gathered, small_shapes, (N_SHARD,))):
        small_full[n] = _cols_from_shards(part, SMALL_AXIS[n])

    meta_full = small_full["meta_tokens"]
    wp = small_full["pool_w"].astype(BF16)
    pscale = small_full["pool_scale"]
    wdkv = jnp.pad(small_full["w_dkv"], ((0, 0), (0, HEAD_W - QK_ROPE))).astype(BF16)[None]
    wuk = small_full["w_uk"].reshape(KV_RANK, nh, QK_NOPE).transpose(1, 0, 2)
    wk_h = jnp.concatenate([jnp.zeros((nh, KV_RANK, HEAD_W - QK_NOPE), F32), wuk], axis=-1).astype(BF16)
    wv_h = small_full["w_uv"].reshape(KV_RANK, nh, V_HEAD).transpose(1, 0, 2).astype(BF16)
    wdq = small_full["w_dq"].astype(BF16)
    wuq = small_full["w_uq"].reshape(-1, Q_RANK, nh, QK_NOPE + QK_ROPE).transpose(0, 2, 1, 3)
    wq_h = jnp.concatenate([wuq[..., QK_NOPE:], jnp.zeros(wuq.shape[:-1] + (HEAD_W - QK_NOPE - QK_ROPE,), F32),
                            wuq[..., :QK_NOPE]], axis=-1).astype(BF16)
    wo_h = small_full["w_o"].reshape(-1, nh, V_HEAD, d).astype(BF16)

    ctab, stab = rope_tables(r)

    h = jnp.concatenate([jnp.zeros((META_ROW0, d), F32), meta_full, x[0],
                         jnp.zeros((r - FRONT - seq, d), F32)], axis=0)
    target = jnp.concatenate([jnp.zeros((FRONT, d), F32), loss_target[0],
                              jnp.zeros((r - FRONT - seq, d), F32)], axis=0)
    saved = []
    kv = None
    for l in range(depth):
        sv = {"h0": h}
        ag_wait(2 * l, h)
        h, sv["g1"], sv["u1"] = ffn_fwd(h, vec(ffn1_norm[l]), *ffn_w["ffn1", l], f"ffn1_fwd_{l}")
        sv["h1"] = h
        if l < n_a:
            h = pool_fwd(h, vec(mix_norm[l]), wp[l], vec(pscale[l]), f"pool_fwd_{l}")
        else:
            j = l - n_a
            u = norm_fwd(h, vec(mix_norm[l]), f"mixnorm_{l}")
            cq0 = rowmm(u[None], wdq[j][None], f"dq_{l}")
            cq = norm_fwd(cq0, vec(q_latent_norm[j]), f"qnorm_{l}")
            q = proj_rope(cq, wq_h[j], ctab, stab, None, f"qproj_{l}", scale=Q_SCALE)
            o, lse = attn_fwd(q, kv["k"], kv["v"], f"attn_fwd_{l}")
            h = rowmm(o, wo_h[j], f"oproj_{l}", res=h)
            sv.update(u=u, cq0=cq0, cq=cq, q=q, o=o, lse=lse)
        sv["h2"] = h
        ag_wait(2 * l + 1, h)
        h, sv["g2"], sv["u2"] = ffn_fwd(h, vec(ffn2_norm[l]), *ffn_w["ffn2", l], f"ffn2_fwd_{l}")
        saved.append(sv)
        if l == n_a - 1:
            hkv = norm_fwd(h, vec(kv_in_norm), "kvin_norm")
            ckr = rowmm(hkv[None], wdkv, "dkv")
            ckv = norm_fwd(ckr, vec(kv_latent_norm), "kvlat_norm")
            kv = {"h": h, "hkv": hkv, "ckr": ckr, "ckv": ckv,
                  "k": proj_rope(ckv, wk_h, ctab, stab, ckr, "kproj"),
                  "v": rowmm(ckv[None], wv_h, "vproj", out_dtype=BF16, heads_out=True)}

    sse, dh, dfinal = loss_head(h, vec(final_norm), target, seq, "loss_head")
    loss = lax.psum(0.5 / d * sse[0, 0], ("x", "y", "c"))

    G = {}
    FFN = ("ffn1_w_gate", "ffn1_w_up", "ffn1_w_down", "ffn2_w_gate", "ffn2_w_up", "ffn2_w_down")
    per = {n: [lax.empty(W[n].shape, F32) for _ in range(4)] for n in FFN}
    pending = []

    def rs_complete(after):
        f, l, state = pending.pop()
        names = [f + "_w_gate", f + "_w_up", f + "_w_down"]
        for n, g_ in zip(names, rs_finish(state, after, s_arr, c_arr, f"{f}_{l}")):
            per[n] = adamw_into(W[n], g_, M[n], V[n], per[n], l, f"adamw_{n}_{l}")
        gate.append(per[names[-1]][1][0, 0, 0] * 0.0)

    dnorm = {n: [None] * depth for n in ("ffn1_norm", "mix_norm", "ffn2_norm")}
    dqnorm = [None] * (depth - n_a)
    dpool_w, dpool_scale = [None] * n_a, [None] * n_a
    dwdq, dwq_h, dwo_h = [None] * (depth - n_a), [None] * (depth - n_a), [None] * (depth - n_a)

    def ffn_backward(f, l, h_in, dh_, gg, uu):
        gam = ffn1_norm[l] if f == "ffn1" else ffn2_norm[l]
        wg_, wu_, wd_ = ffn_w[f, l]
        dh_in, dgam, hn, dy, dg, du, a = ffn_bwd_act(h_in, vec(gam), dh_, gg, uu, wg_, wu_, wd_, f"{f}_bwd_act_{l}")
        dwg, dwu, dwd = ffn_bwd_weights(hn, dy, a, dg, du, f"{f}_bwd_w_{l}")
        state = rs_begin([dwg, dwu, dwd], f"{f}_{l}")
        gate.append(state[4][0, 0])
        if pending:
            rs_complete(state[4])
        pending.append((f, l, state))
        dnorm[f + "_norm"][l] = dgam[0]
        return dh_in

    dk_tot = dv_tot = None
    for l in reversed(range(depth)):
        sv = saved[l]
        if l == n_a - 1:
            dckv = rowmm(dk_tot, gated(wk_h), "kproj_bwd", nt=True)
            dkr = rope_bwd_sum(dk_tot, ctab, stab, "kproj_bwd_rope")
            dckv = rowmm(dv_tot, wv_h, "vproj_bwd", nt=True, res=dckv)
            dwk_h = tnmm(kv["ckv"][None], dk_tot, "kproj_bwd_w")
            dwv_h = tnmm(kv["ckv"][None], dv_tot, "vproj_bwd_w")
            dlat, dkvlat = norm_bwd(kv["ckr"], vec(kv_latent_norm), dckv, None, "kvlat_norm_bwd")
            dckr = jnp.concatenate([dlat, dkr], axis=1).astype(BF16)
            dhkv = rowmm(dckr[None], wdkv, "dkv_bwd", nt=True)
            dwdkv = tnmm(kv["hkv"][None], dckr[None], "dkv_bwd_w")[0]
            dh, dkvin = norm_bwd(kv["h"], vec(kv_in_norm), dhkv, dh, "kvin_norm_bwd")
            G["w_dkv"] = dwdkv[:, :KV_RANK + QK_ROPE]
            G["w_uk"] = dwk_h[..., HEAD_W - QK_NOPE:].transpose(1, 0, 2).reshape(KV_RANK, nh * QK_NOPE)
            G["w_uv"] = dwv_h.transpose(1, 0, 2).reshape(KV_RANK, nh * V_HEAD)
        dh = ffn_backward("ffn2", l, sv["h2"], dh, sv["g2"], sv["u2"])
        if l < n_a:
            dh, dmix, dpool_w[l], dps = pool_bwd(sv["h1"], vec(mix_norm[l]), wp[l], vec(pscale[l]), dh, f"pool_bwd_{l}")
            dnorm["mix_norm"][l] = dmix[0]
            dpool_scale[l] = dps[0]
        else:
            j = l - n_a
            do, delta = attn_out_bwd(dh, gated(wo_h[j]), sv["o"], f"oproj_bwd_{l}")
            dwo_h[j] = tnmm(sv["o"], dh[None], f"oproj_bwd_w_{l}")
            dq, dk_tot, dv_tot = attn_bwd(sv["q"], kv["k"], kv["v"], do, sv["lse"].reshape(nh, 1, r),
                                          delta.reshape(nh, 1, r), dk_tot, dv_tot, f"attn_bwd_{l}")
            dxq = rope_bwd_heads(dq, ctab, stab, f"qproj_bwd_rope_{l}", scale=SM_SCALE)
            dcq = rowmm(dxq, wq_h[j], f"qproj_bwd_{l}", nt=True)
            dwq_h[j] = tnmm(sv["cq"][None], dxq, f"qproj_bwd_w_{l}")
            dcq0, dqn = norm_bwd(sv["cq0"], vec(q_latent_norm[j]), dcq, None, f"qnorm_bwd_{l}")
            dqnorm[j] = dqn[0]
            dcq0b = dcq0.astype(BF16)
            du = rowmm(dcq0b[None], wdq[j][None], f"dq_bwd_{l}", nt=True)
            dwdq[j] = tnmm(sv["u"][None], dcq0b[None], f"dq_bwd_w_{l}")[0]
            dh, dmix = norm_bwd(sv["h1"], vec(mix_norm[l]), du, dh, f"mixnorm_bwd_{l}")
            dnorm["mix_norm"][l] = dmix[0]
        dh = ffn_backward("ffn1", l, sv["h0"], dh, sv["g1"], sv["u1"])

    grad_x = dh[FRONT:FRONT + seq][None]
    G["meta_tokens"] = dh[META_ROW0:FRONT]
    G["pool_w"] = jnp.stack(dpool_w)
    G["pool_scale"] = jnp.stack(dpool_scale)
    G["w_dq"] = jnp.stack(dwdq)
    dwq = jnp.stack(dwq_h)
    dwq = jnp.concatenate([dwq[..., HEAD_W - QK_NOPE:], dwq[..., :QK_ROPE]], axis=-1)
    G["w_uq"] = dwq.transpose(0, 2, 1, 3).reshape(-1, Q_RANK, nh * (QK_NOPE + QK_ROPE))
    G["w_o"] = jnp.stack(dwo_h).reshape(-1, nh * V_HEAD, d)

    REPL = ("ffn1_norm", "mix_norm", "ffn2_norm", "kv_in_norm", "kv_latent_norm", "q_latent_norm", "final_norm")
    grep = {"ffn1_norm": jnp.stack(dnorm["ffn1_norm"]), "mix_norm": jnp.stack(dnorm["mix_norm"]),
            "ffn2_norm": jnp.stack(dnorm["ffn2_norm"]), "kv_in_norm": dkvin[0], "kv_latent_norm": dkvlat[0],
            "q_latent_norm": jnp.stack(dqnorm), "final_norm": dfinal[0]}

    def pack128(parts):
        flat = jnp.concatenate([p.reshape(-1) for p in parts])
        n = flat.shape[0]
        n_pad = -(-n // (LANES * SUBLANES)) * (LANES * SUBLANES)
        return jnp.pad(flat, (0, n_pad - n)).reshape(-1, LANES)

    rep_shapes = [W[n].shape for n in REPL]
    g_rep = all_reduce_small(pack128([grep[n] for n in REPL]), "ar_repl")
    d_rep, m_rep, v_rep = adamw(pack128([W[n] for n in REPL]), g_rep, pack128([M[n] for n in REPL]),
                                pack128([V[n] for n in REPL]), "adamw_repl")
    out_g, out_d, out_m, out_v = {}, {}, {}, {}
    for dst, buf in ((out_g, g_rep), (out_d, d_rep), (out_m, m_rep), (out_v, v_rep)):
        for n, a in zip(REPL, _unpack(buf, rep_shapes)):
            dst[n] = a

    g_small = jnp.stack([_pack([_cols_to_shards(G[n], SMALL_AXIS[n])[s] for n in SMALL]) for s in range(N_SHARD)])
    g_small = reduce_scatter([gated(g_small)], c_arr, s_arr, "small")[0]
    d_s, m_s, v_s = adamw(_pack([W[n] for n in SMALL]), g_small, _pack([M[n] for n in SMALL]),
                          _pack([V[n] for n in SMALL]), "adamw_small")
    for dst, buf in ((out_g, g_small), (out_d, d_s), (out_m, m_s), (out_v, v_s)):
        for n, a in zip(SMALL, _unpack(buf, small_shapes)):
            dst[n] = a

    rs_complete(d_s)
    for n in FFN:
        out_g[n], out_d[n], out_m[n], out_v[n] = (
            jnp.swapaxes(a, 1, 2) if n in TRANSPOSED else a for a in per[n])

    return (loss, grad_x, *[out_g[n] for n in NAMES], *[out_d[n] for n in NAMES],
            *[out_m[n] for n in NAMES], *[out_v[n] for n in NAMES])


NAMES = ("meta_tokens", "ffn1_norm", "ffn1_w_gate", "ffn1_w_up", "ffn1_w_down", "mix_norm", "ffn2_norm",
         "ffn2_w_gate", "ffn2_w_up", "ffn2_w_down", "pool_w", "pool_scale", "kv_in_norm", "w_dkv",
         "kv_latent_norm", "w_uk", "w_uv", "w_dq", "q_latent_norm", "w_uq", "w_o", "final_norm")
```

```python
import functools
import math

import jax
import jax.numpy as jnp
from jax import lax
from jax.experimental import pallas as pl
from jax.experimental.pallas import tpu as pltpu

F32 = jnp.float32
BF16 = jnp.bfloat16
MESH = pl.DeviceIdType.MESH
ANY = pl.BlockSpec(memory_space=pl.ANY)

EPS = 1e-6
CHUNK = 64
CHUNK_SHIFT = 6
N_META = 16
FRONT = 64
META_ROW0 = FRONT - N_META
POOL_WINDOWS = (2, 4, 8, 16)
HALO = 16
N_HEADS = 8
QK_NOPE = 64
QK_ROPE = 32
V_HEAD = 64
HEAD_W = 128
KV_RANK = 256
Q_RANK = 384
ROPE_THETA = 10000.0
NEG = -1e30
N_SHARD = 4
LANES = 128
SUBLANES = 8
PACK_W = 512
VMEM_BIG = 52 * 1024 * 1024
VMEM_MAX = 60 * 1024 * 1024
WGRAD_ROWS = 1664

ADAM_LR = 0.001
ADAM_B1 = 0.9
ADAM_B2 = 0.999
ADAM_EPS = 1e-08
ADAM_WD = 0.01
ADAM_STEP = 10

NT = (((1,), (1,)), ((), ()))
TN = (((0,), (0,)), ((), ()))


def _params(sem=None, vmem=None):
    return pltpu.CompilerParams(dimension_semantics=sem, vmem_limit_bytes=vmem)


def _tile(n, pref, mult=SUBLANES):
    best = None
    for t in range(mult, min(n, pref) + 1, mult):
        if n % t == 0:
            best = t
    return best if best is not None else n


def _row_tile(r):
    return 640 if r % 640 == 0 else 128


def _rms(x):
    rstd = lax.rsqrt(jnp.mean(x * x, axis=-1, keepdims=True) + EPS)
    return x * rstd, rstd


def _rms_bwd(xh, rstd, dxh):
    return rstd * (dxh - xh * jnp.mean(dxh * xh, axis=-1, keepdims=True))


def _sigmoid(x):
    return 1.0 / (1.0 + jnp.exp(-x))


def _place():
    x, y, c = lax.axis_index("x"), lax.axis_index("y"), lax.axis_index("c")
    chips = [(1 - x, y), (x, 1 - y), (1 - x, 1 - y)]
    return x, y, c, chips


def all_gather_shards(shards, name):
    n = len(shards)
    slot = 2 * lax.axis_index("x") + lax.axis_index("y")
    lands = [lax.dynamic_update_slice(lax.empty((N_SHARD,) + a.shape, a.dtype), a[None], (slot, 0, 0)) for a in shards]

    def body(*refs):
        ins, outs = refs[:n], refs[2 * n:3 * n]
        send1, recv1, send2, recv2 = refs[3 * n:]
        x, y, c, chips = _place()
        s = 2 * x + y
        sib = (x, y, 1 - c)

        def rcopy(k, j, src, dst, to, first):
            return pltpu.make_async_remote_copy(
                src_ref=src, dst_ref=dst,
                send_sem=(send1 if first else send2).at[k, j],
                recv_sem=(recv1 if first else recv2).at[k, j],
                device_id=to, device_id_type=MESH)

        started = []
        for k in range(n):
            hf = ins[k].shape[0] // 2
            for j, (cx, cy) in enumerate(chips):
                r = rcopy(k, j, ins[k].at[pl.ds(c * hf, hf)], outs[k].at[s, pl.ds(c * hf, hf)],
                          (cx, cy, c), True)
                r.start()
                started.append(r)
        for k in range(n):
            hf = ins[k].shape[0] // 2
            for j, (cx, cy) in enumerate(chips):
                blk = outs[k].at[2 * cx + cy, pl.ds(c * hf, hf)]
                rcopy(k, j, blk, blk, (cx, cy, c), True).wait_recv()
                f = rcopy(k, j, blk, blk, sib, False)
                f.start()
                started.append(f)
        for k in range(n):
            hf = ins[k].shape[0] // 2
            for j, (cx, cy) in enumerate(chips):
                blk = outs[k].at[2 * cx + cy, pl.ds((1 - c) * hf, hf)]
                rcopy(k, j, blk, blk, sib, False).wait_recv()
        for r in started:
            r.wait_send()

    return pl.pallas_call(
        body, name=name,
        out_shape=[jax.ShapeDtypeStruct((N_SHARD,) + a.shape, a.dtype) for a in shards],
        in_specs=[ANY] * (2 * n), out_specs=[ANY] * n,
        input_output_aliases={n + k: k for k in range(n)},
        scratch_shapes=[pltpu.SemaphoreType.DMA((n, 3))] * 4,
    )(*shards, *lands)


def sibling_join_halves(ts, name):
    n = len(ts)

    def body(*refs):
        ins, outs = refs[:n], refs[n:2 * n]
        send, recv = refs[2 * n:]
        x, y, c, _ = _place()
        sib = (x, y, 1 - c)

        def copy(k, half):
            hf = ins[k].shape[0] // 2
            rows = pl.ds(half * hf, hf)
            return pltpu.make_async_remote_copy(
                src_ref=ins[k].at[rows], dst_ref=outs[k].at[rows],
                send_sem=send.at[k], recv_sem=recv.at[k], device_id=sib, device_id_type=MESH)

        cps = [copy(k, c) for k in range(n)]
        for r in cps:
            r.start()
        for k in range(n):
            copy(k, 1 - c).wait_recv()
        for r in cps:
            r.wait_send()

    return pl.pallas_call(
        body, name=name,
        out_shape=[jax.ShapeDtypeStruct(a.shape, a.dtype) for a in ts],
        in_specs=[ANY] * n, out_specs=[ANY] * n, input_output_aliases={k: k for k in range(n)},
        scratch_shapes=[pltpu.SemaphoreType.DMA((n,))] * 2,
    )(*ts)


def all_reduce_small(part, name):
    m, w = part.shape

    def body(x_ref, tot_ref, gat_ref, send_sems, recv_sems):
        x, y, c, chips = _place()
        me, sib = (x, y, c), (x, y, 1 - c)

        def slot(px, py, pc):
            return gat_ref.at[4 * px + 2 * py + pc]

        def copy(k, block, to, src=None):
            return pltpu.make_async_remote_copy(
                src_ref=slot(*block) if src is None else src, dst_ref=slot(*block),
                send_sem=send_sems.at[k], recv_sem=recv_sems.at[k], device_id=to, device_id_type=MESH)

        gat_ref[4 * x + 2 * y + c] = x_ref[...]
        first = [copy(0, me, sib, src=x_ref)]
        first += [copy(1 + j, me, (*chip, c), src=x_ref) for j, chip in enumerate(chips)]
        for cp in first:
            cp.start()
        passed = [copy(4 + j, (*chip, c), sib) for j, chip in enumerate(chips)]
        for j, chip in enumerate(chips):
            copy(1 + j, (*chip, c), me).wait_recv()
            passed[j].start()
        copy(0, sib, me).wait_recv()
        for j, chip in enumerate(chips):
            copy(4 + j, (*chip, 1 - c), me).wait_recv()
        for cp in first + passed:
            cp.wait_send()
        tot = gat_ref[0]
        for d in range(1, 8):
            tot = tot + gat_ref[d]
        tot_ref[...] = tot

    return pl.pallas_call(
        body, name=name,
        out_shape=jax.ShapeDtypeStruct((m, w), F32),
        in_specs=[pl.BlockSpec(memory_space=pltpu.VMEM)],
        out_specs=pl.BlockSpec(memory_space=pltpu.VMEM),
        scratch_shapes=[pltpu.VMEM((8, m, w), F32), pltpu.SemaphoreType.DMA((7,)), pltpu.SemaphoreType.DMA((7,))],
    )(part)


def adamw(w, g, m, v, name):
    a, b = w.shape
    tb = _tile(a, 256)
    c1 = 1.0 - ADAM_B1 ** ADAM_STEP
    c2 = 1.0 - ADAM_B2 ** ADAM_STEP

    def body(w_ref, g_ref, m_ref, v_ref, d_ref, mo_ref, vo_ref):
        g_ = g_ref[...]
        m_ = ADAM_B1 * m_ref[...] + (1.0 - ADAM_B1) * g_
        v_ = ADAM_B2 * v_ref[...] + (1.0 - ADAM_B2) * (g_ * g_)
        m_hat = m_ / c1
        v_hat = v_ / c2
        d_ref[...] = -ADAM_LR * (m_hat / (jnp.sqrt(v_hat) + ADAM_EPS) + ADAM_WD * w_ref[...])
        mo_ref[...] = m_
        vo_ref[...] = v_

    spec = pl.BlockSpec((tb, b), lambda i: (i, 0))
    return pl.pallas_call(
        body, name=name,
        out_shape=[jax.ShapeDtypeStruct((a, b), F32)] * 3,
        grid=(a // tb,), in_specs=[spec] * 4, out_specs=[spec] * 3,
        compiler_params=_params(("arbitrary",)),
    )(w, g, m, v)


def adamw_into(w_all, g, m_all, v_all, prev, l, name):
    nl, a, b = w_all.shape
    tb = _tile(a, 256)
    c1 = 1.0 - ADAM_B1 ** ADAM_STEP
    c2 = 1.0 - ADAM_B2 ** ADAM_STEP

    def body(w_ref, g_ref, m_ref, v_ref, p0, p1, p2, p3, go_ref, d_ref, mo_ref, vo_ref):
        g_ = g_ref[...]
        m_ = ADAM_B1 * m_ref[0] + (1.0 - ADAM_B1) * g_
        v_ = ADAM_B2 * v_ref[0] + (1.0 - ADAM_B2) * (g_ * g_)
        m_hat = m_ / c1
        v_hat = v_ / c2
        go_ref[0] = g_
        d_ref[0] = -ADAM_LR * (m_hat / (jnp.sqrt(v_hat) + ADAM_EPS) + ADAM_WD * w_ref[0])
        mo_ref[0] = m_
        vo_ref[0] = v_

    lay = pl.BlockSpec((1, tb, b), lambda i: (l, i, 0))
    return pl.pallas_call(
        body, name=name,
        out_shape=[jax.ShapeDtypeStruct((nl, a, b), F32)] * 4,
        grid=(a // tb,), in_specs=[lay, pl.BlockSpec((tb, b), lambda i: (i, 0)), lay, lay] + [ANY] * 4,
        out_specs=[lay] * 4, input_output_aliases={4: 0, 5: 1, 6: 2, 7: 3},
        compiler_params=_params(("arbitrary",)),
    )(w_all, g, m_all, v_all, *prev)


def reduce_scatter(gs, c_arr, s_arr, tag):
    lands = rs_exchange(gs, f"rs_exchange_{tag}")
    ts = [sum_eight(g, r, s_arr, c_arr, f"rs_sum_{tag}_{k}") for k, (g, r) in enumerate(zip(gs, lands))]
    return sibling_join_halves(ts, f"rs_join_{tag}")


HBM_SPEC = pl.BlockSpec(memory_space=pltpu.HBM)
SEM_SPEC = pl.BlockSpec(memory_space=pltpu.SEMAPHORE)
EFFECT = pltpu.SideEffectType.DATAFLOW_SIDE_EFFECTING


def _in_hbm(a):
    return pltpu.with_memory_space_constraint(a, pltpu.HBM)


def _exchange_copy(k, j, chip, c, s, srcs, lands, send, recv, gather, receiving):
    cx, cy = chip
    src = srcs[k] if gather else srcs[k].at[2 * cx + cy]
    if gather:
        dst = lands[k].at[2 * cx + cy] if receiving else lands[k].at[s]
    else:
        dst = lands[k].at[j]
    return pltpu.make_async_remote_copy(src_ref=src, dst_ref=dst, send_sem=send.at[3 * k + j], recv_sem=recv.at[3 * k + j],
                                        device_id=(cx, cy, c), device_id_type=MESH)


def exchange_start(srcs, lands, name, gather):
    n = len(srcs)

    def body(*refs):
        srcs_in, lands_in = refs[:n], refs[n:2 * n]
        send, recv = refs[2 * n], refs[2 * n + 1]
        token = refs[-1]
        x, y, c, chips = _place()
        for k in range(n):
            for j, chip in enumerate(chips):
                _exchange_copy(k, j, chip, c, 2 * x + y, srcs_in, lands_in, send, recv, gather, False).start()
        token[...] = jnp.zeros_like(token)

    outs = pl.pallas_call(
        body, name=name,
        out_shape=(pltpu.SemaphoreType.DMA((3 * n,)), pltpu.SemaphoreType.DMA((3 * n,)),
                   *[pltpu.HBM(a.shape, a.dtype) for a in srcs], *[pltpu.HBM(a.shape, a.dtype) for a in lands],
                   jax.ShapeDtypeStruct((SUBLANES, LANES), F32)),
        in_specs=[HBM_SPEC] * (2 * n),
        out_specs=(SEM_SPEC, SEM_SPEC, *[HBM_SPEC] * (2 * n), pl.BlockSpec(memory_space=pltpu.VMEM)),
        input_output_aliases={k: 2 + k for k in range(2 * n)},
        compiler_params=pltpu.CompilerParams(has_side_effects=EFFECT),
    )(*[_in_hbm(a) for a in srcs], *[_in_hbm(a) for a in lands])
    return outs[0], outs[1], list(outs[2:2 + n]), list(outs[2 + n:2 + 2 * n]), outs[-1]


def exchange_wait(state, after, name, gather):
    send, recv, srcs, lands, _ = state
    n = len(srcs)

    def body(*refs):
        srcs_in, lands_in = refs[:n], refs[n:2 * n]
        send_, recv_ = refs[2 * n], refs[2 * n + 1]
        x, y, c, chips = _place()
        for k in range(n):
            for j, chip in enumerate(chips):
                cp = _exchange_copy(k, j, chip, c, 2 * x + y, srcs_in, lands_in, send_, recv_, gather, True)
                cp.wait_send()
                cp.wait_recv()

    outs = pl.pallas_call(
        body, name=name,
        out_shape=tuple(pltpu.HBM(a.shape, a.dtype) for a in srcs + lands),
        in_specs=[HBM_SPEC] * (2 * n) + [SEM_SPEC, SEM_SPEC, ANY],
        out_specs=tuple([HBM_SPEC] * (2 * n)),
        input_output_aliases={k: k for k in range(2 * n)},
        compiler_params=pltpu.CompilerParams(has_side_effects=EFFECT),
    )(*srcs, *lands, send, recv, after)
    return list(outs[:n]), list(outs[n:])


RS_SLOTS = 7


def _rs_copies(k, gs, lands, send, recv, x, y, c, chips):
    hf = gs[k].shape[1] // 2
    base = RS_SLOTS * k
    out = []
    for j, (cx, cy) in enumerate(chips):
        for cd in range(2):
            out.append(pltpu.make_async_remote_copy(
                src_ref=gs[k].at[2 * cx + cy, pl.ds(cd * hf, hf)], dst_ref=lands[k].at[2 * j + c],
                send_sem=send.at[base + 2 * j + cd], recv_sem=recv.at[base + 2 * j + c],
                device_id=(cx, cy, cd), device_id_type=MESH))
    out.append(pltpu.make_async_remote_copy(
        src_ref=gs[k].at[2 * x + y, pl.ds((1 - c) * hf, hf)], dst_ref=lands[k].at[RS_SLOTS - 1],
        send_sem=send.at[base + RS_SLOTS - 1], recv_sem=recv.at[base + RS_SLOTS - 1],
        device_id=(x, y, 1 - c), device_id_type=MESH))
    return out


def _rs_wait_all(n, gs, lands, send, recv):
    x, y, c, chips = _place()
    for k in range(n):
        for cp in _rs_copies(k, gs, lands, send, recv, x, y, c, chips):
            cp.wait_send()
        hf = gs[k].shape[1] // 2
        for slot in range(RS_SLOTS):
            pltpu.make_async_remote_copy(
                src_ref=gs[k].at[0, pl.ds(0, hf)], dst_ref=lands[k].at[slot],
                send_sem=send.at[RS_SLOTS * k + slot], recv_sem=recv.at[RS_SLOTS * k + slot],
                device_id=(x, y, 1 - c), device_id_type=MESH).wait_recv()


def _rs_land_shapes(gs):
    return [(RS_SLOTS, g.shape[1] // 2, g.shape[2]) for g in gs]


def rs_exchange(gs, name):
    n = len(gs)

    def body(*refs):
        ins, outs = refs[:n], refs[n:2 * n]
        send, recv = refs[2 * n:]
        x, y, c, chips = _place()
        for k in range(n):
            for cp in _rs_copies(k, ins, outs, send, recv, x, y, c, chips):
                cp.start()
        _rs_wait_all(n, ins, outs, send, recv)

    return pl.pallas_call(
        body, name=name,
        out_shape=[jax.ShapeDtypeStruct(s, F32) for s in _rs_land_shapes(gs)],
        in_specs=[ANY] * n, out_specs=[ANY] * n,
        scratch_shapes=[pltpu.SemaphoreType.DMA((RS_SLOTS * n,))] * 2,
    )(*gs)


def rs_exchange_start(gs, name):
    n = len(gs)
    lands = [lax.empty(s, F32) for s in _rs_land_shapes(gs)]

    def body(*refs):
        ins, lands_in = refs[:n], refs[n:2 * n]
        send, recv = refs[2 * n], refs[2 * n + 1]
        token = refs[-1]
        x, y, c, chips = _place()
        for k in range(n):
            for cp in _rs_copies(k, ins, lands_in, send, recv, x, y, c, chips):
                cp.start()
        token[...] = jnp.zeros_like(token)

    outs = pl.pallas_call(
        body, name=name,
        out_shape=(pltpu.SemaphoreType.DMA((RS_SLOTS * n,)), pltpu.SemaphoreType.DMA((RS_SLOTS * n,)),
                   *[pltpu.HBM(a.shape, a.dtype) for a in gs], *[pltpu.HBM(a.shape, a.dtype) for a in lands],
                   jax.ShapeDtypeStruct((SUBLANES, LANES), F32)),
        in_specs=[HBM_SPEC] * (2 * n),
        out_specs=(SEM_SPEC, SEM_SPEC, *[HBM_SPEC] * (2 * n), pl.BlockSpec(memory_space=pltpu.VMEM)),
        input_output_aliases={k: 2 + k for k in range(2 * n)},
        compiler_params=pltpu.CompilerParams(has_side_effects=EFFECT),
    )(*[_in_hbm(a) for a in gs], *[_in_hbm(a) for a in lands])
    return outs[0], outs[1], list(outs[2:2 + n]), list(outs[2 + n:2 + 2 * n]), outs[-1]


def rs_exchange_wait(state, after, name):
    send, recv, gs, lands, _ = state
    n = len(gs)

    def body(*refs):
        _rs_wait_all(n, refs[:n], refs[n:2 * n], refs[2 * n], refs[2 * n + 1])

    outs = pl.pallas_call(
        body, name=name,
        out_shape=tuple(pltpu.HBM(a.shape, a.dtype) for a in gs + lands),
        in_specs=[HBM_SPEC] * (2 * n) + [SEM_SPEC, SEM_SPEC, ANY],
        out_specs=tuple([HBM_SPEC] * (2 * n)),
        input_output_aliases={k: k for k in range(2 * n)},
        compiler_params=pltpu.CompilerParams(has_side_effects=EFFECT),
    )(*gs, *lands, send, recv, after)
    return list(outs[:n]), list(outs[n:])


def sum_eight(g, land, s_arr, c_arr, name):
    _, a, b = g.shape
    ah = a // 2
    tb = _tile(ah, 256)
    nb = ah // tb

    def body(s_ref, c_ref, g_ref, *rest):
        lands_, o_ref = rest[:RS_SLOTS], rest[RS_SLOTS]
        tot = g_ref[0] + lands_[RS_SLOTS - 1][0]
        for slot in range(RS_SLOTS - 1):
            tot = tot + lands_[slot][0]
        o_ref[...] = tot

    def lspec(slot):
        return pl.BlockSpec((1, tb, b), lambda i, s, c: (slot, i, 0))

    return pl.pallas_call(
        body, name=name,
        out_shape=jax.ShapeDtypeStruct((a, b), F32),
        grid_spec=pltpu.PrefetchScalarGridSpec(
            num_scalar_prefetch=2, grid=(nb,),
            in_specs=[pl.BlockSpec((1, tb, b), lambda i, s, c: (s[0], c[0] * nb + i, 0))]
            + [lspec(slot) for slot in range(RS_SLOTS)],
            out_specs=pl.BlockSpec((tb, b), lambda i, s, c: (c[0] * nb + i, 0))),
        compiler_params=_params(("arbitrary",)),
    )(s_arr, c_arr, g, *([land] * RS_SLOTS))


def rs_begin(gs, tag):
    return rs_exchange_start(gs, f"rs_start_{tag}")


def rs_finish(state, after, s_arr, c_arr, tag):
    gs, lands = rs_exchange_wait(state, after, f"rs_wait_{tag}")
    ts = [sum_eight(g, r, s_arr, c_arr, f"rs_sum_{tag}_{k}") for k, (g, r) in enumerate(zip(gs, lands))]
    return sibling_join_halves(ts, f"rs_join_{tag}")


def ffn_fwd(h, gamma, wg, wu, wd, name):
    r, d = h.shape
    ns, fs, _ = wg.shape
    tm = _row_tile(r)

    def body(h_ref, g_ref, wg_ref, wu_ref, wd_ref, ho_ref, gg_ref, uu_ref, hn_sc, acc_sc):
        s = pl.program_id(1)

        @pl.when(s == 0)
        def _():
            xh, _ = _rms(h_ref[...])
            hn_sc[...] = (xh * g_ref[...]).astype(BF16)
            acc_sc[...] = jnp.zeros_like(acc_sc)

        hn = hn_sc[...]
        g = lax.dot_general(hn, wg_ref[0], NT, preferred_element_type=F32)
        u = lax.dot_general(hn, wu_ref[0], NT, preferred_element_type=F32)
        gg_ref[0] = g.astype(BF16)
        uu_ref[0] = u.astype(BF16)
        a = (g * _sigmoid(g) * u).astype(BF16)
        acc_sc[...] += jnp.dot(a, wd_ref[0], preferred_element_type=F32)

        @pl.when(s == ns - 1)
        def _():
            ho_ref[...] = h_ref[...] + 0.5 * acc_sc[...]

    return pl.pallas_call(
        body, name=name,
        out_shape=[jax.ShapeDtypeStruct((r, d), F32), jax.ShapeDtypeStruct((ns, r, fs), BF16),
                   jax.ShapeDtypeStruct((ns, r, fs), BF16)],
        grid=(r // tm, ns),
        in_specs=[pl.BlockSpec((tm, d), lambda i, s: (i, 0)), pl.BlockSpec((1, d), lambda i, s: (0, 0)),
                  pl.BlockSpec((1, fs, d), lambda i, s: (s, 0, 0)), pl.BlockSpec((1, fs, d), lambda i, s: (s, 0, 0)),
                  pl.BlockSpec((1, fs, d), lambda i, s: (s, 0, 0))],
        out_specs=[pl.BlockSpec((tm, d), lambda i, s: (i, 0)), pl.BlockSpec((1, tm, fs), lambda i, s: (s, i, 0)),
                   pl.BlockSpec((1, tm, fs), lambda i, s: (s, i, 0))],
        scratch_shapes=[pltpu.VMEM((tm, d), BF16), pltpu.VMEM((tm, d), F32)],
        compiler_params=_params(("arbitrary", "arbitrary"), VMEM_BIG),
    )(h, gamma, wg, wu, wd)


def ffn_bwd_act(h, gamma, dh, gg, uu, wg, wu, wd, name):
    r, d = h.shape
    ns, fs, _ = wg.shape
    tm = _row_tile(r)

    def body(h_ref, g_ref, dh_ref, gg_ref, uu_ref, wg_ref, wu_ref, wd_ref,
             dho_ref, dgam_ref, hn_ref, dy_ref, dg_ref, du_ref, a_ref, acc_sc):
        i, s = pl.program_id(0), pl.program_id(1)

        @pl.when(s == 0)
        def _():
            xh, _ = _rms(h_ref[...])
            hn_ref[...] = (xh * g_ref[...]).astype(BF16)
            dy_ref[...] = (0.5 * dh_ref[...]).astype(BF16)
            acc_sc[...] = jnp.zeros_like(acc_sc)

        @pl.when((i == 0) & (s == 0))
        def _():
            dgam_ref[...] = jnp.zeros_like(dgam_ref)

        g = gg_ref[0].astype(F32)
        u = uu_ref[0].astype(F32)
        da = lax.dot_general(dy_ref[...], wd_ref[0], NT, preferred_element_type=F32)
        sig = _sigmoid(g)
        sl = g * sig
        a_ref[0] = (sl * u).astype(BF16)
        du = (da * sl).astype(BF16)
        dg = (da * u * (sig * (1.0 + g * (1.0 - sig)))).astype(BF16)
        dg_ref[0] = dg
        du_ref[0] = du
        acc_sc[...] += (jnp.dot(dg, wg_ref[0], preferred_element_type=F32)
                        + jnp.dot(du, wu_ref[0], preferred_element_type=F32))

        @pl.when(s == ns - 1)
        def _():
            xh, rstd = _rms(h_ref[...])
            dhn = acc_sc[...]
            dgam_ref[...] += jnp.sum(dhn * xh, axis=0, keepdims=True)
            dho_ref[...] = dh_ref[...] + _rms_bwd(xh, rstd, dhn * g_ref[...])

    row = pl.BlockSpec((tm, d), lambda i, s: (i, 0))
    act = pl.BlockSpec((1, tm, fs), lambda i, s: (s, i, 0))
    return pl.pallas_call(
        body, name=name,
        out_shape=[jax.ShapeDtypeStruct((r, d), F32), jax.ShapeDtypeStruct((1, d), F32),
                   jax.ShapeDtypeStruct((r, d), BF16), jax.ShapeDtypeStruct((r, d), BF16),
                   jax.ShapeDtypeStruct((ns, r, fs), BF16), jax.ShapeDtypeStruct((ns, r, fs), BF16),
                   jax.ShapeDtypeStruct((ns, r, fs), BF16)],
        grid=(r // tm, ns),
        in_specs=[row, pl.BlockSpec((1, d), lambda i, s: (0, 0)), row, act, act,
                  pl.BlockSpec((1, fs, d), lambda i, s: (s, 0, 0)), pl.BlockSpec((1, fs, d), lambda i, s: (s, 0, 0)),
                  pl.BlockSpec((1, fs, d), lambda i, s: (s, 0, 0))],
        out_specs=[row, pl.BlockSpec((1, d), lambda i, s: (0, 0)), row, row, act, act, act],
        scratch_shapes=[pltpu.VMEM((tm, d), F32)],
        compiler_params=_params(("arbitrary", "arbitrary"), VMEM_BIG),
    )(h, gamma, dh, gg, uu, wg, wu, wd)


def ffn_bwd_weights(hn, dy, a, dg, du, name):
    r, d = hn.shape
    ns, _, fs = a.shape
    tm = WGRAD_ROWS if r % WGRAD_ROWS == 0 else _row_tile(r)

    def body(hn_ref, dy_ref, a_ref, dg_ref, du_ref, wg_ref, wu_ref, wd_ref):
        @pl.when(pl.program_id(1) == 0)
        def _():
            wg_ref[...] = jnp.zeros_like(wg_ref)
            wu_ref[...] = jnp.zeros_like(wu_ref)
            wd_ref[...] = jnp.zeros_like(wd_ref)

        hn_ = hn_ref[...]
        wg_ref[0] += lax.dot_general(dg_ref[0], hn_, TN, preferred_element_type=F32)
        wu_ref[0] += lax.dot_general(du_ref[0], hn_, TN, preferred_element_type=F32)
        wd_ref[0] += lax.dot_general(a_ref[0], dy_ref[...], TN, preferred_element_type=F32)

    row = pl.BlockSpec((tm, d), lambda s, i: (i, 0))
    act = pl.BlockSpec((1, tm, fs), lambda s, i: (s, i, 0))
    wsp = pl.BlockSpec((1, fs, d), lambda s, i: (s, 0, 0))
    return pl.pallas_call(
        body, name=name,
        out_shape=[jax.ShapeDtypeStruct((ns, fs, d), F32)] * 3,
        grid=(ns, r // tm),
        in_specs=[row, row, act, act, act],
        out_specs=[wsp, wsp, wsp],
        compiler_params=_params(("arbitrary", "arbitrary"), VMEM_MAX),
    )(hn, dy, a, dg, du)


def norm_fwd(x, gamma, name):
    r = x.shape[0]
    w = gamma.shape[1]
    tm = _row_tile(r)

    def body(x_ref, g_ref, o_ref):
        xh, _ = _rms(x_ref[...])
        o_ref[...] = (xh * g_ref[...]).astype(BF16)

    return pl.pallas_call(
        body, name=name, out_shape=jax.ShapeDtypeStruct((r, w), BF16), grid=(r // tm,),
        in_specs=[pl.BlockSpec((tm, w), lambda i: (i, 0)), pl.BlockSpec((1, w), lambda i: (0, 0))],
        out_specs=pl.BlockSpec((tm, w), lambda i: (i, 0)),
        compiler_params=_params(("arbitrary",)),
    )(x, gamma)


def norm_bwd(x, gamma, dy, dres, name):
    r = x.shape[0]
    w = gamma.shape[1]
    tm = _row_tile(r)
    has_res = dres is not None

    def body(*refs):
        if has_res:
            x_ref, g_ref, dy_ref, dr_ref, dx_ref, dgam_ref = refs
        else:
            x_ref, g_ref, dy_ref, dx_ref, dgam_ref = refs

        @pl.when(pl.program_id(0) == 0)
        def _():
            dgam_ref[...] = jnp.zeros_like(dgam_ref)

        xh, rstd = _rms(x_ref[...])
        dy_ = dy_ref[...].astype(F32)
        dgam_ref[...] += jnp.sum(dy_ * xh, axis=0, keepdims=True)
        dx = _rms_bwd(xh, rstd, dy_ * g_ref[...])
        if has_res:
            dx = dx + dr_ref[...]
        dx_ref[...] = dx

    row = pl.BlockSpec((tm, w), lambda i: (i, 0))
    vec = pl.BlockSpec((1, w), lambda i: (0, 0))
    ins = [x, gamma, dy] + ([dres] if has_res else [])
    return pl.pallas_call(
        body, name=name,
        out_shape=[jax.ShapeDtypeStruct((r, w), F32), jax.ShapeDtypeStruct((1, w), F32)],
        grid=(r // tm,), in_specs=[row, vec, row] + ([row] if has_res else []), out_specs=[row, vec],
        compiler_params=_params(("arbitrary",)),
    )(*ins)


def rowmm(a, w, name, *, nt=False, res=None, out_dtype=F32, heads_out=False):
    ha, r, ka = a.shape
    hw = w.shape[0]
    nh = max(ha, hw)
    n = w.shape[1] if nt else w.shape[2]
    tm = _row_tile(r)
    dims = NT if nt else (((1,), (0,)), ((), ()))
    has_res = res is not None

    def body(*refs):
        if has_res:
            a_ref, w_ref, r_ref, o_ref = refs
        else:
            a_ref, w_ref, o_ref = refs
        shared = a_ref[0].astype(BF16) if ha == 1 else None
        acc = None
        for h in range(nh):
            lhs = shared if ha == 1 else a_ref[h].astype(BF16)
            p = lax.dot_general(lhs, w_ref[h if hw > 1 else 0], dims, preferred_element_type=F32)
            if heads_out:
                o_ref[h] = p.astype(out_dtype)
            else:
                acc = p if acc is None else acc + p
        if not heads_out:
            if has_res:
                acc = acc + r_ref[...]
            o_ref[...] = acc.astype(out_dtype)

    in_specs = [pl.BlockSpec((ha, tm, ka), lambda i: (0, i, 0)), pl.BlockSpec(w.shape, lambda i: (0, 0, 0))]
    ins = [a, w]
    if has_res:
        in_specs.append(pl.BlockSpec((tm, n), lambda i: (i, 0)))
        ins.append(res)
    if heads_out:
        out_shape = jax.ShapeDtypeStruct((nh, r, n), out_dtype)
        out_spec = pl.BlockSpec((nh, tm, n), lambda i: (0, i, 0))
    else:
        out_shape = jax.ShapeDtypeStruct((r, n), out_dtype)
        out_spec = pl.BlockSpec((tm, n), lambda i: (i, 0))
    return pl.pallas_call(
        body, name=name, out_shape=out_shape, grid=(r // tm,), in_specs=in_specs, out_specs=out_spec,
        compiler_params=_params(("arbitrary",), VMEM_BIG),
    )(*ins)


def tnmm(a, b, name):
    ha, r, ka = a.shape
    hb, _, nb = b.shape
    nh = max(ha, hb)
    tm = _row_tile(r)

    def body(a_ref, b_ref, o_ref):
        @pl.when(pl.program_id(0) == 0)
        def _():
            o_ref[...] = jnp.zeros_like(o_ref)

        a_shared = a_ref[0].astype(BF16) if ha == 1 else None
        b_shared = b_ref[0].astype(BF16) if hb == 1 else None
        for h in range(nh):
            lhs = a_shared if ha == 1 else a_ref[h].astype(BF16)
            rhs = b_shared if hb == 1 else b_ref[h].astype(BF16)
            o_ref[h] += lax.dot_general(lhs, rhs, TN, preferred_element_type=F32)

    return pl.pallas_call(
        body, name=name, out_shape=jax.ShapeDtypeStruct((nh, ka, nb), F32), grid=(r // tm,),
        in_specs=[pl.BlockSpec((ha, tm, ka), lambda i: (0, i, 0)), pl.BlockSpec((hb, tm, nb), lambda i: (0, i, 0))],
        out_specs=pl.BlockSpec((nh, ka, nb), lambda i: (0, 0, 0)),
        compiler_params=_params(("arbitrary",), VMEM_BIG),
    )(a, b)


def rope_tables(r):
    inv = 1.0 / (ROPE_THETA ** (jnp.arange(0, QK_ROPE, 2, dtype=F32) / QK_ROPE))
    pos = (jnp.arange(r, dtype=F32) - META_ROW0)[:, None]
    ang = pos * inv[None, :]
    cos, sin = jnp.cos(ang), jnp.sin(ang)
    ones = jnp.ones((r, HEAD_W - QK_ROPE), F32)
    ctab = jnp.concatenate([cos, cos, ones], axis=1)
    stab = jnp.concatenate([-sin, sin, jnp.zeros_like(ones)], axis=1)
    return ctab, stab


def _swap_halves(z):
    lane = lax.broadcasted_iota(jnp.int32, z.shape, 1)
    up = pltpu.roll(z, HEAD_W - QK_ROPE // 2, 1)
    down = pltpu.roll(z, QK_ROPE // 2, 1)
    return jnp.where(lane < QK_ROPE // 2, up, jnp.where(lane < QK_ROPE, down, 0.0))


def proj_rope(a, w, ctab, stab, extra, name, scale=1.0):
    r, ka = a.shape
    nh = w.shape[0]
    tm = _row_tile(r)
    has_extra = extra is not None

    def body(*refs):
        if has_extra:
            a_ref, w_ref, c_ref, s_ref, e_ref, o_ref = refs
        else:
            a_ref, w_ref, c_ref, s_ref, o_ref = refs
        a_ = a_ref[...]
        ctab_, stab_ = c_ref[...], s_ref[...]
        if scale != 1.0:
            ctab_, stab_ = ctab_ * scale, stab_ * scale
        for h in range(nh):
            x = jnp.dot(a_, w_ref[h], preferred_element_type=F32)
            if has_extra:
                x = x + e_ref[...]
            o_ref[h] = (x * ctab_ + _swap_halves(x) * stab_).astype(BF16)

    tab = pl.BlockSpec((tm, HEAD_W), lambda i: (i, 0))
    in_specs = [pl.BlockSpec((tm, ka), lambda i: (i, 0)), pl.BlockSpec((nh, ka, HEAD_W), lambda i: (0, 0, 0)),
                tab, tab]
    ins = [a, w, ctab, stab]
    if has_extra:
        in_specs.append(pl.BlockSpec((tm, HEAD_W), lambda i: (i, 2)))
        ins.append(extra)
    return pl.pallas_call(
        body, name=name, out_shape=jax.ShapeDtypeStruct((nh, r, HEAD_W), BF16), grid=(r // tm,),
        in_specs=in_specs, out_specs=pl.BlockSpec((nh, tm, HEAD_W), lambda i: (0, i, 0)),
        compiler_params=_params(("arbitrary",)),
    )(*ins)


def rope_bwd_heads(d, ctab, stab, name, scale=1.0):
    nh, r, _ = d.shape
    tm = _row_tile(r)

    def body(d_ref, c_ref, s_ref, o_ref):
        ctab_, stab_ = c_ref[...], s_ref[...]
        if scale != 1.0:
            ctab_, stab_ = ctab_ * scale, stab_ * scale
        for h in range(nh):
            d_ = d_ref[h]
            o_ref[h] = (d_ * ctab_ + _swap_halves(d_ * stab_)).astype(BF16)

    tab = pl.BlockSpec((tm, HEAD_W), lambda i: (i, 0))
    blk = pl.BlockSpec((nh, tm, HEAD_W), lambda i: (0, i, 0))
    return pl.pallas_call(
        body, name=name, out_shape=jax.ShapeDtypeStruct((nh, r, HEAD_W), BF16), grid=(r // tm,),
        in_specs=[blk, tab, tab], out_specs=blk,
        compiler_params=_params(("arbitrary",)),
    )(d, ctab, stab)


def rope_bwd_sum(d, ctab, stab, name):
    nh, r, _ = d.shape
    tm = _row_tile(r)

    def body(d_ref, c_ref, s_ref, o_ref):
        d_ = d_ref[0]
        for h in range(1, nh):
            d_ = d_ + d_ref[h]
        lane = lax.broadcasted_iota(jnp.int32, d_.shape, 1)
        g = d_ * c_ref[...] + _swap_halves(d_ * s_ref[...])
        o_ref[...] = jnp.where(lane < QK_ROPE, g, 0.0)

    tab = pl.BlockSpec((tm, HEAD_W), lambda i: (i, 0))
    return pl.pallas_call(
        body, name=name, out_shape=jax.ShapeDtypeStruct((r, HEAD_W), F32), grid=(r // tm,),
        in_specs=[pl.BlockSpec((nh, tm, HEAD_W), lambda i: (0, i, 0)), tab, tab], out_specs=tab,
        compiler_params=_params(("arbitrary",)),
    )(d, ctab, stab)


def _attn_tiles(r):
    t = _row_tile(r)
    return t, t


def _mask(q0, k0, nq_, nk_, keys_on_rows=False):
    shape = (nk_, nq_) if keys_on_rows else (nq_, nk_)
    rq = q0 + lax.broadcasted_iota(jnp.int32, shape, 1 if keys_on_rows else 0)
    rk = k0 + lax.broadcasted_iota(jnp.int32, shape, 0 if keys_on_rows else 1)
    return ((rk >> CHUNK_SHIFT) <= (rq >> CHUNK_SHIFT)) & (rk >= META_ROW0)


ATTN_FWD_HEADS = 8
ATTN_FWD_UNROLL = 4
ATTN_BWD_HEADS = 4


SM_SCALE = 1.0 / math.sqrt(QK_NOPE + QK_ROPE)
LOG2E = math.log2(math.e)
Q_SCALE = SM_SCALE * LOG2E


def attn_fwd(q, k, v, name):
    nh, r, dk = q.shape
    dv = v.shape[-1]
    tq, tk = _attn_tiles(r)
    nq, nk = r // tq, r // tk

    def last_k(i):
        return ((i + 1) * tq - 1) // tk

    pairs = [(i, j) for i in range(nq) for j in range(last_k(i) + 1)]
    qi_tab = jnp.asarray([p[0] for p in pairs], jnp.int32)
    kj_tab = jnp.asarray([p[1] for p in pairs], jnp.int32)
    hb = _tile(nh, ATTN_FWD_HEADS, 1)
    hu = _tile(hb, ATTN_FWD_UNROLL, 1)

    def body(qi_ref, kj_ref, q_ref, k_ref, v_ref, o_ref, lse_ref, m_sc, l_sc, acc_sc):
        t = pl.program_id(1)
        i, j = qi_ref[t], kj_ref[t]

        @pl.when(j == 0)
        def _():
            m_sc[...] = jnp.full_like(m_sc, NEG)
            l_sc[...] = jnp.zeros_like(l_sc)
            acc_sc[...] = jnp.zeros_like(acc_sc)

        def step(masked):
            def one_head(hh):
                s = lax.dot_general(q_ref[hh], k_ref[hh], NT, preferred_element_type=F32)
                if masked:
                    s = jnp.where(_mask(i * tq, j * tk, tq, tk), s, NEG)
                m_old = m_sc[hh]
                m_new = jnp.maximum(m_old, jnp.max(s, axis=-1, keepdims=True))
                alpha = jnp.exp2(m_old - m_new)
                p = jnp.exp2(s - jnp.tile(m_new, (1, tk // LANES)))
                l_sc[hh] = alpha * l_sc[hh] + jnp.sum(p, axis=-1, keepdims=True)
                acc_sc[hh] = (alpha[:, :dv] * acc_sc[hh]
                              + jnp.dot(p.astype(BF16), v_ref[hh], preferred_element_type=F32))
                m_sc[hh] = m_new

            def head_group(g, carry):
                for u in range(hu):
                    one_head(g * hu + u)
                return carry

            lax.fori_loop(0, hb // hu, head_group, 0)

        needs_mask = (j == last_k(i)) | (j == 0)
        pl.when(needs_mask)(functools.partial(step, True))
        pl.when(jnp.logical_not(needs_mask))(functools.partial(step, False))

        @pl.when(j == last_k(i))
        def _():
            def one_head(hh, carry):
                l = l_sc[hh]
                o_ref[hh] = (acc_sc[hh] / l[:, :dv]).astype(BF16)
                lse_ref[hh] = (m_sc[hh] + jnp.log2(l))[:, :1]
                return carry

            lax.fori_loop(0, hb, one_head, 0)

    qspec = lambda w: pl.BlockSpec((hb, tq, w), lambda h, t, qi, kj: (h, qi[t], 0))
    kspec = lambda w: pl.BlockSpec((hb, tk, w), lambda h, t, qi, kj: (h, kj[t], 0))
    return pl.pallas_call(
        body, name=name,
        out_shape=[jax.ShapeDtypeStruct((nh, r, dv), BF16), jax.ShapeDtypeStruct((nh, r, 1), F32)],
        grid_spec=pltpu.PrefetchScalarGridSpec(
            num_scalar_prefetch=2, grid=(nh // hb, len(pairs)),
            in_specs=[qspec(dk), kspec(dk), kspec(dv)], out_specs=[qspec(dv), qspec(1)],
            scratch_shapes=[pltpu.VMEM((hb, tq, LANES), F32), pltpu.VMEM((hb, tq, LANES), F32),
                            pltpu.VMEM((hb, tq, dv), F32)]),
        compiler_params=_params(("arbitrary", "arbitrary"), VMEM_BIG),
    )(qi_tab, kj_tab, q, k, v)


def attn_bwd(q, k, v, do, lse, delta, dk_prev, dv_prev, name):
    nh, r, dk = q.shape
    dv = v.shape[-1]
    tq, tk = _attn_tiles(r)
    nq, nk = r // tq, r // tk
    has_prev = dk_prev is not None

    def first_q(j):
        return (j * tk) // tq

    pairs = [(j, i) for j in range(nk) for i in range(first_q(j), nq)]
    kj_tab = jnp.asarray([p[0] for p in pairs], jnp.int32)
    qi_tab = jnp.asarray([p[1] for p in pairs], jnp.int32)
    hb = _tile(nh, ATTN_BWD_HEADS, 1)

    def body(*refs):
        if has_prev:
            (kj_ref, qi_ref, q_ref, k_ref, v_ref, do_ref, lse_ref, dl_ref, pk_ref, pv_ref,
             dq_ref, dk_ref, dv_ref, dk_sc, dv_sc) = refs
        else:
            (kj_ref, qi_ref, q_ref, k_ref, v_ref, do_ref, lse_ref, dl_ref,
             dq_ref, dk_ref, dv_ref, dk_sc, dv_sc) = refs
        t = pl.program_id(1)
        j, i = kj_ref[t], qi_ref[t]

        @pl.when(t == 0)
        def _():
            dq_ref[...] = jnp.zeros_like(dq_ref)

        @pl.when(i == first_q(j))
        def _():
            dk_sc[...] = jnp.zeros_like(dk_sc)
            dv_sc[...] = jnp.zeros_like(dv_sc)

        def step(masked):
            qrows = pl.ds(pl.multiple_of(i * tq, tq), tq)
            for hh in range(hb):
                q_, k_, do_ = q_ref[hh], k_ref[hh], do_ref[hh]
                st = lax.dot_general(k_, q_, NT, preferred_element_type=F32)
                if masked:
                    st = jnp.where(_mask(i * tq, j * tk, tq, tk, keys_on_rows=True), st, NEG)
                pt = jnp.exp2(st - lse_ref[hh])
                dv_sc[hh] += jnp.dot(pt.astype(BF16), do_, preferred_element_type=F32)
                dpt = lax.dot_general(v_ref[hh], do_, NT, preferred_element_type=F32)
                dst = (pt * (dpt - dl_ref[hh])).astype(BF16)
                dk_sc[hh] += jnp.dot(dst, q_, preferred_element_type=F32)
                dq_ref[hh, qrows, :] += lax.dot_general(dst, k_, TN, preferred_element_type=F32)

        needs_mask = (i == first_q(j)) | (j == 0)
        pl.when(needs_mask)(functools.partial(step, True))
        pl.when(jnp.logical_not(needs_mask))(functools.partial(step, False))

        @pl.when(i == nq - 1)
        def _():
            dk_ = dk_sc[...] * (1.0 / LOG2E)
            dv_ = dv_sc[...]
            if has_prev:
                dk_ = dk_ + pk_ref[...]
                dv_ = dv_ + pv_ref[...]
            dk_ref[...] = dk_
            dv_ref[...] = dv_

    krow = lambda w: pl.BlockSpec((hb, tk, w), lambda h, t, kj, qi: (h, kj[t], 0))
    qrow = lambda w: pl.BlockSpec((hb, tq, w), lambda h, t, kj, qi: (h, qi[t], 0))
    qvec = pl.BlockSpec((hb, 1, tq), lambda h, t, kj, qi: (h, 0, qi[t]))
    in_specs = [qrow(dk), krow(dk), krow(dv), qrow(dv), qvec, qvec]
    ins = [q, k, v, do, lse, delta]
    if has_prev:
        in_specs += [krow(dk), krow(dv)]
        ins += [dk_prev, dv_prev]
    return pl.pallas_call(
        body, name=name,
        out_shape=[jax.ShapeDtypeStruct((nh, r, dk), F32), jax.ShapeDtypeStruct((nh, r, dk), F32),
                   jax.ShapeDtypeStruct((nh, r, dv), F32)],
        grid_spec=pltpu.PrefetchScalarGridSpec(
            num_scalar_prefetch=2, grid=(nh // hb, len(pairs)), in_specs=in_specs,
            out_specs=[pl.BlockSpec((hb, r, dk), lambda h, t, kj, qi: (h, 0, 0)), krow(dk), krow(dv)],
            scratch_shapes=[pltpu.VMEM((hb, tk, dk), F32), pltpu.VMEM((hb, tk, dv), F32)]),
        compiler_params=_params(("arbitrary", "arbitrary"), VMEM_MAX),
    )(kj_tab, qi_tab, *ins)


def attn_out_bwd(dattn, wo, o, name):
    r, d = dattn.shape
    nh, dv, _ = wo.shape
    tm = _row_tile(r)

    def body(da_ref, w_ref, o_ref, do_ref, dl_ref):
        da = da_ref[...].astype(BF16)
        for h in range(nh):
            do_ = lax.dot_general(da, w_ref[h], NT, preferred_element_type=F32).astype(BF16)
            do_ref[h] = do_
            dl_ref[h] = jnp.sum(do_.astype(F32) * o_ref[h].astype(F32), axis=-1, keepdims=True)

    return pl.pallas_call(
        body, name=name,
        out_shape=[jax.ShapeDtypeStruct((nh, r, dv), BF16), jax.ShapeDtypeStruct((nh, r, 1), F32)],
        grid=(r // tm,),
        in_specs=[pl.BlockSpec((tm, d), lambda i: (i, 0)), pl.BlockSpec((nh, dv, d), lambda i: (0, 0, 0)),
                  pl.BlockSpec((nh, tm, dv), lambda i: (0, i, 0))],
        out_specs=[pl.BlockSpec((nh, tm, dv), lambda i: (0, i, 0)), pl.BlockSpec((nh, tm, 1), lambda i: (0, i, 0))],
        compiler_params=_params(("arbitrary",)),
    )(dattn, wo, o)


def _pool_counts(row0, n, window):
    rows = row0 + lax.broadcasted_iota(jnp.int32, (n, 1), 0)
    cnt = jnp.clip(rows - META_ROW0 + 1, 1, window)
    return 1.0 / cnt.astype(F32)


def pool_fwd(h, gamma, wp, scale, name):
    r, d = h.shape
    ng, cg, _ = wp.shape
    tm = _row_tile(r)
    hb = tm // HALO

    def body(h_ref, hp_ref, g_ref, w_ref, sc_ref, o_ref):
        i = pl.program_id(0)
        xm = h_ref[...]
        xp = hp_ref[...] * jnp.where(i > 0, 1.0, 0.0)
        xx = jnp.concatenate([xp, xm], axis=0)
        xh, _ = _rms(xx)
        u = xh * g_ref[...]
        for g, win in enumerate(POOL_WINDOWS):
            sl = slice(g * cg, (g + 1) * cg)
            ug = u[:, sl]
            acc, k = ug, 1
            while k < win:
                acc = acc + pltpu.roll(acc, k, 0)
                k *= 2
            pooled = acc[HALO:] * _pool_counts(i * tm, tm, win) - ug[HALO:]
            y = jnp.dot(pooled.astype(BF16), w_ref[g], preferred_element_type=F32)
            o_ref[:, sl] = xm[:, sl] + y * sc_ref[:, sl]

    return pl.pallas_call(
        body, name=name, out_shape=jax.ShapeDtypeStruct((r, d), F32), grid=(r // tm,),
        in_specs=[pl.BlockSpec((tm, d), lambda i: (i, 0)),
                  pl.BlockSpec((HALO, d), lambda i: (jnp.maximum(i * hb - 1, 0), 0)),
                  pl.BlockSpec((1, d), lambda i: (0, 0)), pl.BlockSpec((ng, cg, cg), lambda i: (0, 0, 0)),
                  pl.BlockSpec((1, d), lambda i: (0, 0))],
        out_specs=pl.BlockSpec((tm, d), lambda i: (i, 0)),
        compiler_params=_params(("arbitrary",), VMEM_BIG),
    )(h, h, gamma, wp, scale)


def pool_bwd(h, gamma, wp, scale, dh, name):
    r, d = h.shape
    ng, cg, _ = wp.shape
    tm = _row_tile(r)
    hb = tm // HALO
    nt = r // tm

    def body(h_ref, hp_ref, dh_ref, dn_ref, g_ref, w_ref, sc_ref, dx_ref, dgam_ref, dw_ref, dsc_ref, du_sc):
        i = pl.program_id(0)

        @pl.when(i == 0)
        def _():
            dgam_ref[...] = jnp.zeros_like(dgam_ref)
            dw_ref[...] = jnp.zeros_like(dw_ref)
            dsc_ref[...] = jnp.zeros_like(dsc_ref)

        xm = h_ref[...]
        xp = hp_ref[...] * jnp.where(i > 0, 1.0, 0.0)
        xh_all, rstd_all = _rms(jnp.concatenate([xp, xm], axis=0))
        u = xh_all * g_ref[...]
        dm = dh_ref[...]
        dn = dn_ref[...] * jnp.where(i < nt - 1, 1.0, 0.0)
        dd = jnp.concatenate([dm, dn], axis=0)
        for g, win in enumerate(POOL_WINDOWS):
            sl = slice(g * cg, (g + 1) * cg)
            ug = u[:, sl]
            acc, k = ug, 1
            while k < win:
                acc = acc + pltpu.roll(acc, k, 0)
                k *= 2
            pooled = (acc[HALO:] * _pool_counts(i * tm, tm, win) - ug[HALO:]).astype(BF16)
            y = jnp.dot(pooled, w_ref[g], preferred_element_type=F32)
            dsc_ref[:, sl] += jnp.sum(dm[:, sl] * y, axis=0, keepdims=True)
            dyp = (dd[:, sl] * sc_ref[:, sl]).astype(BF16)
            dw_ref[g] += lax.dot_general(pooled, dyp[:tm], TN, preferred_element_type=F32)
            dpo = lax.dot_general(dyp, w_ref[g], NT, preferred_element_type=F32)
            z = dpo * _pool_counts(i * tm, tm + HALO, win)
            fwd, k = z, 1
            while k < win:
                fwd = fwd + pltpu.roll(fwd, tm + HALO - k, 0)
                k *= 2
            du_sc[:, sl] = fwd[:tm] - dpo[:tm]
        du = du_sc[...]
        xh, rstd = xh_all[HALO:], rstd_all[HALO:]
        dgam_ref[...] += jnp.sum(du * xh, axis=0, keepdims=True)
        dx_ref[...] = dm + _rms_bwd(xh, rstd, du * g_ref[...])

    row = pl.BlockSpec((tm, d), lambda i: (i, 0))
    vec = pl.BlockSpec((1, d), lambda i: (0, 0))
    prev = pl.BlockSpec((HALO, d), lambda i: (jnp.maximum(i * hb - 1, 0), 0))
    nxt = pl.BlockSpec((HALO, d), lambda i: (jnp.minimum((i + 1) * hb, r // HALO - 1), 0))
    wsp = pl.BlockSpec((ng, cg, cg), lambda i: (0, 0, 0))
    return pl.pallas_call(
        body, name=name,
        out_shape=[jax.ShapeDtypeStruct((r, d), F32), jax.ShapeDtypeStruct((1, d), F32),
                   jax.ShapeDtypeStruct((ng, cg, cg), F32), jax.ShapeDtypeStruct((1, d), F32)],
        grid=(nt,), in_specs=[row, prev, row, nxt, vec, wsp, vec], out_specs=[row, vec, wsp, vec],
        scratch_shapes=[pltpu.VMEM((tm, d), F32)],
        compiler_params=_params(("arbitrary",), VMEM_BIG),
    )(h, h, dh, dh, gamma, wp, scale)


def loss_head(h, gamma, target, seq, name):
    r, d = h.shape
    tm = _row_tile(r)

    def body(h_ref, g_ref, t_ref, sse_ref, dh_ref, dgam_ref):
        i = pl.program_id(0)

        @pl.when(i == 0)
        def _():
            sse_ref[...] = jnp.zeros_like(sse_ref)
            dgam_ref[...] = jnp.zeros_like(dgam_ref)

        xh, rstd = _rms(h_ref[...])
        rows = i * tm + lax.broadcasted_iota(jnp.int32, (tm, 1), 0)
        valid = ((rows >= FRONT) & (rows < FRONT + seq)).astype(F32)
        e = (xh * g_ref[...] - t_ref[...]) * valid
        sse_ref[...] += jnp.sum(jnp.sum(e * e, axis=1, keepdims=True), axis=0, keepdims=True)
        dy = e * (1.0 / d)
        dgam_ref[...] += jnp.sum(dy * xh, axis=0, keepdims=True)
        dh_ref[...] = _rms_bwd(xh, rstd, dy * g_ref[...])

    row = pl.BlockSpec((tm, d), lambda i: (i, 0))
    vec = pl.BlockSpec((1, d), lambda i: (0, 0))
    return pl.pallas_call(
        body, name=name,
        out_shape=[jax.ShapeDtypeStruct((1, 1), F32), jax.ShapeDtypeStruct((r, d), F32),
                   jax.ShapeDtypeStruct((1, d), F32)],
        grid=(r // tm,), in_specs=[row, vec, row],
        out_specs=[pl.BlockSpec((1, 1), lambda i: (0, 0)), row, vec],
        compiler_params=_params(("arbitrary",)),
    )(h, gamma, target)


SMALL = ("meta_tokens", "pool_w", "pool_scale", "w_dkv", "w_uk", "w_uv", "w_dq", "w_uq", "w_o")


def _pack(parts):
    flat = jnp.concatenate([p.reshape(-1) for p in parts])
    n = flat.shape[0]
    unit = PACK_W * 2 * SUBLANES
    n_pad = -(-n // unit) * unit
    return jnp.pad(flat, (0, n_pad - n)).reshape(n_pad // PACK_W, PACK_W)


def _unpack(buf, shapes, lead=()):
    flat = buf.reshape(lead + (-1,))
    out, off = [], 0
    for shp in shapes:
        n = math.prod(shp)
        out.append(flat[..., off:off + n].reshape(lead + tuple(shp)))
        off += n
    return out


def _cols_from_shards(a, axis):
    a = jnp.moveaxis(a, 0, axis)
    shp = a.shape
    return a.reshape(shp[:axis] + (shp[axis] * shp[axis + 1],) + shp[axis + 2:])


def _cols_to_shards(a, axis):
    shp = a.shape
    a = a.reshape(shp[:axis] + (N_SHARD, shp[axis] // N_SHARD) + shp[axis + 1:])
    return jnp.moveaxis(a, axis, 0)


SMALL_AXIS = {"meta_tokens": 1, "pool_w": 2, "pool_scale": 1, "w_dkv": 0, "w_uk": 1, "w_uv": 1,
              "w_dq": 1, "w_uq": 2, "w_o": 2}


def kernel(x, meta_tokens, ffn1_norm, ffn1_w_gate, ffn1_w_up, ffn1_w_down, mix_norm, ffn2_norm, ffn2_w_gate, ffn2_w_up, ffn2_w_down, pool_w, pool_scale, kv_in_norm, w_dkv, kv_latent_norm, w_uk, w_uv, w_dq, q_latent_norm, w_uq, w_o, final_norm, loss_target, m_meta_tokens, m_ffn1_norm, m_ffn1_w_gate, m_ffn1_w_up, m_ffn1_w_down, m_mix_norm, m_ffn2_norm, m_ffn2_w_gate, m_ffn2_w_up, m_ffn2_w_down, m_pool_w, m_pool_scale, m_kv_in_norm, m_w_dkv, m_kv_latent_norm, m_w_uk, m_w_uv, m_w_dq, m_q_latent_norm, m_w_uq, m_w_o, m_final_norm, v_meta_tokens, v_ffn1_norm, v_ffn1_w_gate, v_ffn1_w_up, v_ffn1_w_down, v_mix_norm, v_ffn2_norm, v_ffn2_w_gate, v_ffn2_w_up, v_ffn2_w_down, v_pool_w, v_pool_scale, v_kv_in_norm, v_w_dkv, v_kv_latent_norm, v_w_uk, v_w_uv, v_w_dq, v_q_latent_norm, v_w_uq, v_w_o, v_final_norm):
    args = dict(locals())
    W = {n: args[n] for n in NAMES}
    M = {n: args["m_" + n] for n in NAMES}
    V = {n: args["v_" + n] for n in NAMES}
    TRANSPOSED = ("ffn1_w_gate", "ffn1_w_up", "ffn2_w_gate", "ffn2_w_up")
    for n in TRANSPOSED:
        W[n], M[n], V[n] = (jnp.swapaxes(a, 1, 2) for a in (W[n], M[n], V[n]))

    depth = ffn1_norm.shape[0]
    n_a = pool_w.shape[0]
    seq, d = x.shape[1], x.shape[2]
    nh = N_HEADS
    r = -(-(FRONT + seq) // LANES) * LANES

    cx, cy, cc = lax.axis_index("x"), lax.axis_index("y"), lax.axis_index("c")
    c_arr = jnp.reshape(cc, (1,)).astype(jnp.int32)
    s_arr = jnp.reshape(2 * cx + cy, (1,)).astype(jnp.int32)

    small_shapes = [W[n].shape for n in SMALL]

    ffn_src = {f: tuple(W[f + t].astype(BF16) for t in ("_w_gate", "_w_up", "_w_down")) for f in ("ffn1", "ffn2")}
    ffn_order = [(f, l) for l in range(depth) for f in ("ffn1", "ffn2")]
    ffn_w = {}
    gate = []
    ag_state = [None]

    def gated(a):
        if gate:
            a = a + sum(gate[1:], gate[0]).astype(a.dtype)
            gate.clear()
        return a

    def vec(a):
        return gated(a.reshape(1, -1))

    def ag_start(idx, dep):
        f, l = ffn_order[idx]
        shards = [w_[l] + dep for w_ in ffn_src[f]]
        lands = [lax.dynamic_update_slice(lax.empty((N_SHARD,) + a.shape, BF16), a[None], (2 * cx + cy, 0, 0))
                 for a in shards]
        ag_state[0] = exchange_start(shards, lands, f"ag_start_{f}_{l}", True)
        gate.append(ag_state[0][4][0, 0])

    def ag_wait(idx, after):
        f, l = ffn_order[idx]
        _, lands = exchange_wait(ag_state[0], after, f"ag_wait_{f}_{l}", True)
        ffn_w[f, l] = lands
        if idx + 1 < len(ffn_order):
            ag_start(idx + 1, lands[0][0, 0, 0] * jnp.zeros((), BF16))

    gathered = all_gather_shards([_pack([W[n] for n in SMALL])], "ag_small")[0]
    ag_start(0, (gathered[0, 0, 0] * 0.0).astype(BF16))
    small_full = {}
    for n, part in zip(SMALL, _unpack(gathered, small_shapes, (N_SHARD,))):
        small_full[n] = _cols_from_shards(part, SMALL_AXIS[n])

    meta_full = small_full["meta_tokens"]
    wp = small_full["pool_w"].astype(BF16)
    pscale = small_full["pool_scale"]
    wdkv = jnp.pad(small_full["w_dkv"], ((0, 0), (0, HEAD_W - QK_ROPE))).astype(BF16)[None]
    wuk = small_full["w_uk"].reshape(KV_RANK, nh, QK_NOPE).transpose(1, 0, 2)
    wk_h = jnp.concatenate([jnp.zeros((nh, KV_RANK, HEAD_W - QK_NOPE), F32), wuk], axis=-1).astype(BF16)
    wv_h = small_full["w_uv"].reshape(KV_RANK, nh, V_HEAD).transpose(1, 0, 2).astype(BF16)
    wdq = small_full["w_dq"].astype(BF16)
    wuq = small_full["w_uq"].reshape(-1, Q_RANK, nh, QK_NOPE + QK_ROPE).transpose(0, 2, 1, 3)
    wq_h = jnp.concatenate([wuq[..., QK_NOPE:], jnp.zeros(wuq.shape[:-1] + (HEAD_W - QK_NOPE - QK_ROPE,), F32),
                            wuq[..., :QK_NOPE]], axis=-1).astype(BF16)
    wo_h = small_full["w_o"].reshape(-1, nh, V_HEAD, d).astype(BF16)

    ctab, stab = rope_tables(r)

    h = jnp.concatenate([jnp.zeros((META_ROW0, d), F32), meta_full, x[0],
                         jnp.zeros((r - FRONT - seq, d), F32)], axis=0)
    target = jnp.concatenate([jnp.zeros((FRONT, d), F32), loss_target[0],
                              jnp.zeros((r - FRONT - seq, d), F32)], axis=0)
    saved = []
    kv = None
    for l in range(depth):
        sv = {"h0": h}
        ag_wait(2 * l, h)
        h, sv["g1"], sv["u1"] = ffn_fwd(h, vec(ffn1_norm[l]), *ffn_w["ffn1", l], f"ffn1_fwd_{l}")
        sv["h1"] = h
        if l < n_a:
            h = pool_fwd(h, vec(mix_norm[l]), wp[l], vec(pscale[l]), f"pool_fwd_{l}")
        else:
            j = l - n_a
            u = norm_fwd(h, vec(mix_norm[l]), f"mixnorm_{l}")
            cq0 = rowmm(u[None], wdq[j][None], f"dq_{l}")
            cq = norm_fwd(cq0, vec(q_latent_norm[j]), f"qnorm_{l}")
            q = proj_rope(cq, wq_h[j], ctab, stab, None, f"qproj_{l}", scale=Q_SCALE)
            o, lse = attn_fwd(q, kv["k"], kv["v"], f"attn_fwd_{l}")
            h = rowmm(o, wo_h[j], f"oproj_{l}", res=h)
            sv.update(u=u, cq0=cq0, cq=cq, q=q, o=o, lse=lse)
        sv["h2"] = h
        ag_wait(2 * l + 1, h)
        h, sv["g2"], sv["u2"] = ffn_fwd(h, vec(ffn2_norm[l]), *ffn_w["ffn2", l], f"ffn2_fwd_{l}")
        saved.append(sv)
        if l == n_a - 1:
            hkv = norm_fwd(h, vec(kv_in_norm), "kvin_norm")
            ckr = rowmm(hkv[None], wdkv, "dkv")
            ckv = norm_fwd(ckr, vec(kv_latent_norm), "kvlat_norm")
            kv = {"h": h, "hkv": hkv, "ckr": ckr, "ckv": ckv,
                  "k": proj_rope(ckv, wk_h, ctab, stab, ckr, "kproj"),
                  "v": rowmm(ckv[None], wv_h, "vproj", out_dtype=BF16, heads_out=True)}

    sse, dh, dfinal = loss_head(h, vec(final_norm), target, seq, "loss_head")
    loss = lax.psum(0.5 / d * sse[0, 0], ("x", "y", "c"))

    G = {}
    FFN = ("ffn1_w_gate", "ffn1_w_up", "ffn1_w_down", "ffn2_w_gate", "ffn2_w_up", "ffn2_w_down")
    per = {n: [lax.empty(W[n].shape, F32) for _ in range(4)] for n in FFN}
    pending = []

    def rs_complete(after):
        f, l, state = pending.pop()
        names = [f + "_w_gate", f + "_w_up", f + "_w_down"]
        for n, g_ in zip(names, rs_finish(state, after, s_arr, c_arr, f"{f}_{l}")):
            per[n] = adamw_into(W[n], g_, M[n], V[n], per[n], l, f"adamw_{n}_{l}")
        gate.append(per[names[-1]][1][0, 0, 0] * 0.0)

    dnorm = {n: [None] * depth for n in ("ffn1_norm", "mix_norm", "ffn2_norm")}
    dqnorm = [None] * (depth - n_a)
    dpool_w, dpool_scale = [None] * n_a, [None] * n_a
    dwdq, dwq_h, dwo_h = [None] * (depth - n_a), [None] * (depth - n_a), [None] * (depth - n_a)

    def ffn_backward(f, l, h_in, dh_, gg, uu):
        gam = ffn1_norm[l] if f == "ffn1" else ffn2_norm[l]
        wg_, wu_, wd_ = ffn_w[f, l]
        dh_in, dgam, hn, dy, dg, du, a = ffn_bwd_act(h_in, vec(gam), dh_, gg, uu, wg_, wu_, wd_, f"{f}_bwd_act_{l}")
        dwg, dwu, dwd = ffn_bwd_weights(hn, dy, a, dg, du, f"{f}_bwd_w_{l}")
        state = rs_begin([dwg, dwu, dwd], f"{f}_{l}")
        gate.append(state[4][0, 0])
        if pending:
            rs_complete(state[4])
        pending.append((f, l, state))
        dnorm[f + "_norm"][l] = dgam[0]
        return dh_in

    dk_tot = dv_tot = None
    for l in reversed(range(depth)):
        sv = saved[l]
        if l == n_a - 1:
            dckv = rowmm(dk_tot, gated(wk_h), "kproj_bwd", nt=True)
            dkr = rope_bwd_sum(dk_tot, ctab, stab, "kproj_bwd_rope")
            dckv = rowmm(dv_tot, wv_h, "vproj_bwd", nt=True, res=dckv)
            dwk_h = tnmm(kv["ckv"][None], dk_tot, "kproj_bwd_w")
            dwv_h = tnmm(kv["ckv"][None], dv_tot, "vproj_bwd_w")
            dlat, dkvlat = norm_bwd(kv["ckr"], vec(kv_latent_norm), dckv, None, "kvlat_norm_bwd")
            dckr = jnp.concatenate([dlat, dkr], axis=1).astype(BF16)
            dhkv = rowmm(dckr[None], wdkv, "dkv_bwd", nt=True)
            dwdkv = tnmm(kv["hkv"][None], dckr[None], "dkv_bwd_w")[0]
            dh, dkvin = norm_bwd(kv["h"], vec(kv_in_norm), dhkv, dh, "kvin_norm_bwd")
            G["w_dkv"] = dwdkv[:, :KV_RANK + QK_ROPE]
            G["w_uk"] = dwk_h[..., HEAD_W - QK_NOPE:].transpose(1, 0, 2).reshape(KV_RANK, nh * QK_NOPE)
            G["w_uv"] = dwv_h.transpose(1, 0, 2).reshape(KV_RANK, nh * V_HEAD)
        dh = ffn_backward("ffn2", l, sv["h2"], dh, sv["g2"], sv["u2"])
        if l < n_a:
            dh, dmix, dpool_w[l], dps = pool_bwd(sv["h1"], vec(mix_norm[l]), wp[l], vec(pscale[l]), dh, f"pool_bwd_{l}")
            dnorm["mix_norm"][l] = dmix[0]
            dpool_scale[l] = dps[0]
        else:
            j = l - n_a
            do, delta = attn_out_bwd(dh, gated(wo_h[j]), sv["o"], f"oproj_bwd_{l}")
            dwo_h[j] = tnmm(sv["o"], dh[None], f"oproj_bwd_w_{l}")
            dq, dk_tot, dv_tot = attn_bwd(sv["q"], kv["k"], kv["v"], do, sv["lse"].reshape(nh, 1, r),
                                          delta.reshape(nh, 1, r), dk_tot, dv_tot, f"attn_bwd_{l}")
            dxq = rope_bwd_heads(dq, ctab, stab, f"qproj_bwd_rope_{l}", scale=SM_SCALE)
            dcq = rowmm(dxq, wq_h[j], f"qproj_bwd_{l}", nt=True)
            dwq_h[j] = tnmm(sv["cq"][None], dxq, f"qproj_bwd_w_{l}")
            dcq0, dqn = norm_bwd(sv["cq0"], vec(q_latent_norm[j]), dcq, None, f"qnorm_bwd_{l}")
            dqnorm[j] = dqn[0]
            dcq0b = dcq0.astype(BF16)
            du = rowmm(dcq0b[None], wdq[j][None], f"dq_bwd_{l}", nt=True)
            dwdq[j] = tnmm(sv["u"][None], dcq0b[None], f"dq_bwd_w_{l}")[0]
            dh, dmix = norm_bwd(sv["h1"], vec(mix_norm[l]), du, dh, f"mixnorm_bwd_{l}")
            dnorm["mix_norm"][l] = dmix[0]
        dh = ffn_backward("ffn1", l, sv["h0"], dh, sv["g1"], sv["u1"])

    grad_x = dh[FRONT:FRONT + seq][None]
    G["meta_tokens"] = dh[META_ROW0:FRONT]
    G["pool_w"] = jnp.stack(dpool_w)
    G["pool_scale"] = jnp.stack(dpool_scale)
    G["w_dq"] = jnp.stack(dwdq)
    dwq = jnp.stack(dwq_h)
    dwq = jnp.concatenate([dwq[..., HEAD_W - QK_NOPE:], dwq[..., :QK_ROPE]], axis=-1)
    G["w_uq"] = dwq.transpose(0, 2, 1, 3).reshape(-1, Q_RANK, nh * (QK_NOPE + QK_ROPE))
    G["w_o"] = jnp.stack(dwo_h).reshape(-1, nh * V_HEAD, d)

    REPL = ("ffn1_norm", "mix_norm", "ffn2_norm", "kv_in_norm", "kv_latent_norm", "q_latent_norm", "final_norm")
    grep = {"ffn1_norm": jnp.stack(dnorm["ffn1_norm"]), "mix_norm": jnp.stack(dnorm["mix_norm"]),
            "ffn2_norm": jnp.stack(dnorm["ffn2_norm"]), "kv_in_norm": dkvin[0], "kv_latent_norm": dkvlat[0],
            "q_latent_norm": jnp.stack(dqnorm), "final_norm": dfinal[0]}

    def pack128(parts):
        flat = jnp.concatenate([p.reshape(-1) for p in parts])
        n = flat.shape[0]
        n_pad = -(-n // (LANES * SUBLANES)) * (LANES * SUBLANES)
        return jnp.pad(flat, (0, n_pad - n)).reshape(-1, LANES)

    rep_shapes = [W[n].shape for n in REPL]
    g_rep = all_reduce_small(pack128([grep[n] for n in REPL]), "ar_repl")
    d_rep, m_rep, v_rep = adamw(pack128([W[n] for n in REPL]), g_rep, pack128([M[n] for n in REPL]),
                                pack128([V[n] for n in REPL]), "adamw_repl")
    out_g, out_d, out_m, out_v = {}, {}, {}, {}
    for dst, buf in ((out_g, g_rep), (out_d, d_rep), (out_m, m_rep), (out_v, v_rep)):
        for n, a in zip(REPL, _unpack(buf, rep_shapes)):
            dst[n] = a

    g_small = jnp.stack([_pack([_cols_to_shards(G[n], SMALL_AXIS[n])[s] for n in SMALL]) for s in range(N_SHARD)])
    g_small = reduce_scatter([gated(g_small)], c_arr, s_arr, "small")[0]
    d_s, m_s, v_s = adamw(_pack([W[n] for n in SMALL]), g_small, _pack([M[n] for n in SMALL]),
                          _pack([V[n] for n in SMALL]), "adamw_small")
    for dst, buf in ((out_g, g_small), (out_d, d_s), (out_m, m_s), (out_v, v_s)):
        for n, a in zip(SMALL, _unpack(buf, small_shapes)):
            dst[n] = a

    rs_complete(d_s)
    for n in FFN:
        out_g[n], out_d[n], out_m[n], out_v[n] = (
            jnp.swapaxes(a, 1, 2) if n in TRANSPOSED else a for a in per[n])

    return (loss, grad_x, *[out_g[n] for n in NAMES], *[out_d[n] for n in NAMES],
            *[out_m[n] for n in NAMES], *[out_v[n] for n in NAMES])


NAMES = ("meta_tokens", "ffn1_norm", "ffn1_w_gate", "ffn1_w_up", "ffn1_w_down", "mix_norm", "ffn2_norm",
         "ffn2_w_gate", "ffn2_w_up", "ffn2_w_down", "pool_w", "pool_scale", "kv_in_norm", "w_dkv",
         "kv_latent_norm", "w_uk", "w_uv", "w_dq", "q_latent_norm", "w_uq", "w_o", "final_norm")
```

```python
import functools
import math

import jax
import jax.numpy as jnp
from jax import lax
from jax.experimental import pallas as pl
from jax.experimental.pallas import tpu as pltpu

F32 = jnp.float32
BF16 = jnp.bfloat16
MESH = pl.DeviceIdType.MESH
ANY = pl.BlockSpec(memory_space=pl.ANY)

EPS = 1e-6
CHUNK = 64
CHUNK_SHIFT = 6
N_META = 16
FRONT = 64
META_ROW0 = FRONT - N_META
POOL_WINDOWS = (2, 4, 8, 16)
HALO = 16
N_HEADS = 8
QK_NOPE = 64
QK_ROPE = 32
V_HEAD = 64
HEAD_W = 128
KV_RANK = 256
Q_RANK = 384
ROPE_THETA = 10000.0
NEG = -1e30
N_SHARD = 4
LANES = 128
SUBLANES = 8
PACK_W = 512
VMEM_BIG = 52 * 1024 * 1024
VMEM_MAX = 60 * 1024 * 1024
WGRAD_ROWS = 1664
FFN_FWD_SHARDS = 2
FFN_BWD_SHARDS = 2
FFN_BWD_ROW_SPLIT = 2

ADAM_LR = 0.001
ADAM_B1 = 0.9
ADAM_B2 = 0.999
ADAM_EPS = 1e-08
ADAM_WD = 0.01
ADAM_STEP = 10

NT = (((1,), (1,)), ((), ()))
TN = (((0,), (0,)), ((), ()))


def _params(sem=None, vmem=None):
    return pltpu.CompilerParams(dimension_semantics=sem, vmem_limit_bytes=vmem)


def _tile(n, pref, mult=SUBLANES):
    best = None
    for t in range(mult, min(n, pref) + 1, mult):
        if n % t == 0:
            best = t
    return best if best is not None else n


def _row_tile(r):
    return 640 if r % 640 == 0 else 128


def _rms(x):
    rstd = lax.rsqrt(jnp.mean(x * x, axis=-1, keepdims=True) + EPS)
    return x * rstd, rstd


def _rms_bwd(xh, rstd, dxh):
    return rstd * (dxh - xh * jnp.mean(dxh * xh, axis=-1, keepdims=True))


def _sigmoid(x):
    return 1.0 / (1.0 + jnp.exp(-x))


def _place():
    x, y, c = lax.axis_index("x"), lax.axis_index("y"), lax.axis_index("c")
    chips = [(1 - x, y), (x, 1 - y), (1 - x, 1 - y)]
    return x, y, c, chips


def all_gather_shards(shards, name):
    n = len(shards)
    slot = 2 * lax.axis_index("x") + lax.axis_index("y")
    lands = [lax.dynamic_update_slice(lax.empty((N_SHARD,) + a.shape, a.dtype), a[None], (slot, 0, 0)) for a in shards]

    def body(*refs):
        ins, outs = refs[:n], refs[2 * n:3 * n]
        send1, recv1, send2, recv2 = refs[3 * n:]
        x, y, c, chips = _place()
        s = 2 * x + y
        sib = (x, y, 1 - c)

        def rcopy(k, j, src, dst, to, first):
            return pltpu.make_async_remote_copy(
                src_ref=src, dst_ref=dst,
                send_sem=(send1 if first else send2).at[k, j],
                recv_sem=(recv1 if first else recv2).at[k, j],
                device_id=to, device_id_type=MESH)

        started = []
        for k in range(n):
            hf = ins[k].shape[0] // 2
            for j, (cx, cy) in enumerate(chips):
                r = rcopy(k, j, ins[k].at[pl.ds(c * hf, hf)], outs[k].at[s, pl.ds(c * hf, hf)],
                          (cx, cy, c), True)
                r.start()
                started.append(r)
        for k in range(n):
            hf = ins[k].shape[0] // 2
            for j, (cx, cy) in enumerate(chips):
                blk = outs[k].at[2 * cx + cy, pl.ds(c * hf, hf)]
                rcopy(k, j, blk, blk, (cx, cy, c), True).wait_recv()
                f = rcopy(k, j, blk, blk, sib, False)
                f.start()
                started.append(f)
        for k in range(n):
            hf = ins[k].shape[0] // 2
            for j, (cx, cy) in enumerate(chips):
                blk = outs[k].at[2 * cx + cy, pl.ds((1 - c) * hf, hf)]
                rcopy(k, j, blk, blk, sib, False).wait_recv()
        for r in started:
            r.wait_send()

    return pl.pallas_call(
        body, name=name,
        out_shape=[jax.ShapeDtypeStruct((N_SHARD,) + a.shape, a.dtype) for a in shards],
        in_specs=[ANY] * (2 * n), out_specs=[ANY] * n,
        input_output_aliases={n + k: k for k in range(n)},
        scratch_shapes=[pltpu.SemaphoreType.DMA((n, 3))] * 4,
    )(*shards, *lands)


def sibling_join_halves(ts, name):
    n = len(ts)

    def body(*refs):
        ins, outs = refs[:n], refs[n:2 * n]
        send, recv = refs[2 * n:]
        x, y, c, _ = _place()
        sib = (x, y, 1 - c)

        def copy(k, half):
            hf = ins[k].shape[0] // 2
            rows = pl.ds(half * hf, hf)
            return pltpu.make_async_remote_copy(
                src_ref=ins[k].at[rows], dst_ref=outs[k].at[rows],
                send_sem=send.at[k], recv_sem=recv.at[k], device_id=sib, device_id_type=MESH)

        cps = [copy(k, c) for k in range(n)]
        for r in cps:
            r.start()
        for k in range(n):
            copy(k, 1 - c).wait_recv()
        for r in cps:
            r.wait_send()

    return pl.pallas_call(
        body, name=name,
        out_shape=[jax.ShapeDtypeStruct(a.shape, a.dtype) for a in ts],
        in_specs=[ANY] * n, out_specs=[ANY] * n, input_output_aliases={k: k for k in range(n)},
        scratch_shapes=[pltpu.SemaphoreType.DMA((n,))] * 2,
    )(*ts)


def all_reduce_small(part, name):
    m, w = part.shape

    def body(x_ref, tot_ref, gat_ref, send_sems, recv_sems):
        x, y, c, chips = _place()
        me, sib = (x, y, c), (x, y, 1 - c)

        def slot(px, py, pc):
            return gat_ref.at[4 * px + 2 * py + pc]

        def copy(k, block, to, src=None):
            return pltpu.make_async_remote_copy(
                src_ref=slot(*block) if src is None else src, dst_ref=slot(*block),
                send_sem=send_sems.at[k], recv_sem=recv_sems.at[k], device_id=to, device_id_type=MESH)

        gat_ref[4 * x + 2 * y + c] = x_ref[...]
        first = [copy(0, me, sib, src=x_ref)]
        first += [copy(1 + j, me, (*chip, c), src=x_ref) for j, chip in enumerate(chips)]
        for cp in first:
            cp.start()
        passed = [copy(4 + j, (*chip, c), sib) for j, chip in enumerate(chips)]
        for j, chip in enumerate(chips):
            copy(1 + j, (*chip, c), me).wait_recv()
            passed[j].start()
        copy(0, sib, me).wait_recv()
        for j, chip in enumerate(chips):
            copy(4 + j, (*chip, 1 - c), me).wait_recv()
        for cp in first + passed:
            cp.wait_send()
        tot = gat_ref[0]
        for d in range(1, 8):
            tot = tot + gat_ref[d]
        tot_ref[...] = tot

    return pl.pallas_call(
        body, name=name,
        out_shape=jax.ShapeDtypeStruct((m, w), F32),
        in_specs=[pl.BlockSpec(memory_space=pltpu.VMEM)],
        out_specs=pl.BlockSpec(memory_space=pltpu.VMEM),
        scratch_shapes=[pltpu.VMEM((8, m, w), F32), pltpu.SemaphoreType.DMA((7,)), pltpu.SemaphoreType.DMA((7,))],
    )(part)


def adamw(w, g, m, v, name):
    a, b = w.shape
    tb = _tile(a, 256)
    c1 = 1.0 - ADAM_B1 ** ADAM_STEP
    c2 = 1.0 - ADAM_B2 ** ADAM_STEP

    def body(w_ref, g_ref, m_ref, v_ref, d_ref, mo_ref, vo_ref):
        g_ = g_ref[...]
        m_ = ADAM_B1 * m_ref[...] + (1.0 - ADAM_B1) * g_
        v_ = ADAM_B2 * v_ref[...] + (1.0 - ADAM_B2) * (g_ * g_)
        m_hat = m_ / c1
        v_hat = v_ / c2
        d_ref[...] = -ADAM_LR * (m_hat / (jnp.sqrt(v_hat) + ADAM_EPS) + ADAM_WD * w_ref[...])
        mo_ref[...] = m_
        vo_ref[...] = v_

    spec = pl.BlockSpec((tb, b), lambda i: (i, 0))
    return pl.pallas_call(
        body, name=name,
        out_shape=[jax.ShapeDtypeStruct((a, b), F32)] * 3,
        grid=(a // tb,), in_specs=[spec] * 4, out_specs=[spec] * 3,
        compiler_params=_params(("arbitrary",)),
    )(w, g, m, v)


def adamw_into(w_all, g, m_all, v_all, prev, l, name):
    nl, a, b = w_all.shape
    tb = _tile(a, 256)
    c1 = 1.0 - ADAM_B1 ** ADAM_STEP
    c2 = 1.0 - ADAM_B2 ** ADAM_STEP

    def body(w_ref, g_ref, m_ref, v_ref, p0, p1, p2, p3, go_ref, d_ref, mo_ref, vo_ref):
        g_ = g_ref[...]
        m_ = ADAM_B1 * m_ref[0] + (1.0 - ADAM_B1) * g_
        v_ = ADAM_B2 * v_ref[0] + (1.0 - ADAM_B2) * (g_ * g_)
        m_hat = m_ / c1
        v_hat = v_ / c2
        go_ref[0] = g_
        d_ref[0] = -ADAM_LR * (m_hat / (jnp.sqrt(v_hat) + ADAM_EPS) + ADAM_WD * w_ref[0])
        mo_ref[0] = m_
        vo_ref[0] = v_

    lay = pl.BlockSpec((1, tb, b), lambda i: (l, i, 0))
    return pl.pallas_call(
        body, name=name,
        out_shape=[jax.ShapeDtypeStruct((nl, a, b), F32)] * 4,
        grid=(a // tb,), in_specs=[lay, pl.BlockSpec((tb, b), lambda i: (i, 0)), lay, lay] + [ANY] * 4,
        out_specs=[lay] * 4, input_output_aliases={4: 0, 5: 1, 6: 2, 7: 3},
        compiler_params=_params(("arbitrary",)),
    )(w_all, g, m_all, v_all, *prev)


def reduce_scatter(gs, c_arr, s_arr, tag):
    lands = rs_exchange(gs, f"rs_exchange_{tag}")
    ts = [sum_eight(g, r, s_arr, c_arr, f"rs_sum_{tag}_{k}") for k, (g, r) in enumerate(zip(gs, lands))]
    return sibling_join_halves(ts, f"rs_join_{tag}")


HBM_SPEC = pl.BlockSpec(memory_space=pltpu.HBM)
SEM_SPEC = pl.BlockSpec(memory_space=pltpu.SEMAPHORE)
EFFECT = pltpu.SideEffectType.DATAFLOW_SIDE_EFFECTING


def _in_hbm(a):
    return pltpu.with_memory_space_constraint(a, pltpu.HBM)


def _exchange_copy(k, j, chip, c, s, srcs, lands, send, recv, gather, receiving):
    cx, cy = chip
    src = srcs[k] if gather else srcs[k].at[2 * cx + cy]
    if gather:
        dst = lands[k].at[2 * cx + cy] if receiving else lands[k].at[s]
    else:
        dst = lands[k].at[j]
    return pltpu.make_async_remote_copy(src_ref=src, dst_ref=dst, send_sem=send.at[3 * k + j], recv_sem=recv.at[3 * k + j],
                                        device_id=(cx, cy, c), device_id_type=MESH)


def exchange_start(srcs, lands, name, gather):
    n = len(srcs)

    def body(*refs):
        srcs_in, lands_in = refs[:n], refs[n:2 * n]
        send, recv = refs[2 * n], refs[2 * n + 1]
        token = refs[-1]
        x, y, c, chips = _place()
        for k in range(n):
            for j, chip in enumerate(chips):
                _exchange_copy(k, j, chip, c, 2 * x + y, srcs_in, lands_in, send, recv, gather, False).start()
        token[...] = jnp.zeros_like(token)

    outs = pl.pallas_call(
        body, name=name,
        out_shape=(pltpu.SemaphoreType.DMA((3 * n,)), pltpu.SemaphoreType.DMA((3 * n,)),
                   *[pltpu.HBM(a.shape, a.dtype) for a in srcs], *[pltpu.HBM(a.shape, a.dtype) for a in lands],
                   jax.ShapeDtypeStruct((SUBLANES, LANES), F32)),
        in_specs=[HBM_SPEC] * (2 * n),
        out_specs=(SEM_SPEC, SEM_SPEC, *[HBM_SPEC] * (2 * n), pl.BlockSpec(memory_space=pltpu.VMEM)),
        input_output_aliases={k: 2 + k for k in range(2 * n)},
        compiler_params=pltpu.CompilerParams(has_side_effects=EFFECT),
    )(*[_in_hbm(a) for a in srcs], *[_in_hbm(a) for a in lands])
    return outs[0], outs[1], list(outs[2:2 + n]), list(outs[2 + n:2 + 2 * n]), outs[-1]


def exchange_wait(state, after, name, gather):
    send, recv, srcs, lands, _ = state
    n = len(srcs)

    def body(*refs):
        srcs_in, lands_in = refs[:n], refs[n:2 * n]
        send_, recv_ = refs[2 * n], refs[2 * n + 1]
        x, y, c, chips = _place()
        for k in range(n):
            for j, chip in enumerate(chips):
                cp = _exchange_copy(k, j, chip, c, 2 * x + y, srcs_in, lands_in, send_, recv_, gather, True)
                cp.wait_send()
                cp.wait_recv()

    outs = pl.pallas_call(
        body, name=name,
        out_shape=tuple(pltpu.HBM(a.shape, a.dtype) for a in srcs + lands),
        in_specs=[HBM_SPEC] * (2 * n) + [SEM_SPEC, SEM_SPEC, ANY],
        out_specs=tuple([HBM_SPEC] * (2 * n)),
        input_output_aliases={k: k for k in range(2 * n)},
        compiler_params=pltpu.CompilerParams(has_side_effects=EFFECT),
    )(*srcs, *lands, send, recv, after)
    return list(outs[:n]), list(outs[n:])


RS_SLOTS = 7


def _rs_copies(k, gs, lands, send, recv, x, y, c, chips):
    hf = gs[k].shape[1] // 2
    base = RS_SLOTS * k
    out = []
    for j, (cx, cy) in enumerate(chips):
        for cd in range(2):
            out.append(pltpu.make_async_remote_copy(
                src_ref=gs[k].at[2 * cx + cy, pl.ds(cd * hf, hf)], dst_ref=lands[k].at[2 * j + c],
                send_sem=send.at[base + 2 * j + cd], recv_sem=recv.at[base + 2 * j + c],
                device_id=(cx, cy, cd), device_id_type=MESH))
    out.append(pltpu.make_async_remote_copy(
        src_ref=gs[k].at[2 * x + y, pl.ds((1 - c) * hf, hf)], dst_ref=lands[k].at[RS_SLOTS - 1],
        send_sem=send.at[base + RS_SLOTS - 1], recv_sem=recv.at[base + RS_SLOTS - 1],
        device_id=(x, y, 1 - c), device_id_type=MESH))
    return out


def _rs_wait_all(n, gs, lands, send, recv):
    x, y, c, chips = _place()
    for k in range(n):
        for cp in _rs_copies(k, gs, lands, send, recv, x, y, c, chips):
            cp.wait_send()
        hf = gs[k].shape[1] // 2
        for slot in range(RS_SLOTS):
            pltpu.make_async_remote_copy(
                src_ref=gs[k].at[0, pl.ds(0, hf)], dst_ref=lands[k].at[slot],
                send_sem=send.at[RS_SLOTS * k + slot], recv_sem=recv.at[RS_SLOTS * k + slot],
                device_id=(x, y, 1 - c), device_id_type=MESH).wait_recv()


def _rs_land_shapes(gs):
    return [(RS_SLOTS, g.shape[1] // 2, g.shape[2]) for g in gs]


def rs_exchange(gs, name):
    n = len(gs)

    def body(*refs):
        ins, outs = refs[:n], refs[n:2 * n]
        send, recv = refs[2 * n:]
        x, y, c, chips = _place()
        for k in range(n):
            for cp in _rs_copies(k, ins, outs, send, recv, x, y, c, chips):
                cp.start()
        _rs_wait_all(n, ins, outs, send, recv)

    return pl.pallas_call(
        body, name=name,
        out_shape=[jax.ShapeDtypeStruct(s, F32) for s in _rs_land_shapes(gs)],
        in_specs=[ANY] * n, out_specs=[ANY] * n,
        scratch_shapes=[pltpu.SemaphoreType.DMA((RS_SLOTS * n,))] * 2,
    )(*gs)


def rs_exchange_start(gs, name):
    n = len(gs)
    lands = [lax.empty(s, F32) for s in _rs_land_shapes(gs)]

    def body(*refs):
        ins, lands_in = refs[:n], refs[n:2 * n]
        send, recv = refs[2 * n], refs[2 * n + 1]
        token = refs[-1]
        x, y, c, chips = _place()
        for k in range(n):
            for cp in _rs_copies(k, ins, lands_in, send, recv, x, y, c, chips):
                cp.start()
        token[...] = jnp.zeros_like(token)

    outs = pl.pallas_call(
        body, name=name,
        out_shape=(pltpu.SemaphoreType.DMA((RS_SLOTS * n,)), pltpu.SemaphoreType.DMA((RS_SLOTS * n,)),
                   *[pltpu.HBM(a.shape, a.dtype) for a in gs], *[pltpu.HBM(a.shape, a.dtype) for a in lands],
                   jax.ShapeDtypeStruct((SUBLANES, LANES), F32)),
        in_specs=[HBM_SPEC] * (2 * n),
        out_specs=(SEM_SPEC, SEM_SPEC, *[HBM_SPEC] * (2 * n), pl.BlockSpec(memory_space=pltpu.VMEM)),
        input_output_aliases={k: 2 + k for k in range(2 * n)},
        compiler_params=pltpu.CompilerParams(has_side_effects=EFFECT),
    )(*[_in_hbm(a) for a in gs], *[_in_hbm(a) for a in lands])
    return outs[0], outs[1], list(outs[2:2 + n]), list(outs[2 + n:2 + 2 * n]), outs[-1]


def rs_exchange_wait(state, after, name):
    send, recv, gs, lands, _ = state
    n = len(gs)

    def body(*refs):
        _rs_wait_all(n, refs[:n], refs[n:2 * n], refs[2 * n], refs[2 * n + 1])

    outs = pl.pallas_call(
        body, name=name,
        out_shape=tuple(pltpu.HBM(a.shape, a.dtype) for a in gs + lands),
        in_specs=[HBM_SPEC] * (2 * n) + [SEM_SPEC, SEM_SPEC, ANY],
        out_specs=tuple([HBM_SPEC] * (2 * n)),
        input_output_aliases={k: k for k in range(2 * n)},
        compiler_params=pltpu.CompilerParams(has_side_effects=EFFECT),
    )(*gs, *lands, send, recv, after)
    return list(outs[:n]), list(outs[n:])


def sum_eight(g, land, s_arr, c_arr, name):
    _, a, b = g.shape
    ah = a // 2
    tb = _tile(ah, 256)
    nb = ah // tb

    def body(s_ref, c_ref, g_ref, *rest):
        lands_, o_ref = rest[:RS_SLOTS], rest[RS_SLOTS]
        tot = g_ref[0] + lands_[RS_SLOTS - 1][0]
        for slot in range(RS_SLOTS - 1):
            tot = tot + lands_[slot][0]
        o_ref[...] = tot

    def lspec(slot):
        return pl.BlockSpec((1, tb, b), lambda i, s, c: (slot, i, 0))

    return pl.pallas_call(
        body, name=name,
        out_shape=jax.ShapeDtypeStruct((a, b), F32),
        grid_spec=pltpu.PrefetchScalarGridSpec(
            num_scalar_prefetch=2, grid=(nb,),
            in_specs=[pl.BlockSpec((1, tb, b), lambda i, s, c: (s[0], c[0] * nb + i, 0))]
            + [lspec(slot) for slot in range(RS_SLOTS)],
            out_specs=pl.BlockSpec((tb, b), lambda i, s, c: (c[0] * nb + i, 0))),
        compiler_params=_params(("arbitrary",)),
    )(s_arr, c_arr, g, *([land] * RS_SLOTS))


def rs_begin(gs, tag):
    return rs_exchange_start(gs, f"rs_start_{tag}")


def rs_finish(state, after, s_arr, c_arr, tag):
    gs, lands = rs_exchange_wait(state, after, f"rs_wait_{tag}")
    ts = [sum_eight(g, r, s_arr, c_arr, f"rs_sum_{tag}_{k}") for k, (g, r) in enumerate(zip(gs, lands))]
    return sibling_join_halves(ts, f"rs_join_{tag}")


def ffn_fwd(h, gamma, wg, wu, wd, name):
    r, d = h.shape
    ns, fs, _ = wg.shape
    tm = _row_tile(r)
    sb = _tile(ns, FFN_FWD_SHARDS, 1)
    nsteps = ns // sb

    def body(h_ref, g_ref, wg_ref, wu_ref, wd_ref, ho_ref, gg_ref, uu_ref, hn_sc, acc_sc):
        s = pl.program_id(1)

        @pl.when(s == 0)
        def _():
            xh, _ = _rms(h_ref[...])
            hn_sc[...] = (xh * g_ref[...]).astype(BF16)
            acc_sc[...] = jnp.zeros_like(acc_sc)

        hn = hn_sc[...]
        part = None
        for t in range(sb):
            g = lax.dot_general(hn, wg_ref[t], NT, preferred_element_type=F32)
            u = lax.dot_general(hn, wu_ref[t], NT, preferred_element_type=F32)
            gg_ref[t] = g.astype(BF16)
            uu_ref[t] = u.astype(BF16)
            a = (g * _sigmoid(g) * u).astype(BF16)
            p = jnp.dot(a, wd_ref[t], preferred_element_type=F32)
            part = p if part is None else part + p
        acc_sc[...] += part

        @pl.when(s == nsteps - 1)
        def _():
            ho_ref[...] = h_ref[...] + 0.5 * acc_sc[...]

    wsp = pl.BlockSpec((sb, fs, d), lambda i, s: (s, 0, 0))
    act = pl.BlockSpec((sb, tm, fs), lambda i, s: (s, i, 0))
    return pl.pallas_call(
        body, name=name,
        out_shape=[jax.ShapeDtypeStruct((r, d), F32), jax.ShapeDtypeStruct((ns, r, fs), BF16),
                   jax.ShapeDtypeStruct((ns, r, fs), BF16)],
        grid=(r // tm, nsteps),
        in_specs=[pl.BlockSpec((tm, d), lambda i, s: (i, 0)), pl.BlockSpec((1, d), lambda i, s: (0, 0)), wsp, wsp, wsp],
        out_specs=[pl.BlockSpec((tm, d), lambda i, s: (i, 0)), act, act],
        scratch_shapes=[pltpu.VMEM((tm, d), BF16), pltpu.VMEM((tm, d), F32)],
        compiler_params=_params(("arbitrary", "arbitrary"), VMEM_MAX),
    )(h, gamma, wg, wu, wd)


def ffn_bwd_act(h, gamma, dh, gg, uu, wg, wu, wd, name):
    r, d = h.shape
    ns, fs, _ = wg.shape
    tm = _row_tile(r) // FFN_BWD_ROW_SPLIT
    sb = _tile(ns, FFN_BWD_SHARDS, 1)
    nsteps = ns // sb

    def body(h_ref, g_ref, dh_ref, gg_ref, uu_ref, wg_ref, wu_ref, wd_ref,
             dho_ref, dgam_ref, hn_ref, dy_ref, dg_ref, du_ref, a_ref, acc_sc):
        i, s = pl.program_id(0), pl.program_id(1)

        @pl.when(s == 0)
        def _():
            xh, _ = _rms(h_ref[...])
            hn_ref[...] = (xh * g_ref[...]).astype(BF16)
            dy_ref[...] = (0.5 * dh_ref[...]).astype(BF16)
            acc_sc[...] = jnp.zeros_like(acc_sc)

        @pl.when((i == 0) & (s == 0))
        def _():
            dgam_ref[...] = jnp.zeros_like(dgam_ref)

        part = None
        for t in range(sb):
            g = gg_ref[t].astype(F32)
            u = uu_ref[t].astype(F32)
            da = lax.dot_general(dy_ref[...], wd_ref[t], NT, preferred_element_type=F32)
            sig = _sigmoid(g)
            sl = g * sig
            a_ref[t] = (sl * u).astype(BF16)
            du = (da * sl).astype(BF16)
            dg = (da * u * (sig * (1.0 + g * (1.0 - sig)))).astype(BF16)
            dg_ref[t] = dg
            du_ref[t] = du
            p = (jnp.dot(dg, wg_ref[t], preferred_element_type=F32)
                 + jnp.dot(du, wu_ref[t], preferred_element_type=F32))
            part = p if part is None else part + p
        acc_sc[...] += part

        @pl.when(s == nsteps - 1)
        def _():
            xh, rstd = _rms(h_ref[...])
            dhn = acc_sc[...]
            dgam_ref[...] += jnp.sum(dhn * xh, axis=0, keepdims=True)
            dho_ref[...] = dh_ref[...] + _rms_bwd(xh, rstd, dhn * g_ref[...])

    row = pl.BlockSpec((tm, d), lambda i, s: (i, 0))
    act = pl.BlockSpec((sb, tm, fs), lambda i, s: (s, i, 0))
    wsp = pl.BlockSpec((sb, fs, d), lambda i, s: (s, 0, 0))
    return pl.pallas_call(
        body, name=name,
        out_shape=[jax.ShapeDtypeStruct((r, d), F32), jax.ShapeDtypeStruct((1, d), F32),
                   jax.ShapeDtypeStruct((r, d), BF16), jax.ShapeDtypeStruct((r, d), BF16),
                   jax.ShapeDtypeStruct((ns, r, fs), BF16), jax.ShapeDtypeStruct((ns, r, fs), BF16),
                   jax.ShapeDtypeStruct((ns, r, fs), BF16)],
        grid=(r // tm, nsteps),
        in_specs=[row, pl.BlockSpec((1, d), lambda i, s: (0, 0)), row, act, act, wsp, wsp, wsp],
        out_specs=[row, pl.BlockSpec((1, d), lambda i, s: (0, 0)), row, row, act, act, act],
        scratch_shapes=[pltpu.VMEM((tm, d), F32)],
        compiler_params=_params(("arbitrary", "arbitrary"), VMEM_MAX),
    )(h, gamma, dh, gg, uu, wg, wu, wd)


def ffn_bwd_weights(hn, dy, a, dg, du, name):
    r, d = hn.shape
    ns, _, fs = a.shape
    tm = WGRAD_ROWS if r % WGRAD_ROWS == 0 else _row_tile(r)

    def body(hn_ref, dy_ref, a_ref, dg_ref, du_ref, wg_ref, wu_ref, wd_ref):
        @pl.when(pl.program_id(1) == 0)
        def _():
            wg_ref[...] = jnp.zeros_like(wg_ref)
            wu_ref[...] = jnp.zeros_like(wu_ref)
            wd_ref[...] = jnp.zeros_like(wd_ref)

        hn_ = hn_ref[...]
        wg_ref[0] += lax.dot_general(dg_ref[0], hn_, TN, preferred_element_type=F32)
        wu_ref[0] += lax.dot_general(du_ref[0], hn_, TN, preferred_element_type=F32)
        wd_ref[0] += lax.dot_general(a_ref[0], dy_ref[...], TN, preferred_element_type=F32)

    row = pl.BlockSpec((tm, d), lambda s, i: (i, 0))
    act = pl.BlockSpec((1, tm, fs), lambda s, i: (s, i, 0))
    wsp = pl.BlockSpec((1, fs, d), lambda s, i: (s, 0, 0))
    return pl.pallas_call(
        body, name=name,
        out_shape=[jax.ShapeDtypeStruct((ns, fs, d), F32)] * 3,
        grid=(ns, r // tm),
        in_specs=[row, row, act, act, act],
        out_specs=[wsp, wsp, wsp],
        compiler_params=_params(("arbitrary", "arbitrary"), VMEM_MAX),
    )(hn, dy, a, dg, du)


def norm_fwd(x, gamma, name):
    r = x.shape[0]
    w = gamma.shape[1]
    tm = _row_tile(r)

    def body(x_ref, g_ref, o_ref):
        xh, _ = _rms(x_ref[...])
        o_ref[...] = (xh * g_ref[...]).astype(BF16)

    return pl.pallas_call(
        body, name=name, out_shape=jax.ShapeDtypeStruct((r, w), BF16), grid=(r // tm,),
        in_specs=[pl.BlockSpec((tm, w), lambda i: (i, 0)), pl.BlockSpec((1, w), lambda i: (0, 0))],
        out_specs=pl.BlockSpec((tm, w), lambda i: (i, 0)),
        compiler_params=_params(("arbitrary",)),
    )(x, gamma)


def norm_bwd(x, gamma, dy, dres, name):
    r = x.shape[0]
    w = gamma.shape[1]
    tm = _row_tile(r)
    has_res = dres is not None

    def body(*refs):
        if has_res:
            x_ref, g_ref, dy_ref, dr_ref, dx_ref, dgam_ref = refs
        else:
            x_ref, g_ref, dy_ref, dx_ref, dgam_ref = refs

        @pl.when(pl.program_id(0) == 0)
        def _():
            dgam_ref[...] = jnp.zeros_like(dgam_ref)

        xh, rstd = _rms(x_ref[...])
        dy_ = dy_ref[...].astype(F32)
        dgam_ref[...] += jnp.sum(dy_ * xh, axis=0, keepdims=True)
        dx = _rms_bwd(xh, rstd, dy_ * g_ref[...])
        if has_res:
            dx = dx + dr_ref[...]
        dx_ref[...] = dx

    row = pl.BlockSpec((tm, w), lambda i: (i, 0))
    vec = pl.BlockSpec((1, w), lambda i: (0, 0))
    ins = [x, gamma, dy] + ([dres] if has_res else [])
    return pl.pallas_call(
        body, name=name,
        out_shape=[jax.ShapeDtypeStruct((r, w), F32), jax.ShapeDtypeStruct((1, w), F32)],
        grid=(r // tm,), in_specs=[row, vec, row] + ([row] if has_res else []), out_specs=[row, vec],
        compiler_params=_params(("arbitrary",)),
    )(*ins)


def rowmm(a, w, name, *, nt=False, res=None, out_dtype=F32, heads_out=False):
    ha, r, ka = a.shape
    hw = w.shape[0]
    nh = max(ha, hw)
    n = w.shape[1] if nt else w.shape[2]
    tm = _row_tile(r)
    dims = NT if nt else (((1,), (0,)), ((), ()))
    has_res = res is not None

    def body(*refs):
        if has_res:
            a_ref, w_ref, r_ref, o_ref = refs
        else:
            a_ref, w_ref, o_ref = refs
        shared = a_ref[0].astype(BF16) if ha == 1 else None
        acc = None
        for h in range(nh):
            lhs = shared if ha == 1 else a_ref[h].astype(BF16)
            p = lax.dot_general(lhs, w_ref[h if hw > 1 else 0], dims, preferred_element_type=F32)
            if heads_out:
                o_ref[h] = p.astype(out_dtype)
            else:
                acc = p if acc is None else acc + p
        if not heads_out:
            if has_res:
                acc = acc + r_ref[...]
            o_ref[...] = acc.astype(out_dtype)

    in_specs = [pl.BlockSpec((ha, tm, ka), lambda i: (0, i, 0)), pl.BlockSpec(w.shape, lambda i: (0, 0, 0))]
    ins = [a, w]
    if has_res:
        in_specs.append(pl.BlockSpec((tm, n), lambda i: (i, 0)))
        ins.append(res)
    if heads_out:
        out_shape = jax.ShapeDtypeStruct((nh, r, n), out_dtype)
        out_spec = pl.BlockSpec((nh, tm, n), lambda i: (0, i, 0))
    else:
        out_shape = jax.ShapeDtypeStruct((r, n), out_dtype)
        out_spec = pl.BlockSpec((tm, n), lambda i: (i, 0))
    return pl.pallas_call(
        body, name=name, out_shape=out_shape, grid=(r // tm,), in_specs=in_specs, out_specs=out_spec,
        compiler_params=_params(("arbitrary",), VMEM_BIG),
    )(*ins)


def tnmm(a, b, name):
    ha, r, ka = a.shape
    hb, _, nb = b.shape
    nh = max(ha, hb)
    tm = _row_tile(r)

    def body(a_ref, b_ref, o_ref):
        @pl.when(pl.program_id(0) == 0)
        def _():
            o_ref[...] = jnp.zeros_like(o_ref)

        a_shared = a_ref[0].astype(BF16) if ha == 1 else None
        b_shared = b_ref[0].astype(BF16) if hb == 1 else None
        for h in range(nh):
            lhs = a_shared if ha == 1 else a_ref[h].astype(BF16)
            rhs = b_shared if hb == 1 else b_ref[h].astype(BF16)
            o_ref[h] += lax.dot_general(lhs, rhs, TN, preferred_element_type=F32)

    return pl.pallas_call(
        body, name=name, out_shape=jax.ShapeDtypeStruct((nh, ka, nb), F32), grid=(r // tm,),
        in_specs=[pl.BlockSpec((ha, tm, ka), lambda i: (0, i, 0)), pl.BlockSpec((hb, tm, nb), lambda i: (0, i, 0))],
        out_specs=pl.BlockSpec((nh, ka, nb), lambda i: (0, 0, 0)),
        compiler_params=_params(("arbitrary",), VMEM_BIG),
    )(a, b)


def rope_tables(r):
    inv = 1.0 / (ROPE_THETA ** (jnp.arange(0, QK_ROPE, 2, dtype=F32) / QK_ROPE))
    pos = (jnp.arange(r, dtype=F32) - META_ROW0)[:, None]
    ang = pos * inv[None, :]
    cos, sin = jnp.cos(ang), jnp.sin(ang)
    ones = jnp.ones((r, HEAD_W - QK_ROPE), F32)
    ctab = jnp.concatenate([cos, cos, ones], axis=1)
    stab = jnp.concatenate([-sin, sin, jnp.zeros_like(ones)], axis=1)
    return ctab, stab


def _swap_halves(z):
    lane = lax.broadcasted_iota(jnp.int32, z.shape, 1)
    up = pltpu.roll(z, HEAD_W - QK_ROPE // 2, 1)
    down = pltpu.roll(z, QK_ROPE // 2, 1)
    return jnp.where(lane < QK_ROPE // 2, up, jnp.where(lane < QK_ROPE, down, 0.0))


def proj_rope(a, w, ctab, stab, extra, name, scale=1.0):
    r, ka = a.shape
    nh = w.shape[0]
    tm = _row_tile(r)
    has_extra = extra is not None

    def body(*refs):
        if has_extra:
            a_ref, w_ref, c_ref, s_ref, e_ref, o_ref = refs
        else:
            a_ref, w_ref, c_ref, s_ref, o_ref = refs
        a_ = a_ref[...]
        ctab_, stab_ = c_ref[...], s_ref[...]
        if scale != 1.0:
            ctab_, stab_ = ctab_ * scale, stab_ * scale
        for h in range(nh):
            x = jnp.dot(a_, w_ref[h], preferred_element_type=F32)
            if has_extra:
                x = x + e_ref[...]
            o_ref[h] = (x * ctab_ + _swap_halves(x) * stab_).astype(BF16)

    tab = pl.BlockSpec((tm, HEAD_W), lambda i: (i, 0))
    in_specs = [pl.BlockSpec((tm, ka), lambda i: (i, 0)), pl.BlockSpec((nh, ka, HEAD_W), lambda i: (0, 0, 0)),
                tab, tab]
    ins = [a, w, ctab, stab]
    if has_extra:
        in_specs.append(pl.BlockSpec((tm, HEAD_W), lambda i: (i, 2)))
        ins.append(extra)
    return pl.pallas_call(
        body, name=name, out_shape=jax.ShapeDtypeStruct((nh, r, HEAD_W), BF16), grid=(r // tm,),
        in_specs=in_specs, out_specs=pl.BlockSpec((nh, tm, HEAD_W), lambda i: (0, i, 0)),
        compiler_params=_params(("arbitrary",)),
    )(*ins)


def rope_bwd_heads(d, ctab, stab, name, scale=1.0):
    nh, r, _ = d.shape
    tm = _row_tile(r)

    def body(d_ref, c_ref, s_ref, o_ref):
        ctab_, stab_ = c_ref[...], s_ref[...]
        if scale != 1.0:
            ctab_, stab_ = ctab_ * scale, stab_ * scale
        for h in range(nh):
            d_ = d_ref[h]
            o_ref[h] = (d_ * ctab_ + _swap_halves(d_ * stab_)).astype(BF16)

    tab = pl.BlockSpec((tm, HEAD_W), lambda i: (i, 0))
    blk = pl.BlockSpec((nh, tm, HEAD_W), lambda i: (0, i, 0))
    return pl.pallas_call(
        body, name=name, out_shape=jax.ShapeDtypeStruct((nh, r, HEAD_W), BF16), grid=(r // tm,),
        in_specs=[blk, tab, tab], out_specs=blk,
        compiler_params=_params(("arbitrary",)),
    )(d, ctab, stab)


def rope_bwd_sum(d, ctab, stab, name):
    nh, r, _ = d.shape
    tm = _row_tile(r)

    def body(d_ref, c_ref, s_ref, o_ref):
        d_ = d_ref[0]
        for h in range(1, nh):
            d_ = d_ + d_ref[h]
        lane = lax.broadcasted_iota(jnp.int32, d_.shape, 1)
        g = d_ * c_ref[...] + _swap_halves(d_ * s_ref[...])
        o_ref[...] = jnp.where(lane < QK_ROPE, g, 0.0)

    tab = pl.BlockSpec((tm, HEAD_W), lambda i: (i, 0))
    return pl.pallas_call(
        body, name=name, out_shape=jax.ShapeDtypeStruct((r, HEAD_W), F32), grid=(r // tm,),
        in_specs=[pl.BlockSpec((nh, tm, HEAD_W), lambda i: (0, i, 0)), tab, tab], out_specs=tab,
        compiler_params=_params(("arbitrary",)),
    )(d, ctab, stab)


def _attn_tiles(r):
    t = _row_tile(r)
    return t, t


def _mask(q0, k0, nq_, nk_, keys_on_rows=False):
    shape = (nk_, nq_) if keys_on_rows else (nq_, nk_)
    rq = q0 + lax.broadcasted_iota(jnp.int32, shape, 1 if keys_on_rows else 0)
    rk = k0 + lax.broadcasted_iota(jnp.int32, shape, 0 if keys_on_rows else 1)
    return ((rk >> CHUNK_SHIFT) <= (rq >> CHUNK_SHIFT)) & (rk >= META_ROW0)


ATTN_FWD_HEADS = 8
ATTN_FWD_UNROLL = 4
ATTN_BWD_HEADS = 4


SM_SCALE = 1.0 / math.sqrt(QK_NOPE + QK_ROPE)
LOG2E = math.log2(math.e)
Q_SCALE = SM_SCALE * LOG2E


def attn_fwd(q, k, v, name):
    nh, r, dk = q.shape
    dv = v.shape[-1]
    tq, tk = _attn_tiles(r)
    nq, nk = r // tq, r // tk

    def last_k(i):
        return ((i + 1) * tq - 1) // tk

    pairs = [(i, j) for i in range(nq) for j in range(last_k(i) + 1)]
    qi_tab = jnp.asarray([p[0] for p in pairs], jnp.int32)
    kj_tab = jnp.asarray([p[1] for p in pairs], jnp.int32)
    hb = _tile(nh, ATTN_FWD_HEADS, 1)
    hu = _tile(hb, ATTN_FWD_UNROLL, 1)

    def body(qi_ref, kj_ref, q_ref, k_ref, v_ref, o_ref, lse_ref, m_sc, l_sc, acc_sc):
        t = pl.program_id(1)
        i, j = qi_ref[t], kj_ref[t]

        @pl.when(j == 0)
        def _():
            m_sc[...] = jnp.full_like(m_sc, NEG)
            l_sc[...] = jnp.zeros_like(l_sc)
            acc_sc[...] = jnp.zeros_like(acc_sc)

        def step(masked):
            def one_head(hh):
                s = lax.dot_general(q_ref[hh], k_ref[hh], NT, preferred_element_type=F32)
                if masked:
                    s = jnp.where(_mask(i * tq, j * tk, tq, tk), s, NEG)
                m_old = m_sc[hh]
                m_new = jnp.maximum(m_old, jnp.max(s, axis=-1, keepdims=True))
                alpha = jnp.exp2(m_old - m_new)
                p = jnp.exp2(s - jnp.tile(m_new, (1, tk // LANES)))
                l_sc[hh] = alpha * l_sc[hh] + jnp.sum(p, axis=-1, keepdims=True)
                acc_sc[hh] = (alpha[:, :dv] * acc_sc[hh]
                              + jnp.dot(p.astype(BF16), v_ref[hh], preferred_element_type=F32))
                m_sc[hh] = m_new

            def head_group(g, carry):
                for u in range(hu):
                    one_head(g * hu + u)
                return carry

            lax.fori_loop(0, hb // hu, head_group, 0)

        needs_mask = (j == last_k(i)) | (j == 0)
        pl.when(needs_mask)(functools.partial(step, True))
        pl.when(jnp.logical_not(needs_mask))(functools.partial(step, False))

        @pl.when(j == last_k(i))
        def _():
            def one_head(hh, carry):
                l = l_sc[hh]
                o_ref[hh] = (acc_sc[hh] / l[:, :dv]).astype(BF16)
                lse_ref[hh] = (m_sc[hh] + jnp.log2(l))[:, :1]
                return carry

            lax.fori_loop(0, hb, one_head, 0)

    qspec = lambda w: pl.BlockSpec((hb, tq, w), lambda h, t, qi, kj: (h, qi[t], 0))
    kspec = lambda w: pl.BlockSpec((hb, tk, w), lambda h, t, qi, kj: (h, kj[t], 0))
    return pl.pallas_call(
        body, name=name,
        out_shape=[jax.ShapeDtypeStruct((nh, r, dv), BF16), jax.ShapeDtypeStruct((nh, r, 1), F32)],
        grid_spec=pltpu.PrefetchScalarGridSpec(
            num_scalar_prefetch=2, grid=(nh // hb, len(pairs)),
            in_specs=[qspec(dk), kspec(dk), kspec(dv)], out_specs=[qspec(dv), qspec(1)],
            scratch_shapes=[pltpu.VMEM((hb, tq, LANES), F32), pltpu.VMEM((hb, tq, LANES), F32),
                            pltpu.VMEM((hb, tq, dv), F32)]),
        compiler_params=_params(("arbitrary", "arbitrary"), VMEM_BIG),
    )(qi_tab, kj_tab, q, k, v)


def attn_bwd(q, k, v, do, lse, delta, dk_prev, dv_prev, name):
    nh, r, dk = q.shape
    dv = v.shape[-1]
    tq, tk = _attn_tiles(r)
    nq, nk = r // tq, r // tk
    has_prev = dk_prev is not None

    def first_q(j):
        return (j * tk) // tq

    pairs = [(j, i) for j in range(nk) for i in range(first_q(j), nq)]
    kj_tab = jnp.asarray([p[0] for p in pairs], jnp.int32)
    qi_tab = jnp.asarray([p[1] for p in pairs], jnp.int32)
    hb = _tile(nh, ATTN_BWD_HEADS, 1)

    def body(*refs):
        if has_prev:
            (kj_ref, qi_ref, q_ref, k_ref, v_ref, do_ref, lse_ref, dl_ref, pk_ref, pv_ref,
             dq_ref, dk_ref, dv_ref, dk_sc, dv_sc) = refs
        else:
            (kj_ref, qi_ref, q_ref, k_ref, v_ref, do_ref, lse_ref, dl_ref,
             dq_ref, dk_ref, dv_ref, dk_sc, dv_sc) = refs
        t = pl.program_id(1)
        j, i = kj_ref[t], qi_ref[t]

        @pl.when(t == 0)
        def _():
            dq_ref[...] = jnp.zeros_like(dq_ref)

        @pl.when(i == first_q(j))
        def _():
            dk_sc[...] = jnp.zeros_like(dk_sc)
            dv_sc[...] = jnp.zeros_like(dv_sc)

        def step(masked):
            qrows = pl.ds(pl.multiple_of(i * tq, tq), tq)
            for hh in range(hb):
                q_, k_, do_ = q_ref[hh], k_ref[hh], do_ref[hh]
                st = lax.dot_general(k_, q_, NT, preferred_element_type=F32)
                if masked:
                    st = jnp.where(_mask(i * tq, j * tk, tq, tk, keys_on_rows=True), st, NEG)
                pt = jnp.exp2(st - lse_ref[hh])
                dv_sc[hh] += jnp.dot(pt.astype(BF16), do_, preferred_element_type=F32)
                dpt = lax.dot_general(v_ref[hh], do_, NT, preferred_element_type=F32)
                dst = (pt * (dpt - dl_ref[hh])).astype(BF16)
                dk_sc[hh] += jnp.dot(dst, q_, preferred_element_type=F32)
                dq_ref[hh, qrows, :] += lax.dot_general(dst, k_, TN, preferred_element_type=F32)

        needs_mask = (i == first_q(j)) | (j == 0)
        pl.when(needs_mask)(functools.partial(step, True))
        pl.when(jnp.logical_not(needs_mask))(functools.partial(step, False))

        @pl.when(i == nq - 1)
        def _():
            dk_ = dk_sc[...] * (1.0 / LOG2E)
            dv_ = dv_sc[...]
            if has_prev:
                dk_ = dk_ + pk_ref[...]
                dv_ = dv_ + pv_ref[...]
            dk_ref[...] = dk_
            dv_ref[...] = dv_

    krow = lambda w: pl.BlockSpec((hb, tk, w), lambda h, t, kj, qi: (h, kj[t], 0))
    qrow = lambda w: pl.BlockSpec((hb, tq, w), lambda h, t, kj, qi: (h, qi[t], 0))
    qvec = pl.BlockSpec((hb, 1, tq), lambda h, t, kj, qi: (h, 0, qi[t]))
    in_specs = [qrow(dk), krow(dk), krow(dv), qrow(dv), qvec, qvec]
    ins = [q, k, v, do, lse, delta]
    if has_prev:
        in_specs += [krow(dk), krow(dv)]
        ins += [dk_prev, dv_prev]
    return pl.pallas_call(
        body, name=name,
        out_shape=[jax.ShapeDtypeStruct((nh, r, dk), F32), jax.ShapeDtypeStruct((nh, r, dk), F32),
                   jax.ShapeDtypeStruct((nh, r, dv), F32)],
        grid_spec=pltpu.PrefetchScalarGridSpec(
            num_scalar_prefetch=2, grid=(nh // hb, len(pairs)), in_specs=in_specs,
            out_specs=[pl.BlockSpec((hb, r, dk), lambda h, t, kj, qi: (h, 0, 0)), krow(dk), krow(dv)],
            scratch_shapes=[pltpu.VMEM((hb, tk, dk), F32), pltpu.VMEM((hb, tk, dv), F32)]),
        compiler_params=_params(("arbitrary", "arbitrary"), VMEM_MAX),
    )(kj_tab, qi_tab, *ins)


def attn_out_bwd(dattn, wo, o, name):
    r, d = dattn.shape
    nh, dv, _ = wo.shape
    tm = _row_tile(r)

    def body(da_ref, w_ref, o_ref, do_ref, dl_ref):
        da = da_ref[...].astype(BF16)
        for h in range(nh):
            do_ = lax.dot_general(da, w_ref[h], NT, preferred_element_type=F32).astype(BF16)
            do_ref[h] = do_
            dl_ref[h] = jnp.sum(do_.astype(F32) * o_ref[h].astype(F32), axis=-1, keepdims=True)

    return pl.pallas_call(
        body, name=name,
        out_shape=[jax.ShapeDtypeStruct((nh, r, dv), BF16), jax.ShapeDtypeStruct((nh, r, 1), F32)],
        grid=(r // tm,),
        in_specs=[pl.BlockSpec((tm, d), lambda i: (i, 0)), pl.BlockSpec((nh, dv, d), lambda i: (0, 0, 0)),
                  pl.BlockSpec((nh, tm, dv), lambda i: (0, i, 0))],
        out_specs=[pl.BlockSpec((nh, tm, dv), lambda i: (0, i, 0)), pl.BlockSpec((nh, tm, 1), lambda i: (0, i, 0))],
        compiler_params=_params(("arbitrary",)),
    )(dattn, wo, o)


def _pool_counts(row0, n, window):
    rows = row0 + lax.broadcasted_iota(jnp.int32, (n, 1), 0)
    cnt = jnp.clip(rows - META_ROW0 + 1, 1, window)
    return 1.0 / cnt.astype(F32)


def pool_fwd(h, gamma, wp, scale, name):
    r, d = h.shape
    ng, cg, _ = wp.shape
    tm = _row_tile(r)
    hb = tm // HALO

    def body(h_ref, hp_ref, g_ref, w_ref, sc_ref, o_ref):
        i = pl.program_id(0)
        xm = h_ref[...]
        xp = hp_ref[...] * jnp.where(i > 0, 1.0, 0.0)
        xx = jnp.concatenate([xp, xm], axis=0)
        xh, _ = _rms(xx)
        u = xh * g_ref[...]
        for g, win in enumerate(POOL_WINDOWS):
            sl = slice(g * cg, (g + 1) * cg)
            ug = u[:, sl]
            acc, k = ug, 1
            while k < win:
                acc = acc + pltpu.roll(acc, k, 0)
                k *= 2
            pooled = acc[HALO:] * _pool_counts(i * tm, tm, win) - ug[HALO:]
            y = jnp.dot(pooled.astype(BF16), w_ref[g], preferred_element_type=F32)
            o_ref[:, sl] = xm[:, sl] + y * sc_ref[:, sl]

    return pl.pallas_call(
        body, name=name, out_shape=jax.ShapeDtypeStruct((r, d), F32), grid=(r // tm,),
        in_specs=[pl.BlockSpec((tm, d), lambda i: (i, 0)),
                  pl.BlockSpec((HALO, d), lambda i: (jnp.maximum(i * hb - 1, 0), 0)),
                  pl.BlockSpec((1, d), lambda i: (0, 0)), pl.BlockSpec((ng, cg, cg), lambda i: (0, 0, 0)),
                  pl.BlockSpec((1, d), lambda i: (0, 0))],
        out_specs=pl.BlockSpec((tm, d), lambda i: (i, 0)),
        compiler_params=_params(("arbitrary",), VMEM_BIG),
    )(h, h, gamma, wp, scale)


def pool_bwd(h, gamma, wp, scale, dh, name):
    r, d = h.shape
    ng, cg, _ = wp.shape
    tm = _row_tile(r)
    hb = tm // HALO
    nt = r // tm

    def body(h_ref, hp_ref, dh_ref, dn_ref, g_ref, w_ref, sc_ref, dx_ref, dgam_ref, dw_ref, dsc_ref, du_sc):
        i = pl.program_id(0)

        @pl.when(i == 0)
        def _():
            dgam_ref[...] = jnp.zeros_like(dgam_ref)
            dw_ref[...] = jnp.zeros_like(dw_ref)
            dsc_ref[...] = jnp.zeros_like(dsc_ref)

        xm = h_ref[...]
        xp = hp_ref[...] * jnp.where(i > 0, 1.0, 0.0)
        xh_all, rstd_all = _rms(jnp.concatenate([xp, xm], axis=0))
        u = xh_all * g_ref[...]
        dm = dh_ref[...]
        dn = dn_ref[...] * jnp.where(i < nt - 1, 1.0, 0.0)
        dd = jnp.concatenate([dm, dn], axis=0)
        for g, win in enumerate(POOL_WINDOWS):
            sl = slice(g * cg, (g + 1) * cg)
            ug = u[:, sl]
            acc, k = ug, 1
            while k < win:
                acc = acc + pltpu.roll(acc, k, 0)
                k *= 2
            pooled = (acc[HALO:] * _pool_counts(i * tm, tm, win) - ug[HALO:]).astype(BF16)
            y = jnp.dot(pooled, w_ref[g], preferred_element_type=F32)
            dsc_ref[:, sl] += jnp.sum(dm[:, sl] * y, axis=0, keepdims=True)
            dyp = (dd[:, sl] * sc_ref[:, sl]).astype(BF16)
            dw_ref[g] += lax.dot_general(pooled, dyp[:tm], TN, preferred_element_type=F32)
            dpo = lax.dot_general(dyp, w_ref[g], NT, preferred_element_type=F32)
            z = dpo * _pool_counts(i * tm, tm + HALO, win)
            fwd, k = z, 1
            while k < win:
                fwd = fwd + pltpu.roll(fwd, tm + HALO - k, 0)
                k *= 2
            du_sc[:, sl] = fwd[:tm] - dpo[:tm]
        du = du_sc[...]
        xh, rstd = xh_all[HALO:], rstd_all[HALO:]
        dgam_ref[...] += jnp.sum(du * xh, axis=0, keepdims=True)
        dx_ref[...] = dm + _rms_bwd(xh, rstd, du * g_ref[...])

    row = pl.BlockSpec((tm, d), lambda i: (i, 0))
    vec = pl.BlockSpec((1, d), lambda i: (0, 0))
    prev = pl.BlockSpec((HALO, d), lambda i: (jnp.maximum(i * hb - 1, 0), 0))
    nxt = pl.BlockSpec((HALO, d), lambda i: (jnp.minimum((i + 1) * hb, r // HALO - 1), 0))
    wsp = pl.BlockSpec((ng, cg, cg), lambda i: (0, 0, 0))
    return pl.pallas_call(
        body, name=name,
        out_shape=[jax.ShapeDtypeStruct((r, d), F32), jax.ShapeDtypeStruct((1, d), F32),
                   jax.ShapeDtypeStruct((ng, cg, cg), F32), jax.ShapeDtypeStruct((1, d), F32)],
        grid=(nt,), in_specs=[row, prev, row, nxt, vec, wsp, vec], out_specs=[row, vec, wsp, vec],
        scratch_shapes=[pltpu.VMEM((tm, d), F32)],
        compiler_params=_params(("arbitrary",), VMEM_BIG),
    )(h, h, dh, dh, gamma, wp, scale)


def loss_head(h, gamma, target, seq, name):
    r, d = h.shape
    tm = _row_tile(r)

    def body(h_ref, g_ref, t_ref, sse_ref, dh_ref, dgam_ref):
        i = pl.program_id(0)

        @pl.when(i == 0)
        def _():
            sse_ref[...] = jnp.zeros_like(sse_ref)
            dgam_ref[...] = jnp.zeros_like(dgam_ref)

        xh, rstd = _rms(h_ref[...])
        rows = i * tm + lax.broadcasted_iota(jnp.int32, (tm, 1), 0)
        valid = ((rows >= FRONT) & (rows < FRONT + seq)).astype(F32)
        e = (xh * g_ref[...] - t_ref[...]) * valid
        sse_ref[...] += jnp.sum(jnp.sum(e * e, axis=1, keepdims=True), axis=0, keepdims=True)
        dy = e * (1.0 / d)
        dgam_ref[...] += jnp.sum(dy * xh, axis=0, keepdims=True)
        dh_ref[...] = _rms_bwd(xh, rstd, dy * g_ref[...])

    row = pl.BlockSpec((tm, d), lambda i: (i, 0))
    vec = pl.BlockSpec((1, d), lambda i: (0, 0))
    return pl.pallas_call(
        body, name=name,
        out_shape=[jax.ShapeDtypeStruct((1, 1), F32), jax.ShapeDtypeStruct((r, d), F32),
                   jax.ShapeDtypeStruct((1, d), F32)],
        grid=(r // tm,), in_specs=[row, vec, row],
        out_specs=[pl.BlockSpec((1, 1), lambda i: (0, 0)), row, vec],
        compiler_params=_params(("arbitrary",)),
    )(h, gamma, target)


SMALL = ("meta_tokens", "pool_w", "pool_scale", "w_dkv", "w_uk", "w_uv", "w_dq", "w_uq", "w_o")


def _pack(parts):
    flat = jnp.concatenate([p.reshape(-1) for p in parts])
    n = flat.shape[0]
    unit = PACK_W * 2 * SUBLANES
    n_pad = -(-n // unit) * unit
    return jnp.pad(flat, (0, n_pad - n)).reshape(n_pad // PACK_W, PACK_W)


def _unpack(buf, shapes, lead=()):
    flat = buf.reshape(lead + (-1,))
    out, off = [], 0
    for shp in shapes:
        n = math.prod(shp)
        out.append(flat[..., off:off + n].reshape(lead + tuple(shp)))
        off += n
    return out


def _cols_from_shards(a, axis):
    a = jnp.moveaxis(a, 0, axis)
    shp = a.shape
    return a.reshape(shp[:axis] + (shp[axis] * shp[axis + 1],) + shp[axis + 2:])


def _cols_to_shards(a, axis):
    shp = a.shape
    a = a.reshape(shp[:axis] + (N_SHARD, shp[axis] // N_SHARD) + shp[axis + 1:])
    return jnp.moveaxis(a, axis, 0)


SMALL_AXIS = {"meta_tokens": 1, "pool_w": 2, "pool_scale": 1, "w_dkv": 0, "w_uk": 1, "w_uv": 1,
              "w_dq": 1, "w_uq": 2, "w_o": 2}


def kernel(x, meta_tokens, ffn1_norm, ffn1_w_gate, ffn1_w_up, ffn1_w_down, mix_norm, ffn2_norm, ffn2_w_gate, ffn2_w_up, ffn2_w_down, pool_w, pool_scale, kv_in_norm, w_dkv, kv_latent_norm, w_uk, w_uv, w_dq, q_latent_norm, w_uq, w_o, final_norm, loss_target, m_meta_tokens, m_ffn1_norm, m_ffn1_w_gate, m_ffn1_w_up, m_ffn1_w_down, m_mix_norm, m_ffn2_norm, m_ffn2_w_gate, m_ffn2_w_up, m_ffn2_w_down, m_pool_w, m_pool_scale, m_kv_in_norm, m_w_dkv, m_kv_latent_norm, m_w_uk, m_w_uv, m_w_dq, m_q_latent_norm, m_w_uq, m_w_o, m_final_norm, v_meta_tokens, v_ffn1_norm, v_ffn1_w_gate, v_ffn1_w_up, v_ffn1_w_down, v_mix_norm, v_ffn2_norm, v_ffn2_w_gate, v_ffn2_w_up, v_ffn2_w_down, v_pool_w, v_pool_scale, v_kv_in_norm, v_w_dkv, v_kv_latent_norm, v_w_uk, v_w_uv, v_w_dq, v_q_latent_norm, v_w_uq, v_w_o, v_final_norm):
    args = dict(locals())
    W = {n: args[n] for n in NAMES}
    M = {n: args["m_" + n] for n in NAMES}
    V = {n: args["v_" + n] for n in NAMES}
    TRANSPOSED = ("ffn1_w_gate", "ffn1_w_up", "ffn2_w_gate", "ffn2_w_up")
    for n in TRANSPOSED:
        W[n], M[n], V[n] = (jnp.swapaxes(a, 1, 2) for a in (W[n], M[n], V[n]))

    depth = ffn1_norm.shape[0]
    n_a = pool_w.shape[0]
    seq, d = x.shape[1], x.shape[2]
    nh = N_HEADS
    r = -(-(FRONT + seq) // LANES) * LANES

    cx, cy, cc = lax.axis_index("x"), lax.axis_index("y"), lax.axis_index("c")
    c_arr = jnp.reshape(cc, (1,)).astype(jnp.int32)
    s_arr = jnp.reshape(2 * cx + cy, (1,)).astype(jnp.int32)

    small_shapes = [W[n].shape for n in SMALL]

    ffn_src = {f: tuple(W[f + t].astype(BF16) for t in ("_w_gate", "_w_up", "_w_down")) for f in ("ffn1", "ffn2")}
    ffn_order = [(f, l) for l in range(depth) for f in ("ffn1", "ffn2")]
    ffn_w = {}
    gate = []
    ag_state = [None]

    def gated(a):
        if gate:
            a = a + sum(gate[1:], gate[0]).astype(a.dtype)
            gate.clear()
        return a

    def vec(a):
        return gated(a.reshape(1, -1))

    def ag_start(idx, dep):
        f, l = ffn_order[idx]
        shards = [w_[l] + dep for w_ in ffn_src[f]]
        lands = [lax.dynamic_update_slice(lax.empty((N_SHARD,) + a.shape, BF16), a[None], (2 * cx + cy, 0, 0))
                 for a in shards]
        ag_state[0] = exchange_start(shards, lands, f"ag_start_{f}_{l}", True)
        gate.append(ag_state[0][4][0, 0])

    def ag_wait(idx, after):
        f, l = ffn_order[idx]
        _, lands = exchange_wait(ag_state[0], after, f"ag_wait_{f}_{l}", True)
        ffn_w[f, l] = lands
        if idx + 1 < len(ffn_order):
            ag_start(idx + 1, lands[0][0, 0, 0] * jnp.zeros((), BF16))

    gathered = all_gather_shards([_pack([W[n] for n in SMALL])], "ag_small")[0]
    ag_start(0, (gathered[0, 0, 0] * 0.0).astype(BF16))
    small_full = {}
    for n, part in zip(SMALL, _unpack(gathered, small_shapes, (N_SHARD,))):
        small_full[n] = _cols_from_shards(part, SMALL_AXIS[n])

    meta_full = small_full["meta_tokens"]
    wp = small_full["pool_w"].astype(BF16)
    pscale = small_full["pool_scale"]
    wdkv = jnp.pad(small_full["w_dkv"], ((0, 0), (0, HEAD_W - QK_ROPE))).astype(BF16)[None]
    wuk = small_full["w_uk"].reshape(KV_RANK, nh, QK_NOPE).transpose(1, 0, 2)
    wk_h = jnp.concatenate([jnp.zeros((nh, KV_RANK, HEAD_W - QK_NOPE), F32), wuk], axis=-1).astype(BF16)
    wv_h = small_full["w_uv"].reshape(KV_RANK, nh, V_HEAD).transpose(1, 0, 2).astype(BF16)
    wdq = small_full["w_dq"].astype(BF16)
    wuq = small_full["w_uq"].reshape(-1, Q_RANK, nh, QK_NOPE + QK_ROPE).transpose(0, 2, 1, 3)
    wq_h = jnp.concatenate([wuq[..., QK_NOPE:], jnp.zeros(wuq.shape[:-1] + (HEAD_W - QK_NOPE - QK_ROPE,), F32),
                            wuq[..., :QK_NOPE]], axis=-1).astype(BF16)
    wo_h = small_full["w_o"].reshape(-1, nh, V_HEAD, d).astype(BF16)

    ctab, stab = rope_tables(r)

    h = jnp.concatenate([jnp.zeros((META_ROW0, d), F32), meta_full, x[0],
                         jnp.zeros((r - FRONT - seq, d), F32)], axis=0)
    target = jnp.concatenate([jnp.zeros((FRONT, d), F32), loss_target[0],
                              jnp.zeros((r - FRONT - seq, d), F32)], axis=0)
    saved = []
    kv = None
    for l in range(depth):
        sv = {"h0": h}
        ag_wait(2 * l, h)
        h, sv["g1"], sv["u1"] = ffn_fwd(h, vec(ffn1_norm[l]), *ffn_w["ffn1", l], f"ffn1_fwd_{l}")
        sv["h1"] = h
        if l < n_a:
            h = pool_fwd(h, vec(mix_norm[l]), wp[l], vec(pscale[l]), f"pool_fwd_{l}")
        else:
            j = l - n_a
            u = norm_fwd(h, vec(mix_norm[l]), f"mixnorm_{l}")
            cq0 = rowmm(u[None], wdq[j][None], f"dq_{l}")
            cq = norm_fwd(cq0, vec(q_latent_norm[j]), f"qnorm_{l}")
            q = proj_rope(cq, wq_h[j], ctab, stab, None, f"qproj_{l}", scale=Q_SCALE)
            o, lse = attn_fwd(q, kv["k"], kv["v"], f"attn_fwd_{l}")
            h = rowmm(o, wo_h[j], f"oproj_{l}", res=h)
            sv.update(u=u, cq0=cq0, cq=cq, q=q, o=o, lse=lse)
        sv["h2"] = h
        ag_wait(2 * l + 1, h)
        h, sv["g2"], sv["u2"] = ffn_fwd(h, vec(ffn2_norm[l]), *ffn_w["ffn2", l], f"ffn2_fwd_{l}")
        saved.append(sv)
        if l == n_a - 1:
            hkv = norm_fwd(h, vec(kv_in_norm), "kvin_norm")
            ckr = rowmm(hkv[None], wdkv, "dkv")
            ckv = norm_fwd(ckr, vec(kv_latent_norm), "kvlat_norm")
            kv = {"h": h, "hkv": hkv, "ckr": ckr, "ckv": ckv,
                  "k": proj_rope(ckv, wk_h, ctab, stab, ckr, "kproj"),
                  "v": rowmm(ckv[None], wv_h, "vproj", out_dtype=BF16, heads_out=True)}

    sse, dh, dfinal = loss_head(h, vec(final_norm), target, seq, "loss_head")
    loss = lax.psum(0.5 / d * sse[0, 0], ("x", "y", "c"))

    G = {}
    FFN = ("ffn1_w_gate", "ffn1_w_up", "ffn1_w_down", "ffn2_w_gate", "ffn2_w_up", "ffn2_w_down")
    per = {n: [lax.empty(W[n].shape, F32) for _ in range(4)] for n in FFN}
    pending = []

    def rs_complete(after):
        f, l, state = pending.pop()
        names = [f + "_w_gate", f + "_w_up", f + "_w_down"]
        for n, g_ in zip(names, rs_finish(state, after, s_arr, c_arr, f"{f}_{l}")):
            per[n] = adamw_into(W[n], g_, M[n], V[n], per[n], l, f"adamw_{n}_{l}")
        gate.append(per[names[-1]][1][0, 0, 0] * 0.0)

    dnorm = {n: [None] * depth for n in ("ffn1_norm", "mix_norm", "ffn2_norm")}
    dqnorm = [None] * (depth - n_a)
    dpool_w, dpool_scale = [None] * n_a, [None] * n_a
    dwdq, dwq_h, dwo_h = [None] * (depth - n_a), [None] * (depth - n_a), [None] * (depth - n_a)

    def ffn_backward(f, l, h_in, dh_, gg, uu):
        gam = ffn1_norm[l] if f == "ffn1" else ffn2_norm[l]
        wg_, wu_, wd_ = ffn_w[f, l]
        dh_in, dgam, hn, dy, dg, du, a = ffn_bwd_act(h_in, vec(gam), dh_, gg, uu, wg_, wu_, wd_, f"{f}_bwd_act_{l}")
        dwg, dwu, dwd = ffn_bwd_weights(hn, dy, a, dg, du, f"{f}_bwd_w_{l}")
        state = rs_begin([dwg, dwu, dwd], f"{f}_{l}")
        gate.append(state[4][0, 0])
        if pending:
            rs_complete(state[4])
        pending.append((f, l, state))
        dnorm[f + "_norm"][l] = dgam[0]
        return dh_in

    dk_tot = dv_tot = None
    for l in reversed(range(depth)):
        sv = saved[l]
        if l == n_a - 1:
            dckv = rowmm(dk_tot, gated(wk_h), "kproj_bwd", nt=True)
            dkr = rope_bwd_sum(dk_tot, ctab, stab, "kproj_bwd_rope")
            dckv = rowmm(dv_tot, wv_h, "vproj_bwd", nt=True, res=dckv)
            dwk_h = tnmm(kv["ckv"][None], dk_tot, "kproj_bwd_w")
            dwv_h = tnmm(kv["ckv"][None], dv_tot, "vproj_bwd_w")
            dlat, dkvlat = norm_bwd(kv["ckr"], vec(kv_latent_norm), dckv, None, "kvlat_norm_bwd")
            dckr = jnp.concatenate([dlat, dkr], axis=1).astype(BF16)
            dhkv = rowmm(dckr[None], wdkv, "dkv_bwd", nt=True)
            dwdkv = tnmm(kv["hkv"][None], dckr[None], "dkv_bwd_w")[0]
            dh, dkvin = norm_bwd(kv["h"], vec(kv_in_norm), dhkv, dh, "kvin_norm_bwd")
            G["w_dkv"] = dwdkv[:, :KV_RANK + QK_ROPE]
            G["w_uk"] = dwk_h[..., HEAD_W - QK_NOPE:].transpose(1, 0, 2).reshape(KV_RANK, nh * QK_NOPE)
            G["w_uv"] = dwv_h.transpose(1, 0, 2).reshape(KV_RANK, nh * V_HEAD)
        dh = ffn_backward("ffn2", l, sv["h2"], dh, sv["g2"], sv["u2"])
        if l < n_a:
            dh, dmix, dpool_w[l], dps = pool_bwd(sv["h1"], vec(mix_norm[l]), wp[l], vec(pscale[l]), dh, f"pool_bwd_{l}")
            dnorm["mix_norm"][l] = dmix[0]
            dpool_scale[l] = dps[0]
        else:
            j = l - n_a
            do, delta = attn_out_bwd(dh, gated(wo_h[j]), sv["o"], f"oproj_bwd_{l}")
            dwo_h[j] = tnmm(sv["o"], dh[None], f"oproj_bwd_w_{l}")
            dq, dk_tot, dv_tot = attn_bwd(sv["q"], kv["k"], kv["v"], do, sv["lse"].reshape(nh, 1, r),
                                          delta.reshape(nh, 1, r), dk_tot, dv_tot, f"attn_bwd_{l}")
            dxq = rope_bwd_heads(dq, ctab, stab, f"qproj_bwd_rope_{l}", scale=SM_SCALE)
            dcq = rowmm(dxq, wq_h[j], f"qproj_bwd_{l}", nt=True)
            dwq_h[j] = tnmm(sv["cq"][None], dxq, f"qproj_bwd_w_{l}")
            dcq0, dqn = norm_bwd(sv["cq0"], vec(q_latent_norm[j]), dcq, None, f"qnorm_bwd_{l}")
            dqnorm[j] = dqn[0]
            dcq0b = dcq0.astype(BF16)
            du = rowmm(dcq0b[None], wdq[j][None], f"dq_bwd_{l}", nt=True)
            dwdq[j] = tnmm(sv["u"][None], dcq0b[None], f"dq_bwd_w_{l}")[0]
            dh, dmix = norm_bwd(sv["h1"], vec(mix_norm[l]), du, dh, f"mixnorm_bwd_{l}")
            dnorm["mix_norm"][l] = dmix[0]
        dh = ffn_backward("ffn1", l, sv["h0"], dh, sv["g1"], sv["u1"])

    grad_x = dh[FRONT:FRONT + seq][None]
    G["meta_tokens"] = dh[META_ROW0:FRONT]
    G["pool_w"] = jnp.stack(dpool_w)
    G["pool_scale"] = jnp.stack(dpool_scale)
    G["w_dq"] = jnp.stack(dwdq)
    dwq = jnp.stack(dwq_h)
    dwq = jnp.concatenate([dwq[..., HEAD_W - QK_NOPE:], dwq[..., :QK_ROPE]], axis=-1)
    G["w_uq"] = dwq.transpose(0, 2, 1, 3).reshape(-1, Q_RANK, nh * (QK_NOPE + QK_ROPE))
    G["w_o"] = jnp.stack(dwo_h).reshape(-1, nh * V_HEAD, d)

    REPL = ("ffn1_norm", "mix_norm", "ffn2_norm", "kv_in_norm", "kv_latent_norm", "q_latent_norm", "final_norm")
    grep = {"ffn1_norm": jnp.stack(dnorm["ffn1_norm"]), "mix_norm": jnp.stack(dnorm["mix_norm"]),
            "ffn2_norm": jnp.stack(dnorm["ffn2_norm"]), "kv_in_norm": dkvin[0], "kv_latent_norm": dkvlat[0],
            "q_latent_norm": jnp.stack(dqnorm), "final_norm": dfinal[0]}

    def pack128(parts):
        flat = jnp.concatenate([p.reshape(-1) for p in parts])
        n = flat.shape[0]
        n_pad = -(-n // (LANES * SUBLANES)) * (LANES * SUBLANES)
        return jnp.pad(flat, (0, n_pad - n)).reshape(-1, LANES)

    rep_shapes = [W[n].shape for n in REPL]
    g_rep = all_reduce_small(pack128([grep[n] for n in REPL]), "ar_repl")
    d_rep, m_rep, v_rep = adamw(pack128([W[n] for n in REPL]), g_rep, pack128([M[n] for n in REPL]),
                                pack128([V[n] for n in REPL]), "adamw_repl")
    out_g, out_d, out_m, out_v = {}, {}, {}, {}
    for dst, buf in ((out_g, g_rep), (out_d, d_rep), (out_m, m_rep), (out_v, v_rep)):
        for n, a in zip(REPL, _unpack(buf, rep_shapes)):
            dst[n] = a

    g_small = jnp.stack([_pack([_cols_to_shards(G[n], SMALL_AXIS[n])[s] for n in SMALL]) for s in range(N_SHARD)])
    g_small = reduce_scatter([gated(g_small)], c_arr, s_arr, "small")[0]
    d_s, m_s, v_s = adamw(_pack([W[n] for n in SMALL]), g_small, _pack([M[n] for n in SMALL]),
                          _pack([V[n] for n in SMALL]), "adamw_small")
    for dst, buf in ((out_g, g_small), (out_d, d_s), (out_m, m_s), (out_v, v_s)):
        for n, a in zip(SMALL, _unpack(buf, small_shapes)):
            dst[n] = a

    rs_complete(d_s)
    for n in FFN:
        out_g[n], out_d[n], out_m[n], out_v[n] = (
            jnp.swapaxes(a, 1, 2) if n in TRANSPOSED else a for a in per[n])

    return (loss, grad_x, *[out_g[n] for n in NAMES], *[out_d[n] for n in NAMES],
            *[out_m[n] for n in NAMES], *[out_v[n] for n in NAMES])


NAMES = ("meta_tokens", "ffn1_norm", "ffn1_w_gate", "ffn1_w_up", "ffn1_w_down", "mix_norm", "ffn2_norm",
         "ffn2_w_gate", "ffn2_w_up", "ffn2_w_down", "pool_w", "pool_scale", "kv_in_norm", "w_dkv",
         "kv_latent_norm", "w_uk", "w_uv", "w_dq", "q_latent_norm", "w_uq", "w_o", "final_norm")
```

```python
import functools
import math

import jax
import jax.numpy as jnp
from jax import lax
from jax.experimental import pallas as pl
from jax.experimental.pallas import tpu as pltpu

F32 = jnp.float32
BF16 = jnp.bfloat16
MESH = pl.DeviceIdType.MESH
ANY = pl.BlockSpec(memory_space=pl.ANY)

EPS = 1e-6
CHUNK = 64
CHUNK_SHIFT = 6
N_META = 16
FRONT = 64
META_ROW0 = FRONT - N_META
POOL_WINDOWS = (2, 4, 8, 16)
HALO = 16
N_HEADS = 8
QK_NOPE = 64
QK_ROPE = 32
V_HEAD = 64
HEAD_W = 128
KV_RANK = 256
Q_RANK = 384
ROPE_THETA = 10000.0
NEG = -1e30
N_SHARD = 4
LANES = 128
SUBLANES = 8
PACK_W = 512
VMEM_BIG = 52 * 1024 * 1024
VMEM_MAX = 60 * 1024 * 1024
WGRAD_ROWS = 1664
FFN_FWD_SHARDS = 2
FFN_BWD_SHARDS = 2
FFN_BWD_ROW_SPLIT = 2

ADAM_LR = 0.001
ADAM_B1 = 0.9
ADAM_B2 = 0.999
ADAM_EPS = 1e-08
ADAM_WD = 0.01
ADAM_STEP = 10

NT = (((1,), (1,)), ((), ()))
TN = (((0,), (0,)), ((), ()))


def _params(sem=None, vmem=None):
    return pltpu.CompilerParams(dimension_semantics=sem, vmem_limit_bytes=vmem)


def _tile(n, pref, mult=SUBLANES):
    best = None
    for t in range(mult, min(n, pref) + 1, mult):
        if n % t == 0:
            best = t
    return best if best is not None else n


def _row_tile(r):
    return 640 if r % 640 == 0 else 128


def _rms(x):
    rstd = lax.rsqrt(jnp.mean(x * x, axis=-1, keepdims=True) + EPS)
    return x * rstd, rstd


def _rms_bwd(xh, rstd, dxh):
    return rstd * (dxh - xh * jnp.mean(dxh * xh, axis=-1, keepdims=True))


def _sigmoid(x):
    return 1.0 / (1.0 + jnp.exp(-x))


def _place():
    x, y, c = lax.axis_index("x"), lax.axis_index("y"), lax.axis_index("c")
    chips = [(1 - x, y), (x, 1 - y), (1 - x, 1 - y)]
    return x, y, c, chips


def all_gather_shards(shards, name):
    n = len(shards)
    slot = 2 * lax.axis_index("x") + lax.axis_index("y")
    lands = [lax.dynamic_update_slice(lax.empty((N_SHARD,) + a.shape, a.dtype), a[None], (slot, 0, 0)) for a in shards]

    def body(*refs):
        ins, outs = refs[:n], refs[2 * n:3 * n]
        send1, recv1, send2, recv2 = refs[3 * n:]
        x, y, c, chips = _place()
        s = 2 * x + y
        sib = (x, y, 1 - c)

        def rcopy(k, j, src, dst, to, first):
            return pltpu.make_async_remote_copy(
                src_ref=src, dst_ref=dst,
                send_sem=(send1 if first else send2).at[k, j],
                recv_sem=(recv1 if first else recv2).at[k, j],
                device_id=to, device_id_type=MESH)

        started = []
        for k in range(n):
            hf = ins[k].shape[0] // 2
            for j, (cx, cy) in enumerate(chips):
                r = rcopy(k, j, ins[k].at[pl.ds(c * hf, hf)], outs[k].at[s, pl.ds(c * hf, hf)],
                          (cx, cy, c), True)
                r.start()
                started.append(r)
        for k in range(n):
            hf = ins[k].shape[0] // 2
            for j, (cx, cy) in enumerate(chips):
                blk = outs[k].at[2 * cx + cy, pl.ds(c * hf, hf)]
                rcopy(k, j, blk, blk, (cx, cy, c), True).wait_recv()
                f = rcopy(k, j, blk, blk, sib, False)
                f.start()
                started.append(f)
        for k in range(n):
            hf = ins[k].shape[0] // 2
            for j, (cx, cy) in enumerate(chips):
                blk = outs[k].at[2 * cx + cy, pl.ds((1 - c) * hf, hf)]
                rcopy(k, j, blk, blk, sib, False).wait_recv()
        for r in started:
            r.wait_send()

    return pl.pallas_call(
        body, name=name,
        out_shape=[jax.ShapeDtypeStruct((N_SHARD,) + a.shape, a.dtype) for a in shards],
        in_specs=[ANY] * (2 * n), out_specs=[ANY] * n,
        input_output_aliases={n + k: k for k in range(n)},
        scratch_shapes=[pltpu.SemaphoreType.DMA((n, 3))] * 4,
    )(*shards, *lands)


def sibling_join_halves(ts, name):
    n = len(ts)

    def body(*refs):
        ins, outs = refs[:n], refs[n:2 * n]
        send, recv = refs[2 * n:]
        x, y, c, _ = _place()
        sib = (x, y, 1 - c)

        def copy(k, half):
            hf = ins[k].shape[0] // 2
            rows = pl.ds(half * hf, hf)
            return pltpu.make_async_remote_copy(
                src_ref=ins[k].at[rows], dst_ref=outs[k].at[rows],
                send_sem=send.at[k], recv_sem=recv.at[k], device_id=sib, device_id_type=MESH)

        cps = [copy(k, c) for k in range(n)]
        for r in cps:
            r.start()
        for k in range(n):
            copy(k, 1 - c).wait_recv()
        for r in cps:
            r.wait_send()

    return pl.pallas_call(
        body, name=name,
        out_shape=[jax.ShapeDtypeStruct(a.shape, a.dtype) for a in ts],
        in_specs=[ANY] * n, out_specs=[ANY] * n, input_output_aliases={k: k for k in range(n)},
        scratch_shapes=[pltpu.SemaphoreType.DMA((n,))] * 2,
    )(*ts)


def all_reduce_small(part, name):
    m, w = part.shape

    def body(x_ref, tot_ref, gat_ref, send_sems, recv_sems):
        x, y, c, chips = _place()
        me, sib = (x, y, c), (x, y, 1 - c)

        def slot(px, py, pc):
            return gat_ref.at[4 * px + 2 * py + pc]

        def copy(k, block, to, src=None):
            return pltpu.make_async_remote_copy(
                src_ref=slot(*block) if src is None else src, dst_ref=slot(*block),
                send_sem=send_sems.at[k], recv_sem=recv_sems.at[k], device_id=to, device_id_type=MESH)

        gat_ref[4 * x + 2 * y + c] = x_ref[...]
        first = [copy(0, me, sib, src=x_ref)]
        first += [copy(1 + j, me, (*chip, c), src=x_ref) for j, chip in enumerate(chips)]
        for cp in first:
            cp.start()
        passed = [copy(4 + j, (*chip, c), sib) for j, chip in enumerate(chips)]
        for j, chip in enumerate(chips):
            copy(1 + j, (*chip, c), me).wait_recv()
            passed[j].start()
        copy(0, sib, me).wait_recv()
        for j, chip in enumerate(chips):
            copy(4 + j, (*chip, 1 - c), me).wait_recv()
        for cp in first + passed:
            cp.wait_send()
        tot = gat_ref[0]
        for d in range(1, 8):
            tot = tot + gat_ref[d]
        tot_ref[...] = tot

    return pl.pallas_call(
        body, name=name,
        out_shape=jax.ShapeDtypeStruct((m, w), F32),
        in_specs=[pl.BlockSpec(memory_space=pltpu.VMEM)],
        out_specs=pl.BlockSpec(memory_space=pltpu.VMEM),
        scratch_shapes=[pltpu.VMEM((8, m, w), F32), pltpu.SemaphoreType.DMA((7,)), pltpu.SemaphoreType.DMA((7,))],
    )(part)


def adamw(w, g, m, v, name):
    a, b = w.shape
    tb = _tile(a, 256)
    c1 = 1.0 - ADAM_B1 ** ADAM_STEP
    c2 = 1.0 - ADAM_B2 ** ADAM_STEP

    def body(w_ref, g_ref, m_ref, v_ref, d_ref, mo_ref, vo_ref):
        g_ = g_ref[...]
        m_ = ADAM_B1 * m_ref[...] + (1.0 - ADAM_B1) * g_
        v_ = ADAM_B2 * v_ref[...] + (1.0 - ADAM_B2) * (g_ * g_)
        m_hat = m_ / c1
        v_hat = v_ / c2
        d_ref[...] = -ADAM_LR * (m_hat / (jnp.sqrt(v_hat) + ADAM_EPS) + ADAM_WD * w_ref[...])
        mo_ref[...] = m_
        vo_ref[...] = v_

    spec = pl.BlockSpec((tb, b), lambda i: (i, 0))
    return pl.pallas_call(
        body, name=name,
        out_shape=[jax.ShapeDtypeStruct((a, b), F32)] * 3,
        grid=(a // tb,), in_specs=[spec] * 4, out_specs=[spec] * 3,
        compiler_params=_params(("arbitrary",)),
    )(w, g, m, v)


def adamw_into(w_all, g, m_all, v_all, prev, l, name):
    nl, a, b = w_all.shape
    tb = _tile(a, 256)
    c1 = 1.0 - ADAM_B1 ** ADAM_STEP
    c2 = 1.0 - ADAM_B2 ** ADAM_STEP

    def body(w_ref, g_ref, m_ref, v_ref, p0, p1, p2, p3, go_ref, d_ref, mo_ref, vo_ref):
        g_ = g_ref[...]
        m_ = ADAM_B1 * m_ref[0] + (1.0 - ADAM_B1) * g_
        v_ = ADAM_B2 * v_ref[0] + (1.0 - ADAM_B2) * (g_ * g_)
        m_hat = m_ / c1
        v_hat = v_ / c2
        go_ref[0] = g_
        d_ref[0] = -ADAM_LR * (m_hat / (jnp.sqrt(v_hat) + ADAM_EPS) + ADAM_WD * w_ref[0])
        mo_ref[0] = m_
        vo_ref[0] = v_

    lay = pl.BlockSpec((1, tb, b), lambda i: (l, i, 0))
    return pl.pallas_call(
        body, name=name,
        out_shape=[jax.ShapeDtypeStruct((nl, a, b), F32)] * 4,
        grid=(a // tb,), in_specs=[lay, pl.BlockSpec((tb, b), lambda i: (i, 0)), lay, lay] + [ANY] * 4,
        out_specs=[lay] * 4, input_output_aliases={4: 0, 5: 1, 6: 2, 7: 3},
        compiler_params=_params(("arbitrary",)),
    )(w_all, g, m_all, v_all, *prev)


def reduce_scatter(gs, c_arr, s_arr, tag):
    lands = rs_exchange(gs, f"rs_exchange_{tag}")
    ts = [sum_eight(g, r, s_arr, c_arr, f"rs_sum_{tag}_{k}") for k, (g, r) in enumerate(zip(gs, lands))]
    return sibling_join_halves(ts, f"rs_join_{tag}")


HBM_SPEC = pl.BlockSpec(memory_space=pltpu.HBM)
SEM_SPEC = pl.BlockSpec(memory_space=pltpu.SEMAPHORE)
EFFECT = pltpu.SideEffectType.DATAFLOW_SIDE_EFFECTING


def _in_hbm(a):
    return pltpu.with_memory_space_constraint(a, pltpu.HBM)


def _exchange_copy(k, j, chip, c, s, srcs, lands, send, recv, gather, receiving):
    cx, cy = chip
    src = srcs[k] if gather else srcs[k].at[2 * cx + cy]
    if gather:
        dst = lands[k].at[2 * cx + cy] if receiving else lands[k].at[s]
    else:
        dst = lands[k].at[j]
    return pltpu.make_async_remote_copy(src_ref=src, dst_ref=dst, send_sem=send.at[3 * k + j], recv_sem=recv.at[3 * k + j],
                                        device_id=(cx, cy, c), device_id_type=MESH)


def exchange_start(srcs, lands, name, gather):
    n = len(srcs)

    def body(*refs):
        srcs_in, lands_in = refs[:n], refs[n:2 * n]
        send, recv = refs[2 * n], refs[2 * n + 1]
        token = refs[-1]
        x, y, c, chips = _place()
        for k in range(n):
            for j, chip in enumerate(chips):
                _exchange_copy(k, j, chip, c, 2 * x + y, srcs_in, lands_in, send, recv, gather, False).start()
        token[...] = jnp.zeros_like(token)

    outs = pl.pallas_call(
        body, name=name,
        out_shape=(pltpu.SemaphoreType.DMA((3 * n,)), pltpu.SemaphoreType.DMA((3 * n,)),
                   *[pltpu.HBM(a.shape, a.dtype) for a in srcs], *[pltpu.HBM(a.shape, a.dtype) for a in lands],
                   jax.ShapeDtypeStruct((SUBLANES, LANES), F32)),
        in_specs=[HBM_SPEC] * (2 * n),
        out_specs=(SEM_SPEC, SEM_SPEC, *[HBM_SPEC] * (2 * n), pl.BlockSpec(memory_space=pltpu.VMEM)),
        input_output_aliases={k: 2 + k for k in range(2 * n)},
        compiler_params=pltpu.CompilerParams(has_side_effects=EFFECT),
    )(*[_in_hbm(a) for a in srcs], *[_in_hbm(a) for a in lands])
    return outs[0], outs[1], list(outs[2:2 + n]), list(outs[2 + n:2 + 2 * n]), outs[-1]


def exchange_wait(state, after, name, gather):
    send, recv, srcs, lands, _ = state
    n = len(srcs)

    def body(*refs):
        srcs_in, lands_in = refs[:n], refs[n:2 * n]
        send_, recv_ = refs[2 * n], refs[2 * n + 1]
        x, y, c, chips = _place()
        for k in range(n):
            for j, chip in enumerate(chips):
                cp = _exchange_copy(k, j, chip, c, 2 * x + y, srcs_in, lands_in, send_, recv_, gather, True)
                cp.wait_send()
                cp.wait_recv()

    outs = pl.pallas_call(
        body, name=name,
        out_shape=tuple(pltpu.HBM(a.shape, a.dtype) for a in srcs + lands),
        in_specs=[HBM_SPEC] * (2 * n) + [SEM_SPEC, SEM_SPEC, ANY],
        out_specs=tuple([HBM_SPEC] * (2 * n)),
        input_output_aliases={k: k for k in range(2 * n)},
        compiler_params=pltpu.CompilerParams(has_side_effects=EFFECT),
    )(*srcs, *lands, send, recv, after)
    return list(outs[:n]), list(outs[n:])


RS_SLOTS = 7


def _rs_copies(k, gs, lands, send, recv, x, y, c, chips):
    hf = gs[k].shape[1] // 2
    base = RS_SLOTS * k
    out = []
    for j, (cx, cy) in enumerate(chips):
        for cd in range(2):
            out.append(pltpu.make_async_remote_copy(
                src_ref=gs[k].at[2 * cx + cy, pl.ds(cd * hf, hf)], dst_ref=lands[k].at[2 * j + c],
                send_sem=send.at[base + 2 * j + cd], recv_sem=recv.at[base + 2 * j + c],
                device_id=(cx, cy, cd), device_id_type=MESH))
    out.append(pltpu.make_async_remote_copy(
        src_ref=gs[k].at[2 * x + y, pl.ds((1 - c) * hf, hf)], dst_ref=lands[k].at[RS_SLOTS - 1],
        send_sem=send.at[base + RS_SLOTS - 1], recv_sem=recv.at[base + RS_SLOTS - 1],
        device_id=(x, y, 1 - c), device_id_type=MESH))
    return out


def _rs_wait_all(n, gs, lands, send, recv):
    x, y, c, chips = _place()
    for k in range(n):
        for cp in _rs_copies(k, gs, lands, send, recv, x, y, c, chips):
            cp.wait_send()
        hf = gs[k].shape[1] // 2
        for slot in range(RS_SLOTS):
            pltpu.make_async_remote_copy(
                src_ref=gs[k].at[0, pl.ds(0, hf)], dst_ref=lands[k].at[slot],
                send_sem=send.at[RS_SLOTS * k + slot], recv_sem=recv.at[RS_SLOTS * k + slot],
                device_id=(x, y, 1 - c), device_id_type=MESH).wait_recv()


def _rs_land_shapes(gs):
    return [(RS_SLOTS, g.shape[1] // 2, g.shape[2]) for g in gs]


def rs_exchange(gs, name):
    n = len(gs)

    def body(*refs):
        ins, outs = refs[:n], refs[n:2 * n]
        send, recv = refs[2 * n:]
        x, y, c, chips = _place()
        for k in range(n):
            for cp in _rs_copies(k, ins, outs, send, recv, x, y, c, chips):
                cp.start()
        _rs_wait_all(n, ins, outs, send, recv)

    return pl.pallas_call(
        body, name=name,
        out_shape=[jax.ShapeDtypeStruct(s, F32) for s in _rs_land_shapes(gs)],
        in_specs=[ANY] * n, out_specs=[ANY] * n,
        scratch_shapes=[pltpu.SemaphoreType.DMA((RS_SLOTS * n,))] * 2,
    )(*gs)


def rs_exchange_start(gs, name):
    n = len(gs)
    lands = [lax.empty(s, F32) for s in _rs_land_shapes(gs)]

    def body(*refs):
        ins, lands_in = refs[:n], refs[n:2 * n]
        send, recv = refs[2 * n], refs[2 * n + 1]
        token = refs[-1]
        x, y, c, chips = _place()
        for k in range(n):
            for cp in _rs_copies(k, ins, lands_in, send, recv, x, y, c, chips):
                cp.start()
        token[...] = jnp.zeros_like(token)

    outs = pl.pallas_call(
        body, name=name,
        out_shape=(pltpu.SemaphoreType.DMA((RS_SLOTS * n,)), pltpu.SemaphoreType.DMA((RS_SLOTS * n,)),
                   *[pltpu.HBM(a.shape, a.dtype) for a in gs], *[pltpu.HBM(a.shape, a.dtype) for a in lands],
                   jax.ShapeDtypeStruct((SUBLANES, LANES), F32)),
        in_specs=[HBM_SPEC] * (2 * n),
        out_specs=(SEM_SPEC, SEM_SPEC, *[HBM_SPEC] * (2 * n), pl.BlockSpec(memory_space=pltpu.VMEM)),
        input_output_aliases={k: 2 + k for k in range(2 * n)},
        compiler_params=pltpu.CompilerParams(has_side_effects=EFFECT),
    )(*[_in_hbm(a) for a in gs], *[_in_hbm(a) for a in lands])
    return outs[0], outs[1], list(outs[2:2 + n]), list(outs[2 + n:2 + 2 * n]), outs[-1]


def rs_exchange_wait(state, after, name):
    send, recv, gs, lands, _ = state
    n = len(gs)

    def body(*refs):
        _rs_wait_all(n, refs[:n], refs[n:2 * n], refs[2 * n], refs[2 * n + 1])

    outs = pl.pallas_call(
        body, name=name,
        out_shape=tuple(pltpu.HBM(a.shape, a.dtype) for a in gs + lands),
        in_specs=[HBM_SPEC] * (2 * n) + [SEM_SPEC, SEM_SPEC, ANY],
        out_specs=tuple([HBM_SPEC] * (2 * n)),
        input_output_aliases={k: k for k in range(2 * n)},
        compiler_params=pltpu.CompilerParams(has_side_effects=EFFECT),
    )(*gs, *lands, send, recv, after)
    return list(outs[:n]), list(outs[n:])


def sum_eight(g, land, s_arr, c_arr, name):
    _, a, b = g.shape
    ah = a // 2
    tb = _tile(ah, 256)
    nb = ah // tb

    def body(s_ref, c_ref, g_ref, *rest):
        lands_, o_ref = rest[:RS_SLOTS], rest[RS_SLOTS]
        tot = g_ref[0] + lands_[RS_SLOTS - 1][0]
        for slot in range(RS_SLOTS - 1):
            tot = tot + lands_[slot][0]
        o_ref[...] = tot

    def lspec(slot):
        return pl.BlockSpec((1, tb, b), lambda i, s, c: (slot, i, 0))

    return pl.pallas_call(
        body, name=name,
        out_shape=jax.ShapeDtypeStruct((a, b), F32),
        grid_spec=pltpu.PrefetchScalarGridSpec(
            num_scalar_prefetch=2, grid=(nb,),
            in_specs=[pl.BlockSpec((1, tb, b), lambda i, s, c: (s[0], c[0] * nb + i, 0))]
            + [lspec(slot) for slot in range(RS_SLOTS)],
            out_specs=pl.BlockSpec((tb, b), lambda i, s, c: (c[0] * nb + i, 0))),
        compiler_params=_params(("arbitrary",)),
    )(s_arr, c_arr, g, *([land] * RS_SLOTS))


def rs_begin(gs, tag):
    return rs_exchange_start(gs, f"rs_start_{tag}")


def rs_finish(state, after, s_arr, c_arr, tag):
    gs, lands = rs_exchange_wait(state, after, f"rs_wait_{tag}")
    ts = [sum_eight(g, r, s_arr, c_arr, f"rs_sum_{tag}_{k}") for k, (g, r) in enumerate(zip(gs, lands))]
    return sibling_join_halves(ts, f"rs_join_{tag}")


def ffn_fwd(h, gamma, wg, wu, wd, name):
    r, d = h.shape
    ns, fs, _ = wg.shape
    tm = _row_tile(r)
    sb = _tile(ns, FFN_FWD_SHARDS, 1)
    nsteps = ns // sb

    def body(h_ref, g_ref, wg_ref, wu_ref, wd_ref, ho_ref, gg_ref, uu_ref, hn_sc, acc_sc):
        s = pl.program_id(1)

        @pl.when(s == 0)
        def _():
            xh, _ = _rms(h_ref[...])
            hn_sc[...] = (xh * g_ref[...]).astype(BF16)
            acc_sc[...] = jnp.zeros_like(acc_sc)

        hn = hn_sc[...]
        part = None
        for t in range(sb):
            g = lax.dot_general(hn, wg_ref[t], NT, preferred_element_type=F32)
            u = lax.dot_general(hn, wu_ref[t], NT, preferred_element_type=F32)
            gg_ref[t] = g.astype(BF16)
            uu_ref[t] = u.astype(BF16)
            a = (g * _sigmoid(g) * u).astype(BF16)
            p = jnp.dot(a, wd_ref[t], preferred_element_type=F32)
            part = p if part is None else part + p
        acc_sc[...] += part

        @pl.when(s == nsteps - 1)
        def _():
            ho_ref[...] = h_ref[...] + 0.5 * acc_sc[...]

    wsp = pl.BlockSpec((sb, fs, d), lambda i, s: (s, 0, 0))
    act = pl.BlockSpec((sb, tm, fs), lambda i, s: (s, i, 0))
    return pl.pallas_call(
        body, name=name,
        out_shape=[jax.ShapeDtypeStruct((r, d), F32), jax.ShapeDtypeStruct((ns, r, fs), BF16),
                   jax.ShapeDtypeStruct((ns, r, fs), BF16)],
        grid=(r // tm, nsteps),
        in_specs=[pl.BlockSpec((tm, d), lambda i, s: (i, 0)), pl.BlockSpec((1, d), lambda i, s: (0, 0)), wsp, wsp, wsp],
        out_specs=[pl.BlockSpec((tm, d), lambda i, s: (i, 0)), act, act],
        scratch_shapes=[pltpu.VMEM((tm, d), BF16), pltpu.VMEM((tm, d), F32)],
        compiler_params=_params(("arbitrary", "arbitrary"), VMEM_MAX),
    )(h, gamma, wg, wu, wd)


def ffn_bwd_act(h, gamma, dh, gg, uu, wg, wu, wd, name):
    r, d = h.shape
    ns, fs, _ = wg.shape
    tm = _row_tile(r) // FFN_BWD_ROW_SPLIT
    sb = _tile(ns, FFN_BWD_SHARDS, 1)
    nsteps = ns // sb

    def body(h_ref, g_ref, dh_ref, gg_ref, uu_ref, wg_ref, wu_ref, wd_ref,
             dho_ref, dgam_ref, hn_ref, dy_ref, dg_ref, du_ref, a_ref, acc_sc):
        i, s = pl.program_id(0), pl.program_id(1)

        @pl.when(s == 0)
        def _():
            xh, _ = _rms(h_ref[...])
            hn_ref[...] = (xh * g_ref[...]).astype(BF16)
            dy_ref[...] = (0.5 * dh_ref[...]).astype(BF16)
            acc_sc[...] = jnp.zeros_like(acc_sc)

        @pl.when((i == 0) & (s == 0))
        def _():
            dgam_ref[...] = jnp.zeros_like(dgam_ref)

        part = None
        for t in range(sb):
            g = gg_ref[t].astype(F32)
            u = uu_ref[t].astype(F32)
            da = lax.dot_general(dy_ref[...], wd_ref[t], NT, preferred_element_type=F32)
            sig = _sigmoid(g)
            sl = g * sig
            a_ref[t] = (sl * u).astype(BF16)
            du = (da * sl).astype(BF16)
            dg = (da * u * (sig * (1.0 + g * (1.0 - sig)))).astype(BF16)
            dg_ref[t] = dg
            du_ref[t] = du
            p = (jnp.dot(dg, wg_ref[t], preferred_element_type=F32)
                 + jnp.dot(du, wu_ref[t], preferred_element_type=F32))
            part = p if part is None else part + p
        acc_sc[...] += part

        @pl.when(s == nsteps - 1)
        def _():
            xh, rstd = _rms(h_ref[...])
            dhn = acc_sc[...]
            dgam_ref[...] += jnp.sum(dhn * xh, axis=0, keepdims=True)
            dho_ref[...] = dh_ref[...] + _rms_bwd(xh, rstd, dhn * g_ref[...])

    row = pl.BlockSpec((tm, d), lambda i, s: (i, 0))
    act = pl.BlockSpec((sb, tm, fs), lambda i, s: (s, i, 0))
    wsp = pl.BlockSpec((sb, fs, d), lambda i, s: (s, 0, 0))
    return pl.pallas_call(
        body, name=name,
        out_shape=[jax.ShapeDtypeStruct((r, d), F32), jax.ShapeDtypeStruct((1, d), F32),
                   jax.ShapeDtypeStruct((r, d), BF16), jax.ShapeDtypeStruct((r, d), BF16),
                   jax.ShapeDtypeStruct((ns, r, fs), BF16), jax.ShapeDtypeStruct((ns, r, fs), BF16),
                   jax.ShapeDtypeStruct((ns, r, fs), BF16)],
        grid=(r // tm, nsteps),
        in_specs=[row, pl.BlockSpec((1, d), lambda i, s: (0, 0)), row, act, act, wsp, wsp, wsp],
        out_specs=[row, pl.BlockSpec((1, d), lambda i, s: (0, 0)), row, row, act, act, act],
        scratch_shapes=[pltpu.VMEM((tm, d), F32)],
        compiler_params=_params(("arbitrary", "arbitrary"), VMEM_MAX),
    )(h, gamma, dh, gg, uu, wg, wu, wd)


def ffn_bwd_weights(hn, dy, a, dg, du, name):
    r, d = hn.shape
    ns, _, fs = a.shape
    tm = WGRAD_ROWS if r % WGRAD_ROWS == 0 else _row_tile(r)

    def body(hn_ref, dy_ref, a_ref, dg_ref, du_ref, wg_ref, wu_ref, wd_ref):
        @pl.when(pl.program_id(1) == 0)
        def _():
            wg_ref[...] = jnp.zeros_like(wg_ref)
            wu_ref[...] = jnp.zeros_like(wu_ref)
            wd_ref[...] = jnp.zeros_like(wd_ref)

        hn_ = hn_ref[...]
        wg_ref[0] += lax.dot_general(dg_ref[0], hn_, TN, preferred_element_type=F32)
        wu_ref[0] += lax.dot_general(du_ref[0], hn_, TN, preferred_element_type=F32)
        wd_ref[0] += lax.dot_general(a_ref[0], dy_ref[...], TN, preferred_element_type=F32)

    row = pl.BlockSpec((tm, d), lambda s, i: (i, 0))
    act = pl.BlockSpec((1, tm, fs), lambda s, i: (s, i, 0))
    wsp = pl.BlockSpec((1, fs, d), lambda s, i: (s, 0, 0))
    return pl.pallas_call(
        body, name=name,
        out_shape=[jax.ShapeDtypeStruct((ns, fs, d), F32)] * 3,
        grid=(ns, r // tm),
        in_specs=[row, row, act, act, act],
        out_specs=[wsp, wsp, wsp],
        compiler_params=_params(("arbitrary", "arbitrary"), VMEM_MAX),
    )(hn, dy, a, dg, du)


def norm_fwd(x, gamma, name):
    r = x.shape[0]
    w = gamma.shape[1]
    tm = _row_tile(r)

    def body(x_ref, g_ref, o_ref):
        xh, _ = _rms(x_ref[...])
        o_ref[...] = (xh * g_ref[...]).astype(BF16)

    return pl.pallas_call(
        body, name=name, out_shape=jax.ShapeDtypeStruct((r, w), BF16), grid=(r // tm,),
        in_specs=[pl.BlockSpec((tm, w), lambda i: (i, 0)), pl.BlockSpec((1, w), lambda i: (0, 0))],
        out_specs=pl.BlockSpec((tm, w), lambda i: (i, 0)),
        compiler_params=_params(("arbitrary",)),
    )(x, gamma)


def norm_bwd(x, gamma, dy, dres, name):
    r = x.shape[0]
    w = gamma.shape[1]
    tm = _row_tile(r)
    has_res = dres is not None

    def body(*refs):
        if has_res:
            x_ref, g_ref, dy_ref, dr_ref, dx_ref, dgam_ref = refs
        else:
            x_ref, g_ref, dy_ref, dx_ref, dgam_ref = refs

        @pl.when(pl.program_id(0) == 0)
        def _():
            dgam_ref[...] = jnp.zeros_like(dgam_ref)

        xh, rstd = _rms(x_ref[...])
        dy_ = dy_ref[...].astype(F32)
        dgam_ref[...] += jnp.sum(dy_ * xh, axis=0, keepdims=True)
        dx = _rms_bwd(xh, rstd, dy_ * g_ref[...])
        if has_res:
            dx = dx + dr_ref[...]
        dx_ref[...] = dx

    row = pl.BlockSpec((tm, w), lambda i: (i, 0))
    vec = pl.BlockSpec((1, w), lambda i: (0, 0))
    ins = [x, gamma, dy] + ([dres] if has_res else [])
    return pl.pallas_call(
        body, name=name,
        out_shape=[jax.ShapeDtypeStruct((r, w), F32), jax.ShapeDtypeStruct((1, w), F32)],
        grid=(r // tm,), in_specs=[row, vec, row] + ([row] if has_res else []), out_specs=[row, vec],
        compiler_params=_params(("arbitrary",)),
    )(*ins)


def rowmm(a, w, name, *, nt=False, res=None, out_dtype=F32, heads_out=False):
    ha, r, ka = a.shape
    hw = w.shape[0]
    nh = max(ha, hw)
    n = w.shape[1] if nt else w.shape[2]
    tm = _row_tile(r)
    dims = NT if nt else (((1,), (0,)), ((), ()))
    has_res = res is not None

    def body(*refs):
        if has_res:
            a_ref, w_ref, r_ref, o_ref = refs
        else:
            a_ref, w_ref, o_ref = refs
        shared = a_ref[0].astype(BF16) if ha == 1 else None
        acc = None
        for h in range(nh):
            lhs = shared if ha == 1 else a_ref[h].astype(BF16)
            p = lax.dot_general(lhs, w_ref[h if hw > 1 else 0], dims, preferred_element_type=F32)
            if heads_out:
                o_ref[h] = p.astype(out_dtype)
            else:
                acc = p if acc is None else acc + p
        if not heads_out:
            if has_res:
                acc = acc + r_ref[...]
            o_ref[...] = acc.astype(out_dtype)

    in_specs = [pl.BlockSpec((ha, tm, ka), lambda i: (0, i, 0)), pl.BlockSpec(w.shape, lambda i: (0, 0, 0))]
    ins = [a, w]
    if has_res:
        in_specs.append(pl.BlockSpec((tm, n), lambda i: (i, 0)))
        ins.append(res)
    if heads_out:
        out_shape = jax.ShapeDtypeStruct((nh, r, n), out_dtype)
        out_spec = pl.BlockSpec((nh, tm, n), lambda i: (0, i, 0))
    else:
        out_shape = jax.ShapeDtypeStruct((r, n), out_dtype)
        out_spec = pl.BlockSpec((tm, n), lambda i: (i, 0))
    return pl.pallas_call(
        body, name=name, out_shape=out_shape, grid=(r // tm,), in_specs=in_specs, out_specs=out_spec,
        compiler_params=_params(("arbitrary",), VMEM_BIG),
    )(*ins)


def tnmm(a, b, name):
    ha, r, ka = a.shape
    hb, _, nb = b.shape
    nh = max(ha, hb)
    tm = _row_tile(r)

    def body(a_ref, b_ref, o_ref):
        @pl.when(pl.program_id(0) == 0)
        def _():
            o_ref[...] = jnp.zeros_like(o_ref)

        a_shared = a_ref[0].astype(BF16) if ha == 1 else None
        b_shared = b_ref[0].astype(BF16) if hb == 1 else None
        for h in range(nh):
            lhs = a_shared if ha == 1 else a_ref[h].astype(BF16)
            rhs = b_shared if hb == 1 else b_ref[h].astype(BF16)
            o_ref[h] += lax.dot_general(lhs, rhs, TN, preferred_element_type=F32)

    return pl.pallas_call(
        body, name=name, out_shape=jax.ShapeDtypeStruct((nh, ka, nb), F32), grid=(r // tm,),
        in_specs=[pl.BlockSpec((ha, tm, ka), lambda i: (0, i, 0)), pl.BlockSpec((hb, tm, nb), lambda i: (0, i, 0))],
        out_specs=pl.BlockSpec((nh, ka, nb), lambda i: (0, 0, 0)),
        compiler_params=_params(("arbitrary",), VMEM_BIG),
    )(a, b)


def rope_tables(r):
    inv = 1.0 / (ROPE_THETA ** (jnp.arange(0, QK_ROPE, 2, dtype=F32) / QK_ROPE))
    pos = (jnp.arange(r, dtype=F32) - META_ROW0)[:, None]
    ang = pos * inv[None, :]
    cos, sin = jnp.cos(ang), jnp.sin(ang)
    ones = jnp.ones((r, HEAD_W - QK_ROPE), F32)
    ctab = jnp.concatenate([cos, cos, ones], axis=1)
    stab = jnp.concatenate([-sin, sin, jnp.zeros_like(ones)], axis=1)
    return ctab, stab


def _swap_halves(z):
    lane = lax.broadcasted_iota(jnp.int32, z.shape, 1)
    up = pltpu.roll(z, HEAD_W - QK_ROPE // 2, 1)
    down = pltpu.roll(z, QK_ROPE // 2, 1)
    return jnp.where(lane < QK_ROPE // 2, up, jnp.where(lane < QK_ROPE, down, 0.0))


def proj_rope(a, w, ctab, stab, extra, name, scale=1.0):
    r, ka = a.shape
    nh = w.shape[0]
    tm = _row_tile(r)
    has_extra = extra is not None

    def body(*refs):
        if has_extra:
            a_ref, w_ref, c_ref, s_ref, e_ref, o_ref = refs
        else:
            a_ref, w_ref, c_ref, s_ref, o_ref = refs
        a_ = a_ref[...]
        ctab_, stab_ = c_ref[...], s_ref[...]
        if scale != 1.0:
            ctab_, stab_ = ctab_ * scale, stab_ * scale
        for h in range(nh):
            x = jnp.dot(a_, w_ref[h], preferred_element_type=F32)
            if has_extra:
                x = x + e_ref[...]
            o_ref[h] = (x * ctab_ + _swap_halves(x) * stab_).astype(BF16)

    tab = pl.BlockSpec((tm, HEAD_W), lambda i: (i, 0))
    in_specs = [pl.BlockSpec((tm, ka), lambda i: (i, 0)), pl.BlockSpec((nh, ka, HEAD_W), lambda i: (0, 0, 0)),
                tab, tab]
    ins = [a, w, ctab, stab]
    if has_extra:
        in_specs.append(pl.BlockSpec((tm, HEAD_W), lambda i: (i, 2)))
        ins.append(extra)
    return pl.pallas_call(
        body, name=name, out_shape=jax.ShapeDtypeStruct((nh, r, HEAD_W), BF16), grid=(r // tm,),
        in_specs=in_specs, out_specs=pl.BlockSpec((nh, tm, HEAD_W), lambda i: (0, i, 0)),
        compiler_params=_params(("arbitrary",)),
    )(*ins)


def rope_bwd_heads(d, ctab, stab, name, scale=1.0):
    nh, r, _ = d.shape
    tm = _row_tile(r)

    def body(d_ref, c_ref, s_ref, o_ref):
        ctab_, stab_ = c_ref[...], s_ref[...]
        if scale != 1.0:
            ctab_, stab_ = ctab_ * scale, stab_ * scale
        for h in range(nh):
            d_ = d_ref[h]
            o_ref[h] = (d_ * ctab_ + _swap_halves(d_ * stab_)).astype(BF16)

    tab = pl.BlockSpec((tm, HEAD_W), lambda i: (i, 0))
    blk = pl.BlockSpec((nh, tm, HEAD_W), lambda i: (0, i, 0))
    return pl.pallas_call(
        body, name=name, out_shape=jax.ShapeDtypeStruct((nh, r, HEAD_W), BF16), grid=(r // tm,),
        in_specs=[blk, tab, tab], out_specs=blk,
        compiler_params=_params(("arbitrary",)),
    )(d, ctab, stab)


def rope_bwd_sum(d, ctab, stab, name):
    nh, r, _ = d.shape
    tm = _row_tile(r)

    def body(d_ref, c_ref, s_ref, o_ref):
        d_ = d_ref[0]
        for h in range(1, nh):
            d_ = d_ + d_ref[h]
        lane = lax.broadcasted_iota(jnp.int32, d_.shape, 1)
        g = d_ * c_ref[...] + _swap_halves(d_ * s_ref[...])
        o_ref[...] = jnp.where(lane < QK_ROPE, g, 0.0)

    tab = pl.BlockSpec((tm, HEAD_W), lambda i: (i, 0))
    return pl.pallas_call(
        body, name=name, out_shape=jax.ShapeDtypeStruct((r, HEAD_W), F32), grid=(r // tm,),
        in_specs=[pl.BlockSpec((nh, tm, HEAD_W), lambda i: (0, i, 0)), tab, tab], out_specs=tab,
        compiler_params=_params(("arbitrary",)),
    )(d, ctab, stab)


def _attn_tiles(r):
    t = _row_tile(r)
    return t, t


def _mask(q0, k0, nq_, nk_, keys_on_rows=False):
    shape = (nk_, nq_) if keys_on_rows else (nq_, nk_)
    rq = q0 + lax.broadcasted_iota(jnp.int32, shape, 1 if keys_on_rows else 0)
    rk = k0 + lax.broadcasted_iota(jnp.int32, shape, 0 if keys_on_rows else 1)
    return ((rk >> CHUNK_SHIFT) <= (rq >> CHUNK_SHIFT)) & (rk >= META_ROW0)


ATTN_FWD_HEADS = 8
ATTN_FWD_UNROLL = 4
ATTN_BWD_HEADS = 4


SM_SCALE = 1.0 / math.sqrt(QK_NOPE + QK_ROPE)
LOG2E = math.log2(math.e)
Q_SCALE = SM_SCALE * LOG2E


def attn_fwd(q, k, v, name):
    nh, r, dk = q.shape
    dv = v.shape[-1]
    tq, tk = _attn_tiles(r)
    nq, nk = r // tq, r // tk

    def last_k(i):
        return ((i + 1) * tq - 1) // tk

    pairs = [(i, j) for i in range(nq) for j in range(last_k(i) + 1)]
    qi_tab = jnp.asarray([p[0] for p in pairs], jnp.int32)
    kj_tab = jnp.asarray([p[1] for p in pairs], jnp.int32)
    hb = _tile(nh, ATTN_FWD_HEADS, 1)
    hu = _tile(hb, ATTN_FWD_UNROLL, 1)

    def body(qi_ref, kj_ref, q_ref, k_ref, v_ref, o_ref, lse_ref, m_sc, l_sc, acc_sc):
        t = pl.program_id(1)
        i, j = qi_ref[t], kj_ref[t]

        @pl.when(j == 0)
        def _():
            m_sc[...] = jnp.full_like(m_sc, NEG)
            l_sc[...] = jnp.zeros_like(l_sc)
            acc_sc[...] = jnp.zeros_like(acc_sc)

        def step(masked):
            def one_head(hh):
                s = lax.dot_general(q_ref[hh], k_ref[hh], NT, preferred_element_type=F32)
                if masked:
                    s = jnp.where(_mask(i * tq, j * tk, tq, tk), s, NEG)
                m_old = m_sc[hh]
                m_new = jnp.maximum(m_old, jnp.max(s, axis=-1, keepdims=True))
                alpha = jnp.exp2(m_old - m_new)
                p = jnp.exp2(s - jnp.tile(m_new, (1, tk // LANES)))
                l_sc[hh] = alpha * l_sc[hh] + jnp.sum(p, axis=-1, keepdims=True)
                acc_sc[hh] = (alpha[:, :dv] * acc_sc[hh]
                              + jnp.dot(p.astype(BF16), v_ref[hh], preferred_element_type=F32))
                m_sc[hh] = m_new

            def head_group(g, carry):
                for u in range(hu):
                    one_head(g * hu + u)
                return carry

            lax.fori_loop(0, hb // hu, head_group, 0)

        needs_mask = (j == last_k(i)) | (j == 0)
        pl.when(needs_mask)(functools.partial(step, True))
        pl.when(jnp.logical_not(needs_mask))(functools.partial(step, False))

        @pl.when(j == last_k(i))
        def _():
            def one_head(hh, carry):
                l = l_sc[hh]
                o_ref[hh] = (acc_sc[hh] / l[:, :dv]).astype(BF16)
                lse_ref[hh] = jnp.transpose(m_sc[hh] + jnp.log2(l))[:1, :]
                return carry

            lax.fori_loop(0, hb, one_head, 0)

    qspec = lambda w: pl.BlockSpec((hb, tq, w), lambda h, t, qi, kj: (h, qi[t], 0))
    kspec = lambda w: pl.BlockSpec((hb, tk, w), lambda h, t, qi, kj: (h, kj[t], 0))
    return pl.pallas_call(
        body, name=name,
        out_shape=[jax.ShapeDtypeStruct((nh, r, dv), BF16), jax.ShapeDtypeStruct((nh, 1, r), F32)],
        grid_spec=pltpu.PrefetchScalarGridSpec(
            num_scalar_prefetch=2, grid=(nh // hb, len(pairs)),
            in_specs=[qspec(dk), kspec(dk), kspec(dv)],
            out_specs=[qspec(dv), pl.BlockSpec((hb, 1, tq), lambda h, t, qi, kj: (h, 0, qi[t]))],
            scratch_shapes=[pltpu.VMEM((hb, tq, LANES), F32), pltpu.VMEM((hb, tq, LANES), F32),
                            pltpu.VMEM((hb, tq, dv), F32)]),
        compiler_params=_params(("arbitrary", "arbitrary"), VMEM_BIG),
    )(qi_tab, kj_tab, q, k, v)


def attn_bwd(q, k, v, do, lse, delta, dk_prev, dv_prev, name):
    nh, r, dk = q.shape
    dv = v.shape[-1]
    tq, tk = _attn_tiles(r)
    nq, nk = r // tq, r // tk
    has_prev = dk_prev is not None

    def first_q(j):
        return (j * tk) // tq

    pairs = [(j, i) for j in range(nk) for i in range(first_q(j), nq)]
    kj_tab = jnp.asarray([p[0] for p in pairs], jnp.int32)
    qi_tab = jnp.asarray([p[1] for p in pairs], jnp.int32)
    hb = _tile(nh, ATTN_BWD_HEADS, 1)

    def body(*refs):
        if has_prev:
            (kj_ref, qi_ref, q_ref, k_ref, v_ref, do_ref, lse_ref, dl_ref, pk_ref, pv_ref,
             dq_ref, dk_ref, dv_ref, dk_sc, dv_sc) = refs
        else:
            (kj_ref, qi_ref, q_ref, k_ref, v_ref, do_ref, lse_ref, dl_ref,
             dq_ref, dk_ref, dv_ref, dk_sc, dv_sc) = refs
        t = pl.program_id(1)
        j, i = kj_ref[t], qi_ref[t]

        @pl.when(t == 0)
        def _():
            dq_ref[...] = jnp.zeros_like(dq_ref)

        @pl.when(i == first_q(j))
        def _():
            dk_sc[...] = jnp.zeros_like(dk_sc)
            dv_sc[...] = jnp.zeros_like(dv_sc)

        def step(masked):
            qrows = pl.ds(pl.multiple_of(i * tq, tq), tq)
            for hh in range(hb):
                q_, k_, do_ = q_ref[hh], k_ref[hh], do_ref[hh]
                st = lax.dot_general(k_, q_, NT, preferred_element_type=F32)
                if masked:
                    st = jnp.where(_mask(i * tq, j * tk, tq, tk, keys_on_rows=True), st, NEG)
                pt = jnp.exp2(st - lse_ref[hh])
                dv_sc[hh] += jnp.dot(pt.astype(BF16), do_, preferred_element_type=F32)
                dpt = lax.dot_general(v_ref[hh], do_, NT, preferred_element_type=F32)
                dst = (pt * (dpt - dl_ref[hh])).astype(BF16)
                dk_sc[hh] += jnp.dot(dst, q_, preferred_element_type=F32)
                dq_ref[hh, qrows, :] += lax.dot_general(dst, k_, TN, preferred_element_type=F32)

        needs_mask = (i == first_q(j)) | (j == 0)
        pl.when(needs_mask)(functools.partial(step, True))
        pl.when(jnp.logical_not(needs_mask))(functools.partial(step, False))

        @pl.when(i == nq - 1)
        def _():
            dk_ = dk_sc[...] * (1.0 / LOG2E)
            dv_ = dv_sc[...]
            if has_prev:
                dk_ = dk_ + pk_ref[...]
                dv_ = dv_ + pv_ref[...]
            dk_ref[...] = dk_
            dv_ref[...] = dv_

    krow = lambda w: pl.BlockSpec((hb, tk, w), lambda h, t, kj, qi: (h, kj[t], 0))
    qrow = lambda w: pl.BlockSpec((hb, tq, w), lambda h, t, kj, qi: (h, qi[t], 0))
    qvec = pl.BlockSpec((hb, 1, tq), lambda h, t, kj, qi: (h, 0, qi[t]))
    in_specs = [qrow(dk), krow(dk), krow(dv), qrow(dv), qvec, qvec]
    ins = [q, k, v, do, lse, delta]
    if has_prev:
        in_specs += [krow(dk), krow(dv)]
        ins += [dk_prev, dv_prev]
    return pl.pallas_call(
        body, name=name,
        out_shape=[jax.ShapeDtypeStruct((nh, r, dk), F32), jax.ShapeDtypeStruct((nh, r, dk), F32),
                   jax.ShapeDtypeStruct((nh, r, dv), F32)],
        grid_spec=pltpu.PrefetchScalarGridSpec(
            num_scalar_prefetch=2, grid=(nh // hb, len(pairs)), in_specs=in_specs,
            out_specs=[pl.BlockSpec((hb, r, dk), lambda h, t, kj, qi: (h, 0, 0)), krow(dk), krow(dv)],
            scratch_shapes=[pltpu.VMEM((hb, tk, dk), F32), pltpu.VMEM((hb, tk, dv), F32)]),
        compiler_params=_params(("arbitrary", "arbitrary"), VMEM_MAX),
    )(kj_tab, qi_tab, *ins)


def attn_out_bwd(dattn, wo, o, name):
    r, d = dattn.shape
    nh, dv, _ = wo.shape
    tm = _row_tile(r)

    def body(da_ref, w_ref, o_ref, do_ref, dl_ref):
        da = da_ref[...].astype(BF16)
        for h in range(nh):
            do_ = lax.dot_general(da, w_ref[h], NT, preferred_element_type=F32).astype(BF16)
            do_ref[h] = do_
            col = jnp.sum(do_.astype(F32) * o_ref[h].astype(F32), axis=-1, keepdims=True)
            dl_ref[h] = jnp.transpose(jnp.broadcast_to(col, (tm, LANES)))[:1, :]

    return pl.pallas_call(
        body, name=name,
        out_shape=[jax.ShapeDtypeStruct((nh, r, dv), BF16), jax.ShapeDtypeStruct((nh, 1, r), F32)],
        grid=(r // tm,),
        in_specs=[pl.BlockSpec((tm, d), lambda i: (i, 0)), pl.BlockSpec((nh, dv, d), lambda i: (0, 0, 0)),
                  pl.BlockSpec((nh, tm, dv), lambda i: (0, i, 0))],
        out_specs=[pl.BlockSpec((nh, tm, dv), lambda i: (0, i, 0)), pl.BlockSpec((nh, 1, tm), lambda i: (0, 0, i))],
        compiler_params=_params(("arbitrary",)),
    )(dattn, wo, o)


def _pool_counts(row0, n, window):
    rows = row0 + lax.broadcasted_iota(jnp.int32, (n, 1), 0)
    cnt = jnp.clip(rows - META_ROW0 + 1, 1, window)
    return 1.0 / cnt.astype(F32)


def pool_fwd(h, gamma, wp, scale, name):
    r, d = h.shape
    ng, cg, _ = wp.shape
    tm = _row_tile(r)
    hb = tm // HALO

    def body(h_ref, hp_ref, g_ref, w_ref, sc_ref, o_ref):
        i = pl.program_id(0)
        xm = h_ref[...]
        xp = hp_ref[...] * jnp.where(i > 0, 1.0, 0.0)
        xx = jnp.concatenate([xp, xm], axis=0)
        xh, _ = _rms(xx)
        u = xh * g_ref[...]
        for g, win in enumerate(POOL_WINDOWS):
            sl = slice(g * cg, (g + 1) * cg)
            ug = u[:, sl]
            acc, k = ug, 1
            while k < win:
                acc = acc + pltpu.roll(acc, k, 0)
                k *= 2
            pooled = acc[HALO:] * _pool_counts(i * tm, tm, win) - ug[HALO:]
            y = jnp.dot(pooled.astype(BF16), w_ref[g], preferred_element_type=F32)
            o_ref[:, sl] = xm[:, sl] + y * sc_ref[:, sl]

    return pl.pallas_call(
        body, name=name, out_shape=jax.ShapeDtypeStruct((r, d), F32), grid=(r // tm,),
        in_specs=[pl.BlockSpec((tm, d), lambda i: (i, 0)),
                  pl.BlockSpec((HALO, d), lambda i: (jnp.maximum(i * hb - 1, 0), 0)),
                  pl.BlockSpec((1, d), lambda i: (0, 0)), pl.BlockSpec((ng, cg, cg), lambda i: (0, 0, 0)),
                  pl.BlockSpec((1, d), lambda i: (0, 0))],
        out_specs=pl.BlockSpec((tm, d), lambda i: (i, 0)),
        compiler_params=_params(("arbitrary",), VMEM_BIG),
    )(h, h, gamma, wp, scale)


def pool_bwd(h, gamma, wp, scale, dh, name):
    r, d = h.shape
    ng, cg, _ = wp.shape
    tm = _row_tile(r)
    hb = tm // HALO
    nt = r // tm

    def body(h_ref, hp_ref, dh_ref, dn_ref, g_ref, w_ref, sc_ref, dx_ref, dgam_ref, dw_ref, dsc_ref, du_sc):
        i = pl.program_id(0)

        @pl.when(i == 0)
        def _():
            dgam_ref[...] = jnp.zeros_like(dgam_ref)
            dw_ref[...] = jnp.zeros_like(dw_ref)
            dsc_ref[...] = jnp.zeros_like(dsc_ref)

        xm = h_ref[...]
        xp = hp_ref[...] * jnp.where(i > 0, 1.0, 0.0)
        xh_all, rstd_all = _rms(jnp.concatenate([xp, xm], axis=0))
        u = xh_all * g_ref[...]
        dm = dh_ref[...]
        dn = dn_ref[...] * jnp.where(i < nt - 1, 1.0, 0.0)
        dd = jnp.concatenate([dm, dn], axis=0)
        for g, win in enumerate(POOL_WINDOWS):
            sl = slice(g * cg, (g + 1) * cg)
            ug = u[:, sl]
            acc, k = ug, 1
            while k < win:
                acc = acc + pltpu.roll(acc, k, 0)
                k *= 2
            pooled = (acc[HALO:] * _pool_counts(i * tm, tm, win) - ug[HALO:]).astype(BF16)
            y = jnp.dot(pooled, w_ref[g], preferred_element_type=F32)
            dsc_ref[:, sl] += jnp.sum(dm[:, sl] * y, axis=0, keepdims=True)
            dyp = (dd[:, sl] * sc_ref[:, sl]).astype(BF16)
            dw_ref[g] += lax.dot_general(pooled, dyp[:tm], TN, preferred_element_type=F32)
            dpo = lax.dot_general(dyp, w_ref[g], NT, preferred_element_type=F32)
            z = dpo * _pool_counts(i * tm, tm + HALO, win)
            fwd, k = z, 1
            while k < win:
                fwd = fwd + pltpu.roll(fwd, tm + HALO - k, 0)
                k *= 2
            du_sc[:, sl] = fwd[:tm] - dpo[:tm]
        du = du_sc[...]
        xh, rstd = xh_all[HALO:], rstd_all[HALO:]
        dgam_ref[...] += jnp.sum(du * xh, axis=0, keepdims=True)
        dx_ref[...] = dm + _rms_bwd(xh, rstd, du * g_ref[...])

    row = pl.BlockSpec((tm, d), lambda i: (i, 0))
    vec = pl.BlockSpec((1, d), lambda i: (0, 0))
    prev = pl.BlockSpec((HALO, d), lambda i: (jnp.maximum(i * hb - 1, 0), 0))
    nxt = pl.BlockSpec((HALO, d), lambda i: (jnp.minimum((i + 1) * hb, r // HALO - 1), 0))
    wsp = pl.BlockSpec((ng, cg, cg), lambda i: (0, 0, 0))
    return pl.pallas_call(
        body, name=name,
        out_shape=[jax.ShapeDtypeStruct((r, d), F32), jax.ShapeDtypeStruct((1, d), F32),
                   jax.ShapeDtypeStruct((ng, cg, cg), F32), jax.ShapeDtypeStruct((1, d), F32)],
        grid=(nt,), in_specs=[row, prev, row, nxt, vec, wsp, vec], out_specs=[row, vec, wsp, vec],
        scratch_shapes=[pltpu.VMEM((tm, d), F32)],
        compiler_params=_params(("arbitrary",), VMEM_BIG),
    )(h, h, dh, dh, gamma, wp, scale)


def loss_head(h, gamma, target, seq, name):
    r, d = h.shape
    tm = _row_tile(r)

    def body(h_ref, g_ref, t_ref, sse_ref, dh_ref, dgam_ref):
        i = pl.program_id(0)

        @pl.when(i == 0)
        def _():
            sse_ref[...] = jnp.zeros_like(sse_ref)
            dgam_ref[...] = jnp.zeros_like(dgam_ref)

        xh, rstd = _rms(h_ref[...])
        rows = i * tm + lax.broadcasted_iota(jnp.int32, (tm, 1), 0)
        valid = ((rows >= FRONT) & (rows < FRONT + seq)).astype(F32)
        e = (xh * g_ref[...] - t_ref[...]) * valid
        sse_ref[...] += jnp.sum(jnp.sum(e * e, axis=1, keepdims=True), axis=0, keepdims=True)
        dy = e * (1.0 / d)
        dgam_ref[...] += jnp.sum(dy * xh, axis=0, keepdims=True)
        dh_ref[...] = _rms_bwd(xh, rstd, dy * g_ref[...])

    row = pl.BlockSpec((tm, d), lambda i: (i, 0))
    vec = pl.BlockSpec((1, d), lambda i: (0, 0))
    return pl.pallas_call(
        body, name=name,
        out_shape=[jax.ShapeDtypeStruct((1, 1), F32), jax.ShapeDtypeStruct((r, d), F32),
                   jax.ShapeDtypeStruct((1, d), F32)],
        grid=(r // tm,), in_specs=[row, vec, row],
        out_specs=[pl.BlockSpec((1, 1), lambda i: (0, 0)), row, vec],
        compiler_params=_params(("arbitrary",)),
    )(h, gamma, target)


SMALL = ("meta_tokens", "pool_w", "pool_scale", "w_dkv", "w_uk", "w_uv", "w_dq", "w_uq", "w_o")


def _pack(parts):
    flat = jnp.concatenate([p.reshape(-1) for p in parts])
    n = flat.shape[0]
    unit = PACK_W * 2 * SUBLANES
    n_pad = -(-n // unit) * unit
    return jnp.pad(flat, (0, n_pad - n)).reshape(n_pad // PACK_W, PACK_W)


def _unpack(buf, shapes, lead=()):
    flat = buf.reshape(lead + (-1,))
    out, off = [], 0
    for shp in shapes:
        n = math.prod(shp)
        out.append(flat[..., off:off + n].reshape(lead + tuple(shp)))
        off += n
    return out


def _cols_from_shards(a, axis):
    a = jnp.moveaxis(a, 0, axis)
    shp = a.shape
    return a.reshape(shp[:axis] + (shp[axis] * shp[axis + 1],) + shp[axis + 2:])


def _cols_to_shards(a, axis):
    shp = a.shape
    a = a.reshape(shp[:axis] + (N_SHARD, shp[axis] // N_SHARD) + shp[axis + 1:])
    return jnp.moveaxis(a, axis, 0)


SMALL_AXIS = {"meta_tokens": 1, "pool_w": 2, "pool_scale": 1, "w_dkv": 0, "w_uk": 1, "w_uv": 1,
              "w_dq": 1, "w_uq": 2, "w_o": 2}


def kernel(x, meta_tokens, ffn1_norm, ffn1_w_gate, ffn1_w_up, ffn1_w_down, mix_norm, ffn2_norm, ffn2_w_gate, ffn2_w_up, ffn2_w_down, pool_w, pool_scale, kv_in_norm, w_dkv, kv_latent_norm, w_uk, w_uv, w_dq, q_latent_norm, w_uq, w_o, final_norm, loss_target, m_meta_tokens, m_ffn1_norm, m_ffn1_w_gate, m_ffn1_w_up, m_ffn1_w_down, m_mix_norm, m_ffn2_norm, m_ffn2_w_gate, m_ffn2_w_up, m_ffn2_w_down, m_pool_w, m_pool_scale, m_kv_in_norm, m_w_dkv, m_kv_latent_norm, m_w_uk, m_w_uv, m_w_dq, m_q_latent_norm, m_w_uq, m_w_o, m_final_norm, v_meta_tokens, v_ffn1_norm, v_ffn1_w_gate, v_ffn1_w_up, v_ffn1_w_down, v_mix_norm, v_ffn2_norm, v_ffn2_w_gate, v_ffn2_w_up, v_ffn2_w_down, v_pool_w, v_pool_scale, v_kv_in_norm, v_w_dkv, v_kv_latent_norm, v_w_uk, v_w_uv, v_w_dq, v_q_latent_norm, v_w_uq, v_w_o, v_final_norm):
    args = dict(locals())
    W = {n: args[n] for n in NAMES}
    M = {n: args["m_" + n] for n in NAMES}
    V = {n: args["v_" + n] for n in NAMES}
    TRANSPOSED = ("ffn1_w_gate", "ffn1_w_up", "ffn2_w_gate", "ffn2_w_up")
    for n in TRANSPOSED:
        W[n], M[n], V[n] = (jnp.swapaxes(a, 1, 2) for a in (W[n], M[n], V[n]))

    depth = ffn1_norm.shape[0]
    n_a = pool_w.shape[0]
    seq, d = x.shape[1], x.shape[2]
    nh = N_HEADS
    r = -(-(FRONT + seq) // LANES) * LANES

    cx, cy, cc = lax.axis_index("x"), lax.axis_index("y"), lax.axis_index("c")
    c_arr = jnp.reshape(cc, (1,)).astype(jnp.int32)
    s_arr = jnp.reshape(2 * cx + cy, (1,)).astype(jnp.int32)

    small_shapes = [W[n].shape for n in SMALL]

    ffn_src = {f: tuple(W[f + t].astype(BF16) for t in ("_w_gate", "_w_up", "_w_down")) for f in ("ffn1", "ffn2")}
    ffn_order = [(f, l) for l in range(depth) for f in ("ffn1", "ffn2")]
    ffn_w = {}
    gate = []
    ag_state = [None]

    def gated(a):
        if gate:
            a = a + sum(gate[1:], gate[0]).astype(a.dtype)
            gate.clear()
        return a

    def vec(a):
        return gated(a.reshape(1, -1))

    def ag_start(idx, dep):
        f, l = ffn_order[idx]
        shards = [w_[l] + dep for w_ in ffn_src[f]]
        lands = [lax.dynamic_update_slice(lax.empty((N_SHARD,) + a.shape, BF16), a[None], (2 * cx + cy, 0, 0))
                 for a in shards]
        ag_state[0] = exchange_start(shards, lands, f"ag_start_{f}_{l}", True)
        gate.append(ag_state[0][4][0, 0])

    def ag_wait(idx, after):
        f, l = ffn_order[idx]
        _, lands = exchange_wait(ag_state[0], after, f"ag_wait_{f}_{l}", True)
        ffn_w[f, l] = lands
        if idx + 1 < len(ffn_order):
            ag_start(idx + 1, lands[0][0, 0, 0] * jnp.zeros((), BF16))

    gathered = all_gather_shards([_pack([W[n] for n in SMALL])], "ag_small")[0]
    ag_start(0, (gathered[0, 0, 0] * 0.0).astype(BF16))
    small_full = {}
    for n, part in zip(SMALL, _unpack(gathered, small_shapes, (N_SHARD,))):
        small_full[n] = _cols_from_shards(part, SMALL_AXIS[n])

    meta_full = small_full["meta_tokens"]
    wp = small_full["pool_w"].astype(BF16)
    pscale = small_full["pool_scale"]
    wdkv = jnp.pad(small_full["w_dkv"], ((0, 0), (0, HEAD_W - QK_ROPE))).astype(BF16)[None]
    wuk = small_full["w_uk"].reshape(KV_RANK, nh, QK_NOPE).transpose(1, 0, 2)
    wk_h = jnp.concatenate([jnp.zeros((nh, KV_RANK, HEAD_W - QK_NOPE), F32), wuk], axis=-1).astype(BF16)
    wv_h = small_full["w_uv"].reshape(KV_RANK, nh, V_HEAD).transpose(1, 0, 2).astype(BF16)
    wdq = small_full["w_dq"].astype(BF16)
    wuq = small_full["w_uq"].reshape(-1, Q_RANK, nh, QK_NOPE + QK_ROPE).transpose(0, 2, 1, 3)
    wq_h = jnp.concatenate([wuq[..., QK_NOPE:], jnp.zeros(wuq.shape[:-1] + (HEAD_W - QK_NOPE - QK_ROPE,), F32),
                            wuq[..., :QK_NOPE]], axis=-1).astype(BF16)
    wo_h = small_full["w_o"].reshape(-1, nh, V_HEAD, d).astype(BF16)

    ctab, stab = rope_tables(r)

    h = jnp.concatenate([jnp.zeros((META_ROW0, d), F32), meta_full, x[0],
                         jnp.zeros((r - FRONT - seq, d), F32)], axis=0)
    target = jnp.concatenate([jnp.zeros((FRONT, d), F32), loss_target[0],
                              jnp.zeros((r - FRONT - seq, d), F32)], axis=0)
    saved = []
    kv = None
    for l in range(depth):
        sv = {"h0": h}
        ag_wait(2 * l, h)
        h, sv["g1"], sv["u1"] = ffn_fwd(h, vec(ffn1_norm[l]), *ffn_w["ffn1", l], f"ffn1_fwd_{l}")
        sv["h1"] = h
        if l < n_a:
            h = pool_fwd(h, vec(mix_norm[l]), wp[l], vec(pscale[l]), f"pool_fwd_{l}")
        else:
            j = l - n_a
            u = norm_fwd(h, vec(mix_norm[l]), f"mixnorm_{l}")
            cq0 = rowmm(u[None], wdq[j][None], f"dq_{l}")
            cq = norm_fwd(cq0, vec(q_latent_norm[j]), f"qnorm_{l}")
            q = proj_rope(cq, wq_h[j], ctab, stab, None, f"qproj_{l}", scale=Q_SCALE)
            o, lse = attn_fwd(q, kv["k"], kv["v"], f"attn_fwd_{l}")
            h = rowmm(o, wo_h[j], f"oproj_{l}", res=h)
            sv.update(u=u, cq0=cq0, cq=cq, q=q, o=o, lse=lse)
        sv["h2"] = h
        ag_wait(2 * l + 1, h)
        h, sv["g2"], sv["u2"] = ffn_fwd(h, vec(ffn2_norm[l]), *ffn_w["ffn2", l], f"ffn2_fwd_{l}")
        saved.append(sv)
        if l == n_a - 1:
            hkv = norm_fwd(h, vec(kv_in_norm), "kvin_norm")
            ckr = rowmm(hkv[None], wdkv, "dkv")
            ckv = norm_fwd(ckr, vec(kv_latent_norm), "kvlat_norm")
            kv = {"h": h, "hkv": hkv, "ckr": ckr, "ckv": ckv,
                  "k": proj_rope(ckv, wk_h, ctab, stab, ckr, "kproj"),
                  "v": rowmm(ckv[None], wv_h, "vproj", out_dtype=BF16, heads_out=True)}

    sse, dh, dfinal = loss_head(h, vec(final_norm), target, seq, "loss_head")
    loss = lax.psum(0.5 / d * sse[0, 0], ("x", "y", "c"))

    G = {}
    FFN = ("ffn1_w_gate", "ffn1_w_up", "ffn1_w_down", "ffn2_w_gate", "ffn2_w_up", "ffn2_w_down")
    per = {n: [lax.empty(W[n].shape, F32) for _ in range(4)] for n in FFN}
    pending = []

    def rs_complete(after):
        f, l, state = pending.pop()
        names = [f + "_w_gate", f + "_w_up", f + "_w_down"]
        for n, g_ in zip(names, rs_finish(state, after, s_arr, c_arr, f"{f}_{l}")):
            per[n] = adamw_into(W[n], g_, M[n], V[n], per[n], l, f"adamw_{n}_{l}")
        gate.append(per[names[-1]][1][0, 0, 0] * 0.0)

    dnorm = {n: [None] * depth for n in ("ffn1_norm", "mix_norm", "ffn2_norm")}
    dqnorm = [None] * (depth - n_a)
    dpool_w, dpool_scale = [None] * n_a, [None] * n_a
    dwdq, dwq_h, dwo_h = [None] * (depth - n_a), [None] * (depth - n_a), [None] * (depth - n_a)

    def ffn_backward(f, l, h_in, dh_, gg, uu):
        gam = ffn1_norm[l] if f == "ffn1" else ffn2_norm[l]
        wg_, wu_, wd_ = ffn_w[f, l]
        dh_in, dgam, hn, dy, dg, du, a = ffn_bwd_act(h_in, vec(gam), dh_, gg, uu, wg_, wu_, wd_, f"{f}_bwd_act_{l}")
        dwg, dwu, dwd = ffn_bwd_weights(hn, dy, a, dg, du, f"{f}_bwd_w_{l}")
        state = rs_begin([dwg, dwu, dwd], f"{f}_{l}")
        gate.append(state[4][0, 0])
        if pending:
            rs_complete(state[4])
        pending.append((f, l, state))
        dnorm[f + "_norm"][l] = dgam[0]
        return dh_in

    dk_tot = dv_tot = None
    for l in reversed(range(depth)):
        sv = saved[l]
        if l == n_a - 1:
            dckv = rowmm(dk_tot, gated(wk_h), "kproj_bwd", nt=True)
            dkr = rope_bwd_sum(dk_tot, ctab, stab, "kproj_bwd_rope")
            dckv = rowmm(dv_tot, wv_h, "vproj_bwd", nt=True, res=dckv)
            dwk_h = tnmm(kv["ckv"][None], dk_tot, "kproj_bwd_w")
            dwv_h = tnmm(kv["ckv"][None], dv_tot, "vproj_bwd_w")
            dlat, dkvlat = norm_bwd(kv["ckr"], vec(kv_latent_norm), dckv, None, "kvlat_norm_bwd")
            dckr = jnp.concatenate([dlat, dkr], axis=1).astype(BF16)
            dhkv = rowmm(dckr[None], wdkv, "dkv_bwd", nt=True)
            dwdkv = tnmm(kv["hkv"][None], dckr[None], "dkv_bwd_w")[0]
            dh, dkvin = norm_bwd(kv["h"], vec(kv_in_norm), dhkv, dh, "kvin_norm_bwd")
            G["w_dkv"] = dwdkv[:, :KV_RANK + QK_ROPE]
            G["w_uk"] = dwk_h[..., HEAD_W - QK_NOPE:].transpose(1, 0, 2).reshape(KV_RANK, nh * QK_NOPE)
            G["w_uv"] = dwv_h.transpose(1, 0, 2).reshape(KV_RANK, nh * V_HEAD)
        dh = ffn_backward("ffn2", l, sv["h2"], dh, sv["g2"], sv["u2"])
        if l < n_a:
            dh, dmix, dpool_w[l], dps = pool_bwd(sv["h1"], vec(mix_norm[l]), wp[l], vec(pscale[l]), dh, f"pool_bwd_{l}")
            dnorm["mix_norm"][l] = dmix[0]
            dpool_scale[l] = dps[0]
        else:
            j = l - n_a
            do, delta = attn_out_bwd(dh, gated(wo_h[j]), sv["o"], f"oproj_bwd_{l}")
            dwo_h[j] = tnmm(sv["o"], dh[None], f"oproj_bwd_w_{l}")
            dq, dk_tot, dv_tot = attn_bwd(sv["q"], kv["k"], kv["v"], do, sv["lse"], delta, dk_tot, dv_tot,
                                          f"attn_bwd_{l}")
            dxq = rope_bwd_heads(dq, ctab, stab, f"qproj_bwd_rope_{l}", scale=SM_SCALE)
            dcq = rowmm(dxq, wq_h[j], f"qproj_bwd_{l}", nt=True)
            dwq_h[j] = tnmm(sv["cq"][None], dxq, f"qproj_bwd_w_{l}")
            dcq0, dqn = norm_bwd(sv["cq0"], vec(q_latent_norm[j]), dcq, None, f"qnorm_bwd_{l}")
            dqnorm[j] = dqn[0]
            dcq0b = dcq0.astype(BF16)
            du = rowmm(dcq0b[None], wdq[j][None], f"dq_bwd_{l}", nt=True)
            dwdq[j] = tnmm(sv["u"][None], dcq0b[None], f"dq_bwd_w_{l}")[0]
            dh, dmix = norm_bwd(sv["h1"], vec(mix_norm[l]), du, dh, f"mixnorm_bwd_{l}")
            dnorm["mix_norm"][l] = dmix[0]
        dh = ffn_backward("ffn1", l, sv["h0"], dh, sv["g1"], sv["u1"])

    grad_x = dh[FRONT:FRONT + seq][None]
    G["meta_tokens"] = dh[META_ROW0:FRONT]
    G["pool_w"] = jnp.stack(dpool_w)
    G["pool_scale"] = jnp.stack(dpool_scale)
    G["w_dq"] = jnp.stack(dwdq)
    dwq = jnp.stack(dwq_h)
    dwq = jnp.concatenate([dwq[..., HEAD_W - QK_NOPE:], dwq[..., :QK_ROPE]], axis=-1)
    G["w_uq"] = dwq.transpose(0, 2, 1, 3).reshape(-1, Q_RANK, nh * (QK_NOPE + QK_ROPE))
    G["w_o"] = jnp.stack(dwo_h).reshape(-1, nh * V_HEAD, d)

    REPL = ("ffn1_norm", "mix_norm", "ffn2_norm", "kv_in_norm", "kv_latent_norm", "q_latent_norm", "final_norm")
    grep = {"ffn1_norm": jnp.stack(dnorm["ffn1_norm"]), "mix_norm": jnp.stack(dnorm["mix_norm"]),
            "ffn2_norm": jnp.stack(dnorm["ffn2_norm"]), "kv_in_norm": dkvin[0], "kv_latent_norm": dkvlat[0],
            "q_latent_norm": jnp.stack(dqnorm), "final_norm": dfinal[0]}

    def pack128(parts):
        flat = jnp.concatenate([p.reshape(-1) for p in parts])
        n = flat.shape[0]
        n_pad = -(-n // (LANES * SUBLANES)) * (LANES * SUBLANES)
        return jnp.pad(flat, (0, n_pad - n)).reshape(-1, LANES)

    rep_shapes = [W[n].shape for n in REPL]
    g_rep = all_reduce_small(pack128([grep[n] for n in REPL]), "ar_repl")
    d_rep, m_rep, v_rep = adamw(pack128([W[n] for n in REPL]), g_rep, pack128([M[n] for n in REPL]),
                                pack128([V[n] for n in REPL]), "adamw_repl")
    out_g, out_d, out_m, out_v = {}, {}, {}, {}
    for dst, buf in ((out_g, g_rep), (out_d, d_rep), (out_m, m_rep), (out_v, v_rep)):
        for n, a in zip(REPL, _unpack(buf, rep_shapes)):
            dst[n] = a

    g_small = jnp.stack([_pack([_cols_to_shards(G[n], SMALL_AXIS[n])[s] for n in SMALL]) for s in range(N_SHARD)])
    g_small = reduce_scatter([gated(g_small)], c_arr, s_arr, "small")[0]
    d_s, m_s, v_s = adamw(_pack([W[n] for n in SMALL]), g_small, _pack([M[n] for n in SMALL]),
                          _pack([V[n] for n in SMALL]), "adamw_small")
    for dst, buf in ((out_g, g_small), (out_d, d_s), (out_m, m_s), (out_v, v_s)):
        for n, a in zip(SMALL, _unpack(buf, small_shapes)):
            dst[n] = a

    rs_complete(d_s)
    for n in FFN:
        out_g[n], out_d[n], out_m[n], out_v[n] = (
            jnp.swapaxes(a, 1, 2) if n in TRANSPOSED else a for a in per[n])

    return (loss, grad_x, *[out_g[n] for n in NAMES], *[out_d[n] for n in NAMES],
            *[out_m[n] for n in NAMES], *[out_v[n] for n in NAMES])


NAMES = ("meta_tokens", "ffn1_norm", "ffn1_w_gate", "ffn1_w_up", "ffn1_w_down", "mix_norm", "ffn2_norm",
         "ffn2_w_gate", "ffn2_w_up", "ffn2_w_down", "pool_w", "pool_scale", "kv_in_norm", "w_dkv",
         "kv_latent_norm", "w_uk", "w_uv", "w_dq", "q_latent_norm", "w_uq", "w_o", "final_norm")
```

```python
import functools
import math

import jax
import jax.numpy as jnp
from jax import lax
from jax.experimental import pallas as pl
from jax.experimental.pallas import tpu as pltpu

F32 = jnp.float32
BF16 = jnp.bfloat16
MESH = pl.DeviceIdType.MESH
ANY = pl.BlockSpec(memory_space=pl.ANY)

EPS = 1e-6
CHUNK = 64
CHUNK_SHIFT = 6
N_META = 16
FRONT = 64
META_ROW0 = FRONT - N_META
POOL_WINDOWS = (2, 4, 8, 16)
HALO = 16
N_HEADS = 8
QK_NOPE = 64
QK_ROPE = 32
V_HEAD = 64
HEAD_W = 128
KV_RANK = 256
Q_RANK = 384
ROPE_THETA = 10000.0
NEG = -1e30
N_SHARD = 4
LANES = 128
SUBLANES = 8
PACK_W = 512
VMEM_BIG = 52 * 1024 * 1024
VMEM_MAX = 60 * 1024 * 1024
WGRAD_ROWS = 1664
FFN_FWD_SHARDS = 2
FFN_BWD_SHARDS = 2
FFN_BWD_ROW_SPLIT = 2

ADAM_LR = 0.001
ADAM_B1 = 0.9
ADAM_B2 = 0.999
ADAM_EPS = 1e-08
ADAM_WD = 0.01
ADAM_STEP = 10

NT = (((1,), (1,)), ((), ()))
TN = (((0,), (0,)), ((), ()))


def _params(sem=None, vmem=None):
    return pltpu.CompilerParams(dimension_semantics=sem, vmem_limit_bytes=vmem)


def _tile(n, pref, mult=SUBLANES):
    best = None
    for t in range(mult, min(n, pref) + 1, mult):
        if n % t == 0:
            best = t
    return best if best is not None else n


def _row_tile(r):
    return 640 if r % 640 == 0 else 128


def _rms(x):
    rstd = lax.rsqrt(jnp.mean(x * x, axis=-1, keepdims=True) + EPS)
    return x * rstd, rstd


def _rms_bwd(xh, rstd, dxh):
    return rstd * (dxh - xh * jnp.mean(dxh * xh, axis=-1, keepdims=True))


def _sigmoid(x):
    return 1.0 / (1.0 + jnp.exp(-x))


def _place():
    x, y, c = lax.axis_index("x"), lax.axis_index("y"), lax.axis_index("c")
    chips = [(1 - x, y), (x, 1 - y), (1 - x, 1 - y)]
    return x, y, c, chips


def all_gather_shards(shards, name):
    n = len(shards)
    slot = 2 * lax.axis_index("x") + lax.axis_index("y")
    lands = [lax.dynamic_update_slice(lax.empty((N_SHARD,) + a.shape, a.dtype), a[None], (slot, 0, 0)) for a in shards]

    def body(*refs):
        ins, outs = refs[:n], refs[2 * n:3 * n]
        send1, recv1, send2, recv2 = refs[3 * n:]
        x, y, c, chips = _place()
        s = 2 * x + y
        sib = (x, y, 1 - c)

        def rcopy(k, j, src, dst, to, first):
            return pltpu.make_async_remote_copy(
                src_ref=src, dst_ref=dst,
                send_sem=(send1 if first else send2).at[k, j],
                recv_sem=(recv1 if first else recv2).at[k, j],
                device_id=to, device_id_type=MESH)

        started = []
        for k in range(n):
            hf = ins[k].shape[0] // 2
            for j, (cx, cy) in enumerate(chips):
                r = rcopy(k, j, ins[k].at[pl.ds(c * hf, hf)], outs[k].at[s, pl.ds(c * hf, hf)],
                          (cx, cy, c), True)
                r.start()
                started.append(r)
        for k in range(n):
            hf = ins[k].shape[0] // 2
            for j, (cx, cy) in enumerate(chips):
                blk = outs[k].at[2 * cx + cy, pl.ds(c * hf, hf)]
                rcopy(k, j, blk, blk, (cx, cy, c), True).wait_recv()
                f = rcopy(k, j, blk, blk, sib, False)
                f.start()
                started.append(f)
        for k in range(n):
            hf = ins[k].shape[0] // 2
            for j, (cx, cy) in enumerate(chips):
                blk = outs[k].at[2 * cx + cy, pl.ds((1 - c) * hf, hf)]
                rcopy(k, j, blk, blk, sib, False).wait_recv()
        for r in started:
            r.wait_send()

    return pl.pallas_call(
        body, name=name,
        out_shape=[jax.ShapeDtypeStruct((N_SHARD,) + a.shape, a.dtype) for a in shards],
        in_specs=[ANY] * (2 * n), out_specs=[ANY] * n,
        input_output_aliases={n + k: k for k in range(n)},
        scratch_shapes=[pltpu.SemaphoreType.DMA((n, 3))] * 4,
    )(*shards, *lands)


def sibling_join_halves(ts, name):
    n = len(ts)

    def body(*refs):
        ins, outs = refs[:n], refs[n:2 * n]
        send, recv = refs[2 * n:]
        x, y, c, _ = _place()
        sib = (x, y, 1 - c)

        def copy(k, half):
            hf = ins[k].shape[0] // 2
            rows = pl.ds(half * hf, hf)
            return pltpu.make_async_remote_copy(
                src_ref=ins[k].at[rows], dst_ref=outs[k].at[rows],
                send_sem=send.at[k], recv_sem=recv.at[k], device_id=sib, device_id_type=MESH)

        cps = [copy(k, c) for k in range(n)]
        for r in cps:
            r.start()
        for k in range(n):
            copy(k, 1 - c).wait_recv()
        for r in cps:
            r.wait_send()

    return pl.pallas_call(
        body, name=name,
        out_shape=[jax.ShapeDtypeStruct(a.shape, a.dtype) for a in ts],
        in_specs=[ANY] * n, out_specs=[ANY] * n, input_output_aliases={k: k for k in range(n)},
        scratch_shapes=[pltpu.SemaphoreType.DMA((n,))] * 2,
    )(*ts)


def all_reduce_small(part, name):
    m, w = part.shape

    def body(x_ref, tot_ref, gat_ref, send_sems, recv_sems):
        x, y, c, chips = _place()
        me, sib = (x, y, c), (x, y, 1 - c)

        def slot(px, py, pc):
            return gat_ref.at[4 * px + 2 * py + pc]

        def copy(k, block, to, src=None):
            return pltpu.make_async_remote_copy(
                src_ref=slot(*block) if src is None else src, dst_ref=slot(*block),
                send_sem=send_sems.at[k], recv_sem=recv_sems.at[k], device_id=to, device_id_type=MESH)

        gat_ref[4 * x + 2 * y + c] = x_ref[...]
        first = [copy(0, me, sib, src=x_ref)]
        first += [copy(1 + j, me, (*chip, c), src=x_ref) for j, chip in enumerate(chips)]
        for cp in first:
            cp.start()
        passed = [copy(4 + j, (*chip, c), sib) for j, chip in enumerate(chips)]
        for j, chip in enumerate(chips):
            copy(1 + j, (*chip, c), me).wait_recv()
            passed[j].start()
        copy(0, sib, me).wait_recv()
        for j, chip in enumerate(chips):
            copy(4 + j, (*chip, 1 - c), me).wait_recv()
        for cp in first + passed:
            cp.wait_send()
        tot = gat_ref[0]
        for d in range(1, 8):
            tot = tot + gat_ref[d]
        tot_ref[...] = tot

    return pl.pallas_call(
        body, name=name,
        out_shape=jax.ShapeDtypeStruct((m, w), F32),
        in_specs=[pl.BlockSpec(memory_space=pltpu.VMEM)],
        out_specs=pl.BlockSpec(memory_space=pltpu.VMEM),
        scratch_shapes=[pltpu.VMEM((8, m, w), F32), pltpu.SemaphoreType.DMA((7,)), pltpu.SemaphoreType.DMA((7,))],
    )(part)


def adamw(w, g, m, v, name):
    a, b = w.shape
    tb = _tile(a, 256)
    c1 = 1.0 - ADAM_B1 ** ADAM_STEP
    c2 = 1.0 - ADAM_B2 ** ADAM_STEP

    def body(w_ref, g_ref, m_ref, v_ref, d_ref, mo_ref, vo_ref):
        g_ = g_ref[...]
        m_ = ADAM_B1 * m_ref[...] + (1.0 - ADAM_B1) * g_
        v_ = ADAM_B2 * v_ref[...] + (1.0 - ADAM_B2) * (g_ * g_)
        m_hat = m_ / c1
        v_hat = v_ / c2
        d_ref[...] = -ADAM_LR * (m_hat / (jnp.sqrt(v_hat) + ADAM_EPS) + ADAM_WD * w_ref[...])
        mo_ref[...] = m_
        vo_ref[...] = v_

    spec = pl.BlockSpec((tb, b), lambda i: (i, 0))
    return pl.pallas_call(
        body, name=name,
        out_shape=[jax.ShapeDtypeStruct((a, b), F32)] * 3,
        grid=(a // tb,), in_specs=[spec] * 4, out_specs=[spec] * 3,
        compiler_params=_params(("arbitrary",)),
    )(w, g, m, v)


def adamw_into(w_all, g, m_all, v_all, prev, l, name):
    nl, a, b = w_all.shape
    tb = _tile(a, 256)
    c1 = 1.0 - ADAM_B1 ** ADAM_STEP
    c2 = 1.0 - ADAM_B2 ** ADAM_STEP

    def body(w_ref, g_ref, m_ref, v_ref, p0, p1, p2, p3, go_ref, d_ref, mo_ref, vo_ref):
        g_ = g_ref[...]
        m_ = ADAM_B1 * m_ref[0] + (1.0 - ADAM_B1) * g_
        v_ = ADAM_B2 * v_ref[0] + (1.0 - ADAM_B2) * (g_ * g_)
        m_hat = m_ / c1
        v_hat = v_ / c2
        go_ref[0] = g_
        d_ref[0] = -ADAM_LR * (m_hat / (jnp.sqrt(v_hat) + ADAM_EPS) + ADAM_WD * w_ref[0])
        mo_ref[0] = m_
        vo_ref[0] = v_

    lay = pl.BlockSpec((1, tb, b), lambda i: (l, i, 0))
    return pl.pallas_call(
        body, name=name,
        out_shape=[jax.ShapeDtypeStruct((nl, a, b), F32)] * 4,
        grid=(a // tb,), in_specs=[lay, pl.BlockSpec((tb, b), lambda i: (i, 0)), lay, lay] + [ANY] * 4,
        out_specs=[lay] * 4, input_output_aliases={4: 0, 5: 1, 6: 2, 7: 3},
        compiler_params=_params(("arbitrary",)),
    )(w_all, g, m_all, v_all, *prev)


def reduce_scatter(gs, c_arr, s_arr, tag):
    lands = rs_exchange(gs, f"rs_exchange_{tag}")
    ts = [sum_eight(g, r, s_arr, c_arr, f"rs_sum_{tag}_{k}") for k, (g, r) in enumerate(zip(gs, lands))]
    return sibling_join_halves(ts, f"rs_join_{tag}")


HBM_SPEC = pl.BlockSpec(memory_space=pltpu.HBM)
SEM_SPEC = pl.BlockSpec(memory_space=pltpu.SEMAPHORE)
EFFECT = pltpu.SideEffectType.DATAFLOW_SIDE_EFFECTING


def _in_hbm(a):
    return pltpu.with_memory_space_constraint(a, pltpu.HBM)


def _exchange_copy(k, j, chip, c, s, srcs, lands, send, recv, gather, receiving):
    cx, cy = chip
    src = srcs[k] if gather else srcs[k].at[2 * cx + cy]
    if gather:
        dst = lands[k].at[2 * cx + cy] if receiving else lands[k].at[s]
    else:
        dst = lands[k].at[j]
    return pltpu.make_async_remote_copy(src_ref=src, dst_ref=dst, send_sem=send.at[3 * k + j], recv_sem=recv.at[3 * k + j],
                                        device_id=(cx, cy, c), device_id_type=MESH)


def exchange_start(srcs, lands, name, gather):
    n = len(srcs)

    def body(*refs):
        srcs_in, lands_in = refs[:n], refs[n:2 * n]
        send, recv = refs[2 * n], refs[2 * n + 1]
        token = refs[-1]
        x, y, c, chips = _place()
        for k in range(n):
            for j, chip in enumerate(chips):
                _exchange_copy(k, j, chip, c, 2 * x + y, srcs_in, lands_in, send, recv, gather, False).start()
        token[...] = jnp.zeros_like(token)

    outs = pl.pallas_call(
        body, name=name,
        out_shape=(pltpu.SemaphoreType.DMA((3 * n,)), pltpu.SemaphoreType.DMA((3 * n,)),
                   *[pltpu.HBM(a.shape, a.dtype) for a in srcs], *[pltpu.HBM(a.shape, a.dtype) for a in lands],
                   jax.ShapeDtypeStruct((SUBLANES, LANES), F32)),
        in_specs=[HBM_SPEC] * (2 * n),
        out_specs=(SEM_SPEC, SEM_SPEC, *[HBM_SPEC] * (2 * n), pl.BlockSpec(memory_space=pltpu.VMEM)),
        input_output_aliases={k: 2 + k for k in range(2 * n)},
        compiler_params=pltpu.CompilerParams(has_side_effects=EFFECT),
    )(*[_in_hbm(a) for a in srcs], *[_in_hbm(a) for a in lands])
    return outs[0], outs[1], list(outs[2:2 + n]), list(outs[2 + n:2 + 2 * n]), outs[-1]


def exchange_wait(state, after, name, gather):
    send, recv, srcs, lands, _ = state
    n = len(srcs)

    def body(*refs):
        srcs_in, lands_in = refs[:n], refs[n:2 * n]
        send_, recv_ = refs[2 * n], refs[2 * n + 1]
        x, y, c, chips = _place()
        for k in range(n):
            for j, chip in enumerate(chips):
                cp = _exchange_copy(k, j, chip, c, 2 * x + y, srcs_in, lands_in, send_, recv_, gather, True)
                cp.wait_send()
                cp.wait_recv()

    outs = pl.pallas_call(
        body, name=name,
        out_shape=tuple(pltpu.HBM(a.shape, a.dtype) for a in srcs + lands),
        in_specs=[HBM_SPEC] * (2 * n) + [SEM_SPEC, SEM_SPEC, ANY],
        out_specs=tuple([HBM_SPEC] * (2 * n)),
        input_output_aliases={k: k for k in range(2 * n)},
        compiler_params=pltpu.CompilerParams(has_side_effects=EFFECT),
    )(*srcs, *lands, send, recv, after)
    return list(outs[:n]), list(outs[n:])


RS_SLOTS = 7


def _rs_copies(k, gs, lands, send, recv, x, y, c, chips):
    hf = gs[k].shape[1] // 2
    base = RS_SLOTS * k
    out = []
    for j, (cx, cy) in enumerate(chips):
        for cd in range(2):
            out.append(pltpu.make_async_remote_copy(
                src_ref=gs[k].at[2 * cx + cy, pl.ds(cd * hf, hf)], dst_ref=lands[k].at[2 * j + c],
                send_sem=send.at[base + 2 * j + cd], recv_sem=recv.at[base + 2 * j + c],
                device_id=(cx, cy, cd), device_id_type=MESH))
    out.append(pltpu.make_async_remote_copy(
        src_ref=gs[k].at[2 * x + y, pl.ds((1 - c) * hf, hf)], dst_ref=lands[k].at[RS_SLOTS - 1],
        send_sem=send.at[base + RS_SLOTS - 1], recv_sem=recv.at[base + RS_SLOTS - 1],
        device_id=(x, y, 1 - c), device_id_type=MESH))
    return out


def _rs_wait_all(n, gs, lands, send, recv):
    x, y, c, chips = _place()
    for k in range(n):
        for cp in _rs_copies(k, gs, lands, send, recv, x, y, c, chips):
            cp.wait_send()
        hf = gs[k].shape[1] // 2
        for slot in range(RS_SLOTS):
            pltpu.make_async_remote_copy(
                src_ref=gs[k].at[0, pl.ds(0, hf)], dst_ref=lands[k].at[slot],
                send_sem=send.at[RS_SLOTS * k + slot], recv_sem=recv.at[RS_SLOTS * k + slot],
                device_id=(x, y, 1 - c), device_id_type=MESH).wait_recv()


def _rs_land_shapes(gs):
    return [(RS_SLOTS, g.shape[1] // 2, g.shape[2]) for g in gs]


def rs_exchange(gs, name):
    n = len(gs)

    def body(*refs):
        ins, outs = refs[:n], refs[n:2 * n]
        send, recv = refs[2 * n:]
        x, y, c, chips = _place()
        for k in range(n):
            for cp in _rs_copies(k, ins, outs, send, recv, x, y, c, chips):
                cp.start()
        _rs_wait_all(n, ins, outs, send, recv)

    return pl.pallas_call(
        body, name=name,
        out_shape=[jax.ShapeDtypeStruct(s, F32) for s in _rs_land_shapes(gs)],
        in_specs=[ANY] * n, out_specs=[ANY] * n,
        scratch_shapes=[pltpu.SemaphoreType.DMA((RS_SLOTS * n,))] * 2,
    )(*gs)


def rs_exchange_start(gs, name):
    n = len(gs)
    lands = [lax.empty(s, F32) for s in _rs_land_shapes(gs)]

    def body(*refs):
        ins, lands_in = refs[:n], refs[n:2 * n]
        send, recv = refs[2 * n], refs[2 * n + 1]
        token = refs[-1]
        x, y, c, chips = _place()
        for k in range(n):
            for cp in _rs_copies(k, ins, lands_in, send, recv, x, y, c, chips):
                cp.start()
        token[...] = jnp.zeros_like(token)

    outs = pl.pallas_call(
        body, name=name,
        out_shape=(pltpu.SemaphoreType.DMA((RS_SLOTS * n,)), pltpu.SemaphoreType.DMA((RS_SLOTS * n,)),
                   *[pltpu.HBM(a.shape, a.dtype) for a in gs], *[pltpu.HBM(a.shape, a.dtype) for a in lands],
                   jax.ShapeDtypeStruct((SUBLANES, LANES), F32)),
        in_specs=[HBM_SPEC] * (2 * n),
        out_specs=(SEM_SPEC, SEM_SPEC, *[HBM_SPEC] * (2 * n), pl.BlockSpec(memory_space=pltpu.VMEM)),
        input_output_aliases={k: 2 + k for k in range(2 * n)},
        compiler_params=pltpu.CompilerParams(has_side_effects=EFFECT),
    )(*[_in_hbm(a) for a in gs], *[_in_hbm(a) for a in lands])
    return outs[0], outs[1], list(outs[2:2 + n]), list(outs[2 + n:2 + 2 * n]), outs[-1]


def rs_exchange_wait(state, after, name):
    send, recv, gs, lands, _ = state
    n = len(gs)

    def body(*refs):
        _rs_wait_all(n, refs[:n], refs[n:2 * n], refs[2 * n], refs[2 * n + 1])

    outs = pl.pallas_call(
        body, name=name,
        out_shape=tuple(pltpu.HBM(a.shape, a.dtype) for a in gs + lands),
        in_specs=[HBM_SPEC] * (2 * n) + [SEM_SPEC, SEM_SPEC, ANY],
        out_specs=tuple([HBM_SPEC] * (2 * n)),
        input_output_aliases={k: k for k in range(2 * n)},
        compiler_params=pltpu.CompilerParams(has_side_effects=EFFECT),
    )(*gs, *lands, send, recv, after)
    return list(outs[:n]), list(outs[n:])


def sum_eight(g, land, s_arr, c_arr, name):
    _, a, b = g.shape
    ah = a // 2
    tb = _tile(ah, 256)
    nb = ah // tb

    def body(s_ref, c_ref, g_ref, *rest):
        lands_, o_ref = rest[:RS_SLOTS], rest[RS_SLOTS]
        tot = g_ref[0] + lands_[RS_SLOTS - 1][0]
        for slot in range(RS_SLOTS - 1):
            tot = tot + lands_[slot][0]
        o_ref[...] = tot

    def lspec(slot):
        return pl.BlockSpec((1, tb, b), lambda i, s, c: (slot, i, 0))

    return pl.pallas_call(
        body, name=name,
        out_shape=jax.ShapeDtypeStruct((a, b), F32),
        grid_spec=pltpu.PrefetchScalarGridSpec(
            num_scalar_prefetch=2, grid=(nb,),
            in_specs=[pl.BlockSpec((1, tb, b), lambda i, s, c: (s[0], c[0] * nb + i, 0))]
            + [lspec(slot) for slot in range(RS_SLOTS)],
            out_specs=pl.BlockSpec((tb, b), lambda i, s, c: (c[0] * nb + i, 0))),
        compiler_params=_params(("arbitrary",)),
    )(s_arr, c_arr, g, *([land] * RS_SLOTS))


def rs_begin(gs, tag):
    return rs_exchange_start(gs, f"rs_start_{tag}")


def rs_finish(state, after, s_arr, c_arr, tag):
    gs, lands = rs_exchange_wait(state, after, f"rs_wait_{tag}")
    ts = [sum_eight(g, r, s_arr, c_arr, f"rs_sum_{tag}_{k}") for k, (g, r) in enumerate(zip(gs, lands))]
    return sibling_join_halves(ts, f"rs_join_{tag}")


def ffn_fwd(h, gamma, wg, wu, wd, name):
    r, d = h.shape
    ns, fs, _ = wg.shape
    tm = _row_tile(r)
    sb = _tile(ns, FFN_FWD_SHARDS, 1)
    nsteps = ns // sb

    def body(h_ref, g_ref, wg_ref, wu_ref, wd_ref, ho_ref, gg_ref, uu_ref, hn_sc, acc_sc):
        s = pl.program_id(1)

        @pl.when(s == 0)
        def _():
            xh, _ = _rms(h_ref[...])
            hn_sc[...] = (xh * g_ref[...]).astype(BF16)
            acc_sc[...] = jnp.zeros_like(acc_sc)

        hn = hn_sc[...]
        part = None
        for t in range(sb):
            g = lax.dot_general(hn, wg_ref[t], NT, preferred_element_type=F32)
            u = lax.dot_general(hn, wu_ref[t], NT, preferred_element_type=F32)
            gg_ref[t] = g.astype(BF16)
            uu_ref[t] = u.astype(BF16)
            a = (g * _sigmoid(g) * u).astype(BF16)
            p = jnp.dot(a, wd_ref[t], preferred_element_type=F32)
            part = p if part is None else part + p
        acc_sc[...] += part

        @pl.when(s == nsteps - 1)
        def _():
            ho_ref[...] = h_ref[...] + 0.5 * acc_sc[...]

    wsp = pl.BlockSpec((sb, fs, d), lambda i, s: (s, 0, 0))
    act = pl.BlockSpec((sb, tm, fs), lambda i, s: (s, i, 0))
    return pl.pallas_call(
        body, name=name,
        out_shape=[jax.ShapeDtypeStruct((r, d), F32), jax.ShapeDtypeStruct((ns, r, fs), BF16),
                   jax.ShapeDtypeStruct((ns, r, fs), BF16)],
        grid=(r // tm, nsteps),
        in_specs=[pl.BlockSpec((tm, d), lambda i, s: (i, 0)), pl.BlockSpec((1, d), lambda i, s: (0, 0)), wsp, wsp, wsp],
        out_specs=[pl.BlockSpec((tm, d), lambda i, s: (i, 0)), act, act],
        scratch_shapes=[pltpu.VMEM((tm, d), BF16), pltpu.VMEM((tm, d), F32)],
        compiler_params=_params(("arbitrary", "arbitrary"), VMEM_MAX),
    )(h, gamma, wg, wu, wd)


def ffn_bwd_act(h, gamma, dh, gg, uu, wg, wu, wd, name):
    r, d = h.shape
    ns, fs, _ = wg.shape
    tm = _row_tile(r) // FFN_BWD_ROW_SPLIT
    sb = _tile(ns, FFN_BWD_SHARDS, 1)
    nsteps = ns // sb

    def body(h_ref, g_ref, dh_ref, gg_ref, uu_ref, wg_ref, wu_ref, wd_ref,
             dho_ref, dgam_ref, hn_ref, dy_ref, dg_ref, du_ref, a_ref, acc_sc):
        i, s = pl.program_id(0), pl.program_id(1)

        @pl.when(s == 0)
        def _():
            xh, _ = _rms(h_ref[...])
            hn_ref[...] = (xh * g_ref[...]).astype(BF16)
            dy_ref[...] = (0.5 * dh_ref[...]).astype(BF16)
            acc_sc[...] = jnp.zeros_like(acc_sc)

        @pl.when((i == 0) & (s == 0))
        def _():
            dgam_ref[...] = jnp.zeros_like(dgam_ref)

        part = None
        for t in range(sb):
            g = gg_ref[t].astype(F32)
            u = uu_ref[t].astype(F32)
            da = lax.dot_general(dy_ref[...], wd_ref[t], NT, preferred_element_type=F32)
            sig = _sigmoid(g)
            sl = g * sig
            a_ref[t] = (sl * u).astype(BF16)
            du = (da * sl).astype(BF16)
            dg = (da * u * (sig * (1.0 + g * (1.0 - sig)))).astype(BF16)
            dg_ref[t] = dg
            du_ref[t] = du
            p = (jnp.dot(dg, wg_ref[t], preferred_element_type=F32)
                 + jnp.dot(du, wu_ref[t], preferred_element_type=F32))
            part = p if part is None else part + p
        acc_sc[...] += part

        @pl.when(s == nsteps - 1)
        def _():
            xh, rstd = _rms(h_ref[...])
            dhn = acc_sc[...]
            dgam_ref[...] += jnp.sum(dhn * xh, axis=0, keepdims=True)
            dho_ref[...] = dh_ref[...] + _rms_bwd(xh, rstd, dhn * g_ref[...])

    row = pl.BlockSpec((tm, d), lambda i, s: (i, 0))
    act = pl.BlockSpec((sb, tm, fs), lambda i, s: (s, i, 0))
    wsp = pl.BlockSpec((sb, fs, d), lambda i, s: (s, 0, 0))
    return pl.pallas_call(
        body, name=name,
        out_shape=[jax.ShapeDtypeStruct((r, d), F32), jax.ShapeDtypeStruct((1, d), F32),
                   jax.ShapeDtypeStruct((r, d), BF16), jax.ShapeDtypeStruct((r, d), BF16),
                   jax.ShapeDtypeStruct((ns, r, fs), BF16), jax.ShapeDtypeStruct((ns, r, fs), BF16),
                   jax.ShapeDtypeStruct((ns, r, fs), BF16)],
        grid=(r // tm, nsteps),
        in_specs=[row, pl.BlockSpec((1, d), lambda i, s: (0, 0)), row, act, act, wsp, wsp, wsp],
        out_specs=[row, pl.BlockSpec((1, d), lambda i, s: (0, 0)), row, row, act, act, act],
        scratch_shapes=[pltpu.VMEM((tm, d), F32)],
        compiler_params=_params(("arbitrary", "arbitrary"), VMEM_MAX),
    )(h, gamma, dh, gg, uu, wg, wu, wd)


def ffn_bwd_weights(hn, dy, a, dg, du, name):
    r, d = hn.shape
    ns, _, fs = a.shape
    tm = WGRAD_ROWS if r % WGRAD_ROWS == 0 else _row_tile(r)

    def body(hn_ref, dy_ref, a_ref, dg_ref, du_ref, wg_ref, wu_ref, wd_ref):
        @pl.when(pl.program_id(1) == 0)
        def _():
            wg_ref[...] = jnp.zeros_like(wg_ref)
            wu_ref[...] = jnp.zeros_like(wu_ref)
            wd_ref[...] = jnp.zeros_like(wd_ref)

        hn_ = hn_ref[...]
        wg_ref[0] += lax.dot_general(dg_ref[0], hn_, TN, preferred_element_type=F32)
        wu_ref[0] += lax.dot_general(du_ref[0], hn_, TN, preferred_element_type=F32)
        wd_ref[0] += lax.dot_general(a_ref[0], dy_ref[...], TN, preferred_element_type=F32)

    row = pl.BlockSpec((tm, d), lambda s, i: (i, 0))
    act = pl.BlockSpec((1, tm, fs), lambda s, i: (s, i, 0))
    wsp = pl.BlockSpec((1, fs, d), lambda s, i: (s, 0, 0))
    return pl.pallas_call(
        body, name=name,
        out_shape=[jax.ShapeDtypeStruct((ns, fs, d), F32)] * 3,
        grid=(ns, r // tm),
        in_specs=[row, row, act, act, act],
        out_specs=[wsp, wsp, wsp],
        compiler_params=_params(("arbitrary", "arbitrary"), VMEM_MAX),
    )(hn, dy, a, dg, du)


def norm_fwd(x, gamma, name):
    r = x.shape[0]
    w = gamma.shape[1]
    tm = _row_tile(r)

    def body(x_ref, g_ref, o_ref):
        xh, _ = _rms(x_ref[...])
        o_ref[...] = (xh * g_ref[...]).astype(BF16)

    return pl.pallas_call(
        body, name=name, out_shape=jax.ShapeDtypeStruct((r, w), BF16), grid=(r // tm,),
        in_specs=[pl.BlockSpec((tm, w), lambda i: (i, 0)), pl.BlockSpec((1, w), lambda i: (0, 0))],
        out_specs=pl.BlockSpec((tm, w), lambda i: (i, 0)),
        compiler_params=_params(("arbitrary",)),
    )(x, gamma)


def norm_bwd(x, gamma, dy, dres, name):
    r = x.shape[0]
    w = gamma.shape[1]
    tm = _row_tile(r)
    has_res = dres is not None

    def body(*refs):
        if has_res:
            x_ref, g_ref, dy_ref, dr_ref, dx_ref, dgam_ref = refs
        else:
            x_ref, g_ref, dy_ref, dx_ref, dgam_ref = refs

        @pl.when(pl.program_id(0) == 0)
        def _():
            dgam_ref[...] = jnp.zeros_like(dgam_ref)

        xh, rstd = _rms(x_ref[...])
        dy_ = dy_ref[...].astype(F32)
        dgam_ref[...] += jnp.sum(dy_ * xh, axis=0, keepdims=True)
        dx = _rms_bwd(xh, rstd, dy_ * g_ref[...])
        if has_res:
            dx = dx + dr_ref[...]
        dx_ref[...] = dx

    row = pl.BlockSpec((tm, w), lambda i: (i, 0))
    vec = pl.BlockSpec((1, w), lambda i: (0, 0))
    ins = [x, gamma, dy] + ([dres] if has_res else [])
    return pl.pallas_call(
        body, name=name,
        out_shape=[jax.ShapeDtypeStruct((r, w), F32), jax.ShapeDtypeStruct((1, w), F32)],
        grid=(r // tm,), in_specs=[row, vec, row] + ([row] if has_res else []), out_specs=[row, vec],
        compiler_params=_params(("arbitrary",)),
    )(*ins)


def rowmm(a, w, name, *, nt=False, res=None, out_dtype=F32, heads_out=False):
    ha, r, ka = a.shape
    hw = w.shape[0]
    nh = max(ha, hw)
    n = w.shape[1] if nt else w.shape[2]
    tm = _row_tile(r)
    dims = NT if nt else (((1,), (0,)), ((), ()))
    has_res = res is not None

    def body(*refs):
        if has_res:
            a_ref, w_ref, r_ref, o_ref = refs
        else:
            a_ref, w_ref, o_ref = refs
        shared = a_ref[0].astype(BF16) if ha == 1 else None
        acc = None
        for h in range(nh):
            lhs = shared if ha == 1 else a_ref[h].astype(BF16)
            p = lax.dot_general(lhs, w_ref[h if hw > 1 else 0], dims, preferred_element_type=F32)
            if heads_out:
                o_ref[h] = p.astype(out_dtype)
            else:
                acc = p if acc is None else acc + p
        if not heads_out:
            if has_res:
                acc = acc + r_ref[...]
            o_ref[...] = acc.astype(out_dtype)

    in_specs = [pl.BlockSpec((ha, tm, ka), lambda i: (0, i, 0)), pl.BlockSpec(w.shape, lambda i: (0, 0, 0))]
    ins = [a, w]
    if has_res:
        in_specs.append(pl.BlockSpec((tm, n), lambda i: (i, 0)))
        ins.append(res)
    if heads_out:
        out_shape = jax.ShapeDtypeStruct((nh, r, n), out_dtype)
        out_spec = pl.BlockSpec((nh, tm, n), lambda i: (0, i, 0))
    else:
        out_shape = jax.ShapeDtypeStruct((r, n), out_dtype)
        out_spec = pl.BlockSpec((tm, n), lambda i: (i, 0))
    return pl.pallas_call(
        body, name=name, out_shape=out_shape, grid=(r // tm,), in_specs=in_specs, out_specs=out_spec,
        compiler_params=_params(("arbitrary",), VMEM_BIG),
    )(*ins)


def tnmm(a, b, name):
    ha, r, ka = a.shape
    hb, _, nb = b.shape
    nh = max(ha, hb)
    tm = _row_tile(r)

    def body(a_ref, b_ref, o_ref):
        @pl.when(pl.program_id(0) == 0)
        def _():
            o_ref[...] = jnp.zeros_like(o_ref)

        a_shared = a_ref[0].astype(BF16) if ha == 1 else None
        b_shared = b_ref[0].astype(BF16) if hb == 1 else None
        for h in range(nh):
            lhs = a_shared if ha == 1 else a_ref[h].astype(BF16)
            rhs = b_shared if hb == 1 else b_ref[h].astype(BF16)
            o_ref[h] += lax.dot_general(lhs, rhs, TN, preferred_element_type=F32)

    return pl.pallas_call(
        body, name=name, out_shape=jax.ShapeDtypeStruct((nh, ka, nb), F32), grid=(r // tm,),
        in_specs=[pl.BlockSpec((ha, tm, ka), lambda i: (0, i, 0)), pl.BlockSpec((hb, tm, nb), lambda i: (0, i, 0))],
        out_specs=pl.BlockSpec((nh, ka, nb), lambda i: (0, 0, 0)),
        compiler_params=_params(("arbitrary",), VMEM_BIG),
    )(a, b)


def rope_tables(r):
    inv = 1.0 / (ROPE_THETA ** (jnp.arange(0, QK_ROPE, 2, dtype=F32) / QK_ROPE))
    pos = (jnp.arange(r, dtype=F32) - META_ROW0)[:, None]
    ang = pos * inv[None, :]
    cos, sin = jnp.cos(ang), jnp.sin(ang)
    ones = jnp.ones((r, HEAD_W - QK_ROPE), F32)
    ctab = jnp.concatenate([cos, cos, ones], axis=1)
    stab = jnp.concatenate([-sin, sin, jnp.zeros_like(ones)], axis=1)
    return ctab, stab


def _swap_halves(z):
    lane = lax.broadcasted_iota(jnp.int32, z.shape, 1)
    up = pltpu.roll(z, HEAD_W - QK_ROPE // 2, 1)
    down = pltpu.roll(z, QK_ROPE // 2, 1)
    return jnp.where(lane < QK_ROPE // 2, up, jnp.where(lane < QK_ROPE, down, 0.0))


def proj_rope(a, w, ctab, stab, extra, name, scale=1.0):
    r, ka = a.shape
    nh = w.shape[0]
    tm = _row_tile(r)
    has_extra = extra is not None

    def body(*refs):
        if has_extra:
            a_ref, w_ref, c_ref, s_ref, e_ref, o_ref = refs
        else:
            a_ref, w_ref, c_ref, s_ref, o_ref = refs
        a_ = a_ref[...]
        ctab_, stab_ = c_ref[...], s_ref[...]
        if scale != 1.0:
            ctab_, stab_ = ctab_ * scale, stab_ * scale
        for h in range(nh):
            x = jnp.dot(a_, w_ref[h], preferred_element_type=F32)
            if has_extra:
                x = x + e_ref[...]
            o_ref[h] = (x * ctab_ + _swap_halves(x) * stab_).astype(BF16)

    tab = pl.BlockSpec((tm, HEAD_W), lambda i: (i, 0))
    in_specs = [pl.BlockSpec((tm, ka), lambda i: (i, 0)), pl.BlockSpec((nh, ka, HEAD_W), lambda i: (0, 0, 0)),
                tab, tab]
    ins = [a, w, ctab, stab]
    if has_extra:
        in_specs.append(pl.BlockSpec((tm, HEAD_W), lambda i: (i, 2)))
        ins.append(extra)
    return pl.pallas_call(
        body, name=name, out_shape=jax.ShapeDtypeStruct((nh, r, HEAD_W), BF16), grid=(r // tm,),
        in_specs=in_specs, out_specs=pl.BlockSpec((nh, tm, HEAD_W), lambda i: (0, i, 0)),
        compiler_params=_params(("arbitrary",)),
    )(*ins)


def rope_bwd_heads(d, ctab, stab, name, scale=1.0):
    nh, r, _ = d.shape
    tm = _row_tile(r)

    def body(d_ref, c_ref, s_ref, o_ref):
        ctab_, stab_ = c_ref[...], s_ref[...]
        if scale != 1.0:
            ctab_, stab_ = ctab_ * scale, stab_ * scale
        for h in range(nh):
            d_ = d_ref[h]
            o_ref[h] = (d_ * ctab_ + _swap_halves(d_ * stab_)).astype(BF16)

    tab = pl.BlockSpec((tm, HEAD_W), lambda i: (i, 0))
    blk = pl.BlockSpec((nh, tm, HEAD_W), lambda i: (0, i, 0))
    return pl.pallas_call(
        body, name=name, out_shape=jax.ShapeDtypeStruct((nh, r, HEAD_W), BF16), grid=(r // tm,),
        in_specs=[blk, tab, tab], out_specs=blk,
        compiler_params=_params(("arbitrary",)),
    )(d, ctab, stab)


def rope_bwd_sum(d, ctab, stab, name):
    nh, r, _ = d.shape
    tm = _row_tile(r)

    def body(d_ref, c_ref, s_ref, o_ref):
        d_ = d_ref[0]
        for h in range(1, nh):
            d_ = d_ + d_ref[h]
        lane = lax.broadcasted_iota(jnp.int32, d_.shape, 1)
        g = d_ * c_ref[...] + _swap_halves(d_ * s_ref[...])
        o_ref[...] = jnp.where(lane < QK_ROPE, g, 0.0)

    tab = pl.BlockSpec((tm, HEAD_W), lambda i: (i, 0))
    return pl.pallas_call(
        body, name=name, out_shape=jax.ShapeDtypeStruct((r, HEAD_W), F32), grid=(r // tm,),
        in_specs=[pl.BlockSpec((nh, tm, HEAD_W), lambda i: (0, i, 0)), tab, tab], out_specs=tab,
        compiler_params=_params(("arbitrary",)),
    )(d, ctab, stab)


def _attn_tiles(r):
    t = _row_tile(r)
    return t, t


def _mask(q0, k0, nq_, nk_, keys_on_rows=False):
    shape = (nk_, nq_) if keys_on_rows else (nq_, nk_)
    rq = q0 + lax.broadcasted_iota(jnp.int32, shape, 1 if keys_on_rows else 0)
    rk = k0 + lax.broadcasted_iota(jnp.int32, shape, 0 if keys_on_rows else 1)
    return ((rk >> CHUNK_SHIFT) <= (rq >> CHUNK_SHIFT)) & (rk >= META_ROW0)


ATTN_FWD_HEADS = 8
ATTN_FWD_UNROLL = 8
ATTN_BWD_HEADS = 4


SM_SCALE = 1.0 / math.sqrt(QK_NOPE + QK_ROPE)
LOG2E = math.log2(math.e)
Q_SCALE = SM_SCALE * LOG2E


def attn_fwd(q, k, v, name):
    nh, r, dk = q.shape
    dv = v.shape[-1]
    tq, tk = _attn_tiles(r)
    nq, nk = r // tq, r // tk

    def last_k(i):
        return ((i + 1) * tq - 1) // tk

    pairs = [(i, j) for i in range(nq) for j in range(last_k(i) + 1)]
    qi_tab = jnp.asarray([p[0] for p in pairs], jnp.int32)
    kj_tab = jnp.asarray([p[1] for p in pairs], jnp.int32)
    hb = _tile(nh, ATTN_FWD_HEADS, 1)
    hu = _tile(hb, ATTN_FWD_UNROLL, 1)

    def body(qi_ref, kj_ref, q_ref, k_ref, v_ref, o_ref, lse_ref, m_sc, l_sc, acc_sc):
        t = pl.program_id(1)
        i, j = qi_ref[t], kj_ref[t]

        @pl.when(j == 0)
        def _():
            m_sc[...] = jnp.full_like(m_sc, NEG)
            l_sc[...] = jnp.zeros_like(l_sc)
            acc_sc[...] = jnp.zeros_like(acc_sc)

        def step(masked):
            def one_head(hh):
                s = lax.dot_general(q_ref[hh], k_ref[hh], NT, preferred_element_type=F32)
                if masked:
                    s = jnp.where(_mask(i * tq, j * tk, tq, tk), s, NEG)
                m_old = m_sc[hh]
                m_new = jnp.maximum(m_old, jnp.max(s, axis=-1, keepdims=True))
                alpha = jnp.exp2(m_old - m_new)
                p = jnp.exp2(s - jnp.tile(m_new, (1, tk // LANES)))
                l_sc[hh] = alpha * l_sc[hh] + jnp.sum(p, axis=-1, keepdims=True)
                acc_sc[hh] = (alpha[:, :dv] * acc_sc[hh]
                              + jnp.dot(p.astype(BF16), v_ref[hh], preferred_element_type=F32))
                m_sc[hh] = m_new

            def head_group(g, carry):
                for u in range(hu):
                    one_head(g * hu + u)
                return carry

            lax.fori_loop(0, hb // hu, head_group, 0)

        needs_mask = (j == last_k(i)) | (j == 0)
        pl.when(needs_mask)(functools.partial(step, True))
        pl.when(jnp.logical_not(needs_mask))(functools.partial(step, False))

        @pl.when(j == last_k(i))
        def _():
            def one_head(hh, carry):
                l = l_sc[hh]
                o_ref[hh] = (acc_sc[hh] / l[:, :dv]).astype(BF16)
                lse_ref[hh] = jnp.transpose(m_sc[hh] + jnp.log2(l))[:1, :]
                return carry

            lax.fori_loop(0, hb, one_head, 0)

    qspec = lambda w: pl.BlockSpec((hb, tq, w), lambda h, t, qi, kj: (h, qi[t], 0))
    kspec = lambda w: pl.BlockSpec((hb, tk, w), lambda h, t, qi, kj: (h, kj[t], 0))
    return pl.pallas_call(
        body, name=name,
        out_shape=[jax.ShapeDtypeStruct((nh, r, dv), BF16), jax.ShapeDtypeStruct((nh, 1, r), F32)],
        grid_spec=pltpu.PrefetchScalarGridSpec(
            num_scalar_prefetch=2, grid=(nh // hb, len(pairs)),
            in_specs=[qspec(dk), kspec(dk), kspec(dv)],
            out_specs=[qspec(dv), pl.BlockSpec((hb, 1, tq), lambda h, t, qi, kj: (h, 0, qi[t]))],
            scratch_shapes=[pltpu.VMEM((hb, tq, LANES), F32), pltpu.VMEM((hb, tq, LANES), F32),
                            pltpu.VMEM((hb, tq, dv), F32)]),
        compiler_params=_params(("arbitrary", "arbitrary"), VMEM_BIG),
    )(qi_tab, kj_tab, q, k, v)


def attn_bwd(q, k, v, do, lse, delta, dk_prev, dv_prev, name):
    nh, r, dk = q.shape
    dv = v.shape[-1]
    tq, tk = _attn_tiles(r)
    nq, nk = r // tq, r // tk
    has_prev = dk_prev is not None

    def first_q(j):
        return (j * tk) // tq

    pairs = [(j, i) for j in range(nk) for i in range(first_q(j), nq)]
    kj_tab = jnp.asarray([p[0] for p in pairs], jnp.int32)
    qi_tab = jnp.asarray([p[1] for p in pairs], jnp.int32)
    hb = _tile(nh, ATTN_BWD_HEADS, 1)

    def body(*refs):
        if has_prev:
            (kj_ref, qi_ref, q_ref, k_ref, v_ref, do_ref, lse_ref, dl_ref, pk_ref, pv_ref,
             dq_ref, dk_ref, dv_ref, dk_sc, dv_sc) = refs
        else:
            (kj_ref, qi_ref, q_ref, k_ref, v_ref, do_ref, lse_ref, dl_ref,
             dq_ref, dk_ref, dv_ref, dk_sc, dv_sc) = refs
        t = pl.program_id(1)
        j, i = kj_ref[t], qi_ref[t]

        @pl.when(t == 0)
        def _():
            dq_ref[...] = jnp.zeros_like(dq_ref)

        @pl.when(i == first_q(j))
        def _():
            dk_sc[...] = jnp.zeros_like(dk_sc)
            dv_sc[...] = jnp.zeros_like(dv_sc)

        def step(masked):
            qrows = pl.ds(pl.multiple_of(i * tq, tq), tq)
            for hh in range(hb):
                q_, k_, do_ = q_ref[hh], k_ref[hh], do_ref[hh]
                st = lax.dot_general(k_, q_, NT, preferred_element_type=F32)
                if masked:
                    st = jnp.where(_mask(i * tq, j * tk, tq, tk, keys_on_rows=True), st, NEG)
                pt = jnp.exp2(st - lse_ref[hh])
                dv_sc[hh] += jnp.dot(pt.astype(BF16), do_, preferred_element_type=F32)
                dpt = lax.dot_general(v_ref[hh], do_, NT, preferred_element_type=F32)
                dst = (pt * (dpt - dl_ref[hh])).astype(BF16)
                dk_sc[hh] += jnp.dot(dst, q_, preferred_element_type=F32)
                dq_ref[hh, qrows, :] += lax.dot_general(dst, k_, TN, preferred_element_type=F32)

        needs_mask = (i == first_q(j)) | (j == 0)
        pl.when(needs_mask)(functools.partial(step, True))
        pl.when(jnp.logical_not(needs_mask))(functools.partial(step, False))

        @pl.when(i == nq - 1)
        def _():
            dk_ = dk_sc[...] * (1.0 / LOG2E)
            dv_ = dv_sc[...]
            if has_prev:
                dk_ = dk_ + pk_ref[...]
                dv_ = dv_ + pv_ref[...]
            dk_ref[...] = dk_
            dv_ref[...] = dv_

    krow = lambda w: pl.BlockSpec((hb, tk, w), lambda h, t, kj, qi: (h, kj[t], 0))
    qrow = lambda w: pl.BlockSpec((hb, tq, w), lambda h, t, kj, qi: (h, qi[t], 0))
    qvec = pl.BlockSpec((hb, 1, tq), lambda h, t, kj, qi: (h, 0, qi[t]))
    in_specs = [qrow(dk), krow(dk), krow(dv), qrow(dv), qvec, qvec]
    ins = [q, k, v, do, lse, delta]
    if has_prev:
        in_specs += [krow(dk), krow(dv)]
        ins += [dk_prev, dv_prev]
    return pl.pallas_call(
        body, name=name,
        out_shape=[jax.ShapeDtypeStruct((nh, r, dk), F32), jax.ShapeDtypeStruct((nh, r, dk), F32),
                   jax.ShapeDtypeStruct((nh, r, dv), F32)],
        grid_spec=pltpu.PrefetchScalarGridSpec(
            num_scalar_prefetch=2, grid=(nh // hb, len(pairs)), in_specs=in_specs,
            out_specs=[pl.BlockSpec((hb, r, dk), lambda h, t, kj, qi: (h, 0, 0)), krow(dk), krow(dv)],
            scratch_shapes=[pltpu.VMEM((hb, tk, dk), F32), pltpu.VMEM((hb, tk, dv), F32)]),
        compiler_params=_params(("arbitrary", "arbitrary"), VMEM_MAX),
    )(kj_tab, qi_tab, *ins)


def attn_out_bwd(dattn, wo, o, name):
    r, d = dattn.shape
    nh, dv, _ = wo.shape
    tm = _row_tile(r)

    def body(da_ref, w_ref, o_ref, do_ref, dl_ref):
        da = da_ref[...].astype(BF16)
        for h in range(nh):
            do_ = lax.dot_general(da, w_ref[h], NT, preferred_element_type=F32).astype(BF16)
            do_ref[h] = do_
            col = jnp.sum(do_.astype(F32) * o_ref[h].astype(F32), axis=-1, keepdims=True)
            dl_ref[h] = jnp.transpose(jnp.broadcast_to(col, (tm, LANES)))[:1, :]

    return pl.pallas_call(
        body, name=name,
        out_shape=[jax.ShapeDtypeStruct((nh, r, dv), BF16), jax.ShapeDtypeStruct((nh, 1, r), F32)],
        grid=(r // tm,),
        in_specs=[pl.BlockSpec((tm, d), lambda i: (i, 0)), pl.BlockSpec((nh, dv, d), lambda i: (0, 0, 0)),
                  pl.BlockSpec((nh, tm, dv), lambda i: (0, i, 0))],
        out_specs=[pl.BlockSpec((nh, tm, dv), lambda i: (0, i, 0)), pl.BlockSpec((nh, 1, tm), lambda i: (0, 0, i))],
        compiler_params=_params(("arbitrary",)),
    )(dattn, wo, o)


def _pool_counts(row0, n, window):
    rows = row0 + lax.broadcasted_iota(jnp.int32, (n, 1), 0)
    cnt = jnp.clip(rows - META_ROW0 + 1, 1, window)
    return 1.0 / cnt.astype(F32)


def pool_fwd(h, gamma, wp, scale, name):
    r, d = h.shape
    ng, cg, _ = wp.shape
    tm = _row_tile(r)
    hb = tm // HALO

    def body(h_ref, hp_ref, g_ref, w_ref, sc_ref, o_ref):
        i = pl.program_id(0)
        xm = h_ref[...]
        xp = hp_ref[...] * jnp.where(i > 0, 1.0, 0.0)
        xx = jnp.concatenate([xp, xm], axis=0)
        xh, _ = _rms(xx)
        u = xh * g_ref[...]
        for g, win in enumerate(POOL_WINDOWS):
            sl = slice(g * cg, (g + 1) * cg)
            ug = u[:, sl]
            acc, k = ug, 1
            while k < win:
                acc = acc + pltpu.roll(acc, k, 0)
                k *= 2
            pooled = acc[HALO:] * _pool_counts(i * tm, tm, win) - ug[HALO:]
            y = jnp.dot(pooled.astype(BF16), w_ref[g], preferred_element_type=F32)
            o_ref[:, sl] = xm[:, sl] + y * sc_ref[:, sl]

    return pl.pallas_call(
        body, name=name, out_shape=jax.ShapeDtypeStruct((r, d), F32), grid=(r // tm,),
        in_specs=[pl.BlockSpec((tm, d), lambda i: (i, 0)),
                  pl.BlockSpec((HALO, d), lambda i: (jnp.maximum(i * hb - 1, 0), 0)),
                  pl.BlockSpec((1, d), lambda i: (0, 0)), pl.BlockSpec((ng, cg, cg), lambda i: (0, 0, 0)),
                  pl.BlockSpec((1, d), lambda i: (0, 0))],
        out_specs=pl.BlockSpec((tm, d), lambda i: (i, 0)),
        compiler_params=_params(("arbitrary",), VMEM_BIG),
    )(h, h, gamma, wp, scale)


def pool_bwd(h, gamma, wp, scale, dh, name):
    r, d = h.shape
    ng, cg, _ = wp.shape
    tm = _row_tile(r)
    hb = tm // HALO
    nt = r // tm

    def body(h_ref, hp_ref, dh_ref, dn_ref, g_ref, w_ref, sc_ref, dx_ref, dgam_ref, dw_ref, dsc_ref, du_sc):
        i = pl.program_id(0)

        @pl.when(i == 0)
        def _():
            dgam_ref[...] = jnp.zeros_like(dgam_ref)
            dw_ref[...] = jnp.zeros_like(dw_ref)
            dsc_ref[...] = jnp.zeros_like(dsc_ref)

        xm = h_ref[...]
        xp = hp_ref[...] * jnp.where(i > 0, 1.0, 0.0)
        xh_all, rstd_all = _rms(jnp.concatenate([xp, xm], axis=0))
        u = xh_all * g_ref[...]
        dm = dh_ref[...]
        dn = dn_ref[...] * jnp.where(i < nt - 1, 1.0, 0.0)
        dd = jnp.concatenate([dm, dn], axis=0)
        for g, win in enumerate(POOL_WINDOWS):
            sl = slice(g * cg, (g + 1) * cg)
            ug = u[:, sl]
            acc, k = ug, 1
            while k < win:
                acc = acc + pltpu.roll(acc, k, 0)
                k *= 2
            pooled = (acc[HALO:] * _pool_counts(i * tm, tm, win) - ug[HALO:]).astype(BF16)
            y = jnp.dot(pooled, w_ref[g], preferred_element_type=F32)
            dsc_ref[:, sl] += jnp.sum(dm[:, sl] * y, axis=0, keepdims=True)
            dyp = (dd[:, sl] * sc_ref[:, sl]).astype(BF16)
            dw_ref[g] += lax.dot_general(pooled, dyp[:tm], TN, preferred_element_type=F32)
            dpo = lax.dot_general(dyp, w_ref[g], NT, preferred_element_type=F32)
            z = dpo * _pool_counts(i * tm, tm + HALO, win)
            fwd, k = z, 1
            while k < win:
                fwd = fwd + pltpu.roll(fwd, tm + HALO - k, 0)
                k *= 2
            du_sc[:, sl] = fwd[:tm] - dpo[:tm]
        du = du_sc[...]
        xh, rstd = xh_all[HALO:], rstd_all[HALO:]
        dgam_ref[...] += jnp.sum(du * xh, axis=0, keepdims=True)
        dx_ref[...] = dm + _rms_bwd(xh, rstd, du * g_ref[...])

    row = pl.BlockSpec((tm, d), lambda i: (i, 0))
    vec = pl.BlockSpec((1, d), lambda i: (0, 0))
    prev = pl.BlockSpec((HALO, d), lambda i: (jnp.maximum(i * hb - 1, 0), 0))
    nxt = pl.BlockSpec((HALO, d), lambda i: (jnp.minimum((i + 1) * hb, r // HALO - 1), 0))
    wsp = pl.BlockSpec((ng, cg, cg), lambda i: (0, 0, 0))
    return pl.pallas_call(
        body, name=name,
        out_shape=[jax.ShapeDtypeStruct((r, d), F32), jax.ShapeDtypeStruct((1, d), F32),
                   jax.ShapeDtypeStruct((ng, cg, cg), F32), jax.ShapeDtypeStruct((1, d), F32)],
        grid=(nt,), in_specs=[row, prev, row, nxt, vec, wsp, vec], out_specs=[row, vec, wsp, vec],
        scratch_shapes=[pltpu.VMEM((tm, d), F32)],
        compiler_params=_params(("arbitrary",), VMEM_BIG),
    )(h, h, dh, dh, gamma, wp, scale)


def loss_head(h, gamma, target, seq, name):
    r, d = h.shape
    tm = _row_tile(r)

    def body(h_ref, g_ref, t_ref, sse_ref, dh_ref, dgam_ref):
        i = pl.program_id(0)

        @pl.when(i == 0)
        def _():
            sse_ref[...] = jnp.zeros_like(sse_ref)
            dgam_ref[...] = jnp.zeros_like(dgam_ref)

        xh, rstd = _rms(h_ref[...])
        rows = i * tm + lax.broadcasted_iota(jnp.int32, (tm, 1), 0)
        valid = ((rows >= FRONT) & (rows < FRONT + seq)).astype(F32)
        e = (xh * g_ref[...] - t_ref[...]) * valid
        sse_ref[...] += jnp.sum(jnp.sum(e * e, axis=1, keepdims=True), axis=0, keepdims=True)
        dy = e * (1.0 / d)
        dgam_ref[...] += jnp.sum(dy * xh, axis=0, keepdims=True)
        dh_ref[...] = _rms_bwd(xh, rstd, dy * g_ref[...])

    row = pl.BlockSpec((tm, d), lambda i: (i, 0))
    vec = pl.BlockSpec((1, d), lambda i: (0, 0))
    return pl.pallas_call(
        body, name=name,
        out_shape=[jax.ShapeDtypeStruct((1, 1), F32), jax.ShapeDtypeStruct((r, d), F32),
                   jax.ShapeDtypeStruct((1, d), F32)],
        grid=(r // tm,), in_specs=[row, vec, row],
        out_specs=[pl.BlockSpec((1, 1), lambda i: (0, 0)), row, vec],
        compiler_params=_params(("arbitrary",)),
    )(h, gamma, target)


SMALL = ("meta_tokens", "pool_w", "pool_scale", "w_dkv", "w_uk", "w_uv", "w_dq", "w_uq", "w_o")


def _pack(parts):
    flat = jnp.concatenate([p.reshape(-1) for p in parts])
    n = flat.shape[0]
    unit = PACK_W * 2 * SUBLANES
    n_pad = -(-n // unit) * unit
    return jnp.pad(flat, (0, n_pad - n)).reshape(n_pad // PACK_W, PACK_W)


def _unpack(buf, shapes, lead=()):
    flat = buf.reshape(lead + (-1,))
    out, off = [], 0
    for shp in shapes:
        n = math.prod(shp)
        out.append(flat[..., off:off + n].reshape(lead + tuple(shp)))
        off += n
    return out


def _cols_from_shards(a, axis):
    a = jnp.moveaxis(a, 0, axis)
    shp = a.shape
    return a.reshape(shp[:axis] + (shp[axis] * shp[axis + 1],) + shp[axis + 2:])


def _cols_to_shards(a, axis):
    shp = a.shape
    a = a.reshape(shp[:axis] + (N_SHARD, shp[axis] // N_SHARD) + shp[axis + 1:])
    return jnp.moveaxis(a, axis, 0)


SMALL_AXIS = {"meta_tokens": 1, "pool_w": 2, "pool_scale": 1, "w_dkv": 0, "w_uk": 1, "w_uv": 1,
              "w_dq": 1, "w_uq": 2, "w_o": 2}


def kernel(x, meta_tokens, ffn1_norm, ffn1_w_gate, ffn1_w_up, ffn1_w_down, mix_norm, ffn2_norm, ffn2_w_gate, ffn2_w_up, ffn2_w_down, pool_w, pool_scale, kv_in_norm, w_dkv, kv_latent_norm, w_uk, w_uv, w_dq, q_latent_norm, w_uq, w_o, final_norm, loss_target, m_meta_tokens, m_ffn1_norm, m_ffn1_w_gate, m_ffn1_w_up, m_ffn1_w_down, m_mix_norm, m_ffn2_norm, m_ffn2_w_gate, m_ffn2_w_up, m_ffn2_w_down, m_pool_w, m_pool_scale, m_kv_in_norm, m_w_dkv, m_kv_latent_norm, m_w_uk, m_w_uv, m_w_dq, m_q_latent_norm, m_w_uq, m_w_o, m_final_norm, v_meta_tokens, v_ffn1_norm, v_ffn1_w_gate, v_ffn1_w_up, v_ffn1_w_down, v_mix_norm, v_ffn2_norm, v_ffn2_w_gate, v_ffn2_w_up, v_ffn2_w_down, v_pool_w, v_pool_scale, v_kv_in_norm, v_w_dkv, v_kv_latent_norm, v_w_uk, v_w_uv, v_w_dq, v_q_latent_norm, v_w_uq, v_w_o, v_final_norm):
    args = dict(locals())
    W = {n: args[n] for n in NAMES}
    M = {n: args["m_" + n] for n in NAMES}
    V = {n: args["v_" + n] for n in NAMES}
    TRANSPOSED = ("ffn1_w_gate", "ffn1_w_up", "ffn2_w_gate", "ffn2_w_up")
    for n in TRANSPOSED:
        W[n], M[n], V[n] = (jnp.swapaxes(a, 1, 2) for a in (W[n], M[n], V[n]))

    depth = ffn1_norm.shape[0]
    n_a = pool_w.shape[0]
    seq, d = x.shape[1], x.shape[2]
    nh = N_HEADS
    r = -(-(FRONT + seq) // LANES) * LANES

    cx, cy, cc = lax.axis_index("x"), lax.axis_index("y"), lax.axis_index("c")
    c_arr = jnp.reshape(cc, (1,)).astype(jnp.int32)
    s_arr = jnp.reshape(2 * cx + cy, (1,)).astype(jnp.int32)

    small_shapes = [W[n].shape for n in SMALL]

    ffn_src = {f: tuple(W[f + t].astype(BF16) for t in ("_w_gate", "_w_up", "_w_down")) for f in ("ffn1", "ffn2")}
    ffn_order = [(f, l) for l in range(depth) for f in ("ffn1", "ffn2")]
    ffn_w = {}
    gate = []
    ag_state = [None]

    def gated(a):
        if gate:
            a = a + sum(gate[1:], gate[0]).astype(a.dtype)
            gate.clear()
        return a

    def vec(a):
        return gated(a.reshape(1, -1))

    def ag_start(idx, dep):
        f, l = ffn_order[idx]
        shards = [w_[l] + dep for w_ in ffn_src[f]]
        lands = [lax.dynamic_update_slice(lax.empty((N_SHARD,) + a.shape, BF16), a[None], (2 * cx + cy, 0, 0))
                 for a in shards]
        ag_state[0] = exchange_start(shards, lands, f"ag_start_{f}_{l}", True)
        gate.append(ag_state[0][4][0, 0])

    def ag_wait(idx, after):
        f, l = ffn_order[idx]
        _, lands = exchange_wait(ag_state[0], after, f"ag_wait_{f}_{l}", True)
        ffn_w[f, l] = lands
        if idx + 1 < len(ffn_order):
            ag_start(idx + 1, lands[0][0, 0, 0] * jnp.zeros((), BF16))

    gathered = all_gather_shards([_pack([W[n] for n in SMALL])], "ag_small")[0]
    ag_start(0, (gathered[0, 0, 0] * 0.0).astype(BF16))
    small_full = {}
    for n, part in zip(SMALL, _unpack(gathered, small_shapes, (N_SHARD,))):
        small_full[n] = _cols_from_shards(part, SMALL_AXIS[n])

    meta_full = small_full["meta_tokens"]
    wp = small_full["pool_w"].astype(BF16)
    pscale = small_full["pool_scale"]
    wdkv = jnp.pad(small_full["w_dkv"], ((0, 0), (0, HEAD_W - QK_ROPE))).astype(BF16)[None]
    wuk = small_full["w_uk"].reshape(KV_RANK, nh, QK_NOPE).transpose(1, 0, 2)
    wk_h = jnp.concatenate([jnp.zeros((nh, KV_RANK, HEAD_W - QK_NOPE), F32), wuk], axis=-1).astype(BF16)
    wv_h = small_full["w_uv"].reshape(KV_RANK, nh, V_HEAD).transpose(1, 0, 2).astype(BF16)
    wdq = small_full["w_dq"].astype(BF16)
    wuq = small_full["w_uq"].reshape(-1, Q_RANK, nh, QK_NOPE + QK_ROPE).transpose(0, 2, 1, 3)
    wq_h = jnp.concatenate([wuq[..., QK_NOPE:], jnp.zeros(wuq.shape[:-1] + (HEAD_W - QK_NOPE - QK_ROPE,), F32),
                            wuq[..., :QK_NOPE]], axis=-1).astype(BF16)
    wo_h = small_full["w_o"].reshape(-1, nh, V_HEAD, d).astype(BF16)

    ctab, stab = rope_tables(r)

    h = jnp.concatenate([jnp.zeros((META_ROW0, d), F32), meta_full, x[0],
                         jnp.zeros((r - FRONT - seq, d), F32)], axis=0)
    target = jnp.concatenate([jnp.zeros((FRONT, d), F32), loss_target[0],
                              jnp.zeros((r - FRONT - seq, d), F32)], axis=0)
    saved = []
    kv = None
    for l in range(depth):
        sv = {"h0": h}
        ag_wait(2 * l, h)
        h, sv["g1"], sv["u1"] = ffn_fwd(h, vec(ffn1_norm[l]), *ffn_w["ffn1", l], f"ffn1_fwd_{l}")
        sv["h1"] = h
        if l < n_a:
            h = pool_fwd(h, vec(mix_norm[l]), wp[l], vec(pscale[l]), f"pool_fwd_{l}")
        else:
            j = l - n_a
            u = norm_fwd(h, vec(mix_norm[l]), f"mixnorm_{l}")
            cq0 = rowmm(u[None], wdq[j][None], f"dq_{l}")
            cq = norm_fwd(cq0, vec(q_latent_norm[j]), f"qnorm_{l}")
            q = proj_rope(cq, wq_h[j], ctab, stab, None, f"qproj_{l}", scale=Q_SCALE)
            o, lse = attn_fwd(q, kv["k"], kv["v"], f"attn_fwd_{l}")
            h = rowmm(o, wo_h[j], f"oproj_{l}", res=h)
            sv.update(u=u, cq0=cq0, cq=cq, q=q, o=o, lse=lse)
        sv["h2"] = h
        ag_wait(2 * l + 1, h)
        h, sv["g2"], sv["u2"] = ffn_fwd(h, vec(ffn2_norm[l]), *ffn_w["ffn2", l], f"ffn2_fwd_{l}")
        saved.append(sv)
        if l == n_a - 1:
            hkv = norm_fwd(h, vec(kv_in_norm), "kvin_norm")
            ckr = rowmm(hkv[None], wdkv, "dkv")
            ckv = norm_fwd(ckr, vec(kv_latent_norm), "kvlat_norm")
            kv = {"h": h, "hkv": hkv, "ckr": ckr, "ckv": ckv,
                  "k": proj_rope(ckv, wk_h, ctab, stab, ckr, "kproj"),
                  "v": rowmm(ckv[None], wv_h, "vproj", out_dtype=BF16, heads_out=True)}

    sse, dh, dfinal = loss_head(h, vec(final_norm), target, seq, "loss_head")
    loss = lax.psum(0.5 / d * sse[0, 0], ("x", "y", "c"))

    G = {}
    FFN = ("ffn1_w_gate", "ffn1_w_up", "ffn1_w_down", "ffn2_w_gate", "ffn2_w_up", "ffn2_w_down")
    per = {n: [lax.empty(W[n].shape, F32) for _ in range(4)] for n in FFN}
    pending = []

    def rs_complete(after):
        f, l, state = pending.pop()
        names = [f + "_w_gate", f + "_w_up", f + "_w_down"]
        for n, g_ in zip(names, rs_finish(state, after, s_arr, c_arr, f"{f}_{l}")):
            per[n] = adamw_into(W[n], g_, M[n], V[n], per[n], l, f"adamw_{n}_{l}")
        gate.append(per[names[-1]][1][0, 0, 0] * 0.0)

    dnorm = {n: [None] * depth for n in ("ffn1_norm", "mix_norm", "ffn2_norm")}
    dqnorm = [None] * (depth - n_a)
    dpool_w, dpool_scale = [None] * n_a, [None] * n_a
    dwdq, dwq_h, dwo_h = [None] * (depth - n_a), [None] * (depth - n_a), [None] * (depth - n_a)

    def ffn_backward(f, l, h_in, dh_, gg, uu):
        gam = ffn1_norm[l] if f == "ffn1" else ffn2_norm[l]
        wg_, wu_, wd_ = ffn_w[f, l]
        dh_in, dgam, hn, dy, dg, du, a = ffn_bwd_act(h_in, vec(gam), dh_, gg, uu, wg_, wu_, wd_, f"{f}_bwd_act_{l}")
        dwg, dwu, dwd = ffn_bwd_weights(hn, dy, a, dg, du, f"{f}_bwd_w_{l}")
        state = rs_begin([dwg, dwu, dwd], f"{f}_{l}")
        gate.append(state[4][0, 0])
        if pending:
            rs_complete(state[4])
        pending.append((f, l, state))
        dnorm[f + "_norm"][l] = dgam[0]
        return dh_in

    dk_tot = dv_tot = None
    for l in reversed(range(depth)):
        sv = saved[l]
        if l == n_a - 1:
            dckv = rowmm(dk_tot, gated(wk_h), "kproj_bwd", nt=True)
            dkr = rope_bwd_sum(dk_tot, ctab, stab, "kproj_bwd_rope")
            dckv = rowmm(dv_tot, wv_h, "vproj_bwd", nt=True, res=dckv)
            dwk_h = tnmm(kv["ckv"][None], dk_tot, "kproj_bwd_w")
            dwv_h = tnmm(kv["ckv"][None], dv_tot, "vproj_bwd_w")
            dlat, dkvlat = norm_bwd(kv["ckr"], vec(kv_latent_norm), dckv, None, "kvlat_norm_bwd")
            dckr = jnp.concatenate([dlat, dkr], axis=1).astype(BF16)
            dhkv = rowmm(dckr[None], wdkv, "dkv_bwd", nt=True)
            dwdkv = tnmm(kv["hkv"][None], dckr[None], "dkv_bwd_w")[0]
            dh, dkvin = norm_bwd(kv["h"], vec(kv_in_norm), dhkv, dh, "kvin_norm_bwd")
            G["w_dkv"] = dwdkv[:, :KV_RANK + QK_ROPE]
            G["w_uk"] = dwk_h[..., HEAD_W - QK_NOPE:].transpose(1, 0, 2).reshape(KV_RANK, nh * QK_NOPE)
            G["w_uv"] = dwv_h.transpose(1, 0, 2).reshape(KV_RANK, nh * V_HEAD)
        dh = ffn_backward("ffn2", l, sv["h2"], dh, sv["g2"], sv["u2"])
        if l < n_a:
            dh, dmix, dpool_w[l], dps = pool_bwd(sv["h1"], vec(mix_norm[l]), wp[l], vec(pscale[l]), dh, f"pool_bwd_{l}")
            dnorm["mix_norm"][l] = dmix[0]
            dpool_scale[l] = dps[0]
        else:
            j = l - n_a
            do, delta = attn_out_bwd(dh, gated(wo_h[j]), sv["o"], f"oproj_bwd_{l}")
            dwo_h[j] = tnmm(sv["o"], dh[None], f"oproj_bwd_w_{l}")
            dq, dk_tot, dv_tot = attn_bwd(sv["q"], kv["k"], kv["v"], do, sv["lse"], delta, dk_tot, dv_tot,
                                          f"attn_bwd_{l}")
            dxq = rope_bwd_heads(dq, ctab, stab, f"qproj_bwd_rope_{l}", scale=SM_SCALE)
            dcq = rowmm(dxq, wq_h[j], f"qproj_bwd_{l}", nt=True)
            dwq_h[j] = tnmm(sv["cq"][None], dxq, f"qproj_bwd_w_{l}")
            dcq0, dqn = norm_bwd(sv["cq0"], vec(q_latent_norm[j]), dcq, None, f"qnorm_bwd_{l}")
            dqnorm[j] = dqn[0]
            dcq0b = dcq0.astype(BF16)
            du = rowmm(dcq0b[None], wdq[j][None], f"dq_bwd_{l}", nt=True)
            dwdq[j] = tnmm(sv["u"][None], dcq0b[None], f"dq_bwd_w_{l}")[0]
            dh, dmix = norm_bwd(sv["h1"], vec(mix_norm[l]), du, dh, f"mixnorm_bwd_{l}")
            dnorm["mix_norm"][l] = dmix[0]
        dh = ffn_backward("ffn1", l, sv["h0"], dh, sv["g1"], sv["u1"])

    grad_x = dh[FRONT:FRONT + seq][None]
    G["meta_tokens"] = dh[META_ROW0:FRONT]
    G["pool_w"] = jnp.stack(dpool_w)
    G["pool_scale"] = jnp.stack(dpool_scale)
    G["w_dq"] = jnp.stack(dwdq)
    dwq = jnp.stack(dwq_h)
    dwq = jnp.concatenate([dwq[..., HEAD_W - QK_NOPE:], dwq[..., :QK_ROPE]], axis=-1)
    G["w_uq"] = dwq.transpose(0, 2, 1, 3).reshape(-1, Q_RANK, nh * (QK_NOPE + QK_ROPE))
    G["w_o"] = jnp.stack(dwo_h).reshape(-1, nh * V_HEAD, d)

    REPL = ("ffn1_norm", "mix_norm", "ffn2_norm", "kv_in_norm", "kv_latent_norm", "q_latent_norm", "final_norm")
    grep = {"ffn1_norm": jnp.stack(dnorm["ffn1_norm"]), "mix_norm": jnp.stack(dnorm["mix_norm"]),
            "ffn2_norm": jnp.stack(dnorm["ffn2_norm"]), "kv_in_norm": dkvin[0], "kv_latent_norm": dkvlat[0],
            "q_latent_norm": jnp.stack(dqnorm), "final_norm": dfinal[0]}

    def pack128(parts):
        flat = jnp.concatenate([p.reshape(-1) for p in parts])
        n = flat.shape[0]
        n_pad = -(-n // (LANES * SUBLANES)) * (LANES * SUBLANES)
        return jnp.pad(flat, (0, n_pad - n)).reshape(-1, LANES)

    rep_shapes = [W[n].shape for n in REPL]
    g_rep = all_reduce_small(pack128([grep[n] for n in REPL]), "ar_repl")
    d_rep, m_rep, v_rep = adamw(pack128([W[n] for n in REPL]), g_rep, pack128([M[n] for n in REPL]),
                                pack128([V[n] for n in REPL]), "adamw_repl")
    out_g, out_d, out_m, out_v = {}, {}, {}, {}
    for dst, buf in ((out_g, g_rep), (out_d, d_rep), (out_m, m_rep), (out_v, v_rep)):
        for n, a in zip(REPL, _unpack(buf, rep_shapes)):
            dst[n] = a

    g_small = jnp.stack([_pack([_cols_to_shards(G[n], SMALL_AXIS[n])[s] for n in SMALL]) for s in range(N_SHARD)])
    g_small = reduce_scatter([gated(g_small)], c_arr, s_arr, "small")[0]
    d_s, m_s, v_s = adamw(_pack([W[n] for n in SMALL]), g_small, _pack([M[n] for n in SMALL]),
                          _pack([V[n] for n in SMALL]), "adamw_small")
    for dst, buf in ((out_g, g_small), (out_d, d_s), (out_m, m_s), (out_v, v_s)):
        for n, a in zip(SMALL, _unpack(buf, small_shapes)):
            dst[n] = a

    rs_complete(d_s)
    for n in FFN:
        out_g[n], out_d[n], out_m[n], out_v[n] = (
            jnp.swapaxes(a, 1, 2) if n in TRANSPOSED else a for a in per[n])

    return (loss, grad_x, *[out_g[n] for n in NAMES], *[out_d[n] for n in NAMES],
            *[out_m[n] for n in NAMES], *[out_v[n] for n in NAMES])


NAMES = ("meta_tokens", "ffn1_norm", "ffn1_w_gate", "ffn1_w_up", "ffn1_w_down", "mix_norm", "ffn2_norm",
         "ffn2_w_gate", "ffn2_w_up", "ffn2_w_down", "pool_w", "pool_scale", "kv_in_norm", "w_dkv",
         "kv_latent_norm", "w_uk", "w_uv", "w_dq", "q_latent_norm", "w_uq", "w_o", "final_norm")
```
